```python
import math
import jax, jax.numpy as jnp
from jax import lax
import numpy as np

D_MODEL = 1024
BATCH = 16
SEQ = 256
DEPTH = 2
DEC_BATCH = 2
DEC_SEQ = 2048
PAST_LEN = 512

GRID_W = 64
N_MIXERS = 2
N_DIFF_LAYERS = (DEPTH + 1) // 2
N_SWA_LAYERS = DEPTH // 2
DIFF_HEADS = 8
DIFF_HD = 64
SWA_HEADS = 16
SWA_KV_HEADS = 4
SWA_GROUP = SWA_HEADS // SWA_KV_HEADS
SWA_HD = 64
ROT_DIM = 64
WINDOW = 128
BLOCK = 128
D_FF = -(-8 * D_MODEL // (3 * 256)) * 256
D_QKV_DIFF = 3 * DIFF_HEADS * 2 * DIFF_HD
D_QKV_SWA = (SWA_HEADS + 2 * SWA_KV_HEADS) * SWA_HD
ROPE_BASE = 10000.0
EPS = 1e-6
NEG_INF = -1e30

kernel_name = "hybrid_diff_swa_dit_step"


def rmsnorm(x, g):
    xf = x.astype(jnp.float32)
    y = xf * lax.rsqrt(jnp.mean(xf * xf, axis=-1, keepdims=True) + EPS)
    return (y * g.astype(jnp.float32)).astype(x.dtype)


def modulation(cond, w_mod, b_mod):
    m = jax.nn.silu(cond) @ w_mod + b_mod
    return jnp.split(m[:, None, :], 6, axis=-1)


def pre_norm_modulate(x, g, shift, scale):
    return rmsnorm(x, g) * (1 + scale) + shift


def post_norm_residual(x, y, g, gate):
    return x + gate * rmsnorm(y, g)


def swiglu(h, w_gate, w_up, w_down):
    return (jax.nn.silu(h @ w_gate) * (h @ w_up)) @ w_down


def axial_rope_tables(n_lat, rot_dim):
    rows = n_lat // GRID_W
    t = jnp.arange(rows * GRID_W)
    row = (t // GRID_W).astype(jnp.float32)
    col = (t % GRID_W).astype(jnp.float32)
    nf = rot_dim // 4
    inv = ROPE_BASE ** (-jnp.arange(nf, dtype=jnp.float32) / nf)
    ar = row[:, None] * inv[None, :]
    ac = col[:, None] * inv[None, :]
    ang = jnp.concatenate([ar, ar, ac, ac], axis=-1)
    return jnp.cos(ang), jnp.sin(ang)


def apply_axial_rope(x, cos, sin):
    x1, x2, x3, x4 = jnp.split(x, 4, axis=-1)
    rot = jnp.concatenate([-x2, x1, -x4, x3], axis=-1)
    shape = (1, x.shape[1]) + (1,) * (x.ndim - 3) + (x.shape[-1],)
    out = x.astype(jnp.float32) * cos.reshape(shape) + rot.astype(jnp.float32) * sin.reshape(shape)
    return out.astype(x.dtype)


def diff_lambda_value(lam_params, lam_init):
    lp = lam_params.astype(jnp.float32)
    return jnp.exp(jnp.sum(lp[0] * lp[1])) - jnp.exp(jnp.sum(lp[2] * lp[3])) + lam_init


def diff_project(h, w_qkv):
    B, L, _ = h.shape
    q, k, v = jnp.split(h @ w_qkv, 3, axis=-1)
    q = q.reshape(B, L, DIFF_HEADS, 2, DIFF_HD)
    k = k.reshape(B, L, DIFF_HEADS, 2, DIFF_HD)
    v = v.reshape(B, L, DIFF_HEADS, 2 * DIFF_HD)
    return q, k, v


def diff_block_attention(q, k, v, lam):
    B, Lq = q.shape[:2]
    nb = Lq // BLOCK
    scale = DIFF_HD ** -0.5
    qb = q.reshape(B, nb, BLOCK, DIFF_HEADS, 2, DIFF_HD).swapaxes(0, 1)

    def one_block(qi):
        s = jnp.einsum('bqhmd,bkhmd->bhmqk', qi, k).astype(jnp.float32) * scale
        p = jax.nn.softmax(s, axis=-1)
        a = p[:, :, 0] - lam * p[:, :, 1]
        return jnp.einsum('bhqk,bkhe->bqhe', a.astype(v.dtype), v)

    out = lax.map(one_block, qb)
    return out.swapaxes(0, 1).reshape(B, Lq, DIFF_HEADS, 2 * DIFF_HD)


def diff_output(o, subln_g, lam_init, w_o):
    B, L = o.shape[:2]
    o = rmsnorm(o, subln_g) * (1.0 - lam_init)
    return o.reshape(B, L, DIFF_HEADS * 2 * DIFF_HD) @ w_o


def swa_project(h, w_qkv):
    B, L, _ = h.shape
    nq = SWA_HEADS * SWA_HD
    nkv = SWA_KV_HEADS * SWA_HD
    qkv = h @ w_qkv
    q = qkv[..., :nq].reshape(B, L, SWA_KV_HEADS, SWA_GROUP, SWA_HD)
    k = qkv[..., nq:nq + nkv].reshape(B, L, SWA_KV_HEADS, SWA_HD)
    v = qkv[..., nq + nkv:].reshape(B, L, SWA_KV_HEADS, SWA_HD)
    return q, k, v


def sink_softmax(s, sink):
    sk = sink.astype(jnp.float32).reshape(SWA_KV_HEADS, SWA_GROUP)[None, :, :, None, None]
    m = jnp.maximum(jnp.max(s, axis=-1, keepdims=True), sk)
    e = jnp.exp(s - m)
    return e / (jnp.sum(e, axis=-1, keepdims=True) + jnp.exp(sk - m))


def swa_context_attention(q, k, v, sink):
    B, Lq = q.shape[:2]
    nb = Lq // BLOCK
    scale = SWA_HD ** -0.5
    qb = q.reshape(B, nb, BLOCK, SWA_KV_HEADS, SWA_GROUP, SWA_HD).swapaxes(0, 1)

    def one_block(qi):
        s = jnp.einsum('bqkgd,bjkd->bkgqj', qi, k).astype(jnp.float32) * scale
        p = sink_softmax(s, sink)
        return jnp.einsum('bkgqj,bjkd->bqkgd', p.astype(v.dtype), v)

    out = lax.map(one_block, qb)
    return out.swapaxes(0, 1).reshape(B, Lq, SWA_HEADS * SWA_HD)


def swa_latent_attention(q, k, v, k_ctx, v_ctx, sink):
    B, L = q.shape[:2]
    nb = L // BLOCK
    Lc = k_ctx.shape[1]
    span = BLOCK + 2 * WINDOW
    scale = SWA_HD ** -0.5
    pad = ((0, 0), (WINDOW, WINDOW), (0, 0), (0, 0))
    kp = jnp.pad(k, pad)
    vp = jnp.pad(v, pad)
    qb = q.reshape(B, nb, BLOCK, SWA_KV_HEADS, SWA_GROUP, SWA_HD).swapaxes(0, 1)

    def one_block(args):
        qi, i = args
        start = i * BLOCK
        kn = lax.dynamic_slice_in_dim(kp, start, span, axis=1)
        vn = lax.dynamic_slice_in_dim(vp, start, span, axis=1)
        q_pos = start + jnp.arange(BLOCK)
        k_pos = start - WINDOW + jnp.arange(span)
        valid = ((k_pos[None, :] >= 0) & (k_pos[None, :] < L)
                 & (jnp.abs(q_pos[:, None] - k_pos[None, :]) <= WINDOW))
        s_lat = jnp.einsum('bqkgd,bjkd->bkgqj', qi, kn).astype(jnp.float32) * scale
        s_lat = jnp.where(valid, s_lat, NEG_INF)
        s_ctx = jnp.einsum('bqkgd,bjkd->bkgqj', qi, k_ctx).astype(jnp.float32) * scale
        p = sink_softmax(jnp.concatenate([s_ctx, s_lat], axis=-1), sink).astype(v.dtype)
        return (jnp.einsum('bkgqj,bjkd->bqkgd', p[..., :Lc], v_ctx)
                + jnp.einsum('bkgqj,bjkd->bqkgd', p[..., Lc:], vn))

    out = lax.map(one_block, (qb, jnp.arange(nb)))
    return out.swapaxes(0, 1).reshape(B, L, SWA_HEADS * SWA_HD)


def setup_inputs(seed: int = 0) -> dict:
    key = jax.random.key(seed)
    ks = jax.random.split(key, 24)
    f32 = jnp.float32
    n = lambda k, s: jax.random.normal(k, s, f32)
    D = D_MODEL
    return {
        "x_prompt": n(ks[0], (BATCH, SEQ, D)),
        "x_sample": n(ks[1], (DEC_BATCH, DEC_SEQ, D)),
        "cache_diff_k": n(ks[2], (DEC_BATCH, N_DIFF_LAYERS, PAST_LEN, DIFF_HEADS, 2 * DIFF_HD)),
        "cache_diff_v": n(ks[3], (DEC_BATCH, N_DIFF_LAYERS, PAST_LEN, DIFF_HEADS, 2 * DIFF_HD)),
        "cache_swa_k": n(ks[4], (DEC_BATCH, N_SWA_LAYERS, PAST_LEN, SWA_KV_HEADS, SWA_HD)),
        "cache_swa_v": n(ks[5], (DEC_BATCH, N_SWA_LAYERS, PAST_LEN, SWA_KV_HEADS, SWA_HD)),
        "c": n(ks[6], (DEC_BATCH, D)),
        "c_ctx": n(ks[7], (D,)),
        "w_mod": n(ks[8], (DEPTH, D, 6 * D)) * (0.5 * D ** -0.5),
        "b_mod": n(ks[9], (DEPTH, 6 * D)) * 0.01,
        "norm_g": 1.0 + 0.05 * n(ks[10], (DEPTH, 4, D)),
        "w_qkv_diff": n(ks[11], (N_DIFF_LAYERS, D, D_QKV_DIFF)) * D ** -0.5,
        "diff_lambda": n(ks[12], (N_DIFF_LAYERS, 4, DIFF_HD)) * 0.1,
        "diff_subln_g": 1.0 + 0.05 * n(ks[13], (N_DIFF_LAYERS, 2 * DIFF_HD)),
        "w_o_diff": n(ks[14], (N_DIFF_LAYERS, DIFF_HEADS * 2 * DIFF_HD, D)) * (DIFF_HEADS * 2 * DIFF_HD) ** -0.5,
        "w_qkv_swa": n(ks[15], (N_SWA_LAYERS, D, D_QKV_SWA)) * D ** -0.5,
        "swa_sink": n(ks[16], (N_SWA_LAYERS, SWA_HEADS)) * 0.5,
        "w_o_swa": n(ks[17], (N_SWA_LAYERS, SWA_HEADS * SWA_HD, D)) * (SWA_HEADS * SWA_HD) ** -0.5,
        "w_gate": n(ks[18], (DEPTH, D, D_FF)) * D ** -0.5,
        "w_up": n(ks[19], (DEPTH, D, D_FF)) * D ** -0.5,
        "w_down": n(ks[20], (DEPTH, D_FF, D)) * D_FF ** -0.5,
    }


def reference(x_prompt, x_sample, cache_diff_k, cache_diff_v, cache_swa_k, cache_swa_v, c, c_ctx,
              w_mod, b_mod, norm_g, w_qkv_diff, diff_lambda, diff_subln_g, w_o_diff,
              w_qkv_swa, swa_sink, w_o_swa, w_gate, w_up, w_down):
    Bp, Lp = x_prompt.shape[:2]
    Bs, Ls = x_sample.shape[:2]
    Lc = cache_diff_k.shape[2]
    cos, sin = axial_rope_tables(Ls, ROT_DIM)
    xp, xs = x_prompt, x_sample
    diff_k_out, diff_v_out, swa_k_out, swa_v_out = [], [], [], []

    for i in range(DEPTH):
        mp = modulation(c_ctx[None, :], w_mod[i], b_mod[i])
        ms = modulation(c, w_mod[i], b_mod[i])
        hp = pre_norm_modulate(xp, norm_g[i, 0], mp[0], mp[1])
        hs = pre_norm_modulate(xs, norm_g[i, 0], ms[0], ms[1])
        j = i // N_MIXERS
        if i % N_MIXERS == 0:
            lam_init = 0.8 - 0.6 * math.exp(-0.3 * i)
            lam = diff_lambda_value(diff_lambda[j], lam_init)
            qp, kp, vp = diff_project(hp, w_qkv_diff[j])
            yp = diff_output(diff_block_attention(qp, kp, vp, lam), diff_subln_g[j], lam_init, w_o_diff[j])
            diff_k_out.append(kp.reshape(Bp, Lp, DIFF_HEADS, 2 * DIFF_HD))
            diff_v_out.append(vp)
            qs, ks_, vs = diff_project(hs, w_qkv_diff[j])
            qs = apply_axial_rope(qs, cos, sin)
            ks_ = apply_axial_rope(ks_, cos, sin)
            kc = cache_diff_k[:, j].reshape(Bs, Lc, DIFF_HEADS, 2, DIFF_HD)
            k_all = jnp.concatenate([kc, ks_], axis=1)
            v_all = jnp.concatenate([cache_diff_v[:, j], vs], axis=1)
            ys = diff_output(diff_block_attention(qs, k_all, v_all, lam), diff_subln_g[j], lam_init, w_o_diff[j])
        else:
            qp, kp, vp = swa_project(hp, w_qkv_swa[j])
            yp = swa_context_attention(qp, kp, vp, swa_sink[j]) @ w_o_swa[j]
            swa_k_out.append(kp)
            swa_v_out.append(vp)
            qs, ks_, vs = swa_project(hs, w_qkv_swa[j])
            qs = apply_axial_rope(qs, cos, sin)
            ks_ = apply_axial_rope(ks_, cos, sin)
            ys = swa_latent_attention(qs, ks_, vs, cache_swa_k[:, j], cache_swa_v[:, j], swa_sink[j]) @ w_o_swa[j]
        xp = post_norm_residual(xp, yp, norm_g[i, 1], mp[2])
        xs = post_norm_residual(xs, ys, norm_g[i, 1], ms[2])
        hp = pre_norm_modulate(xp, norm_g[i, 2], mp[3], mp[4])
        hs = pre_norm_modulate(xs, norm_g[i, 2], ms[3], ms[4])
        xp = post_norm_residual(xp, swiglu(hp, w_gate[i], w_up[i], w_down[i]), norm_g[i, 3], mp[5])
        xs = post_norm_residual(xs, swiglu(hs, w_gate[i], w_up[i], w_down[i]), norm_g[i, 3], ms[5])

    new_diff_k = jnp.stack(diff_k_out, axis=1)
    new_diff_v = jnp.stack(diff_v_out, axis=1)
    new_swa_k = jnp.stack(swa_k_out, axis=1)
    new_swa_v = jnp.stack(swa_v_out, axis=1)
    return (xp, xs, new_diff_k, new_diff_v, new_swa_k, new_swa_v)
```

```python
import functools
import math

import jax
import jax.numpy as jnp
from jax import lax
from jax.experimental import pallas as pl
from jax.experimental.pallas import tpu as pltpu

F32 = jnp.float32
BF16 = jnp.bfloat16

GRID_W = 64
N_MIXERS = 2
DIFF_HEADS = 8
DIFF_HD = 64
SWA_HEADS = 16
SWA_KV_HEADS = 4
SWA_GROUP = SWA_HEADS // SWA_KV_HEADS
SWA_HD = 64
ROT_DIM = 64
WINDOW = 128
ROPE_BASE = 10000.0
EPS = 1e-6
NEG_INF = -1e30

LANES = 128
ROW_TILE = 512
VMEM_LIMIT = 48 * 1024 * 1024
NT_DIMS = (((1,), (1,)), ((), ()))


def _params(n_axes):
    return pltpu.CompilerParams(dimension_semantics=("arbitrary",) * n_axes,
                                vmem_limit_bytes=VMEM_LIMIT)


def _resident(shape):
    return pl.BlockSpec(shape, lambda *_: (0,) * len(shape), pipeline_mode=pl.Buffered(1))


def _rms(x, g):
    ms = jnp.mean(x * x, axis=-1, keepdims=True)
    return (x * lax.rsqrt(ms + EPS)) * g


def _half_masks(dtype):
    lane = lax.broadcasted_iota(jnp.int32, (1, LANES), 1)
    lo = lane < (LANES // 2)
    return jnp.where(lo, 1.0, 0.0).astype(dtype), jnp.where(lo, 0.0, 1.0).astype(dtype)


def _mod_kernel(cond_ref, w_ref, b_ref, out_ref):
    c = cond_ref[...]
    s = c * jax.nn.sigmoid(c)
    out_ref[...] = jnp.dot(s.astype(BF16), w_ref[...].astype(BF16),
                           preferred_element_type=F32) + b_ref[...]


def _modulation(cond8, w_mod, b_mod):
    depth, d, n = w_mod.shape
    tn = 1536
    return pl.pallas_call(
        _mod_kernel,
        out_shape=jax.ShapeDtypeStruct((depth, 8, n), F32),
        grid=(depth, n // tn),
        in_specs=[pl.BlockSpec((8, d), lambda i, j: (0, 0)),
                  pl.BlockSpec((None, d, tn), lambda i, j: (i, 0, j)),
                  pl.BlockSpec((None, 1, tn), lambda i, j: (i, 0, j))],
        out_specs=pl.BlockSpec((None, 8, tn), lambda i, j: (i, 0, j)),
        compiler_params=_params(2),
        name="modulation",
    )(cond8, w_mod, b_mod.reshape(depth, 1, n))


def _mod_row(mod_ref, row0, tiles_per_batch):
    return row0 + pl.program_id(0) // tiles_per_batch


def _rope_slab(xs, cos, sin_signed, lo16):
    left = pltpu.roll(xs, LANES - 16, 1)
    right = pltpu.roll(xs, 16, 1)
    return xs * cos + jnp.where(lo16, left, right) * sin_signed


def _lo16_mask():
    lane = lax.broadcasted_iota(jnp.int32, (1, LANES), 1)
    return (lane % 32) < 16


def _pre_norm(x_ref, g_ref, mod_ref, r, d, slot):
    shift = mod_ref[pl.ds(r, 1), slot * d:(slot + 1) * d]
    scale = mod_ref[pl.ds(r, 1), (slot + 1) * d:(slot + 2) * d]
    return _rms(x_ref[...], g_ref[...]) * (1 + scale) + shift


def _qkv_diff_kernel(x_ref, g_ref, mod_ref, w_ref, *rest, latent, row0, tiles_per_batch):
    d = x_ref.shape[1]
    if latent:
        cos_ref, sin_ref, q_ref, k_ref, v_ref = rest
        cos, sin = cos_ref[...], sin_ref[...]
        lo16 = _lo16_mask()
    else:
        q_ref, k_ref, v_ref = rest
    r = _mod_row(mod_ref, row0, tiles_per_batch)
    h = _pre_norm(x_ref, g_ref, mod_ref, r, d, 0).astype(BF16)
    cw = 512
    for c in range(3 * d // cw):
        acc = jnp.dot(h, w_ref[:, c * cw:(c + 1) * cw], preferred_element_type=F32)
        which, off = divmod(c * cw, d)
        dst = (q_ref, k_ref, v_ref)[which]
        for s in range(cw // LANES):
            xs = acc[:, s * LANES:(s + 1) * LANES]
            if latent and which < 2:
                xs = _rope_slab(xs, cos, sin, lo16)
            if which == 0:
                xs = xs * (DIFF_HD ** -0.5)
            lo = off + s * LANES
            dst[:, lo:lo + LANES] = xs.astype(dst.dtype)


def _qkv_diff(x, g, mod, w, tables, *, latent, row0, tiles_per_batch):
    n, d = x.shape
    tm = ROW_TILE
    in_specs = [pl.BlockSpec((tm, d), lambda i: (i, 0)),
                _resident((1, d)), _resident(mod.shape), _resident(w.shape)]
    args = [x, g, mod, w]
    kv_dtype = F32
    if latent:
        nt = tables[0].shape[0] // tm
        in_specs += [pl.BlockSpec((tm, LANES), lambda i: (i % nt, 0))] * 2
        args += list(tables)
        kv_dtype = BF16
    out_spec = pl.BlockSpec((tm, d), lambda i: (i, 0))
    return pl.pallas_call(
        functools.partial(_qkv_diff_kernel, latent=latent, row0=row0, tiles_per_batch=tiles_per_batch),
        out_shape=(jax.ShapeDtypeStruct((n, d), BF16),
                   jax.ShapeDtypeStruct((n, d), kv_dtype),
                   jax.ShapeDtypeStruct((n, d), kv_dtype)),
        grid=(n // tm,),
        in_specs=in_specs,
        out_specs=(out_spec, out_spec, out_spec),
        compiler_params=_params(1),
        name="qkv_diff_latent" if latent else "qkv_diff_prompt",
    )(*args)


def _qkv_swa_kernel(x_ref, g_ref, mod_ref, w_ref, *rest, latent, row0, tiles_per_batch):
    d = x_ref.shape[1]
    nkv = SWA_KV_HEADS * SWA_HD
    if latent:
        cos_ref, sin_ref, q_ref, kd_ref, vd_ref = rest
        cos, sin = cos_ref[...], sin_ref[...]
        lo16 = _lo16_mask()
    else:
        q_ref, kd_ref, vd_ref, k_ref, v_ref = rest
    lane = lax.broadcasted_iota(jnp.int32, (1, LANES), 1)
    lo64 = lane < (LANES // 2)
    r = _mod_row(mod_ref, row0, tiles_per_batch)
    h = _pre_norm(x_ref, g_ref, mod_ref, r, d, 0).astype(BF16)
    cw = 512
    for c in range(d // cw):
        acc = jnp.dot(h, w_ref[:, c * cw:(c + 1) * cw], preferred_element_type=F32)
        for s in range(cw // LANES):
            xs = acc[:, s * LANES:(s + 1) * LANES]
            if latent:
                xs = _rope_slab(xs, cos, sin, lo16)
            lo = c * cw + s * LANES
            q_ref[:, lo:lo + LANES] = (xs * (SWA_HD ** -0.5)).astype(BF16)
    kv = jnp.dot(h, w_ref[:, d:d + 2 * nkv], preferred_element_type=F32)
    if not latent:
        k_ref[...] = kv[:, :nkv]
        v_ref[...] = kv[:, nkv:]
    for which, dst in enumerate((kd_ref, vd_ref)):
        for s in range(nkv // LANES):
            xs = kv[:, which * nkv + s * LANES: which * nkv + (s + 1) * LANES]
            if latent and which == 0:
                xs = _rope_slab(xs, cos, sin, lo16)
            sw = pltpu.roll(xs, LANES // 2, 1)
            dst[:, (2 * s) * LANES:(2 * s + 1) * LANES] = jnp.where(lo64, xs, sw).astype(BF16)
            dst[:, (2 * s + 1) * LANES:(2 * s + 2) * LANES] = jnp.where(lo64, sw, xs).astype(BF16)


def _qkv_swa(x, g, mod, w, tables, *, latent, row0, tiles_per_batch):
    n, d = x.shape
    tm = ROW_TILE
    nkv = SWA_KV_HEADS * SWA_HD
    in_specs = [pl.BlockSpec((tm, d), lambda i: (i, 0)),
                _resident((1, d)), _resident(mod.shape), _resident(w.shape)]
    args = [x, g, mod, w]
    out_shape = [jax.ShapeDtypeStruct((n, d), BF16),
                 jax.ShapeDtypeStruct((n, 2 * nkv), BF16),
                 jax.ShapeDtypeStruct((n, 2 * nkv), BF16)]
    out_specs = [pl.BlockSpec((tm, d), lambda i: (i, 0)),
                 pl.BlockSpec((tm, 2 * nkv), lambda i: (i, 0)),
                 pl.BlockSpec((tm, 2 * nkv), lambda i: (i, 0))]
    if latent:
        nt = tables[0].shape[0] // tm
        in_specs += [pl.BlockSpec((tm, LANES), lambda i: (i % nt, 0))] * 2
        args += list(tables)
    else:
        out_shape += [jax.ShapeDtypeStruct((n, nkv), F32)] * 2
        out_specs += [pl.BlockSpec((tm, nkv), lambda i: (i, 0))] * 2
    return pl.pallas_call(
        functools.partial(_qkv_swa_kernel, latent=latent, row0=row0, tiles_per_batch=tiles_per_batch),
        out_shape=tuple(out_shape),
        grid=(n // tm,),
        in_specs=in_specs,
        out_specs=tuple(out_specs),
        compiler_params=_params(1),
        name="qkv_swa_latent" if latent else "qkv_swa_prompt",
    )(*args)


def _diff_lambda(lam_ref, lam_init):
    lp = lam_ref[...]
    a = jnp.sum(lp[0:1] * lp[1:2], axis=-1, keepdims=True)
    b = jnp.sum(lp[2:3] * lp[3:4], axis=-1, keepdims=True)
    return jnp.exp(a) - jnp.exp(b) + lam_init


def _softmax_rows(s):
    m = jnp.max(s, axis=-1, keepdims=True)
    e = jnp.exp(s - m)
    return e / jnp.sum(e, axis=-1, keepdims=True)


def _diff_head(q, k, v, lam, g, lam_init):
    m_lo, m_hi = _half_masks(BF16)
    s1 = lax.dot_general(q * m_lo, k, NT_DIMS, preferred_element_type=F32)
    s2 = lax.dot_general(q * m_hi, k, NT_DIMS, preferred_element_type=F32)
    a = _softmax_rows(s1) - lam * _softmax_rows(s2)
    o = jnp.dot(a.astype(BF16), v, preferred_element_type=F32)
    return _rms(o, g) * (1.0 - lam_init)


def _diff_prompt_kernel(q_ref, k_ref, v_ref, lam_ref, g_ref, o_ref, *, lam_init):
    lam = _diff_lambda(lam_ref, lam_init)
    g = g_ref[...]
    for h in range(DIFF_HEADS):
        sl = slice(h * LANES, (h + 1) * LANES)
        o = _diff_head(q_ref[:, sl], k_ref[:, sl].astype(BF16), v_ref[:, sl].astype(BF16),
                       lam, g, lam_init)
        o_ref[:, sl] = o.astype(BF16)


def _diff_prompt_attention(q, k, v, lam_params, subln_g, *, seq, lam_init):
    n, d = q.shape
    spec = pl.BlockSpec((seq, d), lambda b: (b, 0))
    return pl.pallas_call(
        functools.partial(_diff_prompt_kernel, lam_init=lam_init),
        out_shape=jax.ShapeDtypeStruct((n, d), BF16),
        grid=(n // seq,),
        in_specs=[spec, spec, spec, _resident(lam_params.shape), _resident(subln_g.shape)],
        out_specs=spec,
        compiler_params=_params(1),
        name="diff_attn_prompt",
    )(q, k, v, lam_params, subln_g)


def _diff_latent_kernel(q_ref, kc_ref, vc_ref, kl_ref, vl_ref, lam_ref, g_ref, o_ref,
                        kk_ref, vv_ref, *, lam_init):
    lc = kc_ref.shape[0]

    @pl.when(pl.program_id(2) == 0)
    def _():
        kk_ref[0:lc, :] = kc_ref[...].astype(BF16)
        kk_ref[lc:, :] = kl_ref[...]
        vv_ref[0:lc, :] = vc_ref[...].astype(BF16)
        vv_ref[lc:, :] = vl_ref[...]

    lam = _diff_lambda(lam_ref, lam_init)
    o = _diff_head(q_ref[...], kk_ref[...], vv_ref[...], lam, g_ref[...], lam_init)
    o_ref[...] = o.astype(BF16)


def _diff_latent_attention(q, k, v, cache_k, cache_v, lam_params, subln_g, *, seq, lam_init):
    n, d = q.shape
    nb, lc, _ = cache_k.shape
    tq = 512
    nq = seq // tq
    q_spec = pl.BlockSpec((tq, LANES), lambda b, h, i: (b * nq + i, h))
    c_spec = pl.BlockSpec((None, lc, LANES), lambda b, h, i: (b, 0, h))
    l_spec = pl.BlockSpec((seq, LANES), lambda b, h, i: (b, h))
    return pl.pallas_call(
        functools.partial(_diff_latent_kernel, lam_init=lam_init),
        out_shape=jax.ShapeDtypeStruct((n, d), BF16),
        grid=(nb, DIFF_HEADS, nq),
        in_specs=[q_spec, c_spec, c_spec, l_spec, l_spec,
                  _resident(lam_params.shape), _resident(subln_g.shape)],
        out_specs=q_spec,
        scratch_shapes=[pltpu.VMEM((lc + seq, LANES), BF16), pltpu.VMEM((lc + seq, LANES), BF16)],
        compiler_params=_params(3),
        name="diff_attn_latent",
    )(q, cache_k, cache_v, k, v, lam_params, subln_g)


def _sink_probs(scores, sk):
    m = sk
    for s in scores:
        m = jnp.maximum(m, jnp.max(s, axis=-1, keepdims=True))
    es = [jnp.exp(s - m) for s in scores]
    denom = jnp.exp(sk - m)
    for e in es:
        denom = denom + jnp.sum(e, axis=-1, keepdims=True)
    return [e / denom for e in es]


def _swa_prompt_kernel(sink_ref, q_ref, kd_ref, vd_ref, o_ref):
    m_lo, m_hi = _half_masks(BF16)
    lane = lax.broadcasted_iota(jnp.int32, (1, LANES), 1)
    lo64 = lane < (LANES // 2)
    for blk in range(SWA_HEADS // 2):
        j = (2 * blk) // SWA_GROUP
        kd = kd_ref[:, j * LANES:(j + 1) * LANES]
        vd = vd_ref[:, j * LANES:(j + 1) * LANES]
        qb = q_ref[:, blk * LANES:(blk + 1) * LANES]
        halves = []
        for half, msk in enumerate((m_lo, m_hi)):
            sk = jnp.full((1, 1), sink_ref[2 * blk + half], F32)
            s = lax.dot_general(qb * msk, kd, NT_DIMS, preferred_element_type=F32)
            (p,) = _sink_probs([s], sk)
            halves.append(jnp.dot(p.astype(BF16), vd, preferred_element_type=F32))
        o_ref[:, blk * LANES:(blk + 1) * LANES] = jnp.where(lo64, halves[0], halves[1]).astype(BF16)


def _swa_prompt_attention(q, kd, vd, sink, *, seq):
    n, d = q.shape
    wkv = kd.shape[1]
    return pl.pallas_call(
        _swa_prompt_kernel,
        out_shape=jax.ShapeDtypeStruct((n, d), BF16),
        grid=(n // seq,),
        in_specs=[pl.BlockSpec(memory_space=pltpu.SMEM),
                  pl.BlockSpec((seq, d), lambda b: (b, 0)),
                  pl.BlockSpec((seq, wkv), lambda b: (b, 0)),
                  pl.BlockSpec((seq, wkv), lambda b: (b, 0))],
        out_specs=pl.BlockSpec((seq, d), lambda b: (b, 0)),
        compiler_params=_params(1),
        name="swa_attn_prompt",
    )(sink, q, kd, vd)


def _swa_latent_kernel(sink_ref, q_ref, kc_ref, vc_ref, kl_ref, vl_ref, o_ref, kcd_ref, vcd_ref,
                       *, tq, span):
    pair = pl.program_id(1)
    qi = pl.program_id(2)
    seq = kl_ref.shape[0]
    lane = lax.broadcasted_iota(jnp.int32, (1, LANES), 1)
    lo64 = lane < (LANES // 2)
    m_lo, m_hi = _half_masks(BF16)

    @pl.when(qi == 0)
    def _():
        for src, dst in ((kc_ref, kcd_ref), (vc_ref, vcd_ref)):
            xs = src[...]
            sw = pltpu.roll(xs, LANES // 2, 1)
            dst[0] = jnp.where(lo64, xs, sw).astype(BF16)
            dst[1] = jnp.where(lo64, sw, xs).astype(BF16)

    ws = pl.multiple_of(jnp.clip(qi * tq - WINDOW, 0, seq - span), WINDOW)
    q_pos = qi * tq + lax.broadcasted_iota(jnp.int32, (tq, span), 0)
    k_pos = ws + lax.broadcasted_iota(jnp.int32, (tq, span), 1)
    valid = jnp.abs(q_pos - k_pos) <= WINDOW
    heads_per_step = 2 * SWA_GROUP
    for jj in range(2):
        kc, vc = kcd_ref[jj], vcd_ref[jj]
        kw = kl_ref[pl.ds(ws, span), jj * LANES:(jj + 1) * LANES]
        vw = vl_ref[pl.ds(ws, span), jj * LANES:(jj + 1) * LANES]
        for gb in range(SWA_GROUP // 2):
            blk = jj * (SWA_GROUP // 2) + gb
            qb = q_ref[:, blk * LANES:(blk + 1) * LANES]
            halves = []
            for half, msk in enumerate((m_lo, m_hi)):
                head = pair * heads_per_step + 2 * blk + half
                sk = jnp.full((1, 1), sink_ref[head], F32)
                qm = qb * msk
                s_c = lax.dot_general(qm, kc, NT_DIMS, preferred_element_type=F32)
                s_l = lax.dot_general(qm, kw, NT_DIMS, preferred_element_type=F32)
                s_l = jnp.where(valid, s_l, NEG_INF)
                p_c, p_l = _sink_probs([s_c, s_l], sk)
                halves.append(jnp.dot(p_c.astype(BF16), vc, preferred_element_type=F32)
                              + jnp.dot(p_l.astype(BF16), vw, preferred_element_type=F32))
            o_ref[:, blk * LANES:(blk + 1) * LANES] = jnp.where(lo64, halves[0], halves[1]).astype(BF16)


def _swa_latent_attention(q, kd, vd, cache_k, cache_v, sink, *, seq):
    n, d = q.shape
    nb, lc, wc = cache_k.shape
    tq = 256
    span = tq + 2 * WINDOW
    nq = seq // tq
    npair = SWA_KV_HEADS // 2
    wq = d // npair
    q_spec = pl.BlockSpec((tq, wq), lambda b, p, i: (b * nq + i, p))
    c_spec = pl.BlockSpec((None, lc, LANES), lambda b, p, i: (b, 0, p))
    l_spec = pl.BlockSpec((seq, 2 * LANES), lambda b, p, i: (b, p))
    return pl.pallas_call(
        functools.partial(_swa_latent_kernel, tq=tq, span=span),
        out_shape=jax.ShapeDtypeStruct((n, d), BF16),
        grid=(nb, npair, nq),
        in_specs=[pl.BlockSpec(memory_space=pltpu.SMEM), q_spec, c_spec, c_spec, l_spec, l_spec],
        out_specs=q_spec,
        scratch_shapes=[pltpu.VMEM((2, lc, LANES), BF16), pltpu.VMEM((2, lc, LANES), BF16)],
        compiler_params=_params(3),
        name="swa_attn_latent",
    )(sink, q, cache_k, cache_v, kd, vd)


def _out_proj_kernel(o_ref, w_ref, x_ref, g1_ref, g2_ref, mod_ref, xo_ref, h_ref, *, row0, tiles_per_batch):
    d = x_ref.shape[1]
    r = _mod_row(mod_ref, row0, tiles_per_batch)
    gate = mod_ref[pl.ds(r, 1), 2 * d:3 * d]
    y = jnp.dot(o_ref[...], w_ref[...], preferred_element_type=F32)
    x = x_ref[...] + gate * _rms(y, g1_ref[...])
    xo_ref[...] = x
    shift = mod_ref[pl.ds(r, 1), 3 * d:4 * d]
    scale = mod_ref[pl.ds(r, 1), 4 * d:5 * d]
    h_ref[...] = (_rms(x, g2_ref[...]) * (1 + scale) + shift).astype(BF16)


def _out_proj(o, w, x, g1, g2, mod, *, row0, tiles_per_batch):
    n, d = x.shape
    tm = ROW_TILE
    row = pl.BlockSpec((tm, d), lambda i: (i, 0))
    return pl.pallas_call(
        functools.partial(_out_proj_kernel, row0=row0, tiles_per_batch=tiles_per_batch),
        out_shape=(jax.ShapeDtypeStruct((n, d), F32), jax.ShapeDtypeStruct((n, d), BF16)),
        grid=(n // tm,),
        in_specs=[row, _resident(w.shape), row, _resident((1, d)), _resident((1, d)), _resident(mod.shape)],
        out_specs=(row, row),
        compiler_params=_params(1),
        name="out_proj",
    )(o, w, x, g1, g2, mod)


def _ffn_kernel(h_ref, wg_ref, wu_ref, wd_ref, x_ref, g_ref, mod_ref, xo_ref, *, row0, tiles_per_batch):
    d = x_ref.shape[1]
    dff = wg_ref.shape[1]
    r = _mod_row(mod_ref, row0, tiles_per_batch)
    gate = mod_ref[pl.ds(r, 1), 5 * d:6 * d]
    h = h_ref[...]
    cw = 256
    y = jnp.zeros((h.shape[0], d), F32)
    for c in range(dff // cw):
        a = jnp.dot(h, wg_ref[:, c * cw:(c + 1) * cw], preferred_element_type=F32)
        u = jnp.dot(h, wu_ref[:, c * cw:(c + 1) * cw], preferred_element_type=F32)
        t = (a * jax.nn.sigmoid(a)) * u
        y = y + jnp.dot(t.astype(BF16), wd_ref[c * cw:(c + 1) * cw, :], preferred_element_type=F32)
    xo_ref[...] = x_ref[...] + gate * _rms(y, g_ref[...])


def _ffn(h, wg, wu, wd, x, g, mod, *, row0, tiles_per_batch):
    n, d = x.shape
    tm = ROW_TILE
    row = pl.BlockSpec((tm, d), lambda i: (i, 0))
    return pl.pallas_call(
        functools.partial(_ffn_kernel, row0=row0, tiles_per_batch=tiles_per_batch),
        out_shape=jax.ShapeDtypeStruct((n, d), F32),
        grid=(n // tm,),
        in_specs=[row, _resident(wg.shape), _resident(wu.shape), _resident(wd.shape), row,
                  _resident((1, d)), _resident(mod.shape)],
        out_specs=row,
        compiler_params=_params(1),
        name="ffn",
    )(h, wg, wu, wd, x, g, mod)


def _rope_tables(n_lat):
    t = jnp.arange(n_lat)
    row = (t // GRID_W).astype(F32)
    col = (t % GRID_W).astype(F32)
    nf = ROT_DIM // 4
    inv = ROPE_BASE ** (-jnp.arange(nf, dtype=F32) / nf)
    ar = row[:, None] * inv[None, :]
    ac = col[:, None] * inv[None, :]
    ang = jnp.concatenate([ar, ar, ac, ac], axis=-1)
    cos, sin = jnp.cos(ang), jnp.sin(ang)
    sign = jnp.where((jnp.arange(ROT_DIM) % 32) < 16, -1.0, 1.0).astype(F32)
    reps = LANES // ROT_DIM
    return jnp.tile(cos, (1, reps)), jnp.tile(sin * sign, (1, reps))


def kernel(x_prompt, x_sample, cache_diff_k, cache_diff_v, cache_swa_k, cache_swa_v, c, c_ctx,
           w_mod, b_mod, norm_g, w_qkv_diff, diff_lambda, diff_subln_g, w_o_diff,
           w_qkv_swa, swa_sink, w_o_swa, w_gate, w_up, w_down):
    bp, lp, d = x_prompt.shape
    bs, ls, _ = x_sample.shape
    lc = cache_diff_k.shape[2]
    depth = w_mod.shape[0]
    tm = ROW_TILE

    cond8 = jnp.concatenate([c_ctx[None, :], c, jnp.zeros((8 - 1 - bs, d), F32)], axis=0)
    mods = _modulation(cond8, w_mod, b_mod)
    tables = _rope_tables(ls)

    xp = x_prompt.reshape(bp * lp, d)
    xs = x_sample.reshape(bs * ls, d)
    p_rows = dict(row0=0, tiles_per_batch=bp * lp // tm)
    s_rows = dict(row0=1, tiles_per_batch=ls // tm)
    diff_k_out, diff_v_out, swa_k_out, swa_v_out = [], [], [], []

    for i in range(depth):
        mod = mods[i]
        g = norm_g[i].reshape(4, 1, d)
        j = i // N_MIXERS
        if i % N_MIXERS == 0:
            lam_init = 0.8 - 0.6 * math.exp(-0.3 * i)
            w = w_qkv_diff[j].astype(BF16)
            sub_g = diff_subln_g[j].reshape(1, -1)
            qp, kp, vp = _qkv_diff(xp, g[0], mod, w, None, latent=False, **p_rows)
            op = _diff_prompt_attention(qp, kp, vp, diff_lambda[j], sub_g, seq=lp, lam_init=lam_init)
            diff_k_out.append(kp.reshape(bp, lp, DIFF_HEADS, 2 * DIFF_HD))
            diff_v_out.append(vp.reshape(bp, lp, DIFF_HEADS, 2 * DIFF_HD))
            qs, ks, vs = _qkv_diff(xs, g[0], mod, w, tables, latent=True, **s_rows)
            os_ = _diff_latent_attention(qs, ks, vs,
                                         cache_diff_k[:, j].reshape(bs, lc, d),
                                         cache_diff_v[:, j].reshape(bs, lc, d),
                                         diff_lambda[j], sub_g, seq=ls, lam_init=lam_init)
            w_o = w_o_diff[j].astype(BF16)
        else:
            w = w_qkv_swa[j].astype(BF16)
            nkv = SWA_KV_HEADS * SWA_HD
            qp, kdp, vdp, kp, vp = _qkv_swa(xp, g[0], mod, w, None, latent=False, **p_rows)
            op = _swa_prompt_attention(qp, kdp, vdp, swa_sink[j], seq=lp)
            swa_k_out.append(kp.reshape(bp, lp, SWA_KV_HEADS, SWA_HD))
            swa_v_out.append(vp.reshape(bp, lp, SWA_KV_HEADS, SWA_HD))
            qs, kds, vds = _qkv_swa(xs, g[0], mod, w, tables, latent=True, **s_rows)
            os_ = _swa_latent_attention(qs, kds, vds,
                                        cache_swa_k[:, j].reshape(bs, lc, nkv),
                                        cache_swa_v[:, j].reshape(bs, lc, nkv),
                                        swa_sink[j], seq=ls)
            w_o = w_o_swa[j].astype(BF16)
        wg, wu, wd = w_gate[i].astype(BF16), w_up[i].astype(BF16), w_down[i].astype(BF16)
        xp, hp = _out_proj(op, w_o, xp, g[1], g[2], mod, **p_rows)
        xs, hs = _out_proj(os_, w_o, xs, g[1], g[2], mod, **s_rows)
        xp = _ffn(hp, wg, wu, wd, xp, g[3], mod, **p_rows)
        xs = _ffn(hs, wg, wu, wd, xs, g[3], mod, **s_rows)

    return (xp.reshape(bp, lp, d), xs.reshape(bs, ls, d),
            jnp.stack(diff_k_out, axis=1), jnp.stack(diff_v_out, axis=1),
            jnp.stack(swa_k_out, axis=1), jnp.stack(swa_v_out, axis=1))
```

```python
import functools
import math

import jax
import jax.numpy as jnp
from jax import lax
from jax.experimental import pallas as pl
from jax.experimental.pallas import tpu as pltpu

F32 = jnp.float32
BF16 = jnp.bfloat16

GRID_W = 64
N_MIXERS = 2
DIFF_HEADS = 8
DIFF_HD = 64
SWA_HEADS = 16
SWA_KV_HEADS = 4
SWA_GROUP = SWA_HEADS // SWA_KV_HEADS
SWA_HD = 64
ROT_DIM = 64
WINDOW = 128
ROPE_BASE = 10000.0
EPS = 1e-6
NEG_INF = -1e30

LANES = 128
SUBLANES = 8
ROW_TILE = 512
PROMPT_SCORE_BUFFERS = 8
VMEM_LIMIT = 48 * 1024 * 1024
NT_DIMS = (((1,), (1,)), ((), ()))


def _params(n_axes):
    return pltpu.CompilerParams(dimension_semantics=("arbitrary",) * n_axes,
                                vmem_limit_bytes=VMEM_LIMIT)


def _resident(shape):
    return pl.BlockSpec(shape, lambda *_: (0,) * len(shape), pipeline_mode=pl.Buffered(1))


def _layer_resident(shape, layer):
    return pl.BlockSpec((None,) + tuple(shape[1:]), lambda *_: (layer,) + (0,) * (len(shape) - 1),
                        pipeline_mode=pl.Buffered(1))


def _rms(x, g):
    ms = jnp.mean(x * x, axis=-1, keepdims=True)
    return (x * lax.rsqrt(ms + EPS)) * g


def _half_masks(dtype):
    lane = lax.broadcasted_iota(jnp.int32, (1, LANES), 1)
    lo = lane < (LANES // 2)
    return jnp.where(lo, 1.0, 0.0).astype(dtype), jnp.where(lo, 0.0, 1.0).astype(dtype)


def _lo64():
    return lax.broadcasted_iota(jnp.int32, (1, LANES), 1) < (LANES // 2)


def _ones_column(rows):
    return jnp.ones((rows, LANES), BF16)


def _ones_row(cols):
    return jnp.ones((LANES, cols), BF16)


def _mod_kernel(cond_ref, w_ref, b_ref, out_ref):
    c = cond_ref[...]
    s = c * jax.nn.sigmoid(c)
    out_ref[...] = jnp.dot(s.astype(BF16), w_ref[...].astype(BF16),
                           preferred_element_type=F32) + b_ref[...]


def _modulation(cond8, w_mod, b_mod):
    depth, d, n = w_mod.shape
    tn = 1536
    return pl.pallas_call(
        _mod_kernel,
        out_shape=jax.ShapeDtypeStruct((depth, 8, n), F32),
        grid=(depth, n // tn),
        in_specs=[pl.BlockSpec((8, d), lambda i, j: (0, 0)),
                  pl.BlockSpec((None, d, tn), lambda i, j: (i, 0, j)),
                  pl.BlockSpec((None, 1, tn), lambda i, j: (i, 0, j))],
        out_specs=pl.BlockSpec((None, 8, tn), lambda i, j: (i, 0, j)),
        compiler_params=_params(2),
        name="modulation",
    )(cond8, w_mod, b_mod.reshape(depth, 1, n))


def _mod_row(row0, tiles_per_batch):
    return row0 + pl.program_id(0) // tiles_per_batch


def _rope_slab(xs, cos, sin_signed, lo16):
    left = pltpu.roll(xs, LANES - 16, 1)
    right = pltpu.roll(xs, 16, 1)
    return xs * cos + jnp.where(lo16, left, right) * sin_signed


def _lo16_mask():
    lane = lax.broadcasted_iota(jnp.int32, (1, LANES), 1)
    return (lane % 32) < 16


def _pre_norm(x_ref, g_ref, mod_ref, r, d, slot):
    shift = mod_ref[pl.ds(r, 1), slot * d:(slot + 1) * d]
    scale = mod_ref[pl.ds(r, 1), (slot + 1) * d:(slot + 2) * d]
    return _rms(x_ref[...], g_ref[...]) * (1 + scale) + shift


def _qkv_diff_kernel(x_ref, g_ref, mod_ref, w_ref, *rest, latent, row0, tiles_per_batch):
    d = x_ref.shape[1]
    if latent:
        cos_ref, sin_ref, q_ref, k_ref, v_ref = rest
        cos, sin = cos_ref[...], sin_ref[...]
        lo16 = _lo16_mask()
    else:
        q_ref, k_ref, v_ref = rest
    r = _mod_row(row0, tiles_per_batch)
    h = _pre_norm(x_ref, g_ref, mod_ref, r, d, 0).astype(BF16)
    cw = 512
    for c in range(3 * d // cw):
        acc = jnp.dot(h, w_ref[:, c * cw:(c + 1) * cw], preferred_element_type=F32)
        which, off = divmod(c * cw, d)
        dst = (q_ref, k_ref, v_ref)[which]
        for s in range(cw // LANES):
            xs = acc[:, s * LANES:(s + 1) * LANES]
            if latent and which < 2:
                xs = _rope_slab(xs, cos, sin, lo16)
            if which == 0:
                xs = xs * (DIFF_HD ** -0.5)
            lo = off + s * LANES
            dst[:, lo:lo + LANES] = xs.astype(dst.dtype)


def _qkv_diff(x, g, mods, w, tables, *, layer, j, latent, row0, tiles_per_batch):
    n, d = x.shape
    tm = ROW_TILE
    in_specs = [pl.BlockSpec((tm, d), lambda i: (i, 0)),
                _resident((1, d)), _layer_resident(mods.shape, layer), _layer_resident(w.shape, j)]
    args = [x, g, mods, w]
    kv_dtype = F32
    if latent:
        nt = tables[0].shape[0] // tm
        in_specs += [pl.BlockSpec((tm, LANES), lambda i: (i % nt, 0))] * 2
        args += list(tables)
        kv_dtype = BF16
    out_spec = pl.BlockSpec((tm, d), lambda i: (i, 0))
    return pl.pallas_call(
        functools.partial(_qkv_diff_kernel, latent=latent, row0=row0, tiles_per_batch=tiles_per_batch),
        out_shape=(jax.ShapeDtypeStruct((n, d), BF16),
                   jax.ShapeDtypeStruct((n, d), kv_dtype),
                   jax.ShapeDtypeStruct((n, d), kv_dtype)),
        grid=(n // tm,),
        in_specs=in_specs,
        out_specs=(out_spec, out_spec, out_spec),
        compiler_params=_params(1),
        name="qkv_diff_latent" if latent else "qkv_diff_prompt",
    )(*args)


def _qkv_swa_kernel(x_ref, g_ref, mod_ref, w_ref, *rest, latent, row0, tiles_per_batch, seq):
    d = x_ref.shape[1]
    nkv = SWA_KV_HEADS * SWA_HD
    if latent:
        cos_ref, sin_ref, q_ref, kd_ref, vd_ref = rest
        cos, sin = cos_ref[...], sin_ref[...]
        lo16 = _lo16_mask()
    else:
        q_ref, kt_ref, vt_ref = rest
    lo64 = _lo64()
    r = _mod_row(row0, tiles_per_batch)
    h = _pre_norm(x_ref, g_ref, mod_ref, r, d, 0).astype(BF16)
    cw = 512
    for c in range(d // cw):
        acc = jnp.dot(h, w_ref[:, c * cw:(c + 1) * cw], preferred_element_type=F32)
        for s in range(cw // LANES):
            xs = acc[:, s * LANES:(s + 1) * LANES]
            if latent:
                xs = _rope_slab(xs, cos, sin, lo16)
            lo = c * cw + s * LANES
            q_ref[:, lo:lo + LANES] = (xs * (SWA_HD ** -0.5)).astype(BF16)
    kv = jnp.dot(h, w_ref[:, d:d + 2 * nkv], preferred_element_type=F32)
    if not latent:
        for b in range(x_ref.shape[0] // seq):
            kt_ref[b] = kv[b * seq:(b + 1) * seq, :nkv].T
            vt_ref[b] = kv[b * seq:(b + 1) * seq, nkv:].T
        return
    for which, dst in enumerate((kd_ref, vd_ref)):
        for s in range(nkv // LANES):
            xs = kv[:, which * nkv + s * LANES: which * nkv + (s + 1) * LANES]
            if which == 0:
                xs = _rope_slab(xs, cos, sin, lo16)
            sw = pltpu.roll(xs, LANES // 2, 1)
            dst[:, (2 * s) * LANES:(2 * s + 1) * LANES] = jnp.where(lo64, xs, sw).astype(BF16)
            dst[:, (2 * s + 1) * LANES:(2 * s + 2) * LANES] = jnp.where(lo64, sw, xs).astype(BF16)


def _qkv_swa(x, g, mods, w, tables, *, layer, j, latent, row0, tiles_per_batch, seq):
    n, d = x.shape
    tm = ROW_TILE
    nkv = SWA_KV_HEADS * SWA_HD
    in_specs = [pl.BlockSpec((tm, d), lambda i: (i, 0)),
                _resident((1, d)), _layer_resident(mods.shape, layer), _layer_resident(w.shape, j)]
    args = [x, g, mods, w]
    out_shape = [jax.ShapeDtypeStruct((n, d), BF16)]
    out_specs = [pl.BlockSpec((tm, d), lambda i: (i, 0))]
    if latent:
        nt = tables[0].shape[0] // tm
        in_specs += [pl.BlockSpec((tm, LANES), lambda i: (i % nt, 0))] * 2
        args += list(tables)
        out_shape += [jax.ShapeDtypeStruct((n, 2 * nkv), BF16)] * 2
        out_specs += [pl.BlockSpec((tm, 2 * nkv), lambda i: (i, 0))] * 2
    else:
        out_shape += [jax.ShapeDtypeStruct((n // seq, nkv, seq), F32)] * 2
        out_specs += [pl.BlockSpec((tm // seq, nkv, seq), lambda i: (i, 0, 0))] * 2
    return pl.pallas_call(
        functools.partial(_qkv_swa_kernel, latent=latent, row0=row0, tiles_per_batch=tiles_per_batch,
                          seq=seq),
        out_shape=tuple(out_shape),
        grid=(n // tm,),
        in_specs=in_specs,
        out_specs=tuple(out_specs),
        compiler_params=_params(1),
        name="qkv_swa_latent" if latent else "qkv_swa_prompt",
    )(*args)


def _diff_lambda(lam_ref, lam_init):
    lp = lam_ref[...]
    a = jnp.sum(lp[0:1] * lp[1:2], axis=-1, keepdims=True)
    b = jnp.sum(lp[2:3] * lp[3:4], axis=-1, keepdims=True)
    return jnp.exp(a) - jnp.exp(b) + lam_init


def _diff_combine(acc, tq, lam, g, lam_init):
    o12 = acc[:, :LANES] / acc[:, LANES:]
    o = o12[:tq] - lam * o12[tq:]
    return _rms(o, g) * (1.0 - lam_init)


def _stack_maps(q):
    m_lo, m_hi = _half_masks(BF16)
    return jnp.concatenate([q * m_lo, q * m_hi], axis=0)


def _run_pipelined(items, scores, finish, s_bufs):
    depth = len(s_bufs)
    states = {i: scores(items[i], s_bufs[i]) for i in range(min(depth - 1, len(items)))}
    for i, item in enumerate(items):
        ahead = i + depth - 1
        if ahead < len(items):
            states[ahead] = scores(items[ahead], s_bufs[ahead % depth])
        finish(item, s_bufs[i % depth], states.pop(i))


def _store_scores(s_ref, col0, s, mrun):
    s_ref[:, col0:col0 + s.shape[1]] = s
    for t in range(s.shape[1] // LANES):
        blk = s[:, t * LANES:(t + 1) * LANES]
        mrun = blk if mrun is None else jnp.maximum(mrun, blk)
    return mrun


def _exp_block(s_ref, col0, width, mb):
    return jnp.concatenate(
        [jnp.exp(s_ref[:, col0 + t * LANES:col0 + (t + 1) * LANES] - mb).astype(BF16)
         for t in range(width // LANES)], axis=1)


def _diff_prompt_kernel(q_ref, k_ref, v_ref, lam_ref, g_ref, o_ref, *s_bufs, lam_init, seq):
    lam = _diff_lambda(lam_ref, lam_init)
    g = g_ref[...]
    ones = _ones_column(seq)
    items = [(r, h) for r in range(q_ref.shape[0] // seq) for h in range(DIFF_HEADS)]

    def scores(item, s_ref):
        r, h = item
        rows, sl = slice(r * seq, (r + 1) * seq), slice(h * LANES, (h + 1) * LANES)
        s = lax.dot_general(_stack_maps(q_ref[rows, sl]), k_ref[rows, sl].astype(BF16), NT_DIMS,
                            preferred_element_type=F32)
        return _store_scores(s_ref, 0, s, None)

    def finish(item, s_ref, mrun):
        r, h = item
        rows, sl = slice(r * seq, (r + 1) * seq), slice(h * LANES, (h + 1) * LANES)
        mb = jnp.broadcast_to(jnp.max(mrun, axis=-1, keepdims=True), (2 * seq, LANES))
        vx = jnp.concatenate([v_ref[rows, sl].astype(BF16), ones], axis=1)
        acc = jnp.dot(_exp_block(s_ref, 0, seq, mb), vx, preferred_element_type=F32)
        o_ref[rows, sl] = _diff_combine(acc, seq, lam, g, lam_init).astype(BF16)

    _run_pipelined(items, scores, finish, s_bufs)


def _diff_prompt_attention(q, k, v, lam_params, subln_g, *, j, seq, lam_init):
    n, d = q.shape
    req = 2
    spec = pl.BlockSpec((req * seq, d), lambda b: (b, 0))
    return pl.pallas_call(
        functools.partial(_diff_prompt_kernel, lam_init=lam_init, seq=seq),
        out_shape=jax.ShapeDtypeStruct((n, d), BF16),
        grid=(n // (req * seq),),
        in_specs=[spec, spec, spec, _layer_resident(lam_params.shape, j), _layer_resident(subln_g.shape, j)],
        out_specs=spec,
        scratch_shapes=[pltpu.VMEM((2 * seq, seq), F32)] * PROMPT_SCORE_BUFFERS,
        compiler_params=_params(1),
        name="diff_attn_prompt",
    )(q, k, v, lam_params, subln_g)


def _diff_latent_kernel(q_ref, kc_ref, vc_ref, kl_ref, vl_ref, lam_ref, g_ref, o_ref,
                        kk_ref, vx_ref, s0_ref, s1_ref, *, lam_init, lc, tq, key_chunk):
    seq = q_ref.shape[0]
    nk = kk_ref.shape[0]
    h = pl.program_id(1)
    kk_ref[0:lc, :] = kc_ref[pl.ds(h, lc, stride=DIFF_HEADS), :].astype(BF16)
    kk_ref[lc:, :] = kl_ref[...]
    vx_ref[0:lc, 0:LANES] = vc_ref[pl.ds(h, lc, stride=DIFF_HEADS), :].astype(BF16)
    vx_ref[lc:, 0:LANES] = vl_ref[...]
    vx_ref[:, LANES:2 * LANES] = _ones_column(nk)

    masks = _half_masks(BF16)
    nchunk = nk // key_chunk
    items = [(rt, m) for rt in range(seq // tq) for m in range(2)]
    lam = _diff_lambda(lam_ref, lam_init)
    g = g_ref[...]
    first_map = {}

    def scores(item, s_ref):
        rt, m = item
        q = q_ref[rt * tq:(rt + 1) * tq, :] * masks[m]
        mrun = None
        for c in range(nchunk):
            s = lax.dot_general(q, kk_ref[c * key_chunk:(c + 1) * key_chunk, :], NT_DIMS,
                                preferred_element_type=F32)
            mrun = _store_scores(s_ref, c * key_chunk, s, mrun)
        return mrun

    def finish(item, s_ref, mrun):
        rt, m = item
        mb = jnp.broadcast_to(jnp.max(mrun, axis=-1, keepdims=True), (tq, LANES))
        acc = None
        for c in range(nchunk):
            part = jnp.dot(_exp_block(s_ref, c * key_chunk, key_chunk, mb),
                           vx_ref[c * key_chunk:(c + 1) * key_chunk, :], preferred_element_type=F32)
            acc = part if acc is None else acc + part
        o_m = acc[:, :LANES] / acc[:, LANES:]
        if m == 0:
            first_map[rt] = o_m
        else:
            o = _rms(first_map.pop(rt) - lam * o_m, g) * (1.0 - lam_init)
            o_ref[rt * tq:(rt + 1) * tq, :] = o.astype(BF16)

    _run_pipelined(items, scores, finish, (s0_ref, s1_ref))


def _diff_latent_attention(q, k, v, cache_k, cache_v, lam_params, subln_g, *, j, seq, lc, lam_init):
    n, d = q.shape
    nb = cache_k.shape[0]
    tq = 512
    key_chunk = 512
    q_spec = pl.BlockSpec((seq, LANES), lambda b, h: (b, h))
    c_spec = pl.BlockSpec((None, lc * DIFF_HEADS, LANES), lambda b, h: (b, j, 0))
    return pl.pallas_call(
        functools.partial(_diff_latent_kernel, lam_init=lam_init, lc=lc, tq=tq, key_chunk=key_chunk),
        out_shape=jax.ShapeDtypeStruct((n, d), BF16),
        grid=(nb, DIFF_HEADS),
        in_specs=[q_spec, c_spec, c_spec, q_spec, q_spec,
                  _layer_resident(lam_params.shape, j), _layer_resident(subln_g.shape, j)],
        out_specs=q_spec,
        scratch_shapes=[pltpu.VMEM((lc + seq, LANES), BF16), pltpu.VMEM((lc + seq, 2 * LANES), BF16),
                        pltpu.VMEM((tq, lc + seq), F32), pltpu.VMEM((tq, lc + seq), F32)],
        compiler_params=_params(2),
        name="diff_attn_latent",
    )(q, cache_k, cache_v, k, v, lam_params, subln_g)


def _stack_group(q_ref, rows, kv_local):
    m_lo, m_hi = _half_masks(BF16)
    parts = []
    for gb in range(SWA_GROUP // 2):
        blk = kv_local * (SWA_GROUP // 2) + gb
        qb = q_ref[rows, blk * LANES:(blk + 1) * LANES]
        parts += [qb * m_lo, qb * m_hi]
    return jnp.concatenate(parts, axis=0)


def _sink_column(sink_ref, first_head, tq):
    return jnp.concatenate([jnp.full((tq, LANES), sink_ref[first_head + g], F32) for g in range(SWA_GROUP)],
                           axis=0)


def _write_group(o_ref, rows, kv_local, o, tq):
    lo64 = _lo64()
    for gb in range(SWA_GROUP // 2):
        blk = kv_local * (SWA_GROUP // 2) + gb
        even = o[(2 * gb) * tq:(2 * gb + 1) * tq]
        odd = o[(2 * gb + 1) * tq:(2 * gb + 2) * tq]
        o_ref[rows, blk * LANES:(blk + 1) * LANES] = jnp.where(lo64, even, odd).astype(BF16)


def _dup_rows(x_t):
    xb = x_t.astype(BF16)
    return jnp.concatenate([xb, xb], axis=0)


def _sink_finish(mrun, sk, rows):
    mb = jnp.maximum(jnp.broadcast_to(jnp.max(mrun, axis=-1, keepdims=True), (rows, LANES)), sk)
    return mb, jnp.exp(sk - mb)


def _swa_prompt_kernel(sink_ref, q_ref, kt_ref, vt_ref, o_ref, *s_bufs):
    seq = kt_ref.shape[2]
    rows = SWA_GROUP * seq
    ones = _ones_row(seq)
    items = [(r, j) for r in range(kt_ref.shape[0]) for j in range(SWA_KV_HEADS)]

    def scores(item, s_ref):
        r, j = item
        kd = _dup_rows(kt_ref[r, j * SWA_HD:(j + 1) * SWA_HD, :])
        s = jnp.dot(_stack_group(q_ref, slice(r * seq, (r + 1) * seq), j), kd, preferred_element_type=F32)
        return _store_scores(s_ref, 0, s, None)

    def finish(item, s_ref, mrun):
        r, j = item
        sk = _sink_column(sink_ref, j * SWA_GROUP, seq)
        mb, sink_term = _sink_finish(mrun, sk, rows)
        vx = jnp.concatenate([_dup_rows(vt_ref[r, j * SWA_HD:(j + 1) * SWA_HD, :]), ones], axis=0)
        acc = lax.dot_general(_exp_block(s_ref, 0, seq, mb), vx, NT_DIMS, preferred_element_type=F32)
        o = acc[:, :LANES] / (acc[:, LANES:] + sink_term)
        _write_group(o_ref, slice(r * seq, (r + 1) * seq), j, o, seq)

    _run_pipelined(items, scores, finish, s_bufs)


def _swa_prompt_attention(q, kt, vt, sink, *, seq):
    n, d = q.shape
    nkv = kt.shape[1]
    req = 2
    t_spec = pl.BlockSpec((req, nkv, seq), lambda b: (b, 0, 0))
    return pl.pallas_call(
        _swa_prompt_kernel,
        out_shape=jax.ShapeDtypeStruct((n, d), BF16),
        grid=(n // (req * seq),),
        in_specs=[pl.BlockSpec(memory_space=pltpu.SMEM),
                  pl.BlockSpec((req * seq, d), lambda b: (b, 0)), t_spec, t_spec],
        out_specs=pl.BlockSpec((req * seq, d), lambda b: (b, 0)),
        scratch_shapes=[pltpu.VMEM((SWA_GROUP * seq, seq), F32)] * (PROMPT_SCORE_BUFFERS // 2),
        compiler_params=_params(1),
        name="swa_attn_prompt",
    )(sink, q, kt, vt)


def _swa_latent_kernel(sink_ref, q_ref, kc_ref, vc_ref, kl_ref, vl_ref, o_ref, kcd_ref, vcx_ref,
                       s0_ref, s1_ref, *, tq, span):
    pair = pl.program_id(1)
    tiles = q_ref.shape[0] // tq
    first_tile = pl.program_id(2) * tiles
    seq = kl_ref.shape[0]
    lc = kc_ref.shape[1]
    rows = SWA_GROUP * tq
    ones_row = _ones_row(lc)
    for jj in range(2):
        kcd_ref[jj] = _dup_rows(kc_ref[jj * SWA_HD:(jj + 1) * SWA_HD, :])
        vcx_ref[jj] = jnp.concatenate([_dup_rows(vc_ref[jj * SWA_HD:(jj + 1) * SWA_HD, :]), ones_row], axis=0)
    ones_col = _ones_column(span)
    items = [(t, jj) for t in range(tiles) for jj in range(2)]
    windows, biases = {}, {}

    def window(t):
        if t not in windows:
            q0 = (first_tile + t) * tq
            windows[t] = (q0, pl.multiple_of(jnp.clip(q0 - WINDOW, 0, seq - span), WINDOW))
        return windows[t]

    def bias_for(t):
        if t not in biases:
            q0, ws = window(t)
            q_pos = q0 + lax.broadcasted_iota(jnp.int32, (tq, span), 0)
            k_pos = ws + lax.broadcasted_iota(jnp.int32, (tq, span), 1)
            b = jnp.where(jnp.abs(q_pos - k_pos) <= WINDOW, 0.0, NEG_INF).astype(F32)
            biases[t] = jnp.concatenate([b] * SWA_GROUP, axis=0)
        return biases[t]

    def scores(item, s_ref):
        t, jj = item
        _, ws = window(t)
        qs = _stack_group(q_ref, slice(t * tq, (t + 1) * tq), jj)
        s_c = jnp.dot(qs, kcd_ref[jj], preferred_element_type=F32)
        mrun = _store_scores(s_ref, 0, s_c, None)
        s_w = lax.dot_general(qs, kl_ref[pl.ds(ws, span), jj * LANES:(jj + 1) * LANES], NT_DIMS,
                              preferred_element_type=F32) + bias_for(t)
        return _store_scores(s_ref, lc, s_w, mrun)

    def finish(item, s_ref, mrun):
        t, jj = item
        _, ws = window(t)
        sk = _sink_column(sink_ref, (2 * pair + jj) * SWA_GROUP, tq)
        mb, sink_term = _sink_finish(mrun, sk, rows)
        vwx = jnp.concatenate([vl_ref[pl.ds(ws, span), jj * LANES:(jj + 1) * LANES], ones_col], axis=1)
        acc = (lax.dot_general(_exp_block(s_ref, 0, lc, mb), vcx_ref[jj], NT_DIMS, preferred_element_type=F32)
               + jnp.dot(_exp_block(s_ref, lc, span, mb), vwx, preferred_element_type=F32))
        o = acc[:, :LANES] / (acc[:, LANES:] + sink_term)
        _write_group(o_ref, slice(t * tq, (t + 1) * tq), jj, o, tq)

    _run_pipelined(items, scores, finish, (s0_ref, s1_ref))


def _swa_latent_attention(q, kd, vd, cache_kt, cache_vt, sink, *, j, seq):
    n, d = q.shape
    nb, _, lc = cache_kt.shape
    tq = 256
    span = tq + 2 * WINDOW
    npair = SWA_KV_HEADS // 2
    wq = d // npair
    parts = 2
    q_spec = pl.BlockSpec((seq // parts, wq), lambda b, p, i: (b * parts + i, p))
    c_spec = pl.BlockSpec((None, 2 * SWA_HD, lc), lambda b, p, i: (b, j * npair + p, 0))
    l_spec = pl.BlockSpec((seq, 2 * LANES), lambda b, p, i: (b, p))
    s_shape = pltpu.VMEM((SWA_GROUP * tq, lc + span), F32)
    return pl.pallas_call(
        functools.partial(_swa_latent_kernel, tq=tq, span=span),
        out_shape=jax.ShapeDtypeStruct((n, d), BF16),
        grid=(nb, npair, parts),
        in_specs=[pl.BlockSpec(memory_space=pltpu.SMEM), q_spec, c_spec, c_spec, l_spec, l_spec],
        out_specs=q_spec,
        scratch_shapes=[pltpu.VMEM((2, 2 * SWA_HD, lc), BF16), pltpu.VMEM((2, 2 * LANES, lc), BF16),
                        s_shape, s_shape],
        compiler_params=_params(3),
        name="swa_attn_latent",
    )(sink, q, cache_kt, cache_vt, kd, vd)


def _out_proj_kernel(o_ref, w_ref, x_ref, g1_ref, g2_ref, mod_ref, xo_ref, h_ref, *, row0, tiles_per_batch):
    d = x_ref.shape[1]
    r = _mod_row(row0, tiles_per_batch)
    gate = mod_ref[pl.ds(r, 1), 2 * d:3 * d]
    y = jnp.dot(o_ref[...], w_ref[...], preferred_element_type=F32)
    x = x_ref[...] + gate * _rms(y, g1_ref[...])
    xo_ref[...] = x
    shift = mod_ref[pl.ds(r, 1), 3 * d:4 * d]
    scale = mod_ref[pl.ds(r, 1), 4 * d:5 * d]
    h_ref[...] = (_rms(x, g2_ref[...]) * (1 + scale) + shift).astype(BF16)


def _out_proj(o, w, x, g1, g2, mods, *, layer, j, row0, tiles_per_batch):
    n, d = x.shape
    tm = ROW_TILE
    row = pl.BlockSpec((tm, d), lambda i: (i, 0))
    return pl.pallas_call(
        functools.partial(_out_proj_kernel, row0=row0, tiles_per_batch=tiles_per_batch),
        out_shape=(jax.ShapeDtypeStruct((n, d), F32), jax.ShapeDtypeStruct((n, d), BF16)),
        grid=(n // tm,),
        in_specs=[row, _layer_resident(w.shape, j), row, _resident((1, d)), _resident((1, d)),
                  _layer_resident(mods.shape, layer)],
        out_specs=(row, row),
        compiler_params=_params(1),
        name="out_proj",
    )(o, w, x, g1, g2, mods)


def _ffn_kernel(h_ref, wg_ref, wu_ref, wd_ref, x_ref, g_ref, mod_ref, xo_ref, *, row0, tiles_per_batch):
    d = x_ref.shape[1]
    dff = wg_ref.shape[1]
    r = _mod_row(row0, tiles_per_batch)
    gate = mod_ref[pl.ds(r, 1), 5 * d:6 * d]
    h = h_ref[...]
    cw = 256
    y = jnp.zeros((h.shape[0], d), F32)
    for c in range(dff // cw):
        a = jnp.dot(h, wg_ref[:, c * cw:(c + 1) * cw], preferred_element_type=F32)
        u = jnp.dot(h, wu_ref[:, c * cw:(c + 1) * cw], preferred_element_type=F32)
        t = (a * jax.nn.sigmoid(a)) * u
        y = y + jnp.dot(t.astype(BF16), wd_ref[c * cw:(c + 1) * cw, :], preferred_element_type=F32)
    xo_ref[...] = x_ref[...] + gate * _rms(y, g_ref[...])


def _ffn(h, wg, wu, wd, x, g, mods, *, layer, row0, tiles_per_batch):
    n, d = x.shape
    tm = ROW_TILE
    row = pl.BlockSpec((tm, d), lambda i: (i, 0))
    return pl.pallas_call(
        functools.partial(_ffn_kernel, row0=row0, tiles_per_batch=tiles_per_batch),
        out_shape=jax.ShapeDtypeStruct((n, d), F32),
        grid=(n // tm,),
        in_specs=[row, _layer_resident(wg.shape, layer), _layer_resident(wu.shape, layer),
                  _layer_resident(wd.shape, layer), row, _resident((1, d)),
                  _layer_resident(mods.shape, layer)],
        out_specs=row,
        compiler_params=_params(1),
        name="ffn",
    )(h, wg, wu, wd, x, g, mods)


def _rope_tables(n_lat):
    t = jnp.arange(n_lat)
    row = (t // GRID_W).astype(F32)
    col = (t % GRID_W).astype(F32)
    nf = ROT_DIM // 4
    inv = ROPE_BASE ** (-jnp.arange(nf, dtype=F32) / nf)
    ar = row[:, None] * inv[None, :]
    ac = col[:, None] * inv[None, :]
    ang = jnp.concatenate([ar, ar, ac, ac], axis=-1)
    cos, sin = jnp.cos(ang), jnp.sin(ang)
    sign = jnp.where((jnp.arange(ROT_DIM) % 32) < 16, -1.0, 1.0).astype(F32)
    reps = LANES // ROT_DIM
    return jnp.tile(cos, (1, reps)), jnp.tile(sin * sign, (1, reps))


def _swa_cache_to_feature_major(cache):
    nb, nl, lc, nh, hd = cache.shape
    return cache.transpose(0, 1, 3, 4, 2).reshape(nb, nl * nh * hd, lc)


def _swa_cache_from_feature_major(xt, seq):
    nb = xt.shape[0]
    return xt.reshape(nb, SWA_KV_HEADS, SWA_HD, seq).transpose(0, 3, 1, 2)


def kernel(x_prompt, x_sample, cache_diff_k, cache_diff_v, cache_swa_k, cache_swa_v, c, c_ctx,
           w_mod, b_mod, norm_g, w_qkv_diff, diff_lambda, diff_subln_g, w_o_diff,
           w_qkv_swa, swa_sink, w_o_swa, w_gate, w_up, w_down):
    bp, lp, d = x_prompt.shape
    bs, ls, _ = x_sample.shape
    lc = cache_diff_k.shape[2]
    depth = w_mod.shape[0]
    tm = ROW_TILE

    cond8 = jnp.concatenate([c_ctx[None, :], c, jnp.zeros((8 - 1 - bs, d), F32)], axis=0)
    mods = _modulation(cond8, w_mod, b_mod)
    tables = _rope_tables(ls)

    w_qkv_diff, w_o_diff = w_qkv_diff.astype(BF16), w_o_diff.astype(BF16)
    w_qkv_swa, w_o_swa = w_qkv_swa.astype(BF16), w_o_swa.astype(BF16)
    w_gate, w_up, w_down = w_gate.astype(BF16), w_up.astype(BF16), w_down.astype(BF16)
    cdk = cache_diff_k.reshape(bs, -1, 2 * DIFF_HD)
    cdv = cache_diff_v.reshape(bs, -1, 2 * DIFF_HD)
    cskt = _swa_cache_to_feature_major(cache_swa_k)
    csvt = _swa_cache_to_feature_major(cache_swa_v)

    xp = x_prompt.reshape(bp * lp, d)
    xs = x_sample.reshape(bs * ls, d)
    p_rows = dict(row0=0, tiles_per_batch=bp * lp // tm)
    s_rows = dict(row0=1, tiles_per_batch=ls // tm)
    diff_k_out, diff_v_out, swa_k_out, swa_v_out = [], [], [], []

    for i in range(depth):
        g = norm_g[i].reshape(4, 1, d)
        j = i // N_MIXERS
        if i % N_MIXERS == 0:
            lam_init = 0.8 - 0.6 * math.exp(-0.3 * i)
            sub_g = diff_subln_g.reshape(-1, 1, 2 * DIFF_HD)
            qp, kp, vp = _qkv_diff(xp, g[0], mods, w_qkv_diff, None, layer=i, j=j, latent=False, **p_rows)
            op = _diff_prompt_attention(qp, kp, vp, diff_lambda, sub_g, j=j, seq=lp, lam_init=lam_init)
            diff_k_out.append(kp.reshape(bp, lp, DIFF_HEADS, 2 * DIFF_HD))
            diff_v_out.append(vp.reshape(bp, lp, DIFF_HEADS, 2 * DIFF_HD))
            qs, ks, vs = _qkv_diff(xs, g[0], mods, w_qkv_diff, tables, layer=i, j=j, latent=True, **s_rows)
            os_ = _diff_latent_attention(qs, ks, vs, cdk, cdv, diff_lambda, sub_g,
                                         j=j, seq=ls, lc=lc, lam_init=lam_init)
            w_o = w_o_diff
        else:
            qp, ktp, vtp = _qkv_swa(xp, g[0], mods, w_qkv_swa, None, layer=i, j=j, latent=False,
                                    seq=lp, **p_rows)
            op = _swa_prompt_attention(qp, ktp, vtp, swa_sink[j], seq=lp)
            swa_k_out.append(_swa_cache_from_feature_major(ktp, lp))
            swa_v_out.append(_swa_cache_from_feature_major(vtp, lp))
            qs, kds, vds = _qkv_swa(xs, g[0], mods, w_qkv_swa, tables, layer=i, j=j, latent=True,
                                    seq=ls, **s_rows)
            os_ = _swa_latent_attention(qs, kds, vds, cskt, csvt, swa_sink[j], j=j, seq=ls)
            w_o = w_o_swa
        xp, hp = _out_proj(op, w_o, xp, g[1], g[2], mods, layer=i, j=j, **p_rows)
        xs, hs = _out_proj(os_, w_o, xs, g[1], g[2], mods, layer=i, j=j, **s_rows)
        xp = _ffn(hp, w_gate, w_up, w_down, xp, g[3], mods, layer=i, **p_rows)
        xs = _ffn(hs, w_gate, w_up, w_down, xs, g[3], mods, layer=i, **s_rows)

    return (xp.reshape(bp, lp, d), xs.reshape(bs, ls, d),
            jnp.stack(diff_k_out, axis=1), jnp.stack(diff_v_out, axis=1),
            jnp.stack(swa_k_out, axis=1), jnp.stack(swa_v_out, axis=1))
```

```python
import functools
import math

import jax
import jax.numpy as jnp
from jax import lax
from jax.experimental import pallas as pl
from jax.experimental.pallas import tpu as pltpu

F32 = jnp.float32
BF16 = jnp.bfloat16

GRID_W = 64
N_MIXERS = 2
DIFF_HEADS = 8
DIFF_HD = 64
SWA_HEADS = 16
SWA_KV_HEADS = 4
SWA_GROUP = SWA_HEADS // SWA_KV_HEADS
SWA_HD = 64
ROT_DIM = 64
WINDOW = 128
ROPE_BASE = 10000.0
EPS = 1e-6
NEG_INF = -1e30

LANES = 128
SUBLANES = 8
ROW_TILE = 512
PROMPT_SCORE_BUFFERS = 8
VMEM_LIMIT = 48 * 1024 * 1024
NT_DIMS = (((1,), (1,)), ((), ()))


def _params(n_axes):
    return pltpu.CompilerParams(dimension_semantics=("arbitrary",) * n_axes,
                                vmem_limit_bytes=VMEM_LIMIT)


def _resident(shape):
    return pl.BlockSpec(shape, lambda *_: (0,) * len(shape), pipeline_mode=pl.Buffered(1))


def _layer_resident(shape, layer):
    return pl.BlockSpec((None,) + tuple(shape[1:]), lambda *_: (layer,) + (0,) * (len(shape) - 1),
                        pipeline_mode=pl.Buffered(1))


def _rms(x, g):
    ms = jnp.mean(x * x, axis=-1, keepdims=True)
    return (x * lax.rsqrt(ms + EPS)) * g


def _half_masks(dtype):
    lane = lax.broadcasted_iota(jnp.int32, (1, LANES), 1)
    lo = lane < (LANES // 2)
    return jnp.where(lo, 1.0, 0.0).astype(dtype), jnp.where(lo, 0.0, 1.0).astype(dtype)


def _lo64():
    return lax.broadcasted_iota(jnp.int32, (1, LANES), 1) < (LANES // 2)


def _ones_column(rows):
    return jnp.ones((rows, LANES), BF16)


def _ones_row(cols):
    return jnp.ones((LANES, cols), BF16)


def _mod_kernel(cond_ref, w_ref, b_ref, out_ref):
    c = cond_ref[...]
    s = c * jax.nn.sigmoid(c)
    out_ref[...] = jnp.dot(s.astype(BF16), w_ref[...].astype(BF16),
                           preferred_element_type=F32) + b_ref[...]


def _modulation(cond8, w_mod, b_mod):
    depth, d, n = w_mod.shape
    tn = 1536
    return pl.pallas_call(
        _mod_kernel,
        out_shape=jax.ShapeDtypeStruct((depth, 8, n), F32),
        grid=(depth, n // tn),
        in_specs=[pl.BlockSpec((8, d), lambda i, j: (0, 0)),
                  pl.BlockSpec((None, d, tn), lambda i, j: (i, 0, j)),
                  pl.BlockSpec((None, 1, tn), lambda i, j: (i, 0, j))],
        out_specs=pl.BlockSpec((None, 8, tn), lambda i, j: (i, 0, j)),
        compiler_params=_params(2),
        name="modulation",
    )(cond8, w_mod, b_mod.reshape(depth, 1, n))


def _mod_row(row0, tiles_per_batch):
    return row0 + pl.program_id(0) // tiles_per_batch


def _rope_slab(xs, cos, sin_signed, lo16):
    left = pltpu.roll(xs, LANES - 16, 1)
    right = pltpu.roll(xs, 16, 1)
    return xs * cos + jnp.where(lo16, left, right) * sin_signed


def _lo16_mask():
    lane = lax.broadcasted_iota(jnp.int32, (1, LANES), 1)
    return (lane % 32) < 16


def _pre_norm(x_ref, g_ref, mod_ref, r, d, slot):
    shift = mod_ref[pl.ds(r, 1), slot * d:(slot + 1) * d]
    scale = mod_ref[pl.ds(r, 1), (slot + 1) * d:(slot + 2) * d]
    return _rms(x_ref[...], g_ref[0]) * (1 + scale) + shift


def _qkv_diff_kernel(x_ref, g_ref, mod_ref, w_ref, *rest, latent, row0, tiles_per_batch):
    d = x_ref.shape[1]
    if latent:
        cos_ref, sin_ref, q_ref, k_ref, v_ref = rest
        cos, sin = cos_ref[...], sin_ref[...]
        lo16 = _lo16_mask()
    else:
        q_ref, k_ref, v_ref = rest
    r = _mod_row(row0, tiles_per_batch)
    h = _pre_norm(x_ref, g_ref, mod_ref, r, d, 0).astype(BF16)
    cw = 512
    for c in range(3 * d // cw):
        acc = jnp.dot(h, w_ref[:, c * cw:(c + 1) * cw], preferred_element_type=F32)
        which, off = divmod(c * cw, d)
        dst = (q_ref, k_ref, v_ref)[which]
        for s in range(cw // LANES):
            xs = acc[:, s * LANES:(s + 1) * LANES]
            if latent and which < 2:
                xs = _rope_slab(xs, cos, sin, lo16)
            if which == 0:
                xs = xs * (DIFF_HD ** -0.5)
            lo = off + s * LANES
            dst[:, lo:lo + LANES] = xs.astype(dst.dtype)


def _qkv_diff(x, n, tile0, g, mods, w, tables, *, layer, j, latent, row0, tiles_per_batch):
    d = x.shape[1]
    tm = ROW_TILE
    in_specs = [pl.BlockSpec((tm, d), lambda i: (i + tile0, 0)), _layer_resident(g.shape, layer),
                _layer_resident(mods.shape, layer), _layer_resident(w.shape, j)]
    args = [x, g, mods, w]
    kv_dtype = F32
    if latent:
        nt = tables[0].shape[0] // tm
        in_specs += [pl.BlockSpec((tm, LANES), lambda i: (i % nt, 0))] * 2
        args += list(tables)
        kv_dtype = BF16
    out_spec = pl.BlockSpec((tm, d), lambda i: (i, 0))
    return pl.pallas_call(
        functools.partial(_qkv_diff_kernel, latent=latent, row0=row0, tiles_per_batch=tiles_per_batch),
        out_shape=(jax.ShapeDtypeStruct((n, d), BF16),
                   jax.ShapeDtypeStruct((n, d), kv_dtype),
                   jax.ShapeDtypeStruct((n, d), kv_dtype)),
        grid=(n // tm,),
        in_specs=in_specs,
        out_specs=(out_spec, out_spec, out_spec),
        compiler_params=_params(1),
        name="qkv_diff_latent" if latent else "qkv_diff_prompt",
    )(*args)


def _qkv_swa_kernel(x_ref, g_ref, mod_ref, w_ref, *rest, latent, row0, tiles_per_batch, seq):
    d = x_ref.shape[1]
    nkv = SWA_KV_HEADS * SWA_HD
    if latent:
        cos_ref, sin_ref, q_ref, kd_ref, vd_ref = rest
        cos, sin = cos_ref[...], sin_ref[...]
        lo16 = _lo16_mask()
    else:
        q_ref, kt_ref, vt_ref = rest
    lo64 = _lo64()
    r = _mod_row(row0, tiles_per_batch)
    h = _pre_norm(x_ref, g_ref, mod_ref, r, d, 0).astype(BF16)
    cw = 512
    for c in range(d // cw):
        acc = jnp.dot(h, w_ref[:, c * cw:(c + 1) * cw], preferred_element_type=F32)
        for s in range(cw // LANES):
            xs = acc[:, s * LANES:(s + 1) * LANES]
            if latent:
                xs = _rope_slab(xs, cos, sin, lo16)
            lo = c * cw + s * LANES
            q_ref[:, lo:lo + LANES] = (xs * (SWA_HD ** -0.5)).astype(BF16)
    kv = jnp.dot(h, w_ref[:, d:d + 2 * nkv], preferred_element_type=F32)
    if not latent:
        for b in range(x_ref.shape[0] // seq):
            kt_ref[b] = kv[b * seq:(b + 1) * seq, :nkv].T
            vt_ref[b] = kv[b * seq:(b + 1) * seq, nkv:].T
        return
    for which, dst in enumerate((kd_ref, vd_ref)):
        for s in range(nkv // LANES):
            xs = kv[:, which * nkv + s * LANES: which * nkv + (s + 1) * LANES]
            if which == 0:
                xs = _rope_slab(xs, cos, sin, lo16)
            sw = pltpu.roll(xs, LANES // 2, 1)
            dst[:, (2 * s) * LANES:(2 * s + 1) * LANES] = jnp.where(lo64, xs, sw).astype(BF16)
            dst[:, (2 * s + 1) * LANES:(2 * s + 2) * LANES] = jnp.where(lo64, sw, xs).astype(BF16)


def _qkv_swa(x, n, tile0, g, mods, w, tables, *, layer, j, latent, row0, tiles_per_batch, seq):
    d = x.shape[1]
    tm = ROW_TILE
    nkv = SWA_KV_HEADS * SWA_HD
    in_specs = [pl.BlockSpec((tm, d), lambda i: (i + tile0, 0)), _layer_resident(g.shape, layer),
                _layer_resident(mods.shape, layer), _layer_resident(w.shape, j)]
    args = [x, g, mods, w]
    out_shape = [jax.ShapeDtypeStruct((n, d), BF16)]
    out_specs = [pl.BlockSpec((tm, d), lambda i: (i, 0))]
    if latent:
        nt = tables[0].shape[0] // tm
        in_specs += [pl.BlockSpec((tm, LANES), lambda i: (i % nt, 0))] * 2
        args += list(tables)
        out_shape += [jax.ShapeDtypeStruct((n, 2 * nkv), BF16)] * 2
        out_specs += [pl.BlockSpec((tm, 2 * nkv), lambda i: (i, 0))] * 2
    else:
        out_shape += [jax.ShapeDtypeStruct((n // seq, nkv, seq), F32)] * 2
        out_specs += [pl.BlockSpec((tm // seq, nkv, seq), lambda i: (i, 0, 0))] * 2
    return pl.pallas_call(
        functools.partial(_qkv_swa_kernel, latent=latent, row0=row0, tiles_per_batch=tiles_per_batch,
                          seq=seq),
        out_shape=tuple(out_shape),
        grid=(n // tm,),
        in_specs=in_specs,
        out_specs=tuple(out_specs),
        compiler_params=_params(1),
        name="qkv_swa_latent" if latent else "qkv_swa_prompt",
    )(*args)


def _diff_lambda(lam_ref, lam_init):
    lp = lam_ref[...]
    a = jnp.sum(lp[0:1] * lp[1:2], axis=-1, keepdims=True)
    b = jnp.sum(lp[2:3] * lp[3:4], axis=-1, keepdims=True)
    return jnp.exp(a) - jnp.exp(b) + lam_init


def _diff_combine(acc, tq, lam, g, lam_init):
    o12 = acc[:, :LANES] / acc[:, LANES:]
    o = o12[:tq] - lam * o12[tq:]
    return _rms(o, g) * (1.0 - lam_init)


def _stack_maps(q):
    m_lo, m_hi = _half_masks(BF16)
    return jnp.concatenate([q * m_lo, q * m_hi], axis=0)


def _run_pipelined(items, scores, finish, s_bufs):
    depth = len(s_bufs)
    states = {i: scores(items[i], s_bufs[i]) for i in range(min(depth - 1, len(items)))}
    for i, item in enumerate(items):
        ahead = i + depth - 1
        if ahead < len(items):
            states[ahead] = scores(items[ahead], s_bufs[ahead % depth])
        finish(item, s_bufs[i % depth], states.pop(i))


def _store_scores(s_ref, col0, s, mrun):
    s_ref[:, col0:col0 + s.shape[1]] = s
    for t in range(s.shape[1] // LANES):
        blk = s[:, t * LANES:(t + 1) * LANES]
        mrun = blk if mrun is None else jnp.maximum(mrun, blk)
    return mrun


def _exp_block(s_ref, col0, width, mb):
    return jnp.concatenate(
        [jnp.exp(s_ref[:, col0 + t * LANES:col0 + (t + 1) * LANES] - mb).astype(BF16)
         for t in range(width // LANES)], axis=1)


def _diff_prompt_kernel(q_ref, k_ref, v_ref, lam_ref, g_ref, o_ref, *s_bufs, lam_init, seq):
    lam = _diff_lambda(lam_ref, lam_init)
    g = g_ref[...]
    ones = _ones_column(seq)
    items = [(r, h) for r in range(q_ref.shape[0] // seq) for h in range(DIFF_HEADS)]

    def scores(item, s_ref):
        r, h = item
        rows, sl = slice(r * seq, (r + 1) * seq), slice(h * LANES, (h + 1) * LANES)
        s = lax.dot_general(_stack_maps(q_ref[rows, sl]), k_ref[rows, sl].astype(BF16), NT_DIMS,
                            preferred_element_type=F32)
        return _store_scores(s_ref, 0, s, None)

    def finish(item, s_ref, mrun):
        r, h = item
        rows, sl = slice(r * seq, (r + 1) * seq), slice(h * LANES, (h + 1) * LANES)
        mb = jnp.broadcast_to(jnp.max(mrun, axis=-1, keepdims=True), (2 * seq, LANES))
        vx = jnp.concatenate([v_ref[rows, sl].astype(BF16), ones], axis=1)
        acc = jnp.dot(_exp_block(s_ref, 0, seq, mb), vx, preferred_element_type=F32)
        o_ref[rows, sl] = _diff_combine(acc, seq, lam, g, lam_init).astype(BF16)

    _run_pipelined(items, scores, finish, s_bufs)


def _diff_prompt_attention(q, k, v, lam_params, subln_g, *, j, seq, lam_init):
    n, d = q.shape
    req = 2
    spec = pl.BlockSpec((req * seq, d), lambda b: (b, 0))
    return pl.pallas_call(
        functools.partial(_diff_prompt_kernel, lam_init=lam_init, seq=seq),
        out_shape=jax.ShapeDtypeStruct((n, d), BF16),
        grid=(n // (req * seq),),
        in_specs=[spec, spec, spec, _layer_resident(lam_params.shape, j), _layer_resident(subln_g.shape, j)],
        out_specs=spec,
        scratch_shapes=[pltpu.VMEM((2 * seq, seq), F32)] * PROMPT_SCORE_BUFFERS,
        compiler_params=_params(1),
        name="diff_attn_prompt",
    )(q, k, v, lam_params, subln_g)


def _diff_latent_kernel(q_ref, kc_ref, vc_ref, kl_ref, vl_ref, lam_ref, g_ref, o_ref,
                        kk_ref, vx_ref, s0_ref, s1_ref, *, lam_init, lc, tq, key_chunk):
    seq = q_ref.shape[0]
    nk = kk_ref.shape[0]
    h = pl.program_id(1)
    kk_ref[0:lc, :] = kc_ref[pl.ds(h, lc, stride=DIFF_HEADS), :].astype(BF16)
    kk_ref[lc:, :] = kl_ref[...]
    vx_ref[0:lc, 0:LANES] = vc_ref[pl.ds(h, lc, stride=DIFF_HEADS), :].astype(BF16)
    vx_ref[lc:, 0:LANES] = vl_ref[...]
    vx_ref[:, LANES:2 * LANES] = _ones_column(nk)

    masks = _half_masks(BF16)
    nchunk = nk // key_chunk
    items = [(rt, m) for rt in range(seq // tq) for m in range(2)]
    lam = _diff_lambda(lam_ref, lam_init)
    g = g_ref[...]
    first_map = {}

    def scores(item, s_ref):
        rt, m = item
        q = q_ref[rt * tq:(rt + 1) * tq, :] * masks[m]
        mrun = None
        for c in range(nchunk):
            s = lax.dot_general(q, kk_ref[c * key_chunk:(c + 1) * key_chunk, :], NT_DIMS,
                                preferred_element_type=F32)
            mrun = _store_scores(s_ref, c * key_chunk, s, mrun)
        return mrun

    def finish(item, s_ref, mrun):
        rt, m = item
        mb = jnp.broadcast_to(jnp.max(mrun, axis=-1, keepdims=True), (tq, LANES))
        acc = None
        for c in range(nchunk):
            part = jnp.dot(_exp_block(s_ref, c * key_chunk, key_chunk, mb),
                           vx_ref[c * key_chunk:(c + 1) * key_chunk, :], preferred_element_type=F32)
            acc = part if acc is None else acc + part
        o_m = acc[:, :LANES] / acc[:, LANES:]
        if m == 0:
            first_map[rt] = o_m
        else:
            o = _rms(first_map.pop(rt) - lam * o_m, g) * (1.0 - lam_init)
            o_ref[rt * tq:(rt + 1) * tq, :] = o.astype(BF16)

    _run_pipelined(items, scores, finish, (s0_ref, s1_ref))


def _diff_latent_attention(q, k, v, cache_k, cache_v, lam_params, subln_g, *, j, seq, lc, lam_init):
    n, d = q.shape
    nb = cache_k.shape[0]
    tq = 512
    key_chunk = 512
    q_spec = pl.BlockSpec((seq, LANES), lambda b, h: (b, h))
    c_spec = pl.BlockSpec((None, lc * DIFF_HEADS, LANES), lambda b, h: (b, j, 0))
    return pl.pallas_call(
        functools.partial(_diff_latent_kernel, lam_init=lam_init, lc=lc, tq=tq, key_chunk=key_chunk),
        out_shape=jax.ShapeDtypeStruct((n, d), BF16),
        grid=(nb, DIFF_HEADS),
        in_specs=[q_spec, c_spec, c_spec, q_spec, q_spec,
                  _layer_resident(lam_params.shape, j), _layer_resident(subln_g.shape, j)],
        out_specs=q_spec,
        scratch_shapes=[pltpu.VMEM((lc + seq, LANES), BF16), pltpu.VMEM((lc + seq, 2 * LANES), BF16),
                        pltpu.VMEM((tq, lc + seq), F32), pltpu.VMEM((tq, lc + seq), F32)],
        compiler_params=_params(2),
        name="diff_attn_latent",
    )(q, cache_k, cache_v, k, v, lam_params, subln_g)


def _stack_group(q_ref, rows, kv_local):
    m_lo, m_hi = _half_masks(BF16)
    parts = []
    for gb in range(SWA_GROUP // 2):
        blk = kv_local * (SWA_GROUP // 2) + gb
        qb = q_ref[rows, blk * LANES:(blk + 1) * LANES]
        parts += [qb * m_lo, qb * m_hi]
    return jnp.concatenate(parts, axis=0)


def _sink_column(sink_ref, first_head, tq):
    return jnp.concatenate([jnp.full((tq, LANES), sink_ref[first_head + g], F32) for g in range(SWA_GROUP)],
                           axis=0)


def _write_group(o_ref, rows, kv_local, o, tq):
    lo64 = _lo64()
    for gb in range(SWA_GROUP // 2):
        blk = kv_local * (SWA_GROUP // 2) + gb
        even = o[(2 * gb) * tq:(2 * gb + 1) * tq]
        odd = o[(2 * gb + 1) * tq:(2 * gb + 2) * tq]
        o_ref[rows, blk * LANES:(blk + 1) * LANES] = jnp.where(lo64, even, odd).astype(BF16)


def _dup_rows(x_t):
    xb = x_t.astype(BF16)
    return jnp.concatenate([xb, xb], axis=0)


def _sink_finish(mrun, sk, rows):
    mb = jnp.maximum(jnp.broadcast_to(jnp.max(mrun, axis=-1, keepdims=True), (rows, LANES)), sk)
    return mb, jnp.exp(sk - mb)


def _swa_prompt_kernel(sink_ref, q_ref, kt_ref, vt_ref, o_ref, *s_bufs):
    seq = kt_ref.shape[2]
    rows = SWA_GROUP * seq
    ones = _ones_row(seq)
    items = [(r, j) for r in range(kt_ref.shape[0]) for j in range(SWA_KV_HEADS)]

    def scores(item, s_ref):
        r, j = item
        kd = _dup_rows(kt_ref[r, j * SWA_HD:(j + 1) * SWA_HD, :])
        s = jnp.dot(_stack_group(q_ref, slice(r * seq, (r + 1) * seq), j), kd, preferred_element_type=F32)
        return _store_scores(s_ref, 0, s, None)

    def finish(item, s_ref, mrun):
        r, j = item
        sk = _sink_column(sink_ref, j * SWA_GROUP, seq)
        mb, sink_term = _sink_finish(mrun, sk, rows)
        vx = jnp.concatenate([_dup_rows(vt_ref[r, j * SWA_HD:(j + 1) * SWA_HD, :]), ones], axis=0)
        acc = lax.dot_general(_exp_block(s_ref, 0, seq, mb), vx, NT_DIMS, preferred_element_type=F32)
        o = acc[:, :LANES] / (acc[:, LANES:] + sink_term)
        _write_group(o_ref, slice(r * seq, (r + 1) * seq), j, o, seq)

    _run_pipelined(items, scores, finish, s_bufs)


def _swa_prompt_attention(q, kt, vt, sink, *, seq):
    n, d = q.shape
    nkv = kt.shape[1]
    req = 2
    t_spec = pl.BlockSpec((req, nkv, seq), lambda b: (b, 0, 0))
    return pl.pallas_call(
        _swa_prompt_kernel,
        out_shape=jax.ShapeDtypeStruct((n, d), BF16),
        grid=(n // (req * seq),),
        in_specs=[pl.BlockSpec(memory_space=pltpu.SMEM),
                  pl.BlockSpec((req * seq, d), lambda b: (b, 0)), t_spec, t_spec],
        out_specs=pl.BlockSpec((req * seq, d), lambda b: (b, 0)),
        scratch_shapes=[pltpu.VMEM((SWA_GROUP * seq, seq), F32)] * (PROMPT_SCORE_BUFFERS // 2),
        compiler_params=_params(1),
        name="swa_attn_prompt",
    )(sink, q, kt, vt)


def _swa_latent_kernel(sink_ref, q_ref, kc_ref, vc_ref, kl_ref, vl_ref, o_ref, kcd_ref, vcx_ref,
                       s0_ref, s1_ref, *, tq, span):
    pair = pl.program_id(1)
    tiles = q_ref.shape[0] // tq
    first_tile = pl.program_id(2) * tiles
    seq = kl_ref.shape[0]
    lc = kc_ref.shape[1]
    rows = SWA_GROUP * tq
    ones_row = _ones_row(lc)
    for jj in range(2):
        kcd_ref[jj] = _dup_rows(kc_ref[jj * SWA_HD:(jj + 1) * SWA_HD, :])
        vcx_ref[jj] = jnp.concatenate([_dup_rows(vc_ref[jj * SWA_HD:(jj + 1) * SWA_HD, :]), ones_row], axis=0)
    ones_col = _ones_column(span)
    items = [(t, jj) for t in range(tiles) for jj in range(2)]
    windows, biases = {}, {}

    def window(t):
        if t not in windows:
            q0 = (first_tile + t) * tq
            windows[t] = (q0, pl.multiple_of(jnp.clip(q0 - WINDOW, 0, seq - span), WINDOW))
        return windows[t]

    def bias_for(t):
        if t not in biases:
            q0, ws = window(t)
            q_pos = q0 + lax.broadcasted_iota(jnp.int32, (tq, span), 0)
            k_pos = ws + lax.broadcasted_iota(jnp.int32, (tq, span), 1)
            b = jnp.where(jnp.abs(q_pos - k_pos) <= WINDOW, 0.0, NEG_INF).astype(F32)
            biases[t] = jnp.concatenate([b] * SWA_GROUP, axis=0)
        return biases[t]

    def scores(item, s_ref):
        t, jj = item
        _, ws = window(t)
        qs = _stack_group(q_ref, slice(t * tq, (t + 1) * tq), jj)
        s_c = jnp.dot(qs, kcd_ref[jj], preferred_element_type=F32)
        mrun = _store_scores(s_ref, 0, s_c, None)
        s_w = lax.dot_general(qs, kl_ref[pl.ds(ws, span), jj * LANES:(jj + 1) * LANES], NT_DIMS,
                              preferred_element_type=F32) + bias_for(t)
        return _store_scores(s_ref, lc, s_w, mrun)

    def finish(item, s_ref, mrun):
        t, jj = item
        _, ws = window(t)
        sk = _sink_column(sink_ref, (2 * pair + jj) * SWA_GROUP, tq)
        mb, sink_term = _sink_finish(mrun, sk, rows)
        vwx = jnp.concatenate([vl_ref[pl.ds(ws, span), jj * LANES:(jj + 1) * LANES], ones_col], axis=1)
        acc = (lax.dot_general(_exp_block(s_ref, 0, lc, mb), vcx_ref[jj], NT_DIMS, preferred_element_type=F32)
               + jnp.dot(_exp_block(s_ref, lc, span, mb), vwx, preferred_element_type=F32))
        o = acc[:, :LANES] / (acc[:, LANES:] + sink_term)
        _write_group(o_ref, slice(t * tq, (t + 1) * tq), jj, o, tq)

    _run_pipelined(items, scores, finish, (s0_ref, s1_ref))


def _swa_latent_attention(q, kd, vd, cache_kt, cache_vt, sink, *, j, seq):
    n, d = q.shape
    nb, _, lc = cache_kt.shape
    tq = 256
    span = tq + 2 * WINDOW
    npair = SWA_KV_HEADS // 2
    wq = d // npair
    parts = 2
    q_spec = pl.BlockSpec((seq // parts, wq), lambda b, p, i: (b * parts + i, p))
    c_spec = pl.BlockSpec((None, 2 * SWA_HD, lc), lambda b, p, i: (b, j * npair + p, 0))
    l_spec = pl.BlockSpec((seq, 2 * LANES), lambda b, p, i: (b, p))
    s_shape = pltpu.VMEM((SWA_GROUP * tq, lc + span), F32)
    return pl.pallas_call(
        functools.partial(_swa_latent_kernel, tq=tq, span=span),
        out_shape=jax.ShapeDtypeStruct((n, d), BF16),
        grid=(nb, npair, parts),
        in_specs=[pl.BlockSpec(memory_space=pltpu.SMEM), q_spec, c_spec, c_spec, l_spec, l_spec],
        out_specs=q_spec,
        scratch_shapes=[pltpu.VMEM((2, 2 * SWA_HD, lc), BF16), pltpu.VMEM((2, 2 * LANES, lc), BF16),
                        s_shape, s_shape],
        compiler_params=_params(3),
        name="swa_attn_latent",
    )(sink, q, cache_kt, cache_vt, kd, vd)


def _post_attn_ffn_kernel(*refs, n_prompt_tiles, tiles_per_request, split_x, split_out):
    refs = list(refs)
    op_ref, os_ref = refs[:2]
    x_refs = refs[2:4] if split_x else refs[2:3]
    wo_ref, wg_ref, wu_ref, wd_ref, g_ref, mod_ref = refs[2 + len(x_refs):8 + len(x_refs)]
    out_refs = refs[8 + len(x_refs):]
    d = wo_ref.shape[1]
    dff = wg_ref.shape[1]
    i = pl.program_id(0)
    is_prompt = i < n_prompt_tiles
    r = jnp.where(is_prompt, 0, 1 + (i - n_prompt_tiles) // tiles_per_request)

    def mod(slot):
        return mod_ref[pl.ds(r, 1), slot * d:(slot + 1) * d]

    o = jnp.where(is_prompt, op_ref[...], os_ref[...])
    x = jnp.where(is_prompt, x_refs[0][...], x_refs[1][...]) if split_x else x_refs[0][...]
    y = jnp.dot(o, wo_ref[...], preferred_element_type=F32)
    x = x + mod(2) * _rms(y, g_ref[1])
    h = (_rms(x, g_ref[2]) * (1 + mod(4)) + mod(3)).astype(BF16)
    cw = 256
    y = jnp.zeros((h.shape[0], d), F32)
    for c in range(dff // cw):
        a = jnp.dot(h, wg_ref[:, c * cw:(c + 1) * cw], preferred_element_type=F32)
        u = jnp.dot(h, wu_ref[:, c * cw:(c + 1) * cw], preferred_element_type=F32)
        t = (a * jax.nn.sigmoid(a)) * u
        y = y + jnp.dot(t.astype(BF16), wd_ref[c * cw:(c + 1) * cw, :], preferred_element_type=F32)
    out = x + mod(5) * _rms(y, g_ref[3])
    if split_out:
        @pl.when(is_prompt)
        def _():
            out_refs[0][...] = out

        @pl.when(jnp.logical_not(is_prompt))
        def _():
            out_refs[1][...] = out
    else:
        out_refs[0][...] = out


def _post_attn_ffn(o_p, o_s, xs_in, w_o, wg, wu, wd, g, mods, *, layer, j, tiles_per_request, split_out):
    n_p, d = o_p.shape
    n_s = o_s.shape[0]
    tm = ROW_TILE
    tp, ts = n_p // tm, n_s // tm
    prompt_rows = pl.BlockSpec((tm, d), lambda i: (jnp.minimum(i, tp - 1), 0))
    latent_rows = pl.BlockSpec((tm, d), lambda i: (jnp.maximum(i - tp, 0), 0))
    all_rows = pl.BlockSpec((tm, d), lambda i: (i, 0))
    split_x = len(xs_in) == 2
    in_specs = [prompt_rows, latent_rows] + ([prompt_rows, latent_rows] if split_x else [all_rows])
    in_specs += [_layer_resident(w_o.shape, j), _layer_resident(wg.shape, layer),
                 _layer_resident(wu.shape, layer), _layer_resident(wd.shape, layer),
                 _layer_resident(g.shape, layer), _layer_resident(mods.shape, layer)]
    if split_out:
        out_shape = (jax.ShapeDtypeStruct((n_p, d), F32), jax.ShapeDtypeStruct((n_s, d), F32))
        out_specs = (prompt_rows, latent_rows)
    else:
        out_shape = jax.ShapeDtypeStruct((n_p + n_s, d), F32)
        out_specs = all_rows
    return pl.pallas_call(
        functools.partial(_post_attn_ffn_kernel, n_prompt_tiles=tp, tiles_per_request=tiles_per_request,
                          split_x=split_x, split_out=split_out),
        out_shape=out_shape,
        grid=(tp + ts,),
        in_specs=in_specs,
        out_specs=out_specs,
        compiler_params=_params(1),
        name="post_attn_ffn",
    )(o_p, o_s, *xs_in, w_o, wg, wu, wd, g, mods)


def _rope_tables(n_lat):
    t = jnp.arange(n_lat)
    row = (t // GRID_W).astype(F32)
    col = (t % GRID_W).astype(F32)
    nf = ROT_DIM // 4
    inv = ROPE_BASE ** (-jnp.arange(nf, dtype=F32) / nf)
    ar = row[:, None] * inv[None, :]
    ac = col[:, None] * inv[None, :]
    ang = jnp.concatenate([ar, ar, ac, ac], axis=-1)
    cos, sin = jnp.cos(ang), jnp.sin(ang)
    sign = jnp.where((jnp.arange(ROT_DIM) % 32) < 16, -1.0, 1.0).astype(F32)
    reps = LANES // ROT_DIM
    return jnp.tile(cos, (1, reps)), jnp.tile(sin * sign, (1, reps))


def _swa_cache_to_feature_major(cache):
    nb, nl, lc, nh, hd = cache.shape
    return cache.transpose(0, 1, 3, 4, 2).reshape(nb, nl * nh * hd, lc)


def _swa_cache_from_feature_major(xt, seq):
    nb = xt.shape[0]
    return xt.reshape(nb, SWA_KV_HEADS, SWA_HD, seq).transpose(0, 3, 1, 2)


def kernel(x_prompt, x_sample, cache_diff_k, cache_diff_v, cache_swa_k, cache_swa_v, c, c_ctx,
           w_mod, b_mod, norm_g, w_qkv_diff, diff_lambda, diff_subln_g, w_o_diff,
           w_qkv_swa, swa_sink, w_o_swa, w_gate, w_up, w_down):
    bp, lp, d = x_prompt.shape
    bs, ls, _ = x_sample.shape
    lc = cache_diff_k.shape[2]
    depth = w_mod.shape[0]
    tm = ROW_TILE

    cond8 = jnp.concatenate([c_ctx[None, :], c, jnp.zeros((8 - 1 - bs, d), F32)], axis=0)
    mods = _modulation(cond8, w_mod, b_mod)
    tables = _rope_tables(ls)

    w_qkv_diff, w_o_diff = w_qkv_diff.astype(BF16), w_o_diff.astype(BF16)
    w_qkv_swa, w_o_swa = w_qkv_swa.astype(BF16), w_o_swa.astype(BF16)
    w_gate, w_up, w_down = w_gate.astype(BF16), w_up.astype(BF16), w_down.astype(BF16)
    cdk = cache_diff_k.reshape(bs, -1, 2 * DIFF_HD)
    cdv = cache_diff_v.reshape(bs, -1, 2 * DIFF_HD)
    cskt = _swa_cache_to_feature_major(cache_swa_k)
    csvt = _swa_cache_to_feature_major(cache_swa_v)

    n_p, n_s = bp * lp, bs * ls
    x_parts = (x_prompt.reshape(n_p, d), x_sample.reshape(n_s, d))
    g = norm_g.reshape(depth, 4, 1, d)
    sub_g = diff_subln_g.reshape(-1, 1, 2 * DIFF_HD)
    p_rows = dict(row0=0, tiles_per_batch=n_p // tm)
    s_rows = dict(row0=1, tiles_per_batch=ls // tm)
    diff_k_out, diff_v_out, swa_k_out, swa_v_out = [], [], [], []

    for i in range(depth):
        j = i // N_MIXERS
        if len(x_parts) == 2:
            p_src, s_src = (x_parts[0], n_p, 0), (x_parts[1], n_s, 0)
        else:
            p_src, s_src = (x_parts[0], n_p, 0), (x_parts[0], n_s, n_p // tm)
        if i % N_MIXERS == 0:
            lam_init = 0.8 - 0.6 * math.exp(-0.3 * i)
            qp, kp, vp = _qkv_diff(*p_src, g, mods, w_qkv_diff, None, layer=i, j=j, latent=False, **p_rows)
            op = _diff_prompt_attention(qp, kp, vp, diff_lambda, sub_g, j=j, seq=lp, lam_init=lam_init)
            diff_k_out.append(kp.reshape(bp, lp, DIFF_HEADS, 2 * DIFF_HD))
            diff_v_out.append(vp.reshape(bp, lp, DIFF_HEADS, 2 * DIFF_HD))
            qs, ks, vs = _qkv_diff(*s_src, g, mods, w_qkv_diff, tables, layer=i, j=j, latent=True, **s_rows)
            os_ = _diff_latent_attention(qs, ks, vs, cdk, cdv, diff_lambda, sub_g,
                                         j=j, seq=ls, lc=lc, lam_init=lam_init)
            w_o = w_o_diff
        else:
            qp, ktp, vtp = _qkv_swa(*p_src, g, mods, w_qkv_swa, None, layer=i, j=j, latent=False,
                                    seq=lp, **p_rows)
            op = _swa_prompt_attention(qp, ktp, vtp, swa_sink[j], seq=lp)
            swa_k_out.append(_swa_cache_from_feature_major(ktp, lp))
            swa_v_out.append(_swa_cache_from_feature_major(vtp, lp))
            qs, kds, vds = _qkv_swa(*s_src, g, mods, w_qkv_swa, tables, layer=i, j=j, latent=True,
                                    seq=ls, **s_rows)
            os_ = _swa_latent_attention(qs, kds, vds, cskt, csvt, swa_sink[j], j=j, seq=ls)
            w_o = w_o_swa
        last = i == depth - 1
        out = _post_attn_ffn(op, os_, x_parts, w_o, w_gate, w_up, w_down, g, mods, layer=i, j=j,
                             tiles_per_request=ls // tm, split_out=last)
        x_parts = out if last else (out,)
    xp, xs = x_parts

    return (xp.reshape(bp, lp, d), xs.reshape(bs, ls, d),
            jnp.stack(diff_k_out, axis=1), jnp.stack(diff_v_out, axis=1),
            jnp.stack(swa_k_out, axis=1), jnp.stack(swa_v_out, axis=1))
```

```python
import functools
import math

import jax
import jax.numpy as jnp
from jax import lax
from jax.experimental import pallas as pl
from jax.experimental.pallas import tpu as pltpu

F32 = jnp.float32
BF16 = jnp.bfloat16

GRID_W = 64
N_MIXERS = 2
DIFF_HEADS = 8
DIFF_HD = 64
SWA_HEADS = 16
SWA_KV_HEADS = 4
SWA_GROUP = SWA_HEADS // SWA_KV_HEADS
SWA_HD = 64
ROT_DIM = 64
WINDOW = 128
ROPE_BASE = 10000.0
EPS = 1e-6
NEG_INF = -1e30

LANES = 128
SUBLANES = 8
ROW_TILE = 512
PROMPT_SCORE_BUFFERS = 8
VMEM_LIMIT = 48 * 1024 * 1024
FFN_VMEM_LIMIT = 58 * 1024 * 1024
WEIGHT_STAGE_ROWS = 128
NT_DIMS = (((1,), (1,)), ((), ()))


def _params(n_axes):
    return pltpu.CompilerParams(dimension_semantics=("arbitrary",) * n_axes,
                                vmem_limit_bytes=VMEM_LIMIT)


def _resident(shape):
    return pl.BlockSpec(shape, lambda *_: (0,) * len(shape), pipeline_mode=pl.Buffered(1))


def _layer_resident(shape, layer):
    return pl.BlockSpec((None,) + tuple(shape[1:]), lambda *_: (layer,) + (0,) * (len(shape) - 1),
                        pipeline_mode=pl.Buffered(1))


def _rms(x, g):
    ms = jnp.mean(x * x, axis=-1, keepdims=True)
    return (x * lax.rsqrt(ms + EPS)) * g


def _half_masks(dtype):
    lane = lax.broadcasted_iota(jnp.int32, (1, LANES), 1)
    lo = lane < (LANES // 2)
    return jnp.where(lo, 1.0, 0.0).astype(dtype), jnp.where(lo, 0.0, 1.0).astype(dtype)


def _lo64():
    return lax.broadcasted_iota(jnp.int32, (1, LANES), 1) < (LANES // 2)


def _ones_column(rows):
    return jnp.ones((rows, LANES), BF16)


def _ones_row(cols):
    return jnp.ones((LANES, cols), BF16)


def _mod_kernel(cond_ref, w_ref, b_ref, out_ref):
    c = cond_ref[...]
    s = c * jax.nn.sigmoid(c)
    out_ref[...] = jnp.dot(s.astype(BF16), w_ref[...].astype(BF16),
                           preferred_element_type=F32) + b_ref[...]


def _modulation(cond8, w_mod, b_mod):
    depth, d, n = w_mod.shape
    tn = 1536
    return pl.pallas_call(
        _mod_kernel,
        out_shape=jax.ShapeDtypeStruct((depth, 8, n), F32),
        grid=(depth, n // tn),
        in_specs=[pl.BlockSpec((8, d), lambda i, j: (0, 0)),
                  pl.BlockSpec((None, d, tn), lambda i, j: (i, 0, j)),
                  pl.BlockSpec((None, 1, tn), lambda i, j: (i, 0, j))],
        out_specs=pl.BlockSpec((None, 8, tn), lambda i, j: (i, 0, j)),
        compiler_params=_params(2),
        name="modulation",
    )(cond8, w_mod, b_mod.reshape(depth, 1, n))


def _mod_row(row0, tiles_per_batch):
    return row0 + pl.program_id(0) // tiles_per_batch


def _rope_slab(xs, cos, sin_signed, lo16):
    left = pltpu.roll(xs, LANES - 16, 1)
    right = pltpu.roll(xs, 16, 1)
    return xs * cos + jnp.where(lo16, left, right) * sin_signed


def _lo16_mask():
    lane = lax.broadcasted_iota(jnp.int32, (1, LANES), 1)
    return (lane % 32) < 16


def _pre_norm(x_ref, g_ref, mod_ref, r, d, slot):
    shift = mod_ref[pl.ds(r, 1), slot * d:(slot + 1) * d]
    scale = mod_ref[pl.ds(r, 1), (slot + 1) * d:(slot + 2) * d]
    return _rms(x_ref[...], g_ref[0]) * (1 + scale) + shift


def _qkv_diff_kernel(x_ref, g_ref, mod_ref, w_ref, *rest, latent, row0, tiles_per_batch):
    d = x_ref.shape[1]
    if latent:
        cos_ref, sin_ref, q_ref, k_ref, v_ref = rest
        cos, sin = cos_ref[...], sin_ref[...]
        lo16 = _lo16_mask()
    else:
        q_ref, k_ref, v_ref = rest
    r = _mod_row(row0, tiles_per_batch)
    h = _pre_norm(x_ref, g_ref, mod_ref, r, d, 0).astype(BF16)
    cw = 512
    for c in range(3 * d // cw):
        acc = jnp.dot(h, w_ref[:, c * cw:(c + 1) * cw], preferred_element_type=F32)
        which, off = divmod(c * cw, d)
        dst = (q_ref, k_ref, v_ref)[which]
        for s in range(cw // LANES):
            xs = acc[:, s * LANES:(s + 1) * LANES]
            if latent and which < 2:
                xs = _rope_slab(xs, cos, sin, lo16)
            if which == 0:
                xs = xs * (DIFF_HD ** -0.5)
            lo = off + s * LANES
            dst[:, lo:lo + LANES] = xs.astype(dst.dtype)


def _qkv_diff(x, n, tile0, g, mods, w, tables, *, layer, j, latent, row0, tiles_per_batch):
    d = x.shape[1]
    tm = ROW_TILE
    in_specs = [pl.BlockSpec((tm, d), lambda i: (i + tile0, 0)), _layer_resident(g.shape, layer),
                _layer_resident(mods.shape, layer), _layer_resident(w.shape, j)]
    args = [x, g, mods, w]
    kv_dtype = F32
    if latent:
        nt = tables[0].shape[0] // tm
        in_specs += [pl.BlockSpec((tm, LANES), lambda i: (i % nt, 0))] * 2
        args += list(tables)
        kv_dtype = BF16
    out_spec = pl.BlockSpec((tm, d), lambda i: (i, 0))
    return pl.pallas_call(
        functools.partial(_qkv_diff_kernel, latent=latent, row0=row0, tiles_per_batch=tiles_per_batch),
        out_shape=(jax.ShapeDtypeStruct((n, d), BF16),
                   jax.ShapeDtypeStruct((n, d), kv_dtype),
                   jax.ShapeDtypeStruct((n, d), kv_dtype)),
        grid=(n // tm,),
        in_specs=in_specs,
        out_specs=(out_spec, out_spec, out_spec),
        compiler_params=_params(1),
        name="qkv_diff_latent" if latent else "qkv_diff_prompt",
    )(*args)


def _qkv_swa_kernel(x_ref, g_ref, mod_ref, w_ref, *rest, latent, row0, tiles_per_batch, seq):
    d = x_ref.shape[1]
    nkv = SWA_KV_HEADS * SWA_HD
    if latent:
        cos_ref, sin_ref, q_ref, kd_ref, vd_ref = rest
        cos, sin = cos_ref[...], sin_ref[...]
        lo16 = _lo16_mask()
    else:
        q_ref, kt_ref, vt_ref = rest
    lo64 = _lo64()
    r = _mod_row(row0, tiles_per_batch)
    h = _pre_norm(x_ref, g_ref, mod_ref, r, d, 0).astype(BF16)
    cw = 512
    for c in range(d // cw):
        acc = jnp.dot(h, w_ref[:, c * cw:(c + 1) * cw], preferred_element_type=F32)
        for s in range(cw // LANES):
            xs = acc[:, s * LANES:(s + 1) * LANES]
            if latent:
                xs = _rope_slab(xs, cos, sin, lo16)
            lo = c * cw + s * LANES
            q_ref[:, lo:lo + LANES] = (xs * (SWA_HD ** -0.5)).astype(BF16)
    kv = jnp.dot(h, w_ref[:, d:d + 2 * nkv], preferred_element_type=F32)
    if not latent:
        for b in range(x_ref.shape[0] // seq):
            kt_ref[b] = kv[b * seq:(b + 1) * seq, :nkv].T
            vt_ref[b] = kv[b * seq:(b + 1) * seq, nkv:].T
        return
    for which, dst in enumerate((kd_ref, vd_ref)):
        for s in range(nkv // LANES):
            xs = kv[:, which * nkv + s * LANES: which * nkv + (s + 1) * LANES]
            if which == 0:
                xs = _rope_slab(xs, cos, sin, lo16)
            sw = pltpu.roll(xs, LANES // 2, 1)
            dst[:, (2 * s) * LANES:(2 * s + 1) * LANES] = jnp.where(lo64, xs, sw).astype(BF16)
            dst[:, (2 * s + 1) * LANES:(2 * s + 2) * LANES] = jnp.where(lo64, sw, xs).astype(BF16)


def _qkv_swa(x, n, tile0, g, mods, w, tables, *, layer, j, latent, row0, tiles_per_batch, seq):
    d = x.shape[1]
    tm = ROW_TILE
    nkv = SWA_KV_HEADS * SWA_HD
    in_specs = [pl.BlockSpec((tm, d), lambda i: (i + tile0, 0)), _layer_resident(g.shape, layer),
                _layer_resident(mods.shape, layer), _layer_resident(w.shape, j)]
    args = [x, g, mods, w]
    out_shape = [jax.ShapeDtypeStruct((n, d), BF16)]
    out_specs = [pl.BlockSpec((tm, d), lambda i: (i, 0))]
    if latent:
        nt = tables[0].shape[0] // tm
        in_specs += [pl.BlockSpec((tm, LANES), lambda i: (i % nt, 0))] * 2
        args += list(tables)
        out_shape += [jax.ShapeDtypeStruct((n, 2 * nkv), BF16)] * 2
        out_specs += [pl.BlockSpec((tm, 2 * nkv), lambda i: (i, 0))] * 2
    else:
        out_shape += [jax.ShapeDtypeStruct((n // seq, nkv, seq), F32)] * 2
        out_specs += [pl.BlockSpec((tm // seq, nkv, seq), lambda i: (i, 0, 0))] * 2
    return pl.pallas_call(
        functools.partial(_qkv_swa_kernel, latent=latent, row0=row0, tiles_per_batch=tiles_per_batch,
                          seq=seq),
        out_shape=tuple(out_shape),
        grid=(n // tm,),
        in_specs=in_specs,
        out_specs=tuple(out_specs),
        compiler_params=_params(1),
        name="qkv_swa_latent" if latent else "qkv_swa_prompt",
    )(*args)


def _diff_lambda(lam_ref, lam_init):
    lp = lam_ref[...]
    a = jnp.sum(lp[0:1] * lp[1:2], axis=-1, keepdims=True)
    b = jnp.sum(lp[2:3] * lp[3:4], axis=-1, keepdims=True)
    return jnp.exp(a) - jnp.exp(b) + lam_init


def _diff_combine(acc, tq, lam, g, lam_init):
    o12 = acc[:, :LANES] / acc[:, LANES:]
    o = o12[:tq] - lam * o12[tq:]
    return _rms(o, g) * (1.0 - lam_init)


def _stack_maps(q):
    m_lo, m_hi = _half_masks(BF16)
    return jnp.concatenate([q * m_lo, q * m_hi], axis=0)


def _run_pipelined(items, scores, finish, s_bufs):
    depth = len(s_bufs)
    states = {i: scores(items[i], s_bufs[i]) for i in range(min(depth - 1, len(items)))}
    for i, item in enumerate(items):
        ahead = i + depth - 1
        if ahead < len(items):
            states[ahead] = scores(items[ahead], s_bufs[ahead % depth])
        finish(item, s_bufs[i % depth], states.pop(i))


def _store_scores(s_ref, col0, s, mrun):
    s_ref[:, col0:col0 + s.shape[1]] = s
    for t in range(s.shape[1] // LANES):
        blk = s[:, t * LANES:(t + 1) * LANES]
        mrun = blk if mrun is None else jnp.maximum(mrun, blk)
    return mrun


def _exp_block(s_ref, col0, width, mb):
    return jnp.concatenate(
        [jnp.exp(s_ref[:, col0 + t * LANES:col0 + (t + 1) * LANES] - mb).astype(BF16)
         for t in range(width // LANES)], axis=1)


def _diff_prompt_kernel(q_ref, k_ref, v_ref, lam_ref, g_ref, o_ref, *s_bufs, lam_init, seq):
    lam = _diff_lambda(lam_ref, lam_init)
    g = g_ref[...]
    ones = _ones_column(seq)
    items = [(r, h) for r in range(q_ref.shape[0] // seq) for h in range(DIFF_HEADS)]

    def scores(item, s_ref):
        r, h = item
        rows, sl = slice(r * seq, (r + 1) * seq), slice(h * LANES, (h + 1) * LANES)
        s = lax.dot_general(_stack_maps(q_ref[rows, sl]), k_ref[rows, sl].astype(BF16), NT_DIMS,
                            preferred_element_type=F32)
        return _store_scores(s_ref, 0, s, None)

    def finish(item, s_ref, mrun):
        r, h = item
        rows, sl = slice(r * seq, (r + 1) * seq), slice(h * LANES, (h + 1) * LANES)
        mb = jnp.broadcast_to(jnp.max(mrun, axis=-1, keepdims=True), (2 * seq, LANES))
        vx = jnp.concatenate([v_ref[rows, sl].astype(BF16), ones], axis=1)
        acc = jnp.dot(_exp_block(s_ref, 0, seq, mb), vx, preferred_element_type=F32)
        o_ref[rows, sl] = _diff_combine(acc, seq, lam, g, lam_init).astype(BF16)

    _run_pipelined(items, scores, finish, s_bufs)


def _diff_prompt_attention(q, k, v, lam_params, subln_g, *, j, seq, lam_init):
    n, d = q.shape
    req = 2
    spec = pl.BlockSpec((req * seq, d), lambda b: (b, 0))
    return pl.pallas_call(
        functools.partial(_diff_prompt_kernel, lam_init=lam_init, seq=seq),
        out_shape=jax.ShapeDtypeStruct((n, d), BF16),
        grid=(n // (req * seq),),
        in_specs=[spec, spec, spec, _layer_resident(lam_params.shape, j), _layer_resident(subln_g.shape, j)],
        out_specs=spec,
        scratch_shapes=[pltpu.VMEM((2 * seq, seq), F32)] * PROMPT_SCORE_BUFFERS,
        compiler_params=_params(1),
        name="diff_attn_prompt",
    )(q, k, v, lam_params, subln_g)


def _diff_latent_kernel(q_ref, kc_ref, vc_ref, kl_ref, vl_ref, lam_ref, g_ref, o_ref,
                        kk_ref, vx_ref, s0_ref, s1_ref, *, lam_init, lc, tq, key_chunk):
    seq = q_ref.shape[0]
    nk = kk_ref.shape[0]
    h = pl.program_id(1)
    kk_ref[0:lc, :] = kc_ref[pl.ds(h, lc, stride=DIFF_HEADS), :].astype(BF16)
    kk_ref[lc:, :] = kl_ref[...]
    vx_ref[0:lc, 0:LANES] = vc_ref[pl.ds(h, lc, stride=DIFF_HEADS), :].astype(BF16)
    vx_ref[lc:, 0:LANES] = vl_ref[...]
    vx_ref[:, LANES:2 * LANES] = _ones_column(nk)

    masks = _half_masks(BF16)
    nchunk = nk // key_chunk
    items = [(rt, m) for rt in range(seq // tq) for m in range(2)]
    lam = _diff_lambda(lam_ref, lam_init)
    g = g_ref[...]
    first_map = {}

    def scores(item, s_ref):
        rt, m = item
        q = q_ref[rt * tq:(rt + 1) * tq, :] * masks[m]
        mrun = None
        for c in range(nchunk):
            s = lax.dot_general(q, kk_ref[c * key_chunk:(c + 1) * key_chunk, :], NT_DIMS,
                                preferred_element_type=F32)
            mrun = _store_scores(s_ref, c * key_chunk, s, mrun)
        return mrun

    def finish(item, s_ref, mrun):
        rt, m = item
        mb = jnp.broadcast_to(jnp.max(mrun, axis=-1, keepdims=True), (tq, LANES))
        acc = None
        for c in range(nchunk):
            part = jnp.dot(_exp_block(s_ref, c * key_chunk, key_chunk, mb),
                           vx_ref[c * key_chunk:(c + 1) * key_chunk, :], preferred_element_type=F32)
            acc = part if acc is None else acc + part
        o_m = acc[:, :LANES] / acc[:, LANES:]
        if m == 0:
            first_map[rt] = o_m
        else:
            o = _rms(first_map.pop(rt) - lam * o_m, g) * (1.0 - lam_init)
            o_ref[rt * tq:(rt + 1) * tq, :] = o.astype(BF16)

    _run_pipelined(items, scores, finish, (s0_ref, s1_ref))


def _diff_latent_attention(q, k, v, cache_k, cache_v, lam_params, subln_g, *, j, seq, lc, lam_init):
    n, d = q.shape
    nb = cache_k.shape[0]
    tq = 512
    key_chunk = 512
    q_spec = pl.BlockSpec((seq, LANES), lambda b, h: (b, h))
    c_spec = pl.BlockSpec((None, lc * DIFF_HEADS, LANES), lambda b, h: (b, j, 0))
    return pl.pallas_call(
        functools.partial(_diff_latent_kernel, lam_init=lam_init, lc=lc, tq=tq, key_chunk=key_chunk),
        out_shape=jax.ShapeDtypeStruct((n, d), BF16),
        grid=(nb, DIFF_HEADS),
        in_specs=[q_spec, c_spec, c_spec, q_spec, q_spec,
                  _layer_resident(lam_params.shape, j), _layer_resident(subln_g.shape, j)],
        out_specs=q_spec,
        scratch_shapes=[pltpu.VMEM((lc + seq, LANES), BF16), pltpu.VMEM((lc + seq, 2 * LANES), BF16),
                        pltpu.VMEM((tq, lc + seq), F32), pltpu.VMEM((tq, lc + seq), F32)],
        compiler_params=_params(2),
        name="diff_attn_latent",
    )(q, cache_k, cache_v, k, v, lam_params, subln_g)


def _stack_group(q_ref, rows, kv_local):
    m_lo, m_hi = _half_masks(BF16)
    parts = []
    for gb in range(SWA_GROUP // 2):
        blk = kv_local * (SWA_GROUP // 2) + gb
        qb = q_ref[rows, blk * LANES:(blk + 1) * LANES]
        parts += [qb * m_lo, qb * m_hi]
    return jnp.concatenate(parts, axis=0)


def _sink_column(sink_ref, first_head, tq):
    return jnp.concatenate([jnp.full((tq, LANES), sink_ref[first_head + g], F32) for g in range(SWA_GROUP)],
                           axis=0)


def _write_group(o_ref, rows, kv_local, o, tq):
    lo64 = _lo64()
    for gb in range(SWA_GROUP // 2):
        blk = kv_local * (SWA_GROUP // 2) + gb
        even = o[(2 * gb) * tq:(2 * gb + 1) * tq]
        odd = o[(2 * gb + 1) * tq:(2 * gb + 2) * tq]
        o_ref[rows, blk * LANES:(blk + 1) * LANES] = jnp.where(lo64, even, odd).astype(BF16)


def _dup_rows(x_t):
    xb = x_t.astype(BF16)
    return jnp.concatenate([xb, xb], axis=0)


def _sink_finish(mrun, sk, rows):
    mb = jnp.maximum(jnp.broadcast_to(jnp.max(mrun, axis=-1, keepdims=True), (rows, LANES)), sk)
    return mb, jnp.exp(sk - mb)


def _swa_prompt_kernel(sink_ref, q_ref, kt_ref, vt_ref, o_ref, *s_bufs):
    seq = kt_ref.shape[2]
    rows = SWA_GROUP * seq
    ones = _ones_row(seq)
    items = [(r, j) for r in range(kt_ref.shape[0]) for j in range(SWA_KV_HEADS)]

    def scores(item, s_ref):
        r, j = item
        kd = _dup_rows(kt_ref[r, j * SWA_HD:(j + 1) * SWA_HD, :])
        s = jnp.dot(_stack_group(q_ref, slice(r * seq, (r + 1) * seq), j), kd, preferred_element_type=F32)
        return _store_scores(s_ref, 0, s, None)

    def finish(item, s_ref, mrun):
        r, j = item
        sk = _sink_column(sink_ref, j * SWA_GROUP, seq)
        mb, sink_term = _sink_finish(mrun, sk, rows)
        vx = jnp.concatenate([_dup_rows(vt_ref[r, j * SWA_HD:(j + 1) * SWA_HD, :]), ones], axis=0)
        acc = lax.dot_general(_exp_block(s_ref, 0, seq, mb), vx, NT_DIMS, preferred_element_type=F32)
        o = acc[:, :LANES] / (acc[:, LANES:] + sink_term)
        _write_group(o_ref, slice(r * seq, (r + 1) * seq), j, o, seq)

    _run_pipelined(items, scores, finish, s_bufs)


def _swa_prompt_attention(q, kt, vt, sink, *, seq):
    n, d = q.shape
    nkv = kt.shape[1]
    req = 2
    t_spec = pl.BlockSpec((req, nkv, seq), lambda b: (b, 0, 0))
    return pl.pallas_call(
        _swa_prompt_kernel,
        out_shape=jax.ShapeDtypeStruct((n, d), BF16),
        grid=(n // (req * seq),),
        in_specs=[pl.BlockSpec(memory_space=pltpu.SMEM),
                  pl.BlockSpec((req * seq, d), lambda b: (b, 0)), t_spec, t_spec],
        out_specs=pl.BlockSpec((req * seq, d), lambda b: (b, 0)),
        scratch_shapes=[pltpu.VMEM((SWA_GROUP * seq, seq), F32)] * (PROMPT_SCORE_BUFFERS // 2),
        compiler_params=_params(1),
        name="swa_attn_prompt",
    )(sink, q, kt, vt)


def _swa_latent_kernel(sink_ref, q_ref, kc_ref, vc_ref, kl_ref, vl_ref, o_ref, kcd_ref, vcx_ref,
                       s0_ref, s1_ref, *, tq, span):
    pair = pl.program_id(1)
    tiles = q_ref.shape[0] // tq
    first_tile = pl.program_id(2) * tiles
    seq = kl_ref.shape[0]
    lc = kc_ref.shape[1]
    rows = SWA_GROUP * tq
    ones_row = _ones_row(lc)
    for jj in range(2):
        kcd_ref[jj] = _dup_rows(kc_ref[jj * SWA_HD:(jj + 1) * SWA_HD, :])
        vcx_ref[jj] = jnp.concatenate([_dup_rows(vc_ref[jj * SWA_HD:(jj + 1) * SWA_HD, :]), ones_row], axis=0)
    ones_col = _ones_column(span)
    items = [(t, jj) for t in range(tiles) for jj in range(2)]
    windows, biases = {}, {}

    def window(t):
        if t not in windows:
            q0 = (first_tile + t) * tq
            windows[t] = (q0, pl.multiple_of(jnp.clip(q0 - WINDOW, 0, seq - span), WINDOW))
        return windows[t]

    def bias_for(t):
        if t not in biases:
            q0, ws = window(t)
            q_pos = q0 + lax.broadcasted_iota(jnp.int32, (tq, span), 0)
            k_pos = ws + lax.broadcasted_iota(jnp.int32, (tq, span), 1)
            b = jnp.where(jnp.abs(q_pos - k_pos) <= WINDOW, 0.0, NEG_INF).astype(F32)
            biases[t] = jnp.concatenate([b] * SWA_GROUP, axis=0)
        return biases[t]

    def scores(item, s_ref):
        t, jj = item
        _, ws = window(t)
        qs = _stack_group(q_ref, slice(t * tq, (t + 1) * tq), jj)
        s_c = jnp.dot(qs, kcd_ref[jj], preferred_element_type=F32)
        mrun = _store_scores(s_ref, 0, s_c, None)
        s_w = lax.dot_general(qs, kl_ref[pl.ds(ws, span), jj * LANES:(jj + 1) * LANES], NT_DIMS,
                              preferred_element_type=F32) + bias_for(t)
        return _store_scores(s_ref, lc, s_w, mrun)

    def finish(item, s_ref, mrun):
        t, jj = item
        _, ws = window(t)
        sk = _sink_column(sink_ref, (2 * pair + jj) * SWA_GROUP, tq)
        mb, sink_term = _sink_finish(mrun, sk, rows)
        vwx = jnp.concatenate([vl_ref[pl.ds(ws, span), jj * LANES:(jj + 1) * LANES], ones_col], axis=1)
        acc = (lax.dot_general(_exp_block(s_ref, 0, lc, mb), vcx_ref[jj], NT_DIMS, preferred_element_type=F32)
               + jnp.dot(_exp_block(s_ref, lc, span, mb), vwx, preferred_element_type=F32))
        o = acc[:, :LANES] / (acc[:, LANES:] + sink_term)
        _write_group(o_ref, slice(t * tq, (t + 1) * tq), jj, o, tq)

    _run_pipelined(items, scores, finish, (s0_ref, s1_ref))


def _swa_latent_attention(q, kd, vd, cache_kt, cache_vt, sink, *, j, seq):
    n, d = q.shape
    nb, _, lc = cache_kt.shape
    tq = 256
    span = tq + 2 * WINDOW
    npair = SWA_KV_HEADS // 2
    wq = d // npair
    parts = 2
    q_spec = pl.BlockSpec((seq // parts, wq), lambda b, p, i: (b * parts + i, p))
    c_spec = pl.BlockSpec((None, 2 * SWA_HD, lc), lambda b, p, i: (b, j * npair + p, 0))
    l_spec = pl.BlockSpec((seq, 2 * LANES), lambda b, p, i: (b, p))
    s_shape = pltpu.VMEM((SWA_GROUP * tq, lc + span), F32)
    return pl.pallas_call(
        functools.partial(_swa_latent_kernel, tq=tq, span=span),
        out_shape=jax.ShapeDtypeStruct((n, d), BF16),
        grid=(nb, npair, parts),
        in_specs=[pl.BlockSpec(memory_space=pltpu.SMEM), q_spec, c_spec, c_spec, l_spec, l_spec],
        out_specs=q_spec,
        scratch_shapes=[pltpu.VMEM((2, 2 * SWA_HD, lc), BF16), pltpu.VMEM((2, 2 * LANES, lc), BF16),
                        s_shape, s_shape],
        compiler_params=_params(3),
        name="swa_attn_latent",
    )(sink, q, cache_kt, cache_vt, kd, vd)


def _load_weights_as_bf16(jobs, stages, sems):
    chunks = []
    for src, dst in jobs:
        stage, sem = stages[src.shape[1]], sems[src.shape[1]]
        for k in range(src.shape[0] // WEIGHT_STAGE_ROWS):
            rows = pl.ds(k * WEIGHT_STAGE_ROWS, WEIGHT_STAGE_ROWS)
            chunks.append((src.at[rows, :], dst, rows, stage, sem))
    slot_of, copies, used = [], [], {}
    for src, dst, rows, stage, sem in chunks:
        slot = used.get(id(stage), 0) % 2
        used[id(stage)] = used.get(id(stage), 0) + 1
        slot_of.append(slot)
        copies.append(pltpu.make_async_copy(src, stage.at[slot], sem.at[slot]))
    started = set()

    def start_next(stage):
        for n, chunk in enumerate(chunks):
            if chunk[3] is stage and n not in started:
                started.add(n)
                copies[n].start()
                return

    for stage in {id(c[3]): c[3] for c in chunks}.values():
        start_next(stage)
        start_next(stage)
    for n, (src, dst, rows, stage, sem) in enumerate(chunks):
        copies[n].wait()
        dst[rows, :] = stage[slot_of[n]].astype(BF16)
        start_next(stage)


def _post_attn_ffn_kernel(*refs, layer, j, n_prompt_tiles, tiles_per_request, split_x, split_out):
    refs = list(refs)
    op_ref, os_ref = refs[:2]
    x_refs = refs[2:4] if split_x else refs[2:3]
    wo_hbm, wg_hbm, wu_hbm, wd_hbm, g_ref, mod_ref = refs[2 + len(x_refs):8 + len(x_refs)]
    n_out = 2 if split_out else 1
    out_refs = refs[8 + len(x_refs):8 + len(x_refs) + n_out]
    wo_ref, wg_ref, wu_ref, wd_ref, stage_d, stage_ff, sem_d, sem_ff = refs[8 + len(x_refs) + n_out:]
    d = wo_ref.shape[1]
    dff = wg_ref.shape[1]
    i = pl.program_id(0)

    @pl.when(i == 0)
    def _():
        _load_weights_as_bf16(
            [(wo_hbm.at[j], wo_ref), (wg_hbm.at[layer], wg_ref), (wu_hbm.at[layer], wu_ref),
             (wd_hbm.at[layer], wd_ref)],
            {d: stage_d, dff: stage_ff}, {d: sem_d, dff: sem_ff})

    is_prompt = i < n_prompt_tiles
    r = jnp.where(is_prompt, 0, 1 + (i - n_prompt_tiles) // tiles_per_request)

    def mod(slot):
        return mod_ref[pl.ds(r, 1), slot * d:(slot + 1) * d]

    o = jnp.where(is_prompt, op_ref[...], os_ref[...])
    x = jnp.where(is_prompt, x_refs[0][...], x_refs[1][...]) if split_x else x_refs[0][...]
    y = jnp.dot(o, wo_ref[...], preferred_element_type=F32)
    x = x + mod(2) * _rms(y, g_ref[1])
    h = (_rms(x, g_ref[2]) * (1 + mod(4)) + mod(3)).astype(BF16)
    cw = 256
    y = jnp.zeros((h.shape[0], d), F32)
    for c in range(dff // cw):
        a = jnp.dot(h, wg_ref[:, c * cw:(c + 1) * cw], preferred_element_type=F32)
        u = jnp.dot(h, wu_ref[:, c * cw:(c + 1) * cw], preferred_element_type=F32)
        t = (a * jax.nn.sigmoid(a)) * u
        y = y + jnp.dot(t.astype(BF16), wd_ref[c * cw:(c + 1) * cw, :], preferred_element_type=F32)
    out = x + mod(5) * _rms(y, g_ref[3])
    if split_out:
        @pl.when(is_prompt)
        def _():
            out_refs[0][...] = out

        @pl.when(jnp.logical_not(is_prompt))
        def _():
            out_refs[1][...] = out
    else:
        out_refs[0][...] = out


def _post_attn_ffn(o_p, o_s, xs_in, w_o, wg, wu, wd, g, mods, *, layer, j, tiles_per_request, split_out):
    n_p, d = o_p.shape
    n_s = o_s.shape[0]
    tm = ROW_TILE
    tp, ts = n_p // tm, n_s // tm
    prompt_rows = pl.BlockSpec((tm, d), lambda i: (jnp.minimum(i, tp - 1), 0))
    latent_rows = pl.BlockSpec((tm, d), lambda i: (jnp.maximum(i - tp, 0), 0))
    all_rows = pl.BlockSpec((tm, d), lambda i: (i, 0))
    split_x = len(xs_in) == 2
    in_specs = [prompt_rows, latent_rows] + ([prompt_rows, latent_rows] if split_x else [all_rows])
    hbm = pl.BlockSpec(memory_space=pl.ANY)
    in_specs += [hbm, hbm, hbm, hbm, _layer_resident(g.shape, layer), _layer_resident(mods.shape, layer)]
    if split_out:
        out_shape = (jax.ShapeDtypeStruct((n_p, d), F32), jax.ShapeDtypeStruct((n_s, d), F32))
        out_specs = (prompt_rows, latent_rows)
    else:
        out_shape = jax.ShapeDtypeStruct((n_p + n_s, d), F32)
        out_specs = all_rows
    dff = wg.shape[2]
    scratch = [pltpu.VMEM((d, d), BF16), pltpu.VMEM((d, dff), BF16), pltpu.VMEM((d, dff), BF16),
               pltpu.VMEM((dff, d), BF16),
               pltpu.VMEM((2, WEIGHT_STAGE_ROWS, d), F32), pltpu.VMEM((2, WEIGHT_STAGE_ROWS, dff), F32),
               pltpu.SemaphoreType.DMA((2,)), pltpu.SemaphoreType.DMA((2,))]
    return pl.pallas_call(
        functools.partial(_post_attn_ffn_kernel, layer=layer, j=j, n_prompt_tiles=tp,
                          tiles_per_request=tiles_per_request, split_x=split_x, split_out=split_out),
        out_shape=out_shape,
        grid=(tp + ts,),
        in_specs=in_specs,
        out_specs=out_specs,
        scratch_shapes=scratch,
        compiler_params=pltpu.CompilerParams(dimension_semantics=("arbitrary",),
                                             vmem_limit_bytes=FFN_VMEM_LIMIT),
        name="post_attn_ffn",
    )(o_p, o_s, *xs_in, w_o, wg, wu, wd, g, mods)


def _rope_tables(n_lat):
    t = jnp.arange(n_lat)
    row = (t // GRID_W).astype(F32)
    col = (t % GRID_W).astype(F32)
    nf = ROT_DIM // 4
    inv = ROPE_BASE ** (-jnp.arange(nf, dtype=F32) / nf)
    ar = row[:, None] * inv[None, :]
    ac = col[:, None] * inv[None, :]
    ang = jnp.concatenate([ar, ar, ac, ac], axis=-1)
    cos, sin = jnp.cos(ang), jnp.sin(ang)
    sign = jnp.where((jnp.arange(ROT_DIM) % 32) < 16, -1.0, 1.0).astype(F32)
    reps = LANES // ROT_DIM
    return jnp.tile(cos, (1, reps)), jnp.tile(sin * sign, (1, reps))


def _swa_cache_to_feature_major(cache):
    nb, nl, lc, nh, hd = cache.shape
    return cache.transpose(0, 1, 3, 4, 2).reshape(nb, nl * nh * hd, lc)


def _swa_cache_from_feature_major(xt, seq):
    nb = xt.shape[0]
    return xt.reshape(nb, SWA_KV_HEADS, SWA_HD, seq).transpose(0, 3, 1, 2)


def kernel(x_prompt, x_sample, cache_diff_k, cache_diff_v, cache_swa_k, cache_swa_v, c, c_ctx,
           w_mod, b_mod, norm_g, w_qkv_diff, diff_lambda, diff_subln_g, w_o_diff,
           w_qkv_swa, swa_sink, w_o_swa, w_gate, w_up, w_down):
    bp, lp, d = x_prompt.shape
    bs, ls, _ = x_sample.shape
    lc = cache_diff_k.shape[2]
    depth = w_mod.shape[0]
    tm = ROW_TILE

    cond8 = jnp.concatenate([c_ctx[None, :], c, jnp.zeros((8 - 1 - bs, d), F32)], axis=0)
    mods = _modulation(cond8, w_mod, b_mod)
    tables = _rope_tables(ls)

    w_qkv_diff, w_qkv_swa = w_qkv_diff.astype(BF16), w_qkv_swa.astype(BF16)
    cdk = cache_diff_k.reshape(bs, -1, 2 * DIFF_HD)
    cdv = cache_diff_v.reshape(bs, -1, 2 * DIFF_HD)
    cskt = _swa_cache_to_feature_major(cache_swa_k)
    csvt = _swa_cache_to_feature_major(cache_swa_v)

    n_p, n_s = bp * lp, bs * ls
    x_parts = (x_prompt.reshape(n_p, d), x_sample.reshape(n_s, d))
    g = norm_g.reshape(depth, 4, 1, d)
    sub_g = diff_subln_g.reshape(-1, 1, 2 * DIFF_HD)
    p_rows = dict(row0=0, tiles_per_batch=n_p // tm)
    s_rows = dict(row0=1, tiles_per_batch=ls // tm)
    diff_k_out, diff_v_out, swa_k_out, swa_v_out = [], [], [], []

    for i in range(depth):
        j = i // N_MIXERS
        if len(x_parts) == 2:
            p_src, s_src = (x_parts[0], n_p, 0), (x_parts[1], n_s, 0)
        else:
            p_src, s_src = (x_parts[0], n_p, 0), (x_parts[0], n_s, n_p // tm)
        if i % N_MIXERS == 0:
            lam_init = 0.8 - 0.6 * math.exp(-0.3 * i)
            qp, kp, vp = _qkv_diff(*p_src, g, mods, w_qkv_diff, None, layer=i, j=j, latent=False, **p_rows)
            op = _diff_prompt_attention(qp, kp, vp, diff_lambda, sub_g, j=j, seq=lp, lam_init=lam_init)
            diff_k_out.append(kp.reshape(bp, lp, DIFF_HEADS, 2 * DIFF_HD))
            diff_v_out.append(vp.reshape(bp, lp, DIFF_HEADS, 2 * DIFF_HD))
            qs, ks, vs = _qkv_diff(*s_src, g, mods, w_qkv_diff, tables, layer=i, j=j, latent=True, **s_rows)
            os_ = _diff_latent_attention(qs, ks, vs, cdk, cdv, diff_lambda, sub_g,
                                         j=j, seq=ls, lc=lc, lam_init=lam_init)
            w_o = w_o_diff
        else:
            qp, ktp, vtp = _qkv_swa(*p_src, g, mods, w_qkv_swa, None, layer=i, j=j, latent=False,
                                    seq=lp, **p_rows)
            op = _swa_prompt_attention(qp, ktp, vtp, swa_sink[j], seq=lp)
            swa_k_out.append(_swa_cache_from_feature_major(ktp, lp))
            swa_v_out.append(_swa_cache_from_feature_major(vtp, lp))
            qs, kds, vds = _qkv_swa(*s_src, g, mods, w_qkv_swa, tables, layer=i, j=j, latent=True,
                                    seq=ls, **s_rows)
            os_ = _swa_latent_attention(qs, kds, vds, cskt, csvt, swa_sink[j], j=j, seq=ls)
            w_o = w_o_swa
        last = i == depth - 1
        out = _post_attn_ffn(op, os_, x_parts, w_o, w_gate, w_up, w_down, g, mods, layer=i, j=j,
                             tiles_per_request=ls // tm, split_out=last)
        x_parts = out if last else (out,)
    xp, xs = x_parts

    return (xp.reshape(bp, lp, d), xs.reshape(bs, ls, d),
            jnp.stack(diff_k_out, axis=1), jnp.stack(diff_v_out, axis=1),
            jnp.stack(swa_k_out, axis=1), jnp.stack(swa_v_out, axis=1))
```

```python
import functools
import math

import jax
import jax.numpy as jnp
from jax import lax
from jax.experimental import pallas as pl
from jax.experimental.pallas import tpu as pltpu

F32 = jnp.float32
BF16 = jnp.bfloat16

GRID_W = 64
N_MIXERS = 2
DIFF_HEADS = 8
DIFF_HD = 64
SWA_HEADS = 16
SWA_KV_HEADS = 4
SWA_GROUP = SWA_HEADS // SWA_KV_HEADS
SWA_HD = 64
ROT_DIM = 64
WINDOW = 128
ROPE_BASE = 10000.0
EPS = 1e-6
NEG_INF = -1e30

LANES = 128
SUBLANES = 8
ROW_TILE = 512
PROMPT_SCORE_BUFFERS = 8
VMEM_LIMIT = 48 * 1024 * 1024
FFN_VMEM_LIMIT = 58 * 1024 * 1024
WEIGHT_STAGE_ROWS = 128
WEIGHT_STAGE_SLOTS = 6
NT_DIMS = (((1,), (1,)), ((), ()))


def _params(n_axes):
    return pltpu.CompilerParams(dimension_semantics=("arbitrary",) * n_axes,
                                vmem_limit_bytes=VMEM_LIMIT)


def _resident(shape):
    return pl.BlockSpec(shape, lambda *_: (0,) * len(shape), pipeline_mode=pl.Buffered(1))


def _layer_resident(shape, layer):
    return pl.BlockSpec((None,) + tuple(shape[1:]), lambda *_: (layer,) + (0,) * (len(shape) - 1),
                        pipeline_mode=pl.Buffered(1))


def _rms(x, g):
    ms = jnp.mean(x * x, axis=-1, keepdims=True)
    return (x * lax.rsqrt(ms + EPS)) * g


def _half_masks(dtype):
    lane = lax.broadcasted_iota(jnp.int32, (1, LANES), 1)
    lo = lane < (LANES // 2)
    return jnp.where(lo, 1.0, 0.0).astype(dtype), jnp.where(lo, 0.0, 1.0).astype(dtype)


def _lo64():
    return lax.broadcasted_iota(jnp.int32, (1, LANES), 1) < (LANES // 2)


def _ones_column(rows):
    return jnp.ones((rows, LANES), BF16)


def _ones_row(cols):
    return jnp.ones((LANES, cols), BF16)


def _mod_kernel(cond_ref, w_ref, b_ref, out_ref):
    c = cond_ref[...]
    s = c * jax.nn.sigmoid(c)
    out_ref[...] = jnp.dot(s.astype(BF16), w_ref[...].astype(BF16),
                           preferred_element_type=F32) + b_ref[...]


def _modulation(cond8, w_mod, b_mod):
    depth, d, n = w_mod.shape
    tn = 1536
    return pl.pallas_call(
        _mod_kernel,
        out_shape=jax.ShapeDtypeStruct((depth, 8, n), F32),
        grid=(depth, n // tn),
        in_specs=[pl.BlockSpec((8, d), lambda i, j: (0, 0)),
                  pl.BlockSpec((None, d, tn), lambda i, j: (i, 0, j)),
                  pl.BlockSpec((None, 1, tn), lambda i, j: (i, 0, j))],
        out_specs=pl.BlockSpec((None, 8, tn), lambda i, j: (i, 0, j)),
        compiler_params=_params(2),
        name="modulation",
    )(cond8, w_mod, b_mod.reshape(depth, 1, n))


def _mod_row(row0, tiles_per_batch):
    return row0 + pl.program_id(0) // tiles_per_batch


def _rope_slab(xs, cos, sin_signed, lo16):
    left = pltpu.roll(xs, LANES - 16, 1)
    right = pltpu.roll(xs, 16, 1)
    return xs * cos + jnp.where(lo16, left, right) * sin_signed


def _lo16_mask():
    lane = lax.broadcasted_iota(jnp.int32, (1, LANES), 1)
    return (lane % 32) < 16


def _pre_norm(x_ref, g_ref, mod_ref, r, d, slot):
    shift = mod_ref[pl.ds(r, 1), slot * d:(slot + 1) * d]
    scale = mod_ref[pl.ds(r, 1), (slot + 1) * d:(slot + 2) * d]
    return _rms(x_ref[...], g_ref[0]) * (1 + scale) + shift


def _qkv_diff_kernel(x_ref, g_ref, mod_ref, w_ref, *rest, latent, row0, tiles_per_batch):
    d = x_ref.shape[1]
    if latent:
        cos_ref, sin_ref, q_ref, k_ref, v_ref = rest
        cos, sin = cos_ref[...], sin_ref[...]
        lo16 = _lo16_mask()
    else:
        q_ref, k_ref, v_ref = rest
    r = _mod_row(row0, tiles_per_batch)
    h = _pre_norm(x_ref, g_ref, mod_ref, r, d, 0).astype(BF16)
    cw = 512
    for c in range(3 * d // cw):
        acc = jnp.dot(h, w_ref[:, c * cw:(c + 1) * cw], preferred_element_type=F32)
        which, off = divmod(c * cw, d)
        dst = (q_ref, k_ref, v_ref)[which]
        for s in range(cw // LANES):
            xs = acc[:, s * LANES:(s + 1) * LANES]
            if latent and which < 2:
                xs = _rope_slab(xs, cos, sin, lo16)
            if which == 0:
                xs = xs * (DIFF_HD ** -0.5)
            lo = off + s * LANES
            dst[:, lo:lo + LANES] = xs.astype(dst.dtype)


def _qkv_diff(x, n, tile0, g, mods, w, tables, *, layer, j, latent, row0, tiles_per_batch):
    d = x.shape[1]
    tm = ROW_TILE
    in_specs = [pl.BlockSpec((tm, d), lambda i: (i + tile0, 0)), _layer_resident(g.shape, layer),
                _layer_resident(mods.shape, layer), _layer_resident(w.shape, j)]
    args = [x, g, mods, w]
    kv_dtype = F32
    if latent:
        nt = tables[0].shape[0] // tm
        in_specs += [pl.BlockSpec((tm, LANES), lambda i: (i % nt, 0))] * 2
        args += list(tables)
        kv_dtype = BF16
    out_spec = pl.BlockSpec((tm, d), lambda i: (i, 0))
    return pl.pallas_call(
        functools.partial(_qkv_diff_kernel, latent=latent, row0=row0, tiles_per_batch=tiles_per_batch),
        out_shape=(jax.ShapeDtypeStruct((n, d), BF16),
                   jax.ShapeDtypeStruct((n, d), kv_dtype),
                   jax.ShapeDtypeStruct((n, d), kv_dtype)),
        grid=(n // tm,),
        in_specs=in_specs,
        out_specs=(out_spec, out_spec, out_spec),
        compiler_params=_params(1),
        name="qkv_diff_latent" if latent else "qkv_diff_prompt",
    )(*args)


def _qkv_swa_kernel(x_ref, g_ref, mod_ref, w_ref, *rest, latent, row0, tiles_per_batch, seq):
    d = x_ref.shape[1]
    nkv = SWA_KV_HEADS * SWA_HD
    if latent:
        cos_ref, sin_ref, q_ref, kd_ref, vd_ref = rest
        cos, sin = cos_ref[...], sin_ref[...]
        lo16 = _lo16_mask()
    else:
        q_ref, kt_ref, vt_ref = rest
    lo64 = _lo64()
    r = _mod_row(row0, tiles_per_batch)
    h = _pre_norm(x_ref, g_ref, mod_ref, r, d, 0).astype(BF16)
    cw = 512
    for c in range(d // cw):
        acc = jnp.dot(h, w_ref[:, c * cw:(c + 1) * cw], preferred_element_type=F32)
        for s in range(cw // LANES):
            xs = acc[:, s * LANES:(s + 1) * LANES]
            if latent:
                xs = _rope_slab(xs, cos, sin, lo16)
            lo = c * cw + s * LANES
            q_ref[:, lo:lo + LANES] = (xs * (SWA_HD ** -0.5)).astype(BF16)
    kv = jnp.dot(h, w_ref[:, d:d + 2 * nkv], preferred_element_type=F32)
    if not latent:
        for b in range(x_ref.shape[0] // seq):
            kt_ref[b] = kv[b * seq:(b + 1) * seq, :nkv].T
            vt_ref[b] = kv[b * seq:(b + 1) * seq, nkv:].T
        return
    for which, dst in enumerate((kd_ref, vd_ref)):
        for s in range(nkv // LANES):
            xs = kv[:, which * nkv + s * LANES: which * nkv + (s + 1) * LANES]
            if which == 0:
                xs = _rope_slab(xs, cos, sin, lo16)
            sw = pltpu.roll(xs, LANES // 2, 1)
            dst[:, (2 * s) * LANES:(2 * s + 1) * LANES] = jnp.where(lo64, xs, sw).astype(BF16)
            dst[:, (2 * s + 1) * LANES:(2 * s + 2) * LANES] = jnp.where(lo64, sw, xs).astype(BF16)


def _qkv_swa(x, n, tile0, g, mods, w, tables, *, layer, j, latent, row0, tiles_per_batch, seq):
    d = x.shape[1]
    tm = ROW_TILE
    nkv = SWA_KV_HEADS * SWA_HD
    in_specs = [pl.BlockSpec((tm, d), lambda i: (i + tile0, 0)), _layer_resident(g.shape, layer),
                _layer_resident(mods.shape, layer), _layer_resident(w.shape, j)]
    args = [x, g, mods, w]
    out_shape = [jax.ShapeDtypeStruct((n, d), BF16)]
    out_specs = [pl.BlockSpec((tm, d), lambda i: (i, 0))]
    if latent:
        nt = tables[0].shape[0] // tm
        in_specs += [pl.BlockSpec((tm, LANES), lambda i: (i % nt, 0))] * 2
        args += list(tables)
        out_shape += [jax.ShapeDtypeStruct((n, 2 * nkv), BF16)] * 2
        out_specs += [pl.BlockSpec((tm, 2 * nkv), lambda i: (i, 0))] * 2
    else:
        out_shape += [jax.ShapeDtypeStruct((n // seq, nkv, seq), F32)] * 2
        out_specs += [pl.BlockSpec((tm // seq, nkv, seq), lambda i: (i, 0, 0))] * 2
    return pl.pallas_call(
        functools.partial(_qkv_swa_kernel, latent=latent, row0=row0, tiles_per_batch=tiles_per_batch,
                          seq=seq),
        out_shape=tuple(out_shape),
        grid=(n // tm,),
        in_specs=in_specs,
        out_specs=tuple(out_specs),
        compiler_params=_params(1),
        name="qkv_swa_latent" if latent else "qkv_swa_prompt",
    )(*args)


def _diff_lambda(lam_ref, lam_init):
    lp = lam_ref[...]
    a = jnp.sum(lp[0:1] * lp[1:2], axis=-1, keepdims=True)
    b = jnp.sum(lp[2:3] * lp[3:4], axis=-1, keepdims=True)
    return jnp.exp(a) - jnp.exp(b) + lam_init


def _diff_combine(acc, tq, lam, g, lam_init):
    o12 = acc[:, :LANES] / acc[:, LANES:]
    o = o12[:tq] - lam * o12[tq:]
    return _rms(o, g) * (1.0 - lam_init)


def _stack_maps(q):
    m_lo, m_hi = _half_masks(BF16)
    return jnp.concatenate([q * m_lo, q * m_hi], axis=0)


def _run_pipelined(items, scores, finish, s_bufs):
    depth = len(s_bufs)
    states = {i: scores(items[i], s_bufs[i]) for i in range(min(depth - 1, len(items)))}
    for i, item in enumerate(items):
        ahead = i + depth - 1
        if ahead < len(items):
            states[ahead] = scores(items[ahead], s_bufs[ahead % depth])
        finish(item, s_bufs[i % depth], states.pop(i))


def _store_scores(s_ref, col0, s, mrun):
    s_ref[:, col0:col0 + s.shape[1]] = s
    for t in range(s.shape[1] // LANES):
        blk = s[:, t * LANES:(t + 1) * LANES]
        mrun = blk if mrun is None else jnp.maximum(mrun, blk)
    return mrun


def _exp_block(s_ref, col0, width, mb):
    return jnp.concatenate(
        [jnp.exp(s_ref[:, col0 + t * LANES:col0 + (t + 1) * LANES] - mb).astype(BF16)
         for t in range(width // LANES)], axis=1)


def _diff_prompt_kernel(q_ref, k_ref, v_ref, lam_ref, g_ref, o_ref, *s_bufs, lam_init, seq):
    lam = _diff_lambda(lam_ref, lam_init)
    g = g_ref[...]
    ones = _ones_column(seq)
    items = [(r, h) for r in range(q_ref.shape[0] // seq) for h in range(DIFF_HEADS)]

    def scores(item, s_ref):
        r, h = item
        rows, sl = slice(r * seq, (r + 1) * seq), slice(h * LANES, (h + 1) * LANES)
        s = lax.dot_general(_stack_maps(q_ref[rows, sl]), k_ref[rows, sl].astype(BF16), NT_DIMS,
                            preferred_element_type=F32)
        return _store_scores(s_ref, 0, s, None)

    def finish(item, s_ref, mrun):
        r, h = item
        rows, sl = slice(r * seq, (r + 1) * seq), slice(h * LANES, (h + 1) * LANES)
        mb = jnp.broadcast_to(jnp.max(mrun, axis=-1, keepdims=True), (2 * seq, LANES))
        vx = jnp.concatenate([v_ref[rows, sl].astype(BF16), ones], axis=1)
        acc = jnp.dot(_exp_block(s_ref, 0, seq, mb), vx, preferred_element_type=F32)
        o_ref[rows, sl] = _diff_combine(acc, seq, lam, g, lam_init).astype(BF16)

    _run_pipelined(items, scores, finish, s_bufs)


def _diff_prompt_attention(q, k, v, lam_params, subln_g, *, j, seq, lam_init):
    n, d = q.shape
    req = 2
    spec = pl.BlockSpec((req * seq, d), lambda b: (b, 0))
    return pl.pallas_call(
        functools.partial(_diff_prompt_kernel, lam_init=lam_init, seq=seq),
        out_shape=jax.ShapeDtypeStruct((n, d), BF16),
        grid=(n // (req * seq),),
        in_specs=[spec, spec, spec, _layer_resident(lam_params.shape, j), _layer_resident(subln_g.shape, j)],
        out_specs=spec,
        scratch_shapes=[pltpu.VMEM((2 * seq, seq), F32)] * PROMPT_SCORE_BUFFERS,
        compiler_params=_params(1),
        name="diff_attn_prompt",
    )(q, k, v, lam_params, subln_g)


def _diff_latent_kernel(q_ref, kc_ref, vc_ref, kl_ref, vl_ref, lam_ref, g_ref, o_ref,
                        kk_ref, vx_ref, s0_ref, s1_ref, *, lam_init, lc, tq, key_chunk):
    seq = q_ref.shape[0]
    nk = kk_ref.shape[0]
    h = pl.program_id(1)
    kk_ref[0:lc, :] = kc_ref[pl.ds(h, lc, stride=DIFF_HEADS), :].astype(BF16)
    kk_ref[lc:, :] = kl_ref[...]
    vx_ref[0:lc, 0:LANES] = vc_ref[pl.ds(h, lc, stride=DIFF_HEADS), :].astype(BF16)
    vx_ref[lc:, 0:LANES] = vl_ref[...]
    vx_ref[:, LANES:2 * LANES] = _ones_column(nk)

    masks = _half_masks(BF16)
    nchunk = nk // key_chunk
    items = [(rt, m) for rt in range(seq // tq) for m in range(2)]
    lam = _diff_lambda(lam_ref, lam_init)
    g = g_ref[...]
    first_map = {}

    def scores(item, s_ref):
        rt, m = item
        q = q_ref[rt * tq:(rt + 1) * tq, :] * masks[m]
        mrun = None
        for c in range(nchunk):
            s = lax.dot_general(q, kk_ref[c * key_chunk:(c + 1) * key_chunk, :], NT_DIMS,
                                preferred_element_type=F32)
            mrun = _store_scores(s_ref, c * key_chunk, s, mrun)
        return mrun

    def finish(item, s_ref, mrun):
        rt, m = item
        mb = jnp.broadcast_to(jnp.max(mrun, axis=-1, keepdims=True), (tq, LANES))
        acc = None
        for c in range(nchunk):
            part = jnp.dot(_exp_block(s_ref, c * key_chunk, key_chunk, mb),
                           vx_ref[c * key_chunk:(c + 1) * key_chunk, :], preferred_element_type=F32)
            acc = part if acc is None else acc + part
        o_m = acc[:, :LANES] / acc[:, LANES:]
        if m == 0:
            first_map[rt] = o_m
        else:
            o = _rms(first_map.pop(rt) - lam * o_m, g) * (1.0 - lam_init)
            o_ref[rt * tq:(rt + 1) * tq, :] = o.astype(BF16)

    _run_pipelined(items, scores, finish, (s0_ref, s1_ref))


def _diff_latent_attention(q, k, v, cache_k, cache_v, lam_params, subln_g, *, j, seq, lc, lam_init):
    n, d = q.shape
    nb = cache_k.shape[0]
    tq = 512
    key_chunk = 512
    q_spec = pl.BlockSpec((seq, LANES), lambda b, h: (b, h))
    c_spec = pl.BlockSpec((None, lc * DIFF_HEADS, LANES), lambda b, h: (b, j, 0))
    return pl.pallas_call(
        functools.partial(_diff_latent_kernel, lam_init=lam_init, lc=lc, tq=tq, key_chunk=key_chunk),
        out_shape=jax.ShapeDtypeStruct((n, d), BF16),
        grid=(nb, DIFF_HEADS),
        in_specs=[q_spec, c_spec, c_spec, q_spec, q_spec,
                  _layer_resident(lam_params.shape, j), _layer_resident(subln_g.shape, j)],
        out_specs=q_spec,
        scratch_shapes=[pltpu.VMEM((lc + seq, LANES), BF16), pltpu.VMEM((lc + seq, 2 * LANES), BF16),
                        pltpu.VMEM((tq, lc + seq), F32), pltpu.VMEM((tq, lc + seq), F32)],
        compiler_params=_params(2),
        name="diff_attn_latent",
    )(q, cache_k, cache_v, k, v, lam_params, subln_g)


def _stack_group(q_ref, rows, kv_local):
    m_lo, m_hi = _half_masks(BF16)
    parts = []
    for gb in range(SWA_GROUP // 2):
        blk = kv_local * (SWA_GROUP // 2) + gb
        qb = q_ref[rows, blk * LANES:(blk + 1) * LANES]
        parts += [qb * m_lo, qb * m_hi]
    return jnp.concatenate(parts, axis=0)


def _sink_column(sink_ref, first_head, tq):
    return jnp.concatenate([jnp.full((tq, LANES), sink_ref[first_head + g], F32) for g in range(SWA_GROUP)],
                           axis=0)


def _write_group(o_ref, rows, kv_local, o, tq):
    lo64 = _lo64()
    for gb in range(SWA_GROUP // 2):
        blk = kv_local * (SWA_GROUP // 2) + gb
        even = o[(2 * gb) * tq:(2 * gb + 1) * tq]
        odd = o[(2 * gb + 1) * tq:(2 * gb + 2) * tq]
        o_ref[rows, blk * LANES:(blk + 1) * LANES] = jnp.where(lo64, even, odd).astype(BF16)


def _dup_rows(x_t):
    xb = x_t.astype(BF16)
    return jnp.concatenate([xb, xb], axis=0)


def _sink_finish(mrun, sk, rows):
    mb = jnp.maximum(jnp.broadcast_to(jnp.max(mrun, axis=-1, keepdims=True), (rows, LANES)), sk)
    return mb, jnp.exp(sk - mb)


def _swa_prompt_kernel(sink_ref, q_ref, kt_ref, vt_ref, o_ref, *s_bufs):
    seq = kt_ref.shape[2]
    rows = SWA_GROUP * seq
    ones = _ones_row(seq)
    items = [(r, j) for r in range(kt_ref.shape[0]) for j in range(SWA_KV_HEADS)]

    def scores(item, s_ref):
        r, j = item
        kd = _dup_rows(kt_ref[r, j * SWA_HD:(j + 1) * SWA_HD, :])
        s = jnp.dot(_stack_group(q_ref, slice(r * seq, (r + 1) * seq), j), kd, preferred_element_type=F32)
        return _store_scores(s_ref, 0, s, None)

    def finish(item, s_ref, mrun):
        r, j = item
        sk = _sink_column(sink_ref, j * SWA_GROUP, seq)
        mb, sink_term = _sink_finish(mrun, sk, rows)
        vx = jnp.concatenate([_dup_rows(vt_ref[r, j * SWA_HD:(j + 1) * SWA_HD, :]), ones], axis=0)
        acc = lax.dot_general(_exp_block(s_ref, 0, seq, mb), vx, NT_DIMS, preferred_element_type=F32)
        o = acc[:, :LANES] / (acc[:, LANES:] + sink_term)
        _write_group(o_ref, slice(r * seq, (r + 1) * seq), j, o, seq)

    _run_pipelined(items, scores, finish, s_bufs)


def _swa_prompt_attention(q, kt, vt, sink, *, seq):
    n, d = q.shape
    nkv = kt.shape[1]
    req = 2
    t_spec = pl.BlockSpec((req, nkv, seq), lambda b: (b, 0, 0))
    return pl.pallas_call(
        _swa_prompt_kernel,
        out_shape=jax.ShapeDtypeStruct((n, d), BF16),
        grid=(n // (req * seq),),
        in_specs=[pl.BlockSpec(memory_space=pltpu.SMEM),
                  pl.BlockSpec((req * seq, d), lambda b: (b, 0)), t_spec, t_spec],
        out_specs=pl.BlockSpec((req * seq, d), lambda b: (b, 0)),
        scratch_shapes=[pltpu.VMEM((SWA_GROUP * seq, seq), F32)] * (PROMPT_SCORE_BUFFERS // 2),
        compiler_params=_params(1),
        name="swa_attn_prompt",
    )(sink, q, kt, vt)


def _swa_latent_kernel(sink_ref, q_ref, kc_ref, vc_ref, kl_ref, vl_ref, o_ref, kcd_ref, vcx_ref,
                       s0_ref, s1_ref, *, tq, span):
    pair = pl.program_id(1)
    tiles = q_ref.shape[0] // tq
    first_tile = pl.program_id(2) * tiles
    seq = kl_ref.shape[0]
    lc = kc_ref.shape[1]
    rows = SWA_GROUP * tq
    ones_row = _ones_row(lc)
    for jj in range(2):
        kcd_ref[jj] = _dup_rows(kc_ref[jj * SWA_HD:(jj + 1) * SWA_HD, :])
        vcx_ref[jj] = jnp.concatenate([_dup_rows(vc_ref[jj * SWA_HD:(jj + 1) * SWA_HD, :]), ones_row], axis=0)
    ones_col = _ones_column(span)
    items = [(t, jj) for t in range(tiles) for jj in range(2)]
    windows, biases = {}, {}

    def window(t):
        if t not in windows:
            q0 = (first_tile + t) * tq
            windows[t] = (q0, pl.multiple_of(jnp.clip(q0 - WINDOW, 0, seq - span), WINDOW))
        return windows[t]

    def bias_for(t):
        if t not in biases:
            q0, ws = window(t)
            q_pos = q0 + lax.broadcasted_iota(jnp.int32, (tq, span), 0)
            k_pos = ws + lax.broadcasted_iota(jnp.int32, (tq, span), 1)
            b = jnp.where(jnp.abs(q_pos - k_pos) <= WINDOW, 0.0, NEG_INF).astype(F32)
            biases[t] = jnp.concatenate([b] * SWA_GROUP, axis=0)
        return biases[t]

    def scores(item, s_ref):
        t, jj = item
        _, ws = window(t)
        qs = _stack_group(q_ref, slice(t * tq, (t + 1) * tq), jj)
        s_c = jnp.dot(qs, kcd_ref[jj], preferred_element_type=F32)
        mrun = _store_scores(s_ref, 0, s_c, None)
        s_w = lax.dot_general(qs, kl_ref[pl.ds(ws, span), jj * LANES:(jj + 1) * LANES], NT_DIMS,
                              preferred_element_type=F32) + bias_for(t)
        return _store_scores(s_ref, lc, s_w, mrun)

    def finish(item, s_ref, mrun):
        t, jj = item
        _, ws = window(t)
        sk = _sink_column(sink_ref, (2 * pair + jj) * SWA_GROUP, tq)
        mb, sink_term = _sink_finish(mrun, sk, rows)
        vwx = jnp.concatenate([vl_ref[pl.ds(ws, span), jj * LANES:(jj + 1) * LANES], ones_col], axis=1)
        acc = (lax.dot_general(_exp_block(s_ref, 0, lc, mb), vcx_ref[jj], NT_DIMS, preferred_element_type=F32)
               + jnp.dot(_exp_block(s_ref, lc, span, mb), vwx, preferred_element_type=F32))
        o = acc[:, :LANES] / (acc[:, LANES:] + sink_term)
        _write_group(o_ref, slice(t * tq, (t + 1) * tq), jj, o, tq)

    _run_pipelined(items, scores, finish, (s0_ref, s1_ref))


def _swa_latent_attention(q, kd, vd, cache_kt, cache_vt, sink, *, j, seq):
    n, d = q.shape
    nb, _, lc = cache_kt.shape
    tq = 256
    span = tq + 2 * WINDOW
    npair = SWA_KV_HEADS // 2
    wq = d // npair
    parts = 2
    q_spec = pl.BlockSpec((seq // parts, wq), lambda b, p, i: (b * parts + i, p))
    c_spec = pl.BlockSpec((None, 2 * SWA_HD, lc), lambda b, p, i: (b, j * npair + p, 0))
    l_spec = pl.BlockSpec((seq, 2 * LANES), lambda b, p, i: (b, p))
    s_shape = pltpu.VMEM((SWA_GROUP * tq, lc + span), F32)
    return pl.pallas_call(
        functools.partial(_swa_latent_kernel, tq=tq, span=span),
        out_shape=jax.ShapeDtypeStruct((n, d), BF16),
        grid=(nb, npair, parts),
        in_specs=[pl.BlockSpec(memory_space=pltpu.SMEM), q_spec, c_spec, c_spec, l_spec, l_spec],
        out_specs=q_spec,
        scratch_shapes=[pltpu.VMEM((2, 2 * SWA_HD, lc), BF16), pltpu.VMEM((2, 2 * LANES, lc), BF16),
                        s_shape, s_shape],
        compiler_params=_params(3),
        name="swa_attn_latent",
    )(sink, q, cache_kt, cache_vt, kd, vd)


def _load_weights_as_bf16(jobs, stages, sems):
    order = []
    rings = {w: [] for w in stages}
    for src, dst in jobs:
        w = src.shape[1]
        slots = stages[w].shape[0]
        for k in range(src.shape[0] // WEIGHT_STAGE_ROWS):
            rows = pl.ds(k * WEIGHT_STAGE_ROWS, WEIGHT_STAGE_ROWS)
            slot = len(rings[w]) % slots
            copy = pltpu.make_async_copy(src.at[rows, :], stages[w].at[slot], sems[w].at[slot])
            order.append((w, len(rings[w])))
            rings[w].append((copy, slot, dst, rows))
    for w, ring in rings.items():
        for copy, _, _, _ in ring[:stages[w].shape[0]]:
            copy.start()
    for w, k in order:
        copy, slot, dst, rows = rings[w][k]
        copy.wait()
        dst[rows, :] = stages[w][slot].astype(BF16)
        ahead = k + stages[w].shape[0]
        if ahead < len(rings[w]):
            rings[w][ahead][0].start()


def _post_attn_ffn_kernel(*refs, layer, j, n_prompt_tiles, tiles_per_request, split_x, split_out):
    refs = list(refs)
    op_ref, os_ref = refs[:2]
    x_refs = refs[2:4] if split_x else refs[2:3]
    wo_hbm, wg_hbm, wu_hbm, wd_hbm, g_ref, mod_ref = refs[2 + len(x_refs):8 + len(x_refs)]
    n_out = 2 if split_out else 1
    out_refs = refs[8 + len(x_refs):8 + len(x_refs) + n_out]
    wo_ref, wg_ref, wu_ref, wd_ref, stage_d, stage_ff, sem_d, sem_ff = refs[8 + len(x_refs) + n_out:]
    d = wo_ref.shape[1]
    dff = wg_ref.shape[1]
    i = pl.program_id(0)

    @pl.when(i == 0)
    def _():
        _load_weights_as_bf16(
            [(wo_hbm.at[j], wo_ref), (wg_hbm.at[layer], wg_ref), (wu_hbm.at[layer], wu_ref),
             (wd_hbm.at[layer], wd_ref)],
            {d: stage_d, dff: stage_ff}, {d: sem_d, dff: sem_ff})

    is_prompt = i < n_prompt_tiles
    r = jnp.where(is_prompt, 0, 1 + (i - n_prompt_tiles) // tiles_per_request)

    def mod(slot):
        return mod_ref[pl.ds(r, 1), slot * d:(slot + 1) * d]

    o = jnp.where(is_prompt, op_ref[...], os_ref[...])
    x = jnp.where(is_prompt, x_refs[0][...], x_refs[1][...]) if split_x else x_refs[0][...]
    y = jnp.dot(o, wo_ref[...], preferred_element_type=F32)
    x = x + mod(2) * _rms(y, g_ref[1])
    h = (_rms(x, g_ref[2]) * (1 + mod(4)) + mod(3)).astype(BF16)
    cw = 256
    y = jnp.zeros((h.shape[0], d), F32)
    for c in range(dff // cw):
        a = jnp.dot(h, wg_ref[:, c * cw:(c + 1) * cw], preferred_element_type=F32)
        u = jnp.dot(h, wu_ref[:, c * cw:(c + 1) * cw], preferred_element_type=F32)
        t = (a * jax.nn.sigmoid(a)) * u
        y = y + jnp.dot(t.astype(BF16), wd_ref[c * cw:(c + 1) * cw, :], preferred_element_type=F32)
    out = x + mod(5) * _rms(y, g_ref[3])
    if split_out:
        @pl.when(is_prompt)
        def _():
            out_refs[0][...] = out

        @pl.when(jnp.logical_not(is_prompt))
        def _():
            out_refs[1][...] = out
    else:
        out_refs[0][...] = out


def _post_attn_ffn(o_p, o_s, xs_in, w_o, wg, wu, wd, g, mods, *, layer, j, tiles_per_request, split_out):
    n_p, d = o_p.shape
    n_s = o_s.shape[0]
    tm = ROW_TILE
    tp, ts = n_p // tm, n_s // tm
    prompt_rows = pl.BlockSpec((tm, d), lambda i: (jnp.minimum(i, tp - 1), 0))
    latent_rows = pl.BlockSpec((tm, d), lambda i: (jnp.maximum(i - tp, 0), 0))
    all_rows = pl.BlockSpec((tm, d), lambda i: (i, 0))
    split_x = len(xs_in) == 2
    in_specs = [prompt_rows, latent_rows] + ([prompt_rows, latent_rows] if split_x else [all_rows])
    hbm = pl.BlockSpec(memory_space=pl.ANY)
    in_specs += [hbm, hbm, hbm, hbm, _layer_resident(g.shape, layer), _layer_resident(mods.shape, layer)]
    if split_out:
        out_shape = (jax.ShapeDtypeStruct((n_p, d), F32), jax.ShapeDtypeStruct((n_s, d), F32))
        out_specs = (prompt_rows, latent_rows)
    else:
        out_shape = jax.ShapeDtypeStruct((n_p + n_s, d), F32)
        out_specs = all_rows
    dff = wg.shape[2]
    scratch = [pltpu.VMEM((d, d), BF16), pltpu.VMEM((d, dff), BF16), pltpu.VMEM((d, dff), BF16),
               pltpu.VMEM((dff, d), BF16),
               pltpu.VMEM((WEIGHT_STAGE_SLOTS, WEIGHT_STAGE_ROWS, d), F32),
               pltpu.VMEM((WEIGHT_STAGE_SLOTS, WEIGHT_STAGE_ROWS, dff), F32),
               pltpu.SemaphoreType.DMA((WEIGHT_STAGE_SLOTS,)), pltpu.SemaphoreType.DMA((WEIGHT_STAGE_SLOTS,))]
    return pl.pallas_call(
        functools.partial(_post_attn_ffn_kernel, layer=layer, j=j, n_prompt_tiles=tp,
                          tiles_per_request=tiles_per_request, split_x=split_x, split_out=split_out),
        out_shape=out_shape,
        grid=(tp + ts,),
        in_specs=in_specs,
        out_specs=out_specs,
        scratch_shapes=scratch,
        compiler_params=pltpu.CompilerParams(dimension_semantics=("arbitrary",),
                                             vmem_limit_bytes=FFN_VMEM_LIMIT),
        name="post_attn_ffn",
    )(o_p, o_s, *xs_in, w_o, wg, wu, wd, g, mods)


def _rope_tables(n_lat):
    t = jnp.arange(n_lat)
    row = (t // GRID_W).astype(F32)
    col = (t % GRID_W).astype(F32)
    nf = ROT_DIM // 4
    inv = ROPE_BASE ** (-jnp.arange(nf, dtype=F32) / nf)
    ar = row[:, None] * inv[None, :]
    ac = col[:, None] * inv[None, :]
    ang = jnp.concatenate([ar, ar, ac, ac], axis=-1)
    cos, sin = jnp.cos(ang), jnp.sin(ang)
    sign = jnp.where((jnp.arange(ROT_DIM) % 32) < 16, -1.0, 1.0).astype(F32)
    reps = LANES // ROT_DIM
    return jnp.tile(cos, (1, reps)), jnp.tile(sin * sign, (1, reps))


def _swa_cache_to_feature_major(cache):
    nb, nl, lc, nh, hd = cache.shape
    return cache.transpose(0, 1, 3, 4, 2).reshape(nb, nl * nh * hd, lc)


def _swa_cache_from_feature_major(xt, seq):
    nb = xt.shape[0]
    return xt.reshape(nb, SWA_KV_HEADS, SWA_HD, seq).transpose(0, 3, 1, 2)


def kernel(x_prompt, x_sample, cache_diff_k, cache_diff_v, cache_swa_k, cache_swa_v, c, c_ctx,
           w_mod, b_mod, norm_g, w_qkv_diff, diff_lambda, diff_subln_g, w_o_diff,
           w_qkv_swa, swa_sink, w_o_swa, w_gate, w_up, w_down):
    bp, lp, d = x_prompt.shape
    bs, ls, _ = x_sample.shape
    lc = cache_diff_k.shape[2]
    depth = w_mod.shape[0]
    tm = ROW_TILE

    cond8 = jnp.concatenate([c_ctx[None, :], c, jnp.zeros((8 - 1 - bs, d), F32)], axis=0)
    mods = _modulation(cond8, w_mod, b_mod)
    tables = _rope_tables(ls)

    w_qkv_diff, w_qkv_swa = w_qkv_diff.astype(BF16), w_qkv_swa.astype(BF16)
    cdk = cache_diff_k.reshape(bs, -1, 2 * DIFF_HD)
    cdv = cache_diff_v.reshape(bs, -1, 2 * DIFF_HD)
    cskt = _swa_cache_to_feature_major(cache_swa_k)
    csvt = _swa_cache_to_feature_major(cache_swa_v)

    n_p, n_s = bp * lp, bs * ls
    x_parts = (x_prompt.reshape(n_p, d), x_sample.reshape(n_s, d))
    g = norm_g.reshape(depth, 4, 1, d)
    sub_g = diff_subln_g.reshape(-1, 1, 2 * DIFF_HD)
    p_rows = dict(row0=0, tiles_per_batch=n_p // tm)
    s_rows = dict(row0=1, tiles_per_batch=ls // tm)
    diff_k_out, diff_v_out, swa_k_out, swa_v_out = [], [], [], []

    for i in range(depth):
        j = i // N_MIXERS
        if len(x_parts) == 2:
            p_src, s_src = (x_parts[0], n_p, 0), (x_parts[1], n_s, 0)
        else:
            p_src, s_src = (x_parts[0], n_p, 0), (x_parts[0], n_s, n_p // tm)
        if i % N_MIXERS == 0:
            lam_init = 0.8 - 0.6 * math.exp(-0.3 * i)
            qp, kp, vp = _qkv_diff(*p_src, g, mods, w_qkv_diff, None, layer=i, j=j, latent=False, **p_rows)
            op = _diff_prompt_attention(qp, kp, vp, diff_lambda, sub_g, j=j, seq=lp, lam_init=lam_init)
            diff_k_out.append(kp.reshape(bp, lp, DIFF_HEADS, 2 * DIFF_HD))
            diff_v_out.append(vp.reshape(bp, lp, DIFF_HEADS, 2 * DIFF_HD))
            qs, ks, vs = _qkv_diff(*s_src, g, mods, w_qkv_diff, tables, layer=i, j=j, latent=True, **s_rows)
            os_ = _diff_latent_attention(qs, ks, vs, cdk, cdv, diff_lambda, sub_g,
                                         j=j, seq=ls, lc=lc, lam_init=lam_init)
            w_o = w_o_diff
        else:
            qp, ktp, vtp = _qkv_swa(*p_src, g, mods, w_qkv_swa, None, layer=i, j=j, latent=False,
                                    seq=lp, **p_rows)
            op = _swa_prompt_attention(qp, ktp, vtp, swa_sink[j], seq=lp)
            swa_k_out.append(_swa_cache_from_feature_major(ktp, lp))
            swa_v_out.append(_swa_cache_from_feature_major(vtp, lp))
            qs, kds, vds = _qkv_swa(*s_src, g, mods, w_qkv_swa, tables, layer=i, j=j, latent=True,
                                    seq=ls, **s_rows)
            os_ = _swa_latent_attention(qs, kds, vds, cskt, csvt, swa_sink[j], j=j, seq=ls)
            w_o = w_o_swa
        last = i == depth - 1
        out = _post_attn_ffn(op, os_, x_parts, w_o, w_gate, w_up, w_down, g, mods, layer=i, j=j,
                             tiles_per_request=ls // tm, split_out=last)
        x_parts = out if last else (out,)
    xp, xs = x_parts

    return (xp.reshape(bp, lp, d), xs.reshape(bs, ls, d),
            jnp.stack(diff_k_out, axis=1), jnp.stack(diff_v_out, axis=1),
            jnp.stack(swa_k_out, axis=1), jnp.stack(swa_v_out, axis=1))
```

```python
import functools
import math

import jax
import jax.numpy as jnp
from jax import lax
from jax.experimental import pallas as pl
from jax.experimental.pallas import tpu as pltpu

F32 = jnp.float32
BF16 = jnp.bfloat16

GRID_W = 64
N_MIXERS = 2
DIFF_HEADS = 8
DIFF_HD = 64
SWA_HEADS = 16
SWA_KV_HEADS = 4
SWA_GROUP = SWA_HEADS // SWA_KV_HEADS
SWA_HD = 64
ROT_DIM = 64
WINDOW = 128
ROPE_BASE = 10000.0
EPS = 1e-6
NEG_INF = -1e30

LANES = 128
SUBLANES = 8
ROW_TILE = 512
PROMPT_SCORE_BUFFERS = 8
VMEM_LIMIT = 48 * 1024 * 1024
FFN_VMEM_LIMIT = 58 * 1024 * 1024
WEIGHT_STAGE_ROWS = 128
WEIGHT_STAGE_SLOTS = 6
QKV_STAGE_SLOTS = 4
FFN_CHUNK = 256
NT_DIMS = (((1,), (1,)), ((), ()))


def _params(n_axes):
    return pltpu.CompilerParams(dimension_semantics=("arbitrary",) * n_axes,
                                vmem_limit_bytes=VMEM_LIMIT)


def _resident(shape):
    return pl.BlockSpec(shape, lambda *_: (0,) * len(shape), pipeline_mode=pl.Buffered(1))


def _layer_resident(shape, layer):
    return pl.BlockSpec((None,) + tuple(shape[1:]), lambda *_: (layer,) + (0,) * (len(shape) - 1),
                        pipeline_mode=pl.Buffered(1))


def _rms(x, g):
    ms = jnp.mean(x * x, axis=-1, keepdims=True)
    return (x * lax.rsqrt(ms + EPS)) * g


def _half_masks(dtype):
    lane = lax.broadcasted_iota(jnp.int32, (1, LANES), 1)
    lo = lane < (LANES // 2)
    return jnp.where(lo, 1.0, 0.0).astype(dtype), jnp.where(lo, 0.0, 1.0).astype(dtype)


def _lo64():
    return lax.broadcasted_iota(jnp.int32, (1, LANES), 1) < (LANES // 2)


def _ones_column(rows):
    return jnp.ones((rows, LANES), BF16)


def _ones_row(cols):
    return jnp.ones((LANES, cols), BF16)


def _mod_kernel(cond_ref, w_ref, b_ref, out_ref):
    c = cond_ref[...]
    s = c * jax.nn.sigmoid(c)
    out_ref[...] = jnp.dot(s.astype(BF16), w_ref[...].astype(BF16),
                           preferred_element_type=F32) + b_ref[...]


def _modulation(cond8, w_mod, b_mod):
    depth, d, n = w_mod.shape
    tn = 1536
    return pl.pallas_call(
        _mod_kernel,
        out_shape=jax.ShapeDtypeStruct((depth, 8, n), F32),
        grid=(depth, n // tn),
        in_specs=[pl.BlockSpec((8, d), lambda i, j: (0, 0)),
                  pl.BlockSpec((None, d, tn), lambda i, j: (i, 0, j)),
                  pl.BlockSpec((None, 1, tn), lambda i, j: (i, 0, j))],
        out_specs=pl.BlockSpec((None, 8, tn), lambda i, j: (i, 0, j)),
        compiler_params=_params(2),
        name="modulation",
    )(cond8, w_mod, b_mod.reshape(depth, 1, n))


def _mod_row(row0, tiles_per_batch):
    return row0 + pl.program_id(0) // tiles_per_batch


def _rope_slab(xs, cos, sin_signed, lo16):
    left = pltpu.roll(xs, LANES - 16, 1)
    right = pltpu.roll(xs, 16, 1)
    return xs * cos + jnp.where(lo16, left, right) * sin_signed


def _lo16_mask():
    lane = lax.broadcasted_iota(jnp.int32, (1, LANES), 1)
    return (lane % 32) < 16


def _stage_and_export_weight(w_hbm, wb_out, w_vmem, stage, sem, sem_out, n_steps):
    i = pl.program_id(0)
    export = pltpu.make_async_copy(w_vmem, wb_out, sem_out)

    @pl.when(i == 0)
    def _():
        _load_weights_as_bf16([(w_hbm, w_vmem)], {w_hbm.shape[1]: stage}, {w_hbm.shape[1]: sem})
        export.start()

    @pl.when(i == n_steps - 1)
    def _():
        export.wait()


def _pre_norm(x_ref, g_ref, mod_ref, r, d, slot):
    shift = mod_ref[pl.ds(r, 1), slot * d:(slot + 1) * d]
    scale = mod_ref[pl.ds(r, 1), (slot + 1) * d:(slot + 2) * d]
    return _rms(x_ref[...], g_ref[0] * (1 + scale)) + shift


def _qkv_diff_kernel(x_ref, g_ref, mod_ref, w_ref, *rest, latent, j, n_steps, row0, tiles_per_batch):
    d = x_ref.shape[1]
    if latent:
        cos_ref, sin_ref, q_ref, k_ref, v_ref = rest
        cos, sin = cos_ref[...], sin_ref[...]
        lo16 = _lo16_mask()
    else:
        q_ref, k_ref, v_ref, wb_out, w_vmem, stage, sem, sem_out = rest
        _stage_and_export_weight(w_ref.at[j], wb_out, w_vmem, stage, sem, sem_out, n_steps)
        w_ref = w_vmem
    r = _mod_row(row0, tiles_per_batch)
    h = _pre_norm(x_ref, g_ref, mod_ref, r, d, 0).astype(BF16)
    cw = 512
    for c in range(3 * d // cw):
        acc = jnp.dot(h, w_ref[:, c * cw:(c + 1) * cw], preferred_element_type=F32)
        which, off = divmod(c * cw, d)
        dst = (q_ref, k_ref, v_ref)[which]
        for s in range(cw // LANES):
            xs = acc[:, s * LANES:(s + 1) * LANES]
            if latent and which < 2:
                xs = _rope_slab(xs, cos, sin, lo16)
            if which == 0:
                xs = xs * (DIFF_HD ** -0.5)
            lo = off + s * LANES
            dst[:, lo:lo + LANES] = xs.astype(dst.dtype)


def _qkv_diff(x, n, tile0, g, mods, w, tables, *, layer, j, latent, row0, tiles_per_batch):
    d = x.shape[1]
    tm = ROW_TILE
    w_spec, w_out_shapes, w_out_specs, scratch = _qkv_weight_plumbing(w, j, latent)
    in_specs = [pl.BlockSpec((tm, d), lambda i: (i + tile0, 0)), _layer_resident(g.shape, layer),
                _layer_resident(mods.shape, layer), w_spec]
    args = [x, g, mods, w]
    kv_dtype = F32
    if latent:
        nt = tables[0].shape[0] // tm
        in_specs += [pl.BlockSpec((tm, LANES), lambda i: (i % nt, 0))] * 2
        args += list(tables)
        kv_dtype = BF16
    out_spec = pl.BlockSpec((tm, d), lambda i: (i, 0))
    return pl.pallas_call(
        functools.partial(_qkv_diff_kernel, latent=latent, j=j, n_steps=n // tm, row0=row0,
                          tiles_per_batch=tiles_per_batch),
        out_shape=(jax.ShapeDtypeStruct((n, d), BF16),
                   jax.ShapeDtypeStruct((n, d), kv_dtype),
                   jax.ShapeDtypeStruct((n, d), kv_dtype), *w_out_shapes),
        grid=(n // tm,),
        in_specs=in_specs,
        out_specs=(out_spec, out_spec, out_spec, *w_out_specs),
        scratch_shapes=scratch,
        compiler_params=_params(1),
        name="qkv_diff_latent" if latent else "qkv_diff_prompt",
    )(*args)


def _qkv_weight_plumbing(w, j, latent):
    if latent:
        return _resident(w.shape), [], [], []
    _, d, cols = w.shape
    hbm = pl.BlockSpec(memory_space=pl.ANY)
    scratch = [pltpu.VMEM((d, cols), BF16), pltpu.VMEM((QKV_STAGE_SLOTS, WEIGHT_STAGE_ROWS, cols), F32),
               pltpu.SemaphoreType.DMA((QKV_STAGE_SLOTS,)), pltpu.SemaphoreType.DMA(())]
    return hbm, [jax.ShapeDtypeStruct((d, cols), BF16)], [hbm], scratch


def _qkv_swa_kernel(x_ref, g_ref, mod_ref, w_ref, *rest, latent, j, n_steps, row0, tiles_per_batch, seq):
    d = x_ref.shape[1]
    nkv = SWA_KV_HEADS * SWA_HD
    if latent:
        cos_ref, sin_ref, q_ref, kd_ref, vd_ref = rest
        cos, sin = cos_ref[...], sin_ref[...]
        lo16 = _lo16_mask()
    else:
        q_ref, kt_ref, vt_ref, wb_out, w_vmem, stage, sem, sem_out = rest
        _stage_and_export_weight(w_ref.at[j], wb_out, w_vmem, stage, sem, sem_out, n_steps)
        w_ref = w_vmem
    lo64 = _lo64()
    r = _mod_row(row0, tiles_per_batch)
    h = _pre_norm(x_ref, g_ref, mod_ref, r, d, 0).astype(BF16)
    cw = 512
    for c in range(d // cw):
        acc = jnp.dot(h, w_ref[:, c * cw:(c + 1) * cw], preferred_element_type=F32)
        for s in range(cw // LANES):
            xs = acc[:, s * LANES:(s + 1) * LANES]
            if latent:
                xs = _rope_slab(xs, cos, sin, lo16)
            lo = c * cw + s * LANES
            q_ref[:, lo:lo + LANES] = (xs * (SWA_HD ** -0.5)).astype(BF16)
    kv = jnp.dot(h, w_ref[:, d:d + 2 * nkv], preferred_element_type=F32)
    if not latent:
        for b in range(x_ref.shape[0] // seq):
            kt_ref[b] = kv[b * seq:(b + 1) * seq, :nkv].T
            vt_ref[b] = kv[b * seq:(b + 1) * seq, nkv:].T
        return
    for which, dst in enumerate((kd_ref, vd_ref)):
        for s in range(nkv // LANES):
            xs = kv[:, which * nkv + s * LANES: which * nkv + (s + 1) * LANES]
            if which == 0:
                xs = _rope_slab(xs, cos, sin, lo16)
            sw = pltpu.roll(xs, LANES // 2, 1)
            dst[:, (2 * s) * LANES:(2 * s + 1) * LANES] = jnp.where(lo64, xs, sw).astype(BF16)
            dst[:, (2 * s + 1) * LANES:(2 * s + 2) * LANES] = jnp.where(lo64, sw, xs).astype(BF16)


def _qkv_swa(x, n, tile0, g, mods, w, tables, *, layer, j, latent, row0, tiles_per_batch, seq):
    d = x.shape[1]
    tm = ROW_TILE
    nkv = SWA_KV_HEADS * SWA_HD
    w_spec, w_out_shapes, w_out_specs, scratch = _qkv_weight_plumbing(w, j, latent)
    in_specs = [pl.BlockSpec((tm, d), lambda i: (i + tile0, 0)), _layer_resident(g.shape, layer),
                _layer_resident(mods.shape, layer), w_spec]
    args = [x, g, mods, w]
    out_shape = [jax.ShapeDtypeStruct((n, d), BF16)]
    out_specs = [pl.BlockSpec((tm, d), lambda i: (i, 0))]
    if latent:
        nt = tables[0].shape[0] // tm
        in_specs += [pl.BlockSpec((tm, LANES), lambda i: (i % nt, 0))] * 2
        args += list(tables)
        out_shape += [jax.ShapeDtypeStruct((n, 2 * nkv), BF16)] * 2
        out_specs += [pl.BlockSpec((tm, 2 * nkv), lambda i: (i, 0))] * 2
    else:
        out_shape += [jax.ShapeDtypeStruct((n // seq, nkv, seq), F32)] * 2
        out_specs += [pl.BlockSpec((tm // seq, nkv, seq), lambda i: (i, 0, 0))] * 2
    return pl.pallas_call(
        functools.partial(_qkv_swa_kernel, latent=latent, j=j, n_steps=n // tm, row0=row0,
                          tiles_per_batch=tiles_per_batch, seq=seq),
        out_shape=(*out_shape, *w_out_shapes),
        grid=(n // tm,),
        in_specs=in_specs,
        out_specs=(*out_specs, *w_out_specs),
        scratch_shapes=scratch,
        compiler_params=_params(1),
        name="qkv_swa_latent" if latent else "qkv_swa_prompt",
    )(*args)


def _diff_lambda(lam_ref, lam_init):
    lp = lam_ref[...]
    a = jnp.sum(lp[0:1] * lp[1:2], axis=-1, keepdims=True)
    b = jnp.sum(lp[2:3] * lp[3:4], axis=-1, keepdims=True)
    return jnp.exp(a) - jnp.exp(b) + lam_init


def _diff_combine(acc, tq, lam, g, lam_init):
    o12 = acc[:, :LANES] / acc[:, LANES:]
    o = o12[:tq] - lam * o12[tq:]
    return _rms(o, g) * (1.0 - lam_init)


def _stack_maps(q):
    m_lo, m_hi = _half_masks(BF16)
    return jnp.concatenate([q * m_lo, q * m_hi], axis=0)


def _run_pipelined(items, scores, finish, s_bufs):
    depth = len(s_bufs)
    states = {i: scores(items[i], s_bufs[i]) for i in range(min(depth - 1, len(items)))}
    for i, item in enumerate(items):
        ahead = i + depth - 1
        if ahead < len(items):
            states[ahead] = scores(items[ahead], s_bufs[ahead % depth])
        finish(item, s_bufs[i % depth], states.pop(i))


def _store_scores(s_ref, col0, s, mrun):
    s_ref[:, col0:col0 + s.shape[1]] = s
    for t in range(s.shape[1] // LANES):
        blk = s[:, t * LANES:(t + 1) * LANES]
        mrun = blk if mrun is None else jnp.maximum(mrun, blk)
    return mrun


def _exp_block(s_ref, col0, width, mb):
    return jnp.concatenate(
        [jnp.exp(s_ref[:, col0 + t * LANES:col0 + (t + 1) * LANES] - mb).astype(BF16)
         for t in range(width // LANES)], axis=1)


def _diff_prompt_kernel(q_ref, k_ref, v_ref, lam_ref, g_ref, o_ref, *s_bufs, lam_init, seq):
    lam = _diff_lambda(lam_ref, lam_init)
    g = g_ref[...]
    ones = _ones_column(seq)
    items = [(r, h) for r in range(q_ref.shape[0] // seq) for h in range(DIFF_HEADS)]

    def scores(item, s_ref):
        r, h = item
        rows, sl = slice(r * seq, (r + 1) * seq), slice(h * LANES, (h + 1) * LANES)
        s = lax.dot_general(_stack_maps(q_ref[rows, sl]), k_ref[rows, sl].astype(BF16), NT_DIMS,
                            preferred_element_type=F32)
        return _store_scores(s_ref, 0, s, None)

    def finish(item, s_ref, mrun):
        r, h = item
        rows, sl = slice(r * seq, (r + 1) * seq), slice(h * LANES, (h + 1) * LANES)
        mb = jnp.broadcast_to(jnp.max(mrun, axis=-1, keepdims=True), (2 * seq, LANES))
        vx = jnp.concatenate([v_ref[rows, sl].astype(BF16), ones], axis=1)
        acc = jnp.dot(_exp_block(s_ref, 0, seq, mb), vx, preferred_element_type=F32)
        o_ref[rows, sl] = _diff_combine(acc, seq, lam, g, lam_init).astype(BF16)

    _run_pipelined(items, scores, finish, s_bufs)


def _diff_prompt_attention(q, k, v, lam_params, subln_g, *, j, seq, lam_init):
    n, d = q.shape
    req = 2
    spec = pl.BlockSpec((req * seq, d), lambda b: (b, 0))
    return pl.pallas_call(
        functools.partial(_diff_prompt_kernel, lam_init=lam_init, seq=seq),
        out_shape=jax.ShapeDtypeStruct((n, d), BF16),
        grid=(n // (req * seq),),
        in_specs=[spec, spec, spec, _layer_resident(lam_params.shape, j), _layer_resident(subln_g.shape, j)],
        out_specs=spec,
        scratch_shapes=[pltpu.VMEM((2 * seq, seq), F32)] * PROMPT_SCORE_BUFFERS,
        compiler_params=_params(1),
        name="diff_attn_prompt",
    )(q, k, v, lam_params, subln_g)


def _diff_latent_kernel(q_ref, kc_ref, vc_ref, kl_ref, vl_ref, lam_ref, g_ref, o_ref,
                        kk_ref, vx_ref, s0_ref, s1_ref, *, lam_init, lc, tq, key_chunk):
    seq = q_ref.shape[0]
    heads = kk_ref.shape[0]
    nk = kk_ref.shape[1]
    for hh in range(heads):
        h = pl.program_id(1) * heads + hh
        sl = slice(hh * LANES, (hh + 1) * LANES)
        kk_ref[hh, 0:lc, :] = kc_ref[pl.ds(h, lc, stride=DIFF_HEADS), :].astype(BF16)
        kk_ref[hh, lc:, :] = kl_ref[:, sl]
        vx_ref[hh, 0:lc, 0:LANES] = vc_ref[pl.ds(h, lc, stride=DIFF_HEADS), :].astype(BF16)
        vx_ref[hh, lc:, 0:LANES] = vl_ref[:, sl]
        vx_ref[hh, :, LANES:2 * LANES] = _ones_column(nk)

    masks = _half_masks(BF16)
    nchunk = nk // key_chunk
    items = [(hh, rt, m) for hh in range(heads) for rt in range(seq // tq) for m in range(2)]
    lam = _diff_lambda(lam_ref, lam_init)
    g = g_ref[...]
    first_map = {}

    def scores(item, s_ref):
        hh, rt, m = item
        q = q_ref[rt * tq:(rt + 1) * tq, hh * LANES:(hh + 1) * LANES] * masks[m]
        mrun = None
        for c in range(nchunk):
            s = lax.dot_general(q, kk_ref[hh, c * key_chunk:(c + 1) * key_chunk, :], NT_DIMS,
                                preferred_element_type=F32)
            mrun = _store_scores(s_ref, c * key_chunk, s, mrun)
        return mrun

    def finish(item, s_ref, mrun):
        hh, rt, m = item
        mb = jnp.broadcast_to(jnp.max(mrun, axis=-1, keepdims=True), (tq, LANES))
        acc = None
        for c in range(nchunk):
            part = jnp.dot(_exp_block(s_ref, c * key_chunk, key_chunk, mb),
                           vx_ref[hh, c * key_chunk:(c + 1) * key_chunk, :], preferred_element_type=F32)
            acc = part if acc is None else acc + part
        o_m = acc[:, :LANES] / acc[:, LANES:]
        if m == 0:
            first_map[hh, rt] = o_m
        else:
            o = _rms(first_map.pop((hh, rt)) - lam * o_m, g) * (1.0 - lam_init)
            o_ref[rt * tq:(rt + 1) * tq, hh * LANES:(hh + 1) * LANES] = o.astype(BF16)

    _run_pipelined(items, scores, finish, (s0_ref, s1_ref))


def _diff_latent_attention(q, k, v, cache_k, cache_v, lam_params, subln_g, *, j, seq, lc, lam_init):
    n, d = q.shape
    nb = cache_k.shape[0]
    tq = 512
    key_chunk = 512
    heads = 1
    q_spec = pl.BlockSpec((seq, heads * LANES), lambda b, h: (b, h))
    c_spec = pl.BlockSpec((None, lc * DIFF_HEADS, LANES), lambda b, h: (b, j, 0))
    return pl.pallas_call(
        functools.partial(_diff_latent_kernel, lam_init=lam_init, lc=lc, tq=tq, key_chunk=key_chunk),
        out_shape=jax.ShapeDtypeStruct((n, d), BF16),
        grid=(nb, DIFF_HEADS // heads),
        in_specs=[q_spec, c_spec, c_spec, q_spec, q_spec,
                  _layer_resident(lam_params.shape, j), _layer_resident(subln_g.shape, j)],
        out_specs=q_spec,
        scratch_shapes=[pltpu.VMEM((heads, lc + seq, LANES), BF16),
                        pltpu.VMEM((heads, lc + seq, 2 * LANES), BF16),
                        pltpu.VMEM((tq, lc + seq), F32), pltpu.VMEM((tq, lc + seq), F32)],
        compiler_params=_params(2),
        name="diff_attn_latent",
    )(q, cache_k, cache_v, k, v, lam_params, subln_g)


def _stack_group(q_ref, rows, kv_local):
    m_lo, m_hi = _half_masks(BF16)
    parts = []
    for gb in range(SWA_GROUP // 2):
        blk = kv_local * (SWA_GROUP // 2) + gb
        qb = q_ref[rows, blk * LANES:(blk + 1) * LANES]
        parts += [qb * m_lo, qb * m_hi]
    return jnp.concatenate(parts, axis=0)


def _sink_column(sink_ref, first_head, tq):
    return jnp.concatenate([jnp.full((tq, LANES), sink_ref[first_head + g], F32) for g in range(SWA_GROUP)],
                           axis=0)


def _write_group(o_ref, rows, kv_local, o, tq):
    lo64 = _lo64()
    for gb in range(SWA_GROUP // 2):
        blk = kv_local * (SWA_GROUP // 2) + gb
        even = o[(2 * gb) * tq:(2 * gb + 1) * tq]
        odd = o[(2 * gb + 1) * tq:(2 * gb + 2) * tq]
        o_ref[rows, blk * LANES:(blk + 1) * LANES] = jnp.where(lo64, even, odd).astype(BF16)


def _dup_rows(x_t):
    xb = x_t.astype(BF16)
    return jnp.concatenate([xb, xb], axis=0)


def _sink_finish(mrun, sk, rows):
    mb = jnp.maximum(jnp.broadcast_to(jnp.max(mrun, axis=-1, keepdims=True), (rows, LANES)), sk)
    return mb, jnp.exp(sk - mb)


def _swa_prompt_kernel(sink_ref, q_ref, kt_ref, vt_ref, o_ref, *s_bufs):
    seq = kt_ref.shape[2]
    rows = SWA_GROUP * seq
    ones = _ones_row(seq)
    items = [(r, j) for r in range(kt_ref.shape[0]) for j in range(SWA_KV_HEADS)]

    def scores(item, s_ref):
        r, j = item
        kd = _dup_rows(kt_ref[r, j * SWA_HD:(j + 1) * SWA_HD, :])
        s = jnp.dot(_stack_group(q_ref, slice(r * seq, (r + 1) * seq), j), kd, preferred_element_type=F32)
        return _store_scores(s_ref, 0, s, None)

    def finish(item, s_ref, mrun):
        r, j = item
        sk = _sink_column(sink_ref, j * SWA_GROUP, seq)
        mb, sink_term = _sink_finish(mrun, sk, rows)
        vx = jnp.concatenate([_dup_rows(vt_ref[r, j * SWA_HD:(j + 1) * SWA_HD, :]), ones], axis=0)
        acc = lax.dot_general(_exp_block(s_ref, 0, seq, mb), vx, NT_DIMS, preferred_element_type=F32)
        o = acc[:, :LANES] / (acc[:, LANES:] + sink_term)
        _write_group(o_ref, slice(r * seq, (r + 1) * seq), j, o, seq)

    _run_pipelined(items, scores, finish, s_bufs)


def _swa_prompt_attention(q, kt, vt, sink, *, seq):
    n, d = q.shape
    nkv = kt.shape[1]
    req = 2
    t_spec = pl.BlockSpec((req, nkv, seq), lambda b: (b, 0, 0))
    return pl.pallas_call(
        _swa_prompt_kernel,
        out_shape=jax.ShapeDtypeStruct((n, d), BF16),
        grid=(n // (req * seq),),
        in_specs=[pl.BlockSpec(memory_space=pltpu.SMEM),
                  pl.BlockSpec((req * seq, d), lambda b: (b, 0)), t_spec, t_spec],
        out_specs=pl.BlockSpec((req * seq, d), lambda b: (b, 0)),
        scratch_shapes=[pltpu.VMEM((SWA_GROUP * seq, seq), F32)] * (PROMPT_SCORE_BUFFERS // 2),
        compiler_params=_params(1),
        name="swa_attn_prompt",
    )(sink, q, kt, vt)


def _swa_latent_kernel(sink_ref, q_ref, kc_ref, vc_ref, kl_ref, vl_ref, o_ref, kcd_ref, vcx_ref,
                       s0_ref, s1_ref, *, tq, span):
    pair = pl.program_id(1)
    tiles = q_ref.shape[0] // tq
    first_tile = pl.program_id(2) * tiles
    seq = kl_ref.shape[0]
    lc = kc_ref.shape[1]
    rows = SWA_GROUP * tq
    ones_row = _ones_row(lc)
    for jj in range(2):
        kcd_ref[jj] = _dup_rows(kc_ref[jj * SWA_HD:(jj + 1) * SWA_HD, :])
        vcx_ref[jj] = jnp.concatenate([_dup_rows(vc_ref[jj * SWA_HD:(jj + 1) * SWA_HD, :]), ones_row], axis=0)
    ones_col = _ones_column(span)
    items = [(t, jj) for t in range(tiles) for jj in range(2)]
    windows, biases = {}, {}

    def window(t):
        if t not in windows:
            q0 = (first_tile + t) * tq
            windows[t] = (q0, pl.multiple_of(jnp.clip(q0 - WINDOW, 0, seq - span), WINDOW))
        return windows[t]

    def bias_for(t):
        if t not in biases:
            q0, ws = window(t)
            q_pos = q0 + lax.broadcasted_iota(jnp.int32, (tq, span), 0)
            k_pos = ws + lax.broadcasted_iota(jnp.int32, (tq, span), 1)
            b = jnp.where(jnp.abs(q_pos - k_pos) <= WINDOW, 0.0, NEG_INF).astype(F32)
            biases[t] = jnp.concatenate([b] * SWA_GROUP, axis=0)
        return biases[t]

    def scores(item, s_ref):
        t, jj = item
        _, ws = window(t)
        qs = _stack_group(q_ref, slice(t * tq, (t + 1) * tq), jj)
        s_c = jnp.dot(qs, kcd_ref[jj], preferred_element_type=F32)
        mrun = _store_scores(s_ref, 0, s_c, None)
        s_w = lax.dot_general(qs, kl_ref[pl.ds(ws, span), jj * LANES:(jj + 1) * LANES], NT_DIMS,
                              preferred_element_type=F32) + bias_for(t)
        return _store_scores(s_ref, lc, s_w, mrun)

    def finish(item, s_ref, mrun):
        t, jj = item
        _, ws = window(t)
        sk = _sink_column(sink_ref, (2 * pair + jj) * SWA_GROUP, tq)
        mb, sink_term = _sink_finish(mrun, sk, rows)
        vwx = jnp.concatenate([vl_ref[pl.ds(ws, span), jj * LANES:(jj + 1) * LANES], ones_col], axis=1)
        acc = (lax.dot_general(_exp_block(s_ref, 0, lc, mb), vcx_ref[jj], NT_DIMS, preferred_element_type=F32)
               + jnp.dot(_exp_block(s_ref, lc, span, mb), vwx, preferred_element_type=F32))
        o = acc[:, :LANES] / (acc[:, LANES:] + sink_term)
        _write_group(o_ref, slice(t * tq, (t + 1) * tq), jj, o, tq)

    _run_pipelined(items, scores, finish, (s0_ref, s1_ref))


def _swa_latent_attention(q, kd, vd, cache_kt, cache_vt, sink, *, j, seq):
    n, d = q.shape
    nb, _, lc = cache_kt.shape
    tq = 256
    span = tq + 2 * WINDOW
    npair = SWA_KV_HEADS // 2
    wq = d // npair
    parts = 2
    q_spec = pl.BlockSpec((seq // parts, wq), lambda b, p, i: (b * parts + i, p))
    c_spec = pl.BlockSpec((None, 2 * SWA_HD, lc), lambda b, p, i: (b, j * npair + p, 0))
    l_spec = pl.BlockSpec((seq, 2 * LANES), lambda b, p, i: (b, p))
    s_shape = pltpu.VMEM((SWA_GROUP * tq, lc + span), F32)
    return pl.pallas_call(
        functools.partial(_swa_latent_kernel, tq=tq, span=span),
        out_shape=jax.ShapeDtypeStruct((n, d), BF16),
        grid=(nb, npair, parts),
        in_specs=[pl.BlockSpec(memory_space=pltpu.SMEM), q_spec, c_spec, c_spec, l_spec, l_spec],
        out_specs=q_spec,
        scratch_shapes=[pltpu.VMEM((2, 2 * SWA_HD, lc), BF16), pltpu.VMEM((2, 2 * LANES, lc), BF16),
                        s_shape, s_shape],
        compiler_params=_params(3),
        name="swa_attn_latent",
    )(sink, q, cache_kt, cache_vt, kd, vd)


def _load_weights_as_bf16(jobs, stages, sems):
    order = []
    rings = {w: [] for w in stages}
    for src, dst in jobs:
        w = src.shape[1]
        slots = stages[w].shape[0]
        for k in range(src.shape[0] // WEIGHT_STAGE_ROWS):
            rows = pl.ds(k * WEIGHT_STAGE_ROWS, WEIGHT_STAGE_ROWS)
            slot = len(rings[w]) % slots
            copy = pltpu.make_async_copy(src.at[rows, :], stages[w].at[slot], sems[w].at[slot])
            order.append((w, len(rings[w])))
            rings[w].append((copy, slot, dst, rows))
    for w, ring in rings.items():
        for copy, _, _, _ in ring[:stages[w].shape[0]]:
            copy.start()
    for w, k in order:
        copy, slot, dst, rows = rings[w][k]
        copy.wait()
        dst[rows, :] = stages[w][slot].astype(BF16)
        ahead = k + stages[w].shape[0]
        if ahead < len(rings[w]):
            rings[w][ahead][0].start()


def _post_attn_ffn_kernel(*refs, layer, j, n_prompt_tiles, tiles_per_request, split_x, split_out):
    refs = list(refs)
    op_ref, os_ref = refs[:2]
    x_refs = refs[2:4] if split_x else refs[2:3]
    wo_hbm, wg_hbm, wu_hbm, wd_hbm, g_ref, mod_ref = refs[2 + len(x_refs):8 + len(x_refs)]
    n_out = 2 if split_out else 1
    out_refs = refs[8 + len(x_refs):8 + len(x_refs) + n_out]
    wo_ref, wg_ref, wu_ref, wd_ref, stage_d, stage_ff, sem_d, sem_ff = refs[8 + len(x_refs) + n_out:]
    d = wo_ref.shape[1]
    dff = wg_ref.shape[1]
    i = pl.program_id(0)

    @pl.when(i == 0)
    def _():
        _load_weights_as_bf16(
            [(wo_hbm.at[j], wo_ref), (wg_hbm.at[layer], wg_ref), (wu_hbm.at[layer], wu_ref),
             (wd_hbm.at[layer], wd_ref)],
            {d: stage_d, dff: stage_ff}, {d: sem_d, dff: sem_ff})

    is_prompt = i < n_prompt_tiles
    r = jnp.where(is_prompt, 0, 1 + (i - n_prompt_tiles) // tiles_per_request)

    def mod(slot):
        return mod_ref[pl.ds(r, 1), slot * d:(slot + 1) * d]

    o = jnp.where(is_prompt, op_ref[...], os_ref[...])
    x = jnp.where(is_prompt, x_refs[0][...], x_refs[1][...]) if split_x else x_refs[0][...]
    y = jnp.dot(o, wo_ref[...], preferred_element_type=F32)
    x = x + _rms(y, mod(2) * g_ref[1])
    h = (_rms(x, g_ref[2] * (1 + mod(4))) + mod(3)).astype(BF16)
    y = jnp.zeros((h.shape[0], d), F32)
    lo = 0
    while lo < dff:
        hi = min(lo + FFN_CHUNK, dff)
        a = jnp.dot(h, wg_ref[:, lo:hi], preferred_element_type=F32)
        u = jnp.dot(h, wu_ref[:, lo:hi], preferred_element_type=F32)
        t = (a * jax.nn.sigmoid(a)) * u
        y = y + jnp.dot(t.astype(BF16), wd_ref[lo:hi, :], preferred_element_type=F32)
        lo = hi
    out = x + _rms(y, mod(5) * g_ref[3])
    if split_out:
        @pl.when(is_prompt)
        def _():
            out_refs[0][...] = out

        @pl.when(jnp.logical_not(is_prompt))
        def _():
            out_refs[1][...] = out
    else:
        out_refs[0][...] = out


def _post_attn_ffn(o_p, o_s, xs_in, w_o, wg, wu, wd, g, mods, *, layer, j, tiles_per_request, split_out):
    n_p, d = o_p.shape
    n_s = o_s.shape[0]
    tm = ROW_TILE
    tp, ts = n_p // tm, n_s // tm
    prompt_rows = pl.BlockSpec((tm, d), lambda i: (jnp.minimum(i, tp - 1), 0))
    latent_rows = pl.BlockSpec((tm, d), lambda i: (jnp.maximum(i - tp, 0), 0))
    all_rows = pl.BlockSpec((tm, d), lambda i: (i, 0))
    split_x = len(xs_in) == 2
    in_specs = [prompt_rows, latent_rows] + ([prompt_rows, latent_rows] if split_x else [all_rows])
    hbm = pl.BlockSpec(memory_space=pl.ANY)
    in_specs += [hbm, hbm, hbm, hbm, _layer_resident(g.shape, layer), _layer_resident(mods.shape, layer)]
    if split_out:
        out_shape = (jax.ShapeDtypeStruct((n_p, d), F32), jax.ShapeDtypeStruct((n_s, d), F32))
        out_specs = (prompt_rows, latent_rows)
    else:
        out_shape = jax.ShapeDtypeStruct((n_p + n_s, d), F32)
        out_specs = all_rows
    dff = wg.shape[2]
    scratch = [pltpu.VMEM((d, d), BF16), pltpu.VMEM((d, dff), BF16), pltpu.VMEM((d, dff), BF16),
               pltpu.VMEM((dff, d), BF16),
               pltpu.VMEM((WEIGHT_STAGE_SLOTS, WEIGHT_STAGE_ROWS, d), F32),
               pltpu.VMEM((WEIGHT_STAGE_SLOTS, WEIGHT_STAGE_ROWS, dff), F32),
               pltpu.SemaphoreType.DMA((WEIGHT_STAGE_SLOTS,)), pltpu.SemaphoreType.DMA((WEIGHT_STAGE_SLOTS,))]
    return pl.pallas_call(
        functools.partial(_post_attn_ffn_kernel, layer=layer, j=j, n_prompt_tiles=tp,
                          tiles_per_request=tiles_per_request, split_x=split_x, split_out=split_out),
        out_shape=out_shape,
        grid=(tp + ts,),
        in_specs=in_specs,
        out_specs=out_specs,
        scratch_shapes=scratch,
        compiler_params=pltpu.CompilerParams(dimension_semantics=("arbitrary",),
                                             vmem_limit_bytes=FFN_VMEM_LIMIT),
        name="post_attn_ffn",
    )(o_p, o_s, *xs_in, w_o, wg, wu, wd, g, mods)


def _rope_tables(n_lat):
    t = jnp.arange(n_lat)
    row = (t // GRID_W).astype(F32)
    col = (t % GRID_W).astype(F32)
    nf = ROT_DIM // 4
    inv = ROPE_BASE ** (-jnp.arange(nf, dtype=F32) / nf)
    ar = row[:, None] * inv[None, :]
    ac = col[:, None] * inv[None, :]
    ang = jnp.concatenate([ar, ar, ac, ac], axis=-1)
    cos, sin = jnp.cos(ang), jnp.sin(ang)
    sign = jnp.where((jnp.arange(ROT_DIM) % 32) < 16, -1.0, 1.0).astype(F32)
    reps = LANES // ROT_DIM
    return jnp.tile(cos, (1, reps)), jnp.tile(sin * sign, (1, reps))


def _swa_cache_to_feature_major(cache):
    nb, nl, lc, nh, hd = cache.shape
    return cache.transpose(0, 1, 3, 4, 2).reshape(nb, nl * nh * hd, lc)


def _swa_cache_from_feature_major(xt, seq):
    nb = xt.shape[0]
    return xt.reshape(nb, SWA_KV_HEADS, SWA_HD, seq).transpose(0, 3, 1, 2)


def kernel(x_prompt, x_sample, cache_diff_k, cache_diff_v, cache_swa_k, cache_swa_v, c, c_ctx,
           w_mod, b_mod, norm_g, w_qkv_diff, diff_lambda, diff_subln_g, w_o_diff,
           w_qkv_swa, swa_sink, w_o_swa, w_gate, w_up, w_down):
    bp, lp, d = x_prompt.shape
    bs, ls, _ = x_sample.shape
    lc = cache_diff_k.shape[2]
    depth = w_mod.shape[0]
    tm = ROW_TILE

    cond8 = jnp.concatenate([c_ctx[None, :], c, jnp.zeros((8 - 1 - bs, d), F32)], axis=0)
    mods = _modulation(cond8, w_mod, b_mod)
    tables = _rope_tables(ls)

    cdk = cache_diff_k.reshape(bs, -1, 2 * DIFF_HD)
    cdv = cache_diff_v.reshape(bs, -1, 2 * DIFF_HD)
    cskt = _swa_cache_to_feature_major(cache_swa_k)
    csvt = _swa_cache_to_feature_major(cache_swa_v)

    n_p, n_s = bp * lp, bs * ls
    x_parts = (x_prompt.reshape(n_p, d), x_sample.reshape(n_s, d))
    g = norm_g.reshape(depth, 4, 1, d)
    sub_g = diff_subln_g.reshape(-1, 1, 2 * DIFF_HD)
    p_rows = dict(row0=0, tiles_per_batch=n_p // tm)
    s_rows = dict(row0=1, tiles_per_batch=ls // tm)
    diff_k_out, diff_v_out, swa_k_out, swa_v_out = [], [], [], []

    for i in range(depth):
        j = i // N_MIXERS
        if len(x_parts) == 2:
            p_src, s_src = (x_parts[0], n_p, 0), (x_parts[1], n_s, 0)
        else:
            p_src, s_src = (x_parts[0], n_p, 0), (x_parts[0], n_s, n_p // tm)
        if i % N_MIXERS == 0:
            lam_init = 0.8 - 0.6 * math.exp(-0.3 * i)
            qp, kp, vp, wb = _qkv_diff(*p_src, g, mods, w_qkv_diff, None, layer=i, j=j, latent=False,
                                       **p_rows)
            op = _diff_prompt_attention(qp, kp, vp, diff_lambda, sub_g, j=j, seq=lp, lam_init=lam_init)
            diff_k_out.append(kp.reshape(bp, lp, DIFF_HEADS, 2 * DIFF_HD))
            diff_v_out.append(vp.reshape(bp, lp, DIFF_HEADS, 2 * DIFF_HD))
            qs, ks, vs = _qkv_diff(*s_src, g, mods, wb, tables, layer=i, j=j, latent=True, **s_rows)
            os_ = _diff_latent_attention(qs, ks, vs, cdk, cdv, diff_lambda, sub_g,
                                         j=j, seq=ls, lc=lc, lam_init=lam_init)
            w_o = w_o_diff
        else:
            qp, ktp, vtp, wb = _qkv_swa(*p_src, g, mods, w_qkv_swa, None, layer=i, j=j, latent=False,
                                        seq=lp, **p_rows)
            op = _swa_prompt_attention(qp, ktp, vtp, swa_sink[j], seq=lp)
            swa_k_out.append(_swa_cache_from_feature_major(ktp, lp))
            swa_v_out.append(_swa_cache_from_feature_major(vtp, lp))
            qs, kds, vds = _qkv_swa(*s_src, g, mods, wb, tables, layer=i, j=j, latent=True,
                                    seq=ls, **s_rows)
            os_ = _swa_latent_attention(qs, kds, vds, cskt, csvt, swa_sink[j], j=j, seq=ls)
            w_o = w_o_swa
        last = i == depth - 1
        out = _post_attn_ffn(op, os_, x_parts, w_o, w_gate, w_up, w_down, g, mods, layer=i, j=j,
                             tiles_per_request=ls // tm, split_out=last)
        x_parts = out if last else (out,)
    xp, xs = x_parts

    return (xp.reshape(bp, lp, d), xs.reshape(bs, ls, d),
            jnp.stack(diff_k_out, axis=1), jnp.stack(diff_v_out, axis=1),
            jnp.stack(swa_k_out, axis=1), jnp.stack(swa_v_out, axis=1))
```

```python
import functools
import math

import jax
import jax.numpy as jnp
from jax import lax
from jax.experimental import pallas as pl
from jax.experimental.pallas import tpu as pltpu

F32 = jnp.float32
BF16 = jnp.bfloat16

GRID_W = 64
N_MIXERS = 2
DIFF_HEADS = 8
DIFF_HD = 64
SWA_HEADS = 16
SWA_KV_HEADS = 4
SWA_GROUP = SWA_HEADS // SWA_KV_HEADS
SWA_HD = 64
ROT_DIM = 64
WINDOW = 128
ROPE_BASE = 10000.0
EPS = 1e-6
NEG_INF = -1e30

LANES = 128
SUBLANES = 8
ROW_TILE = 512
PROMPT_SCORE_BUFFERS = 8
VMEM_LIMIT = 48 * 1024 * 1024
FFN_VMEM_LIMIT = 58 * 1024 * 1024
WEIGHT_STAGE_ROWS = 128
WEIGHT_STAGE_SLOTS = 6
QKV_STAGE_SLOTS = 4
FFN_CHUNK = 256
NT_DIMS = (((1,), (1,)), ((), ()))


def _params(n_axes):
    return pltpu.CompilerParams(dimension_semantics=("arbitrary",) * n_axes,
                                vmem_limit_bytes=VMEM_LIMIT)


def _resident(shape):
    return pl.BlockSpec(shape, lambda *_: (0,) * len(shape), pipeline_mode=pl.Buffered(1))


def _layer_resident(shape, layer):
    return pl.BlockSpec((None,) + tuple(shape[1:]), lambda *_: (layer,) + (0,) * (len(shape) - 1),
                        pipeline_mode=pl.Buffered(1))


def _rms(x, g):
    ms = jnp.mean(x * x, axis=-1, keepdims=True)
    return (x * lax.rsqrt(ms + EPS)) * g


def _half_masks(dtype):
    lane = lax.broadcasted_iota(jnp.int32, (1, LANES), 1)
    lo = lane < (LANES // 2)
    return jnp.where(lo, 1.0, 0.0).astype(dtype), jnp.where(lo, 0.0, 1.0).astype(dtype)


def _lo64():
    return lax.broadcasted_iota(jnp.int32, (1, LANES), 1) < (LANES // 2)


def _ones_column(rows):
    return jnp.ones((rows, LANES), BF16)


def _ones_row(cols):
    return jnp.ones((LANES, cols), BF16)


def _mod_kernel(cond_ref, w_ref, b_ref, out_ref):
    c = cond_ref[...]
    s = c * jax.nn.sigmoid(c)
    out_ref[...] = jnp.dot(s.astype(BF16), w_ref[...].astype(BF16),
                           preferred_element_type=F32) + b_ref[...]


def _modulation(cond8, w_mod, b_mod):
    depth, d, n = w_mod.shape
    tn = 1536
    return pl.pallas_call(
        _mod_kernel,
        out_shape=jax.ShapeDtypeStruct((depth, 8, n), F32),
        grid=(depth, n // tn),
        in_specs=[pl.BlockSpec((8, d), lambda i, j: (0, 0)),
                  pl.BlockSpec((None, d, tn), lambda i, j: (i, 0, j)),
                  pl.BlockSpec((None, 1, tn), lambda i, j: (i, 0, j))],
        out_specs=pl.BlockSpec((None, 8, tn), lambda i, j: (i, 0, j)),
        compiler_params=_params(2),
        name="modulation",
    )(cond8, w_mod, b_mod.reshape(depth, 1, n))


def _mod_row(row0, tiles_per_batch):
    return row0 + pl.program_id(0) // tiles_per_batch


def _rope_slab(xs, cos, sin_signed, lo16):
    left = pltpu.roll(xs, LANES - 16, 1)
    right = pltpu.roll(xs, 16, 1)
    return xs * cos + jnp.where(lo16, left, right) * sin_signed


def _lo16_mask():
    lane = lax.broadcasted_iota(jnp.int32, (1, LANES), 1)
    return (lane % 32) < 16


def _stage_and_export_weight(w_hbm, wb_out, w_vmem, stage, sem, sem_out, n_steps):
    i = pl.program_id(0)
    export = pltpu.make_async_copy(w_vmem, wb_out, sem_out)

    @pl.when(i == 0)
    def _():
        _load_weights_as_bf16([(w_hbm, w_vmem)], {w_hbm.shape[1]: stage}, {w_hbm.shape[1]: sem})
        export.start()

    @pl.when(i == n_steps - 1)
    def _():
        export.wait()


def _pre_norm(x_ref, g_ref, mod_ref, r, d, slot):
    shift = mod_ref[pl.ds(r, 1), slot * d:(slot + 1) * d]
    scale = mod_ref[pl.ds(r, 1), (slot + 1) * d:(slot + 2) * d]
    return _rms(x_ref[...], g_ref[0] * (1 + scale)) + shift


def _qkv_diff_kernel(x_ref, g_ref, mod_ref, w_ref, *rest, latent, j, n_steps, row0, tiles_per_batch):
    d = x_ref.shape[1]
    if latent:
        cos_ref, sin_ref, q_ref, k_ref, v_ref = rest
        cos, sin = cos_ref[...], sin_ref[...]
        lo16 = _lo16_mask()
    else:
        q_ref, k_ref, v_ref, wb_out, w_vmem, stage, sem, sem_out = rest
        _stage_and_export_weight(w_ref.at[j], wb_out, w_vmem, stage, sem, sem_out, n_steps)
        w_ref = w_vmem
    r = _mod_row(row0, tiles_per_batch)
    h = _pre_norm(x_ref, g_ref, mod_ref, r, d, 0).astype(BF16)
    cw = 512
    for c in range(3 * d // cw):
        acc = jnp.dot(h, w_ref[:, c * cw:(c + 1) * cw], preferred_element_type=F32)
        which, off = divmod(c * cw, d)
        dst = (q_ref, k_ref, v_ref)[which]
        for s in range(cw // LANES):
            xs = acc[:, s * LANES:(s + 1) * LANES]
            if latent and which < 2:
                xs = _rope_slab(xs, cos, sin, lo16)
            if which == 0:
                xs = xs * (DIFF_HD ** -0.5)
            lo = off + s * LANES
            if latent or which == 0:
                dst[:, lo:lo + LANES] = xs.astype(dst.dtype)
            else:
                dst[pl.ds(lo // LANES, x_ref.shape[0], stride=DIFF_HEADS), :] = xs


def _qkv_diff(x, n, tile0, g, mods, w, tables, *, layer, j, latent, row0, tiles_per_batch):
    d = x.shape[1]
    tm = ROW_TILE
    w_spec, w_out_shapes, w_out_specs, scratch = _qkv_weight_plumbing(w, j, latent)
    in_specs = [pl.BlockSpec((tm, d), lambda i: (i + tile0, 0)), _layer_resident(g.shape, layer),
                _layer_resident(mods.shape, layer), w_spec]
    args = [x, g, mods, w]
    kv_dtype = F32
    if latent:
        nt = tables[0].shape[0] // tm
        in_specs += [pl.BlockSpec((tm, LANES), lambda i: (i % nt, 0))] * 2
        args += list(tables)
        kv_dtype = BF16
    out_spec = pl.BlockSpec((tm, d), lambda i: (i, 0))
    kv_shape, kv_spec = jax.ShapeDtypeStruct((n, d), kv_dtype), out_spec
    if not latent:
        kv_shape = jax.ShapeDtypeStruct((n * DIFF_HEADS, d // DIFF_HEADS), kv_dtype)
        kv_spec = pl.BlockSpec((tm * DIFF_HEADS, d // DIFF_HEADS), lambda i: (i, 0))
    return pl.pallas_call(
        functools.partial(_qkv_diff_kernel, latent=latent, j=j, n_steps=n // tm, row0=row0,
                          tiles_per_batch=tiles_per_batch),
        out_shape=(jax.ShapeDtypeStruct((n, d), BF16), kv_shape, kv_shape, *w_out_shapes),
        grid=(n // tm,),
        in_specs=in_specs,
        out_specs=(out_spec, kv_spec, kv_spec, *w_out_specs),
        scratch_shapes=scratch,
        compiler_params=_params(1),
        name="qkv_diff_latent" if latent else "qkv_diff_prompt",
    )(*args)


def _qkv_weight_plumbing(w, j, latent):
    if latent:
        return _resident(w.shape), [], [], []
    _, d, cols = w.shape
    hbm = pl.BlockSpec(memory_space=pl.ANY)
    scratch = [pltpu.VMEM((d, cols), BF16), pltpu.VMEM((QKV_STAGE_SLOTS, WEIGHT_STAGE_ROWS, cols), F32),
               pltpu.SemaphoreType.DMA((QKV_STAGE_SLOTS,)), pltpu.SemaphoreType.DMA(())]
    return hbm, [jax.ShapeDtypeStruct((d, cols), BF16)], [hbm], scratch


def _qkv_swa_kernel(x_ref, g_ref, mod_ref, w_ref, *rest, latent, j, n_steps, row0, tiles_per_batch, seq):
    d = x_ref.shape[1]
    nkv = SWA_KV_HEADS * SWA_HD
    if latent:
        cos_ref, sin_ref, q_ref, kd_ref, vd_ref = rest
        cos, sin = cos_ref[...], sin_ref[...]
        lo16 = _lo16_mask()
    else:
        q_ref, kt_ref, vt_ref, wb_out, w_vmem, stage, sem, sem_out = rest
        _stage_and_export_weight(w_ref.at[j], wb_out, w_vmem, stage, sem, sem_out, n_steps)
        w_ref = w_vmem
    lo64 = _lo64()
    r = _mod_row(row0, tiles_per_batch)
    h = _pre_norm(x_ref, g_ref, mod_ref, r, d, 0).astype(BF16)
    cw = 512
    for c in range(d // cw):
        acc = jnp.dot(h, w_ref[:, c * cw:(c + 1) * cw], preferred_element_type=F32)
        for s in range(cw // LANES):
            xs = acc[:, s * LANES:(s + 1) * LANES]
            if latent:
                xs = _rope_slab(xs, cos, sin, lo16)
            lo = c * cw + s * LANES
            q_ref[:, lo:lo + LANES] = (xs * (SWA_HD ** -0.5)).astype(BF16)
    kv = jnp.dot(h, w_ref[:, d:d + 2 * nkv], preferred_element_type=F32)
    if not latent:
        for b in range(x_ref.shape[0] // seq):
            kt_ref[b] = kv[b * seq:(b + 1) * seq, :nkv].T
            vt_ref[b] = kv[b * seq:(b + 1) * seq, nkv:].T
        return
    for which, dst in enumerate((kd_ref, vd_ref)):
        for s in range(nkv // LANES):
            xs = kv[:, which * nkv + s * LANES: which * nkv + (s + 1) * LANES]
            if which == 0:
                xs = _rope_slab(xs, cos, sin, lo16)
            sw = pltpu.roll(xs, LANES // 2, 1)
            dst[:, (2 * s) * LANES:(2 * s + 1) * LANES] = jnp.where(lo64, xs, sw).astype(BF16)
            dst[:, (2 * s + 1) * LANES:(2 * s + 2) * LANES] = jnp.where(lo64, sw, xs).astype(BF16)


def _qkv_swa(x, n, tile0, g, mods, w, tables, *, layer, j, latent, row0, tiles_per_batch, seq):
    d = x.shape[1]
    tm = ROW_TILE
    nkv = SWA_KV_HEADS * SWA_HD
    w_spec, w_out_shapes, w_out_specs, scratch = _qkv_weight_plumbing(w, j, latent)
    in_specs = [pl.BlockSpec((tm, d), lambda i: (i + tile0, 0)), _layer_resident(g.shape, layer),
                _layer_resident(mods.shape, layer), w_spec]
    args = [x, g, mods, w]
    out_shape = [jax.ShapeDtypeStruct((n, d), BF16)]
    out_specs = [pl.BlockSpec((tm, d), lambda i: (i, 0))]
    if latent:
        nt = tables[0].shape[0] // tm
        in_specs += [pl.BlockSpec((tm, LANES), lambda i: (i % nt, 0))] * 2
        args += list(tables)
        out_shape += [jax.ShapeDtypeStruct((n, 2 * nkv), BF16)] * 2
        out_specs += [pl.BlockSpec((tm, 2 * nkv), lambda i: (i, 0))] * 2
    else:
        out_shape += [jax.ShapeDtypeStruct((n // seq, nkv, seq), F32)] * 2
        out_specs += [pl.BlockSpec((tm // seq, nkv, seq), lambda i: (i, 0, 0))] * 2
    return pl.pallas_call(
        functools.partial(_qkv_swa_kernel, latent=latent, j=j, n_steps=n // tm, row0=row0,
                          tiles_per_batch=tiles_per_batch, seq=seq),
        out_shape=(*out_shape, *w_out_shapes),
        grid=(n // tm,),
        in_specs=in_specs,
        out_specs=(*out_specs, *w_out_specs),
        scratch_shapes=scratch,
        compiler_params=_params(1),
        name="qkv_swa_latent" if latent else "qkv_swa_prompt",
    )(*args)


def _diff_lambda(lam_ref, lam_init):
    lp = lam_ref[...]
    a = jnp.sum(lp[0:1] * lp[1:2], axis=-1, keepdims=True)
    b = jnp.sum(lp[2:3] * lp[3:4], axis=-1, keepdims=True)
    return jnp.exp(a) - jnp.exp(b) + lam_init


def _diff_combine(acc, tq, lam, g, lam_init):
    o12 = acc[:, :LANES] / acc[:, LANES:]
    o = o12[:tq] - lam * o12[tq:]
    return _rms(o, g) * (1.0 - lam_init)


def _stack_maps(q):
    m_lo, m_hi = _half_masks(BF16)
    return jnp.concatenate([q * m_lo, q * m_hi], axis=0)


def _run_pipelined(items, scores, finish, s_bufs):
    depth = len(s_bufs)
    states = {i: scores(items[i], s_bufs[i]) for i in range(min(depth - 1, len(items)))}
    for i, item in enumerate(items):
        ahead = i + depth - 1
        if ahead < len(items):
            states[ahead] = scores(items[ahead], s_bufs[ahead % depth])
        finish(item, s_bufs[i % depth], states.pop(i))


def _store_scores(s_ref, col0, s, mrun):
    s_ref[:, col0:col0 + s.shape[1]] = s
    for t in range(s.shape[1] // LANES):
        blk = s[:, t * LANES:(t + 1) * LANES]
        mrun = blk if mrun is None else jnp.maximum(mrun, blk)
    return mrun


def _exp_block(s_ref, col0, width, mb):
    return jnp.concatenate(
        [jnp.exp(s_ref[:, col0 + t * LANES:col0 + (t + 1) * LANES] - mb).astype(BF16)
         for t in range(width // LANES)], axis=1)


def _diff_prompt_kernel(q_ref, k_ref, v_ref, lam_ref, g_ref, o_ref, *s_bufs, lam_init, seq):
    lam = _diff_lambda(lam_ref, lam_init)
    g = g_ref[...]
    ones = _ones_column(seq)
    items = [(r, h) for r in range(q_ref.shape[0] // seq) for h in range(DIFF_HEADS)]

    def head_rows(ref, r, h):
        return ref[pl.ds(r * seq * DIFF_HEADS + h, seq, stride=DIFF_HEADS), :]

    def scores(item, s_ref):
        r, h = item
        rows, sl = slice(r * seq, (r + 1) * seq), slice(h * LANES, (h + 1) * LANES)
        s = lax.dot_general(_stack_maps(q_ref[rows, sl]), head_rows(k_ref, r, h).astype(BF16), NT_DIMS,
                            preferred_element_type=F32)
        return _store_scores(s_ref, 0, s, None)

    def finish(item, s_ref, mrun):
        r, h = item
        rows, sl = slice(r * seq, (r + 1) * seq), slice(h * LANES, (h + 1) * LANES)
        mb = jnp.broadcast_to(jnp.max(mrun, axis=-1, keepdims=True), (2 * seq, LANES))
        vx = jnp.concatenate([head_rows(v_ref, r, h).astype(BF16), ones], axis=1)
        acc = jnp.dot(_exp_block(s_ref, 0, seq, mb), vx, preferred_element_type=F32)
        o_ref[rows, sl] = _diff_combine(acc, seq, lam, g, lam_init).astype(BF16)

    _run_pipelined(items, scores, finish, s_bufs)


def _diff_prompt_attention(q, k, v, lam_params, subln_g, *, j, seq, lam_init):
    n, d = q.shape
    req = 2
    spec = pl.BlockSpec((req * seq, d), lambda b: (b, 0))
    kv_spec = pl.BlockSpec((req * seq * DIFF_HEADS, d // DIFF_HEADS), lambda b: (b, 0))
    return pl.pallas_call(
        functools.partial(_diff_prompt_kernel, lam_init=lam_init, seq=seq),
        out_shape=jax.ShapeDtypeStruct((n, d), BF16),
        grid=(n // (req * seq),),
        in_specs=[spec, kv_spec, kv_spec, _layer_resident(lam_params.shape, j),
                  _layer_resident(subln_g.shape, j)],
        out_specs=spec,
        scratch_shapes=[pltpu.VMEM((2 * seq, seq), F32)] * PROMPT_SCORE_BUFFERS,
        compiler_params=_params(1),
        name="diff_attn_prompt",
    )(q, k, v, lam_params, subln_g)


def _diff_latent_kernel(q_ref, kc_ref, vc_ref, kl_ref, vl_ref, lam_ref, g_ref, o_ref,
                        kk_ref, vx_ref, s0_ref, s1_ref, *, lam_init, lc, tq, key_chunk):
    seq = q_ref.shape[0]
    heads = kk_ref.shape[0]
    nk = kk_ref.shape[1]
    for hh in range(heads):
        h = pl.program_id(1) * heads + hh
        sl = slice(hh * LANES, (hh + 1) * LANES)
        kk_ref[hh, 0:lc, :] = kc_ref[pl.ds(h, lc, stride=DIFF_HEADS), :].astype(BF16)
        kk_ref[hh, lc:, :] = kl_ref[:, sl]
        vx_ref[hh, 0:lc, 0:LANES] = vc_ref[pl.ds(h, lc, stride=DIFF_HEADS), :].astype(BF16)
        vx_ref[hh, lc:, 0:LANES] = vl_ref[:, sl]
        vx_ref[hh, :, LANES:2 * LANES] = _ones_column(nk)

    masks = _half_masks(BF16)
    nchunk = nk // key_chunk
    items = [(hh, rt, m) for hh in range(heads) for rt in range(seq // tq) for m in range(2)]
    lam = _diff_lambda(lam_ref, lam_init)
    g = g_ref[...]
    first_map = {}

    def scores(item, s_ref):
        hh, rt, m = item
        q = q_ref[rt * tq:(rt + 1) * tq, hh * LANES:(hh + 1) * LANES] * masks[m]
        mrun = None
        for c in range(nchunk):
            s = lax.dot_general(q, kk_ref[hh, c * key_chunk:(c + 1) * key_chunk, :], NT_DIMS,
                                preferred_element_type=F32)
            mrun = _store_scores(s_ref, c * key_chunk, s, mrun)
        return mrun

    def finish(item, s_ref, mrun):
        hh, rt, m = item
        mb = jnp.broadcast_to(jnp.max(mrun, axis=-1, keepdims=True), (tq, LANES))
        acc = None
        for c in range(nchunk):
            part = jnp.dot(_exp_block(s_ref, c * key_chunk, key_chunk, mb),
                           vx_ref[hh, c * key_chunk:(c + 1) * key_chunk, :], preferred_element_type=F32)
            acc = part if acc is None else acc + part
        o_m = acc[:, :LANES] / acc[:, LANES:]
        if m == 0:
            first_map[hh, rt] = o_m
        else:
            o = _rms(first_map.pop((hh, rt)) - lam * o_m, g) * (1.0 - lam_init)
            o_ref[rt * tq:(rt + 1) * tq, hh * LANES:(hh + 1) * LANES] = o.astype(BF16)

    _run_pipelined(items, scores, finish, (s0_ref, s1_ref))


def _diff_latent_attention(q, k, v, cache_k, cache_v, lam_params, subln_g, *, j, seq, lc, lam_init):
    n, d = q.shape
    nb = cache_k.shape[0]
    tq = 512
    key_chunk = 512
    heads = 1
    q_spec = pl.BlockSpec((seq, heads * LANES), lambda b, h: (b, h))
    c_spec = pl.BlockSpec((None, lc * DIFF_HEADS, LANES), lambda b, h: (b, j, 0))
    return pl.pallas_call(
        functools.partial(_diff_latent_kernel, lam_init=lam_init, lc=lc, tq=tq, key_chunk=key_chunk),
        out_shape=jax.ShapeDtypeStruct((n, d), BF16),
        grid=(nb, DIFF_HEADS // heads),
        in_specs=[q_spec, c_spec, c_spec, q_spec, q_spec,
                  _layer_resident(lam_params.shape, j), _layer_resident(subln_g.shape, j)],
        out_specs=q_spec,
        scratch_shapes=[pltpu.VMEM((heads, lc + seq, LANES), BF16),
                        pltpu.VMEM((heads, lc + seq, 2 * LANES), BF16),
                        pltpu.VMEM((tq, lc + seq), F32), pltpu.VMEM((tq, lc + seq), F32)],
        compiler_params=_params(2),
        name="diff_attn_latent",
    )(q, cache_k, cache_v, k, v, lam_params, subln_g)


def _stack_group(q_ref, rows, kv_local):
    m_lo, m_hi = _half_masks(BF16)
    parts = []
    for gb in range(SWA_GROUP // 2):
        blk = kv_local * (SWA_GROUP // 2) + gb
        qb = q_ref[rows, blk * LANES:(blk + 1) * LANES]
        parts += [qb * m_lo, qb * m_hi]
    return jnp.concatenate(parts, axis=0)


def _sink_column(sink_ref, first_head, tq):
    return jnp.concatenate([jnp.full((tq, LANES), sink_ref[first_head + g], F32) for g in range(SWA_GROUP)],
                           axis=0)


def _write_group(o_ref, rows, kv_local, o, tq):
    lo64 = _lo64()
    for gb in range(SWA_GROUP // 2):
        blk = kv_local * (SWA_GROUP // 2) + gb
        even = o[(2 * gb) * tq:(2 * gb + 1) * tq]
        odd = o[(2 * gb + 1) * tq:(2 * gb + 2) * tq]
        o_ref[rows, blk * LANES:(blk + 1) * LANES] = jnp.where(lo64, even, odd).astype(BF16)


def _dup_rows(x_t):
    xb = x_t.astype(BF16)
    return jnp.concatenate([xb, xb], axis=0)


def _sink_finish(mrun, sk, rows):
    mb = jnp.maximum(jnp.broadcast_to(jnp.max(mrun, axis=-1, keepdims=True), (rows, LANES)), sk)
    return mb, jnp.exp(sk - mb)


def _swa_prompt_kernel(sink_ref, q_ref, kt_ref, vt_ref, o_ref, *s_bufs):
    seq = kt_ref.shape[2]
    rows = SWA_GROUP * seq
    ones = _ones_row(seq)
    items = [(r, j) for r in range(kt_ref.shape[0]) for j in range(SWA_KV_HEADS)]

    def scores(item, s_ref):
        r, j = item
        kd = _dup_rows(kt_ref[r, j * SWA_HD:(j + 1) * SWA_HD, :])
        s = jnp.dot(_stack_group(q_ref, slice(r * seq, (r + 1) * seq), j), kd, preferred_element_type=F32)
        return _store_scores(s_ref, 0, s, None)

    def finish(item, s_ref, mrun):
        r, j = item
        sk = _sink_column(sink_ref, j * SWA_GROUP, seq)
        mb, sink_term = _sink_finish(mrun, sk, rows)
        vx = jnp.concatenate([_dup_rows(vt_ref[r, j * SWA_HD:(j + 1) * SWA_HD, :]), ones], axis=0)
        acc = lax.dot_general(_exp_block(s_ref, 0, seq, mb), vx, NT_DIMS, preferred_element_type=F32)
        o = acc[:, :LANES] / (acc[:, LANES:] + sink_term)
        _write_group(o_ref, slice(r * seq, (r + 1) * seq), j, o, seq)

    _run_pipelined(items, scores, finish, s_bufs)


def _swa_prompt_attention(q, kt, vt, sink, *, seq):
    n, d = q.shape
    nkv = kt.shape[1]
    req = 2
    t_spec = pl.BlockSpec((req, nkv, seq), lambda b: (b, 0, 0))
    return pl.pallas_call(
        _swa_prompt_kernel,
        out_shape=jax.ShapeDtypeStruct((n, d), BF16),
        grid=(n // (req * seq),),
        in_specs=[pl.BlockSpec(memory_space=pltpu.SMEM),
                  pl.BlockSpec((req * seq, d), lambda b: (b, 0)), t_spec, t_spec],
        out_specs=pl.BlockSpec((req * seq, d), lambda b: (b, 0)),
        scratch_shapes=[pltpu.VMEM((SWA_GROUP * seq, seq), F32)] * (PROMPT_SCORE_BUFFERS // 2),
        compiler_params=_params(1),
        name="swa_attn_prompt",
    )(sink, q, kt, vt)


def _swa_latent_kernel(sink_ref, q_ref, kc_ref, vc_ref, kl_ref, vl_ref, o_ref, kcd_ref, vcx_ref,
                       s0_ref, s1_ref, *, tq, span):
    pair = pl.program_id(1)
    tiles = q_ref.shape[0] // tq
    first_tile = pl.program_id(2) * tiles
    seq = kl_ref.shape[0]
    lc = kc_ref.shape[1]
    rows = SWA_GROUP * tq
    ones_row = _ones_row(lc)
    for jj in range(2):
        kcd_ref[jj] = _dup_rows(kc_ref[jj * SWA_HD:(jj + 1) * SWA_HD, :])
        vcx_ref[jj] = jnp.concatenate([_dup_rows(vc_ref[jj * SWA_HD:(jj + 1) * SWA_HD, :]), ones_row], axis=0)
    ones_col = _ones_column(span)
    items = [(t, jj) for t in range(tiles) for jj in range(2)]
    windows, biases = {}, {}

    def window(t):
        if t not in windows:
            q0 = (first_tile + t) * tq
            windows[t] = (q0, pl.multiple_of(jnp.clip(q0 - WINDOW, 0, seq - span), WINDOW))
        return windows[t]

    def bias_for(t):
        if t not in biases:
            q0, ws = window(t)
            q_pos = q0 + lax.broadcasted_iota(jnp.int32, (tq, span), 0)
            k_pos = ws + lax.broadcasted_iota(jnp.int32, (tq, span), 1)
            b = jnp.where(jnp.abs(q_pos - k_pos) <= WINDOW, 0.0, NEG_INF).astype(F32)
            biases[t] = jnp.concatenate([b] * SWA_GROUP, axis=0)
        return biases[t]

    def scores(item, s_ref):
        t, jj = item
        _, ws = window(t)
        qs = _stack_group(q_ref, slice(t * tq, (t + 1) * tq), jj)
        s_c = jnp.dot(qs, kcd_ref[jj], preferred_element_type=F32)
        mrun = _store_scores(s_ref, 0, s_c, None)
        s_w = lax.dot_general(qs, kl_ref[pl.ds(ws, span), jj * LANES:(jj + 1) * LANES], NT_DIMS,
                              preferred_element_type=F32) + bias_for(t)
        return _store_scores(s_ref, lc, s_w, mrun)

    def finish(item, s_ref, mrun):
        t, jj = item
        _, ws = window(t)
        sk = _sink_column(sink_ref, (2 * pair + jj) * SWA_GROUP, tq)
        mb, sink_term = _sink_finish(mrun, sk, rows)
        vwx = jnp.concatenate([vl_ref[pl.ds(ws, span), jj * LANES:(jj + 1) * LANES], ones_col], axis=1)
        acc = (lax.dot_general(_exp_block(s_ref, 0, lc, mb), vcx_ref[jj], NT_DIMS, preferred_element_type=F32)
               + jnp.dot(_exp_block(s_ref, lc, span, mb), vwx, preferred_element_type=F32))
        o = acc[:, :LANES] / (acc[:, LANES:] + sink_term)
        _write_group(o_ref, slice(t * tq, (t + 1) * tq), jj, o, tq)

    _run_pipelined(items, scores, finish, (s0_ref, s1_ref))


def _swa_latent_attention(q, kd, vd, cache_kt, cache_vt, sink, *, j, seq):
    n, d = q.shape
    nb, _, lc = cache_kt.shape
    tq = 256
    span = tq + 2 * WINDOW
    npair = SWA_KV_HEADS // 2
    wq = d // npair
    parts = 2
    q_spec = pl.BlockSpec((seq // parts, wq), lambda b, p, i: (b * parts + i, p))
    c_spec = pl.BlockSpec((None, 2 * SWA_HD, lc), lambda b, p, i: (b, j * npair + p, 0))
    l_spec = pl.BlockSpec((seq, 2 * LANES), lambda b, p, i: (b, p))
    s_shape = pltpu.VMEM((SWA_GROUP * tq, lc + span), F32)
    return pl.pallas_call(
        functools.partial(_swa_latent_kernel, tq=tq, span=span),
        out_shape=jax.ShapeDtypeStruct((n, d), BF16),
        grid=(nb, npair, parts),
        in_specs=[pl.BlockSpec(memory_space=pltpu.SMEM), q_spec, c_spec, c_spec, l_spec, l_spec],
        out_specs=q_spec,
        scratch_shapes=[pltpu.VMEM((2, 2 * SWA_HD, lc), BF16), pltpu.VMEM((2, 2 * LANES, lc), BF16),
                        s_shape, s_shape],
        compiler_params=_params(3),
        name="swa_attn_latent",
    )(sink, q, cache_kt, cache_vt, kd, vd)


def _load_weights_as_bf16(jobs, stages, sems):
    order = []
    rings = {w: [] for w in stages}
    for src, dst in jobs:
        w = src.shape[1]
        slots = stages[w].shape[0]
        for k in range(src.shape[0] // WEIGHT_STAGE_ROWS):
            rows = pl.ds(k * WEIGHT_STAGE_ROWS, WEIGHT_STAGE_ROWS)
            slot = len(rings[w]) % slots
            copy = pltpu.make_async_copy(src.at[rows, :], stages[w].at[slot], sems[w].at[slot])
            order.append((w, len(rings[w])))
            rings[w].append((copy, slot, dst, rows))
    for w, ring in rings.items():
        for copy, _, _, _ in ring[:stages[w].shape[0]]:
            copy.start()
    for w, k in order:
        copy, slot, dst, rows = rings[w][k]
        copy.wait()
        dst[rows, :] = stages[w][slot].astype(BF16)
        ahead = k + stages[w].shape[0]
        if ahead < len(rings[w]):
            rings[w][ahead][0].start()


def _post_attn_ffn_kernel(*refs, layer, j, n_prompt_tiles, tiles_per_request, split_x, split_out):
    refs = list(refs)
    op_ref, os_ref = refs[:2]
    x_refs = refs[2:4] if split_x else refs[2:3]
    wo_hbm, wg_hbm, wu_hbm, wd_hbm, g_ref, mod_ref = refs[2 + len(x_refs):8 + len(x_refs)]
    n_out = 2 if split_out else 1
    out_refs = refs[8 + len(x_refs):8 + len(x_refs) + n_out]
    wo_ref, wg_ref, wu_ref, wd_ref, stage_d, stage_ff, sem_d, sem_ff = refs[8 + len(x_refs) + n_out:]
    d = wo_ref.shape[1]
    dff = wg_ref.shape[1]
    i = pl.program_id(0)

    @pl.when(i == 0)
    def _():
        _load_weights_as_bf16(
            [(wo_hbm.at[j], wo_ref), (wg_hbm.at[layer], wg_ref), (wu_hbm.at[layer], wu_ref),
             (wd_hbm.at[layer], wd_ref)],
            {d: stage_d, dff: stage_ff}, {d: sem_d, dff: sem_ff})

    is_prompt = i < n_prompt_tiles
    r = jnp.where(is_prompt, 0, 1 + (i - n_prompt_tiles) // tiles_per_request)

    def mod(slot):
        return mod_ref[pl.ds(r, 1), slot * d:(slot + 1) * d]

    o = jnp.where(is_prompt, op_ref[...], os_ref[...])
    x = jnp.where(is_prompt, x_refs[0][...], x_refs[1][...]) if split_x else x_refs[0][...]
    y = jnp.dot(o, wo_ref[...], preferred_element_type=F32)
    x = x + _rms(y, mod(2) * g_ref[1])
    h = (_rms(x, g_ref[2] * (1 + mod(4))) + mod(3)).astype(BF16)
    y = jnp.zeros((h.shape[0], d), F32)
    lo = 0
    while lo < dff:
        hi = min(lo + FFN_CHUNK, dff)
        a = jnp.dot(h, wg_ref[:, lo:hi], preferred_element_type=F32)
        u = jnp.dot(h, wu_ref[:, lo:hi], preferred_element_type=F32)
        t = (a * jax.nn.sigmoid(a)) * u
        y = y + jnp.dot(t.astype(BF16), wd_ref[lo:hi, :], preferred_element_type=F32)
        lo = hi
    out = x + _rms(y, mod(5) * g_ref[3])
    if split_out:
        @pl.when(is_prompt)
        def _():
            out_refs[0][...] = out

        @pl.when(jnp.logical_not(is_prompt))
        def _():
            out_refs[1][...] = out
    else:
        out_refs[0][...] = out


def _post_attn_ffn(o_p, o_s, xs_in, w_o, wg, wu, wd, g, mods, *, layer, j, tiles_per_request, split_out):
    n_p, d = o_p.shape
    n_s = o_s.shape[0]
    tm = ROW_TILE
    tp, ts = n_p // tm, n_s // tm
    prompt_rows = pl.BlockSpec((tm, d), lambda i: (jnp.minimum(i, tp - 1), 0))
    latent_rows = pl.BlockSpec((tm, d), lambda i: (jnp.maximum(i - tp, 0), 0))
    all_rows = pl.BlockSpec((tm, d), lambda i: (i, 0))
    split_x = len(xs_in) == 2
    in_specs = [prompt_rows, latent_rows] + ([prompt_rows, latent_rows] if split_x else [all_rows])
    hbm = pl.BlockSpec(memory_space=pl.ANY)
    in_specs += [hbm, hbm, hbm, hbm, _layer_resident(g.shape, layer), _layer_resident(mods.shape, layer)]
    if split_out:
        out_shape = (jax.ShapeDtypeStruct((n_p, d), F32), jax.ShapeDtypeStruct((n_s, d), F32))
        out_specs = (prompt_rows, latent_rows)
    else:
        out_shape = jax.ShapeDtypeStruct((n_p + n_s, d), F32)
        out_specs = all_rows
    dff = wg.shape[2]
    scratch = [pltpu.VMEM((d, d), BF16), pltpu.VMEM((d, dff), BF16), pltpu.VMEM((d, dff), BF16),
               pltpu.VMEM((dff, d), BF16),
               pltpu.VMEM((WEIGHT_STAGE_SLOTS, WEIGHT_STAGE_ROWS, d), F32),
               pltpu.VMEM((WEIGHT_STAGE_SLOTS, WEIGHT_STAGE_ROWS, dff), F32),
               pltpu.SemaphoreType.DMA((WEIGHT_STAGE_SLOTS,)), pltpu.SemaphoreType.DMA((WEIGHT_STAGE_SLOTS,))]
    return pl.pallas_call(
        functools.partial(_post_attn_ffn_kernel, layer=layer, j=j, n_prompt_tiles=tp,
                          tiles_per_request=tiles_per_request, split_x=split_x, split_out=split_out),
        out_shape=out_shape,
        grid=(tp + ts,),
        in_specs=in_specs,
        out_specs=out_specs,
        scratch_shapes=scratch,
        compiler_params=pltpu.CompilerParams(dimension_semantics=("arbitrary",),
                                             vmem_limit_bytes=FFN_VMEM_LIMIT),
        name="post_attn_ffn",
    )(o_p, o_s, *xs_in, w_o, wg, wu, wd, g, mods)


def _rope_tables(n_lat):
    t = jnp.arange(n_lat)
    row = (t // GRID_W).astype(F32)
    col = (t % GRID_W).astype(F32)
    nf = ROT_DIM // 4
    inv = ROPE_BASE ** (-jnp.arange(nf, dtype=F32) / nf)
    ar = row[:, None] * inv[None, :]
    ac = col[:, None] * inv[None, :]
    ang = jnp.concatenate([ar, ar, ac, ac], axis=-1)
    cos, sin = jnp.cos(ang), jnp.sin(ang)
    sign = jnp.where((jnp.arange(ROT_DIM) % 32) < 16, -1.0, 1.0).astype(F32)
    reps = LANES // ROT_DIM
    return jnp.tile(cos, (1, reps)), jnp.tile(sin * sign, (1, reps))


def _swa_cache_to_feature_major(cache):
    nb, nl, lc, nh, hd = cache.shape
    return cache.transpose(0, 1, 3, 4, 2).reshape(nb, nl * nh * hd, lc)


def _swa_cache_from_feature_major(xt, seq):
    nb = xt.shape[0]
    return xt.reshape(nb, SWA_KV_HEADS, SWA_HD, seq).transpose(0, 3, 1, 2)


def kernel(x_prompt, x_sample, cache_diff_k, cache_diff_v, cache_swa_k, cache_swa_v, c, c_ctx,
           w_mod, b_mod, norm_g, w_qkv_diff, diff_lambda, diff_subln_g, w_o_diff,
           w_qkv_swa, swa_sink, w_o_swa, w_gate, w_up, w_down):
    bp, lp, d = x_prompt.shape
    bs, ls, _ = x_sample.shape
    lc = cache_diff_k.shape[2]
    depth = w_mod.shape[0]
    tm = ROW_TILE

    cond8 = jnp.concatenate([c_ctx[None, :], c, jnp.zeros((8 - 1 - bs, d), F32)], axis=0)
    mods = _modulation(cond8, w_mod, b_mod)
    tables = _rope_tables(ls)

    cdk = cache_diff_k.reshape(bs, -1, 2 * DIFF_HD)
    cdv = cache_diff_v.reshape(bs, -1, 2 * DIFF_HD)
    cskt = _swa_cache_to_feature_major(cache_swa_k)
    csvt = _swa_cache_to_feature_major(cache_swa_v)

    n_p, n_s = bp * lp, bs * ls
    x_parts = (x_prompt.reshape(n_p, d), x_sample.reshape(n_s, d))
    g = norm_g.reshape(depth, 4, 1, d)
    sub_g = diff_subln_g.reshape(-1, 1, 2 * DIFF_HD)
    p_rows = dict(row0=0, tiles_per_batch=n_p // tm)
    s_rows = dict(row0=1, tiles_per_batch=ls // tm)
    diff_k_out, diff_v_out, swa_k_out, swa_v_out = [], [], [], []

    for i in range(depth):
        j = i // N_MIXERS
        if len(x_parts) == 2:
            p_src, s_src = (x_parts[0], n_p, 0), (x_parts[1], n_s, 0)
        else:
            p_src, s_src = (x_parts[0], n_p, 0), (x_parts[0], n_s, n_p // tm)
        if i % N_MIXERS == 0:
            lam_init = 0.8 - 0.6 * math.exp(-0.3 * i)
            qp, kp, vp, wb = _qkv_diff(*p_src, g, mods, w_qkv_diff, None, layer=i, j=j, latent=False,
                                       **p_rows)
            op = _diff_prompt_attention(qp, kp, vp, diff_lambda, sub_g, j=j, seq=lp, lam_init=lam_init)
            diff_k_out.append(kp.reshape(bp, lp, DIFF_HEADS, 2 * DIFF_HD))
            diff_v_out.append(vp.reshape(bp, lp, DIFF_HEADS, 2 * DIFF_HD))
            qs, ks, vs = _qkv_diff(*s_src, g, mods, wb, tables, layer=i, j=j, latent=True, **s_rows)
            os_ = _diff_latent_attention(qs, ks, vs, cdk, cdv, diff_lambda, sub_g,
                                         j=j, seq=ls, lc=lc, lam_init=lam_init)
            w_o = w_o_diff
        else:
            qp, ktp, vtp, wb = _qkv_swa(*p_src, g, mods, w_qkv_swa, None, layer=i, j=j, latent=False,
                                        seq=lp, **p_rows)
            op = _swa_prompt_attention(qp, ktp, vtp, swa_sink[j], seq=lp)
            swa_k_out.append(_swa_cache_from_feature_major(ktp, lp))
            swa_v_out.append(_swa_cache_from_feature_major(vtp, lp))
            qs, kds, vds = _qkv_swa(*s_src, g, mods, wb, tables, layer=i, j=j, latent=True,
                                    seq=ls, **s_rows)
            os_ = _swa_latent_attention(qs, kds, vds, cskt, csvt, swa_sink[j], j=j, seq=ls)
            w_o = w_o_swa
        last = i == depth - 1
        out = _post_attn_ffn(op, os_, x_parts, w_o, w_gate, w_up, w_down, g, mods, layer=i, j=j,
                             tiles_per_request=ls // tm, split_out=last)
        x_parts = out if last else (out,)
    xp, xs = x_parts

    return (xp.reshape(bp, lp, d), xs.reshape(bs, ls, d),
            jnp.stack(diff_k_out, axis=1), jnp.stack(diff_v_out, axis=1),
            jnp.stack(swa_k_out, axis=1), jnp.stack(swa_v_out, axis=1))
```

```python
import functools
import math

import jax
import jax.numpy as jnp
from jax import lax
from jax.experimental import pallas as pl
from jax.experimental.pallas import tpu as pltpu

F32 = jnp.float32
BF16 = jnp.bfloat16

GRID_W = 64
N_MIXERS = 2
DIFF_HEADS = 8
DIFF_HD = 64
SWA_HEADS = 16
SWA_KV_HEADS = 4
SWA_GROUP = SWA_HEADS // SWA_KV_HEADS
SWA_HD = 64
ROT_DIM = 64
WINDOW = 128
ROPE_BASE = 10000.0
EPS = 1e-6
NEG_INF = -1e30

LANES = 128
SUBLANES = 8
ROW_TILE = 512
PROMPT_SCORE_BUFFERS = 8
VMEM_LIMIT = 48 * 1024 * 1024
FFN_VMEM_LIMIT = 58 * 1024 * 1024
WEIGHT_STAGE_ROWS = 128
WEIGHT_STAGE_SLOTS = 6
QKV_STAGE_SLOTS = 4
FFN_CHUNK = 256
NT_DIMS = (((1,), (1,)), ((), ()))
LOG2E = math.log2(math.e)


def _params(n_axes):
    return pltpu.CompilerParams(dimension_semantics=("arbitrary",) * n_axes,
                                vmem_limit_bytes=VMEM_LIMIT)


def _resident(shape):
    return pl.BlockSpec(shape, lambda *_: (0,) * len(shape), pipeline_mode=pl.Buffered(1))


def _layer_resident(shape, layer):
    return pl.BlockSpec((None,) + tuple(shape[1:]), lambda *_: (layer,) + (0,) * (len(shape) - 1),
                        pipeline_mode=pl.Buffered(1))


def _rms(x, g):
    ms = jnp.mean(x * x, axis=-1, keepdims=True)
    return (x * lax.rsqrt(ms + EPS)) * g


def _half_masks(dtype):
    lane = lax.broadcasted_iota(jnp.int32, (1, LANES), 1)
    lo = lane < (LANES // 2)
    return jnp.where(lo, 1.0, 0.0).astype(dtype), jnp.where(lo, 0.0, 1.0).astype(dtype)


def _lo64():
    return lax.broadcasted_iota(jnp.int32, (1, LANES), 1) < (LANES // 2)


def _ones_column(rows):
    return jnp.ones((rows, LANES), BF16)


def _ones_row(cols):
    return jnp.ones((LANES, cols), BF16)


def _mod_kernel(cond_ref, w_ref, b_ref, out_ref):
    c = cond_ref[...]
    s = c * jax.nn.sigmoid(c)
    out_ref[...] = jnp.dot(s.astype(BF16), w_ref[...].astype(BF16),
                           preferred_element_type=F32) + b_ref[...]


def _modulation(cond8, w_mod, b_mod):
    depth, d, n = w_mod.shape
    tn = 1536
    return pl.pallas_call(
        _mod_kernel,
        out_shape=jax.ShapeDtypeStruct((depth, 8, n), F32),
        grid=(depth, n // tn),
        in_specs=[pl.BlockSpec((8, d), lambda i, j: (0, 0)),
                  pl.BlockSpec((None, d, tn), lambda i, j: (i, 0, j)),
                  pl.BlockSpec((None, 1, tn), lambda i, j: (i, 0, j))],
        out_specs=pl.BlockSpec((None, 8, tn), lambda i, j: (i, 0, j)),
        compiler_params=_params(2),
        name="modulation",
    )(cond8, w_mod, b_mod.reshape(depth, 1, n))


def _mod_row(row0, tiles_per_batch):
    return row0 + pl.program_id(0) // tiles_per_batch


def _rope_slab(xs, cos, sin_signed, lo16):
    left = pltpu.roll(xs, LANES - 16, 1)
    right = pltpu.roll(xs, 16, 1)
    return xs * cos + jnp.where(lo16, left, right) * sin_signed


def _lo16_mask():
    lane = lax.broadcasted_iota(jnp.int32, (1, LANES), 1)
    return (lane % 32) < 16


def _stage_and_export_weight(w_hbm, wb_out, w_vmem, stage, sem, sem_out, n_steps):
    i = pl.program_id(0)
    export = pltpu.make_async_copy(w_vmem, wb_out, sem_out)

    @pl.when(i == 0)
    def _():
        _load_weights_as_bf16([(w_hbm, w_vmem)], {w_hbm.shape[1]: stage}, {w_hbm.shape[1]: sem})
        export.start()

    @pl.when(i == n_steps - 1)
    def _():
        export.wait()


def _pre_norm(x_ref, g_ref, mod_ref, r, d, slot):
    shift = mod_ref[pl.ds(r, 1), slot * d:(slot + 1) * d]
    scale = mod_ref[pl.ds(r, 1), (slot + 1) * d:(slot + 2) * d]
    return _rms(x_ref[...], g_ref[0] * (1 + scale)) + shift


def _qkv_diff_kernel(x_ref, g_ref, mod_ref, w_ref, *rest, latent, j, n_steps, row0, tiles_per_batch):
    d = x_ref.shape[1]
    if latent:
        cos_ref, sin_ref, q_ref, k_ref, v_ref = rest
        cos, sin = cos_ref[...], sin_ref[...]
        lo16 = _lo16_mask()
    else:
        q_ref, k_ref, v_ref, wb_out, w_vmem, stage, sem, sem_out = rest
        _stage_and_export_weight(w_ref.at[j], wb_out, w_vmem, stage, sem, sem_out, n_steps)
        w_ref = w_vmem
    r = _mod_row(row0, tiles_per_batch)
    h = _pre_norm(x_ref, g_ref, mod_ref, r, d, 0).astype(BF16)
    cw = 512
    for c in range(3 * d // cw):
        acc = jnp.dot(h, w_ref[:, c * cw:(c + 1) * cw], preferred_element_type=F32)
        which, off = divmod(c * cw, d)
        dst = (q_ref, k_ref, v_ref)[which]
        for s in range(cw // LANES):
            xs = acc[:, s * LANES:(s + 1) * LANES]
            if latent and which < 2:
                xs = _rope_slab(xs, cos, sin, lo16)
            if which == 0:
                xs = xs * (DIFF_HD ** -0.5 * LOG2E)
            lo = off + s * LANES
            if latent or which == 0:
                dst[:, lo:lo + LANES] = xs.astype(dst.dtype)
            else:
                dst[pl.ds(lo // LANES, x_ref.shape[0], stride=DIFF_HEADS), :] = xs


def _qkv_diff(x, n, tile0, g, mods, w, tables, *, layer, j, latent, row0, tiles_per_batch):
    d = x.shape[1]
    tm = ROW_TILE
    w_spec, w_out_shapes, w_out_specs, scratch = _qkv_weight_plumbing(w, j, latent)
    in_specs = [pl.BlockSpec((tm, d), lambda i: (i + tile0, 0)), _layer_resident(g.shape, layer),
                _layer_resident(mods.shape, layer), w_spec]
    args = [x, g, mods, w]
    kv_dtype = F32
    if latent:
        nt = tables[0].shape[0] // tm
        in_specs += [pl.BlockSpec((tm, LANES), lambda i: (i % nt, 0))] * 2
        args += list(tables)
        kv_dtype = BF16
    out_spec = pl.BlockSpec((tm, d), lambda i: (i, 0))
    kv_shape, kv_spec = jax.ShapeDtypeStruct((n, d), kv_dtype), out_spec
    if not latent:
        kv_shape = jax.ShapeDtypeStruct((n * DIFF_HEADS, d // DIFF_HEADS), kv_dtype)
        kv_spec = pl.BlockSpec((tm * DIFF_HEADS, d // DIFF_HEADS), lambda i: (i, 0))
    return pl.pallas_call(
        functools.partial(_qkv_diff_kernel, latent=latent, j=j, n_steps=n // tm, row0=row0,
                          tiles_per_batch=tiles_per_batch),
        out_shape=(jax.ShapeDtypeStruct((n, d), BF16), kv_shape, kv_shape, *w_out_shapes),
        grid=(n // tm,),
        in_specs=in_specs,
        out_specs=(out_spec, kv_spec, kv_spec, *w_out_specs),
        scratch_shapes=scratch,
        compiler_params=_params(1),
        name="qkv_diff_latent" if latent else "qkv_diff_prompt",
    )(*args)


def _qkv_weight_plumbing(w, j, latent):
    if latent:
        return _resident(w.shape), [], [], []
    _, d, cols = w.shape
    hbm = pl.BlockSpec(memory_space=pl.ANY)
    scratch = [pltpu.VMEM((d, cols), BF16), pltpu.VMEM((QKV_STAGE_SLOTS, WEIGHT_STAGE_ROWS, cols), F32),
               pltpu.SemaphoreType.DMA((QKV_STAGE_SLOTS,)), pltpu.SemaphoreType.DMA(())]
    return hbm, [jax.ShapeDtypeStruct((d, cols), BF16)], [hbm], scratch


def _qkv_swa_kernel(x_ref, g_ref, mod_ref, w_ref, *rest, latent, j, n_steps, row0, tiles_per_batch, seq):
    d = x_ref.shape[1]
    nkv = SWA_KV_HEADS * SWA_HD
    if latent:
        cos_ref, sin_ref, q_ref, kd_ref, vd_ref = rest
        cos, sin = cos_ref[...], sin_ref[...]
        lo16 = _lo16_mask()
    else:
        q_ref, kt_ref, vt_ref, wb_out, w_vmem, stage, sem, sem_out = rest
        _stage_and_export_weight(w_ref.at[j], wb_out, w_vmem, stage, sem, sem_out, n_steps)
        w_ref = w_vmem
    lo64 = _lo64()
    r = _mod_row(row0, tiles_per_batch)
    h = _pre_norm(x_ref, g_ref, mod_ref, r, d, 0).astype(BF16)
    cw = 512
    for c in range(d // cw):
        acc = jnp.dot(h, w_ref[:, c * cw:(c + 1) * cw], preferred_element_type=F32)
        for s in range(cw // LANES):
            xs = acc[:, s * LANES:(s + 1) * LANES]
            if latent:
                xs = _rope_slab(xs, cos, sin, lo16)
            lo = c * cw + s * LANES
            q_ref[:, lo:lo + LANES] = (xs * (SWA_HD ** -0.5 * LOG2E)).astype(BF16)
    kv = jnp.dot(h, w_ref[:, d:d + 2 * nkv], preferred_element_type=F32)
    if not latent:
        for b in range(x_ref.shape[0] // seq):
            kt_ref[b] = kv[b * seq:(b + 1) * seq, :nkv].T
            vt_ref[b] = kv[b * seq:(b + 1) * seq, nkv:].T
        return
    for which, dst in enumerate((kd_ref, vd_ref)):
        for s in range(nkv // LANES):
            xs = kv[:, which * nkv + s * LANES: which * nkv + (s + 1) * LANES]
            if which == 0:
                xs = _rope_slab(xs, cos, sin, lo16)
            sw = pltpu.roll(xs, LANES // 2, 1)
            dst[:, (2 * s) * LANES:(2 * s + 1) * LANES] = jnp.where(lo64, xs, sw).astype(BF16)
            dst[:, (2 * s + 1) * LANES:(2 * s + 2) * LANES] = jnp.where(lo64, sw, xs).astype(BF16)


def _qkv_swa(x, n, tile0, g, mods, w, tables, *, layer, j, latent, row0, tiles_per_batch, seq):
    d = x.shape[1]
    tm = ROW_TILE
    nkv = SWA_KV_HEADS * SWA_HD
    w_spec, w_out_shapes, w_out_specs, scratch = _qkv_weight_plumbing(w, j, latent)
    in_specs = [pl.BlockSpec((tm, d), lambda i: (i + tile0, 0)), _layer_resident(g.shape, layer),
                _layer_resident(mods.shape, layer), w_spec]
    args = [x, g, mods, w]
    out_shape = [jax.ShapeDtypeStruct((n, d), BF16)]
    out_specs = [pl.BlockSpec((tm, d), lambda i: (i, 0))]
    if latent:
        nt = tables[0].shape[0] // tm
        in_specs += [pl.BlockSpec((tm, LANES), lambda i: (i % nt, 0))] * 2
        args += list(tables)
        out_shape += [jax.ShapeDtypeStruct((n, 2 * nkv), BF16)] * 2
        out_specs += [pl.BlockSpec((tm, 2 * nkv), lambda i: (i, 0))] * 2
    else:
        out_shape += [jax.ShapeDtypeStruct((n // seq, nkv, seq), F32)] * 2
        out_specs += [pl.BlockSpec((tm // seq, nkv, seq), lambda i: (i, 0, 0))] * 2
    return pl.pallas_call(
        functools.partial(_qkv_swa_kernel, latent=latent, j=j, n_steps=n // tm, row0=row0,
                          tiles_per_batch=tiles_per_batch, seq=seq),
        out_shape=(*out_shape, *w_out_shapes),
        grid=(n // tm,),
        in_specs=in_specs,
        out_specs=(*out_specs, *w_out_specs),
        scratch_shapes=scratch,
        compiler_params=_params(1),
        name="qkv_swa_latent" if latent else "qkv_swa_prompt",
    )(*args)


def _diff_lambda(lam_ref, lam_init):
    lp = lam_ref[...]
    a = jnp.sum(lp[0:1] * lp[1:2], axis=-1, keepdims=True)
    b = jnp.sum(lp[2:3] * lp[3:4], axis=-1, keepdims=True)
    return jnp.exp(a) - jnp.exp(b) + lam_init


def _diff_combine(acc, tq, lam, g, lam_init):
    o12 = acc[:, :LANES] / acc[:, LANES:]
    o = o12[:tq] - lam * o12[tq:]
    return _rms(o, g) * (1.0 - lam_init)


def _stack_maps(q):
    m_lo, m_hi = _half_masks(BF16)
    return jnp.concatenate([q * m_lo, q * m_hi], axis=0)


def _run_pipelined(items, scores, finish, s_bufs):
    depth = len(s_bufs)
    states = {i: scores(items[i], s_bufs[i]) for i in range(min(depth - 1, len(items)))}
    for i, item in enumerate(items):
        ahead = i + depth - 1
        if ahead < len(items):
            states[ahead] = scores(items[ahead], s_bufs[ahead % depth])
        finish(item, s_bufs[i % depth], states.pop(i))


def _store_scores(s_ref, col0, s, mrun):
    s_ref[:, col0:col0 + s.shape[1]] = s
    for t in range(s.shape[1] // LANES):
        blk = s[:, t * LANES:(t + 1) * LANES]
        mrun = blk if mrun is None else jnp.maximum(mrun, blk)
    return mrun


def _exp_block(s_ref, col0, width, mb):
    return jnp.concatenate(
        [jnp.exp2(s_ref[:, col0 + t * LANES:col0 + (t + 1) * LANES] - mb).astype(BF16)
         for t in range(width // LANES)], axis=1)


def _diff_prompt_kernel(q_ref, k_ref, v_ref, lam_ref, g_ref, o_ref, *s_bufs, lam_init, seq):
    lam = _diff_lambda(lam_ref, lam_init)
    g = g_ref[...]
    ones = _ones_column(seq)
    items = [(r, h) for r in range(q_ref.shape[0] // seq) for h in range(DIFF_HEADS)]

    def head_rows(ref, r, h):
        return ref[pl.ds(r * seq * DIFF_HEADS + h, seq, stride=DIFF_HEADS), :]

    def scores(item, s_ref):
        r, h = item
        rows, sl = slice(r * seq, (r + 1) * seq), slice(h * LANES, (h + 1) * LANES)
        s = lax.dot_general(_stack_maps(q_ref[rows, sl]), head_rows(k_ref, r, h).astype(BF16), NT_DIMS,
                            preferred_element_type=F32)
        return _store_scores(s_ref, 0, s, None)

    def finish(item, s_ref, mrun):
        r, h = item
        rows, sl = slice(r * seq, (r + 1) * seq), slice(h * LANES, (h + 1) * LANES)
        mb = jnp.broadcast_to(jnp.max(mrun, axis=-1, keepdims=True), (2 * seq, LANES))
        vx = jnp.concatenate([head_rows(v_ref, r, h).astype(BF16), ones], axis=1)
        acc = jnp.dot(_exp_block(s_ref, 0, seq, mb), vx, preferred_element_type=F32)
        o_ref[rows, sl] = _diff_combine(acc, seq, lam, g, lam_init).astype(BF16)

    _run_pipelined(items, scores, finish, s_bufs)


def _diff_prompt_attention(q, k, v, lam_params, subln_g, *, j, seq, lam_init):
    n, d = q.shape
    req = 2
    spec = pl.BlockSpec((req * seq, d), lambda b: (b, 0))
    kv_spec = pl.BlockSpec((req * seq * DIFF_HEADS, d // DIFF_HEADS), lambda b: (b, 0))
    return pl.pallas_call(
        functools.partial(_diff_prompt_kernel, lam_init=lam_init, seq=seq),
        out_shape=jax.ShapeDtypeStruct((n, d), BF16),
        grid=(n // (req * seq),),
        in_specs=[spec, kv_spec, kv_spec, _layer_resident(lam_params.shape, j),
                  _layer_resident(subln_g.shape, j)],
        out_specs=spec,
        scratch_shapes=[pltpu.VMEM((2 * seq, seq), F32)] * PROMPT_SCORE_BUFFERS,
        compiler_params=_params(1),
        name="diff_attn_prompt",
    )(q, k, v, lam_params, subln_g)


def _diff_latent_kernel(q_ref, kc_ref, vc_ref, kl_ref, vl_ref, lam_ref, g_ref, o_ref,
                        kk_ref, vx_ref, s0_ref, s1_ref, *, lam_init, lc, tq, key_chunk):
    seq = q_ref.shape[0]
    heads = kk_ref.shape[0]
    nk = kk_ref.shape[1]
    for hh in range(heads):
        h = pl.program_id(1) * heads + hh
        sl = slice(hh * LANES, (hh + 1) * LANES)
        kk_ref[hh, 0:lc, :] = kc_ref[pl.ds(h, lc, stride=DIFF_HEADS), :].astype(BF16)
        kk_ref[hh, lc:, :] = kl_ref[:, sl]
        vx_ref[hh, 0:lc, 0:LANES] = vc_ref[pl.ds(h, lc, stride=DIFF_HEADS), :].astype(BF16)
        vx_ref[hh, lc:, 0:LANES] = vl_ref[:, sl]
        vx_ref[hh, :, LANES:2 * LANES] = _ones_column(nk)

    masks = _half_masks(BF16)
    nchunk = nk // key_chunk
    items = [(hh, rt, m) for hh in range(heads) for rt in range(seq // tq) for m in range(2)]
    lam = _diff_lambda(lam_ref, lam_init)
    g = g_ref[...]
    first_map = {}

    def scores(item, s_ref):
        hh, rt, m = item
        q = q_ref[rt * tq:(rt + 1) * tq, hh * LANES:(hh + 1) * LANES] * masks[m]
        mrun = None
        for c in range(nchunk):
            s = lax.dot_general(q, kk_ref[hh, c * key_chunk:(c + 1) * key_chunk, :], NT_DIMS,
                                preferred_element_type=F32)
            mrun = _store_scores(s_ref, c * key_chunk, s, mrun)
        return mrun

    def finish(item, s_ref, mrun):
        hh, rt, m = item
        mb = jnp.broadcast_to(jnp.max(mrun, axis=-1, keepdims=True), (tq, LANES))
        acc = None
        for c in range(nchunk):
            part = jnp.dot(_exp_block(s_ref, c * key_chunk, key_chunk, mb),
                           vx_ref[hh, c * key_chunk:(c + 1) * key_chunk, :], preferred_element_type=F32)
            acc = part if acc is None else acc + part
        o_m = acc[:, :LANES] / acc[:, LANES:]
        if m == 0:
            first_map[hh, rt] = o_m
        else:
            o = _rms(first_map.pop((hh, rt)) - lam * o_m, g) * (1.0 - lam_init)
            o_ref[rt * tq:(rt + 1) * tq, hh * LANES:(hh + 1) * LANES] = o.astype(BF16)

    _run_pipelined(items, scores, finish, (s0_ref, s1_ref))


def _diff_latent_attention(q, k, v, cache_k, cache_v, lam_params, subln_g, *, j, seq, lc, lam_init):
    n, d = q.shape
    nb = cache_k.shape[0]
    tq = 512
    key_chunk = 512
    heads = 1
    q_spec = pl.BlockSpec((seq, heads * LANES), lambda b, h: (b, h))
    c_spec = pl.BlockSpec((None, lc * DIFF_HEADS, LANES), lambda b, h: (b, j, 0))
    return pl.pallas_call(
        functools.partial(_diff_latent_kernel, lam_init=lam_init, lc=lc, tq=tq, key_chunk=key_chunk),
        out_shape=jax.ShapeDtypeStruct((n, d), BF16),
        grid=(nb, DIFF_HEADS // heads),
        in_specs=[q_spec, c_spec, c_spec, q_spec, q_spec,
                  _layer_resident(lam_params.shape, j), _layer_resident(subln_g.shape, j)],
        out_specs=q_spec,
        scratch_shapes=[pltpu.VMEM((heads, lc + seq, LANES), BF16),
                        pltpu.VMEM((heads, lc + seq, 2 * LANES), BF16),
                        pltpu.VMEM((tq, lc + seq), F32), pltpu.VMEM((tq, lc + seq), F32)],
        compiler_params=_params(2),
        name="diff_attn_latent",
    )(q, cache_k, cache_v, k, v, lam_params, subln_g)


def _stack_group(q_ref, rows, kv_local):
    m_lo, m_hi = _half_masks(BF16)
    parts = []
    for gb in range(SWA_GROUP // 2):
        blk = kv_local * (SWA_GROUP // 2) + gb
        qb = q_ref[rows, blk * LANES:(blk + 1) * LANES]
        parts += [qb * m_lo, qb * m_hi]
    return jnp.concatenate(parts, axis=0)


def _sink_column(sink_ref, first_head, tq):
    return jnp.concatenate([jnp.full((tq, LANES), sink_ref[first_head + g] * LOG2E, F32)
                            for g in range(SWA_GROUP)], axis=0)


def _write_group(o_ref, rows, kv_local, o, tq):
    lo64 = _lo64()
    for gb in range(SWA_GROUP // 2):
        blk = kv_local * (SWA_GROUP // 2) + gb
        even = o[(2 * gb) * tq:(2 * gb + 1) * tq]
        odd = o[(2 * gb + 1) * tq:(2 * gb + 2) * tq]
        o_ref[rows, blk * LANES:(blk + 1) * LANES] = jnp.where(lo64, even, odd).astype(BF16)


def _dup_rows(x_t):
    xb = x_t.astype(BF16)
    return jnp.concatenate([xb, xb], axis=0)


def _sink_finish(mrun, sk, rows):
    mb = jnp.maximum(jnp.broadcast_to(jnp.max(mrun, axis=-1, keepdims=True), (rows, LANES)), sk)
    return mb, jnp.exp2(sk - mb)


def _swa_prompt_kernel(sink_ref, q_ref, kt_ref, vt_ref, o_ref, *s_bufs):
    seq = kt_ref.shape[2]
    rows = SWA_GROUP * seq
    ones = _ones_row(seq)
    items = [(r, j) for r in range(kt_ref.shape[0]) for j in range(SWA_KV_HEADS)]

    def scores(item, s_ref):
        r, j = item
        kd = _dup_rows(kt_ref[r, j * SWA_HD:(j + 1) * SWA_HD, :])
        s = jnp.dot(_stack_group(q_ref, slice(r * seq, (r + 1) * seq), j), kd, preferred_element_type=F32)
        return _store_scores(s_ref, 0, s, None)

    def finish(item, s_ref, mrun):
        r, j = item
        sk = _sink_column(sink_ref, j * SWA_GROUP, seq)
        mb, sink_term = _sink_finish(mrun, sk, rows)
        vx = jnp.concatenate([_dup_rows(vt_ref[r, j * SWA_HD:(j + 1) * SWA_HD, :]), ones], axis=0)
        acc = lax.dot_general(_exp_block(s_ref, 0, seq, mb), vx, NT_DIMS, preferred_element_type=F32)
        o = acc[:, :LANES] / (acc[:, LANES:] + sink_term)
        _write_group(o_ref, slice(r * seq, (r + 1) * seq), j, o, seq)

    _run_pipelined(items, scores, finish, s_bufs)


def _swa_prompt_attention(q, kt, vt, sink, *, seq):
    n, d = q.shape
    nkv = kt.shape[1]
    req = 2
    t_spec = pl.BlockSpec((req, nkv, seq), lambda b: (b, 0, 0))
    return pl.pallas_call(
        _swa_prompt_kernel,
        out_shape=jax.ShapeDtypeStruct((n, d), BF16),
        grid=(n // (req * seq),),
        in_specs=[pl.BlockSpec(memory_space=pltpu.SMEM),
                  pl.BlockSpec((req * seq, d), lambda b: (b, 0)), t_spec, t_spec],
        out_specs=pl.BlockSpec((req * seq, d), lambda b: (b, 0)),
        scratch_shapes=[pltpu.VMEM((SWA_GROUP * seq, seq), F32)] * (PROMPT_SCORE_BUFFERS // 2),
        compiler_params=_params(1),
        name="swa_attn_prompt",
    )(sink, q, kt, vt)


def _swa_latent_kernel(sink_ref, q_ref, kc_ref, vc_ref, kl_ref, vl_ref, o_ref, kcd_ref, vcx_ref,
                       s0_ref, s1_ref, *, tq, span):
    pair = pl.program_id(1)
    tiles = q_ref.shape[0] // tq
    first_tile = pl.program_id(2) * tiles
    seq = kl_ref.shape[0]
    lc = kc_ref.shape[1]
    rows = SWA_GROUP * tq
    ones_row = _ones_row(lc)
    for jj in range(2):
        kcd_ref[jj] = _dup_rows(kc_ref[jj * SWA_HD:(jj + 1) * SWA_HD, :])
        vcx_ref[jj] = jnp.concatenate([_dup_rows(vc_ref[jj * SWA_HD:(jj + 1) * SWA_HD, :]), ones_row], axis=0)
    ones_col = _ones_column(span)
    items = [(t, jj) for t in range(tiles) for jj in range(2)]
    windows, biases = {}, {}

    def window(t):
        if t not in windows:
            q0 = (first_tile + t) * tq
            windows[t] = (q0, pl.multiple_of(jnp.clip(q0 - WINDOW, 0, seq - span), WINDOW))
        return windows[t]

    def bias_for(t):
        if t not in biases:
            q0, ws = window(t)
            q_pos = q0 + lax.broadcasted_iota(jnp.int32, (tq, span), 0)
            k_pos = ws + lax.broadcasted_iota(jnp.int32, (tq, span), 1)
            b = jnp.where(jnp.abs(q_pos - k_pos) <= WINDOW, 0.0, NEG_INF).astype(F32)
            biases[t] = jnp.concatenate([b] * SWA_GROUP, axis=0)
        return biases[t]

    def scores(item, s_ref):
        t, jj = item
        _, ws = window(t)
        qs = _stack_group(q_ref, slice(t * tq, (t + 1) * tq), jj)
        s_c = jnp.dot(qs, kcd_ref[jj], preferred_element_type=F32)
        mrun = _store_scores(s_ref, 0, s_c, None)
        s_w = lax.dot_general(qs, kl_ref[pl.ds(ws, span), jj * LANES:(jj + 1) * LANES], NT_DIMS,
                              preferred_element_type=F32) + bias_for(t)
        return _store_scores(s_ref, lc, s_w, mrun)

    def finish(item, s_ref, mrun):
        t, jj = item
        _, ws = window(t)
        sk = _sink_column(sink_ref, (2 * pair + jj) * SWA_GROUP, tq)
        mb, sink_term = _sink_finish(mrun, sk, rows)
        vwx = jnp.concatenate([vl_ref[pl.ds(ws, span), jj * LANES:(jj + 1) * LANES], ones_col], axis=1)
        acc = (lax.dot_general(_exp_block(s_ref, 0, lc, mb), vcx_ref[jj], NT_DIMS, preferred_element_type=F32)
               + jnp.dot(_exp_block(s_ref, lc, span, mb), vwx, preferred_element_type=F32))
        o = acc[:, :LANES] / (acc[:, LANES:] + sink_term)
        _write_group(o_ref, slice(t * tq, (t + 1) * tq), jj, o, tq)

    _run_pipelined(items, scores, finish, (s0_ref, s1_ref))


def _swa_latent_attention(q, kd, vd, cache_kt, cache_vt, sink, *, j, seq):
    n, d = q.shape
    nb, _, lc = cache_kt.shape
    tq = 256
    span = tq + 2 * WINDOW
    npair = SWA_KV_HEADS // 2
    wq = d // npair
    parts = 2
    q_spec = pl.BlockSpec((seq // parts, wq), lambda b, p, i: (b * parts + i, p))
    c_spec = pl.BlockSpec((None, 2 * SWA_HD, lc), lambda b, p, i: (b, j * npair + p, 0))
    l_spec = pl.BlockSpec((seq, 2 * LANES), lambda b, p, i: (b, p))
    s_shape = pltpu.VMEM((SWA_GROUP * tq, lc + span), F32)
    return pl.pallas_call(
        functools.partial(_swa_latent_kernel, tq=tq, span=span),
        out_shape=jax.ShapeDtypeStruct((n, d), BF16),
        grid=(nb, npair, parts),
        in_specs=[pl.BlockSpec(memory_space=pltpu.SMEM), q_spec, c_spec, c_spec, l_spec, l_spec],
        out_specs=q_spec,
        scratch_shapes=[pltpu.VMEM((2, 2 * SWA_HD, lc), BF16), pltpu.VMEM((2, 2 * LANES, lc), BF16),
                        s_shape, s_shape],
        compiler_params=_params(3),
        name="swa_attn_latent",
    )(sink, q, cache_kt, cache_vt, kd, vd)


def _load_weights_as_bf16(jobs, stages, sems):
    order = []
    rings = {w: [] for w in stages}
    for src, dst in jobs:
        w = src.shape[1]
        slots = stages[w].shape[0]
        for k in range(src.shape[0] // WEIGHT_STAGE_ROWS):
            rows = pl.ds(k * WEIGHT_STAGE_ROWS, WEIGHT_STAGE_ROWS)
            slot = len(rings[w]) % slots
            copy = pltpu.make_async_copy(src.at[rows, :], stages[w].at[slot], sems[w].at[slot])
            order.append((w, len(rings[w])))
            rings[w].append((copy, slot, dst, rows))
    for w, ring in rings.items():
        for copy, _, _, _ in ring[:stages[w].shape[0]]:
            copy.start()
    for w, k in order:
        copy, slot, dst, rows = rings[w][k]
        copy.wait()
        dst[rows, :] = stages[w][slot].astype(BF16)
        ahead = k + stages[w].shape[0]
        if ahead < len(rings[w]):
            rings[w][ahead][0].start()


def _post_attn_ffn_kernel(*refs, layer, j, n_prompt_tiles, tiles_per_request, split_x, split_out):
    refs = list(refs)
    op_ref, os_ref = refs[:2]
    x_refs = refs[2:4] if split_x else refs[2:3]
    wo_hbm, wg_hbm, wu_hbm, wd_hbm, g_ref, mod_ref = refs[2 + len(x_refs):8 + len(x_refs)]
    n_out = 2 if split_out else 1
    out_refs = refs[8 + len(x_refs):8 + len(x_refs) + n_out]
    wo_ref, wg_ref, wu_ref, wd_ref, stage_d, stage_ff, sem_d, sem_ff = refs[8 + len(x_refs) + n_out:]
    d = wo_ref.shape[1]
    dff = wg_ref.shape[1]
    i = pl.program_id(0)

    @pl.when(i == 0)
    def _():
        _load_weights_as_bf16(
            [(wo_hbm.at[j], wo_ref), (wg_hbm.at[layer], wg_ref), (wu_hbm.at[layer], wu_ref),
             (wd_hbm.at[layer], wd_ref)],
            {d: stage_d, dff: stage_ff}, {d: sem_d, dff: sem_ff})

    is_prompt = i < n_prompt_tiles
    r = jnp.where(is_prompt, 0, 1 + (i - n_prompt_tiles) // tiles_per_request)

    def mod(slot):
        return mod_ref[pl.ds(r, 1), slot * d:(slot + 1) * d]

    o = jnp.where(is_prompt, op_ref[...], os_ref[...])
    x = jnp.where(is_prompt, x_refs[0][...], x_refs[1][...]) if split_x else x_refs[0][...]
    y = jnp.dot(o, wo_ref[...], preferred_element_type=F32)
    x = x + _rms(y, mod(2) * g_ref[1])
    h = (_rms(x, g_ref[2] * (1 + mod(4))) + mod(3)).astype(BF16)
    y = jnp.zeros((h.shape[0], d), F32)
    lo = 0
    while lo < dff:
        hi = min(lo + FFN_CHUNK, dff)
        a = jnp.dot(h, wg_ref[:, lo:hi], preferred_element_type=F32)
        u = jnp.dot(h, wu_ref[:, lo:hi], preferred_element_type=F32)
        t = (a * jax.nn.sigmoid(a)) * u
        y = y + jnp.dot(t.astype(BF16), wd_ref[lo:hi, :], preferred_element_type=F32)
        lo = hi
    out = x + _rms(y, mod(5) * g_ref[3])
    if split_out:
        @pl.when(is_prompt)
        def _():
            out_refs[0][...] = out

        @pl.when(jnp.logical_not(is_prompt))
        def _():
            out_refs[1][...] = out
    else:
        out_refs[0][...] = out


def _post_attn_ffn(o_p, o_s, xs_in, w_o, wg, wu, wd, g, mods, *, layer, j, tiles_per_request, split_out):
    n_p, d = o_p.shape
    n_s = o_s.shape[0]
    tm = ROW_TILE
    tp, ts = n_p // tm, n_s // tm
    prompt_rows = pl.BlockSpec((tm, d), lambda i: (jnp.minimum(i, tp - 1), 0))
    latent_rows = pl.BlockSpec((tm, d), lambda i: (jnp.maximum(i - tp, 0), 0))
    all_rows = pl.BlockSpec((tm, d), lambda i: (i, 0))
    split_x = len(xs_in) == 2
    in_specs = [prompt_rows, latent_rows] + ([prompt_rows, latent_rows] if split_x else [all_rows])
    hbm = pl.BlockSpec(memory_space=pl.ANY)
    in_specs += [hbm, hbm, hbm, hbm, _layer_resident(g.shape, layer), _layer_resident(mods.shape, layer)]
    if split_out:
        out_shape = (jax.ShapeDtypeStruct((n_p, d), F32), jax.ShapeDtypeStruct((n_s, d), F32))
        out_specs = (prompt_rows, latent_rows)
    else:
        out_shape = jax.ShapeDtypeStruct((n_p + n_s, d), F32)
        out_specs = all_rows
    dff = wg.shape[2]
    scratch = [pltpu.VMEM((d, d), BF16), pltpu.VMEM((d, dff), BF16), pltpu.VMEM((d, dff), BF16),
               pltpu.VMEM((dff, d), BF16),
               pltpu.VMEM((WEIGHT_STAGE_SLOTS, WEIGHT_STAGE_ROWS, d), F32),
               pltpu.VMEM((WEIGHT_STAGE_SLOTS, WEIGHT_STAGE_ROWS, dff), F32),
               pltpu.SemaphoreType.DMA((WEIGHT_STAGE_SLOTS,)), pltpu.SemaphoreType.DMA((WEIGHT_STAGE_SLOTS,))]
    return pl.pallas_call(
        functools.partial(_post_attn_ffn_kernel, layer=layer, j=j, n_prompt_tiles=tp,
                          tiles_per_request=tiles_per_request, split_x=split_x, split_out=split_out),
        out_shape=out_shape,
        grid=(tp + ts,),
        in_specs=in_specs,
        out_specs=out_specs,
        scratch_shapes=scratch,
        compiler_params=pltpu.CompilerParams(dimension_semantics=("arbitrary",),
                                             vmem_limit_bytes=FFN_VMEM_LIMIT),
        name="post_attn_ffn",
    )(o_p, o_s, *xs_in, w_o, wg, wu, wd, g, mods)


def _rope_tables(n_lat):
    t = jnp.arange(n_lat)
    row = (t // GRID_W).astype(F32)
    col = (t % GRID_W).astype(F32)
    nf = ROT_DIM // 4
    inv = ROPE_BASE ** (-jnp.arange(nf, dtype=F32) / nf)
    ar = row[:, None] * inv[None, :]
    ac = col[:, None] * inv[None, :]
    ang = jnp.concatenate([ar, ar, ac, ac], axis=-1)
    cos, sin = jnp.cos(ang), jnp.sin(ang)
    sign = jnp.where((jnp.arange(ROT_DIM) % 32) < 16, -1.0, 1.0).astype(F32)
    reps = LANES // ROT_DIM
    return jnp.tile(cos, (1, reps)), jnp.tile(sin * sign, (1, reps))


def _swa_cache_to_feature_major(cache):
    nb, nl, lc, nh, hd = cache.shape
    return cache.transpose(0, 1, 3, 4, 2).reshape(nb, nl * nh * hd, lc)


def _swa_cache_from_feature_major(xt, seq):
    nb = xt.shape[0]
    return xt.reshape(nb, SWA_KV_HEADS, SWA_HD, seq).transpose(0, 3, 1, 2)


def kernel(x_prompt, x_sample, cache_diff_k, cache_diff_v, cache_swa_k, cache_swa_v, c, c_ctx,
           w_mod, b_mod, norm_g, w_qkv_diff, diff_lambda, diff_subln_g, w_o_diff,
           w_qkv_swa, swa_sink, w_o_swa, w_gate, w_up, w_down):
    bp, lp, d = x_prompt.shape
    bs, ls, _ = x_sample.shape
    lc = cache_diff_k.shape[2]
    depth = w_mod.shape[0]
    tm = ROW_TILE

    cond8 = jnp.concatenate([c_ctx[None, :], c, jnp.zeros((8 - 1 - bs, d), F32)], axis=0)
    mods = _modulation(cond8, w_mod, b_mod)
    tables = _rope_tables(ls)

    cdk = cache_diff_k.reshape(bs, -1, 2 * DIFF_HD)
    cdv = cache_diff_v.reshape(bs, -1, 2 * DIFF_HD)
    cskt = _swa_cache_to_feature_major(cache_swa_k)
    csvt = _swa_cache_to_feature_major(cache_swa_v)

    n_p, n_s = bp * lp, bs * ls
    x_parts = (x_prompt.reshape(n_p, d), x_sample.reshape(n_s, d))
    g = norm_g.reshape(depth, 4, 1, d)
    sub_g = diff_subln_g.reshape(-1, 1, 2 * DIFF_HD)
    p_rows = dict(row0=0, tiles_per_batch=n_p // tm)
    s_rows = dict(row0=1, tiles_per_batch=ls // tm)
    diff_k_out, diff_v_out, swa_k_out, swa_v_out = [], [], [], []

    for i in range(depth):
        j = i // N_MIXERS
        if len(x_parts) == 2:
            p_src, s_src = (x_parts[0], n_p, 0), (x_parts[1], n_s, 0)
        else:
            p_src, s_src = (x_parts[0], n_p, 0), (x_parts[0], n_s, n_p // tm)
        if i % N_MIXERS == 0:
            lam_init = 0.8 - 0.6 * math.exp(-0.3 * i)
            qp, kp, vp, wb = _qkv_diff(*p_src, g, mods, w_qkv_diff, None, layer=i, j=j, latent=False,
                                       **p_rows)
            op = _diff_prompt_attention(qp, kp, vp, diff_lambda, sub_g, j=j, seq=lp, lam_init=lam_init)
            diff_k_out.append(kp.reshape(bp, lp, DIFF_HEADS, 2 * DIFF_HD))
            diff_v_out.append(vp.reshape(bp, lp, DIFF_HEADS, 2 * DIFF_HD))
            qs, ks, vs = _qkv_diff(*s_src, g, mods, wb, tables, layer=i, j=j, latent=True, **s_rows)
            os_ = _diff_latent_attention(qs, ks, vs, cdk, cdv, diff_lambda, sub_g,
                                         j=j, seq=ls, lc=lc, lam_init=lam_init)
            w_o = w_o_diff
        else:
            qp, ktp, vtp, wb = _qkv_swa(*p_src, g, mods, w_qkv_swa, None, layer=i, j=j, latent=False,
                                        seq=lp, **p_rows)
            op = _swa_prompt_attention(qp, ktp, vtp, swa_sink[j], seq=lp)
            swa_k_out.append(_swa_cache_from_feature_major(ktp, lp))
            swa_v_out.append(_swa_cache_from_feature_major(vtp, lp))
            qs, kds, vds = _qkv_swa(*s_src, g, mods, wb, tables, layer=i, j=j, latent=True,
                                    seq=ls, **s_rows)
            os_ = _swa_latent_attention(qs, kds, vds, cskt, csvt, swa_sink[j], j=j, seq=ls)
            w_o = w_o_swa
        last = i == depth - 1
        out = _post_attn_ffn(op, os_, x_parts, w_o, w_gate, w_up, w_down, g, mods, layer=i, j=j,
                             tiles_per_request=ls // tm, split_out=last)
        x_parts = out if last else (out,)
    xp, xs = x_parts

    return (xp.reshape(bp, lp, d), xs.reshape(bs, ls, d),
            jnp.stack(diff_k_out, axis=1), jnp.stack(diff_v_out, axis=1),
            jnp.stack(swa_k_out, axis=1), jnp.stack(swa_v_out, axis=1))
```

```python
import functools
import math

import jax
import jax.numpy as jnp
from jax import lax
from jax.experimental import pallas as pl
from jax.experimental.pallas import tpu as pltpu

F32 = jnp.float32
BF16 = jnp.bfloat16

GRID_W = 64
N_MIXERS = 2
DIFF_HEADS = 8
DIFF_HD = 64
SWA_HEADS = 16
SWA_KV_HEADS = 4
SWA_GROUP = SWA_HEADS // SWA_KV_HEADS
SWA_HD = 64
ROT_DIM = 64
WINDOW = 128
ROPE_BASE = 10000.0
EPS = 1e-6
NEG_INF = -1e30

LANES = 128
SUBLANES = 8
ROW_TILE = 512
QKV_ROW_TILE = 1024
PROMPT_SCORE_BUFFERS = 8
LATENT_SCORE_BUFFERS = 2
VMEM_LIMIT = 48 * 1024 * 1024
FFN_VMEM_LIMIT = 58 * 1024 * 1024
WEIGHT_STAGE_ROWS = 128
WEIGHT_STAGE_SLOTS = 6
QKV_STAGE_SLOTS = 4
FFN_CHUNK = 256
NT_DIMS = (((1,), (1,)), ((), ()))
LOG2E = math.log2(math.e)


def _params(n_axes):
    return pltpu.CompilerParams(dimension_semantics=("arbitrary",) * n_axes,
                                vmem_limit_bytes=VMEM_LIMIT)


def _resident(shape):
    return pl.BlockSpec(shape, lambda *_: (0,) * len(shape), pipeline_mode=pl.Buffered(1))


def _layer_resident(shape, layer):
    return pl.BlockSpec((None,) + tuple(shape[1:]), lambda *_: (layer,) + (0,) * (len(shape) - 1),
                        pipeline_mode=pl.Buffered(1))


def _rms(x, g):
    ms = jnp.mean(x * x, axis=-1, keepdims=True)
    return (x * lax.rsqrt(ms + EPS)) * g


def _half_masks(dtype):
    lane = lax.broadcasted_iota(jnp.int32, (1, LANES), 1)
    lo = lane < (LANES // 2)
    return jnp.where(lo, 1.0, 0.0).astype(dtype), jnp.where(lo, 0.0, 1.0).astype(dtype)


def _lo64():
    return lax.broadcasted_iota(jnp.int32, (1, LANES), 1) < (LANES // 2)


def _ones_column(rows):
    return jnp.ones((rows, LANES), BF16)


def _ones_row(cols):
    return jnp.ones((LANES, cols), BF16)


def _mod_kernel(cond_ref, w_ref, b_ref, out_ref):
    c = cond_ref[...]
    s = c * jax.nn.sigmoid(c)
    out_ref[...] = jnp.dot(s.astype(BF16), w_ref[...].astype(BF16),
                           preferred_element_type=F32) + b_ref[...]


def _modulation(cond8, w_mod, b_mod):
    depth, d, n = w_mod.shape
    tn = 1536
    return pl.pallas_call(
        _mod_kernel,
        out_shape=jax.ShapeDtypeStruct((depth, 8, n), F32),
        grid=(depth, n // tn),
        in_specs=[pl.BlockSpec((8, d), lambda i, j: (0, 0)),
                  pl.BlockSpec((None, d, tn), lambda i, j: (i, 0, j)),
                  pl.BlockSpec((None, 1, tn), lambda i, j: (i, 0, j))],
        out_specs=pl.BlockSpec((None, 8, tn), lambda i, j: (i, 0, j)),
        compiler_params=_params(2),
        name="modulation",
    )(cond8, w_mod, b_mod.reshape(depth, 1, n))


def _mod_row(row0, tiles_per_batch):
    return row0 + pl.program_id(0) // tiles_per_batch


def _rope_slab(xs, cos, sin_signed, lo16):
    left = pltpu.roll(xs, LANES - 16, 1)
    right = pltpu.roll(xs, 16, 1)
    return xs * cos + jnp.where(lo16, left, right) * sin_signed


def _lo16_mask():
    lane = lax.broadcasted_iota(jnp.int32, (1, LANES), 1)
    return (lane % 32) < 16


def _stage_and_export_weight(w_hbm, wb_out, w_vmem, stage, sem, sem_out, n_steps):
    i = pl.program_id(0)
    export = pltpu.make_async_copy(w_vmem, wb_out, sem_out)

    @pl.when(i == 0)
    def _():
        _load_weights_as_bf16([(w_hbm, w_vmem)], {w_hbm.shape[1]: stage}, {w_hbm.shape[1]: sem})
        export.start()

    @pl.when(i == n_steps - 1)
    def _():
        export.wait()


def _pre_norm(x_ref, g_ref, mod_ref, r, d, slot):
    shift = mod_ref[pl.ds(r, 1), slot * d:(slot + 1) * d]
    scale = mod_ref[pl.ds(r, 1), (slot + 1) * d:(slot + 2) * d]
    return _rms(x_ref[...], g_ref[0] * (1 + scale)) + shift


def _qkv_diff_kernel(x_ref, g_ref, mod_ref, w_ref, *rest, latent, j, n_steps, row0, tiles_per_batch):
    d = x_ref.shape[1]
    if latent:
        cos_ref, sin_ref, q_ref, k_ref, v_ref = rest
        cos, sin = cos_ref[...], sin_ref[...]
        lo16 = _lo16_mask()
    else:
        q_ref, k_ref, v_ref, wb_out, w_vmem, stage, sem, sem_out = rest
        _stage_and_export_weight(w_ref.at[j], wb_out, w_vmem, stage, sem, sem_out, n_steps)
        w_ref = w_vmem
    r = _mod_row(row0, tiles_per_batch)
    h = _pre_norm(x_ref, g_ref, mod_ref, r, d, 0).astype(BF16)
    cw = 512
    for c in range(3 * d // cw):
        acc = jnp.dot(h, w_ref[:, c * cw:(c + 1) * cw], preferred_element_type=F32)
        which, off = divmod(c * cw, d)
        dst = (q_ref, k_ref, v_ref)[which]
        for s in range(cw // LANES):
            xs = acc[:, s * LANES:(s + 1) * LANES]
            if latent and which < 2:
                xs = _rope_slab(xs, cos, sin, lo16)
            if which == 0:
                xs = xs * (DIFF_HD ** -0.5 * LOG2E)
            lo = off + s * LANES
            if latent or which == 0:
                dst[:, lo:lo + LANES] = xs.astype(dst.dtype)
            else:
                dst[pl.ds(lo // LANES, x_ref.shape[0], stride=DIFF_HEADS), :] = xs


def _qkv_diff(x, n, tile0, g, mods, w, tables, *, layer, j, latent, row0, tiles_per_batch):
    d = x.shape[1]
    tm = QKV_ROW_TILE
    w_spec, w_out_shapes, w_out_specs, scratch = _qkv_weight_plumbing(w, j, latent)
    in_specs = [pl.BlockSpec((tm, d), lambda i: (i + tile0, 0)), _layer_resident(g.shape, layer),
                _layer_resident(mods.shape, layer), w_spec]
    args = [x, g, mods, w]
    kv_dtype = F32
    if latent:
        nt = tables[0].shape[0] // tm
        in_specs += [pl.BlockSpec((tm, LANES), lambda i: (i % nt, 0))] * 2
        args += list(tables)
        kv_dtype = BF16
    out_spec = pl.BlockSpec((tm, d), lambda i: (i, 0))
    kv_shape, kv_spec = jax.ShapeDtypeStruct((n, d), kv_dtype), out_spec
    if not latent:
        kv_shape = jax.ShapeDtypeStruct((n * DIFF_HEADS, d // DIFF_HEADS), kv_dtype)
        kv_spec = pl.BlockSpec((tm * DIFF_HEADS, d // DIFF_HEADS), lambda i: (i, 0))
    return pl.pallas_call(
        functools.partial(_qkv_diff_kernel, latent=latent, j=j, n_steps=n // tm, row0=row0,
                          tiles_per_batch=tiles_per_batch),
        out_shape=(jax.ShapeDtypeStruct((n, d), BF16), kv_shape, kv_shape, *w_out_shapes),
        grid=(n // tm,),
        in_specs=in_specs,
        out_specs=(out_spec, kv_spec, kv_spec, *w_out_specs),
        scratch_shapes=scratch,
        compiler_params=_params(1),
        name="qkv_diff_latent" if latent else "qkv_diff_prompt",
    )(*args)


def _qkv_weight_plumbing(w, j, latent):
    if latent:
        return _resident(w.shape), [], [], []
    _, d, cols = w.shape
    hbm = pl.BlockSpec(memory_space=pl.ANY)
    scratch = [pltpu.VMEM((d, cols), BF16), pltpu.VMEM((QKV_STAGE_SLOTS, WEIGHT_STAGE_ROWS, cols), F32),
               pltpu.SemaphoreType.DMA((QKV_STAGE_SLOTS,)), pltpu.SemaphoreType.DMA(())]
    return hbm, [jax.ShapeDtypeStruct((d, cols), BF16)], [hbm], scratch


def _qkv_swa_kernel(x_ref, g_ref, mod_ref, w_ref, *rest, latent, j, n_steps, row0, tiles_per_batch, seq):
    d = x_ref.shape[1]
    nkv = SWA_KV_HEADS * SWA_HD
    if latent:
        cos_ref, sin_ref, q_ref, kd_ref, vd_ref = rest
        cos, sin = cos_ref[...], sin_ref[...]
        lo16 = _lo16_mask()
    else:
        q_ref, kt_ref, vt_ref, wb_out, w_vmem, stage, sem, sem_out = rest
        _stage_and_export_weight(w_ref.at[j], wb_out, w_vmem, stage, sem, sem_out, n_steps)
        w_ref = w_vmem
    lo64 = _lo64()
    r = _mod_row(row0, tiles_per_batch)
    h = _pre_norm(x_ref, g_ref, mod_ref, r, d, 0).astype(BF16)
    cw = 512
    for c in range(d // cw):
        acc = jnp.dot(h, w_ref[:, c * cw:(c + 1) * cw], preferred_element_type=F32)
        for s in range(cw // LANES):
            xs = acc[:, s * LANES:(s + 1) * LANES]
            if latent:
                xs = _rope_slab(xs, cos, sin, lo16)
            lo = c * cw + s * LANES
            q_ref[:, lo:lo + LANES] = (xs * (SWA_HD ** -0.5 * LOG2E)).astype(BF16)
    kv = jnp.dot(h, w_ref[:, d:d + 2 * nkv], preferred_element_type=F32)
    if not latent:
        for b in range(x_ref.shape[0] // seq):
            kt_ref[b] = kv[b * seq:(b + 1) * seq, :nkv].T
            vt_ref[b] = kv[b * seq:(b + 1) * seq, nkv:].T
        return
    for which, dst in enumerate((kd_ref, vd_ref)):
        for s in range(nkv // LANES):
            xs = kv[:, which * nkv + s * LANES: which * nkv + (s + 1) * LANES]
            if which == 0:
                xs = _rope_slab(xs, cos, sin, lo16)
            sw = pltpu.roll(xs, LANES // 2, 1)
            dst[:, (2 * s) * LANES:(2 * s + 1) * LANES] = jnp.where(lo64, xs, sw).astype(BF16)
            dst[:, (2 * s + 1) * LANES:(2 * s + 2) * LANES] = jnp.where(lo64, sw, xs).astype(BF16)


def _qkv_swa(x, n, tile0, g, mods, w, tables, *, layer, j, latent, row0, tiles_per_batch, seq):
    d = x.shape[1]
    tm = QKV_ROW_TILE
    nkv = SWA_KV_HEADS * SWA_HD
    w_spec, w_out_shapes, w_out_specs, scratch = _qkv_weight_plumbing(w, j, latent)
    in_specs = [pl.BlockSpec((tm, d), lambda i: (i + tile0, 0)), _layer_resident(g.shape, layer),
                _layer_resident(mods.shape, layer), w_spec]
    args = [x, g, mods, w]
    out_shape = [jax.ShapeDtypeStruct((n, d), BF16)]
    out_specs = [pl.BlockSpec((tm, d), lambda i: (i, 0))]
    if latent:
        nt = tables[0].shape[0] // tm
        in_specs += [pl.BlockSpec((tm, LANES), lambda i: (i % nt, 0))] * 2
        args += list(tables)
        out_shape += [jax.ShapeDtypeStruct((n, 2 * nkv), BF16)] * 2
        out_specs += [pl.BlockSpec((tm, 2 * nkv), lambda i: (i, 0))] * 2
    else:
        out_shape += [jax.ShapeDtypeStruct((n // seq, nkv, seq), F32)] * 2
        out_specs += [pl.BlockSpec((tm // seq, nkv, seq), lambda i: (i, 0, 0))] * 2
    return pl.pallas_call(
        functools.partial(_qkv_swa_kernel, latent=latent, j=j, n_steps=n // tm, row0=row0,
                          tiles_per_batch=tiles_per_batch, seq=seq),
        out_shape=(*out_shape, *w_out_shapes),
        grid=(n // tm,),
        in_specs=in_specs,
        out_specs=(*out_specs, *w_out_specs),
        scratch_shapes=scratch,
        compiler_params=_params(1),
        name="qkv_swa_latent" if latent else "qkv_swa_prompt",
    )(*args)


def _diff_lambda(lam_ref, lam_init):
    lp = lam_ref[...]
    a = jnp.sum(lp[0:1] * lp[1:2], axis=-1, keepdims=True)
    b = jnp.sum(lp[2:3] * lp[3:4], axis=-1, keepdims=True)
    return jnp.exp(a) - jnp.exp(b) + lam_init


def _diff_combine(acc, tq, lam, g, lam_init):
    o12 = acc[:, :LANES] / acc[:, LANES:]
    o = o12[:tq] - lam * o12[tq:]
    return _rms(o, g) * (1.0 - lam_init)


def _stack_maps(q):
    m_lo, m_hi = _half_masks(BF16)
    return jnp.concatenate([q * m_lo, q * m_hi], axis=0)


def _run_pipelined(items, scores, finish, s_bufs):
    depth = len(s_bufs)
    states = {i: scores(items[i], s_bufs[i]) for i in range(min(depth - 1, len(items)))}
    for i, item in enumerate(items):
        ahead = i + depth - 1
        if ahead < len(items):
            states[ahead] = scores(items[ahead], s_bufs[ahead % depth])
        finish(item, s_bufs[i % depth], states.pop(i))


def _store_scores(s_ref, col0, s, mrun):
    s_ref[:, col0:col0 + s.shape[1]] = s
    for t in range(s.shape[1] // LANES):
        blk = s[:, t * LANES:(t + 1) * LANES]
        mrun = blk if mrun is None else jnp.maximum(mrun, blk)
    return mrun


def _exp_block(s_ref, col0, width, mb):
    return jnp.concatenate(
        [jnp.exp2(s_ref[:, col0 + t * LANES:col0 + (t + 1) * LANES] - mb).astype(BF16)
         for t in range(width // LANES)], axis=1)


def _diff_prompt_kernel(q_ref, k_ref, v_ref, lam_ref, g_ref, o_ref, *s_bufs, lam_init, seq):
    lam = _diff_lambda(lam_ref, lam_init)
    g = g_ref[...]
    ones = _ones_column(seq)
    items = [(r, h) for r in range(q_ref.shape[0] // seq) for h in range(DIFF_HEADS)]

    def head_rows(ref, r, h):
        return ref[pl.ds(r * seq * DIFF_HEADS + h, seq, stride=DIFF_HEADS), :]

    def scores(item, s_ref):
        r, h = item
        rows, sl = slice(r * seq, (r + 1) * seq), slice(h * LANES, (h + 1) * LANES)
        s = lax.dot_general(_stack_maps(q_ref[rows, sl]), head_rows(k_ref, r, h).astype(BF16), NT_DIMS,
                            preferred_element_type=F32)
        return _store_scores(s_ref, 0, s, None)

    def finish(item, s_ref, mrun):
        r, h = item
        rows, sl = slice(r * seq, (r + 1) * seq), slice(h * LANES, (h + 1) * LANES)
        mb = jnp.broadcast_to(jnp.max(mrun, axis=-1, keepdims=True), (2 * seq, LANES))
        vx = jnp.concatenate([head_rows(v_ref, r, h).astype(BF16), ones], axis=1)
        acc = jnp.dot(_exp_block(s_ref, 0, seq, mb), vx, preferred_element_type=F32)
        o_ref[rows, sl] = _diff_combine(acc, seq, lam, g, lam_init).astype(BF16)

    _run_pipelined(items, scores, finish, s_bufs)


def _diff_prompt_attention(q, k, v, lam_params, subln_g, *, j, seq, lam_init):
    n, d = q.shape
    req = 2
    spec = pl.BlockSpec((req * seq, d), lambda b: (b, 0))
    kv_spec = pl.BlockSpec((req * seq * DIFF_HEADS, d // DIFF_HEADS), lambda b: (b, 0))
    return pl.pallas_call(
        functools.partial(_diff_prompt_kernel, lam_init=lam_init, seq=seq),
        out_shape=jax.ShapeDtypeStruct((n, d), BF16),
        grid=(n // (req * seq),),
        in_specs=[spec, kv_spec, kv_spec, _layer_resident(lam_params.shape, j),
                  _layer_resident(subln_g.shape, j)],
        out_specs=spec,
        scratch_shapes=[pltpu.VMEM((2 * seq, seq), F32)] * PROMPT_SCORE_BUFFERS,
        compiler_params=_params(1),
        name="diff_attn_prompt",
    )(q, k, v, lam_params, subln_g)


def _diff_latent_kernel(q_ref, kc_ref, vc_ref, kl_ref, vl_ref, lam_ref, g_ref, o_ref,
                        kk_ref, vx_ref, *s_bufs, lam_init, lc, tq, key_chunk):
    seq = q_ref.shape[0]
    heads = kk_ref.shape[0]
    nk = kk_ref.shape[1]
    for hh in range(heads):
        h = pl.program_id(1) * heads + hh
        sl = slice(hh * LANES, (hh + 1) * LANES)
        kk_ref[hh, 0:lc, :] = kc_ref[pl.ds(h, lc, stride=DIFF_HEADS), :].astype(BF16)
        kk_ref[hh, lc:, :] = kl_ref[:, sl]
        vx_ref[hh, 0:lc, 0:LANES] = vc_ref[pl.ds(h, lc, stride=DIFF_HEADS), :].astype(BF16)
        vx_ref[hh, lc:, 0:LANES] = vl_ref[:, sl]
        vx_ref[hh, :, LANES:2 * LANES] = _ones_column(nk)

    masks = _half_masks(BF16)
    nchunk = nk // key_chunk
    items = [(hh, rt, m) for hh in range(heads) for rt in range(seq // tq) for m in range(2)]
    lam = _diff_lambda(lam_ref, lam_init)
    g = g_ref[...]
    first_map = {}

    def scores(item, s_ref):
        hh, rt, m = item
        q = q_ref[rt * tq:(rt + 1) * tq, hh * LANES:(hh + 1) * LANES] * masks[m]
        mrun = None
        for c in range(nchunk):
            s = lax.dot_general(q, kk_ref[hh, c * key_chunk:(c + 1) * key_chunk, :], NT_DIMS,
                                preferred_element_type=F32)
            mrun = _store_scores(s_ref, c * key_chunk, s, mrun)
        return mrun

    def finish(item, s_ref, mrun):
        hh, rt, m = item
        mb = jnp.broadcast_to(jnp.max(mrun, axis=-1, keepdims=True), (tq, LANES))
        acc = None
        for c in range(nchunk):
            part = jnp.dot(_exp_block(s_ref, c * key_chunk, key_chunk, mb),
                           vx_ref[hh, c * key_chunk:(c + 1) * key_chunk, :], preferred_element_type=F32)
            acc = part if acc is None else acc + part
        o_m = acc[:, :LANES] / acc[:, LANES:]
        if m == 0:
            first_map[hh, rt] = o_m
        else:
            o = _rms(first_map.pop((hh, rt)) - lam * o_m, g) * (1.0 - lam_init)
            o_ref[rt * tq:(rt + 1) * tq, hh * LANES:(hh + 1) * LANES] = o.astype(BF16)

    _run_pipelined(items, scores, finish, s_bufs)


def _diff_latent_attention(q, k, v, cache_k, cache_v, lam_params, subln_g, *, j, seq, lc, lam_init):
    n, d = q.shape
    nb = cache_k.shape[0]
    tq = 512
    key_chunk = 512
    heads = 1
    q_spec = pl.BlockSpec((seq, heads * LANES), lambda b, h: (b, h))
    c_spec = pl.BlockSpec((None, lc * DIFF_HEADS, LANES), lambda b, h: (b, j, 0))
    return pl.pallas_call(
        functools.partial(_diff_latent_kernel, lam_init=lam_init, lc=lc, tq=tq, key_chunk=key_chunk),
        out_shape=jax.ShapeDtypeStruct((n, d), BF16),
        grid=(nb, DIFF_HEADS // heads),
        in_specs=[q_spec, c_spec, c_spec, q_spec, q_spec,
                  _layer_resident(lam_params.shape, j), _layer_resident(subln_g.shape, j)],
        out_specs=q_spec,
        scratch_shapes=[pltpu.VMEM((heads, lc + seq, LANES), BF16),
                        pltpu.VMEM((heads, lc + seq, 2 * LANES), BF16),
                        *[pltpu.VMEM((tq, lc + seq), F32)] * LATENT_SCORE_BUFFERS],
        compiler_params=_params(2),
        name="diff_attn_latent",
    )(q, cache_k, cache_v, k, v, lam_params, subln_g)


def _stack_group(q_ref, rows, kv_local):
    m_lo, m_hi = _half_masks(BF16)
    parts = []
    for gb in range(SWA_GROUP // 2):
        blk = kv_local * (SWA_GROUP // 2) + gb
        qb = q_ref[rows, blk * LANES:(blk + 1) * LANES]
        parts += [qb * m_lo, qb * m_hi]
    return jnp.concatenate(parts, axis=0)


def _sink_column(sink_ref, first_head, tq):
    return jnp.concatenate([jnp.full((tq, LANES), sink_ref[first_head + g] * LOG2E, F32)
                            for g in range(SWA_GROUP)], axis=0)


def _write_group(o_ref, rows, kv_local, o, tq):
    lo64 = _lo64()
    for gb in range(SWA_GROUP // 2):
        blk = kv_local * (SWA_GROUP // 2) + gb
        even = o[(2 * gb) * tq:(2 * gb + 1) * tq]
        odd = o[(2 * gb + 1) * tq:(2 * gb + 2) * tq]
        o_ref[rows, blk * LANES:(blk + 1) * LANES] = jnp.where(lo64, even, odd).astype(BF16)


def _dup_rows(x_t):
    xb = x_t.astype(BF16)
    return jnp.concatenate([xb, xb], axis=0)


def _sink_finish(mrun, sk, rows):
    mb = jnp.maximum(jnp.broadcast_to(jnp.max(mrun, axis=-1, keepdims=True), (rows, LANES)), sk)
    return mb, jnp.exp2(sk - mb)


def _swa_prompt_kernel(sink_ref, q_ref, kt_ref, vt_ref, o_ref, *s_bufs):
    seq = kt_ref.shape[2]
    rows = SWA_GROUP * seq
    ones = _ones_row(seq)
    items = [(r, j) for r in range(kt_ref.shape[0]) for j in range(SWA_KV_HEADS)]

    def scores(item, s_ref):
        r, j = item
        kd = _dup_rows(kt_ref[r, j * SWA_HD:(j + 1) * SWA_HD, :])
        s = jnp.dot(_stack_group(q_ref, slice(r * seq, (r + 1) * seq), j), kd, preferred_element_type=F32)
        return _store_scores(s_ref, 0, s, None)

    def finish(item, s_ref, mrun):
        r, j = item
        sk = _sink_column(sink_ref, j * SWA_GROUP, seq)
        mb, sink_term = _sink_finish(mrun, sk, rows)
        vx = jnp.concatenate([_dup_rows(vt_ref[r, j * SWA_HD:(j + 1) * SWA_HD, :]), ones], axis=0)
        acc = lax.dot_general(_exp_block(s_ref, 0, seq, mb), vx, NT_DIMS, preferred_element_type=F32)
        o = acc[:, :LANES] / (acc[:, LANES:] + sink_term)
        _write_group(o_ref, slice(r * seq, (r + 1) * seq), j, o, seq)

    _run_pipelined(items, scores, finish, s_bufs)


def _swa_prompt_attention(q, kt, vt, sink, *, seq):
    n, d = q.shape
    nkv = kt.shape[1]
    req = 2
    t_spec = pl.BlockSpec((req, nkv, seq), lambda b: (b, 0, 0))
    return pl.pallas_call(
        _swa_prompt_kernel,
        out_shape=jax.ShapeDtypeStruct((n, d), BF16),
        grid=(n // (req * seq),),
        in_specs=[pl.BlockSpec(memory_space=pltpu.SMEM),
                  pl.BlockSpec((req * seq, d), lambda b: (b, 0)), t_spec, t_spec],
        out_specs=pl.BlockSpec((req * seq, d), lambda b: (b, 0)),
        scratch_shapes=[pltpu.VMEM((SWA_GROUP * seq, seq), F32)] * (PROMPT_SCORE_BUFFERS // 2),
        compiler_params=_params(1),
        name="swa_attn_prompt",
    )(sink, q, kt, vt)


def _swa_latent_kernel(sink_ref, q_ref, kc_ref, vc_ref, kl_ref, vl_ref, o_ref, kcd_ref, vcx_ref,
                       s0_ref, s1_ref, *, tq, span):
    pair = pl.program_id(1)
    tiles = q_ref.shape[0] // tq
    first_tile = pl.program_id(2) * tiles
    seq = kl_ref.shape[0]
    lc = kc_ref.shape[1]
    rows = SWA_GROUP * tq
    ones_row = _ones_row(lc)
    for jj in range(2):
        kcd_ref[jj] = _dup_rows(kc_ref[jj * SWA_HD:(jj + 1) * SWA_HD, :])
        vcx_ref[jj] = jnp.concatenate([_dup_rows(vc_ref[jj * SWA_HD:(jj + 1) * SWA_HD, :]), ones_row], axis=0)
    ones_col = _ones_column(span)
    items = [(t, jj) for t in range(tiles) for jj in range(2)]
    windows, biases = {}, {}

    def window(t):
        if t not in windows:
            q0 = (first_tile + t) * tq
            windows[t] = (q0, pl.multiple_of(jnp.clip(q0 - WINDOW, 0, seq - span), WINDOW))
        return windows[t]

    def bias_for(t):
        if t not in biases:
            q0, ws = window(t)
            q_pos = q0 + lax.broadcasted_iota(jnp.int32, (tq, span), 0)
            k_pos = ws + lax.broadcasted_iota(jnp.int32, (tq, span), 1)
            b = jnp.where(jnp.abs(q_pos - k_pos) <= WINDOW, 0.0, NEG_INF).astype(F32)
            biases[t] = jnp.concatenate([b] * SWA_GROUP, axis=0)
        return biases[t]

    def scores(item, s_ref):
        t, jj = item
        _, ws = window(t)
        qs = _stack_group(q_ref, slice(t * tq, (t + 1) * tq), jj)
        s_c = jnp.dot(qs, kcd_ref[jj], preferred_element_type=F32)
        mrun = _store_scores(s_ref, 0, s_c, None)
        s_w = lax.dot_general(qs, kl_ref[pl.ds(ws, span), jj * LANES:(jj + 1) * LANES], NT_DIMS,
                              preferred_element_type=F32) + bias_for(t)
        return _store_scores(s_ref, lc, s_w, mrun)

    def finish(item, s_ref, mrun):
        t, jj = item
        _, ws = window(t)
        sk = _sink_column(sink_ref, (2 * pair + jj) * SWA_GROUP, tq)
        mb, sink_term = _sink_finish(mrun, sk, rows)
        vwx = jnp.concatenate([vl_ref[pl.ds(ws, span), jj * LANES:(jj + 1) * LANES], ones_col], axis=1)
        acc = (lax.dot_general(_exp_block(s_ref, 0, lc, mb), vcx_ref[jj], NT_DIMS, preferred_element_type=F32)
               + jnp.dot(_exp_block(s_ref, lc, span, mb), vwx, preferred_element_type=F32))
        o = acc[:, :LANES] / (acc[:, LANES:] + sink_term)
        _write_group(o_ref, slice(t * tq, (t + 1) * tq), jj, o, tq)

    _run_pipelined(items, scores, finish, (s0_ref, s1_ref))


def _swa_latent_attention(q, kd, vd, cache_kt, cache_vt, sink, *, j, seq):
    n, d = q.shape
    nb, _, lc = cache_kt.shape
    tq = 256
    span = tq + 2 * WINDOW
    npair = SWA_KV_HEADS // 2
    wq = d // npair
    parts = 2
    q_spec = pl.BlockSpec((seq // parts, wq), lambda b, p, i: (b * parts + i, p))
    c_spec = pl.BlockSpec((None, 2 * SWA_HD, lc), lambda b, p, i: (b, j * npair + p, 0))
    l_spec = pl.BlockSpec((seq, 2 * LANES), lambda b, p, i: (b, p))
    s_shape = pltpu.VMEM((SWA_GROUP * tq, lc + span), F32)
    return pl.pallas_call(
        functools.partial(_swa_latent_kernel, tq=tq, span=span),
        out_shape=jax.ShapeDtypeStruct((n, d), BF16),
        grid=(nb, npair, parts),
        in_specs=[pl.BlockSpec(memory_space=pltpu.SMEM), q_spec, c_spec, c_spec, l_spec, l_spec],
        out_specs=q_spec,
        scratch_shapes=[pltpu.VMEM((2, 2 * SWA_HD, lc), BF16), pltpu.VMEM((2, 2 * LANES, lc), BF16),
                        s_shape, s_shape],
        compiler_params=_params(3),
        name="swa_attn_latent",
    )(sink, q, cache_kt, cache_vt, kd, vd)


def _load_weights_as_bf16(jobs, stages, sems):
    order = []
    rings = {w: [] for w in stages}
    for src, dst in jobs:
        w = src.shape[1]
        slots = stages[w].shape[0]
        for k in range(src.shape[0] // WEIGHT_STAGE_ROWS):
            rows = pl.ds(k * WEIGHT_STAGE_ROWS, WEIGHT_STAGE_ROWS)
            slot = len(rings[w]) % slots
            copy = pltpu.make_async_copy(src.at[rows, :], stages[w].at[slot], sems[w].at[slot])
            order.append((w, len(rings[w])))
            rings[w].append((copy, slot, dst, rows))
    for w, ring in rings.items():
        for copy, _, _, _ in ring[:stages[w].shape[0]]:
            copy.start()
    for w, k in order:
        copy, slot, dst, rows = rings[w][k]
        copy.wait()
        dst[rows, :] = stages[w][slot].astype(BF16)
        ahead = k + stages[w].shape[0]
        if ahead < len(rings[w]):
            rings[w][ahead][0].start()


def _post_attn_ffn_kernel(*refs, layer, j, n_prompt_tiles, tiles_per_request, split_x, split_out):
    refs = list(refs)
    op_ref, os_ref = refs[:2]
    x_refs = refs[2:4] if split_x else refs[2:3]
    wo_hbm, wg_hbm, wu_hbm, wd_hbm, g_ref, mod_ref = refs[2 + len(x_refs):8 + len(x_refs)]
    n_out = 2 if split_out else 1
    out_refs = refs[8 + len(x_refs):8 + len(x_refs) + n_out]
    wo_ref, wg_ref, wu_ref, wd_ref, stage_d, stage_ff, sem_d, sem_ff = refs[8 + len(x_refs) + n_out:]
    d = wo_ref.shape[1]
    dff = wg_ref.shape[1]
    i = pl.program_id(0)

    @pl.when(i == 0)
    def _():
        _load_weights_as_bf16(
            [(wo_hbm.at[j], wo_ref), (wg_hbm.at[layer], wg_ref), (wu_hbm.at[layer], wu_ref),
             (wd_hbm.at[layer], wd_ref)],
            {d: stage_d, dff: stage_ff}, {d: sem_d, dff: sem_ff})

    is_prompt = i < n_prompt_tiles
    r = jnp.where(is_prompt, 0, 1 + (i - n_prompt_tiles) // tiles_per_request)

    def mod(slot):
        return mod_ref[pl.ds(r, 1), slot * d:(slot + 1) * d]

    o = jnp.where(is_prompt, op_ref[...], os_ref[...])
    x = jnp.where(is_prompt, x_refs[0][...], x_refs[1][...]) if split_x else x_refs[0][...]
    y = jnp.dot(o, wo_ref[...], preferred_element_type=F32)
    x = x + _rms(y, mod(2) * g_ref[1])
    h = (_rms(x, g_ref[2] * (1 + mod(4))) + mod(3)).astype(BF16)
    y = jnp.zeros((h.shape[0], d), F32)
    lo = 0
    while lo < dff:
        hi = min(lo + FFN_CHUNK, dff)
        a = jnp.dot(h, wg_ref[:, lo:hi], preferred_element_type=F32)
        u = jnp.dot(h, wu_ref[:, lo:hi], preferred_element_type=F32)
        t = (a * jax.nn.sigmoid(a)) * u
        y = y + jnp.dot(t.astype(BF16), wd_ref[lo:hi, :], preferred_element_type=F32)
        lo = hi
    out = x + _rms(y, mod(5) * g_ref[3])
    if split_out:
        @pl.when(is_prompt)
        def _():
            out_refs[0][...] = out

        @pl.when(jnp.logical_not(is_prompt))
        def _():
            out_refs[1][...] = out
    else:
        out_refs[0][...] = out


def _post_attn_ffn(o_p, o_s, xs_in, w_o, wg, wu, wd, g, mods, *, layer, j, tiles_per_request, split_out):
    n_p, d = o_p.shape
    n_s = o_s.shape[0]
    tm = ROW_TILE
    tp, ts = n_p // tm, n_s // tm
    prompt_rows = pl.BlockSpec((tm, d), lambda i: (jnp.minimum(i, tp - 1), 0))
    latent_rows = pl.BlockSpec((tm, d), lambda i: (jnp.maximum(i - tp, 0), 0))
    all_rows = pl.BlockSpec((tm, d), lambda i: (i, 0))
    split_x = len(xs_in) == 2
    in_specs = [prompt_rows, latent_rows] + ([prompt_rows, latent_rows] if split_x else [all_rows])
    hbm = pl.BlockSpec(memory_space=pl.ANY)
    in_specs += [hbm, hbm, hbm, hbm, _layer_resident(g.shape, layer), _layer_resident(mods.shape, layer)]
    if split_out:
        out_shape = (jax.ShapeDtypeStruct((n_p, d), F32), jax.ShapeDtypeStruct((n_s, d), F32))
        out_specs = (prompt_rows, latent_rows)
    else:
        out_shape = jax.ShapeDtypeStruct((n_p + n_s, d), F32)
        out_specs = all_rows
    dff = wg.shape[2]
    scratch = [pltpu.VMEM((d, d), BF16), pltpu.VMEM((d, dff), BF16), pltpu.VMEM((d, dff), BF16),
               pltpu.VMEM((dff, d), BF16),
               pltpu.VMEM((WEIGHT_STAGE_SLOTS, WEIGHT_STAGE_ROWS, d), F32),
               pltpu.VMEM((WEIGHT_STAGE_SLOTS, WEIGHT_STAGE_ROWS, dff), F32),
               pltpu.SemaphoreType.DMA((WEIGHT_STAGE_SLOTS,)), pltpu.SemaphoreType.DMA((WEIGHT_STAGE_SLOTS,))]
    return pl.pallas_call(
        functools.partial(_post_attn_ffn_kernel, layer=layer, j=j, n_prompt_tiles=tp,
                          tiles_per_request=tiles_per_request, split_x=split_x, split_out=split_out),
        out_shape=out_shape,
        grid=(tp + ts,),
        in_specs=in_specs,
        out_specs=out_specs,
        scratch_shapes=scratch,
        compiler_params=pltpu.CompilerParams(dimension_semantics=("arbitrary",),
                                             vmem_limit_bytes=FFN_VMEM_LIMIT),
        name="post_attn_ffn",
    )(o_p, o_s, *xs_in, w_o, wg, wu, wd, g, mods)


def _rope_tables(n_lat):
    t = jnp.arange(n_lat)
    row = (t // GRID_W).astype(F32)
    col = (t % GRID_W).astype(F32)
    nf = ROT_DIM // 4
    inv = ROPE_BASE ** (-jnp.arange(nf, dtype=F32) / nf)
    ar = row[:, None] * inv[None, :]
    ac = col[:, None] * inv[None, :]
    ang = jnp.concatenate([ar, ar, ac, ac], axis=-1)
    cos, sin = jnp.cos(ang), jnp.sin(ang)
    sign = jnp.where((jnp.arange(ROT_DIM) % 32) < 16, -1.0, 1.0).astype(F32)
    reps = LANES // ROT_DIM
    return jnp.tile(cos, (1, reps)), jnp.tile(sin * sign, (1, reps))


def _swa_cache_to_feature_major(cache):
    nb, nl, lc, nh, hd = cache.shape
    return cache.transpose(0, 1, 3, 4, 2).reshape(nb, nl * nh * hd, lc)


def _swa_cache_from_feature_major(xt, seq):
    nb = xt.shape[0]
    return xt.reshape(nb, SWA_KV_HEADS, SWA_HD, seq).transpose(0, 3, 1, 2)


def kernel(x_prompt, x_sample, cache_diff_k, cache_diff_v, cache_swa_k, cache_swa_v, c, c_ctx,
           w_mod, b_mod, norm_g, w_qkv_diff, diff_lambda, diff_subln_g, w_o_diff,
           w_qkv_swa, swa_sink, w_o_swa, w_gate, w_up, w_down):
    bp, lp, d = x_prompt.shape
    bs, ls, _ = x_sample.shape
    lc = cache_diff_k.shape[2]
    depth = w_mod.shape[0]
    tm = ROW_TILE

    cond8 = jnp.concatenate([c_ctx[None, :], c, jnp.zeros((8 - 1 - bs, d), F32)], axis=0)
    mods = _modulation(cond8, w_mod, b_mod)
    tables = _rope_tables(ls)

    cdk = cache_diff_k.reshape(bs, -1, 2 * DIFF_HD)
    cdv = cache_diff_v.reshape(bs, -1, 2 * DIFF_HD)
    cskt = _swa_cache_to_feature_major(cache_swa_k)
    csvt = _swa_cache_to_feature_major(cache_swa_v)

    n_p, n_s = bp * lp, bs * ls
    x_parts = (x_prompt.reshape(n_p, d), x_sample.reshape(n_s, d))
    g = norm_g.reshape(depth, 4, 1, d)
    sub_g = diff_subln_g.reshape(-1, 1, 2 * DIFF_HD)
    p_rows = dict(row0=0, tiles_per_batch=n_p // QKV_ROW_TILE)
    s_rows = dict(row0=1, tiles_per_batch=ls // QKV_ROW_TILE)
    diff_k_out, diff_v_out, swa_k_out, swa_v_out = [], [], [], []

    for i in range(depth):
        j = i // N_MIXERS
        if len(x_parts) == 2:
            p_src, s_src = (x_parts[0], n_p, 0), (x_parts[1], n_s, 0)
        else:
            p_src, s_src = (x_parts[0], n_p, 0), (x_parts[0], n_s, n_p // QKV_ROW_TILE)
        if i % N_MIXERS == 0:
            lam_init = 0.8 - 0.6 * math.exp(-0.3 * i)
            qp, kp, vp, wb = _qkv_diff(*p_src, g, mods, w_qkv_diff, None, layer=i, j=j, latent=False,
                                       **p_rows)
            op = _diff_prompt_attention(qp, kp, vp, diff_lambda, sub_g, j=j, seq=lp, lam_init=lam_init)
            diff_k_out.append(kp.reshape(bp, lp, DIFF_HEADS, 2 * DIFF_HD))
            diff_v_out.append(vp.reshape(bp, lp, DIFF_HEADS, 2 * DIFF_HD))
            qs, ks, vs = _qkv_diff(*s_src, g, mods, wb, tables, layer=i, j=j, latent=True, **s_rows)
            os_ = _diff_latent_attention(qs, ks, vs, cdk, cdv, diff_lambda, sub_g,
                                         j=j, seq=ls, lc=lc, lam_init=lam_init)
            w_o = w_o_diff
        else:
            qp, ktp, vtp, wb = _qkv_swa(*p_src, g, mods, w_qkv_swa, None, layer=i, j=j, latent=False,
                                        seq=lp, **p_rows)
            op = _swa_prompt_attention(qp, ktp, vtp, swa_sink[j], seq=lp)
            swa_k_out.append(_swa_cache_from_feature_major(ktp, lp))
            swa_v_out.append(_swa_cache_from_feature_major(vtp, lp))
            qs, kds, vds = _qkv_swa(*s_src, g, mods, wb, tables, layer=i, j=j, latent=True,
                                    seq=ls, **s_rows)
            os_ = _swa_latent_attention(qs, kds, vds, cskt, csvt, swa_sink[j], j=j, seq=ls)
            w_o = w_o_swa
        last = i == depth - 1
        out = _post_attn_ffn(op, os_, x_parts, w_o, w_gate, w_up, w_down, g, mods, layer=i, j=j,
                             tiles_per_request=ls // tm, split_out=last)
        x_parts = out if last else (out,)
    xp, xs = x_parts

    return (xp.reshape(bp, lp, d), xs.reshape(bs, ls, d),
            jnp.stack(diff_k_out, axis=1), jnp.stack(diff_v_out, axis=1),
            jnp.stack(swa_k_out, axis=1), jnp.stack(swa_v_out, axis=1))
```

```python
import functools
import math

import jax
import jax.numpy as jnp
import numpy as np
from jax import lax
from jax.experimental import pallas as pl
from jax.experimental.pallas import tpu as pltpu

F32 = jnp.float32
BF16 = jnp.bfloat16

GRID_W = 64
N_MIXERS = 2
DIFF_HEADS = 8
DIFF_HD = 64
SWA_HEADS = 16
SWA_KV_HEADS = 4
SWA_GROUP = SWA_HEADS // SWA_KV_HEADS
SWA_HD = 64
ROT_DIM = 64
WINDOW = 128
ROPE_BASE = 10000.0
EPS = 1e-6
NEG_INF = -1e30

LANES = 128
SUBLANES = 8
ROW_TILE = 512
QKV_ROW_TILE = 512
PROMPT_SCORE_BUFFERS = 8
LATENT_SCORE_BUFFERS = 2
VMEM_LIMIT = 48 * 1024 * 1024
FFN_VMEM_LIMIT = 58 * 1024 * 1024
WEIGHT_STAGE_ROWS = 128
WEIGHT_STAGE_SLOTS = 6
QKV_STAGE_SLOTS = 4
FFN_CHUNK = 256
NT_DIMS = (((1,), (1,)), ((), ()))
LOG2E = math.log2(math.e)


def _params(n_axes):
    return pltpu.CompilerParams(dimension_semantics=("arbitrary",) * n_axes,
                                vmem_limit_bytes=VMEM_LIMIT)


def _resident(shape):
    return pl.BlockSpec(shape, lambda *_: (0,) * len(shape), pipeline_mode=pl.Buffered(1))


def _layer_resident(shape, layer):
    return pl.BlockSpec((None,) + tuple(shape[1:]), lambda *_: (layer,) + (0,) * (len(shape) - 1),
                        pipeline_mode=pl.Buffered(1))


def _rms(x, g):
    ms = jnp.mean(x * x, axis=-1, keepdims=True)
    return (x * lax.rsqrt(ms + EPS)) * g


def _half_masks(dtype):
    lane = lax.broadcasted_iota(jnp.int32, (1, LANES), 1)
    lo = lane < (LANES // 2)
    return jnp.where(lo, 1.0, 0.0).astype(dtype), jnp.where(lo, 0.0, 1.0).astype(dtype)


def _lo64():
    return lax.broadcasted_iota(jnp.int32, (1, LANES), 1) < (LANES // 2)


def _ones_column(rows):
    return jnp.ones((rows, LANES), BF16)


def _ones_row(cols):
    return jnp.ones((LANES, cols), BF16)


def _mod_kernel(cond_ref, w_ref, b_ref, out_ref):
    c = cond_ref[...]
    s = c * jax.nn.sigmoid(c)
    out_ref[...] = jnp.dot(s.astype(BF16), w_ref[...].astype(BF16),
                           preferred_element_type=F32) + b_ref[pl.ds(pl.program_id(0), 1), :]


def _modulation(cond8, w_mod, b_mod):
    depth, d, n = w_mod.shape
    tn = 1536
    return pl.pallas_call(
        _mod_kernel,
        out_shape=jax.ShapeDtypeStruct((depth, 8, n), F32),
        grid=(depth, n // tn),
        in_specs=[pl.BlockSpec((8, d), lambda i, j: (0, 0)),
                  pl.BlockSpec((None, d, tn), lambda i, j: (i, 0, j)),
                  pl.BlockSpec((depth, tn), lambda i, j: (0, j))],
        out_specs=pl.BlockSpec((None, 8, tn), lambda i, j: (i, 0, j)),
        compiler_params=_params(2),
        name="modulation",
    )(cond8, w_mod, b_mod)


def _mod_row(row0, tiles_per_batch):
    return row0 + pl.program_id(0) // tiles_per_batch


def _rope_slab(xs, cos, sin_signed, lo16):
    left = pltpu.roll(xs, LANES - 16, 1)
    right = pltpu.roll(xs, 16, 1)
    return xs * cos + jnp.where(lo16, left, right) * sin_signed


def _lo16_mask():
    lane = lax.broadcasted_iota(jnp.int32, (1, LANES), 1)
    return (lane % 32) < 16


def _stage_and_export_weight(w_hbm, wb_out, w_vmem, stage, sem, sem_out, n_steps):
    i = pl.program_id(0)
    export = pltpu.make_async_copy(w_vmem, wb_out, sem_out)

    @pl.when(i == 0)
    def _():
        _load_weights_as_bf16([(w_hbm, w_vmem)], {w_hbm.shape[1]: stage}, {w_hbm.shape[1]: sem})
        export.start()

    @pl.when(i == n_steps - 1)
    def _():
        export.wait()


def _pre_norm(x_ref, g_ref, mod_ref, r, d, slot):
    shift = mod_ref[pl.ds(r, 1), slot * d:(slot + 1) * d]
    scale = mod_ref[pl.ds(r, 1), (slot + 1) * d:(slot + 2) * d]
    return _rms(x_ref[...], g_ref[0:1, :] * (1 + scale)) + shift


def _qkv_diff_kernel(x_ref, g_ref, mod_ref, w_ref, *rest, latent, j, n_steps, row0, tiles_per_batch):
    d = x_ref.shape[1]
    if latent:
        cos_ref, sin_ref, q_ref, k_ref, v_ref = rest
        cos, sin = cos_ref[...], sin_ref[...]
        lo16 = _lo16_mask()
    else:
        q_ref, k_ref, v_ref, wb_out, w_vmem, stage, sem, sem_out = rest
        _stage_and_export_weight(w_ref.at[j], wb_out, w_vmem, stage, sem, sem_out, n_steps)
        w_ref = w_vmem
    r = _mod_row(row0, tiles_per_batch)
    h = _pre_norm(x_ref, g_ref, mod_ref, r, d, 0).astype(BF16)
    cw = 512
    for c in range(3 * d // cw):
        acc = jnp.dot(h, w_ref[:, c * cw:(c + 1) * cw], preferred_element_type=F32)
        which, off = divmod(c * cw, d)
        dst = (q_ref, k_ref, v_ref)[which]
        for s in range(cw // LANES):
            xs = acc[:, s * LANES:(s + 1) * LANES]
            if latent and which < 2:
                xs = _rope_slab(xs, cos, sin, lo16)
            if which == 0:
                xs = xs * (DIFF_HD ** -0.5 * LOG2E)
            lo = off + s * LANES
            if latent or which == 0:
                dst[:, lo:lo + LANES] = xs.astype(dst.dtype)
            else:
                dst[pl.ds(lo // LANES, x_ref.shape[0], stride=DIFF_HEADS), :] = xs


def _qkv_diff(x, n, tile0, g, mods, w, tables, *, layer, j, latent, row0, tiles_per_batch):
    d = x.shape[1]
    tm = QKV_ROW_TILE
    w_spec, w_out_shapes, w_out_specs, scratch = _qkv_weight_plumbing(w, j, latent)
    in_specs = [pl.BlockSpec((tm, d), lambda i: (i + tile0, 0)), _layer_resident(g.shape, layer),
                _layer_resident(mods.shape, layer), w_spec]
    args = [x, g, mods, w]
    kv_dtype = F32
    if latent:
        nt = tables[0].shape[0] // tm
        in_specs += [pl.BlockSpec((tm, LANES), lambda i: (i % nt, 0))] * 2
        args += list(tables)
        kv_dtype = BF16
    out_spec = pl.BlockSpec((tm, d), lambda i: (i, 0))
    kv_shape, kv_spec = jax.ShapeDtypeStruct((n, d), kv_dtype), out_spec
    if not latent:
        kv_shape = jax.ShapeDtypeStruct((n * DIFF_HEADS, d // DIFF_HEADS), kv_dtype)
        kv_spec = pl.BlockSpec((tm * DIFF_HEADS, d // DIFF_HEADS), lambda i: (i, 0))
    return pl.pallas_call(
        functools.partial(_qkv_diff_kernel, latent=latent, j=j, n_steps=n // tm, row0=row0,
                          tiles_per_batch=tiles_per_batch),
        out_shape=(jax.ShapeDtypeStruct((n, d), BF16), kv_shape, kv_shape, *w_out_shapes),
        grid=(n // tm,),
        in_specs=in_specs,
        out_specs=(out_spec, kv_spec, kv_spec, *w_out_specs),
        scratch_shapes=scratch,
        compiler_params=_params(1),
        name="qkv_diff_latent" if latent else "qkv_diff_prompt",
    )(*args)


def _qkv_weight_plumbing(w, j, latent):
    if latent:
        return _resident(w.shape), [], [], []
    _, d, cols = w.shape
    hbm = pl.BlockSpec(memory_space=pl.ANY)
    scratch = [pltpu.VMEM((d, cols), BF16), pltpu.VMEM((QKV_STAGE_SLOTS, WEIGHT_STAGE_ROWS, cols), F32),
               pltpu.SemaphoreType.DMA((QKV_STAGE_SLOTS,)), pltpu.SemaphoreType.DMA(())]
    return hbm, [jax.ShapeDtypeStruct((d, cols), BF16)], [hbm], scratch


def _qkv_swa_kernel(x_ref, g_ref, mod_ref, w_ref, *rest, latent, j, n_steps, row0, tiles_per_batch, seq):
    d = x_ref.shape[1]
    nkv = SWA_KV_HEADS * SWA_HD
    if latent:
        cos_ref, sin_ref, q_ref, kd_ref, vd_ref = rest
        cos, sin = cos_ref[...], sin_ref[...]
        lo16 = _lo16_mask()
    else:
        q_ref, kt_ref, vt_ref, wb_out, w_vmem, stage, sem, sem_out = rest
        _stage_and_export_weight(w_ref.at[j], wb_out, w_vmem, stage, sem, sem_out, n_steps)
        w_ref = w_vmem
    lo64 = _lo64()
    r = _mod_row(row0, tiles_per_batch)
    h = _pre_norm(x_ref, g_ref, mod_ref, r, d, 0).astype(BF16)
    cw = 512
    for c in range(d // cw):
        acc = jnp.dot(h, w_ref[:, c * cw:(c + 1) * cw], preferred_element_type=F32)
        for s in range(cw // LANES):
            xs = acc[:, s * LANES:(s + 1) * LANES]
            if latent:
                xs = _rope_slab(xs, cos, sin, lo16)
            lo = c * cw + s * LANES
            q_ref[:, lo:lo + LANES] = (xs * (SWA_HD ** -0.5 * LOG2E)).astype(BF16)
    kv = jnp.dot(h, w_ref[:, d:d + 2 * nkv], preferred_element_type=F32)
    if not latent:
        for b in range(x_ref.shape[0] // seq):
            kt_ref[b] = kv[b * seq:(b + 1) * seq, :nkv].T
            vt_ref[b] = kv[b * seq:(b + 1) * seq, nkv:].T
        return
    for which, dst in enumerate((kd_ref, vd_ref)):
        for s in range(nkv // LANES):
            xs = kv[:, which * nkv + s * LANES: which * nkv + (s + 1) * LANES]
            if which == 0:
                xs = _rope_slab(xs, cos, sin, lo16)
            sw = pltpu.roll(xs, LANES // 2, 1)
            dst[:, (2 * s) * LANES:(2 * s + 1) * LANES] = jnp.where(lo64, xs, sw).astype(BF16)
            dst[:, (2 * s + 1) * LANES:(2 * s + 2) * LANES] = jnp.where(lo64, sw, xs).astype(BF16)


def _qkv_swa(x, n, tile0, g, mods, w, tables, *, layer, j, latent, row0, tiles_per_batch, seq):
    d = x.shape[1]
    tm = QKV_ROW_TILE
    nkv = SWA_KV_HEADS * SWA_HD
    w_spec, w_out_shapes, w_out_specs, scratch = _qkv_weight_plumbing(w, j, latent)
    in_specs = [pl.BlockSpec((tm, d), lambda i: (i + tile0, 0)), _layer_resident(g.shape, layer),
                _layer_resident(mods.shape, layer), w_spec]
    args = [x, g, mods, w]
    out_shape = [jax.ShapeDtypeStruct((n, d), BF16)]
    out_specs = [pl.BlockSpec((tm, d), lambda i: (i, 0))]
    if latent:
        nt = tables[0].shape[0] // tm
        in_specs += [pl.BlockSpec((tm, LANES), lambda i: (i % nt, 0))] * 2
        args += list(tables)
        out_shape += [jax.ShapeDtypeStruct((n, 2 * nkv), BF16)] * 2
        out_specs += [pl.BlockSpec((tm, 2 * nkv), lambda i: (i, 0))] * 2
    else:
        out_shape += [jax.ShapeDtypeStruct((n // seq, nkv, seq), F32)] * 2
        out_specs += [pl.BlockSpec((tm // seq, nkv, seq), lambda i: (i, 0, 0))] * 2
    return pl.pallas_call(
        functools.partial(_qkv_swa_kernel, latent=latent, j=j, n_steps=n // tm, row0=row0,
                          tiles_per_batch=tiles_per_batch, seq=seq),
        out_shape=(*out_shape, *w_out_shapes),
        grid=(n // tm,),
        in_specs=in_specs,
        out_specs=(*out_specs, *w_out_specs),
        scratch_shapes=scratch,
        compiler_params=_params(1),
        name="qkv_swa_latent" if latent else "qkv_swa_prompt",
    )(*args)


def _diff_lambda(lam_ref, lam_init):
    lp = lam_ref[...]
    a = jnp.sum(lp[0:1] * lp[1:2], axis=-1, keepdims=True)
    b = jnp.sum(lp[2:3] * lp[3:4], axis=-1, keepdims=True)
    return jnp.exp(a) - jnp.exp(b) + lam_init


def _diff_combine(acc, tq, lam, g, lam_init):
    o12 = acc[:, :LANES] / acc[:, LANES:]
    o = o12[:tq] - lam * o12[tq:]
    return _rms(o, g) * (1.0 - lam_init)


def _stack_maps(q):
    m_lo, m_hi = _half_masks(BF16)
    return jnp.concatenate([q * m_lo, q * m_hi], axis=0)


def _run_pipelined(items, scores, finish, s_bufs):
    depth = len(s_bufs)
    states = {i: scores(items[i], s_bufs[i]) for i in range(min(depth - 1, len(items)))}
    for i, item in enumerate(items):
        ahead = i + depth - 1
        if ahead < len(items):
            states[ahead] = scores(items[ahead], s_bufs[ahead % depth])
        finish(item, s_bufs[i % depth], states.pop(i))


def _store_scores(s_ref, col0, s, mrun):
    s_ref[:, col0:col0 + s.shape[1]] = s
    for t in range(s.shape[1] // LANES):
        blk = s[:, t * LANES:(t + 1) * LANES]
        mrun = blk if mrun is None else jnp.maximum(mrun, blk)
    return mrun


def _exp_block(s_ref, col0, width, mb):
    return jnp.concatenate(
        [jnp.exp2(s_ref[:, col0 + t * LANES:col0 + (t + 1) * LANES] - mb).astype(BF16)
         for t in range(width // LANES)], axis=1)


def _diff_prompt_kernel(q_ref, k_ref, v_ref, lam_ref, g_ref, o_ref, *s_bufs, lam_init, seq):
    lam = _diff_lambda(lam_ref, lam_init)
    g = g_ref[...]
    ones = _ones_column(seq)
    items = [(r, h) for r in range(q_ref.shape[0] // seq) for h in range(DIFF_HEADS)]

    def head_rows(ref, r, h):
        return ref[pl.ds(r * seq * DIFF_HEADS + h, seq, stride=DIFF_HEADS), :]

    def scores(item, s_ref):
        r, h = item
        rows, sl = slice(r * seq, (r + 1) * seq), slice(h * LANES, (h + 1) * LANES)
        s = lax.dot_general(_stack_maps(q_ref[rows, sl]), head_rows(k_ref, r, h).astype(BF16), NT_DIMS,
                            preferred_element_type=F32)
        return _store_scores(s_ref, 0, s, None)

    def finish(item, s_ref, mrun):
        r, h = item
        rows, sl = slice(r * seq, (r + 1) * seq), slice(h * LANES, (h + 1) * LANES)
        mb = jnp.broadcast_to(jnp.max(mrun, axis=-1, keepdims=True), (2 * seq, LANES))
        vx = jnp.concatenate([head_rows(v_ref, r, h).astype(BF16), ones], axis=1)
        acc = jnp.dot(_exp_block(s_ref, 0, seq, mb), vx, preferred_element_type=F32)
        o_ref[rows, sl] = _diff_combine(acc, seq, lam, g, lam_init).astype(BF16)

    _run_pipelined(items, scores, finish, s_bufs)


def _diff_prompt_attention(q, k, v, lam_params, subln_g, *, j, seq, lam_init):
    n, d = q.shape
    req = 2
    spec = pl.BlockSpec((req * seq, d), lambda b: (b, 0))
    kv_spec = pl.BlockSpec((req * seq * DIFF_HEADS, d // DIFF_HEADS), lambda b: (b, 0))
    return pl.pallas_call(
        functools.partial(_diff_prompt_kernel, lam_init=lam_init, seq=seq),
        out_shape=jax.ShapeDtypeStruct((n, d), BF16),
        grid=(n // (req * seq),),
        in_specs=[spec, kv_spec, kv_spec, _layer_resident(lam_params.shape, j),
                  _layer_resident(subln_g.shape, j)],
        out_specs=spec,
        scratch_shapes=[pltpu.VMEM((2 * seq, seq), F32)] * PROMPT_SCORE_BUFFERS,
        compiler_params=_params(1),
        name="diff_attn_prompt",
    )(q, k, v, lam_params, subln_g)


def _diff_latent_kernel(q_ref, kc_ref, vc_ref, kl_ref, vl_ref, lam_ref, g_ref, o_ref,
                        kk_ref, vx_ref, *s_bufs, lam_init, lc, tq, key_chunk):
    seq = q_ref.shape[0]
    heads = kk_ref.shape[0]
    nk = kk_ref.shape[1]
    for hh in range(heads):
        h = pl.program_id(1) * heads + hh
        sl = slice(hh * LANES, (hh + 1) * LANES)
        kk_ref[hh, 0:lc, :] = kc_ref[pl.ds(h, lc, stride=DIFF_HEADS), :].astype(BF16)
        kk_ref[hh, lc:, :] = kl_ref[:, sl]
        vx_ref[hh, 0:lc, 0:LANES] = vc_ref[pl.ds(h, lc, stride=DIFF_HEADS), :].astype(BF16)
        vx_ref[hh, lc:, 0:LANES] = vl_ref[:, sl]
        vx_ref[hh, :, LANES:2 * LANES] = _ones_column(nk)

    masks = _half_masks(BF16)
    nchunk = nk // key_chunk
    items = [(hh, rt, m) for hh in range(heads) for rt in range(seq // tq) for m in range(2)]
    lam = _diff_lambda(lam_ref, lam_init)
    g = g_ref[...]
    first_map = {}

    def scores(item, s_ref):
        hh, rt, m = item
        q = q_ref[rt * tq:(rt + 1) * tq, hh * LANES:(hh + 1) * LANES] * masks[m]
        mrun = None
        for c in range(nchunk):
            s = lax.dot_general(q, kk_ref[hh, c * key_chunk:(c + 1) * key_chunk, :], NT_DIMS,
                                preferred_element_type=F32)
            mrun = _store_scores(s_ref, c * key_chunk, s, mrun)
        return mrun

    def finish(item, s_ref, mrun):
        hh, rt, m = item
        mb = jnp.broadcast_to(jnp.max(mrun, axis=-1, keepdims=True), (tq, LANES))
        acc = None
        for c in range(nchunk):
            part = jnp.dot(_exp_block(s_ref, c * key_chunk, key_chunk, mb),
                           vx_ref[hh, c * key_chunk:(c + 1) * key_chunk, :], preferred_element_type=F32)
            acc = part if acc is None else acc + part
        o_m = acc[:, :LANES] / acc[:, LANES:]
        if m == 0:
            first_map[hh, rt] = o_m
        else:
            o = _rms(first_map.pop((hh, rt)) - lam * o_m, g) * (1.0 - lam_init)
            o_ref[rt * tq:(rt + 1) * tq, hh * LANES:(hh + 1) * LANES] = o.astype(BF16)

    _run_pipelined(items, scores, finish, s_bufs)


def _diff_latent_attention(q, k, v, cache_k, cache_v, lam_params, subln_g, *, j, seq, lc, lam_init):
    n, d = q.shape
    nb = cache_k.shape[0]
    tq = 512
    key_chunk = 512
    heads = 1
    q_spec = pl.BlockSpec((seq, heads * LANES), lambda b, h: (b, h))
    c_spec = pl.BlockSpec((None, lc * DIFF_HEADS, LANES), lambda b, h: (b, j, 0))
    return pl.pallas_call(
        functools.partial(_diff_latent_kernel, lam_init=lam_init, lc=lc, tq=tq, key_chunk=key_chunk),
        out_shape=jax.ShapeDtypeStruct((n, d), BF16),
        grid=(nb, DIFF_HEADS // heads),
        in_specs=[q_spec, c_spec, c_spec, q_spec, q_spec,
                  _layer_resident(lam_params.shape, j), _layer_resident(subln_g.shape, j)],
        out_specs=q_spec,
        scratch_shapes=[pltpu.VMEM((heads, lc + seq, LANES), BF16),
                        pltpu.VMEM((heads, lc + seq, 2 * LANES), BF16),
                        *[pltpu.VMEM((tq, lc + seq), F32)] * LATENT_SCORE_BUFFERS],
        compiler_params=_params(2),
        name="diff_attn_latent",
    )(q, cache_k, cache_v, k, v, lam_params, subln_g)


def _stack_group(q_ref, rows, kv_local):
    m_lo, m_hi = _half_masks(BF16)
    parts = []
    for gb in range(SWA_GROUP // 2):
        blk = kv_local * (SWA_GROUP // 2) + gb
        qb = q_ref[rows, blk * LANES:(blk + 1) * LANES]
        parts += [qb * m_lo, qb * m_hi]
    return jnp.concatenate(parts, axis=0)


def _sink_column(sink_ref, first_head, tq):
    return jnp.concatenate([jnp.full((tq, LANES), sink_ref[first_head + g] * LOG2E, F32)
                            for g in range(SWA_GROUP)], axis=0)


def _write_group(o_ref, rows, kv_local, o, tq):
    lo64 = _lo64()
    for gb in range(SWA_GROUP // 2):
        blk = kv_local * (SWA_GROUP // 2) + gb
        even = o[(2 * gb) * tq:(2 * gb + 1) * tq]
        odd = o[(2 * gb + 1) * tq:(2 * gb + 2) * tq]
        o_ref[rows, blk * LANES:(blk + 1) * LANES] = jnp.where(lo64, even, odd).astype(BF16)


def _dup_rows(x_t):
    xb = x_t.astype(BF16)
    return jnp.concatenate([xb, xb], axis=0)


def _sink_finish(mrun, sk, rows):
    mb = jnp.maximum(jnp.broadcast_to(jnp.max(mrun, axis=-1, keepdims=True), (rows, LANES)), sk)
    return mb, jnp.exp2(sk - mb)


def _swa_prompt_kernel(sink_ref, q_ref, kt_ref, vt_ref, o_ref, *s_bufs):
    seq = kt_ref.shape[2]
    rows = SWA_GROUP * seq
    ones = _ones_row(seq)
    items = [(r, j) for r in range(kt_ref.shape[0]) for j in range(SWA_KV_HEADS)]

    def scores(item, s_ref):
        r, j = item
        kd = _dup_rows(kt_ref[r, j * SWA_HD:(j + 1) * SWA_HD, :])
        s = jnp.dot(_stack_group(q_ref, slice(r * seq, (r + 1) * seq), j), kd, preferred_element_type=F32)
        return _store_scores(s_ref, 0, s, None)

    def finish(item, s_ref, mrun):
        r, j = item
        sk = _sink_column(sink_ref, j * SWA_GROUP, seq)
        mb, sink_term = _sink_finish(mrun, sk, rows)
        vx = jnp.concatenate([_dup_rows(vt_ref[r, j * SWA_HD:(j + 1) * SWA_HD, :]), ones], axis=0)
        acc = lax.dot_general(_exp_block(s_ref, 0, seq, mb), vx, NT_DIMS, preferred_element_type=F32)
        o = acc[:, :LANES] / (acc[:, LANES:] + sink_term)
        _write_group(o_ref, slice(r * seq, (r + 1) * seq), j, o, seq)

    _run_pipelined(items, scores, finish, s_bufs)


def _swa_prompt_attention(q, kt, vt, sink, *, seq):
    n, d = q.shape
    nkv = kt.shape[1]
    req = 2
    t_spec = pl.BlockSpec((req, nkv, seq), lambda b: (b, 0, 0))
    return pl.pallas_call(
        _swa_prompt_kernel,
        out_shape=jax.ShapeDtypeStruct((n, d), BF16),
        grid=(n // (req * seq),),
        in_specs=[pl.BlockSpec(memory_space=pltpu.SMEM),
                  pl.BlockSpec((req * seq, d), lambda b: (b, 0)), t_spec, t_spec],
        out_specs=pl.BlockSpec((req * seq, d), lambda b: (b, 0)),
        scratch_shapes=[pltpu.VMEM((SWA_GROUP * seq, seq), F32)] * (PROMPT_SCORE_BUFFERS // 2),
        compiler_params=_params(1),
        name="swa_attn_prompt",
    )(sink, q, kt, vt)


def _swa_latent_kernel(sink_ref, q_ref, kc_ref, vc_ref, kl_ref, vl_ref, o_ref, kcd_ref, vcx_ref,
                       s0_ref, s1_ref, *, tq, span):
    pair = pl.program_id(1)
    tiles = q_ref.shape[0] // tq
    first_tile = pl.program_id(2) * tiles
    seq = kl_ref.shape[0]
    lc = kc_ref.shape[1]
    rows = SWA_GROUP * tq
    ones_row = _ones_row(lc)
    for jj in range(2):
        kcd_ref[jj] = _dup_rows(kc_ref[jj * SWA_HD:(jj + 1) * SWA_HD, :])
        vcx_ref[jj] = jnp.concatenate([_dup_rows(vc_ref[jj * SWA_HD:(jj + 1) * SWA_HD, :]), ones_row], axis=0)
    ones_col = _ones_column(span)
    items = [(t, jj) for t in range(tiles) for jj in range(2)]
    windows, biases = {}, {}

    def window(t):
        if t not in windows:
            q0 = (first_tile + t) * tq
            windows[t] = (q0, pl.multiple_of(jnp.clip(q0 - WINDOW, 0, seq - span), WINDOW))
        return windows[t]

    def bias_for(t):
        if t not in biases:
            q0, ws = window(t)
            q_pos = q0 + lax.broadcasted_iota(jnp.int32, (tq, span), 0)
            k_pos = ws + lax.broadcasted_iota(jnp.int32, (tq, span), 1)
            b = jnp.where(jnp.abs(q_pos - k_pos) <= WINDOW, 0.0, NEG_INF).astype(F32)
            biases[t] = jnp.concatenate([b] * SWA_GROUP, axis=0)
        return biases[t]

    def scores(item, s_ref):
        t, jj = item
        _, ws = window(t)
        qs = _stack_group(q_ref, slice(t * tq, (t + 1) * tq), jj)
        s_c = jnp.dot(qs, kcd_ref[jj], preferred_element_type=F32)
        mrun = _store_scores(s_ref, 0, s_c, None)
        s_w = lax.dot_general(qs, kl_ref[pl.ds(ws, span), jj * LANES:(jj + 1) * LANES], NT_DIMS,
                              preferred_element_type=F32) + bias_for(t)
        return _store_scores(s_ref, lc, s_w, mrun)

    def finish(item, s_ref, mrun):
        t, jj = item
        _, ws = window(t)
        sk = _sink_column(sink_ref, (2 * pair + jj) * SWA_GROUP, tq)
        mb, sink_term = _sink_finish(mrun, sk, rows)
        vwx = jnp.concatenate([vl_ref[pl.ds(ws, span), jj * LANES:(jj + 1) * LANES], ones_col], axis=1)
        acc = (lax.dot_general(_exp_block(s_ref, 0, lc, mb), vcx_ref[jj], NT_DIMS, preferred_element_type=F32)
               + jnp.dot(_exp_block(s_ref, lc, span, mb), vwx, preferred_element_type=F32))
        o = acc[:, :LANES] / (acc[:, LANES:] + sink_term)
        _write_group(o_ref, slice(t * tq, (t + 1) * tq), jj, o, tq)

    _run_pipelined(items, scores, finish, (s0_ref, s1_ref))


def _swa_latent_attention(q, kd, vd, cache_kt, cache_vt, sink, *, j, seq):
    n, d = q.shape
    nb, _, lc = cache_kt.shape
    tq = 256
    span = tq + 2 * WINDOW
    npair = SWA_KV_HEADS // 2
    wq = d // npair
    parts = 2
    q_spec = pl.BlockSpec((seq // parts, wq), lambda b, p, i: (b * parts + i, p))
    c_spec = pl.BlockSpec((None, 2 * SWA_HD, lc), lambda b, p, i: (b, j * npair + p, 0))
    l_spec = pl.BlockSpec((seq, 2 * LANES), lambda b, p, i: (b, p))
    s_shape = pltpu.VMEM((SWA_GROUP * tq, lc + span), F32)
    return pl.pallas_call(
        functools.partial(_swa_latent_kernel, tq=tq, span=span),
        out_shape=jax.ShapeDtypeStruct((n, d), BF16),
        grid=(nb, npair, parts),
        in_specs=[pl.BlockSpec(memory_space=pltpu.SMEM), q_spec, c_spec, c_spec, l_spec, l_spec],
        out_specs=q_spec,
        scratch_shapes=[pltpu.VMEM((2, 2 * SWA_HD, lc), BF16), pltpu.VMEM((2, 2 * LANES, lc), BF16),
                        s_shape, s_shape],
        compiler_params=_params(3),
        name="swa_attn_latent",
    )(sink, q, cache_kt, cache_vt, kd, vd)


def _load_weights_as_bf16(jobs, stages, sems):
    order = []
    rings = {w: [] for w in stages}
    for src, dst in jobs:
        w = src.shape[1]
        slots = stages[w].shape[0]
        for k in range(src.shape[0] // WEIGHT_STAGE_ROWS):
            rows = pl.ds(k * WEIGHT_STAGE_ROWS, WEIGHT_STAGE_ROWS)
            slot = len(rings[w]) % slots
            copy = pltpu.make_async_copy(src.at[rows, :], stages[w].at[slot], sems[w].at[slot])
            order.append((w, len(rings[w])))
            rings[w].append((copy, slot, dst, rows))
    for w, ring in rings.items():
        for copy, _, _, _ in ring[:stages[w].shape[0]]:
            copy.start()
    for w, k in order:
        copy, slot, dst, rows = rings[w][k]
        copy.wait()
        dst[rows, :] = stages[w][slot].astype(BF16)
        ahead = k + stages[w].shape[0]
        if ahead < len(rings[w]):
            rings[w][ahead][0].start()


def _post_attn_ffn_kernel(*refs, layer, j, n_prompt_tiles, tiles_per_request, split_x, split_out):
    refs = list(refs)
    op_ref, os_ref = refs[:2]
    x_refs = refs[2:4] if split_x else refs[2:3]
    wo_hbm, wg_hbm, wu_hbm, wd_hbm, g_ref, mod_ref = refs[2 + len(x_refs):8 + len(x_refs)]
    n_out = 2 if split_out else 1
    out_refs = refs[8 + len(x_refs):8 + len(x_refs) + n_out]
    wo_ref, wg_ref, wu_ref, wd_ref, stage_d, stage_ff, sem_d, sem_ff = refs[8 + len(x_refs) + n_out:]
    d = wo_ref.shape[1]
    dff = wg_ref.shape[1]
    i = pl.program_id(0)

    @pl.when(i == 0)
    def _():
        _load_weights_as_bf16(
            [(wo_hbm.at[j], wo_ref), (wg_hbm.at[layer], wg_ref), (wu_hbm.at[layer], wu_ref),
             (wd_hbm.at[layer], wd_ref)],
            {d: stage_d, dff: stage_ff}, {d: sem_d, dff: sem_ff})

    is_prompt = i < n_prompt_tiles
    r = jnp.where(is_prompt, 0, 1 + (i - n_prompt_tiles) // tiles_per_request)

    def mod(slot):
        return mod_ref[pl.ds(r, 1), slot * d:(slot + 1) * d]

    o = jnp.where(is_prompt, op_ref[...], os_ref[...])
    x = jnp.where(is_prompt, x_refs[0][...], x_refs[1][...]) if split_x else x_refs[0][...]
    y = jnp.dot(o, wo_ref[...], preferred_element_type=F32)
    x = x + _rms(y, mod(2) * g_ref[1:2, :])
    h = (_rms(x, g_ref[2:3, :] * (1 + mod(4))) + mod(3)).astype(BF16)
    y = jnp.zeros((h.shape[0], d), F32)
    lo = 0
    while lo < dff:
        hi = min(lo + FFN_CHUNK, dff)
        a = jnp.dot(h, wg_ref[:, lo:hi], preferred_element_type=F32)
        u = jnp.dot(h, wu_ref[:, lo:hi], preferred_element_type=F32)
        t = (a * jax.nn.sigmoid(a)) * u
        y = y + jnp.dot(t.astype(BF16), wd_ref[lo:hi, :], preferred_element_type=F32)
        lo = hi
    out = x + _rms(y, mod(5) * g_ref[3:4, :])
    if split_out:
        @pl.when(is_prompt)
        def _():
            out_refs[0][...] = out

        @pl.when(jnp.logical_not(is_prompt))
        def _():
            out_refs[1][...] = out
    else:
        out_refs[0][...] = out


def _post_attn_ffn(o_p, o_s, xs_in, w_o, wg, wu, wd, g, mods, *, layer, j, tiles_per_request, split_out):
    n_p, d = o_p.shape
    n_s = o_s.shape[0]
    tm = ROW_TILE
    tp, ts = n_p // tm, n_s // tm
    prompt_rows = pl.BlockSpec((tm, d), lambda i: (jnp.minimum(i, tp - 1), 0))
    latent_rows = pl.BlockSpec((tm, d), lambda i: (jnp.maximum(i - tp, 0), 0))
    all_rows = pl.BlockSpec((tm, d), lambda i: (i, 0))
    split_x = len(xs_in) == 2
    in_specs = [prompt_rows, latent_rows] + ([prompt_rows, latent_rows] if split_x else [all_rows])
    hbm = pl.BlockSpec(memory_space=pl.ANY)
    in_specs += [hbm, hbm, hbm, hbm, _layer_resident(g.shape, layer), _layer_resident(mods.shape, layer)]
    if split_out:
        out_shape = (jax.ShapeDtypeStruct((n_p, d), F32), jax.ShapeDtypeStruct((n_s, d), F32))
        out_specs = (prompt_rows, latent_rows)
    else:
        out_shape = jax.ShapeDtypeStruct((n_p + n_s, d), F32)
        out_specs = all_rows
    dff = wg.shape[2]
    scratch = [pltpu.VMEM((d, d), BF16), pltpu.VMEM((d, dff), BF16), pltpu.VMEM((d, dff), BF16),
               pltpu.VMEM((dff, d), BF16),
               pltpu.VMEM((WEIGHT_STAGE_SLOTS, WEIGHT_STAGE_ROWS, d), F32),
               pltpu.VMEM((WEIGHT_STAGE_SLOTS, WEIGHT_STAGE_ROWS, dff), F32),
               pltpu.SemaphoreType.DMA((WEIGHT_STAGE_SLOTS,)), pltpu.SemaphoreType.DMA((WEIGHT_STAGE_SLOTS,))]
    return pl.pallas_call(
        functools.partial(_post_attn_ffn_kernel, layer=layer, j=j, n_prompt_tiles=tp,
                          tiles_per_request=tiles_per_request, split_x=split_x, split_out=split_out),
        out_shape=out_shape,
        grid=(tp + ts,),
        in_specs=in_specs,
        out_specs=out_specs,
        scratch_shapes=scratch,
        compiler_params=pltpu.CompilerParams(dimension_semantics=("arbitrary",),
                                             vmem_limit_bytes=FFN_VMEM_LIMIT),
        name="post_attn_ffn",
    )(o_p, o_s, *xs_in, w_o, wg, wu, wd, g, mods)


def _rope_tables(n_lat):
    t = np.arange(n_lat)
    row = (t // GRID_W).astype(np.float32)
    col = (t % GRID_W).astype(np.float32)
    nf = ROT_DIM // 4
    inv = np.float32(ROPE_BASE) ** (-np.arange(nf, dtype=np.float32) / np.float32(nf))
    ar = row[:, None] * inv[None, :]
    ac = col[:, None] * inv[None, :]
    ang = np.concatenate([ar, ar, ac, ac], axis=-1)
    cos, sin = np.cos(ang), np.sin(ang)
    sign = np.where((np.arange(ROT_DIM) % 32) < 16, -1.0, 1.0).astype(np.float32)
    reps = LANES // ROT_DIM
    return jnp.asarray(np.tile(cos, (1, reps))), jnp.asarray(np.tile(sin * sign, (1, reps)))


def _swa_cache_to_feature_major(cache):
    nb, nl, lc, nh, hd = cache.shape
    return cache.transpose(0, 1, 3, 4, 2).reshape(nb, nl * nh * hd, lc)


def _swa_cache_from_feature_major(xt, seq):
    nb = xt.shape[0]
    return xt.reshape(nb, SWA_KV_HEADS, SWA_HD, seq).transpose(0, 3, 1, 2)


def kernel(x_prompt, x_sample, cache_diff_k, cache_diff_v, cache_swa_k, cache_swa_v, c, c_ctx,
           w_mod, b_mod, norm_g, w_qkv_diff, diff_lambda, diff_subln_g, w_o_diff,
           w_qkv_swa, swa_sink, w_o_swa, w_gate, w_up, w_down):
    bp, lp, d = x_prompt.shape
    bs, ls, _ = x_sample.shape
    lc = cache_diff_k.shape[2]
    depth = w_mod.shape[0]
    tm = ROW_TILE

    cond8 = jnp.concatenate([c_ctx[None, :], c, jnp.zeros((8 - 1 - bs, d), F32)], axis=0)
    mods = _modulation(cond8, w_mod, b_mod)
    tables = _rope_tables(ls)

    cdk = cache_diff_k.reshape(bs, -1, 2 * DIFF_HD)
    cdv = cache_diff_v.reshape(bs, -1, 2 * DIFF_HD)
    cskt = _swa_cache_to_feature_major(cache_swa_k)
    csvt = _swa_cache_to_feature_major(cache_swa_v)

    n_p, n_s = bp * lp, bs * ls
    x_parts = (x_prompt.reshape(n_p, d), x_sample.reshape(n_s, d))
    g = norm_g
    sub_g = diff_subln_g.reshape(-1, 1, 2 * DIFF_HD)
    p_rows = dict(row0=0, tiles_per_batch=n_p // QKV_ROW_TILE)
    s_rows = dict(row0=1, tiles_per_batch=ls // QKV_ROW_TILE)
    diff_k_out, diff_v_out, swa_k_out, swa_v_out = [], [], [], []

    for i in range(depth):
        j = i // N_MIXERS
        if len(x_parts) == 2:
            p_src, s_src = (x_parts[0], n_p, 0), (x_parts[1], n_s, 0)
        else:
            p_src, s_src = (x_parts[0], n_p, 0), (x_parts[0], n_s, n_p // QKV_ROW_TILE)
        if i % N_MIXERS == 0:
            lam_init = 0.8 - 0.6 * math.exp(-0.3 * i)
            qp, kp, vp, wb = _qkv_diff(*p_src, g, mods, w_qkv_diff, None, layer=i, j=j, latent=False,
                                       **p_rows)
            op = _diff_prompt_attention(qp, kp, vp, diff_lambda, sub_g, j=j, seq=lp, lam_init=lam_init)
            diff_k_out.append(kp.reshape(bp, lp, DIFF_HEADS, 2 * DIFF_HD))
            diff_v_out.append(vp.reshape(bp, lp, DIFF_HEADS, 2 * DIFF_HD))
            qs, ks, vs = _qkv_diff(*s_src, g, mods, wb, tables, layer=i, j=j, latent=True, **s_rows)
            os_ = _diff_latent_attention(qs, ks, vs, cdk, cdv, diff_lambda, sub_g,
                                         j=j, seq=ls, lc=lc, lam_init=lam_init)
            w_o = w_o_diff
        else:
            qp, ktp, vtp, wb = _qkv_swa(*p_src, g, mods, w_qkv_swa, None, layer=i, j=j, latent=False,
                                        seq=lp, **p_rows)
            op = _swa_prompt_attention(qp, ktp, vtp, swa_sink[j], seq=lp)
            swa_k_out.append(_swa_cache_from_feature_major(ktp, lp))
            swa_v_out.append(_swa_cache_from_feature_major(vtp, lp))
            qs, kds, vds = _qkv_swa(*s_src, g, mods, wb, tables, layer=i, j=j, latent=True,
                                    seq=ls, **s_rows)
            os_ = _swa_latent_attention(qs, kds, vds, cskt, csvt, swa_sink[j], j=j, seq=ls)
            w_o = w_o_swa
        last = i == depth - 1
        out = _post_attn_ffn(op, os_, x_parts, w_o, w_gate, w_up, w_down, g, mods, layer=i, j=j,
                             tiles_per_request=ls // tm, split_out=last)
        x_parts = out if last else (out,)
    xp, xs = x_parts

    return (xp.reshape(bp, lp, d), xs.reshape(bs, ls, d),
            jnp.stack(diff_k_out, axis=1), jnp.stack(diff_v_out, axis=1),
            jnp.stack(swa_k_out, axis=1), jnp.stack(swa_v_out, axis=1))
```

```python
import functools
import math

import jax
import jax.numpy as jnp
import numpy as np
from jax import lax
from jax.experimental import pallas as pl
from jax.experimental.pallas import tpu as pltpu

F32 = jnp.float32
BF16 = jnp.bfloat16

GRID_W = 64
N_MIXERS = 2
DIFF_HEADS = 8
DIFF_HD = 64
SWA_HEADS = 16
SWA_KV_HEADS = 4
SWA_GROUP = SWA_HEADS // SWA_KV_HEADS
SWA_HD = 64
ROT_DIM = 64
WINDOW = 128
ROPE_BASE = 10000.0
EPS = 1e-6
NEG_INF = -1e30

LANES = 128
SUBLANES = 8
ROW_TILE = 512
QKV_ROW_TILE = 512
PROMPT_SCORE_BUFFERS = 8
LATENT_SCORE_BUFFERS = 2
VMEM_LIMIT = 48 * 1024 * 1024
FFN_VMEM_LIMIT = 58 * 1024 * 1024
WEIGHT_STAGE_ROWS = 128
WEIGHT_STAGE_SLOTS = 6
QKV_STAGE_SLOTS = 4
FFN_CHUNK = 256
NT_DIMS = (((1,), (1,)), ((), ()))
LOG2E = math.log2(math.e)


def _params(n_axes):
    return pltpu.CompilerParams(dimension_semantics=("arbitrary",) * n_axes,
                                vmem_limit_bytes=VMEM_LIMIT)


def _resident(shape):
    return pl.BlockSpec(shape, lambda *_: (0,) * len(shape), pipeline_mode=pl.Buffered(1))


def _layer_resident(shape, layer):
    return pl.BlockSpec((None,) + tuple(shape[1:]), lambda *_: (layer,) + (0,) * (len(shape) - 1),
                        pipeline_mode=pl.Buffered(1))


def _rms(x, g):
    ms = jnp.mean(x * x, axis=-1, keepdims=True)
    return (x * lax.rsqrt(ms + EPS)) * g


def _half_masks(dtype):
    lane = lax.broadcasted_iota(jnp.int32, (1, LANES), 1)
    lo = lane < (LANES // 2)
    return jnp.where(lo, 1.0, 0.0).astype(dtype), jnp.where(lo, 0.0, 1.0).astype(dtype)


def _lo64():
    return lax.broadcasted_iota(jnp.int32, (1, LANES), 1) < (LANES // 2)


def _ones_column(rows):
    return jnp.ones((rows, LANES), BF16)


def _ones_row(cols):
    return jnp.ones((LANES, cols), BF16)


def _mod_kernel(cond_ref, w_ref, b_ref, out_ref):
    c = cond_ref[...]
    s = c * jax.nn.sigmoid(c)
    out_ref[...] = jnp.dot(s.astype(BF16), w_ref[...].astype(BF16),
                           preferred_element_type=F32) + b_ref[pl.ds(pl.program_id(0), 1), :]


def _modulation(cond8, w_mod, b_mod):
    depth, d, n = w_mod.shape
    tn = 1536
    return pl.pallas_call(
        _mod_kernel,
        out_shape=jax.ShapeDtypeStruct((depth, 8, n), F32),
        grid=(depth, n // tn),
        in_specs=[pl.BlockSpec((8, d), lambda i, j: (0, 0)),
                  pl.BlockSpec((None, d, tn), lambda i, j: (i, 0, j)),
                  pl.BlockSpec((depth, tn), lambda i, j: (0, j))],
        out_specs=pl.BlockSpec((None, 8, tn), lambda i, j: (i, 0, j)),
        compiler_params=_params(2),
        name="modulation",
    )(cond8, w_mod, b_mod)


def _rope_slab(xs, cos, sin_signed, lo16):
    left = pltpu.roll(xs, LANES - 16, 1)
    right = pltpu.roll(xs, 16, 1)
    return xs * cos + jnp.where(lo16, left, right) * sin_signed


def _lo16_mask():
    lane = lax.broadcasted_iota(jnp.int32, (1, LANES), 1)
    return (lane % 32) < 16


def _qkv_both_groups(refs, n_prompt_tiles, tiles_per_request, split_x, j, prompt_tile, latent_tile):
    refs = list(refs)
    x_refs = refs[:2] if split_x else refs[:1]
    g_ref, mod_ref, w_hbm, cos_ref, sin_ref = refs[len(x_refs):len(x_refs) + 5]
    outs = refs[len(x_refs) + 5:len(x_refs) + 11]
    w_vmem, stage, sem = refs[len(x_refs) + 11:]
    d = x_refs[0].shape[1]
    i = pl.program_id(0)

    @pl.when(i == 0)
    def _():
        _load_weights_as_bf16([(w_hbm.at[j], w_vmem)], {w_vmem.shape[1]: stage}, {w_vmem.shape[1]: sem})

    def pre_norm(x, r):
        shift, scale = mod_ref[pl.ds(r, 1), 0:d], mod_ref[pl.ds(r, 1), d:2 * d]
        return (_rms(x, g_ref[0:1, :] * (1 + scale)) + shift).astype(BF16)

    @pl.when(i < n_prompt_tiles)
    def _():
        prompt_tile(pre_norm(x_refs[0][...], 0), w_vmem, outs[:3])

    @pl.when(i >= n_prompt_tiles)
    def _():
        r = 1 + (i - n_prompt_tiles) // tiles_per_request
        latent_tile(pre_norm(x_refs[-1][...], r), w_vmem, outs[3:], (cos_ref[...], sin_ref[...], _lo16_mask()))


def _qkv_call(kernel, name, x_parts, g, mods, w, tables, prompt_outs, latent_outs, *, layer, n_p, n_s):
    d = x_parts[0].shape[1]
    tm = QKV_ROW_TILE
    tp, ts = n_p // tm, n_s // tm
    nt = tables[0].shape[0] // tm

    def prompt_block(shape):
        return pl.BlockSpec(shape, lambda i: (jnp.minimum(i, tp - 1),) + (0,) * (len(shape) - 1))

    def latent_block(shape):
        return pl.BlockSpec(shape, lambda i: (jnp.maximum(i - tp, 0),) + (0,) * (len(shape) - 1))

    if len(x_parts) == 2:
        x_specs = [prompt_block((tm, d)), latent_block((tm, d))]
    else:
        x_specs = [pl.BlockSpec((tm, d), lambda i: (i, 0))]
    table_spec = pl.BlockSpec((tm, LANES), lambda i: (jnp.maximum(i - tp, 0) % nt, 0))
    in_specs = x_specs + [_layer_resident(g.shape, layer), _layer_resident(mods.shape, layer),
                          pl.BlockSpec(memory_space=pl.ANY), table_spec, table_spec]
    out_shape, out_specs = [], []
    for outs, n, block in ((prompt_outs, n_p, prompt_block), (latent_outs, n_s, latent_block)):
        for shape_of, dtype in outs:
            out_shape.append(jax.ShapeDtypeStruct(shape_of(n), dtype))
            out_specs.append(block(shape_of(tm)))
    cols = w.shape[2]
    scratch = [pltpu.VMEM((d, cols), BF16), pltpu.VMEM((QKV_STAGE_SLOTS, WEIGHT_STAGE_ROWS, cols), F32),
               pltpu.SemaphoreType.DMA((QKV_STAGE_SLOTS,))]
    return pl.pallas_call(
        kernel,
        out_shape=tuple(out_shape),
        grid=(tp + ts,),
        in_specs=in_specs,
        out_specs=tuple(out_specs),
        scratch_shapes=scratch,
        compiler_params=_params(1),
        name=name,
    )(*x_parts, g, mods, w, *tables)


def _diff_tile(h, w_ref, outs, rope=None):
    q_ref, k_ref, v_ref = outs
    d = h.shape[1]
    cw = 512
    for c in range(3 * d // cw):
        acc = jnp.dot(h, w_ref[:, c * cw:(c + 1) * cw], preferred_element_type=F32)
        which, off = divmod(c * cw, d)
        dst = outs[which]
        for s in range(cw // LANES):
            xs = acc[:, s * LANES:(s + 1) * LANES]
            if rope is not None and which < 2:
                xs = _rope_slab(xs, *rope)
            if which == 0:
                xs = xs * (DIFF_HD ** -0.5 * LOG2E)
            lo = off + s * LANES
            if rope is not None or which == 0:
                dst[:, lo:lo + LANES] = xs.astype(dst.dtype)
            else:
                dst[pl.ds(lo // LANES, h.shape[0], stride=DIFF_HEADS), :] = xs


def _qkv_diff_kernel(*refs, j, n_prompt_tiles, tiles_per_request, split_x):
    _qkv_both_groups(refs, n_prompt_tiles, tiles_per_request, split_x, j, _diff_tile, _diff_tile)


def _qkv_diff(x_parts, g, mods, w, tables, *, layer, j, n_p, n_s, tiles_per_request):
    d = x_parts[0].shape[1]

    def rows(n):
        return (n, d)

    def cache(n):
        return (n * DIFF_HEADS, d // DIFF_HEADS)

    kernel = functools.partial(_qkv_diff_kernel, j=j, n_prompt_tiles=n_p // QKV_ROW_TILE,
                               tiles_per_request=tiles_per_request, split_x=len(x_parts) == 2)
    return _qkv_call(kernel, "qkv_diff", x_parts, g, mods, w, tables,
                     [(rows, BF16), (cache, F32), (cache, F32)], [(rows, BF16)] * 3,
                     layer=layer, n_p=n_p, n_s=n_s)


def _swa_q_tile(h, w_ref, q_ref, rope):
    d = h.shape[1]
    cw = 512
    for c in range(d // cw):
        acc = jnp.dot(h, w_ref[:, c * cw:(c + 1) * cw], preferred_element_type=F32)
        for s in range(cw // LANES):
            xs = acc[:, s * LANES:(s + 1) * LANES]
            if rope is not None:
                xs = _rope_slab(xs, *rope)
            lo = c * cw + s * LANES
            q_ref[:, lo:lo + LANES] = (xs * (SWA_HD ** -0.5 * LOG2E)).astype(BF16)
    nkv = SWA_KV_HEADS * SWA_HD
    return jnp.dot(h, w_ref[:, d:d + 2 * nkv], preferred_element_type=F32), nkv


def _swa_prompt_tile(h, w_ref, outs, *, seq):
    q_ref, kt_ref, vt_ref = outs
    kv, nkv = _swa_q_tile(h, w_ref, q_ref, None)
    for b in range(h.shape[0] // seq):
        kt_ref[b] = kv[b * seq:(b + 1) * seq, :nkv].T
        vt_ref[b] = kv[b * seq:(b + 1) * seq, nkv:].T


def _swa_latent_tile(h, w_ref, outs, rope):
    q_ref, kd_ref, vd_ref = outs
    kv, nkv = _swa_q_tile(h, w_ref, q_ref, rope)
    lo64 = _lo64()
    for which, dst in enumerate((kd_ref, vd_ref)):
        for s in range(nkv // LANES):
            xs = kv[:, which * nkv + s * LANES: which * nkv + (s + 1) * LANES]
            if which == 0:
                xs = _rope_slab(xs, *rope)
            sw = pltpu.roll(xs, LANES // 2, 1)
            dst[:, (2 * s) * LANES:(2 * s + 1) * LANES] = jnp.where(lo64, xs, sw).astype(BF16)
            dst[:, (2 * s + 1) * LANES:(2 * s + 2) * LANES] = jnp.where(lo64, sw, xs).astype(BF16)


def _qkv_swa_kernel(*refs, j, n_prompt_tiles, tiles_per_request, split_x, seq):
    _qkv_both_groups(refs, n_prompt_tiles, tiles_per_request, split_x, j,
                     functools.partial(_swa_prompt_tile, seq=seq), _swa_latent_tile)


def _qkv_swa(x_parts, g, mods, w, tables, *, layer, j, n_p, n_s, tiles_per_request, seq):
    d = x_parts[0].shape[1]
    nkv = SWA_KV_HEADS * SWA_HD

    def rows(n):
        return (n, d)

    def feature_major(n):
        return (n // seq, nkv, seq)

    def duplicated(n):
        return (n, 2 * nkv)

    kernel = functools.partial(_qkv_swa_kernel, j=j, n_prompt_tiles=n_p // QKV_ROW_TILE,
                               tiles_per_request=tiles_per_request, split_x=len(x_parts) == 2, seq=seq)
    return _qkv_call(kernel, "qkv_swa", x_parts, g, mods, w, tables,
                     [(rows, BF16), (feature_major, F32), (feature_major, F32)],
                     [(rows, BF16), (duplicated, BF16), (duplicated, BF16)],
                     layer=layer, n_p=n_p, n_s=n_s)


def _diff_lambda(lam_ref, lam_init):
    lp = lam_ref[...]
    a = jnp.sum(lp[0:1] * lp[1:2], axis=-1, keepdims=True)
    b = jnp.sum(lp[2:3] * lp[3:4], axis=-1, keepdims=True)
    return jnp.exp(a) - jnp.exp(b) + lam_init


def _diff_combine(acc, tq, lam, g, lam_init):
    o12 = acc[:, :LANES] / acc[:, LANES:]
    o = o12[:tq] - lam * o12[tq:]
    return _rms(o, g) * (1.0 - lam_init)


def _stack_maps(q):
    m_lo, m_hi = _half_masks(BF16)
    return jnp.concatenate([q * m_lo, q * m_hi], axis=0)


def _run_pipelined(items, scores, finish, s_bufs):
    depth = len(s_bufs)
    states = {i: scores(items[i], s_bufs[i]) for i in range(min(depth - 1, len(items)))}
    for i, item in enumerate(items):
        ahead = i + depth - 1
        if ahead < len(items):
            states[ahead] = scores(items[ahead], s_bufs[ahead % depth])
        finish(item, s_bufs[i % depth], states.pop(i))


def _store_scores(s_ref, col0, s, mrun):
    s_ref[:, col0:col0 + s.shape[1]] = s
    for t in range(s.shape[1] // LANES):
        blk = s[:, t * LANES:(t + 1) * LANES]
        mrun = blk if mrun is None else jnp.maximum(mrun, blk)
    return mrun


def _exp_block(s_ref, col0, width, mb):
    return jnp.concatenate(
        [jnp.exp2(s_ref[:, col0 + t * LANES:col0 + (t + 1) * LANES] - mb).astype(BF16)
         for t in range(width // LANES)], axis=1)


def _diff_prompt_kernel(q_ref, k_ref, v_ref, lam_ref, g_ref, o_ref, *s_bufs, lam_init, seq):
    lam = _diff_lambda(lam_ref, lam_init)
    g = g_ref[...]
    ones = _ones_column(seq)
    items = [(r, h) for r in range(q_ref.shape[0] // seq) for h in range(DIFF_HEADS)]

    def head_rows(ref, r, h):
        return ref[pl.ds(r * seq * DIFF_HEADS + h, seq, stride=DIFF_HEADS), :]

    def scores(item, s_ref):
        r, h = item
        rows, sl = slice(r * seq, (r + 1) * seq), slice(h * LANES, (h + 1) * LANES)
        s = lax.dot_general(_stack_maps(q_ref[rows, sl]), head_rows(k_ref, r, h).astype(BF16), NT_DIMS,
                            preferred_element_type=F32)
        return _store_scores(s_ref, 0, s, None)

    def finish(item, s_ref, mrun):
        r, h = item
        rows, sl = slice(r * seq, (r + 1) * seq), slice(h * LANES, (h + 1) * LANES)
        mb = jnp.broadcast_to(jnp.max(mrun, axis=-1, keepdims=True), (2 * seq, LANES))
        vx = jnp.concatenate([head_rows(v_ref, r, h).astype(BF16), ones], axis=1)
        acc = jnp.dot(_exp_block(s_ref, 0, seq, mb), vx, preferred_element_type=F32)
        o_ref[rows, sl] = _diff_combine(acc, seq, lam, g, lam_init).astype(BF16)

    _run_pipelined(items, scores, finish, s_bufs)


def _diff_prompt_attention(q, k, v, lam_params, subln_g, *, j, seq, lam_init):
    n, d = q.shape
    req = 2
    spec = pl.BlockSpec((req * seq, d), lambda b: (b, 0))
    kv_spec = pl.BlockSpec((req * seq * DIFF_HEADS, d // DIFF_HEADS), lambda b: (b, 0))
    return pl.pallas_call(
        functools.partial(_diff_prompt_kernel, lam_init=lam_init, seq=seq),
        out_shape=jax.ShapeDtypeStruct((n, d), BF16),
        grid=(n // (req * seq),),
        in_specs=[spec, kv_spec, kv_spec, _layer_resident(lam_params.shape, j),
                  _layer_resident(subln_g.shape, j)],
        out_specs=spec,
        scratch_shapes=[pltpu.VMEM((2 * seq, seq), F32)] * PROMPT_SCORE_BUFFERS,
        compiler_params=_params(1),
        name="diff_attn_prompt",
    )(q, k, v, lam_params, subln_g)


def _diff_latent_kernel(q_ref, kc_ref, vc_ref, kl_ref, vl_ref, lam_ref, g_ref, o_ref,
                        kk_ref, vx_ref, *s_bufs, lam_init, lc, tq, key_chunk):
    seq = q_ref.shape[0]
    heads = kk_ref.shape[0]
    nk = kk_ref.shape[1]
    for hh in range(heads):
        h = pl.program_id(1) * heads + hh
        sl = slice(hh * LANES, (hh + 1) * LANES)
        kk_ref[hh, 0:lc, :] = kc_ref[pl.ds(h, lc, stride=DIFF_HEADS), :].astype(BF16)
        kk_ref[hh, lc:, :] = kl_ref[:, sl]
        vx_ref[hh, 0:lc, 0:LANES] = vc_ref[pl.ds(h, lc, stride=DIFF_HEADS), :].astype(BF16)
        vx_ref[hh, lc:, 0:LANES] = vl_ref[:, sl]
        vx_ref[hh, :, LANES:2 * LANES] = _ones_column(nk)

    masks = _half_masks(BF16)
    nchunk = nk // key_chunk
    items = [(hh, rt, m) for hh in range(heads) for rt in range(seq // tq) for m in range(2)]
    lam = _diff_lambda(lam_ref, lam_init)
    g = g_ref[...]
    first_map = {}

    def scores(item, s_ref):
        hh, rt, m = item
        q = q_ref[rt * tq:(rt + 1) * tq, hh * LANES:(hh + 1) * LANES] * masks[m]
        mrun = None
        for c in range(nchunk):
            s = lax.dot_general(q, kk_ref[hh, c * key_chunk:(c + 1) * key_chunk, :], NT_DIMS,
                                preferred_element_type=F32)
            mrun = _store_scores(s_ref, c * key_chunk, s, mrun)
        return mrun

    def finish(item, s_ref, mrun):
        hh, rt, m = item
        mb = jnp.broadcast_to(jnp.max(mrun, axis=-1, keepdims=True), (tq, LANES))
        acc = None
        for c in range(nchunk):
            part = jnp.dot(_exp_block(s_ref, c * key_chunk, key_chunk, mb),
                           vx_ref[hh, c * key_chunk:(c + 1) * key_chunk, :], preferred_element_type=F32)
            acc = part if acc is None else acc + part
        o_m = acc[:, :LANES] / acc[:, LANES:]
        if m == 0:
            first_map[hh, rt] = o_m
        else:
            o = _rms(first_map.pop((hh, rt)) - lam * o_m, g) * (1.0 - lam_init)
            o_ref[rt * tq:(rt + 1) * tq, hh * LANES:(hh + 1) * LANES] = o.astype(BF16)

    _run_pipelined(items, scores, finish, s_bufs)


def _diff_latent_attention(q, k, v, cache_k, cache_v, lam_params, subln_g, *, j, seq, lc, lam_init):
    n, d = q.shape
    nb = cache_k.shape[0]
    tq = 512
    key_chunk = 512
    heads = 1
    q_spec = pl.BlockSpec((seq, heads * LANES), lambda b, h: (b, h))
    c_spec = pl.BlockSpec((None, lc * DIFF_HEADS, LANES), lambda b, h: (b, j, 0))
    return pl.pallas_call(
        functools.partial(_diff_latent_kernel, lam_init=lam_init, lc=lc, tq=tq, key_chunk=key_chunk),
        out_shape=jax.ShapeDtypeStruct((n, d), BF16),
        grid=(nb, DIFF_HEADS // heads),
        in_specs=[q_spec, c_spec, c_spec, q_spec, q_spec,
                  _layer_resident(lam_params.shape, j), _layer_resident(subln_g.shape, j)],
        out_specs=q_spec,
        scratch_shapes=[pltpu.VMEM((heads, lc + seq, LANES), BF16),
                        pltpu.VMEM((heads, lc + seq, 2 * LANES), BF16),
                        *[pltpu.VMEM((tq, lc + seq), F32)] * LATENT_SCORE_BUFFERS],
        compiler_params=_params(2),
        name="diff_attn_latent",
    )(q, cache_k, cache_v, k, v, lam_params, subln_g)


def _stack_group(q_ref, rows, kv_local):
    m_lo, m_hi = _half_masks(BF16)
    parts = []
    for gb in range(SWA_GROUP // 2):
        blk = kv_local * (SWA_GROUP // 2) + gb
        qb = q_ref[rows, blk * LANES:(blk + 1) * LANES]
        parts += [qb * m_lo, qb * m_hi]
    return jnp.concatenate(parts, axis=0)


def _sink_column(sink_ref, first_head, tq):
    return jnp.concatenate([jnp.full((tq, LANES), sink_ref[first_head + g] * LOG2E, F32)
                            for g in range(SWA_GROUP)], axis=0)


def _write_group(o_ref, rows, kv_local, o, tq):
    lo64 = _lo64()
    for gb in range(SWA_GROUP // 2):
        blk = kv_local * (SWA_GROUP // 2) + gb
        even = o[(2 * gb) * tq:(2 * gb + 1) * tq]
        odd = o[(2 * gb + 1) * tq:(2 * gb + 2) * tq]
        o_ref[rows, blk * LANES:(blk + 1) * LANES] = jnp.where(lo64, even, odd).astype(BF16)


def _dup_rows(x_t):
    xb = x_t.astype(BF16)
    return jnp.concatenate([xb, xb], axis=0)


def _sink_finish(mrun, sk, rows):
    mb = jnp.maximum(jnp.broadcast_to(jnp.max(mrun, axis=-1, keepdims=True), (rows, LANES)), sk)
    return mb, jnp.exp2(sk - mb)


def _swa_prompt_kernel(sink_ref, q_ref, kt_ref, vt_ref, o_ref, *s_bufs):
    seq = kt_ref.shape[2]
    rows = SWA_GROUP * seq
    ones = _ones_row(seq)
    items = [(r, j) for r in range(kt_ref.shape[0]) for j in range(SWA_KV_HEADS)]

    def scores(item, s_ref):
        r, j = item
        kd = _dup_rows(kt_ref[r, j * SWA_HD:(j + 1) * SWA_HD, :])
        s = jnp.dot(_stack_group(q_ref, slice(r * seq, (r + 1) * seq), j), kd, preferred_element_type=F32)
        return _store_scores(s_ref, 0, s, None)

    def finish(item, s_ref, mrun):
        r, j = item
        sk = _sink_column(sink_ref, j * SWA_GROUP, seq)
        mb, sink_term = _sink_finish(mrun, sk, rows)
        vx = jnp.concatenate([_dup_rows(vt_ref[r, j * SWA_HD:(j + 1) * SWA_HD, :]), ones], axis=0)
        acc = lax.dot_general(_exp_block(s_ref, 0, seq, mb), vx, NT_DIMS, preferred_element_type=F32)
        o = acc[:, :LANES] / (acc[:, LANES:] + sink_term)
        _write_group(o_ref, slice(r * seq, (r + 1) * seq), j, o, seq)

    _run_pipelined(items, scores, finish, s_bufs)


def _swa_prompt_attention(q, kt, vt, sink, *, seq):
    n, d = q.shape
    nkv = kt.shape[1]
    req = 2
    t_spec = pl.BlockSpec((req, nkv, seq), lambda b: (b, 0, 0))
    return pl.pallas_call(
        _swa_prompt_kernel,
        out_shape=jax.ShapeDtypeStruct((n, d), BF16),
        grid=(n // (req * seq),),
        in_specs=[pl.BlockSpec(memory_space=pltpu.SMEM),
                  pl.BlockSpec((req * seq, d), lambda b: (b, 0)), t_spec, t_spec],
        out_specs=pl.BlockSpec((req * seq, d), lambda b: (b, 0)),
        scratch_shapes=[pltpu.VMEM((SWA_GROUP * seq, seq), F32)] * (PROMPT_SCORE_BUFFERS // 2),
        compiler_params=_params(1),
        name="swa_attn_prompt",
    )(sink, q, kt, vt)


def _swa_latent_kernel(sink_ref, q_ref, kc_ref, vc_ref, kl_ref, vl_ref, o_ref, kcd_ref, vcx_ref,
                       s0_ref, s1_ref, *, tq, span):
    pair = pl.program_id(1)
    tiles = q_ref.shape[0] // tq
    first_tile = pl.program_id(2) * tiles
    seq = kl_ref.shape[0]
    lc = kc_ref.shape[1]
    rows = SWA_GROUP * tq
    ones_row = _ones_row(lc)
    for jj in range(2):
        kcd_ref[jj] = _dup_rows(kc_ref[jj * SWA_HD:(jj + 1) * SWA_HD, :])
        vcx_ref[jj] = jnp.concatenate([_dup_rows(vc_ref[jj * SWA_HD:(jj + 1) * SWA_HD, :]), ones_row], axis=0)
    ones_col = _ones_column(span)
    items = [(t, jj) for t in range(tiles) for jj in range(2)]
    windows, biases = {}, {}

    def window(t):
        if t not in windows:
            q0 = (first_tile + t) * tq
            windows[t] = (q0, pl.multiple_of(jnp.clip(q0 - WINDOW, 0, seq - span), WINDOW))
        return windows[t]

    def bias_for(t):
        if t not in biases:
            q0, ws = window(t)
            q_pos = q0 + lax.broadcasted_iota(jnp.int32, (tq, span), 0)
            k_pos = ws + lax.broadcasted_iota(jnp.int32, (tq, span), 1)
            b = jnp.where(jnp.abs(q_pos - k_pos) <= WINDOW, 0.0, NEG_INF).astype(F32)
            biases[t] = jnp.concatenate([b] * SWA_GROUP, axis=0)
        return biases[t]

    def scores(item, s_ref):
        t, jj = item
        _, ws = window(t)
        qs = _stack_group(q_ref, slice(t * tq, (t + 1) * tq), jj)
        s_c = jnp.dot(qs, kcd_ref[jj], preferred_element_type=F32)
        mrun = _store_scores(s_ref, 0, s_c, None)
        s_w = lax.dot_general(qs, kl_ref[pl.ds(ws, span), jj * LANES:(jj + 1) * LANES], NT_DIMS,
                              preferred_element_type=F32) + bias_for(t)
        return _store_scores(s_ref, lc, s_w, mrun)

    def finish(item, s_ref, mrun):
        t, jj = item
        _, ws = window(t)
        sk = _sink_column(sink_ref, (2 * pair + jj) * SWA_GROUP, tq)
        mb, sink_term = _sink_finish(mrun, sk, rows)
        vwx = jnp.concatenate([vl_ref[pl.ds(ws, span), jj * LANES:(jj + 1) * LANES], ones_col], axis=1)
        acc = (lax.dot_general(_exp_block(s_ref, 0, lc, mb), vcx_ref[jj], NT_DIMS, preferred_element_type=F32)
               + jnp.dot(_exp_block(s_ref, lc, span, mb), vwx, preferred_element_type=F32))
        o = acc[:, :LANES] / (acc[:, LANES:] + sink_term)
        _write_group(o_ref, slice(t * tq, (t + 1) * tq), jj, o, tq)

    _run_pipelined(items, scores, finish, (s0_ref, s1_ref))


def _swa_latent_attention(q, kd, vd, cache_kt, cache_vt, sink, *, j, seq):
    n, d = q.shape
    nb, _, lc = cache_kt.shape
    tq = 256
    span = tq + 2 * WINDOW
    npair = SWA_KV_HEADS // 2
    wq = d // npair
    parts = 2
    q_spec = pl.BlockSpec((seq // parts, wq), lambda b, p, i: (b * parts + i, p))
    c_spec = pl.BlockSpec((None, 2 * SWA_HD, lc), lambda b, p, i: (b, j * npair + p, 0))
    l_spec = pl.BlockSpec((seq, 2 * LANES), lambda b, p, i: (b, p))
    s_shape = pltpu.VMEM((SWA_GROUP * tq, lc + span), F32)
    return pl.pallas_call(
        functools.partial(_swa_latent_kernel, tq=tq, span=span),
        out_shape=jax.ShapeDtypeStruct((n, d), BF16),
        grid=(nb, npair, parts),
        in_specs=[pl.BlockSpec(memory_space=pltpu.SMEM), q_spec, c_spec, c_spec, l_spec, l_spec],
        out_specs=q_spec,
        scratch_shapes=[pltpu.VMEM((2, 2 * SWA_HD, lc), BF16), pltpu.VMEM((2, 2 * LANES, lc), BF16),
                        s_shape, s_shape],
        compiler_params=_params(3),
        name="swa_attn_latent",
    )(sink, q, cache_kt, cache_vt, kd, vd)


def _load_weights_as_bf16(jobs, stages, sems):
    order = []
    rings = {w: [] for w in stages}
    for src, dst in jobs:
        w = src.shape[1]
        slots = stages[w].shape[0]
        for k in range(src.shape[0] // WEIGHT_STAGE_ROWS):
            rows = pl.ds(k * WEIGHT_STAGE_ROWS, WEIGHT_STAGE_ROWS)
            slot = len(rings[w]) % slots
            copy = pltpu.make_async_copy(src.at[rows, :], stages[w].at[slot], sems[w].at[slot])
            order.append((w, len(rings[w])))
            rings[w].append((copy, slot, dst, rows))
    for w, ring in rings.items():
        for copy, _, _, _ in ring[:stages[w].shape[0]]:
            copy.start()
    for w, k in order:
        copy, slot, dst, rows = rings[w][k]
        copy.wait()
        dst[rows, :] = stages[w][slot].astype(BF16)
        ahead = k + stages[w].shape[0]
        if ahead < len(rings[w]):
            rings[w][ahead][0].start()


def _post_attn_ffn_kernel(*refs, layer, j, n_prompt_tiles, tiles_per_request, split_x, split_out):
    refs = list(refs)
    op_ref, os_ref = refs[:2]
    x_refs = refs[2:4] if split_x else refs[2:3]
    wo_hbm, wg_hbm, wu_hbm, wd_hbm, g_ref, mod_ref = refs[2 + len(x_refs):8 + len(x_refs)]
    n_out = 2 if split_out else 1
    out_refs = refs[8 + len(x_refs):8 + len(x_refs) + n_out]
    wo_ref, wg_ref, wu_ref, wd_ref, stage_d, stage_ff, sem_d, sem_ff = refs[8 + len(x_refs) + n_out:]
    d = wo_ref.shape[1]
    dff = wg_ref.shape[1]
    i = pl.program_id(0)

    @pl.when(i == 0)
    def _():
        _load_weights_as_bf16(
            [(wo_hbm.at[j], wo_ref), (wg_hbm.at[layer], wg_ref), (wu_hbm.at[layer], wu_ref),
             (wd_hbm.at[layer], wd_ref)],
            {d: stage_d, dff: stage_ff}, {d: sem_d, dff: sem_ff})

    is_prompt = i < n_prompt_tiles
    r = jnp.where(is_prompt, 0, 1 + (i - n_prompt_tiles) // tiles_per_request)

    def mod(slot):
        return mod_ref[pl.ds(r, 1), slot * d:(slot + 1) * d]

    o = jnp.where(is_prompt, op_ref[...], os_ref[...])
    x = jnp.where(is_prompt, x_refs[0][...], x_refs[1][...]) if split_x else x_refs[0][...]
    y = jnp.dot(o, wo_ref[...], preferred_element_type=F32)
    x = x + _rms(y, mod(2) * g_ref[1:2, :])
    h = (_rms(x, g_ref[2:3, :] * (1 + mod(4))) + mod(3)).astype(BF16)
    y = jnp.zeros((h.shape[0], d), F32)
    lo = 0
    while lo < dff:
        hi = min(lo + FFN_CHUNK, dff)
        a = jnp.dot(h, wg_ref[:, lo:hi], preferred_element_type=F32)
        u = jnp.dot(h, wu_ref[:, lo:hi], preferred_element_type=F32)
        t = (a * jax.nn.sigmoid(a)) * u
        y = y + jnp.dot(t.astype(BF16), wd_ref[lo:hi, :], preferred_element_type=F32)
        lo = hi
    out = x + _rms(y, mod(5) * g_ref[3:4, :])
    if split_out:
        @pl.when(is_prompt)
        def _():
            out_refs[0][...] = out

        @pl.when(jnp.logical_not(is_prompt))
        def _():
            out_refs[1][...] = out
    else:
        out_refs[0][...] = out


def _post_attn_ffn(o_p, o_s, xs_in, w_o, wg, wu, wd, g, mods, *, layer, j, tiles_per_request, split_out):
    n_p, d = o_p.shape
    n_s = o_s.shape[0]
    tm = ROW_TILE
    tp, ts = n_p // tm, n_s // tm
    prompt_rows = pl.BlockSpec((tm, d), lambda i: (jnp.minimum(i, tp - 1), 0))
    latent_rows = pl.BlockSpec((tm, d), lambda i: (jnp.maximum(i - tp, 0), 0))
    all_rows = pl.BlockSpec((tm, d), lambda i: (i, 0))
    split_x = len(xs_in) == 2
    in_specs = [prompt_rows, latent_rows] + ([prompt_rows, latent_rows] if split_x else [all_rows])
    hbm = pl.BlockSpec(memory_space=pl.ANY)
    in_specs += [hbm, hbm, hbm, hbm, _layer_resident(g.shape, layer), _layer_resident(mods.shape, layer)]
    if split_out:
        out_shape = (jax.ShapeDtypeStruct((n_p, d), F32), jax.ShapeDtypeStruct((n_s, d), F32))
        out_specs = (prompt_rows, latent_rows)
    else:
        out_shape = jax.ShapeDtypeStruct((n_p + n_s, d), F32)
        out_specs = all_rows
    dff = wg.shape[2]
    scratch = [pltpu.VMEM((d, d), BF16), pltpu.VMEM((d, dff), BF16), pltpu.VMEM((d, dff), BF16),
               pltpu.VMEM((dff, d), BF16),
               pltpu.VMEM((WEIGHT_STAGE_SLOTS, WEIGHT_STAGE_ROWS, d), F32),
               pltpu.VMEM((WEIGHT_STAGE_SLOTS, WEIGHT_STAGE_ROWS, dff), F32),
               pltpu.SemaphoreType.DMA((WEIGHT_STAGE_SLOTS,)), pltpu.SemaphoreType.DMA((WEIGHT_STAGE_SLOTS,))]
    return pl.pallas_call(
        functools.partial(_post_attn_ffn_kernel, layer=layer, j=j, n_prompt_tiles=tp,
                          tiles_per_request=tiles_per_request, split_x=split_x, split_out=split_out),
        out_shape=out_shape,
        grid=(tp + ts,),
        in_specs=in_specs,
        out_specs=out_specs,
        scratch_shapes=scratch,
        compiler_params=pltpu.CompilerParams(dimension_semantics=("arbitrary",),
                                             vmem_limit_bytes=FFN_VMEM_LIMIT),
        name="post_attn_ffn",
    )(o_p, o_s, *xs_in, w_o, wg, wu, wd, g, mods)


def _rope_tables(n_lat):
    t = np.arange(n_lat)
    row = (t // GRID_W).astype(np.float32)
    col = (t % GRID_W).astype(np.float32)
    nf = ROT_DIM // 4
    inv = np.float32(ROPE_BASE) ** (-np.arange(nf, dtype=np.float32) / np.float32(nf))
    ar = row[:, None] * inv[None, :]
    ac = col[:, None] * inv[None, :]
    ang = np.concatenate([ar, ar, ac, ac], axis=-1)
    cos, sin = np.cos(ang), np.sin(ang)
    sign = np.where((np.arange(ROT_DIM) % 32) < 16, -1.0, 1.0).astype(np.float32)
    reps = LANES // ROT_DIM
    return jnp.asarray(np.tile(cos, (1, reps))), jnp.asarray(np.tile(sin * sign, (1, reps)))


def _swa_cache_to_feature_major(cache):
    nb, nl, lc, nh, hd = cache.shape
    return cache.transpose(0, 1, 3, 4, 2).reshape(nb, nl * nh * hd, lc)


def _swa_cache_from_feature_major(xt, seq):
    nb = xt.shape[0]
    return xt.reshape(nb, SWA_KV_HEADS, SWA_HD, seq).transpose(0, 3, 1, 2)


def kernel(x_prompt, x_sample, cache_diff_k, cache_diff_v, cache_swa_k, cache_swa_v, c, c_ctx,
           w_mod, b_mod, norm_g, w_qkv_diff, diff_lambda, diff_subln_g, w_o_diff,
           w_qkv_swa, swa_sink, w_o_swa, w_gate, w_up, w_down):
    bp, lp, d = x_prompt.shape
    bs, ls, _ = x_sample.shape
    lc = cache_diff_k.shape[2]
    depth = w_mod.shape[0]
    tm = ROW_TILE

    cond8 = jnp.concatenate([c_ctx[None, :], c, jnp.zeros((8 - 1 - bs, d), F32)], axis=0)
    mods = _modulation(cond8, w_mod, b_mod)
    tables = _rope_tables(ls)

    cdk = cache_diff_k.reshape(bs, -1, 2 * DIFF_HD)
    cdv = cache_diff_v.reshape(bs, -1, 2 * DIFF_HD)
    cskt = _swa_cache_to_feature_major(cache_swa_k)
    csvt = _swa_cache_to_feature_major(cache_swa_v)

    n_p, n_s = bp * lp, bs * ls
    x_parts = (x_prompt.reshape(n_p, d), x_sample.reshape(n_s, d))
    g = norm_g
    sub_g = diff_subln_g.reshape(-1, 1, 2 * DIFF_HD)
    diff_k_out, diff_v_out, swa_k_out, swa_v_out = [], [], [], []

    for i in range(depth):
        j = i // N_MIXERS
        if i % N_MIXERS == 0:
            lam_init = 0.8 - 0.6 * math.exp(-0.3 * i)
            qp, kp, vp, qs, ks, vs = _qkv_diff(x_parts, g, mods, w_qkv_diff, tables, layer=i, j=j,
                                               n_p=n_p, n_s=n_s, tiles_per_request=ls // QKV_ROW_TILE)
            op = _diff_prompt_attention(qp, kp, vp, diff_lambda, sub_g, j=j, seq=lp, lam_init=lam_init)
            diff_k_out.append(kp.reshape(bp, lp, DIFF_HEADS, 2 * DIFF_HD))
            diff_v_out.append(vp.reshape(bp, lp, DIFF_HEADS, 2 * DIFF_HD))
            os_ = _diff_latent_attention(qs, ks, vs, cdk, cdv, diff_lambda, sub_g,
                                         j=j, seq=ls, lc=lc, lam_init=lam_init)
            w_o = w_o_diff
        else:
            qp, ktp, vtp, qs, kds, vds = _qkv_swa(x_parts, g, mods, w_qkv_swa, tables, layer=i, j=j,
                                                  n_p=n_p, n_s=n_s, tiles_per_request=ls // QKV_ROW_TILE,
                                                  seq=lp)
            op = _swa_prompt_attention(qp, ktp, vtp, swa_sink[j], seq=lp)
            swa_k_out.append(_swa_cache_from_feature_major(ktp, lp))
            swa_v_out.append(_swa_cache_from_feature_major(vtp, lp))
            os_ = _swa_latent_attention(qs, kds, vds, cskt, csvt, swa_sink[j], j=j, seq=ls)
            w_o = w_o_swa
        last = i == depth - 1
        out = _post_attn_ffn(op, os_, x_parts, w_o, w_gate, w_up, w_down, g, mods, layer=i, j=j,
                             tiles_per_request=ls // tm, split_out=last)
        x_parts = out if last else (out,)
    xp, xs = x_parts

    return (xp.reshape(bp, lp, d), xs.reshape(bs, ls, d),
            jnp.stack(diff_k_out, axis=1), jnp.stack(diff_v_out, axis=1),
            jnp.stack(swa_k_out, axis=1), jnp.stack(swa_v_out, axis=1))
```

```python
import functools
import math

import jax
import jax.numpy as jnp
import numpy as np
from jax import lax
from jax.experimental import pallas as pl
from jax.experimental.pallas import tpu as pltpu

F32 = jnp.float32
BF16 = jnp.bfloat16

GRID_W = 64
N_MIXERS = 2
DIFF_HEADS = 8
DIFF_HD = 64
SWA_HEADS = 16
SWA_KV_HEADS = 4
SWA_GROUP = SWA_HEADS // SWA_KV_HEADS
SWA_HD = 64
ROT_DIM = 64
WINDOW = 128
ROPE_BASE = 10000.0
EPS = 1e-6
NEG_INF = -1e30

LANES = 128
SUBLANES = 8
ROW_TILE = 512
QKV_ROW_TILE = 512
PROMPT_SCORE_BUFFERS = 8
LATENT_SCORE_BUFFERS = 2
VMEM_LIMIT = 48 * 1024 * 1024
FFN_VMEM_LIMIT = 58 * 1024 * 1024
WEIGHT_STAGE_ROWS = 128
WEIGHT_STAGE_SLOTS = 3
QKV_STAGE_SLOTS = 4
FFN_CHUNK = 256
NT_DIMS = (((1,), (1,)), ((), ()))
LOG2E = math.log2(math.e)


def _params(n_axes):
    return pltpu.CompilerParams(dimension_semantics=("arbitrary",) * n_axes,
                                vmem_limit_bytes=VMEM_LIMIT)


def _resident(shape):
    return pl.BlockSpec(shape, lambda *_: (0,) * len(shape), pipeline_mode=pl.Buffered(1))


def _layer_resident(shape, layer):
    return pl.BlockSpec((None,) + tuple(shape[1:]), lambda *_: (layer,) + (0,) * (len(shape) - 1),
                        pipeline_mode=pl.Buffered(1))


def _rms(x, g):
    ms = jnp.mean(x * x, axis=-1, keepdims=True)
    return (x * lax.rsqrt(ms + EPS)) * g


def _half_masks(dtype):
    lane = lax.broadcasted_iota(jnp.int32, (1, LANES), 1)
    lo = lane < (LANES // 2)
    return jnp.where(lo, 1.0, 0.0).astype(dtype), jnp.where(lo, 0.0, 1.0).astype(dtype)


def _lo64():
    return lax.broadcasted_iota(jnp.int32, (1, LANES), 1) < (LANES // 2)


def _ones_column(rows):
    return jnp.ones((rows, LANES), BF16)


def _ones_row(cols):
    return jnp.ones((LANES, cols), BF16)


def _mod_kernel(cond_ref, w_ref, b_ref, out_ref):
    c = cond_ref[...]
    s = c * jax.nn.sigmoid(c)
    out_ref[...] = jnp.dot(s.astype(BF16), w_ref[...].astype(BF16),
                           preferred_element_type=F32) + b_ref[pl.ds(pl.program_id(0), 1), :]


def _modulation(cond8, w_mod, b_mod):
    depth, d, n = w_mod.shape
    tn = 1536
    return pl.pallas_call(
        _mod_kernel,
        out_shape=jax.ShapeDtypeStruct((depth, 8, n), F32),
        grid=(depth, n // tn),
        in_specs=[pl.BlockSpec((8, d), lambda i, j: (0, 0)),
                  pl.BlockSpec((None, d, tn), lambda i, j: (i, 0, j)),
                  pl.BlockSpec((depth, tn), lambda i, j: (0, j))],
        out_specs=pl.BlockSpec((None, 8, tn), lambda i, j: (i, 0, j)),
        compiler_params=_params(2),
        name="modulation",
    )(cond8, w_mod, b_mod)


def _rope_slab(xs, cos, sin_signed, lo16):
    left = pltpu.roll(xs, LANES - 16, 1)
    right = pltpu.roll(xs, 16, 1)
    return xs * cos + jnp.where(lo16, left, right) * sin_signed


def _lo16_mask():
    lane = lax.broadcasted_iota(jnp.int32, (1, LANES), 1)
    return (lane % 32) < 16


def _qkv_both_groups(refs, n_prompt_tiles, tiles_per_request, split_x, j, prompt_tile, latent_tile):
    refs = list(refs)
    x_refs = refs[:2] if split_x else refs[:1]
    g_ref, mod_ref, w_hbm, cos_ref, sin_ref = refs[len(x_refs):len(x_refs) + 5]
    outs = refs[len(x_refs) + 5:len(x_refs) + 11]
    w_vmem, stage, sem = refs[len(x_refs) + 11:]
    d = x_refs[0].shape[1]
    i = pl.program_id(0)

    @pl.when(i == 0)
    def _():
        _load_weights_as_bf16([(w_hbm.at[j], w_vmem)], {w_vmem.shape[1]: stage}, {w_vmem.shape[1]: sem})

    def pre_norm(x, r):
        shift, scale = mod_ref[pl.ds(r, 1), 0:d], mod_ref[pl.ds(r, 1), d:2 * d]
        return (_rms(x, g_ref[0:1, :] * (1 + scale)) + shift).astype(BF16)

    @pl.when(i < n_prompt_tiles)
    def _():
        prompt_tile(pre_norm(x_refs[0][...], 0), w_vmem, outs[:3])

    @pl.when(i >= n_prompt_tiles)
    def _():
        r = 1 + (i - n_prompt_tiles) // tiles_per_request
        latent_tile(pre_norm(x_refs[-1][...], r), w_vmem, outs[3:], (cos_ref[...], sin_ref[...], _lo16_mask()))


def _qkv_call(kernel, name, x_parts, g, mods, w, tables, prompt_outs, latent_outs, *, layer, n_p, n_s):
    d = x_parts[0].shape[1]
    tm = QKV_ROW_TILE
    tp, ts = n_p // tm, n_s // tm
    nt = tables[0].shape[0] // tm

    def prompt_block(shape):
        return pl.BlockSpec(shape, lambda i: (jnp.minimum(i, tp - 1),) + (0,) * (len(shape) - 1))

    def latent_block(shape):
        return pl.BlockSpec(shape, lambda i: (jnp.maximum(i - tp, 0),) + (0,) * (len(shape) - 1))

    if len(x_parts) == 2:
        x_specs = [prompt_block((tm, d)), latent_block((tm, d))]
    else:
        x_specs = [pl.BlockSpec((tm, d), lambda i: (i, 0))]
    table_spec = pl.BlockSpec((tm, LANES), lambda i: (jnp.maximum(i - tp, 0) % nt, 0))
    in_specs = x_specs + [_layer_resident(g.shape, layer), _layer_resident(mods.shape, layer),
                          pl.BlockSpec(memory_space=pl.ANY), table_spec, table_spec]
    out_shape, out_specs = [], []
    for outs, n, block in ((prompt_outs, n_p, prompt_block), (latent_outs, n_s, latent_block)):
        for shape_of, dtype in outs:
            out_shape.append(jax.ShapeDtypeStruct(shape_of(n), dtype))
            out_specs.append(block(shape_of(tm)))
    cols = w.shape[2]
    scratch = [pltpu.VMEM((d, cols), BF16), pltpu.VMEM((QKV_STAGE_SLOTS, WEIGHT_STAGE_ROWS, cols), F32),
               pltpu.SemaphoreType.DMA((QKV_STAGE_SLOTS,))]
    return pl.pallas_call(
        kernel,
        out_shape=tuple(out_shape),
        grid=(tp + ts,),
        in_specs=in_specs,
        out_specs=tuple(out_specs),
        scratch_shapes=scratch,
        compiler_params=_params(1),
        name=name,
    )(*x_parts, g, mods, w, *tables)


def _diff_tile(h, w_ref, outs, rope=None):
    q_ref, k_ref, v_ref = outs
    d = h.shape[1]
    cw = 512
    for c in range(3 * d // cw):
        acc = jnp.dot(h, w_ref[:, c * cw:(c + 1) * cw], preferred_element_type=F32)
        which, off = divmod(c * cw, d)
        dst = outs[which]
        for s in range(cw // LANES):
            xs = acc[:, s * LANES:(s + 1) * LANES]
            if rope is not None and which < 2:
                xs = _rope_slab(xs, *rope)
            if which == 0:
                xs = xs * (DIFF_HD ** -0.5 * LOG2E)
            lo = off + s * LANES
            if rope is not None or which == 0:
                dst[:, lo:lo + LANES] = xs.astype(dst.dtype)
            else:
                dst[pl.ds(lo // LANES, h.shape[0], stride=DIFF_HEADS), :] = xs


def _qkv_diff_kernel(*refs, j, n_prompt_tiles, tiles_per_request, split_x):
    _qkv_both_groups(refs, n_prompt_tiles, tiles_per_request, split_x, j, _diff_tile, _diff_tile)


def _qkv_diff(x_parts, g, mods, w, tables, *, layer, j, n_p, n_s, tiles_per_request):
    d = x_parts[0].shape[1]

    def rows(n):
        return (n, d)

    def cache(n):
        return (n * DIFF_HEADS, d // DIFF_HEADS)

    kernel = functools.partial(_qkv_diff_kernel, j=j, n_prompt_tiles=n_p // QKV_ROW_TILE,
                               tiles_per_request=tiles_per_request, split_x=len(x_parts) == 2)
    return _qkv_call(kernel, "qkv_diff", x_parts, g, mods, w, tables,
                     [(rows, BF16), (cache, F32), (cache, F32)], [(rows, BF16)] * 3,
                     layer=layer, n_p=n_p, n_s=n_s)


def _swa_q_tile(h, w_ref, q_ref, rope):
    d = h.shape[1]
    cw = 512
    for c in range(d // cw):
        acc = jnp.dot(h, w_ref[:, c * cw:(c + 1) * cw], preferred_element_type=F32)
        for s in range(cw // LANES):
            xs = acc[:, s * LANES:(s + 1) * LANES]
            if rope is not None:
                xs = _rope_slab(xs, *rope)
            lo = c * cw + s * LANES
            q_ref[:, lo:lo + LANES] = (xs * (SWA_HD ** -0.5 * LOG2E)).astype(BF16)
    nkv = SWA_KV_HEADS * SWA_HD
    return jnp.dot(h, w_ref[:, d:d + 2 * nkv], preferred_element_type=F32), nkv


def _swa_prompt_tile(h, w_ref, outs, *, seq):
    q_ref, kt_ref, vt_ref = outs
    kv, nkv = _swa_q_tile(h, w_ref, q_ref, None)
    for b in range(h.shape[0] // seq):
        kt_ref[b] = kv[b * seq:(b + 1) * seq, :nkv].T
        vt_ref[b] = kv[b * seq:(b + 1) * seq, nkv:].T


def _swa_latent_tile(h, w_ref, outs, rope):
    q_ref, kd_ref, vd_ref = outs
    kv, nkv = _swa_q_tile(h, w_ref, q_ref, rope)
    lo64 = _lo64()
    for which, dst in enumerate((kd_ref, vd_ref)):
        for s in range(nkv // LANES):
            xs = kv[:, which * nkv + s * LANES: which * nkv + (s + 1) * LANES]
            if which == 0:
                xs = _rope_slab(xs, *rope)
            sw = pltpu.roll(xs, LANES // 2, 1)
            dst[:, (2 * s) * LANES:(2 * s + 1) * LANES] = jnp.where(lo64, xs, sw).astype(BF16)
            dst[:, (2 * s + 1) * LANES:(2 * s + 2) * LANES] = jnp.where(lo64, sw, xs).astype(BF16)


def _qkv_swa_kernel(*refs, j, n_prompt_tiles, tiles_per_request, split_x, seq):
    _qkv_both_groups(refs, n_prompt_tiles, tiles_per_request, split_x, j,
                     functools.partial(_swa_prompt_tile, seq=seq), _swa_latent_tile)


def _qkv_swa(x_parts, g, mods, w, tables, *, layer, j, n_p, n_s, tiles_per_request, seq):
    d = x_parts[0].shape[1]
    nkv = SWA_KV_HEADS * SWA_HD

    def rows(n):
        return (n, d)

    def feature_major(n):
        return (n // seq, nkv, seq)

    def duplicated(n):
        return (n, 2 * nkv)

    kernel = functools.partial(_qkv_swa_kernel, j=j, n_prompt_tiles=n_p // QKV_ROW_TILE,
                               tiles_per_request=tiles_per_request, split_x=len(x_parts) == 2, seq=seq)
    return _qkv_call(kernel, "qkv_swa", x_parts, g, mods, w, tables,
                     [(rows, BF16), (feature_major, F32), (feature_major, F32)],
                     [(rows, BF16), (duplicated, BF16), (duplicated, BF16)],
                     layer=layer, n_p=n_p, n_s=n_s)


def _diff_lambda(lam_ref, lam_init):
    lp = lam_ref[...]
    a = jnp.sum(lp[0:1] * lp[1:2], axis=-1, keepdims=True)
    b = jnp.sum(lp[2:3] * lp[3:4], axis=-1, keepdims=True)
    return jnp.exp(a) - jnp.exp(b) + lam_init


def _diff_combine(acc, tq, lam, g, lam_init):
    o12 = acc[:, :LANES] / acc[:, LANES:]
    o = o12[:tq] - lam * o12[tq:]
    return _rms(o, g) * (1.0 - lam_init)


def _stack_maps(q):
    m_lo, m_hi = _half_masks(BF16)
    return jnp.concatenate([q * m_lo, q * m_hi], axis=0)


def _run_pipelined(items, scores, finish, s_bufs):
    depth = len(s_bufs)
    states = {i: scores(items[i], s_bufs[i]) for i in range(min(depth - 1, len(items)))}
    for i, item in enumerate(items):
        ahead = i + depth - 1
        if ahead < len(items):
            states[ahead] = scores(items[ahead], s_bufs[ahead % depth])
        finish(item, s_bufs[i % depth], states.pop(i))


def _store_scores(s_ref, col0, s, mrun):
    s_ref[:, col0:col0 + s.shape[1]] = s
    for t in range(s.shape[1] // LANES):
        blk = s[:, t * LANES:(t + 1) * LANES]
        mrun = blk if mrun is None else jnp.maximum(mrun, blk)
    return mrun


def _exp_block(s_ref, col0, width, mb):
    return jnp.concatenate(
        [jnp.exp2(s_ref[:, col0 + t * LANES:col0 + (t + 1) * LANES] - mb).astype(BF16)
         for t in range(width // LANES)], axis=1)


def _diff_prompt_kernel(q_ref, k_ref, v_ref, lam_ref, g_ref, o_ref, *s_bufs, lam_init, seq):
    lam = _diff_lambda(lam_ref, lam_init)
    g = g_ref[...]
    ones = _ones_column(seq)
    items = [(r, h) for r in range(q_ref.shape[0] // seq) for h in range(DIFF_HEADS)]

    def head_rows(ref, r, h):
        return ref[pl.ds(r * seq * DIFF_HEADS + h, seq, stride=DIFF_HEADS), :]

    def scores(item, s_ref):
        r, h = item
        rows, sl = slice(r * seq, (r + 1) * seq), slice(h * LANES, (h + 1) * LANES)
        s = lax.dot_general(_stack_maps(q_ref[rows, sl]), head_rows(k_ref, r, h).astype(BF16), NT_DIMS,
                            preferred_element_type=F32)
        return _store_scores(s_ref, 0, s, None)

    def finish(item, s_ref, mrun):
        r, h = item
        rows, sl = slice(r * seq, (r + 1) * seq), slice(h * LANES, (h + 1) * LANES)
        mb = jnp.broadcast_to(jnp.max(mrun, axis=-1, keepdims=True), (2 * seq, LANES))
        vx = jnp.concatenate([head_rows(v_ref, r, h).astype(BF16), ones], axis=1)
        acc = jnp.dot(_exp_block(s_ref, 0, seq, mb), vx, preferred_element_type=F32)
        o_ref[rows, sl] = _diff_combine(acc, seq, lam, g, lam_init).astype(BF16)

    _run_pipelined(items, scores, finish, s_bufs)


def _diff_prompt_attention(q, k, v, lam_params, subln_g, *, j, seq, lam_init):
    n, d = q.shape
    req = 2
    spec = pl.BlockSpec((req * seq, d), lambda b: (b, 0))
    kv_spec = pl.BlockSpec((req * seq * DIFF_HEADS, d // DIFF_HEADS), lambda b: (b, 0))
    return pl.pallas_call(
        functools.partial(_diff_prompt_kernel, lam_init=lam_init, seq=seq),
        out_shape=jax.ShapeDtypeStruct((n, d), BF16),
        grid=(n // (req * seq),),
        in_specs=[spec, kv_spec, kv_spec, _layer_resident(lam_params.shape, j),
                  _layer_resident(subln_g.shape, j)],
        out_specs=spec,
        scratch_shapes=[pltpu.VMEM((2 * seq, seq), F32)] * PROMPT_SCORE_BUFFERS,
        compiler_params=_params(1),
        name="diff_attn_prompt",
    )(q, k, v, lam_params, subln_g)


def _diff_latent_kernel(q_ref, kc_ref, vc_ref, kl_ref, vl_ref, lam_ref, g_ref, o_ref,
                        kk_ref, vx_ref, *s_bufs, lam_init, lc, tq, key_chunk):
    seq = q_ref.shape[0]
    heads = kk_ref.shape[0]
    nk = kk_ref.shape[1]
    for hh in range(heads):
        h = pl.program_id(1) * heads + hh
        sl = slice(hh * LANES, (hh + 1) * LANES)
        kk_ref[hh, 0:lc, :] = kc_ref[pl.ds(h, lc, stride=DIFF_HEADS), :].astype(BF16)
        kk_ref[hh, lc:, :] = kl_ref[:, sl]
        vx_ref[hh, 0:lc, 0:LANES] = vc_ref[pl.ds(h, lc, stride=DIFF_HEADS), :].astype(BF16)
        vx_ref[hh, lc:, 0:LANES] = vl_ref[:, sl]
        vx_ref[hh, :, LANES:2 * LANES] = _ones_column(nk)

    masks = _half_masks(BF16)
    nchunk = nk // key_chunk
    items = [(hh, rt, m) for hh in range(heads) for rt in range(seq // tq) for m in range(2)]
    lam = _diff_lambda(lam_ref, lam_init)
    g = g_ref[...]
    first_map = {}

    def scores(item, s_ref):
        hh, rt, m = item
        q = q_ref[rt * tq:(rt + 1) * tq, hh * LANES:(hh + 1) * LANES] * masks[m]
        mrun = None
        for c in range(nchunk):
            s = lax.dot_general(q, kk_ref[hh, c * key_chunk:(c + 1) * key_chunk, :], NT_DIMS,
                                preferred_element_type=F32)
            mrun = _store_scores(s_ref, c * key_chunk, s, mrun)
        return mrun

    def finish(item, s_ref, mrun):
        hh, rt, m = item
        mb = jnp.broadcast_to(jnp.max(mrun, axis=-1, keepdims=True), (tq, LANES))
        acc = None
        for c in range(nchunk):
            part = jnp.dot(_exp_block(s_ref, c * key_chunk, key_chunk, mb),
                           vx_ref[hh, c * key_chunk:(c + 1) * key_chunk, :], preferred_element_type=F32)
            acc = part if acc is None else acc + part
        o_m = acc[:, :LANES] / acc[:, LANES:]
        if m == 0:
            first_map[hh, rt] = o_m
        else:
            o = _rms(first_map.pop((hh, rt)) - lam * o_m, g) * (1.0 - lam_init)
            o_ref[rt * tq:(rt + 1) * tq, hh * LANES:(hh + 1) * LANES] = o.astype(BF16)

    _run_pipelined(items, scores, finish, s_bufs)


def _diff_latent_attention(q, k, v, cache_k, cache_v, lam_params, subln_g, *, j, seq, lc, lam_init):
    n, d = q.shape
    nb = cache_k.shape[0]
    tq = 512
    key_chunk = 512
    heads = 1
    q_spec = pl.BlockSpec((seq, heads * LANES), lambda b, h: (b, h))
    c_spec = pl.BlockSpec((None, lc * DIFF_HEADS, LANES), lambda b, h: (b, j, 0))
    return pl.pallas_call(
        functools.partial(_diff_latent_kernel, lam_init=lam_init, lc=lc, tq=tq, key_chunk=key_chunk),
        out_shape=jax.ShapeDtypeStruct((n, d), BF16),
        grid=(nb, DIFF_HEADS // heads),
        in_specs=[q_spec, c_spec, c_spec, q_spec, q_spec,
                  _layer_resident(lam_params.shape, j), _layer_resident(subln_g.shape, j)],
        out_specs=q_spec,
        scratch_shapes=[pltpu.VMEM((heads, lc + seq, LANES), BF16),
                        pltpu.VMEM((heads, lc + seq, 2 * LANES), BF16),
                        *[pltpu.VMEM((tq, lc + seq), F32)] * LATENT_SCORE_BUFFERS],
        compiler_params=_params(2),
        name="diff_attn_latent",
    )(q, cache_k, cache_v, k, v, lam_params, subln_g)


def _stack_group(q_ref, rows, kv_local):
    m_lo, m_hi = _half_masks(BF16)
    parts = []
    for gb in range(SWA_GROUP // 2):
        blk = kv_local * (SWA_GROUP // 2) + gb
        qb = q_ref[rows, blk * LANES:(blk + 1) * LANES]
        parts += [qb * m_lo, qb * m_hi]
    return jnp.concatenate(parts, axis=0)


def _sink_column(sink_ref, first_head, tq):
    return jnp.concatenate([jnp.full((tq, LANES), sink_ref[first_head + g] * LOG2E, F32)
                            for g in range(SWA_GROUP)], axis=0)


def _write_group(o_ref, rows, kv_local, o, tq):
    lo64 = _lo64()
    for gb in range(SWA_GROUP // 2):
        blk = kv_local * (SWA_GROUP // 2) + gb
        even = o[(2 * gb) * tq:(2 * gb + 1) * tq]
        odd = o[(2 * gb + 1) * tq:(2 * gb + 2) * tq]
        o_ref[rows, blk * LANES:(blk + 1) * LANES] = jnp.where(lo64, even, odd).astype(BF16)


def _dup_rows(x_t):
    xb = x_t.astype(BF16)
    return jnp.concatenate([xb, xb], axis=0)


def _sink_finish(mrun, sk, rows):
    mb = jnp.maximum(jnp.broadcast_to(jnp.max(mrun, axis=-1, keepdims=True), (rows, LANES)), sk)
    return mb, jnp.exp2(sk - mb)


def _swa_prompt_kernel(sink_ref, q_ref, kt_ref, vt_ref, o_ref, *s_bufs):
    seq = kt_ref.shape[2]
    rows = SWA_GROUP * seq
    ones = _ones_row(seq)
    items = [(r, j) for r in range(kt_ref.shape[0]) for j in range(SWA_KV_HEADS)]

    def scores(item, s_ref):
        r, j = item
        kd = _dup_rows(kt_ref[r, j * SWA_HD:(j + 1) * SWA_HD, :])
        s = jnp.dot(_stack_group(q_ref, slice(r * seq, (r + 1) * seq), j), kd, preferred_element_type=F32)
        return _store_scores(s_ref, 0, s, None)

    def finish(item, s_ref, mrun):
        r, j = item
        sk = _sink_column(sink_ref, j * SWA_GROUP, seq)
        mb, sink_term = _sink_finish(mrun, sk, rows)
        vx = jnp.concatenate([_dup_rows(vt_ref[r, j * SWA_HD:(j + 1) * SWA_HD, :]), ones], axis=0)
        acc = lax.dot_general(_exp_block(s_ref, 0, seq, mb), vx, NT_DIMS, preferred_element_type=F32)
        o = acc[:, :LANES] / (acc[:, LANES:] + sink_term)
        _write_group(o_ref, slice(r * seq, (r + 1) * seq), j, o, seq)

    _run_pipelined(items, scores, finish, s_bufs)


def _swa_prompt_attention(q, kt, vt, sink, *, seq):
    n, d = q.shape
    nkv = kt.shape[1]
    req = 2
    t_spec = pl.BlockSpec((req, nkv, seq), lambda b: (b, 0, 0))
    return pl.pallas_call(
        _swa_prompt_kernel,
        out_shape=jax.ShapeDtypeStruct((n, d), BF16),
        grid=(n // (req * seq),),
        in_specs=[pl.BlockSpec(memory_space=pltpu.SMEM),
                  pl.BlockSpec((req * seq, d), lambda b: (b, 0)), t_spec, t_spec],
        out_specs=pl.BlockSpec((req * seq, d), lambda b: (b, 0)),
        scratch_shapes=[pltpu.VMEM((SWA_GROUP * seq, seq), F32)] * (PROMPT_SCORE_BUFFERS // 2),
        compiler_params=_params(1),
        name="swa_attn_prompt",
    )(sink, q, kt, vt)


def _swa_latent_kernel(sink_ref, q_ref, kc_ref, vc_ref, kl_ref, vl_ref, o_ref, kcd_ref, vcx_ref,
                       s0_ref, s1_ref, *, tq, span):
    pair = pl.program_id(1)
    tiles = q_ref.shape[0] // tq
    first_tile = pl.program_id(2) * tiles
    seq = kl_ref.shape[0]
    lc = kc_ref.shape[1]
    rows = SWA_GROUP * tq
    ones_row = _ones_row(lc)
    for jj in range(2):
        kcd_ref[jj] = _dup_rows(kc_ref[jj * SWA_HD:(jj + 1) * SWA_HD, :])
        vcx_ref[jj] = jnp.concatenate([_dup_rows(vc_ref[jj * SWA_HD:(jj + 1) * SWA_HD, :]), ones_row], axis=0)
    ones_col = _ones_column(span)
    items = [(t, jj) for t in range(tiles) for jj in range(2)]
    windows, biases = {}, {}

    def window(t):
        if t not in windows:
            q0 = (first_tile + t) * tq
            windows[t] = (q0, pl.multiple_of(jnp.clip(q0 - WINDOW, 0, seq - span), WINDOW))
        return windows[t]

    def bias_for(t):
        if t not in biases:
            q0, ws = window(t)
            q_pos = q0 + lax.broadcasted_iota(jnp.int32, (tq, span), 0)
            k_pos = ws + lax.broadcasted_iota(jnp.int32, (tq, span), 1)
            b = jnp.where(jnp.abs(q_pos - k_pos) <= WINDOW, 0.0, NEG_INF).astype(F32)
            biases[t] = jnp.concatenate([b] * SWA_GROUP, axis=0)
        return biases[t]

    def scores(item, s_ref):
        t, jj = item
        _, ws = window(t)
        qs = _stack_group(q_ref, slice(t * tq, (t + 1) * tq), jj)
        s_c = jnp.dot(qs, kcd_ref[jj], preferred_element_type=F32)
        mrun = _store_scores(s_ref, 0, s_c, None)
        s_w = lax.dot_general(qs, kl_ref[pl.ds(ws, span), jj * LANES:(jj + 1) * LANES], NT_DIMS,
                              preferred_element_type=F32) + bias_for(t)
        return _store_scores(s_ref, lc, s_w, mrun)

    def finish(item, s_ref, mrun):
        t, jj = item
        _, ws = window(t)
        sk = _sink_column(sink_ref, (2 * pair + jj) * SWA_GROUP, tq)
        mb, sink_term = _sink_finish(mrun, sk, rows)
        vwx = jnp.concatenate([vl_ref[pl.ds(ws, span), jj * LANES:(jj + 1) * LANES], ones_col], axis=1)
        acc = (lax.dot_general(_exp_block(s_ref, 0, lc, mb), vcx_ref[jj], NT_DIMS, preferred_element_type=F32)
               + jnp.dot(_exp_block(s_ref, lc, span, mb), vwx, preferred_element_type=F32))
        o = acc[:, :LANES] / (acc[:, LANES:] + sink_term)
        _write_group(o_ref, slice(t * tq, (t + 1) * tq), jj, o, tq)

    _run_pipelined(items, scores, finish, (s0_ref, s1_ref))


def _swa_latent_attention(q, kd, vd, cache_kt, cache_vt, sink, *, j, seq):
    n, d = q.shape
    nb, _, lc = cache_kt.shape
    tq = 256
    span = tq + 2 * WINDOW
    npair = SWA_KV_HEADS // 2
    wq = d // npair
    parts = 2
    q_spec = pl.BlockSpec((seq // parts, wq), lambda b, p, i: (b * parts + i, p))
    c_spec = pl.BlockSpec((None, 2 * SWA_HD, lc), lambda b, p, i: (b, j * npair + p, 0))
    l_spec = pl.BlockSpec((seq, 2 * LANES), lambda b, p, i: (b, p))
    s_shape = pltpu.VMEM((SWA_GROUP * tq, lc + span), F32)
    return pl.pallas_call(
        functools.partial(_swa_latent_kernel, tq=tq, span=span),
        out_shape=jax.ShapeDtypeStruct((n, d), BF16),
        grid=(nb, npair, parts),
        in_specs=[pl.BlockSpec(memory_space=pltpu.SMEM), q_spec, c_spec, c_spec, l_spec, l_spec],
        out_specs=q_spec,
        scratch_shapes=[pltpu.VMEM((2, 2 * SWA_HD, lc), BF16), pltpu.VMEM((2, 2 * LANES, lc), BF16),
                        s_shape, s_shape],
        compiler_params=_params(3),
        name="swa_attn_latent",
    )(sink, q, cache_kt, cache_vt, kd, vd)


def _load_weights_as_bf16(jobs, stages, sems):
    order = []
    rings = {w: [] for w in stages}
    for src, dst in jobs:
        w = src.shape[1]
        slots = stages[w].shape[0]
        for k in range(src.shape[0] // WEIGHT_STAGE_ROWS):
            rows = pl.ds(k * WEIGHT_STAGE_ROWS, WEIGHT_STAGE_ROWS)
            slot = len(rings[w]) % slots
            copy = pltpu.make_async_copy(src.at[rows, :], stages[w].at[slot], sems[w].at[slot])
            order.append((w, len(rings[w])))
            rings[w].append((copy, slot, dst, rows))
    for w, ring in rings.items():
        for copy, _, _, _ in ring[:stages[w].shape[0]]:
            copy.start()
    for w, k in order:
        copy, slot, dst, rows = rings[w][k]
        copy.wait()
        dst[rows, :] = stages[w][slot].astype(BF16)
        ahead = k + stages[w].shape[0]
        if ahead < len(rings[w]):
            rings[w][ahead][0].start()


def _weight_stream(pieces, rings):
    per_ring = {name: [] for name in rings}
    plan = []
    for src, dst, idx, name in pieces:
        stage, sem = rings[name]
        slot = len(per_ring[name]) % stage.shape[0]
        plan.append((name, len(per_ring[name])))
        per_ring[name].append((pltpu.make_async_copy(src, stage.at[slot], sem.at[slot]), slot, dst, idx))
    cursor = [0]

    def prime():
        for name, ring in per_ring.items():
            for copy, _, _, _ in ring[:rings[name][0].shape[0]]:
                copy.start()

    def take(n):
        for name, k in plan[cursor[0]:cursor[0] + n]:
            copy, slot, dst, idx = per_ring[name][k]
            copy.wait()
            dst[idx] = rings[name][0][slot].astype(BF16)
            ahead = k + rings[name][0].shape[0]
            if ahead < len(per_ring[name]):
                per_ring[name][ahead][0].start()
        cursor[0] += n

    return prime, take


def _post_attn_ffn_kernel(*refs, layer, j, n_prompt_tiles, tiles_per_request, split_x, split_out):
    refs = list(refs)
    op_ref, os_ref = refs[:2]
    x_refs = refs[2:4] if split_x else refs[2:3]
    wo_hbm, wg_hbm, wu_hbm, wd_hbm, g_ref, mod_ref = refs[2 + len(x_refs):8 + len(x_refs)]
    n_out = 2 if split_out else 1
    out_refs = refs[8 + len(x_refs):8 + len(x_refs) + n_out]
    wo_ref, wg_ref, wu_ref, wd_ref, stage_row, stage_col, sem_row, sem_col = refs[8 + len(x_refs) + n_out:]
    d = wo_ref.shape[1]
    dff = wg_ref.shape[1]
    cw = FFN_CHUNK
    i = pl.program_id(0)
    is_prompt = i < n_prompt_tiles
    r = jnp.where(is_prompt, 0, 1 + (i - n_prompt_tiles) // tiles_per_request)

    def mod(slot):
        return mod_ref[pl.ds(r, 1), slot * d:(slot + 1) * d]

    def tile(before_out_proj, before_chunk):
        o = jnp.where(is_prompt, op_ref[...], os_ref[...])
        x = jnp.where(is_prompt, x_refs[0][...], x_refs[1][...]) if split_x else x_refs[0][...]
        before_out_proj()
        y = jnp.dot(o, wo_ref[...], preferred_element_type=F32)
        x = x + _rms(y, mod(2) * g_ref[1:2, :])
        h = (_rms(x, g_ref[2:3, :] * (1 + mod(4))) + mod(3)).astype(BF16)
        y = jnp.zeros((h.shape[0], d), F32)
        for c in range(dff // cw):
            before_chunk(c)
            cols = slice(c * cw, (c + 1) * cw)
            a = jnp.dot(h, wg_ref[:, cols], preferred_element_type=F32)
            u = jnp.dot(h, wu_ref[:, cols], preferred_element_type=F32)
            t = (a * jax.nn.sigmoid(a)) * u
            y = y + jnp.dot(t.astype(BF16), wd_ref[cols, :], preferred_element_type=F32)
        out = x + _rms(y, mod(5) * g_ref[3:4, :])
        if split_out:
            @pl.when(is_prompt)
            def _():
                out_refs[0][...] = out

            @pl.when(jnp.logical_not(is_prompt))
            def _():
                out_refs[1][...] = out
        else:
            out_refs[0][...] = out

    @pl.when(i == 0)
    def _():
        row_chunk = stage_row.shape[1]
        pieces = [(wo_hbm.at[j].at[pl.ds(k * row_chunk, row_chunk), :], wo_ref,
                   (pl.ds(k * row_chunk, row_chunk), slice(None)), "row") for k in range(d // row_chunk)]
        for c in range(dff // cw):
            cols = pl.ds(c * cw, cw)
            pieces += [(wg_hbm.at[layer].at[:, cols], wg_ref, (slice(None), cols), "col"),
                       (wu_hbm.at[layer].at[:, cols], wu_ref, (slice(None), cols), "col"),
                       (wd_hbm.at[layer].at[cols, :], wd_ref, (cols, slice(None)), "row")]
        prime, take = _weight_stream(pieces, {"row": (stage_row, sem_row), "col": (stage_col, sem_col)})
        prime()
        tile(lambda: take(d // row_chunk), lambda c: take(3))

    @pl.when(i > 0)
    def _():
        tile(lambda: None, lambda c: None)


def _post_attn_ffn(o_p, o_s, xs_in, w_o, wg, wu, wd, g, mods, *, layer, j, tiles_per_request, split_out):
    n_p, d = o_p.shape
    n_s = o_s.shape[0]
    tm = ROW_TILE
    tp, ts = n_p // tm, n_s // tm
    prompt_rows = pl.BlockSpec((tm, d), lambda i: (jnp.minimum(i, tp - 1), 0))
    latent_rows = pl.BlockSpec((tm, d), lambda i: (jnp.maximum(i - tp, 0), 0))
    all_rows = pl.BlockSpec((tm, d), lambda i: (i, 0))
    split_x = len(xs_in) == 2
    in_specs = [prompt_rows, latent_rows] + ([prompt_rows, latent_rows] if split_x else [all_rows])
    hbm = pl.BlockSpec(memory_space=pl.ANY)
    in_specs += [hbm, hbm, hbm, hbm, _layer_resident(g.shape, layer), _layer_resident(mods.shape, layer)]
    if split_out:
        out_shape = (jax.ShapeDtypeStruct((n_p, d), F32), jax.ShapeDtypeStruct((n_s, d), F32))
        out_specs = (prompt_rows, latent_rows)
    else:
        out_shape = jax.ShapeDtypeStruct((n_p + n_s, d), F32)
        out_specs = all_rows
    dff = wg.shape[2]
    scratch = [pltpu.VMEM((d, d), BF16), pltpu.VMEM((d, dff), BF16), pltpu.VMEM((d, dff), BF16),
               pltpu.VMEM((dff, d), BF16),
               pltpu.VMEM((WEIGHT_STAGE_SLOTS, FFN_CHUNK, d), F32),
               pltpu.VMEM((WEIGHT_STAGE_SLOTS, d, FFN_CHUNK), F32),
               pltpu.SemaphoreType.DMA((WEIGHT_STAGE_SLOTS,)), pltpu.SemaphoreType.DMA((WEIGHT_STAGE_SLOTS,))]
    return pl.pallas_call(
        functools.partial(_post_attn_ffn_kernel, layer=layer, j=j, n_prompt_tiles=tp,
                          tiles_per_request=tiles_per_request, split_x=split_x, split_out=split_out),
        out_shape=out_shape,
        grid=(tp + ts,),
        in_specs=in_specs,
        out_specs=out_specs,
        scratch_shapes=scratch,
        compiler_params=pltpu.CompilerParams(dimension_semantics=("arbitrary",),
                                             vmem_limit_bytes=FFN_VMEM_LIMIT),
        name="post_attn_ffn",
    )(o_p, o_s, *xs_in, w_o, wg, wu, wd, g, mods)


def _rope_tables(n_lat):
    t = np.arange(n_lat)
    row = (t // GRID_W).astype(np.float32)
    col = (t % GRID_W).astype(np.float32)
    nf = ROT_DIM // 4
    inv = np.float32(ROPE_BASE) ** (-np.arange(nf, dtype=np.float32) / np.float32(nf))
    ar = row[:, None] * inv[None, :]
    ac = col[:, None] * inv[None, :]
    ang = np.concatenate([ar, ar, ac, ac], axis=-1)
    cos, sin = np.cos(ang), np.sin(ang)
    sign = np.where((np.arange(ROT_DIM) % 32) < 16, -1.0, 1.0).astype(np.float32)
    reps = LANES // ROT_DIM
    return jnp.asarray(np.tile(cos, (1, reps))), jnp.asarray(np.tile(sin * sign, (1, reps)))


def _swa_cache_to_feature_major(cache):
    nb, nl, lc, nh, hd = cache.shape
    return cache.transpose(0, 1, 3, 4, 2).reshape(nb, nl * nh * hd, lc)


def _swa_cache_from_feature_major(xt, seq):
    nb = xt.shape[0]
    return xt.reshape(nb, SWA_KV_HEADS, SWA_HD, seq).transpose(0, 3, 1, 2)


def kernel(x_prompt, x_sample, cache_diff_k, cache_diff_v, cache_swa_k, cache_swa_v, c, c_ctx,
           w_mod, b_mod, norm_g, w_qkv_diff, diff_lambda, diff_subln_g, w_o_diff,
           w_qkv_swa, swa_sink, w_o_swa, w_gate, w_up, w_down):
    bp, lp, d = x_prompt.shape
    bs, ls, _ = x_sample.shape
    lc = cache_diff_k.shape[2]
    depth = w_mod.shape[0]
    tm = ROW_TILE

    cond8 = jnp.concatenate([c_ctx[None, :], c, jnp.zeros((8 - 1 - bs, d), F32)], axis=0)
    mods = _modulation(cond8, w_mod, b_mod)
    tables = _rope_tables(ls)

    cdk = cache_diff_k.reshape(bs, -1, 2 * DIFF_HD)
    cdv = cache_diff_v.reshape(bs, -1, 2 * DIFF_HD)
    cskt = _swa_cache_to_feature_major(cache_swa_k)
    csvt = _swa_cache_to_feature_major(cache_swa_v)

    n_p, n_s = bp * lp, bs * ls
    x_parts = (x_prompt.reshape(n_p, d), x_sample.reshape(n_s, d))
    g = norm_g
    sub_g = diff_subln_g.reshape(-1, 1, 2 * DIFF_HD)
    diff_k_out, diff_v_out, swa_k_out, swa_v_out = [], [], [], []

    for i in range(depth):
        j = i // N_MIXERS
        if i % N_MIXERS == 0:
            lam_init = 0.8 - 0.6 * math.exp(-0.3 * i)
            qp, kp, vp, qs, ks, vs = _qkv_diff(x_parts, g, mods, w_qkv_diff, tables, layer=i, j=j,
                                               n_p=n_p, n_s=n_s, tiles_per_request=ls // QKV_ROW_TILE)
            op = _diff_prompt_attention(qp, kp, vp, diff_lambda, sub_g, j=j, seq=lp, lam_init=lam_init)
            diff_k_out.append(kp.reshape(bp, lp, DIFF_HEADS, 2 * DIFF_HD))
            diff_v_out.append(vp.reshape(bp, lp, DIFF_HEADS, 2 * DIFF_HD))
            os_ = _diff_latent_attention(qs, ks, vs, cdk, cdv, diff_lambda, sub_g,
                                         j=j, seq=ls, lc=lc, lam_init=lam_init)
            w_o = w_o_diff
        else:
            qp, ktp, vtp, qs, kds, vds = _qkv_swa(x_parts, g, mods, w_qkv_swa, tables, layer=i, j=j,
                                                  n_p=n_p, n_s=n_s, tiles_per_request=ls // QKV_ROW_TILE,
                                                  seq=lp)
            op = _swa_prompt_attention(qp, ktp, vtp, swa_sink[j], seq=lp)
            swa_k_out.append(_swa_cache_from_feature_major(ktp, lp))
            swa_v_out.append(_swa_cache_from_feature_major(vtp, lp))
            os_ = _swa_latent_attention(qs, kds, vds, cskt, csvt, swa_sink[j], j=j, seq=ls)
            w_o = w_o_swa
        last = i == depth - 1
        out = _post_attn_ffn(op, os_, x_parts, w_o, w_gate, w_up, w_down, g, mods, layer=i, j=j,
                             tiles_per_request=ls // tm, split_out=last)
        x_parts = out if last else (out,)
    xp, xs = x_parts

    return (xp.reshape(bp, lp, d), xs.reshape(bs, ls, d),
            jnp.stack(diff_k_out, axis=1), jnp.stack(diff_v_out, axis=1),
            jnp.stack(swa_k_out, axis=1), jnp.stack(swa_v_out, axis=1))
```

```python
import functools
import math

import jax
import jax.numpy as jnp
import numpy as np
from jax import lax
from jax.experimental import pallas as pl
from jax.experimental.pallas import tpu as pltpu

F32 = jnp.float32
BF16 = jnp.bfloat16

GRID_W = 64
N_MIXERS = 2
DIFF_HEADS = 8
DIFF_HD = 64
SWA_HEADS = 16
SWA_KV_HEADS = 4
SWA_GROUP = SWA_HEADS // SWA_KV_HEADS
SWA_HD = 64
ROT_DIM = 64
WINDOW = 128
ROPE_BASE = 10000.0
EPS = 1e-6
NEG_INF = -1e30

LANES = 128
SUBLANES = 8
ROW_TILE = 512
QKV_ROW_TILE = 512
PROMPT_SCORE_BUFFERS = 8
LATENT_SCORE_BUFFERS = 2
VMEM_LIMIT = 48 * 1024 * 1024
FFN_VMEM_LIMIT = 58 * 1024 * 1024
WEIGHT_STAGE_ROWS = 128
WEIGHT_STAGE_SLOTS = 3
WEIGHT_COLUMN_SLOTS = 2
QKV_STAGE_SLOTS = 4
FFN_CHUNK = 256
NT_DIMS = (((1,), (1,)), ((), ()))
LOG2E = math.log2(math.e)


def _params(n_axes):
    return pltpu.CompilerParams(dimension_semantics=("arbitrary",) * n_axes,
                                vmem_limit_bytes=VMEM_LIMIT)


def _resident(shape):
    return pl.BlockSpec(shape, lambda *_: (0,) * len(shape), pipeline_mode=pl.Buffered(1))


def _layer_resident(shape, layer):
    return pl.BlockSpec((None,) + tuple(shape[1:]), lambda *_: (layer,) + (0,) * (len(shape) - 1),
                        pipeline_mode=pl.Buffered(1))


def _rms(x, g):
    ms = jnp.mean(x * x, axis=-1, keepdims=True)
    return (x * lax.rsqrt(ms + EPS)) * g


def _half_masks(dtype):
    lane = lax.broadcasted_iota(jnp.int32, (1, LANES), 1)
    lo = lane < (LANES // 2)
    return jnp.where(lo, 1.0, 0.0).astype(dtype), jnp.where(lo, 0.0, 1.0).astype(dtype)


def _lo64():
    return lax.broadcasted_iota(jnp.int32, (1, LANES), 1) < (LANES // 2)


def _ones_column(rows):
    return jnp.ones((rows, LANES), BF16)


def _ones_row(cols):
    return jnp.ones((LANES, cols), BF16)


def _mod_kernel(cond_ref, w_ref, b_ref, out_ref):
    c = cond_ref[...]
    s = c * jax.nn.sigmoid(c)
    out_ref[...] = jnp.dot(s.astype(BF16), w_ref[...].astype(BF16),
                           preferred_element_type=F32) + b_ref[pl.ds(pl.program_id(0), 1), :]


def _modulation(cond8, w_mod, b_mod):
    depth, d, n = w_mod.shape
    tn = 1536
    return pl.pallas_call(
        _mod_kernel,
        out_shape=jax.ShapeDtypeStruct((depth, 8, n), F32),
        grid=(depth, n // tn),
        in_specs=[pl.BlockSpec((8, d), lambda i, j: (0, 0)),
                  pl.BlockSpec((None, d, tn), lambda i, j: (i, 0, j)),
                  pl.BlockSpec((depth, tn), lambda i, j: (0, j))],
        out_specs=pl.BlockSpec((None, 8, tn), lambda i, j: (i, 0, j)),
        compiler_params=_params(2),
        name="modulation",
    )(cond8, w_mod, b_mod)


def _rope_slab(xs, cos, sin_signed, lo16):
    left = pltpu.roll(xs, LANES - 16, 1)
    right = pltpu.roll(xs, 16, 1)
    return xs * cos + jnp.where(lo16, left, right) * sin_signed


def _lo16_mask():
    lane = lax.broadcasted_iota(jnp.int32, (1, LANES), 1)
    return (lane % 32) < 16


def _qkv_both_groups(refs, n_prompt_tiles, tiles_per_request, split_x, j, prompt_tile, latent_tile):
    refs = list(refs)
    x_refs = refs[:2] if split_x else refs[:1]
    g_ref, mod_ref, w_hbm, cos_ref, sin_ref = refs[len(x_refs):len(x_refs) + 5]
    outs = refs[len(x_refs) + 5:len(x_refs) + 11]
    w_vmem, stage, sem = refs[len(x_refs) + 11:]
    d = x_refs[0].shape[1]
    i = pl.program_id(0)

    @pl.when(i == 0)
    def _():
        _load_weights_as_bf16([(w_hbm.at[j], w_vmem)], {w_vmem.shape[1]: stage}, {w_vmem.shape[1]: sem})

    def pre_norm(x, r):
        shift, scale = mod_ref[pl.ds(r, 1), 0:d], mod_ref[pl.ds(r, 1), d:2 * d]
        return (_rms(x, g_ref[0:1, :] * (1 + scale)) + shift).astype(BF16)

    @pl.when(i < n_prompt_tiles)
    def _():
        prompt_tile(pre_norm(x_refs[0][...], 0), w_vmem, outs[:3])

    @pl.when(i >= n_prompt_tiles)
    def _():
        r = 1 + (i - n_prompt_tiles) // tiles_per_request
        latent_tile(pre_norm(x_refs[-1][...], r), w_vmem, outs[3:], (cos_ref[...], sin_ref[...], _lo16_mask()))


def _qkv_call(kernel, name, x_parts, g, mods, w, tables, prompt_outs, latent_outs, *, layer, n_p, n_s):
    d = x_parts[0].shape[1]
    tm = QKV_ROW_TILE
    tp, ts = n_p // tm, n_s // tm
    nt = tables[0].shape[0] // tm

    def prompt_block(shape):
        return pl.BlockSpec(shape, lambda i: (jnp.minimum(i, tp - 1),) + (0,) * (len(shape) - 1))

    def latent_block(shape):
        return pl.BlockSpec(shape, lambda i: (jnp.maximum(i - tp, 0),) + (0,) * (len(shape) - 1))

    if len(x_parts) == 2:
        x_specs = [prompt_block((tm, d)), latent_block((tm, d))]
    else:
        x_specs = [pl.BlockSpec((tm, d), lambda i: (i, 0))]
    table_spec = pl.BlockSpec((tm, LANES), lambda i: (jnp.maximum(i - tp, 0) % nt, 0))
    in_specs = x_specs + [_layer_resident(g.shape, layer), _layer_resident(mods.shape, layer),
                          pl.BlockSpec(memory_space=pl.ANY), table_spec, table_spec]
    out_shape, out_specs = [], []
    for outs, n, block in ((prompt_outs, n_p, prompt_block), (latent_outs, n_s, latent_block)):
        for shape_of, dtype in outs:
            out_shape.append(jax.ShapeDtypeStruct(shape_of(n), dtype))
            out_specs.append(block(shape_of(tm)))
    cols = w.shape[2]
    scratch = [pltpu.VMEM((d, cols), BF16), pltpu.VMEM((QKV_STAGE_SLOTS, WEIGHT_STAGE_ROWS, cols), F32),
               pltpu.SemaphoreType.DMA((QKV_STAGE_SLOTS,))]
    return pl.pallas_call(
        kernel,
        out_shape=tuple(out_shape),
        grid=(tp + ts,),
        in_specs=in_specs,
        out_specs=tuple(out_specs),
        scratch_shapes=scratch,
        compiler_params=_params(1),
        name=name,
    )(*x_parts, g, mods, w, *tables)


def _diff_tile(h, w_ref, outs, rope=None):
    q_ref, k_ref, v_ref = outs
    d = h.shape[1]
    cw = 512
    for c in range(3 * d // cw):
        acc = jnp.dot(h, w_ref[:, c * cw:(c + 1) * cw], preferred_element_type=F32)
        which, off = divmod(c * cw, d)
        dst = outs[which]
        for s in range(cw // LANES):
            xs = acc[:, s * LANES:(s + 1) * LANES]
            if rope is not None and which < 2:
                xs = _rope_slab(xs, *rope)
            if which == 0:
                xs = xs * (DIFF_HD ** -0.5 * LOG2E)
            lo = off + s * LANES
            if rope is not None or which == 0:
                dst[:, lo:lo + LANES] = xs.astype(dst.dtype)
            else:
                dst[pl.ds(lo // LANES, h.shape[0], stride=DIFF_HEADS), :] = xs


def _qkv_diff_kernel(*refs, j, n_prompt_tiles, tiles_per_request, split_x):
    _qkv_both_groups(refs, n_prompt_tiles, tiles_per_request, split_x, j, _diff_tile, _diff_tile)


def _qkv_diff(x_parts, g, mods, w, tables, *, layer, j, n_p, n_s, tiles_per_request):
    d = x_parts[0].shape[1]

    def rows(n):
        return (n, d)

    def cache(n):
        return (n * DIFF_HEADS, d // DIFF_HEADS)

    kernel = functools.partial(_qkv_diff_kernel, j=j, n_prompt_tiles=n_p // QKV_ROW_TILE,
                               tiles_per_request=tiles_per_request, split_x=len(x_parts) == 2)
    return _qkv_call(kernel, "qkv_diff", x_parts, g, mods, w, tables,
                     [(rows, BF16), (cache, F32), (cache, F32)], [(rows, BF16)] * 3,
                     layer=layer, n_p=n_p, n_s=n_s)


def _swa_q_tile(h, w_ref, q_ref, rope):
    d = h.shape[1]
    cw = 512
    for c in range(d // cw):
        acc = jnp.dot(h, w_ref[:, c * cw:(c + 1) * cw], preferred_element_type=F32)
        for s in range(cw // LANES):
            xs = acc[:, s * LANES:(s + 1) * LANES]
            if rope is not None:
                xs = _rope_slab(xs, *rope)
            lo = c * cw + s * LANES
            q_ref[:, lo:lo + LANES] = (xs * (SWA_HD ** -0.5 * LOG2E)).astype(BF16)
    nkv = SWA_KV_HEADS * SWA_HD
    return jnp.dot(h, w_ref[:, d:d + 2 * nkv], preferred_element_type=F32), nkv


def _swa_prompt_tile(h, w_ref, outs, *, seq):
    q_ref, kt_ref, vt_ref = outs
    kv, nkv = _swa_q_tile(h, w_ref, q_ref, None)
    for b in range(h.shape[0] // seq):
        kt_ref[b] = kv[b * seq:(b + 1) * seq, :nkv].T
        vt_ref[b] = kv[b * seq:(b + 1) * seq, nkv:].T


def _swa_latent_tile(h, w_ref, outs, rope):
    q_ref, kd_ref, vd_ref = outs
    kv, nkv = _swa_q_tile(h, w_ref, q_ref, rope)
    lo64 = _lo64()
    for which, dst in enumerate((kd_ref, vd_ref)):
        for s in range(nkv // LANES):
            xs = kv[:, which * nkv + s * LANES: which * nkv + (s + 1) * LANES]
            if which == 0:
                xs = _rope_slab(xs, *rope)
            sw = pltpu.roll(xs, LANES // 2, 1)
            dst[:, (2 * s) * LANES:(2 * s + 1) * LANES] = jnp.where(lo64, xs, sw).astype(BF16)
            dst[:, (2 * s + 1) * LANES:(2 * s + 2) * LANES] = jnp.where(lo64, sw, xs).astype(BF16)


def _qkv_swa_kernel(*refs, j, n_prompt_tiles, tiles_per_request, split_x, seq):
    _qkv_both_groups(refs, n_prompt_tiles, tiles_per_request, split_x, j,
                     functools.partial(_swa_prompt_tile, seq=seq), _swa_latent_tile)


def _qkv_swa(x_parts, g, mods, w, tables, *, layer, j, n_p, n_s, tiles_per_request, seq):
    d = x_parts[0].shape[1]
    nkv = SWA_KV_HEADS * SWA_HD

    def rows(n):
        return (n, d)

    def feature_major(n):
        return (n // seq, nkv, seq)

    def duplicated(n):
        return (n, 2 * nkv)

    kernel = functools.partial(_qkv_swa_kernel, j=j, n_prompt_tiles=n_p // QKV_ROW_TILE,
                               tiles_per_request=tiles_per_request, split_x=len(x_parts) == 2, seq=seq)
    return _qkv_call(kernel, "qkv_swa", x_parts, g, mods, w, tables,
                     [(rows, BF16), (feature_major, F32), (feature_major, F32)],
                     [(rows, BF16), (duplicated, BF16), (duplicated, BF16)],
                     layer=layer, n_p=n_p, n_s=n_s)


def _diff_lambda(lam_ref, lam_init):
    lp = lam_ref[...]
    a = jnp.sum(lp[0:1] * lp[1:2], axis=-1, keepdims=True)
    b = jnp.sum(lp[2:3] * lp[3:4], axis=-1, keepdims=True)
    return jnp.exp(a) - jnp.exp(b) + lam_init


def _diff_combine(acc, tq, lam, g, lam_init):
    o12 = acc[:, :LANES] / acc[:, LANES:]
    o = o12[:tq] - lam * o12[tq:]
    return _rms(o, g) * (1.0 - lam_init)


def _stack_maps(q):
    m_lo, m_hi = _half_masks(BF16)
    return jnp.concatenate([q * m_lo, q * m_hi], axis=0)


def _run_pipelined(items, scores, finish, s_bufs):
    depth = len(s_bufs)
    states = {i: scores(items[i], s_bufs[i]) for i in range(min(depth - 1, len(items)))}
    for i, item in enumerate(items):
        ahead = i + depth - 1
        if ahead < len(items):
            states[ahead] = scores(items[ahead], s_bufs[ahead % depth])
        finish(item, s_bufs[i % depth], states.pop(i))


def _store_scores(s_ref, col0, s, mrun):
    s_ref[:, col0:col0 + s.shape[1]] = s
    for t in range(s.shape[1] // LANES):
        blk = s[:, t * LANES:(t + 1) * LANES]
        mrun = blk if mrun is None else jnp.maximum(mrun, blk)
    return mrun


def _exp_block(s_ref, col0, width, mb):
    return jnp.concatenate(
        [jnp.exp2(s_ref[:, col0 + t * LANES:col0 + (t + 1) * LANES] - mb).astype(BF16)
         for t in range(width // LANES)], axis=1)


def _diff_prompt_kernel(q_ref, k_ref, v_ref, lam_ref, g_ref, o_ref, *s_bufs, lam_init, seq):
    lam = _diff_lambda(lam_ref, lam_init)
    g = g_ref[...]
    ones = _ones_column(seq)
    items = [(r, h) for r in range(q_ref.shape[0] // seq) for h in range(DIFF_HEADS)]

    def head_rows(ref, r, h):
        return ref[pl.ds(r * seq * DIFF_HEADS + h, seq, stride=DIFF_HEADS), :]

    def scores(item, s_ref):
        r, h = item
        rows, sl = slice(r * seq, (r + 1) * seq), slice(h * LANES, (h + 1) * LANES)
        s = lax.dot_general(_stack_maps(q_ref[rows, sl]), head_rows(k_ref, r, h).astype(BF16), NT_DIMS,
                            preferred_element_type=F32)
        return _store_scores(s_ref, 0, s, None)

    def finish(item, s_ref, mrun):
        r, h = item
        rows, sl = slice(r * seq, (r + 1) * seq), slice(h * LANES, (h + 1) * LANES)
        mb = jnp.broadcast_to(jnp.max(mrun, axis=-1, keepdims=True), (2 * seq, LANES))
        vx = jnp.concatenate([head_rows(v_ref, r, h).astype(BF16), ones], axis=1)
        acc = jnp.dot(_exp_block(s_ref, 0, seq, mb), vx, preferred_element_type=F32)
        o_ref[rows, sl] = _diff_combine(acc, seq, lam, g, lam_init).astype(BF16)

    _run_pipelined(items, scores, finish, s_bufs)


def _diff_prompt_attention(q, k, v, lam_params, subln_g, *, j, seq, lam_init):
    n, d = q.shape
    req = 2
    spec = pl.BlockSpec((req * seq, d), lambda b: (b, 0))
    kv_spec = pl.BlockSpec((req * seq * DIFF_HEADS, d // DIFF_HEADS), lambda b: (b, 0))
    return pl.pallas_call(
        functools.partial(_diff_prompt_kernel, lam_init=lam_init, seq=seq),
        out_shape=jax.ShapeDtypeStruct((n, d), BF16),
        grid=(n // (req * seq),),
        in_specs=[spec, kv_spec, kv_spec, _layer_resident(lam_params.shape, j),
                  _layer_resident(subln_g.shape, j)],
        out_specs=spec,
        scratch_shapes=[pltpu.VMEM((2 * seq, seq), F32)] * PROMPT_SCORE_BUFFERS,
        compiler_params=_params(1),
        name="diff_attn_prompt",
    )(q, k, v, lam_params, subln_g)


def _diff_latent_kernel(q_ref, kc_ref, vc_ref, kl_ref, vl_ref, lam_ref, g_ref, o_ref,
                        kk_ref, vx_ref, *s_bufs, lam_init, lc, tq, key_chunk):
    seq = q_ref.shape[0]
    heads = kk_ref.shape[0]
    nk = kk_ref.shape[1]
    for hh in range(heads):
        h = pl.program_id(1) * heads + hh
        sl = slice(hh * LANES, (hh + 1) * LANES)
        kk_ref[hh, 0:lc, :] = kc_ref[pl.ds(h, lc, stride=DIFF_HEADS), :].astype(BF16)
        kk_ref[hh, lc:, :] = kl_ref[:, sl]
        vx_ref[hh, 0:lc, 0:LANES] = vc_ref[pl.ds(h, lc, stride=DIFF_HEADS), :].astype(BF16)
        vx_ref[hh, lc:, 0:LANES] = vl_ref[:, sl]
        vx_ref[hh, :, LANES:2 * LANES] = _ones_column(nk)

    masks = _half_masks(BF16)
    nchunk = nk // key_chunk
    items = [(hh, rt, m) for hh in range(heads) for rt in range(seq // tq) for m in range(2)]
    lam = _diff_lambda(lam_ref, lam_init)
    g = g_ref[...]
    first_map = {}

    def scores(item, s_ref):
        hh, rt, m = item
        q = q_ref[rt * tq:(rt + 1) * tq, hh * LANES:(hh + 1) * LANES] * masks[m]
        mrun = None
        for c in range(nchunk):
            s = lax.dot_general(q, kk_ref[hh, c * key_chunk:(c + 1) * key_chunk, :], NT_DIMS,
                                preferred_element_type=F32)
            mrun = _store_scores(s_ref, c * key_chunk, s, mrun)
        return mrun

    def finish(item, s_ref, mrun):
        hh, rt, m = item
        mb = jnp.broadcast_to(jnp.max(mrun, axis=-1, keepdims=True), (tq, LANES))
        acc = None
        for c in range(nchunk):
            part = jnp.dot(_exp_block(s_ref, c * key_chunk, key_chunk, mb),
                           vx_ref[hh, c * key_chunk:(c + 1) * key_chunk, :], preferred_element_type=F32)
            acc = part if acc is None else acc + part
        o_m = acc[:, :LANES] / acc[:, LANES:]
        if m == 0:
            first_map[hh, rt] = o_m
        else:
            o = _rms(first_map.pop((hh, rt)) - lam * o_m, g) * (1.0 - lam_init)
            o_ref[rt * tq:(rt + 1) * tq, hh * LANES:(hh + 1) * LANES] = o.astype(BF16)

    _run_pipelined(items, scores, finish, s_bufs)


def _diff_latent_attention(q, k, v, cache_k, cache_v, lam_params, subln_g, *, j, seq, lc, lam_init):
    n, d = q.shape
    nb = cache_k.shape[0]
    tq = 512
    key_chunk = 512
    heads = 1
    q_spec = pl.BlockSpec((seq, heads * LANES), lambda b, h: (b, h))
    c_spec = pl.BlockSpec((None, lc * DIFF_HEADS, LANES), lambda b, h: (b, j, 0))
    return pl.pallas_call(
        functools.partial(_diff_latent_kernel, lam_init=lam_init, lc=lc, tq=tq, key_chunk=key_chunk),
        out_shape=jax.ShapeDtypeStruct((n, d), BF16),
        grid=(nb, DIFF_HEADS // heads),
        in_specs=[q_spec, c_spec, c_spec, q_spec, q_spec,
                  _layer_resident(lam_params.shape, j), _layer_resident(subln_g.shape, j)],
        out_specs=q_spec,
        scratch_shapes=[pltpu.VMEM((heads, lc + seq, LANES), BF16),
                        pltpu.VMEM((heads, lc + seq, 2 * LANES), BF16),
                        *[pltpu.VMEM((tq, lc + seq), F32)] * LATENT_SCORE_BUFFERS],
        compiler_params=_params(2),
        name="diff_attn_latent",
    )(q, cache_k, cache_v, k, v, lam_params, subln_g)


def _stack_group(q_ref, rows, kv_local):
    m_lo, m_hi = _half_masks(BF16)
    parts = []
    for gb in range(SWA_GROUP // 2):
        blk = kv_local * (SWA_GROUP // 2) + gb
        qb = q_ref[rows, blk * LANES:(blk + 1) * LANES]
        parts += [qb * m_lo, qb * m_hi]
    return jnp.concatenate(parts, axis=0)


def _sink_column(sink_ref, first_head, tq):
    return jnp.concatenate([jnp.full((tq, LANES), sink_ref[first_head + g] * LOG2E, F32)
                            for g in range(SWA_GROUP)], axis=0)


def _write_group(o_ref, rows, kv_local, o, tq):
    lo64 = _lo64()
    for gb in range(SWA_GROUP // 2):
        blk = kv_local * (SWA_GROUP // 2) + gb
        even = o[(2 * gb) * tq:(2 * gb + 1) * tq]
        odd = o[(2 * gb + 1) * tq:(2 * gb + 2) * tq]
        o_ref[rows, blk * LANES:(blk + 1) * LANES] = jnp.where(lo64, even, odd).astype(BF16)


def _dup_rows(x_t):
    xb = x_t.astype(BF16)
    return jnp.concatenate([xb, xb], axis=0)


def _sink_finish(mrun, sk, rows):
    mb = jnp.maximum(jnp.broadcast_to(jnp.max(mrun, axis=-1, keepdims=True), (rows, LANES)), sk)
    return mb, jnp.exp2(sk - mb)


def _swa_prompt_kernel(sink_ref, q_ref, kt_ref, vt_ref, o_ref, *s_bufs):
    seq = kt_ref.shape[2]
    rows = SWA_GROUP * seq
    ones = _ones_row(seq)
    items = [(r, j) for r in range(kt_ref.shape[0]) for j in range(SWA_KV_HEADS)]

    def scores(item, s_ref):
        r, j = item
        kd = _dup_rows(kt_ref[r, j * SWA_HD:(j + 1) * SWA_HD, :])
        s = jnp.dot(_stack_group(q_ref, slice(r * seq, (r + 1) * seq), j), kd, preferred_element_type=F32)
        return _store_scores(s_ref, 0, s, None)

    def finish(item, s_ref, mrun):
        r, j = item
        sk = _sink_column(sink_ref, j * SWA_GROUP, seq)
        mb, sink_term = _sink_finish(mrun, sk, rows)
        vx = jnp.concatenate([_dup_rows(vt_ref[r, j * SWA_HD:(j + 1) * SWA_HD, :]), ones], axis=0)
        acc = lax.dot_general(_exp_block(s_ref, 0, seq, mb), vx, NT_DIMS, preferred_element_type=F32)
        o = acc[:, :LANES] / (acc[:, LANES:] + sink_term)
        _write_group(o_ref, slice(r * seq, (r + 1) * seq), j, o, seq)

    _run_pipelined(items, scores, finish, s_bufs)


def _swa_prompt_attention(q, kt, vt, sink, *, seq):
    n, d = q.shape
    nkv = kt.shape[1]
    req = 2
    t_spec = pl.BlockSpec((req, nkv, seq), lambda b: (b, 0, 0))
    return pl.pallas_call(
        _swa_prompt_kernel,
        out_shape=jax.ShapeDtypeStruct((n, d), BF16),
        grid=(n // (req * seq),),
        in_specs=[pl.BlockSpec(memory_space=pltpu.SMEM),
                  pl.BlockSpec((req * seq, d), lambda b: (b, 0)), t_spec, t_spec],
        out_specs=pl.BlockSpec((req * seq, d), lambda b: (b, 0)),
        scratch_shapes=[pltpu.VMEM((SWA_GROUP * seq, seq), F32)] * (PROMPT_SCORE_BUFFERS // 2),
        compiler_params=_params(1),
        name="swa_attn_prompt",
    )(sink, q, kt, vt)


def _swa_latent_kernel(sink_ref, q_ref, kc_ref, vc_ref, kl_ref, vl_ref, o_ref, kcd_ref, vcx_ref,
                       s0_ref, s1_ref, *, tq, span):
    pair = pl.program_id(1)
    tiles = q_ref.shape[0] // tq
    first_tile = pl.program_id(2) * tiles
    seq = kl_ref.shape[0]
    lc = kc_ref.shape[1]
    rows = SWA_GROUP * tq
    ones_row = _ones_row(lc)
    for jj in range(2):
        kcd_ref[jj] = _dup_rows(kc_ref[jj * SWA_HD:(jj + 1) * SWA_HD, :])
        vcx_ref[jj] = jnp.concatenate([_dup_rows(vc_ref[jj * SWA_HD:(jj + 1) * SWA_HD, :]), ones_row], axis=0)
    ones_col = _ones_column(span)
    items = [(t, jj) for t in range(tiles) for jj in range(2)]
    windows, biases = {}, {}

    def window(t):
        if t not in windows:
            q0 = (first_tile + t) * tq
            windows[t] = (q0, pl.multiple_of(jnp.clip(q0 - WINDOW, 0, seq - span), WINDOW))
        return windows[t]

    def bias_for(t):
        if t not in biases:
            q0, ws = window(t)
            q_pos = q0 + lax.broadcasted_iota(jnp.int32, (tq, span), 0)
            k_pos = ws + lax.broadcasted_iota(jnp.int32, (tq, span), 1)
            b = jnp.where(jnp.abs(q_pos - k_pos) <= WINDOW, 0.0, NEG_INF).astype(F32)
            biases[t] = jnp.concatenate([b] * SWA_GROUP, axis=0)
        return biases[t]

    def scores(item, s_ref):
        t, jj = item
        _, ws = window(t)
        qs = _stack_group(q_ref, slice(t * tq, (t + 1) * tq), jj)
        s_c = jnp.dot(qs, kcd_ref[jj], preferred_element_type=F32)
        mrun = _store_scores(s_ref, 0, s_c, None)
        s_w = lax.dot_general(qs, kl_ref[pl.ds(ws, span), jj * LANES:(jj + 1) * LANES], NT_DIMS,
                              preferred_element_type=F32) + bias_for(t)
        return _store_scores(s_ref, lc, s_w, mrun)

    def finish(item, s_ref, mrun):
        t, jj = item
        _, ws = window(t)
        sk = _sink_column(sink_ref, (2 * pair + jj) * SWA_GROUP, tq)
        mb, sink_term = _sink_finish(mrun, sk, rows)
        vwx = jnp.concatenate([vl_ref[pl.ds(ws, span), jj * LANES:(jj + 1) * LANES], ones_col], axis=1)
        acc = (lax.dot_general(_exp_block(s_ref, 0, lc, mb), vcx_ref[jj], NT_DIMS, preferred_element_type=F32)
               + jnp.dot(_exp_block(s_ref, lc, span, mb), vwx, preferred_element_type=F32))
        o = acc[:, :LANES] / (acc[:, LANES:] + sink_term)
        _write_group(o_ref, slice(t * tq, (t + 1) * tq), jj, o, tq)

    _run_pipelined(items, scores, finish, (s0_ref, s1_ref))


def _swa_latent_attention(q, kd, vd, cache_kt, cache_vt, sink, *, j, seq):
    n, d = q.shape
    nb, _, lc = cache_kt.shape
    tq = 256
    span = tq + 2 * WINDOW
    npair = SWA_KV_HEADS // 2
    wq = d // npair
    parts = 2
    q_spec = pl.BlockSpec((seq // parts, wq), lambda b, p, i: (b * parts + i, p))
    c_spec = pl.BlockSpec((None, 2 * SWA_HD, lc), lambda b, p, i: (b, j * npair + p, 0))
    l_spec = pl.BlockSpec((seq, 2 * LANES), lambda b, p, i: (b, p))
    s_shape = pltpu.VMEM((SWA_GROUP * tq, lc + span), F32)
    return pl.pallas_call(
        functools.partial(_swa_latent_kernel, tq=tq, span=span),
        out_shape=jax.ShapeDtypeStruct((n, d), BF16),
        grid=(nb, npair, parts),
        in_specs=[pl.BlockSpec(memory_space=pltpu.SMEM), q_spec, c_spec, c_spec, l_spec, l_spec],
        out_specs=q_spec,
        scratch_shapes=[pltpu.VMEM((2, 2 * SWA_HD, lc), BF16), pltpu.VMEM((2, 2 * LANES, lc), BF16),
                        s_shape, s_shape],
        compiler_params=_params(3),
        name="swa_attn_latent",
    )(sink, q, cache_kt, cache_vt, kd, vd)


def _load_weights_as_bf16(jobs, stages, sems):
    order = []
    rings = {w: [] for w in stages}
    for src, dst in jobs:
        w = src.shape[1]
        slots = stages[w].shape[0]
        for k in range(src.shape[0] // WEIGHT_STAGE_ROWS):
            rows = pl.ds(k * WEIGHT_STAGE_ROWS, WEIGHT_STAGE_ROWS)
            slot = len(rings[w]) % slots
            copy = pltpu.make_async_copy(src.at[rows, :], stages[w].at[slot], sems[w].at[slot])
            order.append((w, len(rings[w])))
            rings[w].append((copy, slot, dst, rows))
    for w, ring in rings.items():
        for copy, _, _, _ in ring[:stages[w].shape[0]]:
            copy.start()
    for w, k in order:
        copy, slot, dst, rows = rings[w][k]
        copy.wait()
        dst[rows, :] = stages[w][slot].astype(BF16)
        ahead = k + stages[w].shape[0]
        if ahead < len(rings[w]):
            rings[w][ahead][0].start()


def _weight_stream(pieces, rings):
    per_ring = {name: [] for name in rings}
    plan = []
    for src, dst, idx, name in pieces:
        stage, sem = rings[name]
        slot = len(per_ring[name]) % stage.shape[0]
        view = stage.at[slot].at[0:src.shape[0], 0:src.shape[1]]
        plan.append((name, len(per_ring[name])))
        per_ring[name].append((pltpu.make_async_copy(src, view, sem.at[slot]), view, dst, idx))
    cursor = [0]

    def prime():
        for name, ring in per_ring.items():
            for copy, _, _, _ in ring[:rings[name][0].shape[0]]:
                copy.start()

    def take(n):
        for name, k in plan[cursor[0]:cursor[0] + n]:
            copy, view, dst, idx = per_ring[name][k]
            copy.wait()
            dst[idx] = view[...].astype(BF16)
            ahead = k + rings[name][0].shape[0]
            if ahead < len(per_ring[name]):
                per_ring[name][ahead][0].start()
        cursor[0] += n

    return prime, take


def _post_attn_ffn_kernel(*refs, layer, j, n_prompt_tiles, tiles_per_request, split_x, split_out):
    refs = list(refs)
    op_ref, os_ref = refs[:2]
    x_refs = refs[2:4] if split_x else refs[2:3]
    wo_hbm, wg_hbm, wu_hbm, wd_hbm, g_ref, mod_ref = refs[2 + len(x_refs):8 + len(x_refs)]
    n_out = 2 if split_out else 1
    out_refs = refs[8 + len(x_refs):8 + len(x_refs) + n_out]
    wo_ref, wg_ref, wu_ref, wd_ref, stage_row, stage_col, sem_row, sem_col = refs[8 + len(x_refs) + n_out:]
    d = wo_ref.shape[1]
    dff = wg_ref.shape[1]
    cw = FFN_CHUNK
    i = pl.program_id(0)
    is_prompt = i < n_prompt_tiles
    r = jnp.where(is_prompt, 0, 1 + (i - n_prompt_tiles) // tiles_per_request)

    def mod(slot):
        return mod_ref[pl.ds(r, 1), slot * d:(slot + 1) * d]

    def tile(before_out_proj, before_chunk):
        o = jnp.where(is_prompt, op_ref[...], os_ref[...])
        x = jnp.where(is_prompt, x_refs[0][...], x_refs[1][...]) if split_x else x_refs[0][...]
        before_out_proj()
        y = jnp.dot(o, wo_ref[...], preferred_element_type=F32)
        x = x + _rms(y, mod(2) * g_ref[1:2, :])
        h = (_rms(x, g_ref[2:3, :] * (1 + mod(4))) + mod(3)).astype(BF16)
        y = jnp.zeros((h.shape[0], d), F32)
        for c in range(dff // cw):
            before_chunk(c)
            cols = slice(c * cw, (c + 1) * cw)
            a = jnp.dot(h, wg_ref[:, cols], preferred_element_type=F32)
            u = jnp.dot(h, wu_ref[:, cols], preferred_element_type=F32)
            t = (a * jax.nn.sigmoid(a)) * u
            y = y + jnp.dot(t.astype(BF16), wd_ref[cols, :], preferred_element_type=F32)
        out = x + _rms(y, mod(5) * g_ref[3:4, :])
        if split_out:
            @pl.when(is_prompt)
            def _():
                out_refs[0][...] = out

            @pl.when(jnp.logical_not(is_prompt))
            def _():
                out_refs[1][...] = out
        else:
            out_refs[0][...] = out

    @pl.when(i == 0)
    def _():
        row_chunk = stage_row.shape[1]
        pieces = [(wo_hbm.at[j].at[pl.ds(k * row_chunk, row_chunk), :], wo_ref,
                   (pl.ds(k * row_chunk, row_chunk), slice(None)), "row") for k in range(d // row_chunk)]
        col_chunk = stage_col.shape[2]
        per_take = [d // row_chunk]
        for c in range(dff // cw):
            n = 0
            if (c * cw) % col_chunk == 0:
                cols = pl.ds(c * cw, min(col_chunk, dff - c * cw))
                pieces += [(wg_hbm.at[layer].at[:, cols], wg_ref, (slice(None), cols), "col"),
                           (wu_hbm.at[layer].at[:, cols], wu_ref, (slice(None), cols), "col")]
                n += 2
            rows = pl.ds(c * cw, cw)
            pieces.append((wd_hbm.at[layer].at[rows, :], wd_ref, (rows, slice(None)), "row"))
            per_take.append(n + 1)
        prime, take = _weight_stream(pieces, {"row": (stage_row, sem_row), "col": (stage_col, sem_col)})
        prime()
        tile(lambda: take(per_take[0]), lambda c: take(per_take[c + 1]))

    @pl.when(i > 0)
    def _():
        tile(lambda: None, lambda c: None)


def _post_attn_ffn(o_p, o_s, xs_in, w_o, wg, wu, wd, g, mods, *, layer, j, tiles_per_request, split_out):
    n_p, d = o_p.shape
    n_s = o_s.shape[0]
    tm = ROW_TILE
    tp, ts = n_p // tm, n_s // tm
    prompt_rows = pl.BlockSpec((tm, d), lambda i: (jnp.minimum(i, tp - 1), 0))
    latent_rows = pl.BlockSpec((tm, d), lambda i: (jnp.maximum(i - tp, 0), 0))
    all_rows = pl.BlockSpec((tm, d), lambda i: (i, 0))
    split_x = len(xs_in) == 2
    in_specs = [prompt_rows, latent_rows] + ([prompt_rows, latent_rows] if split_x else [all_rows])
    hbm = pl.BlockSpec(memory_space=pl.ANY)
    in_specs += [hbm, hbm, hbm, hbm, _layer_resident(g.shape, layer), _layer_resident(mods.shape, layer)]
    if split_out:
        out_shape = (jax.ShapeDtypeStruct((n_p, d), F32), jax.ShapeDtypeStruct((n_s, d), F32))
        out_specs = (prompt_rows, latent_rows)
    else:
        out_shape = jax.ShapeDtypeStruct((n_p + n_s, d), F32)
        out_specs = all_rows
    dff = wg.shape[2]
    scratch = [pltpu.VMEM((d, d), BF16), pltpu.VMEM((d, dff), BF16), pltpu.VMEM((d, dff), BF16),
               pltpu.VMEM((dff, d), BF16),
               pltpu.VMEM((WEIGHT_STAGE_SLOTS, FFN_CHUNK, d), F32),
               pltpu.VMEM((WEIGHT_COLUMN_SLOTS, d, 2 * FFN_CHUNK), F32),
               pltpu.SemaphoreType.DMA((WEIGHT_STAGE_SLOTS,)), pltpu.SemaphoreType.DMA((WEIGHT_COLUMN_SLOTS,))]
    return pl.pallas_call(
        functools.partial(_post_attn_ffn_kernel, layer=layer, j=j, n_prompt_tiles=tp,
                          tiles_per_request=tiles_per_request, split_x=split_x, split_out=split_out),
        out_shape=out_shape,
        grid=(tp + ts,),
        in_specs=in_specs,
        out_specs=out_specs,
        scratch_shapes=scratch,
        compiler_params=pltpu.CompilerParams(dimension_semantics=("arbitrary",),
                                             vmem_limit_bytes=FFN_VMEM_LIMIT),
        name="post_attn_ffn",
    )(o_p, o_s, *xs_in, w_o, wg, wu, wd, g, mods)


def _rope_tables(n_lat):
    t = np.arange(n_lat)
    row = (t // GRID_W).astype(np.float32)
    col = (t % GRID_W).astype(np.float32)
    nf = ROT_DIM // 4
    inv = np.float32(ROPE_BASE) ** (-np.arange(nf, dtype=np.float32) / np.float32(nf))
    ar = row[:, None] * inv[None, :]
    ac = col[:, None] * inv[None, :]
    ang = np.concatenate([ar, ar, ac, ac], axis=-1)
    cos, sin = np.cos(ang), np.sin(ang)
    sign = np.where((np.arange(ROT_DIM) % 32) < 16, -1.0, 1.0).astype(np.float32)
    reps = LANES // ROT_DIM
    return jnp.asarray(np.tile(cos, (1, reps))), jnp.asarray(np.tile(sin * sign, (1, reps)))


def _swa_cache_to_feature_major(cache):
    nb, nl, lc, nh, hd = cache.shape
    return cache.transpose(0, 1, 3, 4, 2).reshape(nb, nl * nh * hd, lc)


def _swa_cache_from_feature_major(xt, seq):
    nb = xt.shape[0]
    return xt.reshape(nb, SWA_KV_HEADS, SWA_HD, seq).transpose(0, 3, 1, 2)


def kernel(x_prompt, x_sample, cache_diff_k, cache_diff_v, cache_swa_k, cache_swa_v, c, c_ctx,
           w_mod, b_mod, norm_g, w_qkv_diff, diff_lambda, diff_subln_g, w_o_diff,
           w_qkv_swa, swa_sink, w_o_swa, w_gate, w_up, w_down):
    bp, lp, d = x_prompt.shape
    bs, ls, _ = x_sample.shape
    lc = cache_diff_k.shape[2]
    depth = w_mod.shape[0]
    tm = ROW_TILE

    cond8 = jnp.concatenate([c_ctx[None, :], c, jnp.zeros((8 - 1 - bs, d), F32)], axis=0)
    mods = _modulation(cond8, w_mod, b_mod)
    tables = _rope_tables(ls)

    cdk = cache_diff_k.reshape(bs, -1, 2 * DIFF_HD)
    cdv = cache_diff_v.reshape(bs, -1, 2 * DIFF_HD)
    cskt = _swa_cache_to_feature_major(cache_swa_k)
    csvt = _swa_cache_to_feature_major(cache_swa_v)

    n_p, n_s = bp * lp, bs * ls
    x_parts = (x_prompt.reshape(n_p, d), x_sample.reshape(n_s, d))
    g = norm_g
    sub_g = diff_subln_g.reshape(-1, 1, 2 * DIFF_HD)
    diff_k_out, diff_v_out, swa_k_out, swa_v_out = [], [], [], []

    for i in range(depth):
        j = i // N_MIXERS
        if i % N_MIXERS == 0:
            lam_init = 0.8 - 0.6 * math.exp(-0.3 * i)
            qp, kp, vp, qs, ks, vs = _qkv_diff(x_parts, g, mods, w_qkv_diff, tables, layer=i, j=j,
                                               n_p=n_p, n_s=n_s, tiles_per_request=ls // QKV_ROW_TILE)
            op = _diff_prompt_attention(qp, kp, vp, diff_lambda, sub_g, j=j, seq=lp, lam_init=lam_init)
            diff_k_out.append(kp.reshape(bp, lp, DIFF_HEADS, 2 * DIFF_HD))
            diff_v_out.append(vp.reshape(bp, lp, DIFF_HEADS, 2 * DIFF_HD))
            os_ = _diff_latent_attention(qs, ks, vs, cdk, cdv, diff_lambda, sub_g,
                                         j=j, seq=ls, lc=lc, lam_init=lam_init)
            w_o = w_o_diff
        else:
            qp, ktp, vtp, qs, kds, vds = _qkv_swa(x_parts, g, mods, w_qkv_swa, tables, layer=i, j=j,
                                                  n_p=n_p, n_s=n_s, tiles_per_request=ls // QKV_ROW_TILE,
                                                  seq=lp)
            op = _swa_prompt_attention(qp, ktp, vtp, swa_sink[j], seq=lp)
            swa_k_out.append(_swa_cache_from_feature_major(ktp, lp))
            swa_v_out.append(_swa_cache_from_feature_major(vtp, lp))
            os_ = _swa_latent_attention(qs, kds, vds, cskt, csvt, swa_sink[j], j=j, seq=ls)
            w_o = w_o_swa
        last = i == depth - 1
        out = _post_attn_ffn(op, os_, x_parts, w_o, w_gate, w_up, w_down, g, mods, layer=i, j=j,
                             tiles_per_request=ls // tm, split_out=last)
        x_parts = out if last else (out,)
    xp, xs = x_parts

    return (xp.reshape(bp, lp, d), xs.reshape(bs, ls, d),
            jnp.stack(diff_k_out, axis=1), jnp.stack(diff_v_out, axis=1),
            jnp.stack(swa_k_out, axis=1), jnp.stack(swa_v_out, axis=1))
```

```python
import functools
import math

import jax
import jax.numpy as jnp
import numpy as np
from jax import lax
from jax.experimental import pallas as pl
from jax.experimental.pallas import tpu as pltpu

F32 = jnp.float32
BF16 = jnp.bfloat16

GRID_W = 64
N_MIXERS = 2
DIFF_HEADS = 8
DIFF_HD = 64
SWA_HEADS = 16
SWA_KV_HEADS = 4
SWA_GROUP = SWA_HEADS // SWA_KV_HEADS
SWA_HD = 64
ROT_DIM = 64
WINDOW = 128
ROPE_BASE = 10000.0
EPS = 1e-6
NEG_INF = -1e30

LANES = 128
SUBLANES = 8
ROW_TILE = 512
QKV_ROW_TILE = 512
PROMPT_SCORE_BUFFERS = 8
LATENT_SCORE_BUFFERS = 2
VMEM_LIMIT = 48 * 1024 * 1024
FFN_VMEM_LIMIT = 58 * 1024 * 1024
WEIGHT_STAGE_ROWS = 128
WEIGHT_STAGE_SLOTS = 3
WEIGHT_COLUMN_SLOTS = 2
QKV_STAGE_SLOTS = 4
FFN_CHUNK = 256
NT_DIMS = (((1,), (1,)), ((), ()))
LOG2E = math.log2(math.e)


def _params(n_axes):
    return pltpu.CompilerParams(dimension_semantics=("arbitrary",) * n_axes,
                                vmem_limit_bytes=VMEM_LIMIT)


def _resident(shape):
    return pl.BlockSpec(shape, lambda *_: (0,) * len(shape), pipeline_mode=pl.Buffered(1))


def _layer_resident(shape, layer):
    return pl.BlockSpec((None,) + tuple(shape[1:]), lambda *_: (layer,) + (0,) * (len(shape) - 1),
                        pipeline_mode=pl.Buffered(1))


def _rms(x, g):
    ms = jnp.mean(x * x, axis=-1, keepdims=True)
    return (x * lax.rsqrt(ms + EPS)) * g


def _half_masks(dtype):
    lane = lax.broadcasted_iota(jnp.int32, (1, LANES), 1)
    lo = lane < (LANES // 2)
    return jnp.where(lo, 1.0, 0.0).astype(dtype), jnp.where(lo, 0.0, 1.0).astype(dtype)


def _lo64():
    return lax.broadcasted_iota(jnp.int32, (1, LANES), 1) < (LANES // 2)


def _ones_column(rows):
    return jnp.ones((rows, LANES), BF16)


def _ones_row(cols):
    return jnp.ones((LANES, cols), BF16)


def _mod_kernel(cond_ref, w_ref, b_ref, out_ref):
    c = cond_ref[...]
    s = c * jax.nn.sigmoid(c)
    out_ref[...] = jnp.dot(s.astype(BF16), w_ref[...].astype(BF16),
                           preferred_element_type=F32) + b_ref[pl.ds(pl.program_id(0), 1), :]


def _modulation(cond8, w_mod, b_mod):
    depth, d, n = w_mod.shape
    tn = 1536
    return pl.pallas_call(
        _mod_kernel,
        out_shape=jax.ShapeDtypeStruct((depth, 8, n), F32),
        grid=(depth, n // tn),
        in_specs=[pl.BlockSpec((8, d), lambda i, j: (0, 0)),
                  pl.BlockSpec((None, d, tn), lambda i, j: (i, 0, j)),
                  pl.BlockSpec((depth, tn), lambda i, j: (0, j))],
        out_specs=pl.BlockSpec((None, 8, tn), lambda i, j: (i, 0, j)),
        compiler_params=_params(2),
        name="modulation",
    )(cond8, w_mod, b_mod)


def _rope_slab(xs, cos, sin_signed, lo16):
    left = pltpu.roll(xs, LANES - 16, 1)
    right = pltpu.roll(xs, 16, 1)
    return xs * cos + jnp.where(lo16, left, right) * sin_signed


def _lo16_mask():
    lane = lax.broadcasted_iota(jnp.int32, (1, LANES), 1)
    return (lane % 32) < 16


def _qkv_both_groups(refs, n_prompt_tiles, tiles_per_request, split_x, j, prompt_tile, latent_tile):
    refs = list(refs)
    x_refs = refs[:2] if split_x else refs[:1]
    g_ref, mod_ref, w_hbm, cos_ref, sin_ref = refs[len(x_refs):len(x_refs) + 5]
    outs = refs[len(x_refs) + 5:len(x_refs) + 11]
    w_vmem, stage, sem = refs[len(x_refs) + 11:]
    d = x_refs[0].shape[1]
    i = pl.program_id(0)

    @pl.when(i == 0)
    def _():
        _load_weights_as_bf16([(w_hbm.at[j], w_vmem)], {w_vmem.shape[1]: stage}, {w_vmem.shape[1]: sem})

    def pre_norm(x, r):
        shift, scale = mod_ref[pl.ds(r, 1), 0:d], mod_ref[pl.ds(r, 1), d:2 * d]
        return (_rms(x, g_ref[0:1, :] * (1 + scale)) + shift).astype(BF16)

    @pl.when(i < n_prompt_tiles)
    def _():
        prompt_tile(pre_norm(x_refs[0][...], 0), w_vmem, outs[:3])

    @pl.when(i >= n_prompt_tiles)
    def _():
        r = 1 + (i - n_prompt_tiles) // tiles_per_request
        latent_tile(pre_norm(x_refs[-1][...], r), w_vmem, outs[3:], (cos_ref[...], sin_ref[...], _lo16_mask()))


def _qkv_call(kernel, name, x_parts, g, mods, w, tables, prompt_outs, latent_outs, *, layer, n_p, n_s):
    d = x_parts[0].shape[1]
    tm = QKV_ROW_TILE
    tp, ts = n_p // tm, n_s // tm
    nt = tables[0].shape[0] // tm

    def prompt_block(shape):
        return pl.BlockSpec(shape, lambda i: (jnp.minimum(i, tp - 1),) + (0,) * (len(shape) - 1))

    def latent_block(shape):
        return pl.BlockSpec(shape, lambda i: (jnp.maximum(i - tp, 0),) + (0,) * (len(shape) - 1))

    if len(x_parts) == 2:
        x_specs = [prompt_block((tm, d)), latent_block((tm, d))]
    else:
        x_specs = [pl.BlockSpec((tm, d), lambda i: (i, 0))]
    table_spec = pl.BlockSpec((tm, LANES), lambda i: (jnp.maximum(i - tp, 0) % nt, 0))
    in_specs = x_specs + [_layer_resident(g.shape, layer), _layer_resident(mods.shape, layer),
                          pl.BlockSpec(memory_space=pl.ANY), table_spec, table_spec]
    out_shape, out_specs = [], []
    for outs, n, block in ((prompt_outs, n_p, prompt_block), (latent_outs, n_s, latent_block)):
        for shape_of, dtype in outs:
            out_shape.append(jax.ShapeDtypeStruct(shape_of(n), dtype))
            out_specs.append(block(shape_of(tm)))
    cols = w.shape[2]
    scratch = [pltpu.VMEM((d, cols), BF16), pltpu.VMEM((QKV_STAGE_SLOTS, WEIGHT_STAGE_ROWS, cols), F32),
               pltpu.SemaphoreType.DMA((QKV_STAGE_SLOTS,))]
    return pl.pallas_call(
        kernel,
        out_shape=tuple(out_shape),
        grid=(tp + ts,),
        in_specs=in_specs,
        out_specs=tuple(out_specs),
        scratch_shapes=scratch,
        compiler_params=_params(1),
        name=name,
    )(*x_parts, g, mods, w, *tables)


def _diff_tile(h, w_ref, outs, rope=None):
    q_ref, k_ref, v_ref = outs
    d = h.shape[1]
    cw = 512
    for c in range(3 * d // cw):
        acc = jnp.dot(h, w_ref[:, c * cw:(c + 1) * cw], preferred_element_type=F32)
        which, off = divmod(c * cw, d)
        dst = outs[which]
        for s in range(cw // LANES):
            xs = acc[:, s * LANES:(s + 1) * LANES]
            if rope is not None and which < 2:
                xs = _rope_slab(xs, *rope)
            if which == 0:
                xs = xs * (DIFF_HD ** -0.5 * LOG2E)
            lo = off + s * LANES
            if rope is not None or which == 0:
                dst[:, lo:lo + LANES] = xs.astype(dst.dtype)
            else:
                dst[pl.ds(lo // LANES, h.shape[0], stride=DIFF_HEADS), :] = xs


def _qkv_diff_kernel(*refs, j, n_prompt_tiles, tiles_per_request, split_x):
    _qkv_both_groups(refs, n_prompt_tiles, tiles_per_request, split_x, j, _diff_tile, _diff_tile)


def _qkv_diff(x_parts, g, mods, w, tables, *, layer, j, n_p, n_s, tiles_per_request):
    d = x_parts[0].shape[1]

    def rows(n):
        return (n, d)

    def cache(n):
        return (n * DIFF_HEADS, d // DIFF_HEADS)

    kernel = functools.partial(_qkv_diff_kernel, j=j, n_prompt_tiles=n_p // QKV_ROW_TILE,
                               tiles_per_request=tiles_per_request, split_x=len(x_parts) == 2)
    return _qkv_call(kernel, "qkv_diff", x_parts, g, mods, w, tables,
                     [(rows, BF16), (cache, F32), (cache, F32)], [(rows, BF16)] * 3,
                     layer=layer, n_p=n_p, n_s=n_s)


def _swa_q_tile(h, w_ref, q_ref, rope):
    d = h.shape[1]
    cw = 512
    for c in range(d // cw):
        acc = jnp.dot(h, w_ref[:, c * cw:(c + 1) * cw], preferred_element_type=F32)
        for s in range(cw // LANES):
            xs = acc[:, s * LANES:(s + 1) * LANES]
            if rope is not None:
                xs = _rope_slab(xs, *rope)
            lo = c * cw + s * LANES
            q_ref[:, lo:lo + LANES] = (xs * (SWA_HD ** -0.5 * LOG2E)).astype(BF16)
    nkv = SWA_KV_HEADS * SWA_HD
    return jnp.dot(h, w_ref[:, d:d + 2 * nkv], preferred_element_type=F32), nkv


def _swa_prompt_tile(h, w_ref, outs, *, seq):
    q_ref, kt_ref, vt_ref = outs
    kv, nkv = _swa_q_tile(h, w_ref, q_ref, None)
    for b in range(h.shape[0] // seq):
        kt_ref[b] = kv[b * seq:(b + 1) * seq, :nkv].T
        vt_ref[b] = kv[b * seq:(b + 1) * seq, nkv:].T


def _swa_latent_tile(h, w_ref, outs, rope):
    q_ref, kd_ref, vd_ref = outs
    kv, nkv = _swa_q_tile(h, w_ref, q_ref, rope)
    lo64 = _lo64()
    for which, dst in enumerate((kd_ref, vd_ref)):
        for s in range(nkv // LANES):
            xs = kv[:, which * nkv + s * LANES: which * nkv + (s + 1) * LANES]
            if which == 0:
                xs = _rope_slab(xs, *rope)
            sw = pltpu.roll(xs, LANES // 2, 1)
            dst[:, (2 * s) * LANES:(2 * s + 1) * LANES] = jnp.where(lo64, xs, sw).astype(BF16)
            dst[:, (2 * s + 1) * LANES:(2 * s + 2) * LANES] = jnp.where(lo64, sw, xs).astype(BF16)


def _qkv_swa_kernel(*refs, j, n_prompt_tiles, tiles_per_request, split_x, seq):
    _qkv_both_groups(refs, n_prompt_tiles, tiles_per_request, split_x, j,
                     functools.partial(_swa_prompt_tile, seq=seq), _swa_latent_tile)


def _qkv_swa(x_parts, g, mods, w, tables, *, layer, j, n_p, n_s, tiles_per_request, seq):
    d = x_parts[0].shape[1]
    nkv = SWA_KV_HEADS * SWA_HD

    def rows(n):
        return (n, d)

    def feature_major(n):
        return (n // seq, nkv, seq)

    def duplicated(n):
        return (n, 2 * nkv)

    kernel = functools.partial(_qkv_swa_kernel, j=j, n_prompt_tiles=n_p // QKV_ROW_TILE,
                               tiles_per_request=tiles_per_request, split_x=len(x_parts) == 2, seq=seq)
    return _qkv_call(kernel, "qkv_swa", x_parts, g, mods, w, tables,
                     [(rows, BF16), (feature_major, F32), (feature_major, F32)],
                     [(rows, BF16), (duplicated, BF16), (duplicated, BF16)],
                     layer=layer, n_p=n_p, n_s=n_s)


def _diff_lambda(lam_ref, lam_init):
    lp = lam_ref[...]
    a = jnp.sum(lp[0:1] * lp[1:2], axis=-1, keepdims=True)
    b = jnp.sum(lp[2:3] * lp[3:4], axis=-1, keepdims=True)
    return jnp.exp(a) - jnp.exp(b) + lam_init


def _diff_combine(acc, tq, lam, g, lam_init):
    o12 = acc[:, :LANES] / acc[:, LANES:]
    o = o12[:tq] - lam * o12[tq:]
    return _rms(o, g) * (1.0 - lam_init)


def _stack_maps(q):
    m_lo, m_hi = _half_masks(BF16)
    return jnp.concatenate([q * m_lo, q * m_hi], axis=0)


def _run_pipelined(items, scores, finish, s_bufs):
    depth = len(s_bufs)
    states = {i: scores(items[i], s_bufs[i]) for i in range(min(depth - 1, len(items)))}
    for i, item in enumerate(items):
        ahead = i + depth - 1
        if ahead < len(items):
            states[ahead] = scores(items[ahead], s_bufs[ahead % depth])
        finish(item, s_bufs[i % depth], states.pop(i))


def _store_scores(s_ref, col0, s, mrun):
    s_ref[:, col0:col0 + s.shape[1]] = s
    for t in range(s.shape[1] // LANES):
        blk = s[:, t * LANES:(t + 1) * LANES]
        mrun = blk if mrun is None else jnp.maximum(mrun, blk)
    return mrun


def _exp_block(s_ref, col0, width, mb):
    return jnp.concatenate(
        [jnp.exp2(s_ref[:, col0 + t * LANES:col0 + (t + 1) * LANES] - mb).astype(BF16)
         for t in range(width // LANES)], axis=1)


def _diff_prompt_kernel(q_ref, k_ref, v_ref, lam_ref, g_ref, o_ref, *s_bufs, lam_init, seq):
    lam = _diff_lambda(lam_ref, lam_init)
    g = g_ref[...]
    ones = _ones_column(seq)
    items = [(r, h) for r in range(q_ref.shape[0] // seq) for h in range(DIFF_HEADS)]

    def head_rows(ref, r, h):
        return ref[pl.ds(r * seq * DIFF_HEADS + h, seq, stride=DIFF_HEADS), :]

    def scores(item, s_ref):
        r, h = item
        rows, sl = slice(r * seq, (r + 1) * seq), slice(h * LANES, (h + 1) * LANES)
        s = lax.dot_general(_stack_maps(q_ref[rows, sl]), head_rows(k_ref, r, h).astype(BF16), NT_DIMS,
                            preferred_element_type=F32)
        return _store_scores(s_ref, 0, s, None)

    def finish(item, s_ref, mrun):
        r, h = item
        rows, sl = slice(r * seq, (r + 1) * seq), slice(h * LANES, (h + 1) * LANES)
        mb = jnp.broadcast_to(jnp.max(mrun, axis=-1, keepdims=True), (2 * seq, LANES))
        vx = jnp.concatenate([head_rows(v_ref, r, h).astype(BF16), ones], axis=1)
        acc = jnp.dot(_exp_block(s_ref, 0, seq, mb), vx, preferred_element_type=F32)
        o_ref[rows, sl] = _diff_combine(acc, seq, lam, g, lam_init).astype(BF16)

    _run_pipelined(items, scores, finish, s_bufs)


def _diff_prompt_attention(q, k, v, lam_params, subln_g, *, j, seq, lam_init):
    n, d = q.shape
    req = 2
    spec = pl.BlockSpec((req * seq, d), lambda b: (b, 0))
    kv_spec = pl.BlockSpec((req * seq * DIFF_HEADS, d // DIFF_HEADS), lambda b: (b, 0))
    return pl.pallas_call(
        functools.partial(_diff_prompt_kernel, lam_init=lam_init, seq=seq),
        out_shape=jax.ShapeDtypeStruct((n, d), BF16),
        grid=(n // (req * seq),),
        in_specs=[spec, kv_spec, kv_spec, _layer_resident(lam_params.shape, j),
                  _layer_resident(subln_g.shape, j)],
        out_specs=spec,
        scratch_shapes=[pltpu.VMEM((2 * seq, seq), F32)] * PROMPT_SCORE_BUFFERS,
        compiler_params=_params(1),
        name="diff_attn_prompt",
    )(q, k, v, lam_params, subln_g)


def _diff_latent_kernel(q_ref, kc_ref, vc_ref, kl_ref, vl_ref, lam_ref, g_ref, o_ref,
                        kk_ref, vx_ref, *s_bufs, lam_init, lc, tq, key_chunk):
    seq = q_ref.shape[0]
    heads = kk_ref.shape[0]
    nk = kk_ref.shape[1]
    for hh in range(heads):
        h = pl.program_id(1) * heads + hh
        sl = slice(hh * LANES, (hh + 1) * LANES)
        kk_ref[hh, 0:lc, :] = kc_ref[pl.ds(h, lc, stride=DIFF_HEADS), :].astype(BF16)
        kk_ref[hh, lc:, :] = kl_ref[:, sl]
        vx_ref[hh, 0:lc, 0:LANES] = vc_ref[pl.ds(h, lc, stride=DIFF_HEADS), :].astype(BF16)
        vx_ref[hh, lc:, 0:LANES] = vl_ref[:, sl]
        vx_ref[hh, :, LANES:2 * LANES] = _ones_column(nk)

    masks = _half_masks(BF16)
    nchunk = nk // key_chunk
    items = [(hh, rt, m) for hh in range(heads) for rt in range(seq // tq) for m in range(2)]
    lam = _diff_lambda(lam_ref, lam_init)
    g = g_ref[...]
    first_map = {}

    def scores(item, s_ref):
        hh, rt, m = item
        q = q_ref[rt * tq:(rt + 1) * tq, hh * LANES:(hh + 1) * LANES] * masks[m]
        mrun = None
        for c in range(nchunk):
            s = lax.dot_general(q, kk_ref[hh, c * key_chunk:(c + 1) * key_chunk, :], NT_DIMS,
                                preferred_element_type=F32)
            mrun = _store_scores(s_ref, c * key_chunk, s, mrun)
        return mrun

    def finish(item, s_ref, mrun):
        hh, rt, m = item
        mb = jnp.broadcast_to(jnp.max(mrun, axis=-1, keepdims=True), (tq, LANES))
        acc = None
        for c in range(nchunk):
            part = jnp.dot(_exp_block(s_ref, c * key_chunk, key_chunk, mb),
                           vx_ref[hh, c * key_chunk:(c + 1) * key_chunk, :], preferred_element_type=F32)
            acc = part if acc is None else acc + part
        o_m = acc[:, :LANES] / acc[:, LANES:]
        if m == 0:
            first_map[hh, rt] = o_m
        else:
            o = _rms(first_map.pop((hh, rt)) - lam * o_m, g) * (1.0 - lam_init)
            o_ref[rt * tq:(rt + 1) * tq, hh * LANES:(hh + 1) * LANES] = o.astype(BF16)

    _run_pipelined(items, scores, finish, s_bufs)


def _diff_latent_attention(q, k, v, cache_k, cache_v, lam_params, subln_g, *, j, seq, lc, lam_init):
    n, d = q.shape
    nb = cache_k.shape[0]
    tq = 512
    key_chunk = 512
    heads = 1
    q_spec = pl.BlockSpec((seq, heads * LANES), lambda b, h: (b, h))
    c_spec = pl.BlockSpec((None, lc * DIFF_HEADS, LANES), lambda b, h: (b, j, 0))
    return pl.pallas_call(
        functools.partial(_diff_latent_kernel, lam_init=lam_init, lc=lc, tq=tq, key_chunk=key_chunk),
        out_shape=jax.ShapeDtypeStruct((n, d), BF16),
        grid=(nb, DIFF_HEADS // heads),
        in_specs=[q_spec, c_spec, c_spec, q_spec, q_spec,
                  _layer_resident(lam_params.shape, j), _layer_resident(subln_g.shape, j)],
        out_specs=q_spec,
        scratch_shapes=[pltpu.VMEM((heads, lc + seq, LANES), BF16),
                        pltpu.VMEM((heads, lc + seq, 2 * LANES), BF16),
                        *[pltpu.VMEM((tq, lc + seq), F32)] * LATENT_SCORE_BUFFERS],
        compiler_params=_params(2),
        name="diff_attn_latent",
    )(q, cache_k, cache_v, k, v, lam_params, subln_g)


def _stack_group(q_ref, rows, kv_local):
    m_lo, m_hi = _half_masks(BF16)
    parts = []
    for gb in range(SWA_GROUP // 2):
        blk = kv_local * (SWA_GROUP // 2) + gb
        qb = q_ref[rows, blk * LANES:(blk + 1) * LANES]
        parts += [qb * m_lo, qb * m_hi]
    return jnp.concatenate(parts, axis=0)


def _sink_column(sink_ref, first_head, tq):
    return jnp.concatenate([jnp.full((tq, LANES), sink_ref[first_head + g] * LOG2E, F32)
                            for g in range(SWA_GROUP)], axis=0)


def _write_group(o_ref, rows, kv_local, o, tq):
    lo64 = _lo64()
    for gb in range(SWA_GROUP // 2):
        blk = kv_local * (SWA_GROUP // 2) + gb
        even = o[(2 * gb) * tq:(2 * gb + 1) * tq]
        odd = o[(2 * gb + 1) * tq:(2 * gb + 2) * tq]
        o_ref[rows, blk * LANES:(blk + 1) * LANES] = jnp.where(lo64, even, odd).astype(BF16)


def _dup_rows(x_t):
    xb = x_t.astype(BF16)
    return jnp.concatenate([xb, xb], axis=0)


def _sink_finish(mrun, sk, rows):
    mb = jnp.maximum(jnp.broadcast_to(jnp.max(mrun, axis=-1, keepdims=True), (rows, LANES)), sk)
    return mb, jnp.exp2(sk - mb)


def _swa_prompt_kernel(sink_ref, q_ref, kt_ref, vt_ref, o_ref, *s_bufs):
    seq = kt_ref.shape[2]
    rows = SWA_GROUP * seq
    ones = _ones_row(seq)
    items = [(r, j) for r in range(kt_ref.shape[0]) for j in range(SWA_KV_HEADS)]

    def scores(item, s_ref):
        r, j = item
        kd = _dup_rows(kt_ref[r, j * SWA_HD:(j + 1) * SWA_HD, :])
        s = jnp.dot(_stack_group(q_ref, slice(r * seq, (r + 1) * seq), j), kd, preferred_element_type=F32)
        return _store_scores(s_ref, 0, s, None)

    def finish(item, s_ref, mrun):
        r, j = item
        sk = _sink_column(sink_ref, j * SWA_GROUP, seq)
        mb, sink_term = _sink_finish(mrun, sk, rows)
        vx = jnp.concatenate([_dup_rows(vt_ref[r, j * SWA_HD:(j + 1) * SWA_HD, :]), ones], axis=0)
        acc = lax.dot_general(_exp_block(s_ref, 0, seq, mb), vx, NT_DIMS, preferred_element_type=F32)
        o = acc[:, :LANES] / (acc[:, LANES:] + sink_term)
        _write_group(o_ref, slice(r * seq, (r + 1) * seq), j, o, seq)

    _run_pipelined(items, scores, finish, s_bufs)


def _swa_prompt_attention(q, kt, vt, sink, *, seq):
    n, d = q.shape
    nkv = kt.shape[1]
    req = 2
    t_spec = pl.BlockSpec((req, nkv, seq), lambda b: (b, 0, 0))
    return pl.pallas_call(
        _swa_prompt_kernel,
        out_shape=jax.ShapeDtypeStruct((n, d), BF16),
        grid=(n // (req * seq),),
        in_specs=[pl.BlockSpec(memory_space=pltpu.SMEM),
                  pl.BlockSpec((req * seq, d), lambda b: (b, 0)), t_spec, t_spec],
        out_specs=pl.BlockSpec((req * seq, d), lambda b: (b, 0)),
        scratch_shapes=[pltpu.VMEM((SWA_GROUP * seq, seq), F32)] * (PROMPT_SCORE_BUFFERS // 2),
        compiler_params=_params(1),
        name="swa_attn_prompt",
    )(sink, q, kt, vt)


def _swa_latent_kernel(sink_ref, q_ref, kc_ref, vc_ref, kl_ref, vl_ref, o_ref, kcd_ref, vcx_ref,
                       s0_ref, s1_ref, *, tq, span):
    pair = pl.program_id(1)
    tiles = q_ref.shape[0] // tq
    first_tile = pl.program_id(2) * tiles
    seq = kl_ref.shape[0]
    lc = kc_ref.shape[1]
    rows = SWA_GROUP * tq
    ones_row = _ones_row(lc)
    for jj in range(2):
        kcd_ref[jj] = _dup_rows(kc_ref[jj * SWA_HD:(jj + 1) * SWA_HD, :])
        vcx_ref[jj] = jnp.concatenate([_dup_rows(vc_ref[jj * SWA_HD:(jj + 1) * SWA_HD, :]), ones_row], axis=0)
    ones_col = _ones_column(span)
    items = [(t, jj) for t in range(tiles) for jj in range(2)]
    windows, biases = {}, {}

    def window(t):
        if t not in windows:
            q0 = (first_tile + t) * tq
            windows[t] = (q0, pl.multiple_of(jnp.clip(q0 - WINDOW, 0, seq - span), WINDOW))
        return windows[t]

    def bias_for(t):
        if t not in biases:
            q0, ws = window(t)
            q_pos = q0 + lax.broadcasted_iota(jnp.int32, (tq, span), 0)
            k_pos = ws + lax.broadcasted_iota(jnp.int32, (tq, span), 1)
            b = jnp.where(jnp.abs(q_pos - k_pos) <= WINDOW, 0.0, NEG_INF).astype(F32)
            biases[t] = jnp.concatenate([b] * SWA_GROUP, axis=0)
        return biases[t]

    def scores(item, s_ref):
        t, jj = item
        _, ws = window(t)
        qs = _stack_group(q_ref, slice(t * tq, (t + 1) * tq), jj)
        s_c = jnp.dot(qs, kcd_ref[jj], preferred_element_type=F32)
        mrun = _store_scores(s_ref, 0, s_c, None)
        s_w = lax.dot_general(qs, kl_ref[pl.ds(ws, span), jj * LANES:(jj + 1) * LANES], NT_DIMS,
                              preferred_element_type=F32) + bias_for(t)
        return _store_scores(s_ref, lc, s_w, mrun)

    def finish(item, s_ref, mrun):
        t, jj = item
        _, ws = window(t)
        sk = _sink_column(sink_ref, (2 * pair + jj) * SWA_GROUP, tq)
        mb, sink_term = _sink_finish(mrun, sk, rows)
        vwx = jnp.concatenate([vl_ref[pl.ds(ws, span), jj * LANES:(jj + 1) * LANES], ones_col], axis=1)
        acc = (lax.dot_general(_exp_block(s_ref, 0, lc, mb), vcx_ref[jj], NT_DIMS, preferred_element_type=F32)
               + jnp.dot(_exp_block(s_ref, lc, span, mb), vwx, preferred_element_type=F32))
        o = acc[:, :LANES] / (acc[:, LANES:] + sink_term)
        _write_group(o_ref, slice(t * tq, (t + 1) * tq), jj, o, tq)

    _run_pipelined(items, scores, finish, (s0_ref, s1_ref))


def _swa_latent_attention(q, kd, vd, cache_kt, cache_vt, sink, *, j, seq):
    n, d = q.shape
    nb, _, lc = cache_kt.shape
    tq = 256
    span = tq + 2 * WINDOW
    npair = SWA_KV_HEADS // 2
    wq = d // npair
    parts = 2
    q_spec = pl.BlockSpec((seq // parts, wq), lambda b, p, i: (b * parts + i, p))
    c_spec = pl.BlockSpec((None, 2 * SWA_HD, lc), lambda b, p, i: (b, j * npair + p, 0))
    l_spec = pl.BlockSpec((seq, 2 * LANES), lambda b, p, i: (b, p))
    s_shape = pltpu.VMEM((SWA_GROUP * tq, lc + span), F32)
    return pl.pallas_call(
        functools.partial(_swa_latent_kernel, tq=tq, span=span),
        out_shape=jax.ShapeDtypeStruct((n, d), BF16),
        grid=(nb, npair, parts),
        in_specs=[pl.BlockSpec(memory_space=pltpu.SMEM), q_spec, c_spec, c_spec, l_spec, l_spec],
        out_specs=q_spec,
        scratch_shapes=[pltpu.VMEM((2, 2 * SWA_HD, lc), BF16), pltpu.VMEM((2, 2 * LANES, lc), BF16),
                        s_shape, s_shape],
        compiler_params=_params(3),
        name="swa_attn_latent",
    )(sink, q, cache_kt, cache_vt, kd, vd)


def _load_weights_as_bf16(jobs, stages, sems):
    order = []
    rings = {w: [] for w in stages}
    for src, dst in jobs:
        w = src.shape[1]
        slots = stages[w].shape[0]
        for k in range(src.shape[0] // WEIGHT_STAGE_ROWS):
            rows = pl.ds(k * WEIGHT_STAGE_ROWS, WEIGHT_STAGE_ROWS)
            slot = len(rings[w]) % slots
            copy = pltpu.make_async_copy(src.at[rows, :], stages[w].at[slot], sems[w].at[slot])
            order.append((w, len(rings[w])))
            rings[w].append((copy, slot, dst, rows))
    for w, ring in rings.items():
        for copy, _, _, _ in ring[:stages[w].shape[0]]:
            copy.start()
    for w, k in order:
        copy, slot, dst, rows = rings[w][k]
        copy.wait()
        dst[rows, :] = stages[w][slot].astype(BF16)
        ahead = k + stages[w].shape[0]
        if ahead < len(rings[w]):
            rings[w][ahead][0].start()


def _weight_stream(pieces, rings):
    per_ring = {name: [] for name in rings}
    plan = []
    for src, dst, idx, name in pieces:
        stage, sem = rings[name]
        slot = len(per_ring[name]) % stage.shape[0]
        view = stage.at[slot].at[0:src.shape[0], 0:src.shape[1]]
        plan.append((name, len(per_ring[name])))
        per_ring[name].append((pltpu.make_async_copy(src, view, sem.at[slot]), view, dst, idx))
    cursor = [0]

    def prime():
        for name, ring in per_ring.items():
            for copy, _, _, _ in ring[:rings[name][0].shape[0]]:
                copy.start()

    def take(n):
        for name, k in plan[cursor[0]:cursor[0] + n]:
            copy, view, dst, idx = per_ring[name][k]
            copy.wait()
            dst[idx] = view[...].astype(BF16)
            ahead = k + rings[name][0].shape[0]
            if ahead < len(per_ring[name]):
                per_ring[name][ahead][0].start()
        cursor[0] += n

    return prime, take


def _post_attn_ffn_kernel(*refs, layer, j, n_prompt_tiles, tiles_per_request, split_x, split_out,
                          overlap_first_tile):
    refs = list(refs)
    op_ref, os_ref = refs[:2]
    x_refs = refs[2:4] if split_x else refs[2:3]
    wo_hbm, wg_hbm, wu_hbm, wd_hbm, g_ref, mod_ref = refs[2 + len(x_refs):8 + len(x_refs)]
    n_out = 2 if split_out else 1
    out_refs = refs[8 + len(x_refs):8 + len(x_refs) + n_out]
    wo_ref, wg_ref, wu_ref, wd_ref, stage_row, stage_col, sem_row, sem_col = refs[8 + len(x_refs) + n_out:]
    d = wo_ref.shape[1]
    dff = wg_ref.shape[1]
    cw = FFN_CHUNK
    i = pl.program_id(0)
    is_prompt = i < n_prompt_tiles
    r = jnp.where(is_prompt, 0, 1 + (i - n_prompt_tiles) // tiles_per_request)

    def mod(slot):
        return mod_ref[pl.ds(r, 1), slot * d:(slot + 1) * d]

    def tile(before_out_proj, before_chunk):
        o = jnp.where(is_prompt, op_ref[...], os_ref[...])
        x = jnp.where(is_prompt, x_refs[0][...], x_refs[1][...]) if split_x else x_refs[0][...]
        before_out_proj()
        y = jnp.dot(o, wo_ref[...], preferred_element_type=F32)
        x = x + _rms(y, mod(2) * g_ref[1:2, :])
        h = (_rms(x, g_ref[2:3, :] * (1 + mod(4))) + mod(3)).astype(BF16)
        y = jnp.zeros((h.shape[0], d), F32)
        for c in range(dff // cw):
            before_chunk(c)
            cols = slice(c * cw, (c + 1) * cw)
            a = jnp.dot(h, wg_ref[:, cols], preferred_element_type=F32)
            u = jnp.dot(h, wu_ref[:, cols], preferred_element_type=F32)
            t = (a * jax.nn.sigmoid(a)) * u
            y = y + jnp.dot(t.astype(BF16), wd_ref[cols, :], preferred_element_type=F32)
        out = x + _rms(y, mod(5) * g_ref[3:4, :])
        if split_out:
            @pl.when(is_prompt)
            def _():
                out_refs[0][...] = out

            @pl.when(jnp.logical_not(is_prompt))
            def _():
                out_refs[1][...] = out
        else:
            out_refs[0][...] = out

    @pl.when(i == 0)
    def _():
        row_chunk = stage_row.shape[1]
        pieces = [(wo_hbm.at[j].at[pl.ds(k * row_chunk, row_chunk), :], wo_ref,
                   (pl.ds(k * row_chunk, row_chunk), slice(None)), "row") for k in range(d // row_chunk)]
        col_chunk = stage_col.shape[2]
        per_take = [d // row_chunk]
        for c in range(dff // cw):
            n = 0
            if (c * cw) % col_chunk == 0:
                cols = pl.ds(c * cw, min(col_chunk, dff - c * cw))
                pieces += [(wg_hbm.at[layer].at[:, cols], wg_ref, (slice(None), cols), "col"),
                           (wu_hbm.at[layer].at[:, cols], wu_ref, (slice(None), cols), "col")]
                n += 2
            rows = pl.ds(c * cw, cw)
            pieces.append((wd_hbm.at[layer].at[rows, :], wd_ref, (rows, slice(None)), "row"))
            per_take.append(n + 1)
        prime, take = _weight_stream(pieces, {"row": (stage_row, sem_row), "col": (stage_col, sem_col)})
        prime()
        if overlap_first_tile:
            tile(lambda: take(per_take[0]), lambda c: take(per_take[c + 1]))
        else:
            take(len(pieces))

    if overlap_first_tile:
        @pl.when(i > 0)
        def _():
            tile(lambda: None, lambda c: None)
    else:
        tile(lambda: None, lambda c: None)


def _post_attn_ffn(o_p, o_s, xs_in, w_o, wg, wu, wd, g, mods, *, layer, j, tiles_per_request, split_out):
    n_p, d = o_p.shape
    n_s = o_s.shape[0]
    tm = ROW_TILE
    tp, ts = n_p // tm, n_s // tm
    prompt_rows = pl.BlockSpec((tm, d), lambda i: (jnp.minimum(i, tp - 1), 0))
    latent_rows = pl.BlockSpec((tm, d), lambda i: (jnp.maximum(i - tp, 0), 0))
    all_rows = pl.BlockSpec((tm, d), lambda i: (i, 0))
    split_x = len(xs_in) == 2
    in_specs = [prompt_rows, latent_rows] + ([prompt_rows, latent_rows] if split_x else [all_rows])
    hbm = pl.BlockSpec(memory_space=pl.ANY)
    in_specs += [hbm, hbm, hbm, hbm, _layer_resident(g.shape, layer), _layer_resident(mods.shape, layer)]
    if split_out:
        out_shape = (jax.ShapeDtypeStruct((n_p, d), F32), jax.ShapeDtypeStruct((n_s, d), F32))
        out_specs = (prompt_rows, latent_rows)
    else:
        out_shape = jax.ShapeDtypeStruct((n_p + n_s, d), F32)
        out_specs = all_rows
    dff = wg.shape[2]
    scratch = [pltpu.VMEM((d, d), BF16), pltpu.VMEM((d, dff), BF16), pltpu.VMEM((d, dff), BF16),
               pltpu.VMEM((dff, d), BF16),
               pltpu.VMEM((WEIGHT_STAGE_SLOTS, FFN_CHUNK, d), F32),
               pltpu.VMEM((WEIGHT_COLUMN_SLOTS, d, 2 * FFN_CHUNK), F32),
               pltpu.SemaphoreType.DMA((WEIGHT_STAGE_SLOTS,)), pltpu.SemaphoreType.DMA((WEIGHT_COLUMN_SLOTS,))]
    return pl.pallas_call(
        functools.partial(_post_attn_ffn_kernel, layer=layer, j=j, n_prompt_tiles=tp,
                          tiles_per_request=tiles_per_request, split_x=split_x, split_out=split_out,
                          overlap_first_tile=not split_x),
        out_shape=out_shape,
        grid=(tp + ts,),
        in_specs=in_specs,
        out_specs=out_specs,
        scratch_shapes=scratch,
        compiler_params=pltpu.CompilerParams(dimension_semantics=("arbitrary",),
                                             vmem_limit_bytes=FFN_VMEM_LIMIT),
        name="post_attn_ffn",
    )(o_p, o_s, *xs_in, w_o, wg, wu, wd, g, mods)


def _rope_tables(n_lat):
    t = np.arange(n_lat)
    row = (t // GRID_W).astype(np.float32)
    col = (t % GRID_W).astype(np.float32)
    nf = ROT_DIM // 4
    inv = np.float32(ROPE_BASE) ** (-np.arange(nf, dtype=np.float32) / np.float32(nf))
    ar = row[:, None] * inv[None, :]
    ac = col[:, None] * inv[None, :]
    ang = np.concatenate([ar, ar, ac, ac], axis=-1)
    cos, sin = np.cos(ang), np.sin(ang)
    sign = np.where((np.arange(ROT_DIM) % 32) < 16, -1.0, 1.0).astype(np.float32)
    reps = LANES // ROT_DIM
    return jnp.asarray(np.tile(cos, (1, reps))), jnp.asarray(np.tile(sin * sign, (1, reps)))


def _swa_cache_to_feature_major(cache):
    nb, nl, lc, nh, hd = cache.shape
    return cache.transpose(0, 1, 3, 4, 2).reshape(nb, nl * nh * hd, lc)


def _swa_cache_from_feature_major(xt, seq):
    nb = xt.shape[0]
    return xt.reshape(nb, SWA_KV_HEADS, SWA_HD, seq).transpose(0, 3, 1, 2)


def kernel(x_prompt, x_sample, cache_diff_k, cache_diff_v, cache_swa_k, cache_swa_v, c, c_ctx,
           w_mod, b_mod, norm_g, w_qkv_diff, diff_lambda, diff_subln_g, w_o_diff,
           w_qkv_swa, swa_sink, w_o_swa, w_gate, w_up, w_down):
    bp, lp, d = x_prompt.shape
    bs, ls, _ = x_sample.shape
    lc = cache_diff_k.shape[2]
    depth = w_mod.shape[0]
    tm = ROW_TILE

    cond8 = jnp.concatenate([c_ctx[None, :], c, jnp.zeros((8 - 1 - bs, d), F32)], axis=0)
    mods = _modulation(cond8, w_mod, b_mod)
    tables = _rope_tables(ls)

    cdk = cache_diff_k.reshape(bs, -1, 2 * DIFF_HD)
    cdv = cache_diff_v.reshape(bs, -1, 2 * DIFF_HD)
    cskt = _swa_cache_to_feature_major(cache_swa_k)
    csvt = _swa_cache_to_feature_major(cache_swa_v)

    n_p, n_s = bp * lp, bs * ls
    x_parts = (x_prompt.reshape(n_p, d), x_sample.reshape(n_s, d))
    g = norm_g
    sub_g = diff_subln_g.reshape(-1, 1, 2 * DIFF_HD)
    diff_k_out, diff_v_out, swa_k_out, swa_v_out = [], [], [], []

    for i in range(depth):
        j = i // N_MIXERS
        if i % N_MIXERS == 0:
            lam_init = 0.8 - 0.6 * math.exp(-0.3 * i)
            qp, kp, vp, qs, ks, vs = _qkv_diff(x_parts, g, mods, w_qkv_diff, tables, layer=i, j=j,
                                               n_p=n_p, n_s=n_s, tiles_per_request=ls // QKV_ROW_TILE)
            op = _diff_prompt_attention(qp, kp, vp, diff_lambda, sub_g, j=j, seq=lp, lam_init=lam_init)
            diff_k_out.append(kp.reshape(bp, lp, DIFF_HEADS, 2 * DIFF_HD))
            diff_v_out.append(vp.reshape(bp, lp, DIFF_HEADS, 2 * DIFF_HD))
            os_ = _diff_latent_attention(qs, ks, vs, cdk, cdv, diff_lambda, sub_g,
                                         j=j, seq=ls, lc=lc, lam_init=lam_init)
            w_o = w_o_diff
        else:
            qp, ktp, vtp, qs, kds, vds = _qkv_swa(x_parts, g, mods, w_qkv_swa, tables, layer=i, j=j,
                                                  n_p=n_p, n_s=n_s, tiles_per_request=ls // QKV_ROW_TILE,
                                                  seq=lp)
            op = _swa_prompt_attention(qp, ktp, vtp, swa_sink[j], seq=lp)
            swa_k_out.append(_swa_cache_from_feature_major(ktp, lp))
            swa_v_out.append(_swa_cache_from_feature_major(vtp, lp))
            os_ = _swa_latent_attention(qs, kds, vds, cskt, csvt, swa_sink[j], j=j, seq=ls)
            w_o = w_o_swa
        last = i == depth - 1
        out = _post_attn_ffn(op, os_, x_parts, w_o, w_gate, w_up, w_down, g, mods, layer=i, j=j,
                             tiles_per_request=ls // tm, split_out=last)
        x_parts = out if last else (out,)
    xp, xs = x_parts

    return (xp.reshape(bp, lp, d), xs.reshape(bs, ls, d),
            jnp.stack(diff_k_out, axis=1), jnp.stack(diff_v_out, axis=1),
            jnp.stack(swa_k_out, axis=1), jnp.stack(swa_v_out, axis=1))
```

```python
import functools
import math

import jax
import jax.numpy as jnp
import numpy as np
from jax import lax
from jax.experimental import pallas as pl
from jax.experimental.pallas import tpu as pltpu

F32 = jnp.float32
BF16 = jnp.bfloat16

GRID_W = 64
N_MIXERS = 2
DIFF_HEADS = 8
DIFF_HD = 64
SWA_HEADS = 16
SWA_KV_HEADS = 4
SWA_GROUP = SWA_HEADS // SWA_KV_HEADS
SWA_HD = 64
ROT_DIM = 64
WINDOW = 128
ROPE_BASE = 10000.0
EPS = 1e-6
NEG_INF = -1e30

LANES = 128
ROW_TILE = 512
QKV_ROW_TILE = 512
PROMPT_SCORE_BUFFERS = 8
LATENT_SCORE_BUFFERS = 2
VMEM_LIMIT = 48 * 1024 * 1024
FFN_VMEM_LIMIT = 58 * 1024 * 1024
WEIGHT_STAGE_ROWS = 128
WEIGHT_STAGE_SLOTS = 3
WEIGHT_COLUMN_SLOTS = 2
QKV_STAGE_SLOTS = 6
FFN_CHUNK = 256
NT_DIMS = (((1,), (1,)), ((), ()))
LOG2E = math.log2(math.e)


def _params(n_axes):
    return pltpu.CompilerParams(dimension_semantics=("arbitrary",) * n_axes,
                                vmem_limit_bytes=VMEM_LIMIT)


def _layer_resident(shape, layer):
    return pl.BlockSpec((None,) + tuple(shape[1:]), lambda *_: (layer,) + (0,) * (len(shape) - 1),
                        pipeline_mode=pl.Buffered(1))


def _rms(x, g):
    ms = jnp.mean(x * x, axis=-1, keepdims=True)
    return (x * lax.rsqrt(ms + EPS)) * g


def _half_masks(dtype):
    lane = lax.broadcasted_iota(jnp.int32, (1, LANES), 1)
    lo = lane < (LANES // 2)
    return jnp.where(lo, 1.0, 0.0).astype(dtype), jnp.where(lo, 0.0, 1.0).astype(dtype)


def _lo64():
    return lax.broadcasted_iota(jnp.int32, (1, LANES), 1) < (LANES // 2)


def _ones_column(rows):
    return jnp.ones((rows, LANES), BF16)


def _ones_row(cols):
    return jnp.ones((LANES, cols), BF16)


def _mod_kernel(cond_ref, w_ref, b_ref, out_ref):
    c = cond_ref[...]
    s = c * jax.nn.sigmoid(c)
    out_ref[...] = jnp.dot(s.astype(BF16), w_ref[...].astype(BF16),
                           preferred_element_type=F32) + b_ref[pl.ds(pl.program_id(0), 1), :]


def _modulation(cond8, w_mod, b_mod):
    depth, d, n = w_mod.shape
    tn = 1536
    return pl.pallas_call(
        _mod_kernel,
        out_shape=jax.ShapeDtypeStruct((depth, 8, n), F32),
        grid=(depth, n // tn),
        in_specs=[pl.BlockSpec((8, d), lambda i, j: (0, 0)),
                  pl.BlockSpec((None, d, tn), lambda i, j: (i, 0, j)),
                  pl.BlockSpec((depth, tn), lambda i, j: (0, j))],
        out_specs=pl.BlockSpec((None, 8, tn), lambda i, j: (i, 0, j)),
        compiler_params=_params(2),
        name="modulation",
    )(cond8, w_mod, b_mod)


def _rope_slab(xs, cos, sin_signed, lo16):
    left = pltpu.roll(xs, LANES - 16, 1)
    right = pltpu.roll(xs, 16, 1)
    return xs * cos + jnp.where(lo16, left, right) * sin_signed


def _lo16_mask():
    lane = lax.broadcasted_iota(jnp.int32, (1, LANES), 1)
    return (lane % 32) < 16


def _qkv_both_groups(refs, n_prompt_tiles, tiles_per_request, split_x, j, prompt_tile, latent_tile):
    refs = list(refs)
    x_refs = refs[:2] if split_x else refs[:1]
    g_ref, mod_ref, w_hbm, cos_ref, sin_ref = refs[len(x_refs):len(x_refs) + 5]
    outs = refs[len(x_refs) + 5:len(x_refs) + 11]
    w_vmem, stage, sem = refs[len(x_refs) + 11:]
    d = x_refs[0].shape[1]
    i = pl.program_id(0)

    @pl.when(i == 0)
    def _():
        _load_weights_as_bf16([(w_hbm.at[j], w_vmem)], {w_vmem.shape[1]: stage}, {w_vmem.shape[1]: sem})

    def pre_norm(x, r):
        shift, scale = mod_ref[pl.ds(r, 1), 0:d], mod_ref[pl.ds(r, 1), d:2 * d]
        return (_rms(x, g_ref[0:1, :] * (1 + scale)) + shift).astype(BF16)

    @pl.when(i < n_prompt_tiles)
    def _():
        prompt_tile(pre_norm(x_refs[0][...], 0), w_vmem, outs[:3])

    @pl.when(i >= n_prompt_tiles)
    def _():
        r = 1 + (i - n_prompt_tiles) // tiles_per_request
        latent_tile(pre_norm(x_refs[-1][...], r), w_vmem, outs[3:], (cos_ref[...], sin_ref[...], _lo16_mask()))


def _qkv_call(kernel, name, x_parts, g, mods, w, tables, prompt_outs, latent_outs, *, layer, n_p, n_s):
    d = x_parts[0].shape[1]
    tm = QKV_ROW_TILE
    tp, ts = n_p // tm, n_s // tm
    nt = tables[0].shape[0] // tm

    def prompt_block(shape):
        return pl.BlockSpec(shape, lambda i: (jnp.minimum(i, tp - 1),) + (0,) * (len(shape) - 1))

    def latent_block(shape):
        return pl.BlockSpec(shape, lambda i: (jnp.maximum(i - tp, 0),) + (0,) * (len(shape) - 1))

    if len(x_parts) == 2:
        x_specs = [prompt_block((tm, d)), latent_block((tm, d))]
    else:
        x_specs = [pl.BlockSpec((tm, d), lambda i: (i, 0))]
    table_spec = pl.BlockSpec((tm, LANES), lambda i: (jnp.maximum(i - tp, 0) % nt, 0))
    in_specs = x_specs + [_layer_resident(g.shape, layer), _layer_resident(mods.shape, layer),
                          pl.BlockSpec(memory_space=pl.ANY), table_spec, table_spec]
    out_shape, out_specs = [], []
    for outs, n, block in ((prompt_outs, n_p, prompt_block), (latent_outs, n_s, latent_block)):
        for shape_of, dtype in outs:
            out_shape.append(jax.ShapeDtypeStruct(shape_of(n), dtype))
            out_specs.append(block(shape_of(tm)))
    cols = w.shape[2]
    scratch = [pltpu.VMEM((d, cols), BF16), pltpu.VMEM((QKV_STAGE_SLOTS, WEIGHT_STAGE_ROWS, cols), F32),
               pltpu.SemaphoreType.DMA((QKV_STAGE_SLOTS,))]
    return pl.pallas_call(
        kernel,
        out_shape=tuple(out_shape),
        grid=(tp + ts,),
        in_specs=in_specs,
        out_specs=tuple(out_specs),
        scratch_shapes=scratch,
        compiler_params=_params(1),
        name=name,
    )(*x_parts, g, mods, w, *tables)


def _diff_tile(h, w_ref, outs, rope=None):
    q_ref, k_ref, v_ref = outs
    d = h.shape[1]
    cw = 512
    for c in range(3 * d // cw):
        acc = jnp.dot(h, w_ref[:, c * cw:(c + 1) * cw], preferred_element_type=F32)
        which, off = divmod(c * cw, d)
        dst = outs[which]
        for s in range(cw // LANES):
            xs = acc[:, s * LANES:(s + 1) * LANES]
            if rope is not None and which < 2:
                xs = _rope_slab(xs, *rope)
            if which == 0:
                xs = xs * (DIFF_HD ** -0.5 * LOG2E)
            lo = off + s * LANES
            if rope is not None or which == 0:
                dst[:, lo:lo + LANES] = xs.astype(dst.dtype)
            else:
                dst[pl.ds(lo // LANES, h.shape[0], stride=DIFF_HEADS), :] = xs


def _qkv_diff_kernel(*refs, j, n_prompt_tiles, tiles_per_request, split_x):
    _qkv_both_groups(refs, n_prompt_tiles, tiles_per_request, split_x, j, _diff_tile, _diff_tile)


def _qkv_diff(x_parts, g, mods, w, tables, *, layer, j, n_p, n_s, tiles_per_request):
    d = x_parts[0].shape[1]

    def rows(n):
        return (n, d)

    def cache(n):
        return (n * DIFF_HEADS, d // DIFF_HEADS)

    kernel = functools.partial(_qkv_diff_kernel, j=j, n_prompt_tiles=n_p // QKV_ROW_TILE,
                               tiles_per_request=tiles_per_request, split_x=len(x_parts) == 2)
    return _qkv_call(kernel, "qkv_diff", x_parts, g, mods, w, tables,
                     [(rows, BF16), (cache, F32), (cache, F32)], [(rows, BF16)] * 3,
                     layer=layer, n_p=n_p, n_s=n_s)


def _swa_q_tile(h, w_ref, q_ref, rope):
    d = h.shape[1]
    cw = 512
    for c in range(d // cw):
        acc = jnp.dot(h, w_ref[:, c * cw:(c + 1) * cw], preferred_element_type=F32)
        for s in range(cw // LANES):
            xs = acc[:, s * LANES:(s + 1) * LANES]
            if rope is not None:
                xs = _rope_slab(xs, *rope)
            lo = c * cw + s * LANES
            q_ref[:, lo:lo + LANES] = (xs * (SWA_HD ** -0.5 * LOG2E)).astype(BF16)
    nkv = SWA_KV_HEADS * SWA_HD
    return jnp.dot(h, w_ref[:, d:d + 2 * nkv], preferred_element_type=F32), nkv


def _swa_prompt_tile(h, w_ref, outs, *, seq):
    q_ref, kt_ref, vt_ref = outs
    kv, nkv = _swa_q_tile(h, w_ref, q_ref, None)
    for b in range(h.shape[0] // seq):
        kt_ref[b] = kv[b * seq:(b + 1) * seq, :nkv].T
        vt_ref[b] = kv[b * seq:(b + 1) * seq, nkv:].T


def _swa_latent_tile(h, w_ref, outs, rope):
    q_ref, kd_ref, vd_ref = outs
    kv, nkv = _swa_q_tile(h, w_ref, q_ref, rope)
    lo64 = _lo64()
    for which, dst in enumerate((kd_ref, vd_ref)):
        for s in range(nkv // LANES):
            xs = kv[:, which * nkv + s * LANES: which * nkv + (s + 1) * LANES]
            if which == 0:
                xs = _rope_slab(xs, *rope)
            sw = pltpu.roll(xs, LANES // 2, 1)
            dst[:, (2 * s) * LANES:(2 * s + 1) * LANES] = jnp.where(lo64, xs, sw).astype(BF16)
            dst[:, (2 * s + 1) * LANES:(2 * s + 2) * LANES] = jnp.where(lo64, sw, xs).astype(BF16)


def _qkv_swa_kernel(*refs, j, n_prompt_tiles, tiles_per_request, split_x, seq):
    _qkv_both_groups(refs, n_prompt_tiles, tiles_per_request, split_x, j,
                     functools.partial(_swa_prompt_tile, seq=seq), _swa_latent_tile)


def _qkv_swa(x_parts, g, mods, w, tables, *, layer, j, n_p, n_s, tiles_per_request, seq):
    d = x_parts[0].shape[1]
    nkv = SWA_KV_HEADS * SWA_HD

    def rows(n):
        return (n, d)

    def feature_major(n):
        return (n // seq, nkv, seq)

    def duplicated(n):
        return (n, 2 * nkv)

    kernel = functools.partial(_qkv_swa_kernel, j=j, n_prompt_tiles=n_p // QKV_ROW_TILE,
                               tiles_per_request=tiles_per_request, split_x=len(x_parts) == 2, seq=seq)
    return _qkv_call(kernel, "qkv_swa", x_parts, g, mods, w, tables,
                     [(rows, BF16), (feature_major, F32), (feature_major, F32)],
                     [(rows, BF16), (duplicated, BF16), (duplicated, BF16)],
                     layer=layer, n_p=n_p, n_s=n_s)


def _diff_lambda(lam_ref, lam_init):
    lp = lam_ref[...]
    a = jnp.sum(lp[0:1] * lp[1:2], axis=-1, keepdims=True)
    b = jnp.sum(lp[2:3] * lp[3:4], axis=-1, keepdims=True)
    return jnp.exp(a) - jnp.exp(b) + lam_init


def _diff_combine(acc, tq, lam, g, lam_init):
    o12 = acc[:, :LANES] / acc[:, LANES:]
    o = o12[:tq] - lam * o12[tq:]
    return _rms(o, g) * (1.0 - lam_init)


def _stack_maps(q):
    m_lo, m_hi = _half_masks(BF16)
    return jnp.concatenate([q * m_lo, q * m_hi], axis=0)


def _run_pipelined(items, scores, finish, s_bufs):
    depth = len(s_bufs)
    states = {i: scores(items[i], s_bufs[i]) for i in range(min(depth - 1, len(items)))}
    for i, item in enumerate(items):
        ahead = i + depth - 1
        if ahead < len(items):
            states[ahead] = scores(items[ahead], s_bufs[ahead % depth])
        finish(item, s_bufs[i % depth], states.pop(i))


def _store_scores(s_ref, col0, s, mrun):
    s_ref[:, col0:col0 + s.shape[1]] = s
    for t in range(s.shape[1] // LANES):
        blk = s[:, t * LANES:(t + 1) * LANES]
        mrun = blk if mrun is None else jnp.maximum(mrun, blk)
    return mrun


def _exp_block(s_ref, col0, width, mb):
    return jnp.concatenate(
        [jnp.exp2(s_ref[:, col0 + t * LANES:col0 + (t + 1) * LANES] - mb).astype(BF16)
         for t in range(width // LANES)], axis=1)


def _diff_prompt_kernel(q_ref, k_ref, v_ref, lam_ref, g_ref, o_ref, *s_bufs, lam_init, seq):
    lam = _diff_lambda(lam_ref, lam_init)
    g = g_ref[...]
    ones = _ones_column(seq)
    items = [(r, h) for r in range(q_ref.shape[0] // seq) for h in range(DIFF_HEADS)]

    def head_rows(ref, r, h):
        return ref[pl.ds(r * seq * DIFF_HEADS + h, seq, stride=DIFF_HEADS), :]

    def scores(item, s_ref):
        r, h = item
        rows, sl = slice(r * seq, (r + 1) * seq), slice(h * LANES, (h + 1) * LANES)
        s = lax.dot_general(_stack_maps(q_ref[rows, sl]), head_rows(k_ref, r, h).astype(BF16), NT_DIMS,
                            preferred_element_type=F32)
        return _store_scores(s_ref, 0, s, None)

    def finish(item, s_ref, mrun):
        r, h = item
        rows, sl = slice(r * seq, (r + 1) * seq), slice(h * LANES, (h + 1) * LANES)
        mb = jnp.broadcast_to(jnp.max(mrun, axis=-1, keepdims=True), (2 * seq, LANES))
        vx = jnp.concatenate([head_rows(v_ref, r, h).astype(BF16), ones], axis=1)
        acc = jnp.dot(_exp_block(s_ref, 0, seq, mb), vx, preferred_element_type=F32)
        o_ref[rows, sl] = _diff_combine(acc, seq, lam, g, lam_init).astype(BF16)

    _run_pipelined(items, scores, finish, s_bufs)


def _diff_prompt_attention(q, k, v, lam_params, subln_g, *, j, seq, lam_init):
    n, d = q.shape
    req = 2
    spec = pl.BlockSpec((req * seq, d), lambda b: (b, 0))
    kv_spec = pl.BlockSpec((req * seq * DIFF_HEADS, d // DIFF_HEADS), lambda b: (b, 0))
    return pl.pallas_call(
        functools.partial(_diff_prompt_kernel, lam_init=lam_init, seq=seq),
        out_shape=jax.ShapeDtypeStruct((n, d), BF16),
        grid=(n // (req * seq),),
        in_specs=[spec, kv_spec, kv_spec, _layer_resident(lam_params.shape, j),
                  _layer_resident(subln_g.shape, j)],
        out_specs=spec,
        scratch_shapes=[pltpu.VMEM((2 * seq, seq), F32)] * PROMPT_SCORE_BUFFERS,
        compiler_params=_params(1),
        name="diff_attn_prompt",
    )(q, k, v, lam_params, subln_g)


def _diff_latent_kernel(q_ref, kc_ref, vc_ref, kl_ref, vl_ref, lam_ref, g_ref, o_ref,
                        kk_ref, vx_ref, *s_bufs, lam_init, lc, tq, key_chunk):
    seq = q_ref.shape[0]
    heads = kk_ref.shape[0]
    nk = kk_ref.shape[1]
    for hh in range(heads):
        h = pl.program_id(1) * heads + hh
        sl = slice(hh * LANES, (hh + 1) * LANES)
        kk_ref[hh, 0:lc, :] = kc_ref[pl.ds(h, lc, stride=DIFF_HEADS), :].astype(BF16)
        kk_ref[hh, lc:, :] = kl_ref[:, sl]
        vx_ref[hh, 0:lc, 0:LANES] = vc_ref[pl.ds(h, lc, stride=DIFF_HEADS), :].astype(BF16)
        vx_ref[hh, lc:, 0:LANES] = vl_ref[:, sl]
        vx_ref[hh, :, LANES:2 * LANES] = _ones_column(nk)

    masks = _half_masks(BF16)
    nchunk = nk // key_chunk
    items = [(hh, rt, m) for hh in range(heads) for rt in range(seq // tq) for m in range(2)]
    lam = _diff_lambda(lam_ref, lam_init)
    g = g_ref[...]
    first_map = {}

    def scores(item, s_ref):
        hh, rt, m = item
        q = q_ref[rt * tq:(rt + 1) * tq, hh * LANES:(hh + 1) * LANES] * masks[m]
        mrun = None
        for c in range(nchunk):
            s = lax.dot_general(q, kk_ref[hh, c * key_chunk:(c + 1) * key_chunk, :], NT_DIMS,
                                preferred_element_type=F32)
            mrun = _store_scores(s_ref, c * key_chunk, s, mrun)
        return mrun

    def finish(item, s_ref, mrun):
        hh, rt, m = item
        mb = jnp.broadcast_to(jnp.max(mrun, axis=-1, keepdims=True), (tq, LANES))
        acc = None
        for c in range(nchunk):
            part = jnp.dot(_exp_block(s_ref, c * key_chunk, key_chunk, mb),
                           vx_ref[hh, c * key_chunk:(c + 1) * key_chunk, :], preferred_element_type=F32)
            acc = part if acc is None else acc + part
        o_m = acc[:, :LANES] / acc[:, LANES:]
        if m == 0:
            first_map[hh, rt] = o_m
        else:
            o = _rms(first_map.pop((hh, rt)) - lam * o_m, g) * (1.0 - lam_init)
            o_ref[rt * tq:(rt + 1) * tq, hh * LANES:(hh + 1) * LANES] = o.astype(BF16)

    _run_pipelined(items, scores, finish, s_bufs)


def _diff_latent_attention(q, k, v, cache_k, cache_v, lam_params, subln_g, *, j, seq, lc, lam_init):
    n, d = q.shape
    nb = cache_k.shape[0]
    tq = 512
    key_chunk = 512
    heads = 1
    q_spec = pl.BlockSpec((seq, heads * LANES), lambda b, h: (b, h))
    c_spec = pl.BlockSpec((None, lc * DIFF_HEADS, LANES), lambda b, h: (b, j, 0))
    return pl.pallas_call(
        functools.partial(_diff_latent_kernel, lam_init=lam_init, lc=lc, tq=tq, key_chunk=key_chunk),
        out_shape=jax.ShapeDtypeStruct((n, d), BF16),
        grid=(nb, DIFF_HEADS // heads),
        in_specs=[q_spec, c_spec, c_spec, q_spec, q_spec,
                  _layer_resident(lam_params.shape, j), _layer_resident(subln_g.shape, j)],
        out_specs=q_spec,
        scratch_shapes=[pltpu.VMEM((heads, lc + seq, LANES), BF16),
                        pltpu.VMEM((heads, lc + seq, 2 * LANES), BF16),
                        *[pltpu.VMEM((tq, lc + seq), F32)] * LATENT_SCORE_BUFFERS],
        compiler_params=_params(2),
        name="diff_attn_latent",
    )(q, cache_k, cache_v, k, v, lam_params, subln_g)


def _stack_group(q_ref, rows, kv_local):
    m_lo, m_hi = _half_masks(BF16)
    parts = []
    for gb in range(SWA_GROUP // 2):
        blk = kv_local * (SWA_GROUP // 2) + gb
        qb = q_ref[rows, blk * LANES:(blk + 1) * LANES]
        parts += [qb * m_lo, qb * m_hi]
    return jnp.concatenate(parts, axis=0)


def _sink_column(sink_ref, first_head, tq):
    return jnp.concatenate([jnp.full((tq, LANES), sink_ref[first_head + g] * LOG2E, F32)
                            for g in range(SWA_GROUP)], axis=0)


def _write_group(o_ref, rows, kv_local, o, tq):
    lo64 = _lo64()
    for gb in range(SWA_GROUP // 2):
        blk = kv_local * (SWA_GROUP // 2) + gb
        even = o[(2 * gb) * tq:(2 * gb + 1) * tq]
        odd = o[(2 * gb + 1) * tq:(2 * gb + 2) * tq]
        o_ref[rows, blk * LANES:(blk + 1) * LANES] = jnp.where(lo64, even, odd).astype(BF16)


def _dup_rows(x_t):
    xb = x_t.astype(BF16)
    return jnp.concatenate([xb, xb], axis=0)


def _sink_finish(mrun, sk, rows):
    mb = jnp.maximum(jnp.broadcast_to(jnp.max(mrun, axis=-1, keepdims=True), (rows, LANES)), sk)
    return mb, jnp.exp2(sk - mb)


def _swa_prompt_kernel(sink_ref, q_ref, kt_ref, vt_ref, o_ref, *s_bufs):
    seq = kt_ref.shape[2]
    rows = SWA_GROUP * seq
    ones = _ones_row(seq)
    items = [(r, j) for r in range(kt_ref.shape[0]) for j in range(SWA_KV_HEADS)]

    def scores(item, s_ref):
        r, j = item
        kd = _dup_rows(kt_ref[r, j * SWA_HD:(j + 1) * SWA_HD, :])
        s = jnp.dot(_stack_group(q_ref, slice(r * seq, (r + 1) * seq), j), kd, preferred_element_type=F32)
        return _store_scores(s_ref, 0, s, None)

    def finish(item, s_ref, mrun):
        r, j = item
        sk = _sink_column(sink_ref, j * SWA_GROUP, seq)
        mb, sink_term = _sink_finish(mrun, sk, rows)
        vx = jnp.concatenate([_dup_rows(vt_ref[r, j * SWA_HD:(j + 1) * SWA_HD, :]), ones], axis=0)
        acc = lax.dot_general(_exp_block(s_ref, 0, seq, mb), vx, NT_DIMS, preferred_element_type=F32)
        o = acc[:, :LANES] / (acc[:, LANES:] + sink_term)
        _write_group(o_ref, slice(r * seq, (r + 1) * seq), j, o, seq)

    _run_pipelined(items, scores, finish, s_bufs)


def _swa_prompt_attention(q, kt, vt, sink, *, seq):
    n, d = q.shape
    nkv = kt.shape[1]
    req = 2
    t_spec = pl.BlockSpec((req, nkv, seq), lambda b: (b, 0, 0))
    return pl.pallas_call(
        _swa_prompt_kernel,
        out_shape=jax.ShapeDtypeStruct((n, d), BF16),
        grid=(n // (req * seq),),
        in_specs=[pl.BlockSpec(memory_space=pltpu.SMEM),
                  pl.BlockSpec((req * seq, d), lambda b: (b, 0)), t_spec, t_spec],
        out_specs=pl.BlockSpec((req * seq, d), lambda b: (b, 0)),
        scratch_shapes=[pltpu.VMEM((SWA_GROUP * seq, seq), F32)] * (PROMPT_SCORE_BUFFERS // 2),
        compiler_params=_params(1),
        name="swa_attn_prompt",
    )(sink, q, kt, vt)


def _swa_latent_kernel(sink_ref, q_ref, kc_ref, vc_ref, kl_ref, vl_ref, o_ref, kcd_ref, vcx_ref,
                       s0_ref, s1_ref, *, tq, span):
    pair = pl.program_id(1)
    tiles = q_ref.shape[0] // tq
    first_tile = pl.program_id(2) * tiles
    seq = kl_ref.shape[0]
    lc = kc_ref.shape[1]
    rows = SWA_GROUP * tq
    ones_row = _ones_row(lc)
    for jj in range(2):
        kcd_ref[jj] = _dup_rows(kc_ref[jj * SWA_HD:(jj + 1) * SWA_HD, :])
        vcx_ref[jj] = jnp.concatenate([_dup_rows(vc_ref[jj * SWA_HD:(jj + 1) * SWA_HD, :]), ones_row], axis=0)
    ones_col = _ones_column(span)
    items = [(t, jj) for t in range(tiles) for jj in range(2)]
    windows, biases = {}, {}

    def window(t):
        if t not in windows:
            q0 = (first_tile + t) * tq
            windows[t] = (q0, pl.multiple_of(jnp.clip(q0 - WINDOW, 0, seq - span), WINDOW))
        return windows[t]

    def bias_for(t):
        if t not in biases:
            q0, ws = window(t)
            q_pos = q0 + lax.broadcasted_iota(jnp.int32, (tq, span), 0)
            k_pos = ws + lax.broadcasted_iota(jnp.int32, (tq, span), 1)
            b = jnp.where(jnp.abs(q_pos - k_pos) <= WINDOW, 0.0, NEG_INF).astype(F32)
            biases[t] = jnp.concatenate([b] * SWA_GROUP, axis=0)
        return biases[t]

    def scores(item, s_ref):
        t, jj = item
        _, ws = window(t)
        qs = _stack_group(q_ref, slice(t * tq, (t + 1) * tq), jj)
        s_c = jnp.dot(qs, kcd_ref[jj], preferred_element_type=F32)
        mrun = _store_scores(s_ref, 0, s_c, None)
        s_w = lax.dot_general(qs, kl_ref[pl.ds(ws, span), jj * LANES:(jj + 1) * LANES], NT_DIMS,
                              preferred_element_type=F32) + bias_for(t)
        return _store_scores(s_ref, lc, s_w, mrun)

    def finish(item, s_ref, mrun):
        t, jj = item
        _, ws = window(t)
        sk = _sink_column(sink_ref, (2 * pair + jj) * SWA_GROUP, tq)
        mb, sink_term = _sink_finish(mrun, sk, rows)
        vwx = jnp.concatenate([vl_ref[pl.ds(ws, span), jj * LANES:(jj + 1) * LANES], ones_col], axis=1)
        acc = (lax.dot_general(_exp_block(s_ref, 0, lc, mb), vcx_ref[jj], NT_DIMS, preferred_element_type=F32)
               + jnp.dot(_exp_block(s_ref, lc, span, mb), vwx, preferred_element_type=F32))
        o = acc[:, :LANES] / (acc[:, LANES:] + sink_term)
        _write_group(o_ref, slice(t * tq, (t + 1) * tq), jj, o, tq)

    _run_pipelined(items, scores, finish, (s0_ref, s1_ref))


def _swa_latent_attention(q, kd, vd, cache_kt, cache_vt, sink, *, j, seq):
    n, d = q.shape
    nb, _, lc = cache_kt.shape
    tq = 256
    span = tq + 2 * WINDOW
    npair = SWA_KV_HEADS // 2
    wq = d // npair
    parts = 1
    q_spec = pl.BlockSpec((seq // parts, wq), lambda b, p, i: (b * parts + i, p))
    c_spec = pl.BlockSpec((None, 2 * SWA_HD, lc), lambda b, p, i: (b, j * npair + p, 0))
    l_spec = pl.BlockSpec((seq, 2 * LANES), lambda b, p, i: (b, p))
    s_shape = pltpu.VMEM((SWA_GROUP * tq, lc + span), F32)
    return pl.pallas_call(
        functools.partial(_swa_latent_kernel, tq=tq, span=span),
        out_shape=jax.ShapeDtypeStruct((n, d), BF16),
        grid=(nb, npair, parts),
        in_specs=[pl.BlockSpec(memory_space=pltpu.SMEM), q_spec, c_spec, c_spec, l_spec, l_spec],
        out_specs=q_spec,
        scratch_shapes=[pltpu.VMEM((2, 2 * SWA_HD, lc), BF16), pltpu.VMEM((2, 2 * LANES, lc), BF16),
                        s_shape, s_shape],
        compiler_params=_params(3),
        name="swa_attn_latent",
    )(sink, q, cache_kt, cache_vt, kd, vd)


def _load_weights_as_bf16(jobs, stages, sems):
    order = []
    rings = {w: [] for w in stages}
    for src, dst in jobs:
        w = src.shape[1]
        slots = stages[w].shape[0]
        for k in range(src.shape[0] // WEIGHT_STAGE_ROWS):
            rows = pl.ds(k * WEIGHT_STAGE_ROWS, WEIGHT_STAGE_ROWS)
            slot = len(rings[w]) % slots
            copy = pltpu.make_async_copy(src.at[rows, :], stages[w].at[slot], sems[w].at[slot])
            order.append((w, len(rings[w])))
            rings[w].append((copy, slot, dst, rows))
    for w, ring in rings.items():
        for copy, _, _, _ in ring[:stages[w].shape[0]]:
            copy.start()
    for w, k in order:
        copy, slot, dst, rows = rings[w][k]
        copy.wait()
        dst[rows, :] = stages[w][slot].astype(BF16)
        ahead = k + stages[w].shape[0]
        if ahead < len(rings[w]):
            rings[w][ahead][0].start()


def _weight_stream(pieces, rings):
    per_ring = {name: [] for name in rings}
    plan = []
    for src, dst, idx, name in pieces:
        stage, sem = rings[name]
        slot = len(per_ring[name]) % stage.shape[0]
        view = stage.at[slot].at[0:src.shape[0], 0:src.shape[1]]
        plan.append((name, len(per_ring[name])))
        per_ring[name].append((pltpu.make_async_copy(src, view, sem.at[slot]), view, dst, idx))
    cursor = [0]

    def prime():
        for name, ring in per_ring.items():
            for copy, _, _, _ in ring[:rings[name][0].shape[0]]:
                copy.start()

    def take(n):
        for name, k in plan[cursor[0]:cursor[0] + n]:
            copy, view, dst, idx = per_ring[name][k]
            copy.wait()
            dst[idx] = view[...].astype(BF16)
            ahead = k + rings[name][0].shape[0]
            if ahead < len(per_ring[name]):
                per_ring[name][ahead][0].start()
        cursor[0] += n

    return prime, take


def _post_attn_ffn_kernel(*refs, layer, j, n_prompt_tiles, tiles_per_request, split_x, split_out,
                          overlap_first_tile):
    refs = list(refs)
    op_ref, os_ref = refs[:2]
    x_refs = refs[2:4] if split_x else refs[2:3]
    wo_hbm, wg_hbm, wu_hbm, wd_hbm, g_ref, mod_ref = refs[2 + len(x_refs):8 + len(x_refs)]
    n_out = 2 if split_out else 1
    out_refs = refs[8 + len(x_refs):8 + len(x_refs) + n_out]
    wo_ref, wg_ref, wu_ref, wd_ref, stage_row, stage_col, sem_row, sem_col = refs[8 + len(x_refs) + n_out:]
    d = wo_ref.shape[1]
    dff = wg_ref.shape[1]
    cw = FFN_CHUNK
    i = pl.program_id(0)
    is_prompt = i < n_prompt_tiles
    r = jnp.where(is_prompt, 0, 1 + (i - n_prompt_tiles) // tiles_per_request)

    def mod(slot):
        return mod_ref[pl.ds(r, 1), slot * d:(slot + 1) * d]

    def tile(before_out_proj, before_chunk):
        o = jnp.where(is_prompt, op_ref[...], os_ref[...])
        x = jnp.where(is_prompt, x_refs[0][...], x_refs[1][...]) if split_x else x_refs[0][...]
        before_out_proj()
        y = jnp.dot(o, wo_ref[...], preferred_element_type=F32)
        x = x + _rms(y, mod(2) * g_ref[1:2, :])
        h = (_rms(x, g_ref[2:3, :] * (1 + mod(4))) + mod(3)).astype(BF16)
        y = jnp.zeros((h.shape[0], d), F32)
        for c in range(dff // cw):
            before_chunk(c)
            cols = slice(c * cw, (c + 1) * cw)
            a = jnp.dot(h, wg_ref[:, cols], preferred_element_type=F32)
            u = jnp.dot(h, wu_ref[:, cols], preferred_element_type=F32)
            t = (a * jax.nn.sigmoid(a)) * u
            y = y + jnp.dot(t.astype(BF16), wd_ref[cols, :], preferred_element_type=F32)
        out = x + _rms(y, mod(5) * g_ref[3:4, :])
        if split_out:
            @pl.when(is_prompt)
            def _():
                out_refs[0][...] = out

            @pl.when(jnp.logical_not(is_prompt))
            def _():
                out_refs[1][...] = out
        else:
            out_refs[0][...] = out

    @pl.when(i == 0)
    def _():
        row_chunk = stage_row.shape[1]
        pieces = [(wo_hbm.at[j].at[pl.ds(k * row_chunk, row_chunk), :], wo_ref,
                   (pl.ds(k * row_chunk, row_chunk), slice(None)), "row") for k in range(d // row_chunk)]
        col_chunk = stage_col.shape[2]
        per_take = [d // row_chunk]
        for c in range(dff // cw):
            n = 0
            if (c * cw) % col_chunk == 0:
                cols = pl.ds(c * cw, min(col_chunk, dff - c * cw))
                pieces += [(wg_hbm.at[layer].at[:, cols], wg_ref, (slice(None), cols), "col"),
                           (wu_hbm.at[layer].at[:, cols], wu_ref, (slice(None), cols), "col")]
                n += 2
            rows = pl.ds(c * cw, cw)
            pieces.append((wd_hbm.at[layer].at[rows, :], wd_ref, (rows, slice(None)), "row"))
            per_take.append(n + 1)
        prime, take = _weight_stream(pieces, {"row": (stage_row, sem_row), "col": (stage_col, sem_col)})
        prime()
        if overlap_first_tile:
            tile(lambda: take(per_take[0]), lambda c: take(per_take[c + 1]))
        else:
            take(len(pieces))

    if overlap_first_tile:
        @pl.when(i > 0)
        def _():
            tile(lambda: None, lambda c: None)
    else:
        tile(lambda: None, lambda c: None)


def _post_attn_ffn(o_p, o_s, xs_in, w_o, wg, wu, wd, g, mods, *, layer, j, tiles_per_request, split_out):
    n_p, d = o_p.shape
    n_s = o_s.shape[0]
    tm = ROW_TILE
    tp, ts = n_p // tm, n_s // tm
    prompt_rows = pl.BlockSpec((tm, d), lambda i: (jnp.minimum(i, tp - 1), 0))
    latent_rows = pl.BlockSpec((tm, d), lambda i: (jnp.maximum(i - tp, 0), 0))
    all_rows = pl.BlockSpec((tm, d), lambda i: (i, 0))
    split_x = len(xs_in) == 2
    in_specs = [prompt_rows, latent_rows] + ([prompt_rows, latent_rows] if split_x else [all_rows])
    hbm = pl.BlockSpec(memory_space=pl.ANY)
    in_specs += [hbm, hbm, hbm, hbm, _layer_resident(g.shape, layer), _layer_resident(mods.shape, layer)]
    if split_out:
        out_shape = (jax.ShapeDtypeStruct((n_p, d), F32), jax.ShapeDtypeStruct((n_s, d), F32))
        out_specs = (prompt_rows, latent_rows)
    else:
        out_shape = jax.ShapeDtypeStruct((n_p + n_s, d), F32)
        out_specs = all_rows
    dff = wg.shape[2]
    scratch = [pltpu.VMEM((d, d), BF16), pltpu.VMEM((d, dff), BF16), pltpu.VMEM((d, dff), BF16),
               pltpu.VMEM((dff, d), BF16),
               pltpu.VMEM((WEIGHT_STAGE_SLOTS, FFN_CHUNK, d), F32),
               pltpu.VMEM((WEIGHT_COLUMN_SLOTS, d, 2 * FFN_CHUNK), F32),
               pltpu.SemaphoreType.DMA((WEIGHT_STAGE_SLOTS,)), pltpu.SemaphoreType.DMA((WEIGHT_COLUMN_SLOTS,))]
    return pl.pallas_call(
        functools.partial(_post_attn_ffn_kernel, layer=layer, j=j, n_prompt_tiles=tp,
                          tiles_per_request=tiles_per_request, split_x=split_x, split_out=split_out,
                          overlap_first_tile=not split_x),
        out_shape=out_shape,
        grid=(tp + ts,),
        in_specs=in_specs,
        out_specs=out_specs,
        scratch_shapes=scratch,
        compiler_params=pltpu.CompilerParams(dimension_semantics=("arbitrary",),
                                             vmem_limit_bytes=FFN_VMEM_LIMIT),
        name="post_attn_ffn",
    )(o_p, o_s, *xs_in, w_o, wg, wu, wd, g, mods)


def _rope_tables(n_lat):
    t = np.arange(n_lat)
    row = (t // GRID_W).astype(np.float32)
    col = (t % GRID_W).astype(np.float32)
    nf = ROT_DIM // 4
    inv = np.float32(ROPE_BASE) ** (-np.arange(nf, dtype=np.float32) / np.float32(nf))
    ar = row[:, None] * inv[None, :]
    ac = col[:, None] * inv[None, :]
    ang = np.concatenate([ar, ar, ac, ac], axis=-1)
    cos, sin = np.cos(ang), np.sin(ang)
    sign = np.where((np.arange(ROT_DIM) % 32) < 16, -1.0, 1.0).astype(np.float32)
    reps = LANES // ROT_DIM
    return jnp.asarray(np.tile(cos, (1, reps))), jnp.asarray(np.tile(sin * sign, (1, reps)))


def _swa_cache_to_feature_major(cache):
    nb, nl, lc, nh, hd = cache.shape
    return cache.transpose(0, 1, 3, 4, 2).reshape(nb, nl * nh * hd, lc)


def _swa_cache_from_feature_major(xt, seq):
    nb = xt.shape[0]
    return xt.reshape(nb, SWA_KV_HEADS, SWA_HD, seq).transpose(0, 3, 1, 2)


def kernel(x_prompt, x_sample, cache_diff_k, cache_diff_v, cache_swa_k, cache_swa_v, c, c_ctx,
           w_mod, b_mod, norm_g, w_qkv_diff, diff_lambda, diff_subln_g, w_o_diff,
           w_qkv_swa, swa_sink, w_o_swa, w_gate, w_up, w_down):
    bp, lp, d = x_prompt.shape
    bs, ls, _ = x_sample.shape
    lc = cache_diff_k.shape[2]
    depth = w_mod.shape[0]
    tm = ROW_TILE

    cond8 = jnp.concatenate([c_ctx[None, :], c, jnp.zeros((8 - 1 - bs, d), F32)], axis=0)
    mods = _modulation(cond8, w_mod, b_mod)
    tables = _rope_tables(ls)

    cdk = cache_diff_k.reshape(bs, -1, 2 * DIFF_HD)
    cdv = cache_diff_v.reshape(bs, -1, 2 * DIFF_HD)
    cskt = _swa_cache_to_feature_major(cache_swa_k)
    csvt = _swa_cache_to_feature_major(cache_swa_v)

    n_p, n_s = bp * lp, bs * ls
    x_parts = (x_prompt.reshape(n_p, d), x_sample.reshape(n_s, d))
    g = norm_g
    sub_g = diff_subln_g.reshape(-1, 1, 2 * DIFF_HD)
    diff_k_out, diff_v_out, swa_k_out, swa_v_out = [], [], [], []

    for i in range(depth):
        j = i // N_MIXERS
        if i % N_MIXERS == 0:
            lam_init = 0.8 - 0.6 * math.exp(-0.3 * i)
            qp, kp, vp, qs, ks, vs = _qkv_diff(x_parts, g, mods, w_qkv_diff, tables, layer=i, j=j,
                                               n_p=n_p, n_s=n_s, tiles_per_request=ls // QKV_ROW_TILE)
            op = _diff_prompt_attention(qp, kp, vp, diff_lambda, sub_g, j=j, seq=lp, lam_init=lam_init)
            diff_k_out.append(kp.reshape(bp, lp, DIFF_HEADS, 2 * DIFF_HD))
            diff_v_out.append(vp.reshape(bp, lp, DIFF_HEADS, 2 * DIFF_HD))
            os_ = _diff_latent_attention(qs, ks, vs, cdk, cdv, diff_lambda, sub_g,
                                         j=j, seq=ls, lc=lc, lam_init=lam_init)
            w_o = w_o_diff
        else:
            qp, ktp, vtp, qs, kds, vds = _qkv_swa(x_parts, g, mods, w_qkv_swa, tables, layer=i, j=j,
                                                  n_p=n_p, n_s=n_s, tiles_per_request=ls // QKV_ROW_TILE,
                                                  seq=lp)
            op = _swa_prompt_attention(qp, ktp, vtp, swa_sink[j], seq=lp)
            swa_k_out.append(_swa_cache_from_feature_major(ktp, lp))
            swa_v_out.append(_swa_cache_from_feature_major(vtp, lp))
            os_ = _swa_latent_attention(qs, kds, vds, cskt, csvt, swa_sink[j], j=j, seq=ls)
            w_o = w_o_swa
        last = i == depth - 1
        out = _post_attn_ffn(op, os_, x_parts, w_o, w_gate, w_up, w_down, g, mods, layer=i, j=j,
                             tiles_per_request=ls // tm, split_out=last)
        x_parts = out if last else (out,)
    xp, xs = x_parts

    return (xp.reshape(bp, lp, d), xs.reshape(bs, ls, d),
            jnp.stack(diff_k_out, axis=1), jnp.stack(diff_v_out, axis=1),
            jnp.stack(swa_k_out, axis=1), jnp.stack(swa_v_out, axis=1))
```

```python
import functools
import math

import jax
import jax.numpy as jnp
import numpy as np
from jax import lax
from jax.experimental import pallas as pl
from jax.experimental.pallas import tpu as pltpu

F32 = jnp.float32
BF16 = jnp.bfloat16

GRID_W = 64
N_MIXERS = 2
DIFF_HEADS = 8
DIFF_HD = 64
SWA_HEADS = 16
SWA_KV_HEADS = 4
SWA_GROUP = SWA_HEADS // SWA_KV_HEADS
SWA_HD = 64
ROT_DIM = 64
WINDOW = 128
ROPE_BASE = 10000.0
EPS = 1e-6
NEG_INF = -1e30

LANES = 128
ROW_TILE = 512
QKV_ROW_TILE = 512
PROMPT_SCORE_BUFFERS = 8
LATENT_SCORE_BUFFERS = 2
VMEM_LIMIT = 48 * 1024 * 1024
FFN_VMEM_LIMIT = 58 * 1024 * 1024
WEIGHT_STAGE_ROWS = 128
WEIGHT_STAGE_SLOTS = 3
WEIGHT_COLUMN_SLOTS = 2
QKV_STAGE_SLOTS = 4
FFN_CHUNK = 256
NT_DIMS = (((1,), (1,)), ((), ()))
LOG2E = math.log2(math.e)


def _params(n_axes):
    return pltpu.CompilerParams(dimension_semantics=("arbitrary",) * n_axes,
                                vmem_limit_bytes=VMEM_LIMIT)


def _layer_resident(shape, layer):
    return pl.BlockSpec((None,) + tuple(shape[1:]), lambda *_: (layer,) + (0,) * (len(shape) - 1),
                        pipeline_mode=pl.Buffered(1))


def _rms(x, g):
    ms = jnp.mean(x * x, axis=-1, keepdims=True)
    return (x * lax.rsqrt(ms + EPS)) * g


def _half_masks(dtype):
    lane = lax.broadcasted_iota(jnp.int32, (1, LANES), 1)
    lo = lane < (LANES // 2)
    return jnp.where(lo, 1.0, 0.0).astype(dtype), jnp.where(lo, 0.0, 1.0).astype(dtype)


def _lo64():
    return lax.broadcasted_iota(jnp.int32, (1, LANES), 1) < (LANES // 2)


def _ones_column(rows):
    return jnp.ones((rows, LANES), BF16)


def _ones_row(cols):
    return jnp.ones((LANES, cols), BF16)


def _mod_kernel(cond_ref, w_ref, b_ref, out_ref):
    c = cond_ref[...]
    s = c * jax.nn.sigmoid(c)
    out_ref[...] = jnp.dot(s.astype(BF16), w_ref[...].astype(BF16),
                           preferred_element_type=F32) + b_ref[pl.ds(pl.program_id(0), 1), :]


def _modulation(cond8, w_mod, b_mod):
    depth, d, n = w_mod.shape
    tn = 1536
    return pl.pallas_call(
        _mod_kernel,
        out_shape=jax.ShapeDtypeStruct((depth, 8, n), F32),
        grid=(depth, n // tn),
        in_specs=[pl.BlockSpec((8, d), lambda i, j: (0, 0)),
                  pl.BlockSpec((None, d, tn), lambda i, j: (i, 0, j)),
                  pl.BlockSpec((depth, tn), lambda i, j: (0, j))],
        out_specs=pl.BlockSpec((None, 8, tn), lambda i, j: (i, 0, j)),
        compiler_params=_params(2),
        name="modulation",
    )(cond8, w_mod, b_mod)


def _rope_slab(xs, cos, sin_signed, lo16):
    left = pltpu.roll(xs, LANES - 16, 1)
    right = pltpu.roll(xs, 16, 1)
    return xs * cos + jnp.where(lo16, left, right) * sin_signed


def _lo16_mask():
    lane = lax.broadcasted_iota(jnp.int32, (1, LANES), 1)
    return (lane % 32) < 16


def _qkv_both_groups(refs, n_prompt_tiles, tiles_per_request, split_x, j, prompt_tile, latent_tile):
    refs = list(refs)
    x_refs = refs[:2] if split_x else refs[:1]
    g_ref, mod_ref, w_hbm, cos_ref, sin_ref = refs[len(x_refs):len(x_refs) + 5]
    outs = refs[len(x_refs) + 5:len(x_refs) + 11]
    w_vmem, stage, sem = refs[len(x_refs) + 11:]
    d = x_refs[0].shape[1]
    i = pl.program_id(0)

    @pl.when(i == 0)
    def _():
        _load_weights_as_bf16([(w_hbm.at[j], w_vmem)], {w_vmem.shape[1]: stage}, {w_vmem.shape[1]: sem})

    def pre_norm(x, r):
        shift, scale = mod_ref[pl.ds(r, 1), 0:d], mod_ref[pl.ds(r, 1), d:2 * d]
        return (_rms(x, g_ref[0:1, :] * (1 + scale)) + shift).astype(BF16)

    @pl.when(i < n_prompt_tiles)
    def _():
        prompt_tile(pre_norm(x_refs[0][...], 0), w_vmem, outs[:3])

    @pl.when(i >= n_prompt_tiles)
    def _():
        r = 1 + (i - n_prompt_tiles) // tiles_per_request
        latent_tile(pre_norm(x_refs[-1][...], r), w_vmem, outs[3:], (cos_ref[...], sin_ref[...], _lo16_mask()))


def _qkv_call(kernel, name, x_parts, g, mods, w, tables, prompt_outs, latent_outs, *, layer, n_p, n_s):
    d = x_parts[0].shape[1]
    tm = QKV_ROW_TILE
    tp, ts = n_p // tm, n_s // tm
    nt = tables[0].shape[0] // tm

    def prompt_block(shape):
        return pl.BlockSpec(shape, lambda i: (jnp.minimum(i, tp - 1),) + (0,) * (len(shape) - 1))

    def latent_block(shape):
        return pl.BlockSpec(shape, lambda i: (jnp.maximum(i - tp, 0),) + (0,) * (len(shape) - 1))

    if len(x_parts) == 2:
        x_specs = [prompt_block((tm, d)), latent_block((tm, d))]
    else:
        x_specs = [pl.BlockSpec((tm, d), lambda i: (i, 0))]
    table_spec = pl.BlockSpec((tm, LANES), lambda i: (jnp.maximum(i - tp, 0) % nt, 0))
    in_specs = x_specs + [_layer_resident(g.shape, layer), _layer_resident(mods.shape, layer),
                          pl.BlockSpec(memory_space=pl.ANY), table_spec, table_spec]
    out_shape, out_specs = [], []
    for outs, n, block in ((prompt_outs, n_p, prompt_block), (latent_outs, n_s, latent_block)):
        for shape_of, dtype in outs:
            out_shape.append(jax.ShapeDtypeStruct(shape_of(n), dtype))
            out_specs.append(block(shape_of(tm)))
    cols = w.shape[2]
    scratch = [pltpu.VMEM((d, cols), BF16), pltpu.VMEM((QKV_STAGE_SLOTS, WEIGHT_STAGE_ROWS, cols), F32),
               pltpu.SemaphoreType.DMA((QKV_STAGE_SLOTS,))]
    return pl.pallas_call(
        kernel,
        out_shape=tuple(out_shape),
        grid=(tp + ts,),
        in_specs=in_specs,
        out_specs=tuple(out_specs),
        scratch_shapes=scratch,
        compiler_params=_params(1),
        name=name,
    )(*x_parts, g, mods, w, *tables)


def _diff_tile(h, w_ref, outs, rope=None):
    q_ref, k_ref, v_ref = outs
    d = h.shape[1]
    cw = 512
    for c in range(3 * d // cw):
        acc = jnp.dot(h, w_ref[:, c * cw:(c + 1) * cw], preferred_element_type=F32)
        which, off = divmod(c * cw, d)
        dst = outs[which]
        for s in range(cw // LANES):
            xs = acc[:, s * LANES:(s + 1) * LANES]
            if rope is not None and which < 2:
                xs = _rope_slab(xs, *rope)
            if which == 0:
                xs = xs * (DIFF_HD ** -0.5 * LOG2E)
            lo = off + s * LANES
            if rope is not None or which == 0:
                dst[:, lo:lo + LANES] = xs.astype(dst.dtype)
            else:
                dst[pl.ds(lo // LANES, h.shape[0], stride=DIFF_HEADS), :] = xs


def _qkv_diff_kernel(*refs, j, n_prompt_tiles, tiles_per_request, split_x):
    _qkv_both_groups(refs, n_prompt_tiles, tiles_per_request, split_x, j, _diff_tile, _diff_tile)


def _qkv_diff(x_parts, g, mods, w, tables, *, layer, j, n_p, n_s, tiles_per_request):
    d = x_parts[0].shape[1]

    def rows(n):
        return (n, d)

    def cache(n):
        return (n * DIFF_HEADS, d // DIFF_HEADS)

    kernel = functools.partial(_qkv_diff_kernel, j=j, n_prompt_tiles=n_p // QKV_ROW_TILE,
                               tiles_per_request=tiles_per_request, split_x=len(x_parts) == 2)
    return _qkv_call(kernel, "qkv_diff", x_parts, g, mods, w, tables,
                     [(rows, BF16), (cache, F32), (cache, F32)], [(rows, BF16)] * 3,
                     layer=layer, n_p=n_p, n_s=n_s)


def _swa_q_tile(h, w_ref, q_ref, rope):
    d = h.shape[1]
    cw = 512
    for c in range(d // cw):
        acc = jnp.dot(h, w_ref[:, c * cw:(c + 1) * cw], preferred_element_type=F32)
        for s in range(cw // LANES):
            xs = acc[:, s * LANES:(s + 1) * LANES]
            if rope is not None:
                xs = _rope_slab(xs, *rope)
            lo = c * cw + s * LANES
            q_ref[:, lo:lo + LANES] = (xs * (SWA_HD ** -0.5 * LOG2E)).astype(BF16)
    nkv = SWA_KV_HEADS * SWA_HD
    return jnp.dot(h, w_ref[:, d:d + 2 * nkv], preferred_element_type=F32), nkv


def _swa_prompt_tile(h, w_ref, outs, *, seq):
    q_ref, kt_ref, vt_ref = outs
    kv, nkv = _swa_q_tile(h, w_ref, q_ref, None)
    for b in range(h.shape[0] // seq):
        kt_ref[b] = kv[b * seq:(b + 1) * seq, :nkv].T
        vt_ref[b] = kv[b * seq:(b + 1) * seq, nkv:].T


def _swa_latent_tile(h, w_ref, outs, rope):
    q_ref, kd_ref, vd_ref = outs
    kv, nkv = _swa_q_tile(h, w_ref, q_ref, rope)
    lo64 = _lo64()
    for which, dst in enumerate((kd_ref, vd_ref)):
        for s in range(nkv // LANES):
            xs = kv[:, which * nkv + s * LANES: which * nkv + (s + 1) * LANES]
            if which == 0:
                xs = _rope_slab(xs, *rope)
            sw = pltpu.roll(xs, LANES // 2, 1)
            dst[:, (2 * s) * LANES:(2 * s + 1) * LANES] = jnp.where(lo64, xs, sw).astype(BF16)
            dst[:, (2 * s + 1) * LANES:(2 * s + 2) * LANES] = jnp.where(lo64, sw, xs).astype(BF16)


def _qkv_swa_kernel(*refs, j, n_prompt_tiles, tiles_per_request, split_x, seq):
    _qkv_both_groups(refs, n_prompt_tiles, tiles_per_request, split_x, j,
                     functools.partial(_swa_prompt_tile, seq=seq), _swa_latent_tile)


def _qkv_swa(x_parts, g, mods, w, tables, *, layer, j, n_p, n_s, tiles_per_request, seq):
    d = x_parts[0].shape[1]
    nkv = SWA_KV_HEADS * SWA_HD

    def rows(n):
        return (n, d)

    def feature_major(n):
        return (n // seq, nkv, seq)

    def duplicated(n):
        return (n, 2 * nkv)

    kernel = functools.partial(_qkv_swa_kernel, j=j, n_prompt_tiles=n_p // QKV_ROW_TILE,
                               tiles_per_request=tiles_per_request, split_x=len(x_parts) == 2, seq=seq)
    return _qkv_call(kernel, "qkv_swa", x_parts, g, mods, w, tables,
                     [(rows, BF16), (feature_major, F32), (feature_major, F32)],
                     [(rows, BF16), (duplicated, BF16), (duplicated, BF16)],
                     layer=layer, n_p=n_p, n_s=n_s)


def _diff_lambda(lam_ref, lam_init):
    lp = lam_ref[...]
    a = jnp.sum(lp[0:1] * lp[1:2], axis=-1, keepdims=True)
    b = jnp.sum(lp[2:3] * lp[3:4], axis=-1, keepdims=True)
    return jnp.exp(a) - jnp.exp(b) + lam_init


def _diff_combine(acc, tq, lam, g, lam_init):
    o12 = acc[:, :LANES] / acc[:, LANES:]
    o = o12[:tq] - lam * o12[tq:]
    return _rms(o, g) * (1.0 - lam_init)


def _stack_maps(q):
    m_lo, m_hi = _half_masks(BF16)
    return jnp.concatenate([q * m_lo, q * m_hi], axis=0)


def _run_pipelined(items, scores, finish, s_bufs):
    depth = len(s_bufs)
    states = {i: scores(items[i], s_bufs[i]) for i in range(min(depth - 1, len(items)))}
    for i, item in enumerate(items):
        ahead = i + depth - 1
        if ahead < len(items):
            states[ahead] = scores(items[ahead], s_bufs[ahead % depth])
        finish(item, s_bufs[i % depth], states.pop(i))


def _store_scores(s_ref, col0, s, mrun):
    s_ref[:, col0:col0 + s.shape[1]] = s
    for t in range(s.shape[1] // LANES):
        blk = s[:, t * LANES:(t + 1) * LANES]
        mrun = blk if mrun is None else jnp.maximum(mrun, blk)
    return mrun


def _exp_block(s_ref, col0, width, mb):
    return jnp.concatenate(
        [jnp.exp2(s_ref[:, col0 + t * LANES:col0 + (t + 1) * LANES] - mb).astype(BF16)
         for t in range(width // LANES)], axis=1)


def _diff_prompt_kernel(q_ref, k_ref, v_ref, lam_ref, g_ref, o_ref, *s_bufs, lam_init, seq):
    lam = _diff_lambda(lam_ref, lam_init)
    g = g_ref[...]
    ones = _ones_column(seq)
    items = [(r, h) for r in range(q_ref.shape[0] // seq) for h in range(DIFF_HEADS)]

    def head_rows(ref, r, h):
        return ref[pl.ds(r * seq * DIFF_HEADS + h, seq, stride=DIFF_HEADS), :]

    def scores(item, s_ref):
        r, h = item
        rows, sl = slice(r * seq, (r + 1) * seq), slice(h * LANES, (h + 1) * LANES)
        s = lax.dot_general(_stack_maps(q_ref[rows, sl]), head_rows(k_ref, r, h).astype(BF16), NT_DIMS,
                            preferred_element_type=F32)
        return _store_scores(s_ref, 0, s, None)

    def finish(item, s_ref, mrun):
        r, h = item
        rows, sl = slice(r * seq, (r + 1) * seq), slice(h * LANES, (h + 1) * LANES)
        mb = jnp.broadcast_to(jnp.max(mrun, axis=-1, keepdims=True), (2 * seq, LANES))
        vx = jnp.concatenate([head_rows(v_ref, r, h).astype(BF16), ones], axis=1)
        acc = jnp.dot(_exp_block(s_ref, 0, seq, mb), vx, preferred_element_type=F32)
        o_ref[rows, sl] = _diff_combine(acc, seq, lam, g, lam_init).astype(BF16)

    _run_pipelined(items, scores, finish, s_bufs)


def _diff_prompt_attention(q, k, v, lam_params, subln_g, *, j, seq, lam_init):
    n, d = q.shape
    req = 2
    spec = pl.BlockSpec((req * seq, d), lambda b: (b, 0))
    kv_spec = pl.BlockSpec((req * seq * DIFF_HEADS, d // DIFF_HEADS), lambda b: (b, 0))
    return pl.pallas_call(
        functools.partial(_diff_prompt_kernel, lam_init=lam_init, seq=seq),
        out_shape=jax.ShapeDtypeStruct((n, d), BF16),
        grid=(n // (req * seq),),
        in_specs=[spec, kv_spec, kv_spec, _layer_resident(lam_params.shape, j),
                  _layer_resident(subln_g.shape, j)],
        out_specs=spec,
        scratch_shapes=[pltpu.VMEM((2 * seq, seq), F32)] * PROMPT_SCORE_BUFFERS,
        compiler_params=_params(1),
        name="diff_attn_prompt",
    )(q, k, v, lam_params, subln_g)


def _diff_latent_kernel(q_ref, kc_ref, vc_ref, kl_ref, vl_ref, lam_ref, g_ref, o_ref,
                        kk_ref, vx_ref, *s_bufs, lam_init, lc, tq, key_chunk):
    seq = q_ref.shape[0]
    heads = kk_ref.shape[0]
    nk = kk_ref.shape[1]
    for hh in range(heads):
        h = pl.program_id(1) * heads + hh
        sl = slice(hh * LANES, (hh + 1) * LANES)
        kk_ref[hh, 0:lc, :] = kc_ref[pl.ds(h, lc, stride=DIFF_HEADS), :].astype(BF16)
        kk_ref[hh, lc:, :] = kl_ref[:, sl]
        vx_ref[hh, 0:lc, 0:LANES] = vc_ref[pl.ds(h, lc, stride=DIFF_HEADS), :].astype(BF16)
        vx_ref[hh, lc:, 0:LANES] = vl_ref[:, sl]
        vx_ref[hh, :, LANES:2 * LANES] = _ones_column(nk)

    masks = _half_masks(BF16)
    nchunk = nk // key_chunk
    items = [(hh, rt, m) for hh in range(heads) for rt in range(seq // tq) for m in range(2)]
    lam = _diff_lambda(lam_ref, lam_init)
    g = g_ref[...]
    first_map = {}

    def scores(item, s_ref):
        hh, rt, m = item
        q = q_ref[rt * tq:(rt + 1) * tq, hh * LANES:(hh + 1) * LANES] * masks[m]
        mrun = None
        for c in range(nchunk):
            s = lax.dot_general(q, kk_ref[hh, c * key_chunk:(c + 1) * key_chunk, :], NT_DIMS,
                                preferred_element_type=F32)
            mrun = _store_scores(s_ref, c * key_chunk, s, mrun)
        return mrun

    def finish(item, s_ref, mrun):
        hh, rt, m = item
        mb = jnp.broadcast_to(jnp.max(mrun, axis=-1, keepdims=True), (tq, LANES))
        acc = None
        for c in range(nchunk):
            part = jnp.dot(_exp_block(s_ref, c * key_chunk, key_chunk, mb),
                           vx_ref[hh, c * key_chunk:(c + 1) * key_chunk, :], preferred_element_type=F32)
            acc = part if acc is None else acc + part
        o_m = acc[:, :LANES] / acc[:, LANES:]
        if m == 0:
            first_map[hh, rt] = o_m
        else:
            o = _rms(first_map.pop((hh, rt)) - lam * o_m, g) * (1.0 - lam_init)
            o_ref[rt * tq:(rt + 1) * tq, hh * LANES:(hh + 1) * LANES] = o.astype(BF16)

    _run_pipelined(items, scores, finish, s_bufs)


def _diff_latent_attention(q, k, v, cache_k, cache_v, lam_params, subln_g, *, j, seq, lc, lam_init):
    n, d = q.shape
    nb = cache_k.shape[0]
    tq = 512
    key_chunk = 512
    heads = 1
    q_spec = pl.BlockSpec((seq, heads * LANES), lambda b, h: (b, h))
    c_spec = pl.BlockSpec((None, lc * DIFF_HEADS, LANES), lambda b, h: (b, j, 0))
    return pl.pallas_call(
        functools.partial(_diff_latent_kernel, lam_init=lam_init, lc=lc, tq=tq, key_chunk=key_chunk),
        out_shape=jax.ShapeDtypeStruct((n, d), BF16),
        grid=(nb, DIFF_HEADS // heads),
        in_specs=[q_spec, c_spec, c_spec, q_spec, q_spec,
                  _layer_resident(lam_params.shape, j), _layer_resident(subln_g.shape, j)],
        out_specs=q_spec,
        scratch_shapes=[pltpu.VMEM((heads, lc + seq, LANES), BF16),
                        pltpu.VMEM((heads, lc + seq, 2 * LANES), BF16),
                        *[pltpu.VMEM((tq, lc + seq), F32)] * LATENT_SCORE_BUFFERS],
        compiler_params=_params(2),
        name="diff_attn_latent",
    )(q, cache_k, cache_v, k, v, lam_params, subln_g)


def _stack_group(q_ref, rows, kv_local):
    m_lo, m_hi = _half_masks(BF16)
    parts = []
    for gb in range(SWA_GROUP // 2):
        blk = kv_local * (SWA_GROUP // 2) + gb
        qb = q_ref[rows, blk * LANES:(blk + 1) * LANES]
        parts += [qb * m_lo, qb * m_hi]
    return jnp.concatenate(parts, axis=0)


def _sink_column(sink_ref, first_head, tq):
    return jnp.concatenate([jnp.full((tq, LANES), sink_ref[first_head + g] * LOG2E, F32)
                            for g in range(SWA_GROUP)], axis=0)


def _write_group(o_ref, rows, kv_local, o, tq):
    lo64 = _lo64()
    for gb in range(SWA_GROUP // 2):
        blk = kv_local * (SWA_GROUP // 2) + gb
        even = o[(2 * gb) * tq:(2 * gb + 1) * tq]
        odd = o[(2 * gb + 1) * tq:(2 * gb + 2) * tq]
        o_ref[rows, blk * LANES:(blk + 1) * LANES] = jnp.where(lo64, even, odd).astype(BF16)


def _dup_rows(x_t):
    xb = x_t.astype(BF16)
    return jnp.concatenate([xb, xb], axis=0)


def _sink_finish(mrun, sk, rows):
    mb = jnp.maximum(jnp.broadcast_to(jnp.max(mrun, axis=-1, keepdims=True), (rows, LANES)), sk)
    return mb, jnp.exp2(sk - mb)


def _swa_prompt_kernel(sink_ref, q_ref, kt_ref, vt_ref, o_ref, *s_bufs):
    seq = kt_ref.shape[2]
    rows = SWA_GROUP * seq
    ones = _ones_row(seq)
    items = [(r, j) for r in range(kt_ref.shape[0]) for j in range(SWA_KV_HEADS)]

    def scores(item, s_ref):
        r, j = item
        kd = _dup_rows(kt_ref[r, j * SWA_HD:(j + 1) * SWA_HD, :])
        s = jnp.dot(_stack_group(q_ref, slice(r * seq, (r + 1) * seq), j), kd, preferred_element_type=F32)
        return _store_scores(s_ref, 0, s, None)

    def finish(item, s_ref, mrun):
        r, j = item
        sk = _sink_column(sink_ref, j * SWA_GROUP, seq)
        mb, sink_term = _sink_finish(mrun, sk, rows)
        vx = jnp.concatenate([_dup_rows(vt_ref[r, j * SWA_HD:(j + 1) * SWA_HD, :]), ones], axis=0)
        acc = lax.dot_general(_exp_block(s_ref, 0, seq, mb), vx, NT_DIMS, preferred_element_type=F32)
        o = acc[:, :LANES] / (acc[:, LANES:] + sink_term)
        _write_group(o_ref, slice(r * seq, (r + 1) * seq), j, o, seq)

    _run_pipelined(items, scores, finish, s_bufs)


def _swa_prompt_attention(q, kt, vt, sink, *, seq):
    n, d = q.shape
    nkv = kt.shape[1]
    req = 2
    t_spec = pl.BlockSpec((req, nkv, seq), lambda b: (b, 0, 0))
    return pl.pallas_call(
        _swa_prompt_kernel,
        out_shape=jax.ShapeDtypeStruct((n, d), BF16),
        grid=(n // (req * seq),),
        in_specs=[pl.BlockSpec(memory_space=pltpu.SMEM),
                  pl.BlockSpec((req * seq, d), lambda b: (b, 0)), t_spec, t_spec],
        out_specs=pl.BlockSpec((req * seq, d), lambda b: (b, 0)),
        scratch_shapes=[pltpu.VMEM((SWA_GROUP * seq, seq), F32)] * (PROMPT_SCORE_BUFFERS // 2),
        compiler_params=_params(1),
        name="swa_attn_prompt",
    )(sink, q, kt, vt)


def _swa_latent_kernel(sink_ref, q_ref, kc_ref, vc_ref, kl_ref, vl_ref, o_ref, kcd_ref, vcx_ref,
                       s0_ref, s1_ref, *, tq, span):
    pair = pl.program_id(1)
    tiles = q_ref.shape[0] // tq
    first_tile = pl.program_id(2) * tiles
    seq = kl_ref.shape[0]
    lc = kc_ref.shape[1]
    rows = SWA_GROUP * tq
    ones_row = _ones_row(lc)
    for jj in range(2):
        kcd_ref[jj] = _dup_rows(kc_ref[jj * SWA_HD:(jj + 1) * SWA_HD, :])
        vcx_ref[jj] = jnp.concatenate([_dup_rows(vc_ref[jj * SWA_HD:(jj + 1) * SWA_HD, :]), ones_row], axis=0)
    ones_col = _ones_column(span)
    items = [(t, jj) for t in range(tiles) for jj in range(2)]
    windows, biases = {}, {}

    def window(t):
        if t not in windows:
            q0 = (first_tile + t) * tq
            windows[t] = (q0, pl.multiple_of(jnp.clip(q0 - WINDOW, 0, seq - span), WINDOW))
        return windows[t]

    def bias_for(t):
        if t not in biases:
            q0, ws = window(t)
            q_pos = q0 + lax.broadcasted_iota(jnp.int32, (tq, span), 0)
            k_pos = ws + lax.broadcasted_iota(jnp.int32, (tq, span), 1)
            b = jnp.where(jnp.abs(q_pos - k_pos) <= WINDOW, 0.0, NEG_INF).astype(F32)
            biases[t] = jnp.concatenate([b] * SWA_GROUP, axis=0)
        return biases[t]

    def scores(item, s_ref):
        t, jj = item
        _, ws = window(t)
        qs = _stack_group(q_ref, slice(t * tq, (t + 1) * tq), jj)
        s_c = jnp.dot(qs, kcd_ref[jj], preferred_element_type=F32)
        mrun = _store_scores(s_ref, 0, s_c, None)
        s_w = lax.dot_general(qs, kl_ref[pl.ds(ws, span), jj * LANES:(jj + 1) * LANES], NT_DIMS,
                              preferred_element_type=F32) + bias_for(t)
        return _store_scores(s_ref, lc, s_w, mrun)

    def finish(item, s_ref, mrun):
        t, jj = item
        _, ws = window(t)
        sk = _sink_column(sink_ref, (2 * pair + jj) * SWA_GROUP, tq)
        mb, sink_term = _sink_finish(mrun, sk, rows)
        vwx = jnp.concatenate([vl_ref[pl.ds(ws, span), jj * LANES:(jj + 1) * LANES], ones_col], axis=1)
        acc = (lax.dot_general(_exp_block(s_ref, 0, lc, mb), vcx_ref[jj], NT_DIMS, preferred_element_type=F32)
               + jnp.dot(_exp_block(s_ref, lc, span, mb), vwx, preferred_element_type=F32))
        o = acc[:, :LANES] / (acc[:, LANES:] + sink_term)
        _write_group(o_ref, slice(t * tq, (t + 1) * tq), jj, o, tq)

    _run_pipelined(items, scores, finish, (s0_ref, s1_ref))


def _swa_latent_attention(q, kd, vd, cache_kt, cache_vt, sink, *, j, seq):
    n, d = q.shape
    nb, _, lc = cache_kt.shape
    tq = 256
    span = tq + 2 * WINDOW
    npair = SWA_KV_HEADS // 2
    wq = d // npair
    parts = 2
    q_spec = pl.BlockSpec((seq // parts, wq), lambda b, p, i: (b * parts + i, p))
    c_spec = pl.BlockSpec((None, 2 * SWA_HD, lc), lambda b, p, i: (b, j * npair + p, 0))
    l_spec = pl.BlockSpec((seq, 2 * LANES), lambda b, p, i: (b, p))
    s_shape = pltpu.VMEM((SWA_GROUP * tq, lc + span), F32)
    return pl.pallas_call(
        functools.partial(_swa_latent_kernel, tq=tq, span=span),
        out_shape=jax.ShapeDtypeStruct((n, d), BF16),
        grid=(nb, npair, parts),
        in_specs=[pl.BlockSpec(memory_space=pltpu.SMEM), q_spec, c_spec, c_spec, l_spec, l_spec],
        out_specs=q_spec,
        scratch_shapes=[pltpu.VMEM((2, 2 * SWA_HD, lc), BF16), pltpu.VMEM((2, 2 * LANES, lc), BF16),
                        s_shape, s_shape],
        compiler_params=_params(3),
        name="swa_attn_latent",
    )(sink, q, cache_kt, cache_vt, kd, vd)


def _load_weights_as_bf16(jobs, stages, sems):
    order = []
    rings = {w: [] for w in stages}
    for src, dst in jobs:
        w = src.shape[1]
        slots = stages[w].shape[0]
        for k in range(src.shape[0] // WEIGHT_STAGE_ROWS):
            rows = pl.ds(k * WEIGHT_STAGE_ROWS, WEIGHT_STAGE_ROWS)
            slot = len(rings[w]) % slots
            copy = pltpu.make_async_copy(src.at[rows, :], stages[w].at[slot], sems[w].at[slot])
            order.append((w, len(rings[w])))
            rings[w].append((copy, slot, dst, rows))
    for w, ring in rings.items():
        for copy, _, _, _ in ring[:stages[w].shape[0]]:
            copy.start()
    for w, k in order:
        copy, slot, dst, rows = rings[w][k]
        copy.wait()
        dst[rows, :] = stages[w][slot].astype(BF16)
        ahead = k + stages[w].shape[0]
        if ahead < len(rings[w]):
            rings[w][ahead][0].start()


def _weight_stream(pieces, rings):
    per_ring = {name: [] for name in rings}
    plan = []
    for src, dst, idx, name in pieces:
        stage, sem = rings[name]
        slot = len(per_ring[name]) % stage.shape[0]
        view = stage.at[slot].at[0:src.shape[0], 0:src.shape[1]]
        plan.append((name, len(per_ring[name])))
        per_ring[name].append((pltpu.make_async_copy(src, view, sem.at[slot]), view, dst, idx))
    cursor = [0]

    def prime():
        for name, ring in per_ring.items():
            for copy, _, _, _ in ring[:rings[name][0].shape[0]]:
                copy.start()

    def take(n):
        for name, k in plan[cursor[0]:cursor[0] + n]:
            copy, view, dst, idx = per_ring[name][k]
            copy.wait()
            dst[idx] = view[...].astype(BF16)
            ahead = k + rings[name][0].shape[0]
            if ahead < len(per_ring[name]):
                per_ring[name][ahead][0].start()
        cursor[0] += n

    return prime, take


def _post_attn_ffn_kernel(*refs, layer, j, n_prompt_tiles, tiles_per_request, split_x, split_out,
                          overlap_first_tile):
    refs = list(refs)
    op_ref, os_ref = refs[:2]
    x_refs = refs[2:4] if split_x else refs[2:3]
    wo_hbm, wg_hbm, wu_hbm, wd_hbm, g_ref, mod_ref = refs[2 + len(x_refs):8 + len(x_refs)]
    n_out = 2 if split_out else 1
    out_refs = refs[8 + len(x_refs):8 + len(x_refs) + n_out]
    wo_ref, wg_ref, wu_ref, wd_ref, stage_row, stage_col, sem_row, sem_col = refs[8 + len(x_refs) + n_out:]
    d = wo_ref.shape[1]
    dff = wg_ref.shape[1]
    cw = FFN_CHUNK
    i = pl.program_id(0)
    is_prompt = i < n_prompt_tiles
    r = jnp.where(is_prompt, 0, 1 + (i - n_prompt_tiles) // tiles_per_request)

    def mod(slot):
        return mod_ref[pl.ds(r, 1), slot * d:(slot + 1) * d]

    def tile(before_out_proj, before_chunk):
        o = jnp.where(is_prompt, op_ref[...], os_ref[...])
        x = jnp.where(is_prompt, x_refs[0][...], x_refs[1][...]) if split_x else x_refs[0][...]
        before_out_proj()
        y = jnp.dot(o, wo_ref[...], preferred_element_type=F32)
        x = x + _rms(y, mod(2) * g_ref[1:2, :])
        h = (_rms(x, g_ref[2:3, :] * (1 + mod(4))) + mod(3)).astype(BF16)
        y = jnp.zeros((h.shape[0], d), F32)
        for c in range(dff // cw):
            before_chunk(c)
            cols = slice(c * cw, (c + 1) * cw)
            a = jnp.dot(h, wg_ref[:, cols], preferred_element_type=F32)
            u = jnp.dot(h, wu_ref[:, cols], preferred_element_type=F32)
            t = (a * jax.nn.sigmoid(a)) * u
            y = y + jnp.dot(t.astype(BF16), wd_ref[cols, :], preferred_element_type=F32)
        out = x + _rms(y, mod(5) * g_ref[3:4, :])
        if split_out:
            @pl.when(is_prompt)
            def _():
                out_refs[0][...] = out

            @pl.when(jnp.logical_not(is_prompt))
            def _():
                out_refs[1][...] = out
        else:
            out_refs[0][...] = out

    @pl.when(i == 0)
    def _():
        row_chunk = stage_row.shape[1]
        pieces = [(wo_hbm.at[j].at[pl.ds(k * row_chunk, row_chunk), :], wo_ref,
                   (pl.ds(k * row_chunk, row_chunk), slice(None)), "row") for k in range(d // row_chunk)]
        col_chunk = stage_col.shape[2]
        per_take = [d // row_chunk]
        for c in range(dff // cw):
            n = 0
            if (c * cw) % col_chunk == 0:
                cols = pl.ds(c * cw, min(col_chunk, dff - c * cw))
                pieces += [(wg_hbm.at[layer].at[:, cols], wg_ref, (slice(None), cols), "col"),
                           (wu_hbm.at[layer].at[:, cols], wu_ref, (slice(None), cols), "col")]
                n += 2
            rows = pl.ds(c * cw, cw)
            pieces.append((wd_hbm.at[layer].at[rows, :], wd_ref, (rows, slice(None)), "row"))
            per_take.append(n + 1)
        prime, take = _weight_stream(pieces, {"row": (stage_row, sem_row), "col": (stage_col, sem_col)})
        prime()
        if overlap_first_tile:
            tile(lambda: take(per_take[0]), lambda c: take(per_take[c + 1]))
        else:
            take(len(pieces))

    if overlap_first_tile:
        @pl.when(i > 0)
        def _():
            tile(lambda: None, lambda c: None)
    else:
        tile(lambda: None, lambda c: None)


def _post_attn_ffn(o_p, o_s, xs_in, w_o, wg, wu, wd, g, mods, *, layer, j, tiles_per_request, split_out):
    n_p, d = o_p.shape
    n_s = o_s.shape[0]
    tm = ROW_TILE
    tp, ts = n_p // tm, n_s // tm
    prompt_rows = pl.BlockSpec((tm, d), lambda i: (jnp.minimum(i, tp - 1), 0))
    latent_rows = pl.BlockSpec((tm, d), lambda i: (jnp.maximum(i - tp, 0), 0))
    all_rows = pl.BlockSpec((tm, d), lambda i: (i, 0))
    split_x = len(xs_in) == 2
    in_specs = [prompt_rows, latent_rows] + ([prompt_rows, latent_rows] if split_x else [all_rows])
    hbm = pl.BlockSpec(memory_space=pl.ANY)
    in_specs += [hbm, hbm, hbm, hbm, _layer_resident(g.shape, layer), _layer_resident(mods.shape, layer)]
    if split_out:
        out_shape = (jax.ShapeDtypeStruct((n_p, d), F32), jax.ShapeDtypeStruct((n_s, d), F32))
        out_specs = (prompt_rows, latent_rows)
    else:
        out_shape = jax.ShapeDtypeStruct((n_p + n_s, d), F32)
        out_specs = all_rows
    dff = wg.shape[2]
    scratch = [pltpu.VMEM((d, d), BF16), pltpu.VMEM((d, dff), BF16), pltpu.VMEM((d, dff), BF16),
               pltpu.VMEM((dff, d), BF16),
               pltpu.VMEM((WEIGHT_STAGE_SLOTS, FFN_CHUNK, d), F32),
               pltpu.VMEM((WEIGHT_COLUMN_SLOTS, d, 2 * FFN_CHUNK), F32),
               pltpu.SemaphoreType.DMA((WEIGHT_STAGE_SLOTS,)), pltpu.SemaphoreType.DMA((WEIGHT_COLUMN_SLOTS,))]
    return pl.pallas_call(
        functools.partial(_post_attn_ffn_kernel, layer=layer, j=j, n_prompt_tiles=tp,
                          tiles_per_request=tiles_per_request, split_x=split_x, split_out=split_out,
                          overlap_first_tile=not split_x),
        out_shape=out_shape,
        grid=(tp + ts,),
        in_specs=in_specs,
        out_specs=out_specs,
        scratch_shapes=scratch,
        compiler_params=pltpu.CompilerParams(dimension_semantics=("arbitrary",),
                                             vmem_limit_bytes=FFN_VMEM_LIMIT),
        name="post_attn_ffn",
    )(o_p, o_s, *xs_in, w_o, wg, wu, wd, g, mods)


def _rope_tables(n_lat):
    t = np.arange(n_lat)
    row = (t // GRID_W).astype(np.float32)
    col = (t % GRID_W).astype(np.float32)
    nf = ROT_DIM // 4
    inv = np.float32(ROPE_BASE) ** (-np.arange(nf, dtype=np.float32) / np.float32(nf))
    ar = row[:, None] * inv[None, :]
    ac = col[:, None] * inv[None, :]
    ang = np.concatenate([ar, ar, ac, ac], axis=-1)
    cos, sin = np.cos(ang), np.sin(ang)
    sign = np.where((np.arange(ROT_DIM) % 32) < 16, -1.0, 1.0).astype(np.float32)
    reps = LANES // ROT_DIM
    return jnp.asarray(np.tile(cos, (1, reps))), jnp.asarray(np.tile(sin * sign, (1, reps)))


def _swa_cache_to_feature_major(cache):
    nb, nl, lc, nh, hd = cache.shape
    return cache.transpose(0, 1, 3, 4, 2).reshape(nb, nl * nh * hd, lc)


def _swa_cache_from_feature_major(xt, seq):
    nb = xt.shape[0]
    return xt.reshape(nb, SWA_KV_HEADS, SWA_HD, seq).transpose(0, 3, 1, 2)


def kernel(x_prompt, x_sample, cache_diff_k, cache_diff_v, cache_swa_k, cache_swa_v, c, c_ctx,
           w_mod, b_mod, norm_g, w_qkv_diff, diff_lambda, diff_subln_g, w_o_diff,
           w_qkv_swa, swa_sink, w_o_swa, w_gate, w_up, w_down):
    bp, lp, d = x_prompt.shape
    bs, ls, _ = x_sample.shape
    lc = cache_diff_k.shape[2]
    depth = w_mod.shape[0]
    tm = ROW_TILE

    cond8 = jnp.concatenate([c_ctx[None, :], c, jnp.zeros((8 - 1 - bs, d), F32)], axis=0)
    mods = _modulation(cond8, w_mod, b_mod)
    tables = _rope_tables(ls)

    cdk = cache_diff_k.reshape(bs, -1, 2 * DIFF_HD)
    cdv = cache_diff_v.reshape(bs, -1, 2 * DIFF_HD)
    cskt = _swa_cache_to_feature_major(cache_swa_k)
    csvt = _swa_cache_to_feature_major(cache_swa_v)

    n_p, n_s = bp * lp, bs * ls
    x_parts = (x_prompt.reshape(n_p, d), x_sample.reshape(n_s, d))
    g = norm_g
    sub_g = diff_subln_g.reshape(-1, 1, 2 * DIFF_HD)
    diff_k_out, diff_v_out, swa_k_out, swa_v_out = [], [], [], []

    for i in range(depth):
        j = i // N_MIXERS
        if i % N_MIXERS == 0:
            lam_init = 0.8 - 0.6 * math.exp(-0.3 * i)
            qp, kp, vp, qs, ks, vs = _qkv_diff(x_parts, g, mods, w_qkv_diff, tables, layer=i, j=j,
                                               n_p=n_p, n_s=n_s, tiles_per_request=ls // QKV_ROW_TILE)
            op = _diff_prompt_attention(qp, kp, vp, diff_lambda, sub_g, j=j, seq=lp, lam_init=lam_init)
            diff_k_out.append(kp.reshape(bp, lp, DIFF_HEADS, 2 * DIFF_HD))
            diff_v_out.append(vp.reshape(bp, lp, DIFF_HEADS, 2 * DIFF_HD))
            os_ = _diff_latent_attention(qs, ks, vs, cdk, cdv, diff_lambda, sub_g,
                                         j=j, seq=ls, lc=lc, lam_init=lam_init)
            w_o = w_o_diff
        else:
            qp, ktp, vtp, qs, kds, vds = _qkv_swa(x_parts, g, mods, w_qkv_swa, tables, layer=i, j=j,
                                                  n_p=n_p, n_s=n_s, tiles_per_request=ls // QKV_ROW_TILE,
                                                  seq=lp)
            op = _swa_prompt_attention(qp, ktp, vtp, swa_sink[j], seq=lp)
            swa_k_out.append(_swa_cache_from_feature_major(ktp, lp))
            swa_v_out.append(_swa_cache_from_feature_major(vtp, lp))
            os_ = _swa_latent_attention(qs, kds, vds, cskt, csvt, swa_sink[j], j=j, seq=ls)
            w_o = w_o_swa
        last = i == depth - 1
        out = _post_attn_ffn(op, os_, x_parts, w_o, w_gate, w_up, w_down, g, mods, layer=i, j=j,
                             tiles_per_request=ls // tm, split_out=last)
        x_parts = out if last else (out,)
    xp, xs = x_parts

    return (xp.reshape(bp, lp, d), xs.reshape(bs, ls, d),
            jnp.stack(diff_k_out, axis=1), jnp.stack(diff_v_out, axis=1),
            jnp.stack(swa_k_out, axis=1), jnp.stack(swa_v_out, axis=1))
```

```python
import functools
import math

import jax
import jax.numpy as jnp
import numpy as np
from jax import lax
from jax.experimental import pallas as pl
from jax.experimental.pallas import tpu as pltpu

F32 = jnp.float32
BF16 = jnp.bfloat16

GRID_W = 64
N_MIXERS = 2
DIFF_HEADS = 8
DIFF_HD = 64
SWA_HEADS = 16
SWA_KV_HEADS = 4
SWA_GROUP = SWA_HEADS // SWA_KV_HEADS
SWA_HD = 64
ROT_DIM = 64
WINDOW = 128
ROPE_BASE = 10000.0
EPS = 1e-6
NEG_INF = -1e30

LANES = 128
ROW_TILE = 512
QKV_ROW_TILE = 512
PROMPT_SCORE_BUFFERS = 8
LATENT_SCORE_BUFFERS = 2
VMEM_LIMIT = 48 * 1024 * 1024
FFN_VMEM_LIMIT = 58 * 1024 * 1024
WEIGHT_STAGE_ROWS = 128
WEIGHT_STAGE_SLOTS = 3
WEIGHT_COLUMN_SLOTS = 2
QKV_STAGE_SLOTS = 4
FFN_CHUNK = 256
NT_DIMS = (((1,), (1,)), ((), ()))
LOG2E = math.log2(math.e)


def _params(n_axes):
    return pltpu.CompilerParams(dimension_semantics=("arbitrary",) * n_axes,
                                vmem_limit_bytes=VMEM_LIMIT)


def _layer_resident(shape, layer):
    return pl.BlockSpec((None,) + tuple(shape[1:]), lambda *_: (layer,) + (0,) * (len(shape) - 1),
                        pipeline_mode=pl.Buffered(1))


def _rms(x, g):
    ms = jnp.mean(x * x, axis=-1, keepdims=True)
    return (x * lax.rsqrt(ms + EPS)) * g


def _half_masks(dtype):
    lane = lax.broadcasted_iota(jnp.int32, (1, LANES), 1)
    lo = lane < (LANES // 2)
    return jnp.where(lo, 1.0, 0.0).astype(dtype), jnp.where(lo, 0.0, 1.0).astype(dtype)


def _lo64():
    return lax.broadcasted_iota(jnp.int32, (1, LANES), 1) < (LANES // 2)


def _ones_column(rows):
    return jnp.ones((rows, LANES), BF16)


def _ones_row(cols):
    return jnp.ones((LANES, cols), BF16)


def _mod_kernel(cond_ref, w_ref, b_ref, out_ref):
    c = cond_ref[...]
    s = c * jax.nn.sigmoid(c)
    out_ref[...] = jnp.dot(s.astype(BF16), w_ref[...].astype(BF16),
                           preferred_element_type=F32) + b_ref[pl.ds(pl.program_id(0), 1), :]


def _modulation(cond8, w_mod, b_mod):
    depth, d, n = w_mod.shape
    tn = 1536
    return pl.pallas_call(
        _mod_kernel,
        out_shape=jax.ShapeDtypeStruct((depth, 8, n), F32),
        grid=(depth, n // tn),
        in_specs=[pl.BlockSpec((8, d), lambda i, j: (0, 0)),
                  pl.BlockSpec((None, d, tn), lambda i, j: (i, 0, j)),
                  pl.BlockSpec((depth, tn), lambda i, j: (0, j))],
        out_specs=pl.BlockSpec((None, 8, tn), lambda i, j: (i, 0, j)),
        compiler_params=_params(2),
        name="modulation",
    )(cond8, w_mod, b_mod)


def _rope_slab(xs, cos, sin_signed, lo16):
    left = pltpu.roll(xs, LANES - 16, 1)
    right = pltpu.roll(xs, 16, 1)
    return xs * cos + jnp.where(lo16, left, right) * sin_signed


def _lo16_mask():
    lane = lax.broadcasted_iota(jnp.int32, (1, LANES), 1)
    return (lane % 32) < 16


def _qkv_both_groups(refs, n_prompt_tiles, tiles_per_request, split_x, j, prompt_tile, latent_tile):
    refs = list(refs)
    x_refs = refs[:2] if split_x else refs[:1]
    g_ref, mod_ref, w_hbm, cos_ref, sin_ref = refs[len(x_refs):len(x_refs) + 5]
    outs = refs[len(x_refs) + 5:len(x_refs) + 11]
    w_vmem, stage, sem = refs[len(x_refs) + 11:]
    d = x_refs[0].shape[1]
    i = pl.program_id(0)

    @pl.when(i == 0)
    def _():
        _load_weights_as_bf16([(w_hbm.at[j], w_vmem)], {w_vmem.shape[1]: stage}, {w_vmem.shape[1]: sem})

    def pre_norm(x, r):
        shift, scale = mod_ref[pl.ds(r, 1), 0:d], mod_ref[pl.ds(r, 1), d:2 * d]
        return (_rms(x, g_ref[0:1, :] * (1 + scale)) + shift).astype(BF16)

    @pl.when(i < n_prompt_tiles)
    def _():
        prompt_tile(pre_norm(x_refs[0][...], 0), w_vmem, outs[:3])

    @pl.when(i >= n_prompt_tiles)
    def _():
        r = 1 + (i - n_prompt_tiles) // tiles_per_request
        latent_tile(pre_norm(x_refs[-1][...], r), w_vmem, outs[3:], (cos_ref[...], sin_ref[...], _lo16_mask()))


def _qkv_call(kernel, name, x_parts, g, mods, w, tables, prompt_outs, latent_outs, *, layer, n_p, n_s):
    d = x_parts[0].shape[1]
    tm = QKV_ROW_TILE
    tp, ts = n_p // tm, n_s // tm
    nt = tables[0].shape[0] // tm

    def prompt_block(shape):
        return pl.BlockSpec(shape, lambda i: (jnp.minimum(i, tp - 1),) + (0,) * (len(shape) - 1))

    def latent_block(shape):
        return pl.BlockSpec(shape, lambda i: (jnp.maximum(i - tp, 0),) + (0,) * (len(shape) - 1))

    if len(x_parts) == 2:
        x_specs = [prompt_block((tm, d)), latent_block((tm, d))]
    else:
        x_specs = [pl.BlockSpec((tm, d), lambda i: (i, 0))]
    table_spec = pl.BlockSpec((tm, LANES), lambda i: (jnp.maximum(i - tp, 0) % nt, 0))
    in_specs = x_specs + [_layer_resident(g.shape, layer), _layer_resident(mods.shape, layer),
                          pl.BlockSpec(memory_space=pl.ANY), table_spec, table_spec]
    out_shape, out_specs = [], []
    for outs, n, block in ((prompt_outs, n_p, prompt_block), (latent_outs, n_s, latent_block)):
        for shape_of, dtype in outs:
            out_shape.append(jax.ShapeDtypeStruct(shape_of(n), dtype))
            out_specs.append(block(shape_of(tm)))
    cols = w.shape[2]
    scratch = [pltpu.VMEM((d, cols), BF16), pltpu.VMEM((QKV_STAGE_SLOTS, WEIGHT_STAGE_ROWS, cols), F32),
               pltpu.SemaphoreType.DMA((QKV_STAGE_SLOTS,))]
    return pl.pallas_call(
        kernel,
        out_shape=tuple(out_shape),
        grid=(tp + ts,),
        in_specs=in_specs,
        out_specs=tuple(out_specs),
        scratch_shapes=scratch,
        compiler_params=_params(1),
        name=name,
    )(*x_parts, g, mods, w, *tables)


def _diff_tile(h, w_ref, outs, rope=None):
    q_ref, k_ref, v_ref = outs
    d = h.shape[1]
    cw = 512
    for c in range(3 * d // cw):
        acc = jnp.dot(h, w_ref[:, c * cw:(c + 1) * cw], preferred_element_type=F32)
        which, off = divmod(c * cw, d)
        dst = outs[which]
        for s in range(cw // LANES):
            xs = acc[:, s * LANES:(s + 1) * LANES]
            if rope is not None and which < 2:
                xs = _rope_slab(xs, *rope)
            if which == 0:
                xs = xs * (DIFF_HD ** -0.5 * LOG2E)
            lo = off + s * LANES
            if rope is not None or which == 0:
                dst[:, lo:lo + LANES] = xs.astype(dst.dtype)
            else:
                dst[pl.ds(lo // LANES, h.shape[0], stride=DIFF_HEADS), :] = xs


def _qkv_diff_kernel(*refs, j, n_prompt_tiles, tiles_per_request, split_x):
    _qkv_both_groups(refs, n_prompt_tiles, tiles_per_request, split_x, j, _diff_tile, _diff_tile)


def _qkv_diff(x_parts, g, mods, w, tables, *, layer, j, n_p, n_s, tiles_per_request):
    d = x_parts[0].shape[1]

    def rows(n):
        return (n, d)

    def cache(n):
        return (n * DIFF_HEADS, d // DIFF_HEADS)

    kernel = functools.partial(_qkv_diff_kernel, j=j, n_prompt_tiles=n_p // QKV_ROW_TILE,
                               tiles_per_request=tiles_per_request, split_x=len(x_parts) == 2)
    return _qkv_call(kernel, "qkv_diff", x_parts, g, mods, w, tables,
                     [(rows, BF16), (cache, F32), (cache, F32)], [(rows, BF16)] * 3,
                     layer=layer, n_p=n_p, n_s=n_s)


def _swa_q_tile(h, w_ref, q_ref, rope):
    d = h.shape[1]
    cw = 512
    for c in range(d // cw):
        acc = jnp.dot(h, w_ref[:, c * cw:(c + 1) * cw], preferred_element_type=F32)
        for s in range(cw // LANES):
            xs = acc[:, s * LANES:(s + 1) * LANES]
            if rope is not None:
                xs = _rope_slab(xs, *rope)
            lo = c * cw + s * LANES
            q_ref[:, lo:lo + LANES] = (xs * (SWA_HD ** -0.5 * LOG2E)).astype(BF16)
    nkv = SWA_KV_HEADS * SWA_HD
    return jnp.dot(h, w_ref[:, d:d + 2 * nkv], preferred_element_type=F32), nkv


def _swa_prompt_tile(h, w_ref, outs, *, seq):
    q_ref, kt_ref, vt_ref = outs
    kv, nkv = _swa_q_tile(h, w_ref, q_ref, None)
    for b in range(h.shape[0] // seq):
        kt_ref[b] = kv[b * seq:(b + 1) * seq, :nkv].T
        vt_ref[b] = kv[b * seq:(b + 1) * seq, nkv:].T


def _swa_latent_tile(h, w_ref, outs, rope):
    q_ref, kd_ref, vd_ref = outs
    kv, nkv = _swa_q_tile(h, w_ref, q_ref, rope)
    lo64 = _lo64()
    for which, dst in enumerate((kd_ref, vd_ref)):
        for s in range(nkv // LANES):
            xs = kv[:, which * nkv + s * LANES: which * nkv + (s + 1) * LANES]
            if which == 0:
                xs = _rope_slab(xs, *rope)
            sw = pltpu.roll(xs, LANES // 2, 1)
            dst[:, (2 * s) * LANES:(2 * s + 1) * LANES] = jnp.where(lo64, xs, sw).astype(BF16)
            dst[:, (2 * s + 1) * LANES:(2 * s + 2) * LANES] = jnp.where(lo64, sw, xs).astype(BF16)


def _qkv_swa_kernel(*refs, j, n_prompt_tiles, tiles_per_request, split_x, seq):
    _qkv_both_groups(refs, n_prompt_tiles, tiles_per_request, split_x, j,
                     functools.partial(_swa_prompt_tile, seq=seq), _swa_latent_tile)


def _qkv_swa(x_parts, g, mods, w, tables, *, layer, j, n_p, n_s, tiles_per_request, seq):
    d = x_parts[0].shape[1]
    nkv = SWA_KV_HEADS * SWA_HD

    def rows(n):
        return (n, d)

    def feature_major(n):
        return (n // seq, nkv, seq)

    def duplicated(n):
        return (n, 2 * nkv)

    kernel = functools.partial(_qkv_swa_kernel, j=j, n_prompt_tiles=n_p // QKV_ROW_TILE,
                               tiles_per_request=tiles_per_request, split_x=len(x_parts) == 2, seq=seq)
    return _qkv_call(kernel, "qkv_swa", x_parts, g, mods, w, tables,
                     [(rows, BF16), (feature_major, F32), (feature_major, F32)],
                     [(rows, BF16), (duplicated, BF16), (duplicated, BF16)],
                     layer=layer, n_p=n_p, n_s=n_s)


def _diff_lambda(lam_ref, lam_init):
    lp = lam_ref[...]
    a = jnp.sum(lp[0:1] * lp[1:2], axis=-1, keepdims=True)
    b = jnp.sum(lp[2:3] * lp[3:4], axis=-1, keepdims=True)
    return jnp.exp(a) - jnp.exp(b) + lam_init


def _diff_combine(acc, tq, lam, g, lam_init):
    o12 = acc[:, :LANES] / acc[:, LANES:]
    o = o12[:tq] - lam * o12[tq:]
    return _rms(o, g) * (1.0 - lam_init)


def _stack_maps(q):
    m_lo, m_hi = _half_masks(BF16)
    return jnp.concatenate([q * m_lo, q * m_hi], axis=0)


def _run_pipelined(items, scores, finish, s_bufs):
    depth = len(s_bufs)
    states = {i: scores(items[i], s_bufs[i]) for i in range(min(depth - 1, len(items)))}
    for i, item in enumerate(items):
        ahead = i + depth - 1
        if ahead < len(items):
            states[ahead] = scores(items[ahead], s_bufs[ahead % depth])
        finish(item, s_bufs[i % depth], states.pop(i))


def _store_scores(s_ref, col0, s, mrun):
    s_ref[:, col0:col0 + s.shape[1]] = s
    for t in range(s.shape[1] // LANES):
        blk = s[:, t * LANES:(t + 1) * LANES]
        mrun = blk if mrun is None else jnp.maximum(mrun, blk)
    return mrun


def _exp_block(s_ref, col0, width, mb):
    return jnp.concatenate(
        [jnp.exp2(s_ref[:, col0 + t * LANES:col0 + (t + 1) * LANES] - mb).astype(BF16)
         for t in range(width // LANES)], axis=1)


def _diff_prompt_kernel(q_ref, k_ref, v_ref, lam_ref, g_ref, o_ref, *s_bufs, lam_init, seq):
    lam = _diff_lambda(lam_ref, lam_init)
    g = g_ref[...]
    ones = _ones_column(seq)
    items = [(r, h) for r in range(q_ref.shape[0] // seq) for h in range(DIFF_HEADS)]

    def head_rows(ref, r, h):
        return ref[pl.ds(r * seq * DIFF_HEADS + h, seq, stride=DIFF_HEADS), :]

    def scores(item, s_ref):
        r, h = item
        rows, sl = slice(r * seq, (r + 1) * seq), slice(h * LANES, (h + 1) * LANES)
        s = lax.dot_general(_stack_maps(q_ref[rows, sl]), head_rows(k_ref, r, h).astype(BF16), NT_DIMS,
                            preferred_element_type=F32)
        return _store_scores(s_ref, 0, s, None)

    def finish(item, s_ref, mrun):
        r, h = item
        rows, sl = slice(r * seq, (r + 1) * seq), slice(h * LANES, (h + 1) * LANES)
        mb = jnp.broadcast_to(jnp.max(mrun, axis=-1, keepdims=True), (2 * seq, LANES))
        vx = jnp.concatenate([head_rows(v_ref, r, h).astype(BF16), ones], axis=1)
        acc = jnp.dot(_exp_block(s_ref, 0, seq, mb), vx, preferred_element_type=F32)
        o_ref[rows, sl] = _diff_combine(acc, seq, lam, g, lam_init).astype(BF16)

    _run_pipelined(items, scores, finish, s_bufs)


def _diff_prompt_attention(q, k, v, lam_params, subln_g, *, j, seq, lam_init):
    n, d = q.shape
    req = 2
    spec = pl.BlockSpec((req * seq, d), lambda b: (b, 0))
    kv_spec = pl.BlockSpec((req * seq * DIFF_HEADS, d // DIFF_HEADS), lambda b: (b, 0))
    return pl.pallas_call(
        functools.partial(_diff_prompt_kernel, lam_init=lam_init, seq=seq),
        out_shape=jax.ShapeDtypeStruct((n, d), BF16),
        grid=(n // (req * seq),),
        in_specs=[spec, kv_spec, kv_spec, _layer_resident(lam_params.shape, j),
                  _layer_resident(subln_g.shape, j)],
        out_specs=spec,
        scratch_shapes=[pltpu.VMEM((2 * seq, seq), F32)] * PROMPT_SCORE_BUFFERS,
        compiler_params=_params(1),
        name="diff_attn_prompt",
    )(q, k, v, lam_params, subln_g)


def _diff_latent_kernel(q_ref, kc_ref, vc_ref, kl_ref, vl_ref, qn_ref, kcn_ref, kln_ref, lam_ref, g_ref, o_ref,
                        kk_ref, vx_ref, m_ref, s0_ref, s1_ref, *, lam_init, lc, tq, key_chunk):
    seq = q_ref.shape[0]
    nk = kk_ref.shape[0]
    head = pl.program_id(1)
    step = pl.program_id(0) * pl.num_programs(1) + head
    last_step = pl.num_programs(0) * pl.num_programs(1) - 1
    next_head = jnp.minimum(step + 1, last_step) % DIFF_HEADS
    masks = _half_masks(BF16)
    nchunk = nk // key_chunk
    s_bufs = (s0_ref, s1_ref)

    def fill_keys(cache_ref, h, latent_ref):
        kk_ref[0:lc, :] = cache_ref[pl.ds(h, lc, stride=DIFF_HEADS), :].astype(BF16)
        kk_ref[lc:, :] = latent_ref[...]

    def scores(q_rows, m, s_ref):
        q = q_rows * masks[m]
        mrun = None
        for c in range(nchunk):
            s = lax.dot_general(q, kk_ref[c * key_chunk:(c + 1) * key_chunk, :], NT_DIMS,
                                preferred_element_type=F32)
            mrun = _store_scores(s_ref, c * key_chunk, s, mrun)
        return mrun

    @pl.when(step == 0)
    def _():
        fill_keys(kc_ref, head, kl_ref)
        m_ref[...] = scores(q_ref[0:tq, :], 0, s0_ref)

    vx_ref[0:lc, 0:LANES] = vc_ref[pl.ds(head, lc, stride=DIFF_HEADS), :].astype(BF16)
    vx_ref[lc:, 0:LANES] = vl_ref[...]
    vx_ref[:, LANES:2 * LANES] = _ones_column(nk)

    items = [(rt, m) for rt in range(seq // tq) for m in range(2)]
    lam = _diff_lambda(lam_ref, lam_init)
    g = g_ref[...]
    first_map = {}

    def finish(item, s_ref, mrun):
        rt, m = item
        mb = jnp.broadcast_to(jnp.max(mrun, axis=-1, keepdims=True), (tq, LANES))
        acc = None
        for c in range(nchunk):
            part = jnp.dot(_exp_block(s_ref, c * key_chunk, key_chunk, mb),
                           vx_ref[c * key_chunk:(c + 1) * key_chunk, :], preferred_element_type=F32)
            acc = part if acc is None else acc + part
        o_m = acc[:, :LANES] / acc[:, LANES:]
        if m == 0:
            first_map[rt] = o_m
        else:
            o = _rms(first_map.pop(rt) - lam * o_m, g) * (1.0 - lam_init)
            o_ref[rt * tq:(rt + 1) * tq, :] = o.astype(BF16)

    mrun = m_ref[...]
    for i, item in enumerate(items):
        s_ahead = s_bufs[(i + 1) % 2]
        if i + 1 < len(items):
            rt, m = items[i + 1]
            ahead = scores(q_ref[rt * tq:(rt + 1) * tq, :], m, s_ahead)
        else:
            fill_keys(kcn_ref, next_head, kln_ref)
            ahead = scores(qn_ref[...], 0, s_ahead)
            m_ref[...] = ahead
        finish(item, s_bufs[i % 2], mrun)
        mrun = ahead


def _diff_latent_attention(q, k, v, cache_k, cache_v, lam_params, subln_g, *, j, seq, lc, lam_init):
    n, d = q.shape
    nb = cache_k.shape[0]
    tq = 512
    key_chunk = 512
    last_step = nb * DIFF_HEADS - 1

    def next_step(b, h):
        return divmod(jnp.minimum(b * DIFF_HEADS + h + 1, last_step), DIFF_HEADS)

    def next_query_tile(b, h):
        nb_, nh = next_step(b, h)
        return nb_ * (seq // tq), nh

    q_spec = pl.BlockSpec((seq, LANES), lambda b, h: (b, h))
    c_spec = pl.BlockSpec((None, lc * DIFF_HEADS, LANES), lambda b, h: (b, j, 0))
    qn_spec = pl.BlockSpec((tq, LANES), next_query_tile)
    kln_spec = pl.BlockSpec((seq, LANES), next_step)
    cn_spec = pl.BlockSpec((None, lc * DIFF_HEADS, LANES), lambda b, h: (next_step(b, h)[0], j, 0))
    return pl.pallas_call(
        functools.partial(_diff_latent_kernel, lam_init=lam_init, lc=lc, tq=tq, key_chunk=key_chunk),
        out_shape=jax.ShapeDtypeStruct((n, d), BF16),
        grid=(nb, DIFF_HEADS),
        in_specs=[q_spec, c_spec, c_spec, q_spec, q_spec, qn_spec, cn_spec, kln_spec,
                  _layer_resident(lam_params.shape, j), _layer_resident(subln_g.shape, j)],
        out_specs=q_spec,
        scratch_shapes=[pltpu.VMEM((lc + seq, LANES), BF16),
                        pltpu.VMEM((lc + seq, 2 * LANES), BF16),
                        pltpu.VMEM((tq, LANES), F32),
                        *[pltpu.VMEM((tq, lc + seq), F32)] * LATENT_SCORE_BUFFERS],
        compiler_params=_params(2),
        name="diff_attn_latent",
    )(q, cache_k, cache_v, k, v, q, cache_k, k, lam_params, subln_g)


def _stack_group(q_ref, rows, kv_local):
    m_lo, m_hi = _half_masks(BF16)
    parts = []
    for gb in range(SWA_GROUP // 2):
        blk = kv_local * (SWA_GROUP // 2) + gb
        qb = q_ref[rows, blk * LANES:(blk + 1) * LANES]
        parts += [qb * m_lo, qb * m_hi]
    return jnp.concatenate(parts, axis=0)


def _sink_column(sink_ref, first_head, tq):
    return jnp.concatenate([jnp.full((tq, LANES), sink_ref[first_head + g] * LOG2E, F32)
                            for g in range(SWA_GROUP)], axis=0)


def _write_group(o_ref, rows, kv_local, o, tq):
    lo64 = _lo64()
    for gb in range(SWA_GROUP // 2):
        blk = kv_local * (SWA_GROUP // 2) + gb
        even = o[(2 * gb) * tq:(2 * gb + 1) * tq]
        odd = o[(2 * gb + 1) * tq:(2 * gb + 2) * tq]
        o_ref[rows, blk * LANES:(blk + 1) * LANES] = jnp.where(lo64, even, odd).astype(BF16)


def _dup_rows(x_t):
    xb = x_t.astype(BF16)
    return jnp.concatenate([xb, xb], axis=0)


def _sink_finish(mrun, sk, rows):
    mb = jnp.maximum(jnp.broadcast_to(jnp.max(mrun, axis=-1, keepdims=True), (rows, LANES)), sk)
    return mb, jnp.exp2(sk - mb)


def _swa_prompt_kernel(sink_ref, q_ref, kt_ref, vt_ref, o_ref, *s_bufs):
    seq = kt_ref.shape[2]
    rows = SWA_GROUP * seq
    ones = _ones_row(seq)
    items = [(r, j) for r in range(kt_ref.shape[0]) for j in range(SWA_KV_HEADS)]

    def scores(item, s_ref):
        r, j = item
        kd = _dup_rows(kt_ref[r, j * SWA_HD:(j + 1) * SWA_HD, :])
        s = jnp.dot(_stack_group(q_ref, slice(r * seq, (r + 1) * seq), j), kd, preferred_element_type=F32)
        return _store_scores(s_ref, 0, s, None)

    def finish(item, s_ref, mrun):
        r, j = item
        sk = _sink_column(sink_ref, j * SWA_GROUP, seq)
        mb, sink_term = _sink_finish(mrun, sk, rows)
        vx = jnp.concatenate([_dup_rows(vt_ref[r, j * SWA_HD:(j + 1) * SWA_HD, :]), ones], axis=0)
        acc = lax.dot_general(_exp_block(s_ref, 0, seq, mb), vx, NT_DIMS, preferred_element_type=F32)
        o = acc[:, :LANES] / (acc[:, LANES:] + sink_term)
        _write_group(o_ref, slice(r * seq, (r + 1) * seq), j, o, seq)

    _run_pipelined(items, scores, finish, s_bufs)


def _swa_prompt_attention(q, kt, vt, sink, *, seq):
    n, d = q.shape
    nkv = kt.shape[1]
    req = 2
    t_spec = pl.BlockSpec((req, nkv, seq), lambda b: (b, 0, 0))
    return pl.pallas_call(
        _swa_prompt_kernel,
        out_shape=jax.ShapeDtypeStruct((n, d), BF16),
        grid=(n // (req * seq),),
        in_specs=[pl.BlockSpec(memory_space=pltpu.SMEM),
                  pl.BlockSpec((req * seq, d), lambda b: (b, 0)), t_spec, t_spec],
        out_specs=pl.BlockSpec((req * seq, d), lambda b: (b, 0)),
        scratch_shapes=[pltpu.VMEM((SWA_GROUP * seq, seq), F32)] * (PROMPT_SCORE_BUFFERS // 2),
        compiler_params=_params(1),
        name="swa_attn_prompt",
    )(sink, q, kt, vt)


def _swa_latent_kernel(sink_ref, q_ref, kc_ref, vc_ref, kl_ref, vl_ref, o_ref, kcd_ref, vcx_ref,
                       s0_ref, s1_ref, *, tq, span):
    pair = pl.program_id(1)
    tiles = q_ref.shape[0] // tq
    first_tile = pl.program_id(2) * tiles
    seq = kl_ref.shape[0]
    lc = kc_ref.shape[1]
    rows = SWA_GROUP * tq
    ones_row = _ones_row(lc)
    for jj in range(2):
        kcd_ref[jj] = _dup_rows(kc_ref[jj * SWA_HD:(jj + 1) * SWA_HD, :])
        vcx_ref[jj] = jnp.concatenate([_dup_rows(vc_ref[jj * SWA_HD:(jj + 1) * SWA_HD, :]), ones_row], axis=0)
    ones_col = _ones_column(span)
    items = [(t, jj) for t in range(tiles) for jj in range(2)]
    windows, biases = {}, {}

    def window(t):
        if t not in windows:
            q0 = (first_tile + t) * tq
            windows[t] = (q0, pl.multiple_of(jnp.clip(q0 - WINDOW, 0, seq - span), WINDOW))
        return windows[t]

    def bias_for(t):
        if t not in biases:
            q0, ws = window(t)
            q_pos = q0 + lax.broadcasted_iota(jnp.int32, (tq, span), 0)
            k_pos = ws + lax.broadcasted_iota(jnp.int32, (tq, span), 1)
            b = jnp.where(jnp.abs(q_pos - k_pos) <= WINDOW, 0.0, NEG_INF).astype(F32)
            biases[t] = jnp.concatenate([b] * SWA_GROUP, axis=0)
        return biases[t]

    def scores(item, s_ref):
        t, jj = item
        _, ws = window(t)
        qs = _stack_group(q_ref, slice(t * tq, (t + 1) * tq), jj)
        s_c = jnp.dot(qs, kcd_ref[jj], preferred_element_type=F32)
        mrun = _store_scores(s_ref, 0, s_c, None)
        s_w = lax.dot_general(qs, kl_ref[pl.ds(ws, span), jj * LANES:(jj + 1) * LANES], NT_DIMS,
                              preferred_element_type=F32) + bias_for(t)
        return _store_scores(s_ref, lc, s_w, mrun)

    def finish(item, s_ref, mrun):
        t, jj = item
        _, ws = window(t)
        sk = _sink_column(sink_ref, (2 * pair + jj) * SWA_GROUP, tq)
        mb, sink_term = _sink_finish(mrun, sk, rows)
        vwx = jnp.concatenate([vl_ref[pl.ds(ws, span), jj * LANES:(jj + 1) * LANES], ones_col], axis=1)
        acc = (lax.dot_general(_exp_block(s_ref, 0, lc, mb), vcx_ref[jj], NT_DIMS, preferred_element_type=F32)
               + jnp.dot(_exp_block(s_ref, lc, span, mb), vwx, preferred_element_type=F32))
        o = acc[:, :LANES] / (acc[:, LANES:] + sink_term)
        _write_group(o_ref, slice(t * tq, (t + 1) * tq), jj, o, tq)

    _run_pipelined(items, scores, finish, (s0_ref, s1_ref))


def _swa_latent_attention(q, kd, vd, cache_kt, cache_vt, sink, *, j, seq):
    n, d = q.shape
    nb, _, lc = cache_kt.shape
    tq = 256
    span = tq + 2 * WINDOW
    npair = SWA_KV_HEADS // 2
    wq = d // npair
    parts = 2
    q_spec = pl.BlockSpec((seq // parts, wq), lambda b, p, i: (b * parts + i, p))
    c_spec = pl.BlockSpec((None, 2 * SWA_HD, lc), lambda b, p, i: (b, j * npair + p, 0))
    l_spec = pl.BlockSpec((seq, 2 * LANES), lambda b, p, i: (b, p))
    s_shape = pltpu.VMEM((SWA_GROUP * tq, lc + span), F32)
    return pl.pallas_call(
        functools.partial(_swa_latent_kernel, tq=tq, span=span),
        out_shape=jax.ShapeDtypeStruct((n, d), BF16),
        grid=(nb, npair, parts),
        in_specs=[pl.BlockSpec(memory_space=pltpu.SMEM), q_spec, c_spec, c_spec, l_spec, l_spec],
        out_specs=q_spec,
        scratch_shapes=[pltpu.VMEM((2, 2 * SWA_HD, lc), BF16), pltpu.VMEM((2, 2 * LANES, lc), BF16),
                        s_shape, s_shape],
        compiler_params=_params(3),
        name="swa_attn_latent",
    )(sink, q, cache_kt, cache_vt, kd, vd)


def _load_weights_as_bf16(jobs, stages, sems):
    order = []
    rings = {w: [] for w in stages}
    for src, dst in jobs:
        w = src.shape[1]
        slots = stages[w].shape[0]
        for k in range(src.shape[0] // WEIGHT_STAGE_ROWS):
            rows = pl.ds(k * WEIGHT_STAGE_ROWS, WEIGHT_STAGE_ROWS)
            slot = len(rings[w]) % slots
            copy = pltpu.make_async_copy(src.at[rows, :], stages[w].at[slot], sems[w].at[slot])
            order.append((w, len(rings[w])))
            rings[w].append((copy, slot, dst, rows))
    for w, ring in rings.items():
        for copy, _, _, _ in ring[:stages[w].shape[0]]:
            copy.start()
    for w, k in order:
        copy, slot, dst, rows = rings[w][k]
        copy.wait()
        dst[rows, :] = stages[w][slot].astype(BF16)
        ahead = k + stages[w].shape[0]
        if ahead < len(rings[w]):
            rings[w][ahead][0].start()


def _weight_stream(pieces, rings):
    per_ring = {name: [] for name in rings}
    plan = []
    for src, dst, idx, name in pieces:
        stage, sem = rings[name]
        slot = len(per_ring[name]) % stage.shape[0]
        view = stage.at[slot].at[0:src.shape[0], 0:src.shape[1]]
        plan.append((name, len(per_ring[name])))
        per_ring[name].append((pltpu.make_async_copy(src, view, sem.at[slot]), view, dst, idx))
    cursor = [0]

    def prime():
        for name, ring in per_ring.items():
            for copy, _, _, _ in ring[:rings[name][0].shape[0]]:
                copy.start()

    def take(n):
        for name, k in plan[cursor[0]:cursor[0] + n]:
            copy, view, dst, idx = per_ring[name][k]
            copy.wait()
            dst[idx] = view[...].astype(BF16)
            ahead = k + rings[name][0].shape[0]
            if ahead < len(per_ring[name]):
                per_ring[name][ahead][0].start()
        cursor[0] += n

    return prime, take


def _post_attn_ffn_kernel(*refs, layer, j, n_prompt_tiles, tiles_per_request, split_x, split_out,
                          overlap_first_tile):
    refs = list(refs)
    op_ref, os_ref = refs[:2]
    x_refs = refs[2:4] if split_x else refs[2:3]
    wo_hbm, wg_hbm, wu_hbm, wd_hbm, g_ref, mod_ref = refs[2 + len(x_refs):8 + len(x_refs)]
    n_out = 2 if split_out else 1
    out_refs = refs[8 + len(x_refs):8 + len(x_refs) + n_out]
    wo_ref, wg_ref, wu_ref, wd_ref, stage_row, stage_col, sem_row, sem_col = refs[8 + len(x_refs) + n_out:]
    d = wo_ref.shape[1]
    dff = wg_ref.shape[1]
    cw = FFN_CHUNK
    i = pl.program_id(0)
    is_prompt = i < n_prompt_tiles
    r = jnp.where(is_prompt, 0, 1 + (i - n_prompt_tiles) // tiles_per_request)

    def mod(slot):
        return mod_ref[pl.ds(r, 1), slot * d:(slot + 1) * d]

    def tile(before_out_proj, before_chunk):
        o = jnp.where(is_prompt, op_ref[...], os_ref[...])
        x = jnp.where(is_prompt, x_refs[0][...], x_refs[1][...]) if split_x else x_refs[0][...]
        before_out_proj()
        y = jnp.dot(o, wo_ref[...], preferred_element_type=F32)
        x = x + _rms(y, mod(2) * g_ref[1:2, :])
        h = (_rms(x, g_ref[2:3, :] * (1 + mod(4))) + mod(3)).astype(BF16)
        y = jnp.zeros((h.shape[0], d), F32)
        for c in range(dff // cw):
            before_chunk(c)
            cols = slice(c * cw, (c + 1) * cw)
            a = jnp.dot(h, wg_ref[:, cols], preferred_element_type=F32)
            u = jnp.dot(h, wu_ref[:, cols], preferred_element_type=F32)
            t = (a * jax.nn.sigmoid(a)) * u
            y = y + jnp.dot(t.astype(BF16), wd_ref[cols, :], preferred_element_type=F32)
        out = x + _rms(y, mod(5) * g_ref[3:4, :])
        if split_out:
            @pl.when(is_prompt)
            def _():
                out_refs[0][...] = out

            @pl.when(jnp.logical_not(is_prompt))
            def _():
                out_refs[1][...] = out
        else:
            out_refs[0][...] = out

    @pl.when(i == 0)
    def _():
        row_chunk = stage_row.shape[1]
        pieces = [(wo_hbm.at[j].at[pl.ds(k * row_chunk, row_chunk), :], wo_ref,
                   (pl.ds(k * row_chunk, row_chunk), slice(None)), "row") for k in range(d // row_chunk)]
        col_chunk = stage_col.shape[2]
        per_take = [d // row_chunk]
        for c in range(dff // cw):
            n = 0
            if (c * cw) % col_chunk == 0:
                cols = pl.ds(c * cw, min(col_chunk, dff - c * cw))
                pieces += [(wg_hbm.at[layer].at[:, cols], wg_ref, (slice(None), cols), "col"),
                           (wu_hbm.at[layer].at[:, cols], wu_ref, (slice(None), cols), "col")]
                n += 2
            rows = pl.ds(c * cw, cw)
            pieces.append((wd_hbm.at[layer].at[rows, :], wd_ref, (rows, slice(None)), "row"))
            per_take.append(n + 1)
        prime, take = _weight_stream(pieces, {"row": (stage_row, sem_row), "col": (stage_col, sem_col)})
        prime()
        if overlap_first_tile:
            tile(lambda: take(per_take[0]), lambda c: take(per_take[c + 1]))
        else:
            take(len(pieces))

    if overlap_first_tile:
        @pl.when(i > 0)
        def _():
            tile(lambda: None, lambda c: None)
    else:
        tile(lambda: None, lambda c: None)


def _post_attn_ffn(o_p, o_s, xs_in, w_o, wg, wu, wd, g, mods, *, layer, j, tiles_per_request, split_out):
    n_p, d = o_p.shape
    n_s = o_s.shape[0]
    tm = ROW_TILE
    tp, ts = n_p // tm, n_s // tm
    prompt_rows = pl.BlockSpec((tm, d), lambda i: (jnp.minimum(i, tp - 1), 0))
    latent_rows = pl.BlockSpec((tm, d), lambda i: (jnp.maximum(i - tp, 0), 0))
    all_rows = pl.BlockSpec((tm, d), lambda i: (i, 0))
    split_x = len(xs_in) == 2
    in_specs = [prompt_rows, latent_rows] + ([prompt_rows, latent_rows] if split_x else [all_rows])
    hbm = pl.BlockSpec(memory_space=pl.ANY)
    in_specs += [hbm, hbm, hbm, hbm, _layer_resident(g.shape, layer), _layer_resident(mods.shape, layer)]
    if split_out:
        out_shape = (jax.ShapeDtypeStruct((n_p, d), F32), jax.ShapeDtypeStruct((n_s, d), F32))
        out_specs = (prompt_rows, latent_rows)
    else:
        out_shape = jax.ShapeDtypeStruct((n_p + n_s, d), F32)
        out_specs = all_rows
    dff = wg.shape[2]
    scratch = [pltpu.VMEM((d, d), BF16), pltpu.VMEM((d, dff), BF16), pltpu.VMEM((d, dff), BF16),
               pltpu.VMEM((dff, d), BF16),
               pltpu.VMEM((WEIGHT_STAGE_SLOTS, FFN_CHUNK, d), F32),
               pltpu.VMEM((WEIGHT_COLUMN_SLOTS, d, 2 * FFN_CHUNK), F32),
               pltpu.SemaphoreType.DMA((WEIGHT_STAGE_SLOTS,)), pltpu.SemaphoreType.DMA((WEIGHT_COLUMN_SLOTS,))]
    return pl.pallas_call(
        functools.partial(_post_attn_ffn_kernel, layer=layer, j=j, n_prompt_tiles=tp,
                          tiles_per_request=tiles_per_request, split_x=split_x, split_out=split_out,
                          overlap_first_tile=not split_x),
        out_shape=out_shape,
        grid=(tp + ts,),
        in_specs=in_specs,
        out_specs=out_specs,
        scratch_shapes=scratch,
        compiler_params=pltpu.CompilerParams(dimension_semantics=("arbitrary",),
                                             vmem_limit_bytes=FFN_VMEM_LIMIT),
        name="post_attn_ffn",
    )(o_p, o_s, *xs_in, w_o, wg, wu, wd, g, mods)


def _rope_tables(n_lat):
    t = np.arange(n_lat)
    row = (t // GRID_W).astype(np.float32)
    col = (t % GRID_W).astype(np.float32)
    nf = ROT_DIM // 4
    inv = np.float32(ROPE_BASE) ** (-np.arange(nf, dtype=np.float32) / np.float32(nf))
    ar = row[:, None] * inv[None, :]
    ac = col[:, None] * inv[None, :]
    ang = np.concatenate([ar, ar, ac, ac], axis=-1)
    cos, sin = np.cos(ang), np.sin(ang)
    sign = np.where((np.arange(ROT_DIM) % 32) < 16, -1.0, 1.0).astype(np.float32)
    reps = LANES // ROT_DIM
    return jnp.asarray(np.tile(cos, (1, reps))), jnp.asarray(np.tile(sin * sign, (1, reps)))


def _swa_cache_to_feature_major(cache):
    nb, nl, lc, nh, hd = cache.shape
    return cache.transpose(0, 1, 3, 4, 2).reshape(nb, nl * nh * hd, lc)


def _swa_cache_from_feature_major(xt, seq):
    nb = xt.shape[0]
    return xt.reshape(nb, SWA_KV_HEADS, SWA_HD, seq).transpose(0, 3, 1, 2)


def kernel(x_prompt, x_sample, cache_diff_k, cache_diff_v, cache_swa_k, cache_swa_v, c, c_ctx,
           w_mod, b_mod, norm_g, w_qkv_diff, diff_lambda, diff_subln_g, w_o_diff,
           w_qkv_swa, swa_sink, w_o_swa, w_gate, w_up, w_down):
    bp, lp, d = x_prompt.shape
    bs, ls, _ = x_sample.shape
    lc = cache_diff_k.shape[2]
    depth = w_mod.shape[0]
    tm = ROW_TILE

    cond8 = jnp.concatenate([c_ctx[None, :], c, jnp.zeros((8 - 1 - bs, d), F32)], axis=0)
    mods = _modulation(cond8, w_mod, b_mod)
    tables = _rope_tables(ls)

    cdk = cache_diff_k.reshape(bs, -1, 2 * DIFF_HD)
    cdv = cache_diff_v.reshape(bs, -1, 2 * DIFF_HD)
    cskt = _swa_cache_to_feature_major(cache_swa_k)
    csvt = _swa_cache_to_feature_major(cache_swa_v)

    n_p, n_s = bp * lp, bs * ls
    x_parts = (x_prompt.reshape(n_p, d), x_sample.reshape(n_s, d))
    g = norm_g
    sub_g = diff_subln_g.reshape(-1, 1, 2 * DIFF_HD)
    diff_k_out, diff_v_out, swa_k_out, swa_v_out = [], [], [], []

    for i in range(depth):
        j = i // N_MIXERS
        if i % N_MIXERS == 0:
            lam_init = 0.8 - 0.6 * math.exp(-0.3 * i)
            qp, kp, vp, qs, ks, vs = _qkv_diff(x_parts, g, mods, w_qkv_diff, tables, layer=i, j=j,
                                               n_p=n_p, n_s=n_s, tiles_per_request=ls // QKV_ROW_TILE)
            op = _diff_prompt_attention(qp, kp, vp, diff_lambda, sub_g, j=j, seq=lp, lam_init=lam_init)
            diff_k_out.append(kp.reshape(bp, lp, DIFF_HEADS, 2 * DIFF_HD))
            diff_v_out.append(vp.reshape(bp, lp, DIFF_HEADS, 2 * DIFF_HD))
            os_ = _diff_latent_attention(qs, ks, vs, cdk, cdv, diff_lambda, sub_g,
                                         j=j, seq=ls, lc=lc, lam_init=lam_init)
            w_o = w_o_diff
        else:
            qp, ktp, vtp, qs, kds, vds = _qkv_swa(x_parts, g, mods, w_qkv_swa, tables, layer=i, j=j,
                                                  n_p=n_p, n_s=n_s, tiles_per_request=ls // QKV_ROW_TILE,
                                                  seq=lp)
            op = _swa_prompt_attention(qp, ktp, vtp, swa_sink[j], seq=lp)
            swa_k_out.append(_swa_cache_from_feature_major(ktp, lp))
            swa_v_out.append(_swa_cache_from_feature_major(vtp, lp))
            os_ = _swa_latent_attention(qs, kds, vds, cskt, csvt, swa_sink[j], j=j, seq=ls)
            w_o = w_o_swa
        last = i == depth - 1
        out = _post_attn_ffn(op, os_, x_parts, w_o, w_gate, w_up, w_down, g, mods, layer=i, j=j,
                             tiles_per_request=ls // tm, split_out=last)
        x_parts = out if last else (out,)
    xp, xs = x_parts

    return (xp.reshape(bp, lp, d), xs.reshape(bs, ls, d),
            jnp.stack(diff_k_out, axis=1), jnp.stack(diff_v_out, axis=1),
            jnp.stack(swa_k_out, axis=1), jnp.stack(swa_v_out, axis=1))
```

```python
import functools
import math

import jax
import jax.numpy as jnp
import numpy as np
from jax import lax
from jax.experimental import pallas as pl
from jax.experimental.pallas import tpu as pltpu

F32 = jnp.float32
BF16 = jnp.bfloat16

GRID_W = 64
N_MIXERS = 2
DIFF_HEADS = 8
DIFF_HD = 64
SWA_HEADS = 16
SWA_KV_HEADS = 4
SWA_GROUP = SWA_HEADS // SWA_KV_HEADS
SWA_HD = 64
ROT_DIM = 64
WINDOW = 128
ROPE_BASE = 10000.0
EPS = 1e-6
NEG_INF = -1e30

LANES = 128
ROW_TILE = 512
QKV_ROW_TILE = 512
PROMPT_SCORE_BUFFERS = 8
LATENT_SCORE_BUFFERS = 2
VMEM_LIMIT = 48 * 1024 * 1024
FFN_VMEM_LIMIT = 58 * 1024 * 1024
WEIGHT_STAGE_ROWS = 128
WEIGHT_STAGE_SLOTS = 3
WEIGHT_COLUMN_SLOTS = 2
QKV_STAGE_SLOTS = 4
FFN_CHUNK = 256
NT_DIMS = (((1,), (1,)), ((), ()))
LOG2E = math.log2(math.e)


def _params(n_axes):
    return pltpu.CompilerParams(dimension_semantics=("arbitrary",) * n_axes,
                                vmem_limit_bytes=VMEM_LIMIT)


def _layer_resident(shape, layer):
    return pl.BlockSpec((None,) + tuple(shape[1:]), lambda *_: (layer,) + (0,) * (len(shape) - 1),
                        pipeline_mode=pl.Buffered(1))


def _rms(x, g):
    ms = jnp.mean(x * x, axis=-1, keepdims=True)
    return (x * lax.rsqrt(ms + EPS)) * g


def _half_masks(dtype):
    lane = lax.broadcasted_iota(jnp.int32, (1, LANES), 1)
    lo = lane < (LANES // 2)
    return jnp.where(lo, 1.0, 0.0).astype(dtype), jnp.where(lo, 0.0, 1.0).astype(dtype)


def _lo64():
    return lax.broadcasted_iota(jnp.int32, (1, LANES), 1) < (LANES // 2)


def _ones_column(rows):
    return jnp.ones((rows, LANES), BF16)


def _ones_row(cols):
    return jnp.ones((LANES, cols), BF16)


def _mod_kernel(cond_ref, w_ref, b_ref, out_ref):
    c = cond_ref[...]
    s = c * jax.nn.sigmoid(c)
    out_ref[...] = jnp.dot(s.astype(BF16), w_ref[...].astype(BF16),
                           preferred_element_type=F32) + b_ref[pl.ds(pl.program_id(0), 1), :]


def _modulation(cond8, w_mod, b_mod):
    depth, d, n = w_mod.shape
    tn = 1536
    return pl.pallas_call(
        _mod_kernel,
        out_shape=jax.ShapeDtypeStruct((depth, 8, n), F32),
        grid=(depth, n // tn),
        in_specs=[pl.BlockSpec((8, d), lambda i, j: (0, 0)),
                  pl.BlockSpec((None, d, tn), lambda i, j: (i, 0, j)),
                  pl.BlockSpec((depth, tn), lambda i, j: (0, j))],
        out_specs=pl.BlockSpec((None, 8, tn), lambda i, j: (i, 0, j)),
        compiler_params=_params(2),
        name="modulation",
    )(cond8, w_mod, b_mod)


def _rope_slab(xs, cos, sin_signed, lo16):
    left = pltpu.roll(xs, LANES - 16, 1)
    right = pltpu.roll(xs, 16, 1)
    return xs * cos + jnp.where(lo16, left, right) * sin_signed


def _lo16_mask():
    lane = lax.broadcasted_iota(jnp.int32, (1, LANES), 1)
    return (lane % 32) < 16


def _qkv_both_groups(refs, n_prompt_tiles, tiles_per_request, split_x, j, prompt_tile, latent_tile):
    refs = list(refs)
    x_refs = refs[:2] if split_x else refs[:1]
    g_ref, mod_ref, w_hbm, cos_ref, sin_ref = refs[len(x_refs):len(x_refs) + 5]
    outs = refs[len(x_refs) + 5:len(x_refs) + 11]
    w_vmem, stage, sem = refs[len(x_refs) + 11:]
    d = x_refs[0].shape[1]
    i = pl.program_id(0)

    @pl.when(i == 0)
    def _():
        _load_weights_as_bf16([(w_hbm.at[j], w_vmem)], {w_vmem.shape[1]: stage}, {w_vmem.shape[1]: sem})

    def pre_norm(x, r):
        shift, scale = mod_ref[pl.ds(r, 1), 0:d], mod_ref[pl.ds(r, 1), d:2 * d]
        return (_rms(x, g_ref[0:1, :] * (1 + scale)) + shift).astype(BF16)

    @pl.when(i < n_prompt_tiles)
    def _():
        prompt_tile(pre_norm(x_refs[0][...], 0), w_vmem, outs[:3])

    @pl.when(i >= n_prompt_tiles)
    def _():
        r = 1 + (i - n_prompt_tiles) // tiles_per_request
        latent_tile(pre_norm(x_refs[-1][...], r), w_vmem, outs[3:], (cos_ref[...], sin_ref[...], _lo16_mask()))


def _qkv_call(kernel, name, x_parts, g, mods, w, tables, prompt_outs, latent_outs, *, layer, n_p, n_s):
    d = x_parts[0].shape[1]
    tm = QKV_ROW_TILE
    tp, ts = n_p // tm, n_s // tm
    nt = tables[0].shape[0] // tm

    def prompt_block(shape):
        return pl.BlockSpec(shape, lambda i: (jnp.minimum(i, tp - 1),) + (0,) * (len(shape) - 1))

    def latent_block(shape):
        return pl.BlockSpec(shape, lambda i: (jnp.maximum(i - tp, 0),) + (0,) * (len(shape) - 1))

    if len(x_parts) == 2:
        x_specs = [prompt_block((tm, d)), latent_block((tm, d))]
    else:
        x_specs = [pl.BlockSpec((tm, d), lambda i: (i, 0))]
    table_spec = pl.BlockSpec((tm, LANES), lambda i: (jnp.maximum(i - tp, 0) % nt, 0))
    in_specs = x_specs + [_layer_resident(g.shape, layer), _layer_resident(mods.shape, layer),
                          pl.BlockSpec(memory_space=pl.ANY), table_spec, table_spec]
    out_shape, out_specs = [], []
    for outs, n, block in ((prompt_outs, n_p, prompt_block), (latent_outs, n_s, latent_block)):
        for shape_of, dtype in outs:
            out_shape.append(jax.ShapeDtypeStruct(shape_of(n), dtype))
            out_specs.append(block(shape_of(tm)))
    cols = w.shape[2]
    scratch = [pltpu.VMEM((d, cols), BF16), pltpu.VMEM((QKV_STAGE_SLOTS, WEIGHT_STAGE_ROWS, cols), F32),
               pltpu.SemaphoreType.DMA((QKV_STAGE_SLOTS,))]
    return pl.pallas_call(
        kernel,
        out_shape=tuple(out_shape),
        grid=(tp + ts,),
        in_specs=in_specs,
        out_specs=tuple(out_specs),
        scratch_shapes=scratch,
        compiler_params=_params(1),
        name=name,
    )(*x_parts, g, mods, w, *tables)


def _diff_tile(h, w_ref, outs, rope=None):
    q_ref, k_ref, v_ref = outs
    d = h.shape[1]
    cw = 512
    for c in range(3 * d // cw):
        acc = jnp.dot(h, w_ref[:, c * cw:(c + 1) * cw], preferred_element_type=F32)
        which, off = divmod(c * cw, d)
        dst = outs[which]
        for s in range(cw // LANES):
            xs = acc[:, s * LANES:(s + 1) * LANES]
            if rope is not None and which < 2:
                xs = _rope_slab(xs, *rope)
            if which == 0:
                xs = xs * (DIFF_HD ** -0.5 * LOG2E)
            lo = off + s * LANES
            if rope is not None or which == 0:
                dst[:, lo:lo + LANES] = xs.astype(dst.dtype)
            else:
                dst[pl.ds(lo // LANES, h.shape[0], stride=DIFF_HEADS), :] = xs


def _qkv_diff_kernel(*refs, j, n_prompt_tiles, tiles_per_request, split_x):
    _qkv_both_groups(refs, n_prompt_tiles, tiles_per_request, split_x, j, _diff_tile, _diff_tile)


def _qkv_diff(x_parts, g, mods, w, tables, *, layer, j, n_p, n_s, tiles_per_request):
    d = x_parts[0].shape[1]

    def rows(n):
        return (n, d)

    def cache(n):
        return (n * DIFF_HEADS, d // DIFF_HEADS)

    kernel = functools.partial(_qkv_diff_kernel, j=j, n_prompt_tiles=n_p // QKV_ROW_TILE,
                               tiles_per_request=tiles_per_request, split_x=len(x_parts) == 2)
    return _qkv_call(kernel, "qkv_diff", x_parts, g, mods, w, tables,
                     [(rows, BF16), (cache, F32), (cache, F32)], [(rows, BF16)] * 3,
                     layer=layer, n_p=n_p, n_s=n_s)


def _swa_q_tile(h, w_ref, q_ref, rope):
    d = h.shape[1]
    cw = 512
    for c in range(d // cw):
        acc = jnp.dot(h, w_ref[:, c * cw:(c + 1) * cw], preferred_element_type=F32)
        for s in range(cw // LANES):
            xs = acc[:, s * LANES:(s + 1) * LANES]
            if rope is not None:
                xs = _rope_slab(xs, *rope)
            lo = c * cw + s * LANES
            q_ref[:, lo:lo + LANES] = (xs * (SWA_HD ** -0.5 * LOG2E)).astype(BF16)
    nkv = SWA_KV_HEADS * SWA_HD
    return jnp.dot(h, w_ref[:, d:d + 2 * nkv], preferred_element_type=F32), nkv


def _swa_prompt_tile(h, w_ref, outs, *, seq):
    q_ref, kt_ref, vt_ref = outs
    kv, nkv = _swa_q_tile(h, w_ref, q_ref, None)
    for b in range(h.shape[0] // seq):
        kt_ref[b] = kv[b * seq:(b + 1) * seq, :nkv].T
        vt_ref[b] = kv[b * seq:(b + 1) * seq, nkv:].T


def _swa_latent_tile(h, w_ref, outs, rope):
    q_ref, kd_ref, vd_ref = outs
    kv, nkv = _swa_q_tile(h, w_ref, q_ref, rope)
    lo64 = _lo64()
    for which, dst in enumerate((kd_ref, vd_ref)):
        for s in range(nkv // LANES):
            xs = kv[:, which * nkv + s * LANES: which * nkv + (s + 1) * LANES]
            if which == 0:
                xs = _rope_slab(xs, *rope)
            sw = pltpu.roll(xs, LANES // 2, 1)
            dst[:, (2 * s) * LANES:(2 * s + 1) * LANES] = jnp.where(lo64, xs, sw).astype(BF16)
            dst[:, (2 * s + 1) * LANES:(2 * s + 2) * LANES] = jnp.where(lo64, sw, xs).astype(BF16)


def _qkv_swa_kernel(*refs, j, n_prompt_tiles, tiles_per_request, split_x, seq):
    _qkv_both_groups(refs, n_prompt_tiles, tiles_per_request, split_x, j,
                     functools.partial(_swa_prompt_tile, seq=seq), _swa_latent_tile)


def _qkv_swa(x_parts, g, mods, w, tables, *, layer, j, n_p, n_s, tiles_per_request, seq):
    d = x_parts[0].shape[1]
    nkv = SWA_KV_HEADS * SWA_HD

    def rows(n):
        return (n, d)

    def feature_major(n):
        return (n // seq, nkv, seq)

    def duplicated(n):
        return (n, 2 * nkv)

    kernel = functools.partial(_qkv_swa_kernel, j=j, n_prompt_tiles=n_p // QKV_ROW_TILE,
                               tiles_per_request=tiles_per_request, split_x=len(x_parts) == 2, seq=seq)
    return _qkv_call(kernel, "qkv_swa", x_parts, g, mods, w, tables,
                     [(rows, BF16), (feature_major, F32), (feature_major, F32)],
                     [(rows, BF16), (duplicated, BF16), (duplicated, BF16)],
                     layer=layer, n_p=n_p, n_s=n_s)


def _diff_lambda(lam_ref, lam_init):
    lp = lam_ref[...]
    a = jnp.sum(lp[0:1] * lp[1:2], axis=-1, keepdims=True)
    b = jnp.sum(lp[2:3] * lp[3:4], axis=-1, keepdims=True)
    return jnp.exp(a) - jnp.exp(b) + lam_init


def _diff_combine(acc, tq, lam, g, lam_init):
    o12 = acc[:, :LANES] / acc[:, LANES:]
    o = o12[:tq] - lam * o12[tq:]
    return _rms(o, g) * (1.0 - lam_init)


def _stack_maps(q):
    m_lo, m_hi = _half_masks(BF16)
    return jnp.concatenate([q * m_lo, q * m_hi], axis=0)


def _run_pipelined(items, scores, finish, s_bufs):
    depth = len(s_bufs)
    states = {i: scores(items[i], s_bufs[i]) for i in range(min(depth - 1, len(items)))}
    for i, item in enumerate(items):
        ahead = i + depth - 1
        if ahead < len(items):
            states[ahead] = scores(items[ahead], s_bufs[ahead % depth])
        finish(item, s_bufs[i % depth], states.pop(i))


def _store_scores(s_ref, col0, s, mrun):
    s_ref[:, col0:col0 + s.shape[1]] = s
    for t in range(s.shape[1] // LANES):
        blk = s[:, t * LANES:(t + 1) * LANES]
        mrun = blk if mrun is None else jnp.maximum(mrun, blk)
    return mrun


def _exp_block(s_ref, col0, width, mb):
    return jnp.concatenate(
        [jnp.exp2(s_ref[:, col0 + t * LANES:col0 + (t + 1) * LANES] - mb).astype(BF16)
         for t in range(width // LANES)], axis=1)


def _diff_prompt_kernel(q_ref, k_ref, v_ref, lam_ref, g_ref, o_ref, *s_bufs, lam_init, seq):
    lam = _diff_lambda(lam_ref, lam_init)
    g = g_ref[...]
    ones = _ones_column(seq)
    items = [(r, h) for r in range(q_ref.shape[0] // seq) for h in range(DIFF_HEADS)]

    def head_rows(ref, r, h):
        return ref[pl.ds(r * seq * DIFF_HEADS + h, seq, stride=DIFF_HEADS), :]

    def scores(item, s_ref):
        r, h = item
        rows, sl = slice(r * seq, (r + 1) * seq), slice(h * LANES, (h + 1) * LANES)
        s = lax.dot_general(_stack_maps(q_ref[rows, sl]), head_rows(k_ref, r, h).astype(BF16), NT_DIMS,
                            preferred_element_type=F32)
        return _store_scores(s_ref, 0, s, None)

    def finish(item, s_ref, mrun):
        r, h = item
        rows, sl = slice(r * seq, (r + 1) * seq), slice(h * LANES, (h + 1) * LANES)
        mb = jnp.broadcast_to(jnp.max(mrun, axis=-1, keepdims=True), (2 * seq, LANES))
        vx = jnp.concatenate([head_rows(v_ref, r, h).astype(BF16), ones], axis=1)
        acc = jnp.dot(_exp_block(s_ref, 0, seq, mb), vx, preferred_element_type=F32)
        o_ref[rows, sl] = _diff_combine(acc, seq, lam, g, lam_init).astype(BF16)

    _run_pipelined(items, scores, finish, s_bufs)


def _diff_prompt_attention(q, k, v, lam_params, subln_g, *, j, seq, lam_init):
    n, d = q.shape
    req = 2
    spec = pl.BlockSpec((req * seq, d), lambda b: (b, 0))
    kv_spec = pl.BlockSpec((req * seq * DIFF_HEADS, d // DIFF_HEADS), lambda b: (b, 0))
    return pl.pallas_call(
        functools.partial(_diff_prompt_kernel, lam_init=lam_init, seq=seq),
        out_shape=jax.ShapeDtypeStruct((n, d), BF16),
        grid=(n // (req * seq),),
        in_specs=[spec, kv_spec, kv_spec, _layer_resident(lam_params.shape, j),
                  _layer_resident(subln_g.shape, j)],
        out_specs=spec,
        scratch_shapes=[pltpu.VMEM((2 * seq, seq), F32)] * PROMPT_SCORE_BUFFERS,
        compiler_params=_params(1),
        name="diff_attn_prompt",
    )(q, k, v, lam_params, subln_g)


def _diff_latent_kernel(q_ref, kc_ref, vc_ref, kl_ref, vl_ref, qn_ref, kcn_ref, kln_ref, lam_ref, g_ref, o_ref,
                        kk_ref, vx_ref, m_ref, s0_ref, s1_ref, *, lam_init, lc, tq, key_chunk):
    seq = q_ref.shape[0]
    nk = kk_ref.shape[0]
    head = pl.program_id(1)
    step = pl.program_id(0) * pl.num_programs(1) + head
    last_step = pl.num_programs(0) * pl.num_programs(1) - 1
    next_head = jnp.minimum(step + 1, last_step) % DIFF_HEADS
    masks = _half_masks(BF16)
    nchunk = nk // key_chunk
    s_bufs = (s0_ref, s1_ref)

    def fill_keys(cache_ref, h, latent_ref):
        kk_ref[0:lc, :] = cache_ref[pl.ds(h, lc, stride=DIFF_HEADS), :].astype(BF16)
        kk_ref[lc:, :] = latent_ref[...]

    def scores(q_rows, m, s_ref):
        q = q_rows * masks[m]
        mrun = None
        for c in range(nchunk):
            s = lax.dot_general(q, kk_ref[c * key_chunk:(c + 1) * key_chunk, :], NT_DIMS,
                                preferred_element_type=F32)
            mrun = _store_scores(s_ref, c * key_chunk, s, mrun)
        return mrun

    @pl.when(step == 0)
    def _():
        fill_keys(kc_ref, head, kl_ref)
        m_ref[...] = scores(q_ref[0:tq, :], 0, s0_ref)

    vx_ref[0:lc, 0:LANES] = vc_ref[pl.ds(head, lc, stride=DIFF_HEADS), :].astype(BF16)
    vx_ref[lc:, 0:LANES] = vl_ref[...]
    vx_ref[:, LANES:2 * LANES] = _ones_column(nk)

    items = [(rt, m) for rt in range(seq // tq) for m in range(2)]
    lam = _diff_lambda(lam_ref, lam_init)
    g = g_ref[...]
    first_map = {}

    def finish(item, s_ref, mrun):
        rt, m = item
        mb = jnp.broadcast_to(jnp.max(mrun, axis=-1, keepdims=True), (tq, LANES))
        acc = None
        for c in range(nchunk):
            part = jnp.dot(_exp_block(s_ref, c * key_chunk, key_chunk, mb),
                           vx_ref[c * key_chunk:(c + 1) * key_chunk, :], preferred_element_type=F32)
            acc = part if acc is None else acc + part
        o_m = acc[:, :LANES] / acc[:, LANES:]
        if m == 0:
            first_map[rt] = o_m
        else:
            o = _rms(first_map.pop(rt) - lam * o_m, g) * (1.0 - lam_init)
            o_ref[rt * tq:(rt + 1) * tq, :] = o.astype(BF16)

    mrun = m_ref[...]
    for i, item in enumerate(items):
        s_ahead = s_bufs[(i + 1) % 2]
        if i + 1 < len(items):
            rt, m = items[i + 1]
            ahead = scores(q_ref[rt * tq:(rt + 1) * tq, :], m, s_ahead)
        else:
            fill_keys(kcn_ref, next_head, kln_ref)
            ahead = scores(qn_ref[...], 0, s_ahead)
            m_ref[...] = ahead
        finish(item, s_bufs[i % 2], mrun)
        mrun = ahead


def _diff_latent_attention(q, k, v, cache_k, cache_v, lam_params, subln_g, *, j, seq, lc, lam_init):
    n, d = q.shape
    nb = cache_k.shape[0]
    tq = 512
    key_chunk = 512
    last_step = nb * DIFF_HEADS - 1

    def next_step(b, h):
        return divmod(jnp.minimum(b * DIFF_HEADS + h + 1, last_step), DIFF_HEADS)

    def next_query_tile(b, h):
        nb_, nh = next_step(b, h)
        return nb_ * (seq // tq), nh

    q_spec = pl.BlockSpec((seq, LANES), lambda b, h: (b, h))
    c_spec = pl.BlockSpec((None, lc * DIFF_HEADS, LANES), lambda b, h: (b, j, 0))
    qn_spec = pl.BlockSpec((tq, LANES), next_query_tile)
    kln_spec = pl.BlockSpec((seq, LANES), next_step)
    cn_spec = pl.BlockSpec((None, lc * DIFF_HEADS, LANES), lambda b, h: (next_step(b, h)[0], j, 0))
    return pl.pallas_call(
        functools.partial(_diff_latent_kernel, lam_init=lam_init, lc=lc, tq=tq, key_chunk=key_chunk),
        out_shape=jax.ShapeDtypeStruct((n, d), BF16),
        grid=(nb, DIFF_HEADS),
        in_specs=[q_spec, c_spec, c_spec, q_spec, q_spec, qn_spec, cn_spec, kln_spec,
                  _layer_resident(lam_params.shape, j), _layer_resident(subln_g.shape, j)],
        out_specs=q_spec,
        scratch_shapes=[pltpu.VMEM((lc + seq, LANES), BF16),
                        pltpu.VMEM((lc + seq, 2 * LANES), BF16),
                        pltpu.VMEM((tq, LANES), F32),
                        *[pltpu.VMEM((tq, lc + seq), F32)] * LATENT_SCORE_BUFFERS],
        compiler_params=_params(2),
        name="diff_attn_latent",
    )(q, cache_k, cache_v, k, v, q, cache_k, k, lam_params, subln_g)


def _stack_group(q_ref, rows, kv_local):
    m_lo, m_hi = _half_masks(BF16)
    parts = []
    for gb in range(SWA_GROUP // 2):
        blk = kv_local * (SWA_GROUP // 2) + gb
        qb = q_ref[rows, blk * LANES:(blk + 1) * LANES]
        parts += [qb * m_lo, qb * m_hi]
    return jnp.concatenate(parts, axis=0)


def _sink_column(sink_ref, first_head, tq):
    return jnp.concatenate([jnp.full((tq, LANES), sink_ref[first_head + g] * LOG2E, F32)
                            for g in range(SWA_GROUP)], axis=0)


def _write_group(o_ref, rows, kv_local, o, tq):
    lo64 = _lo64()
    for gb in range(SWA_GROUP // 2):
        blk = kv_local * (SWA_GROUP // 2) + gb
        even = o[(2 * gb) * tq:(2 * gb + 1) * tq]
        odd = o[(2 * gb + 1) * tq:(2 * gb + 2) * tq]
        o_ref[rows, blk * LANES:(blk + 1) * LANES] = jnp.where(lo64, even, odd).astype(BF16)


def _dup_rows(x_t):
    xb = x_t.astype(BF16)
    return jnp.concatenate([xb, xb], axis=0)


def _sink_finish(mrun, sk, rows):
    mb = jnp.maximum(jnp.broadcast_to(jnp.max(mrun, axis=-1, keepdims=True), (rows, LANES)), sk)
    return mb, jnp.exp2(sk - mb)


def _swa_prompt_kernel(sink_ref, q_ref, kt_ref, vt_ref, o_ref, *s_bufs):
    seq = kt_ref.shape[2]
    rows = SWA_GROUP * seq
    ones = _ones_row(seq)
    items = [(r, j) for r in range(kt_ref.shape[0]) for j in range(SWA_KV_HEADS)]

    def scores(item, s_ref):
        r, j = item
        kd = _dup_rows(kt_ref[r, j * SWA_HD:(j + 1) * SWA_HD, :])
        s = jnp.dot(_stack_group(q_ref, slice(r * seq, (r + 1) * seq), j), kd, preferred_element_type=F32)
        return _store_scores(s_ref, 0, s, None)

    def finish(item, s_ref, mrun):
        r, j = item
        sk = _sink_column(sink_ref, j * SWA_GROUP, seq)
        mb, sink_term = _sink_finish(mrun, sk, rows)
        vx = jnp.concatenate([_dup_rows(vt_ref[r, j * SWA_HD:(j + 1) * SWA_HD, :]), ones], axis=0)
        acc = lax.dot_general(_exp_block(s_ref, 0, seq, mb), vx, NT_DIMS, preferred_element_type=F32)
        o = acc[:, :LANES] / (acc[:, LANES:] + sink_term)
        _write_group(o_ref, slice(r * seq, (r + 1) * seq), j, o, seq)

    _run_pipelined(items, scores, finish, s_bufs)


def _swa_prompt_attention(q, kt, vt, sink, *, seq):
    n, d = q.shape
    nkv = kt.shape[1]
    req = 2
    t_spec = pl.BlockSpec((req, nkv, seq), lambda b: (b, 0, 0))
    return pl.pallas_call(
        _swa_prompt_kernel,
        out_shape=jax.ShapeDtypeStruct((n, d), BF16),
        grid=(n // (req * seq),),
        in_specs=[pl.BlockSpec(memory_space=pltpu.SMEM),
                  pl.BlockSpec((req * seq, d), lambda b: (b, 0)), t_spec, t_spec],
        out_specs=pl.BlockSpec((req * seq, d), lambda b: (b, 0)),
        scratch_shapes=[pltpu.VMEM((SWA_GROUP * seq, seq), F32)] * (PROMPT_SCORE_BUFFERS // 2),
        compiler_params=_params(1),
        name="swa_attn_prompt",
    )(sink, q, kt, vt)


def _swa_latent_kernel(sink_ref, q_ref, kc_ref, vc_ref, kl_ref, vl_ref, qn_ref, kcn_ref, kln_ref, o_ref,
                       kcd_ref, vcx_ref, m_ref, s0_ref, s1_ref, *, tq, span):
    pair, part = pl.program_id(1), pl.program_id(2)
    n_parts = pl.num_programs(2)
    tiles = q_ref.shape[0] // tq
    seq = kl_ref.shape[0]
    lc = kc_ref.shape[1]
    rows = SWA_GROUP * tq
    step = (pl.program_id(0) * pl.num_programs(1) + pair) * n_parts + part
    last_step = pl.num_programs(0) * pl.num_programs(1) * n_parts - 1
    next_part = jnp.minimum(step + 1, last_step) % n_parts
    ones_row = _ones_row(lc)
    for jj in range(2):
        kcd_ref[jj] = _dup_rows(kc_ref[jj * SWA_HD:(jj + 1) * SWA_HD, :])
        vcx_ref[jj] = jnp.concatenate([_dup_rows(vc_ref[jj * SWA_HD:(jj + 1) * SWA_HD, :]), ones_row], axis=0)
    ones_col = _ones_column(span)
    items = [(t, jj) for t in range(tiles) for jj in range(2)]
    s_bufs = (s0_ref, s1_ref)
    windows, biases = {}, {}

    def window_of(part, t):
        q0 = (part * tiles + t) * tq
        return q0, pl.multiple_of(jnp.clip(q0 - WINDOW, 0, seq - span), WINDOW)

    def band_bias(q0, ws):
        q_pos = q0 + lax.broadcasted_iota(jnp.int32, (tq, span), 0)
        k_pos = ws + lax.broadcasted_iota(jnp.int32, (tq, span), 1)
        b = jnp.where(jnp.abs(q_pos - k_pos) <= WINDOW, 0.0, NEG_INF).astype(F32)
        return jnp.concatenate([b] * SWA_GROUP, axis=0)

    def window(t):
        if t not in windows:
            windows[t] = window_of(part, t)
        return windows[t]

    def bias_for(t):
        if t not in biases:
            biases[t] = band_bias(*window(t))
        return biases[t]

    def score_tile(qs, cache_keys, window_keys, bias, s_ref):
        s_c = jnp.dot(qs, cache_keys, preferred_element_type=F32)
        mrun = _store_scores(s_ref, 0, s_c, None)
        s_w = lax.dot_general(qs, window_keys, NT_DIMS, preferred_element_type=F32) + bias
        return _store_scores(s_ref, lc, s_w, mrun)

    def scores(item, s_ref):
        t, jj = item
        _, ws = window(t)
        return score_tile(_stack_group(q_ref, slice(t * tq, (t + 1) * tq), jj), kcd_ref[jj],
                          kl_ref[pl.ds(ws, span), jj * LANES:(jj + 1) * LANES], bias_for(t), s_ref)

    def finish(item, s_ref, mrun):
        t, jj = item
        _, ws = window(t)
        sk = _sink_column(sink_ref, (2 * pair + jj) * SWA_GROUP, tq)
        mb, sink_term = _sink_finish(mrun, sk, rows)
        vwx = jnp.concatenate([vl_ref[pl.ds(ws, span), jj * LANES:(jj + 1) * LANES], ones_col], axis=1)
        acc = (lax.dot_general(_exp_block(s_ref, 0, lc, mb), vcx_ref[jj], NT_DIMS, preferred_element_type=F32)
               + jnp.dot(_exp_block(s_ref, lc, span, mb), vwx, preferred_element_type=F32))
        o = acc[:, :LANES] / (acc[:, LANES:] + sink_term)
        _write_group(o_ref, slice(t * tq, (t + 1) * tq), jj, o, tq)

    @pl.when(step == 0)
    def _():
        q0, ws = window_of(part, 0)
        m_ref[...] = score_tile(_stack_group(q_ref, slice(0, tq), 0), kcd_ref[0],
                                kl_ref[pl.ds(ws, span), 0:LANES], band_bias(q0, ws), s0_ref)

    mrun = m_ref[...]
    for i, item in enumerate(items):
        s_ahead = s_bufs[(i + 1) % 2]
        if i + 1 < len(items):
            ahead = scores(items[i + 1], s_ahead)
        else:
            q0, ws = window_of(next_part, 0)
            ahead = score_tile(_stack_group(qn_ref, slice(0, tq), 0), _dup_rows(kcn_ref[...]),
                               kln_ref[pl.ds(ws, span), :], band_bias(q0, ws), s_ahead)
            m_ref[...] = ahead
        finish(item, s_bufs[i % 2], mrun)
        mrun = ahead


def _swa_latent_attention(q, kd, vd, cache_kt, cache_vt, sink, *, j, seq):
    n, d = q.shape
    nb, _, lc = cache_kt.shape
    tq = 256
    span = tq + 2 * WINDOW
    npair = SWA_KV_HEADS // 2
    wq = d // npair
    parts = 2
    tiles = seq // parts // tq
    last_step = nb * npair * parts - 1

    def next_step(b, p, i):
        rest, ni = divmod(jnp.minimum((b * npair + p) * parts + i + 1, last_step), parts)
        return (*divmod(rest, npair), ni)

    def next_query_tile(b, p, i):
        nb_, np_, ni = next_step(b, p, i)
        return (nb_ * parts + ni) * tiles, 2 * np_

    def next_cache_head(b, p, i):
        nb_, np_, _ = next_step(b, p, i)
        return nb_, 2 * (j * npair + np_), 0

    def next_latent_head(b, p, i):
        nb_, np_, _ = next_step(b, p, i)
        return nb_, 2 * np_

    q_spec = pl.BlockSpec((seq // parts, wq), lambda b, p, i: (b * parts + i, p))
    c_spec = pl.BlockSpec((None, 2 * SWA_HD, lc), lambda b, p, i: (b, j * npair + p, 0))
    l_spec = pl.BlockSpec((seq, 2 * LANES), lambda b, p, i: (b, p))
    qn_spec = pl.BlockSpec((tq, 2 * LANES), next_query_tile)
    cn_spec = pl.BlockSpec((None, SWA_HD, lc), next_cache_head)
    ln_spec = pl.BlockSpec((seq, LANES), next_latent_head)
    s_shape = pltpu.VMEM((SWA_GROUP * tq, lc + span), F32)
    return pl.pallas_call(
        functools.partial(_swa_latent_kernel, tq=tq, span=span),
        out_shape=jax.ShapeDtypeStruct((n, d), BF16),
        grid=(nb, npair, parts),
        in_specs=[pl.BlockSpec(memory_space=pltpu.SMEM), q_spec, c_spec, c_spec, l_spec, l_spec,
                  qn_spec, cn_spec, ln_spec],
        out_specs=q_spec,
        scratch_shapes=[pltpu.VMEM((2, 2 * SWA_HD, lc), BF16), pltpu.VMEM((2, 2 * LANES, lc), BF16),
                        pltpu.VMEM((SWA_GROUP * tq, LANES), F32), s_shape, s_shape],
        compiler_params=_params(3),
        name="swa_attn_latent",
    )(sink, q, cache_kt, cache_vt, kd, vd, q, cache_kt, kd)


def _load_weights_as_bf16(jobs, stages, sems):
    order = []
    rings = {w: [] for w in stages}
    for src, dst in jobs:
        w = src.shape[1]
        slots = stages[w].shape[0]
        for k in range(src.shape[0] // WEIGHT_STAGE_ROWS):
            rows = pl.ds(k * WEIGHT_STAGE_ROWS, WEIGHT_STAGE_ROWS)
            slot = len(rings[w]) % slots
            copy = pltpu.make_async_copy(src.at[rows, :], stages[w].at[slot], sems[w].at[slot])
            order.append((w, len(rings[w])))
            rings[w].append((copy, slot, dst, rows))
    for w, ring in rings.items():
        for copy, _, _, _ in ring[:stages[w].shape[0]]:
            copy.start()
    for w, k in order:
        copy, slot, dst, rows = rings[w][k]
        copy.wait()
        dst[rows, :] = stages[w][slot].astype(BF16)
        ahead = k + stages[w].shape[0]
        if ahead < len(rings[w]):
            rings[w][ahead][0].start()


def _weight_stream(pieces, rings):
    per_ring = {name: [] for name in rings}
    plan = []
    for src, dst, idx, name in pieces:
        stage, sem = rings[name]
        slot = len(per_ring[name]) % stage.shape[0]
        view = stage.at[slot].at[0:src.shape[0], 0:src.shape[1]]
        plan.append((name, len(per_ring[name])))
        per_ring[name].append((pltpu.make_async_copy(src, view, sem.at[slot]), view, dst, idx))
    cursor = [0]

    def prime():
        for name, ring in per_ring.items():
            for copy, _, _, _ in ring[:rings[name][0].shape[0]]:
                copy.start()

    def take(n):
        for name, k in plan[cursor[0]:cursor[0] + n]:
            copy, view, dst, idx = per_ring[name][k]
            copy.wait()
            dst[idx] = view[...].astype(BF16)
            ahead = k + rings[name][0].shape[0]
            if ahead < len(per_ring[name]):
                per_ring[name][ahead][0].start()
        cursor[0] += n

    return prime, take


def _post_attn_ffn_kernel(*refs, layer, j, n_prompt_tiles, tiles_per_request, split_x, split_out,
                          overlap_first_tile):
    refs = list(refs)
    op_ref, os_ref = refs[:2]
    x_refs = refs[2:4] if split_x else refs[2:3]
    wo_hbm, wg_hbm, wu_hbm, wd_hbm, g_ref, mod_ref = refs[2 + len(x_refs):8 + len(x_refs)]
    n_out = 2 if split_out else 1
    out_refs = refs[8 + len(x_refs):8 + len(x_refs) + n_out]
    wo_ref, wg_ref, wu_ref, wd_ref, stage_row, stage_col, sem_row, sem_col = refs[8 + len(x_refs) + n_out:]
    d = wo_ref.shape[1]
    dff = wg_ref.shape[1]
    cw = FFN_CHUNK
    i = pl.program_id(0)
    is_prompt = i < n_prompt_tiles
    r = jnp.where(is_prompt, 0, 1 + (i - n_prompt_tiles) // tiles_per_request)

    def mod(slot):
        return mod_ref[pl.ds(r, 1), slot * d:(slot + 1) * d]

    def tile(before_out_proj, before_chunk):
        o = jnp.where(is_prompt, op_ref[...], os_ref[...])
        x = jnp.where(is_prompt, x_refs[0][...], x_refs[1][...]) if split_x else x_refs[0][...]
        before_out_proj()
        y = jnp.dot(o, wo_ref[...], preferred_element_type=F32)
        x = x + _rms(y, mod(2) * g_ref[1:2, :])
        h = (_rms(x, g_ref[2:3, :] * (1 + mod(4))) + mod(3)).astype(BF16)
        y = jnp.zeros((h.shape[0], d), F32)
        for c in range(dff // cw):
            before_chunk(c)
            cols = slice(c * cw, (c + 1) * cw)
            a = jnp.dot(h, wg_ref[:, cols], preferred_element_type=F32)
            u = jnp.dot(h, wu_ref[:, cols], preferred_element_type=F32)
            t = (a * jax.nn.sigmoid(a)) * u
            y = y + jnp.dot(t.astype(BF16), wd_ref[cols, :], preferred_element_type=F32)
        out = x + _rms(y, mod(5) * g_ref[3:4, :])
        if split_out:
            @pl.when(is_prompt)
            def _():
                out_refs[0][...] = out

            @pl.when(jnp.logical_not(is_prompt))
            def _():
                out_refs[1][...] = out
        else:
            out_refs[0][...] = out

    @pl.when(i == 0)
    def _():
        row_chunk = stage_row.shape[1]
        pieces = [(wo_hbm.at[j].at[pl.ds(k * row_chunk, row_chunk), :], wo_ref,
                   (pl.ds(k * row_chunk, row_chunk), slice(None)), "row") for k in range(d // row_chunk)]
        col_chunk = stage_col.shape[2]
        per_take = [d // row_chunk]
        for c in range(dff // cw):
            n = 0
            if (c * cw) % col_chunk == 0:
                cols = pl.ds(c * cw, min(col_chunk, dff - c * cw))
                pieces += [(wg_hbm.at[layer].at[:, cols], wg_ref, (slice(None), cols), "col"),
                           (wu_hbm.at[layer].at[:, cols], wu_ref, (slice(None), cols), "col")]
                n += 2
            rows = pl.ds(c * cw, cw)
            pieces.append((wd_hbm.at[layer].at[rows, :], wd_ref, (rows, slice(None)), "row"))
            per_take.append(n + 1)
        prime, take = _weight_stream(pieces, {"row": (stage_row, sem_row), "col": (stage_col, sem_col)})
        prime()
        if overlap_first_tile:
            tile(lambda: take(per_take[0]), lambda c: take(per_take[c + 1]))
        else:
            take(len(pieces))

    if overlap_first_tile:
        @pl.when(i > 0)
        def _():
            tile(lambda: None, lambda c: None)
    else:
        tile(lambda: None, lambda c: None)


def _post_attn_ffn(o_p, o_s, xs_in, w_o, wg, wu, wd, g, mods, *, layer, j, tiles_per_request, split_out):
    n_p, d = o_p.shape
    n_s = o_s.shape[0]
    tm = ROW_TILE
    tp, ts = n_p // tm, n_s // tm
    prompt_rows = pl.BlockSpec((tm, d), lambda i: (jnp.minimum(i, tp - 1), 0))
    latent_rows = pl.BlockSpec((tm, d), lambda i: (jnp.maximum(i - tp, 0), 0))
    all_rows = pl.BlockSpec((tm, d), lambda i: (i, 0))
    split_x = len(xs_in) == 2
    in_specs = [prompt_rows, latent_rows] + ([prompt_rows, latent_rows] if split_x else [all_rows])
    hbm = pl.BlockSpec(memory_space=pl.ANY)
    in_specs += [hbm, hbm, hbm, hbm, _layer_resident(g.shape, layer), _layer_resident(mods.shape, layer)]
    if split_out:
        out_shape = (jax.ShapeDtypeStruct((n_p, d), F32), jax.ShapeDtypeStruct((n_s, d), F32))
        out_specs = (prompt_rows, latent_rows)
    else:
        out_shape = jax.ShapeDtypeStruct((n_p + n_s, d), F32)
        out_specs = all_rows
    dff = wg.shape[2]
    scratch = [pltpu.VMEM((d, d), BF16), pltpu.VMEM((d, dff), BF16), pltpu.VMEM((d, dff), BF16),
               pltpu.VMEM((dff, d), BF16),
               pltpu.VMEM((WEIGHT_STAGE_SLOTS, FFN_CHUNK, d), F32),
               pltpu.VMEM((WEIGHT_COLUMN_SLOTS, d, 2 * FFN_CHUNK), F32),
               pltpu.SemaphoreType.DMA((WEIGHT_STAGE_SLOTS,)), pltpu.SemaphoreType.DMA((WEIGHT_COLUMN_SLOTS,))]
    return pl.pallas_call(
        functools.partial(_post_attn_ffn_kernel, layer=layer, j=j, n_prompt_tiles=tp,
                          tiles_per_request=tiles_per_request, split_x=split_x, split_out=split_out,
                          overlap_first_tile=not split_x),
        out_shape=out_shape,
        grid=(tp + ts,),
        in_specs=in_specs,
        out_specs=out_specs,
        scratch_shapes=scratch,
        compiler_params=pltpu.CompilerParams(dimension_semantics=("arbitrary",),
                                             vmem_limit_bytes=FFN_VMEM_LIMIT),
        name="post_attn_ffn",
    )(o_p, o_s, *xs_in, w_o, wg, wu, wd, g, mods)


def _rope_tables(n_lat):
    t = np.arange(n_lat)
    row = (t // GRID_W).astype(np.float32)
    col = (t % GRID_W).astype(np.float32)
    nf = ROT_DIM // 4
    inv = np.float32(ROPE_BASE) ** (-np.arange(nf, dtype=np.float32) / np.float32(nf))
    ar = row[:, None] * inv[None, :]
    ac = col[:, None] * inv[None, :]
    ang = np.concatenate([ar, ar, ac, ac], axis=-1)
    cos, sin = np.cos(ang), np.sin(ang)
    sign = np.where((np.arange(ROT_DIM) % 32) < 16, -1.0, 1.0).astype(np.float32)
    reps = LANES // ROT_DIM
    return jnp.asarray(np.tile(cos, (1, reps))), jnp.asarray(np.tile(sin * sign, (1, reps)))


def _swa_cache_to_feature_major(cache):
    nb, nl, lc, nh, hd = cache.shape
    return cache.transpose(0, 1, 3, 4, 2).reshape(nb, nl * nh * hd, lc)


def _swa_cache_from_feature_major(xt, seq):
    nb = xt.shape[0]
    return xt.reshape(nb, SWA_KV_HEADS, SWA_HD, seq).transpose(0, 3, 1, 2)


def kernel(x_prompt, x_sample, cache_diff_k, cache_diff_v, cache_swa_k, cache_swa_v, c, c_ctx,
           w_mod, b_mod, norm_g, w_qkv_diff, diff_lambda, diff_subln_g, w_o_diff,
           w_qkv_swa, swa_sink, w_o_swa, w_gate, w_up, w_down):
    bp, lp, d = x_prompt.shape
    bs, ls, _ = x_sample.shape
    lc = cache_diff_k.shape[2]
    depth = w_mod.shape[0]
    tm = ROW_TILE

    cond8 = jnp.concatenate([c_ctx[None, :], c, jnp.zeros((8 - 1 - bs, d), F32)], axis=0)
    mods = _modulation(cond8, w_mod, b_mod)
    tables = _rope_tables(ls)

    cdk = cache_diff_k.reshape(bs, -1, 2 * DIFF_HD)
    cdv = cache_diff_v.reshape(bs, -1, 2 * DIFF_HD)
    cskt = _swa_cache_to_feature_major(cache_swa_k)
    csvt = _swa_cache_to_feature_major(cache_swa_v)

    n_p, n_s = bp * lp, bs * ls
    x_parts = (x_prompt.reshape(n_p, d), x_sample.reshape(n_s, d))
    g = norm_g
    sub_g = diff_subln_g.reshape(-1, 1, 2 * DIFF_HD)
    diff_k_out, diff_v_out, swa_k_out, swa_v_out = [], [], [], []

    for i in range(depth):
        j = i // N_MIXERS
        if i % N_MIXERS == 0:
            lam_init = 0.8 - 0.6 * math.exp(-0.3 * i)
            qp, kp, vp, qs, ks, vs = _qkv_diff(x_parts, g, mods, w_qkv_diff, tables, layer=i, j=j,
                                               n_p=n_p, n_s=n_s, tiles_per_request=ls // QKV_ROW_TILE)
            op = _diff_prompt_attention(qp, kp, vp, diff_lambda, sub_g, j=j, seq=lp, lam_init=lam_init)
            diff_k_out.append(kp.reshape(bp, lp, DIFF_HEADS, 2 * DIFF_HD))
            diff_v_out.append(vp.reshape(bp, lp, DIFF_HEADS, 2 * DIFF_HD))
            os_ = _diff_latent_attention(qs, ks, vs, cdk, cdv, diff_lambda, sub_g,
                                         j=j, seq=ls, lc=lc, lam_init=lam_init)
            w_o = w_o_diff
        else:
            qp, ktp, vtp, qs, kds, vds = _qkv_swa(x_parts, g, mods, w_qkv_swa, tables, layer=i, j=j,
                                                  n_p=n_p, n_s=n_s, tiles_per_request=ls // QKV_ROW_TILE,
                                                  seq=lp)
            op = _swa_prompt_attention(qp, ktp, vtp, swa_sink[j], seq=lp)
            swa_k_out.append(_swa_cache_from_feature_major(ktp, lp))
            swa_v_out.append(_swa_cache_from_feature_major(vtp, lp))
            os_ = _swa_latent_attention(qs, kds, vds, cskt, csvt, swa_sink[j], j=j, seq=ls)
            w_o = w_o_swa
        last = i == depth - 1
        out = _post_attn_ffn(op, os_, x_parts, w_o, w_gate, w_up, w_down, g, mods, layer=i, j=j,
                             tiles_per_request=ls // tm, split_out=last)
        x_parts = out if last else (out,)
    xp, xs = x_parts

    return (xp.reshape(bp, lp, d), xs.reshape(bs, ls, d),
            jnp.stack(diff_k_out, axis=1), jnp.stack(diff_v_out, axis=1),
            jnp.stack(swa_k_out, axis=1), jnp.stack(swa_v_out, axis=1))
```

```python
import functools
import math

import jax
import jax.numpy as jnp
import numpy as np
from jax import lax
from jax.experimental import pallas as pl
from jax.experimental.pallas import tpu as pltpu

F32 = jnp.float32
BF16 = jnp.bfloat16

GRID_W = 64
N_MIXERS = 2
DIFF_HEADS = 8
DIFF_HD = 64
SWA_HEADS = 16
SWA_KV_HEADS = 4
SWA_GROUP = SWA_HEADS // SWA_KV_HEADS
SWA_HD = 64
ROT_DIM = 64
WINDOW = 128
ROPE_BASE = 10000.0
EPS = 1e-6
NEG_INF = -1e30

LANES = 128
ROW_TILE = 512
QKV_ROW_TILE = 512
PROMPT_SCORE_BUFFERS = 8
LATENT_SCORE_BUFFERS = 2
VMEM_LIMIT = 48 * 1024 * 1024
FFN_VMEM_LIMIT = 58 * 1024 * 1024
WEIGHT_STAGE_ROWS = 128
WEIGHT_STAGE_SLOTS = 3
WEIGHT_COLUMN_SLOTS = 2
QKV_STAGE_SLOTS = 4
FFN_CHUNK = 256
NT_DIMS = (((1,), (1,)), ((), ()))
LOG2E = math.log2(math.e)


def _params(n_axes):
    return pltpu.CompilerParams(dimension_semantics=("arbitrary",) * n_axes,
                                vmem_limit_bytes=VMEM_LIMIT)


def _layer_resident(shape, layer):
    return pl.BlockSpec((None,) + tuple(shape[1:]), lambda *_: (layer,) + (0,) * (len(shape) - 1),
                        pipeline_mode=pl.Buffered(1))


def _rms(x, g):
    ms = jnp.mean(x * x, axis=-1, keepdims=True)
    return (x * lax.rsqrt(ms + EPS)) * g


def _half_masks(dtype):
    lane = lax.broadcasted_iota(jnp.int32, (1, LANES), 1)
    lo = lane < (LANES // 2)
    return jnp.where(lo, 1.0, 0.0).astype(dtype), jnp.where(lo, 0.0, 1.0).astype(dtype)


def _lo64():
    return lax.broadcasted_iota(jnp.int32, (1, LANES), 1) < (LANES // 2)


def _ones_column(rows):
    return jnp.ones((rows, LANES), BF16)


def _ones_row(cols):
    return jnp.ones((LANES, cols), BF16)


def _mod_kernel(cond_ref, w_ref, b_ref, out_ref):
    c = cond_ref[...]
    s = c * jax.nn.sigmoid(c)
    out_ref[...] = jnp.dot(s.astype(BF16), w_ref[...].astype(BF16),
                           preferred_element_type=F32) + b_ref[pl.ds(pl.program_id(0), 1), :]


def _modulation(cond8, w_mod, b_mod):
    depth, d, n = w_mod.shape
    tn = 1536
    return pl.pallas_call(
        _mod_kernel,
        out_shape=jax.ShapeDtypeStruct((depth, 8, n), F32),
        grid=(depth, n // tn),
        in_specs=[pl.BlockSpec((8, d), lambda i, j: (0, 0)),
                  pl.BlockSpec((None, d, tn), lambda i, j: (i, 0, j)),
                  pl.BlockSpec((depth, tn), lambda i, j: (0, j))],
        out_specs=pl.BlockSpec((None, 8, tn), lambda i, j: (i, 0, j)),
        compiler_params=_params(2),
        name="modulation",
    )(cond8, w_mod, b_mod)


def _rope_slab(xs, cos, sin_signed, lo16):
    left = pltpu.roll(xs, LANES - 16, 1)
    right = pltpu.roll(xs, 16, 1)
    return xs * cos + jnp.where(lo16, left, right) * sin_signed


def _lo16_mask():
    lane = lax.broadcasted_iota(jnp.int32, (1, LANES), 1)
    return (lane % 32) < 16


def _qkv_both_groups(refs, n_prompt_tiles, tiles_per_request, split_x, j, prompt_tile, latent_tile):
    refs = list(refs)
    x_refs = refs[:2] if split_x else refs[:1]
    g_ref, mod_ref, w_hbm, cos_ref, sin_ref = refs[len(x_refs):len(x_refs) + 5]
    outs = refs[len(x_refs) + 5:len(x_refs) + 11]
    w_vmem, stage, sem = refs[len(x_refs) + 11:]
    d = x_refs[0].shape[1]
    i = pl.program_id(0)

    @pl.when(i == 0)
    def _():
        _load_weights_as_bf16([(w_hbm.at[j], w_vmem)], {w_vmem.shape[1]: stage}, {w_vmem.shape[1]: sem})

    def pre_norm(x, r):
        shift, scale = mod_ref[pl.ds(r, 1), 0:d], mod_ref[pl.ds(r, 1), d:2 * d]
        return (_rms(x, g_ref[0:1, :] * (1 + scale)) + shift).astype(BF16)

    @pl.when(i < n_prompt_tiles)
    def _():
        prompt_tile(pre_norm(x_refs[0][...], 0), w_vmem, outs[:3])

    @pl.when(i >= n_prompt_tiles)
    def _():
        r = 1 + (i - n_prompt_tiles) // tiles_per_request
        latent_tile(pre_norm(x_refs[-1][...], r), w_vmem, outs[3:], (cos_ref[...], sin_ref[...], _lo16_mask()))


def _qkv_call(kernel, name, x_parts, g, mods, w, tables, prompt_outs, latent_outs, *, layer, n_p, n_s):
    d = x_parts[0].shape[1]
    tm = QKV_ROW_TILE
    tp, ts = n_p // tm, n_s // tm
    nt = tables[0].shape[0] // tm

    def prompt_block(shape):
        return pl.BlockSpec(shape, lambda i: (jnp.minimum(i, tp - 1),) + (0,) * (len(shape) - 1))

    def latent_block(shape):
        return pl.BlockSpec(shape, lambda i: (jnp.maximum(i - tp, 0),) + (0,) * (len(shape) - 1))

    if len(x_parts) == 2:
        x_specs = [prompt_block((tm, d)), latent_block((tm, d))]
    else:
        x_specs = [pl.BlockSpec((tm, d), lambda i: (i, 0))]
    table_spec = pl.BlockSpec((tm, LANES), lambda i: (jnp.maximum(i - tp, 0) % nt, 0))
    in_specs = x_specs + [_layer_resident(g.shape, layer), _layer_resident(mods.shape, layer),
                          pl.BlockSpec(memory_space=pl.ANY), table_spec, table_spec]
    out_shape, out_specs = [], []
    for outs, n, block in ((prompt_outs, n_p, prompt_block), (latent_outs, n_s, latent_block)):
        for shape_of, dtype in outs:
            out_shape.append(jax.ShapeDtypeStruct(shape_of(n), dtype))
            out_specs.append(block(shape_of(tm)))
    cols = w.shape[2]
    scratch = [pltpu.VMEM((d, cols), BF16), pltpu.VMEM((QKV_STAGE_SLOTS, WEIGHT_STAGE_ROWS, cols), F32),
               pltpu.SemaphoreType.DMA((QKV_STAGE_SLOTS,))]
    return pl.pallas_call(
        kernel,
        out_shape=tuple(out_shape),
        grid=(tp + ts,),
        in_specs=in_specs,
        out_specs=tuple(out_specs),
        scratch_shapes=scratch,
        compiler_params=_params(1),
        name=name,
    )(*x_parts, g, mods, w, *tables)


def _diff_tile(h, w_ref, outs, rope=None):
    q_ref, k_ref, v_ref = outs
    d = h.shape[1]
    cw = 512
    for c in range(3 * d // cw):
        acc = jnp.dot(h, w_ref[:, c * cw:(c + 1) * cw], preferred_element_type=F32)
        which, off = divmod(c * cw, d)
        dst = outs[which]
        for s in range(cw // LANES):
            xs = acc[:, s * LANES:(s + 1) * LANES]
            if rope is not None and which < 2:
                xs = _rope_slab(xs, *rope)
            if which == 0:
                xs = xs * (DIFF_HD ** -0.5 * LOG2E)
            lo = off + s * LANES
            if rope is not None or which == 0:
                dst[:, lo:lo + LANES] = xs.astype(dst.dtype)
            else:
                dst[pl.ds(lo // LANES, h.shape[0], stride=DIFF_HEADS), :] = xs


def _qkv_diff_kernel(*refs, j, n_prompt_tiles, tiles_per_request, split_x):
    _qkv_both_groups(refs, n_prompt_tiles, tiles_per_request, split_x, j, _diff_tile, _diff_tile)


def _qkv_diff(x_parts, g, mods, w, tables, *, layer, j, n_p, n_s, tiles_per_request):
    d = x_parts[0].shape[1]

    def rows(n):
        return (n, d)

    def cache(n):
        return (n * DIFF_HEADS, d // DIFF_HEADS)

    kernel = functools.partial(_qkv_diff_kernel, j=j, n_prompt_tiles=n_p // QKV_ROW_TILE,
                               tiles_per_request=tiles_per_request, split_x=len(x_parts) == 2)
    return _qkv_call(kernel, "qkv_diff", x_parts, g, mods, w, tables,
                     [(rows, BF16), (cache, F32), (cache, F32)], [(rows, BF16)] * 3,
                     layer=layer, n_p=n_p, n_s=n_s)


def _swa_q_tile(h, w_ref, q_ref, rope):
    d = h.shape[1]
    cw = 512
    for c in range(d // cw):
        acc = jnp.dot(h, w_ref[:, c * cw:(c + 1) * cw], preferred_element_type=F32)
        for s in range(cw // LANES):
            xs = acc[:, s * LANES:(s + 1) * LANES]
            if rope is not None:
                xs = _rope_slab(xs, *rope)
            lo = c * cw + s * LANES
            q_ref[:, lo:lo + LANES] = (xs * (SWA_HD ** -0.5 * LOG2E)).astype(BF16)
    nkv = SWA_KV_HEADS * SWA_HD
    return jnp.dot(h, w_ref[:, d:d + 2 * nkv], preferred_element_type=F32), nkv


def _swa_prompt_tile(h, w_ref, outs, *, seq):
    q_ref, kt_ref, vt_ref = outs
    kv, nkv = _swa_q_tile(h, w_ref, q_ref, None)
    for b in range(h.shape[0] // seq):
        kt_ref[b] = kv[b * seq:(b + 1) * seq, :nkv].T
        vt_ref[b] = kv[b * seq:(b + 1) * seq, nkv:].T


def _swa_latent_tile(h, w_ref, outs, rope):
    q_ref, kd_ref, vd_ref = outs
    kv, nkv = _swa_q_tile(h, w_ref, q_ref, rope)
    lo64 = _lo64()
    for which, dst in enumerate((kd_ref, vd_ref)):
        for s in range(nkv // LANES):
            xs = kv[:, which * nkv + s * LANES: which * nkv + (s + 1) * LANES]
            if which == 0:
                xs = _rope_slab(xs, *rope)
            sw = pltpu.roll(xs, LANES // 2, 1)
            dst[:, (2 * s) * LANES:(2 * s + 1) * LANES] = jnp.where(lo64, xs, sw).astype(BF16)
            dst[:, (2 * s + 1) * LANES:(2 * s + 2) * LANES] = jnp.where(lo64, sw, xs).astype(BF16)


def _qkv_swa_kernel(*refs, j, n_prompt_tiles, tiles_per_request, split_x, seq):
    _qkv_both_groups(refs, n_prompt_tiles, tiles_per_request, split_x, j,
                     functools.partial(_swa_prompt_tile, seq=seq), _swa_latent_tile)


def _qkv_swa(x_parts, g, mods, w, tables, *, layer, j, n_p, n_s, tiles_per_request, seq):
    d = x_parts[0].shape[1]
    nkv = SWA_KV_HEADS * SWA_HD

    def rows(n):
        return (n, d)

    def feature_major(n):
        return (n // seq, nkv, seq)

    def duplicated(n):
        return (n, 2 * nkv)

    kernel = functools.partial(_qkv_swa_kernel, j=j, n_prompt_tiles=n_p // QKV_ROW_TILE,
                               tiles_per_request=tiles_per_request, split_x=len(x_parts) == 2, seq=seq)
    return _qkv_call(kernel, "qkv_swa", x_parts, g, mods, w, tables,
                     [(rows, BF16), (feature_major, F32), (feature_major, F32)],
                     [(rows, BF16), (duplicated, BF16), (duplicated, BF16)],
                     layer=layer, n_p=n_p, n_s=n_s)


def _diff_lambda(lam_ref, lam_init):
    lp = lam_ref[...]
    a = jnp.sum(lp[0:1] * lp[1:2], axis=-1, keepdims=True)
    b = jnp.sum(lp[2:3] * lp[3:4], axis=-1, keepdims=True)
    return jnp.exp(a) - jnp.exp(b) + lam_init


def _diff_combine(acc, tq, lam, g, lam_init):
    o12 = acc[:, :LANES] / acc[:, LANES:]
    o = o12[:tq] - lam * o12[tq:]
    return _rms(o, g) * (1.0 - lam_init)


def _stack_maps(q):
    m_lo, m_hi = _half_masks(BF16)
    return jnp.concatenate([q * m_lo, q * m_hi], axis=0)


def _run_pipelined(items, scores, finish, s_bufs):
    depth = len(s_bufs)
    states = {i: scores(items[i], s_bufs[i]) for i in range(min(depth - 1, len(items)))}
    for i, item in enumerate(items):
        ahead = i + depth - 1
        if ahead < len(items):
            states[ahead] = scores(items[ahead], s_bufs[ahead % depth])
        finish(item, s_bufs[i % depth], states.pop(i))


def _store_scores(s_ref, col0, s, mrun):
    s_ref[:, col0:col0 + s.shape[1]] = s
    for t in range(s.shape[1] // LANES):
        blk = s[:, t * LANES:(t + 1) * LANES]
        mrun = blk if mrun is None else jnp.maximum(mrun, blk)
    return mrun


def _exp_block(s_ref, col0, width, mb):
    return jnp.concatenate(
        [jnp.exp2(s_ref[:, col0 + t * LANES:col0 + (t + 1) * LANES] - mb).astype(BF16)
         for t in range(width // LANES)], axis=1)


def _diff_prompt_kernel(q_ref, k_ref, v_ref, lam_ref, g_ref, o_ref, *s_bufs, lam_init, seq):
    lam = _diff_lambda(lam_ref, lam_init)
    g = g_ref[...]
    ones = _ones_column(seq)
    items = [(r, h) for r in range(q_ref.shape[0] // seq) for h in range(DIFF_HEADS)]

    def head_rows(ref, r, h):
        return ref[pl.ds(r * seq * DIFF_HEADS + h, seq, stride=DIFF_HEADS), :]

    def scores(item, s_ref):
        r, h = item
        rows, sl = slice(r * seq, (r + 1) * seq), slice(h * LANES, (h + 1) * LANES)
        s = lax.dot_general(_stack_maps(q_ref[rows, sl]), head_rows(k_ref, r, h).astype(BF16), NT_DIMS,
                            preferred_element_type=F32)
        return _store_scores(s_ref, 0, s, None)

    def finish(item, s_ref, mrun):
        r, h = item
        rows, sl = slice(r * seq, (r + 1) * seq), slice(h * LANES, (h + 1) * LANES)
        mb = jnp.broadcast_to(jnp.max(mrun, axis=-1, keepdims=True), (2 * seq, LANES))
        vx = jnp.concatenate([head_rows(v_ref, r, h).astype(BF16), ones], axis=1)
        acc = jnp.dot(_exp_block(s_ref, 0, seq, mb), vx, preferred_element_type=F32)
        o_ref[rows, sl] = _diff_combine(acc, seq, lam, g, lam_init).astype(BF16)

    _run_pipelined(items, scores, finish, s_bufs)


def _diff_prompt_attention(q, k, v, lam_params, subln_g, *, j, seq, lam_init):
    n, d = q.shape
    req = 2
    spec = pl.BlockSpec((req * seq, d), lambda b: (b, 0))
    kv_spec = pl.BlockSpec((req * seq * DIFF_HEADS, d // DIFF_HEADS), lambda b: (b, 0))
    return pl.pallas_call(
        functools.partial(_diff_prompt_kernel, lam_init=lam_init, seq=seq),
        out_shape=jax.ShapeDtypeStruct((n, d), BF16),
        grid=(n // (req * seq),),
        in_specs=[spec, kv_spec, kv_spec, _layer_resident(lam_params.shape, j),
                  _layer_resident(subln_g.shape, j)],
        out_specs=spec,
        scratch_shapes=[pltpu.VMEM((2 * seq, seq), F32)] * PROMPT_SCORE_BUFFERS,
        compiler_params=_params(1),
        name="diff_attn_prompt",
    )(q, k, v, lam_params, subln_g)


def _diff_latent_kernel(q_ref, kc_ref, vc_ref, kl_ref, vl_ref, qn_ref, kcn_ref, kln_ref, lam_ref, g_ref, o_ref,
                        kk_ref, vx_ref, m_ref, s0_ref, s1_ref, *, lam_init, lc, tq, key_chunk):
    seq = q_ref.shape[0]
    nk = kk_ref.shape[0]
    head = pl.program_id(1)
    step = pl.program_id(0) * pl.num_programs(1) + head
    last_step = pl.num_programs(0) * pl.num_programs(1) - 1
    next_head = jnp.minimum(step + 1, last_step) % DIFF_HEADS
    masks = _half_masks(BF16)
    nchunk = nk // key_chunk
    s_bufs = (s0_ref, s1_ref)

    def fill_keys(cache_ref, h, latent_ref):
        kk_ref[0:lc, :] = cache_ref[pl.ds(h, lc, stride=DIFF_HEADS), :].astype(BF16)
        kk_ref[lc:, :] = latent_ref[...]

    def scores(q_rows, m, s_ref):
        q = q_rows * masks[m]
        mrun = None
        for c in range(nchunk):
            s = lax.dot_general(q, kk_ref[c * key_chunk:(c + 1) * key_chunk, :], NT_DIMS,
                                preferred_element_type=F32)
            mrun = _store_scores(s_ref, c * key_chunk, s, mrun)
        return mrun

    @pl.when(step == 0)
    def _():
        fill_keys(kc_ref, head, kl_ref)
        m_ref[...] = scores(q_ref[0:tq, :], 0, s0_ref)

    vx_ref[0:lc, 0:LANES] = vc_ref[pl.ds(head, lc, stride=DIFF_HEADS), :].astype(BF16)
    vx_ref[lc:, 0:LANES] = vl_ref[...]
    vx_ref[:, LANES:2 * LANES] = _ones_column(nk)

    items = [(rt, m) for rt in range(seq // tq) for m in range(2)]
    lam = _diff_lambda(lam_ref, lam_init)
    g = g_ref[...]
    first_map = {}

    def finish(item, s_ref, mrun):
        rt, m = item
        mb = jnp.broadcast_to(jnp.max(mrun, axis=-1, keepdims=True), (tq, LANES))
        acc = None
        for c in range(nchunk):
            part = jnp.dot(_exp_block(s_ref, c * key_chunk, key_chunk, mb),
                           vx_ref[c * key_chunk:(c + 1) * key_chunk, :], preferred_element_type=F32)
            acc = part if acc is None else acc + part
        o_m = acc[:, :LANES] / acc[:, LANES:]
        if m == 0:
            first_map[rt] = o_m
        else:
            o = _rms(first_map.pop(rt) - lam * o_m, g) * (1.0 - lam_init)
            o_ref[rt * tq:(rt + 1) * tq, :] = o.astype(BF16)

    mrun = m_ref[...]
    for i, item in enumerate(items):
        s_ahead = s_bufs[(i + 1) % 2]
        if i + 1 < len(items):
            rt, m = items[i + 1]
            ahead = scores(q_ref[rt * tq:(rt + 1) * tq, :], m, s_ahead)
        else:
            fill_keys(kcn_ref, next_head, kln_ref)
            ahead = scores(qn_ref[...], 0, s_ahead)
            m_ref[...] = ahead
        finish(item, s_bufs[i % 2], mrun)
        mrun = ahead


def _diff_latent_attention(q, k, v, cache_k, cache_v, lam_params, subln_g, *, j, seq, lc, lam_init):
    n, d = q.shape
    nb = cache_k.shape[0]
    tq = 512
    key_chunk = 512
    last_step = nb * DIFF_HEADS - 1

    def next_step(b, h):
        return divmod(jnp.minimum(b * DIFF_HEADS + h + 1, last_step), DIFF_HEADS)

    def next_query_tile(b, h):
        nb_, nh = next_step(b, h)
        return nb_ * (seq // tq), nh

    q_spec = pl.BlockSpec((seq, LANES), lambda b, h: (b, h))
    c_spec = pl.BlockSpec((None, lc * DIFF_HEADS, LANES), lambda b, h: (b, j, 0))
    qn_spec = pl.BlockSpec((tq, LANES), next_query_tile)
    kln_spec = pl.BlockSpec((seq, LANES), next_step)
    cn_spec = pl.BlockSpec((None, lc * DIFF_HEADS, LANES), lambda b, h: (next_step(b, h)[0], j, 0))
    return pl.pallas_call(
        functools.partial(_diff_latent_kernel, lam_init=lam_init, lc=lc, tq=tq, key_chunk=key_chunk),
        out_shape=jax.ShapeDtypeStruct((n, d), BF16),
        grid=(nb, DIFF_HEADS),
        in_specs=[q_spec, c_spec, c_spec, q_spec, q_spec, qn_spec, cn_spec, kln_spec,
                  _layer_resident(lam_params.shape, j), _layer_resident(subln_g.shape, j)],
        out_specs=q_spec,
        scratch_shapes=[pltpu.VMEM((lc + seq, LANES), BF16),
                        pltpu.VMEM((lc + seq, 2 * LANES), BF16),
                        pltpu.VMEM((tq, LANES), F32),
                        *[pltpu.VMEM((tq, lc + seq), F32)] * LATENT_SCORE_BUFFERS],
        compiler_params=_params(2),
        name="diff_attn_latent",
    )(q, cache_k, cache_v, k, v, q, cache_k, k, lam_params, subln_g)


def _stack_group(q_ref, rows, kv_local):
    m_lo, m_hi = _half_masks(BF16)
    parts = []
    for gb in range(SWA_GROUP // 2):
        blk = kv_local * (SWA_GROUP // 2) + gb
        qb = q_ref[rows, blk * LANES:(blk + 1) * LANES]
        parts += [qb * m_lo, qb * m_hi]
    return jnp.concatenate(parts, axis=0)


def _sink_column(sink_ref, first_head, tq):
    return jnp.concatenate([jnp.full((tq, LANES), sink_ref[first_head + g] * LOG2E, F32)
                            for g in range(SWA_GROUP)], axis=0)


def _write_group(o_ref, rows, kv_local, o, tq):
    lo64 = _lo64()
    for gb in range(SWA_GROUP // 2):
        blk = kv_local * (SWA_GROUP // 2) + gb
        even = o[(2 * gb) * tq:(2 * gb + 1) * tq]
        odd = o[(2 * gb + 1) * tq:(2 * gb + 2) * tq]
        o_ref[rows, blk * LANES:(blk + 1) * LANES] = jnp.where(lo64, even, odd).astype(BF16)


def _dup_rows(x_t):
    xb = x_t.astype(BF16)
    return jnp.concatenate([xb, xb], axis=0)


def _sink_finish(mrun, sk, rows):
    mb = jnp.maximum(jnp.broadcast_to(jnp.max(mrun, axis=-1, keepdims=True), (rows, LANES)), sk)
    return mb, jnp.exp2(sk - mb)


def _swa_prompt_kernel(sink_ref, q_ref, kt_ref, vt_ref, o_ref, *s_bufs):
    seq = kt_ref.shape[2]
    rows = SWA_GROUP * seq
    ones = _ones_row(seq)
    items = [(r, j) for r in range(kt_ref.shape[0]) for j in range(SWA_KV_HEADS)]

    def scores(item, s_ref):
        r, j = item
        kd = _dup_rows(kt_ref[r, j * SWA_HD:(j + 1) * SWA_HD, :])
        s = jnp.dot(_stack_group(q_ref, slice(r * seq, (r + 1) * seq), j), kd, preferred_element_type=F32)
        return _store_scores(s_ref, 0, s, None)

    def finish(item, s_ref, mrun):
        r, j = item
        sk = _sink_column(sink_ref, j * SWA_GROUP, seq)
        mb, sink_term = _sink_finish(mrun, sk, rows)
        vx = jnp.concatenate([_dup_rows(vt_ref[r, j * SWA_HD:(j + 1) * SWA_HD, :]), ones], axis=0)
        acc = lax.dot_general(_exp_block(s_ref, 0, seq, mb), vx, NT_DIMS, preferred_element_type=F32)
        o = acc[:, :LANES] / (acc[:, LANES:] + sink_term)
        _write_group(o_ref, slice(r * seq, (r + 1) * seq), j, o, seq)

    _run_pipelined(items, scores, finish, s_bufs)


def _swa_prompt_attention(q, kt, vt, sink, *, seq):
    n, d = q.shape
    nkv = kt.shape[1]
    req = 2
    t_spec = pl.BlockSpec((req, nkv, seq), lambda b: (b, 0, 0))
    return pl.pallas_call(
        _swa_prompt_kernel,
        out_shape=jax.ShapeDtypeStruct((n, d), BF16),
        grid=(n // (req * seq),),
        in_specs=[pl.BlockSpec(memory_space=pltpu.SMEM),
                  pl.BlockSpec((req * seq, d), lambda b: (b, 0)), t_spec, t_spec],
        out_specs=pl.BlockSpec((req * seq, d), lambda b: (b, 0)),
        scratch_shapes=[pltpu.VMEM((SWA_GROUP * seq, seq), F32)] * (PROMPT_SCORE_BUFFERS // 2),
        compiler_params=_params(1),
        name="swa_attn_prompt",
    )(sink, q, kt, vt)


def _swa_latent_kernel(sink_ref, q_ref, kc_ref, vc_ref, kl_ref, vl_ref, qn_ref, kcn_ref, kln_ref, o_ref,
                       kcd_ref, vcx_ref, m_ref, s0_ref, s1_ref, *, tq, span):
    pair, part = pl.program_id(1), pl.program_id(2)
    n_parts = pl.num_programs(2)
    tiles = q_ref.shape[0] // tq
    seq = kl_ref.shape[0]
    lc = kc_ref.shape[1]
    rows = SWA_GROUP * tq
    step = (pl.program_id(0) * pl.num_programs(1) + pair) * n_parts + part
    last_step = pl.num_programs(0) * pl.num_programs(1) * n_parts - 1
    next_part = jnp.minimum(step + 1, last_step) % n_parts
    ones_row = _ones_row(lc)
    for jj in range(2):
        kcd_ref[jj] = _dup_rows(kc_ref[jj * SWA_HD:(jj + 1) * SWA_HD, :])
        vcx_ref[jj] = jnp.concatenate([_dup_rows(vc_ref[jj * SWA_HD:(jj + 1) * SWA_HD, :]), ones_row], axis=0)
    ones_col = _ones_column(span)
    items = [(t, jj) for t in range(tiles) for jj in range(2)]
    s_bufs = (s0_ref, s1_ref)
    windows, biases = {}, {}

    def window_of(part, t):
        q0 = (part * tiles + t) * tq
        return q0, pl.multiple_of(jnp.clip(q0 - WINDOW, 0, seq - span), WINDOW)

    def band_bias(q0, ws):
        q_pos = q0 + lax.broadcasted_iota(jnp.int32, (tq, span), 0)
        k_pos = ws + lax.broadcasted_iota(jnp.int32, (tq, span), 1)
        b = jnp.where(jnp.abs(q_pos - k_pos) <= WINDOW, 0.0, NEG_INF).astype(F32)
        return jnp.concatenate([b] * SWA_GROUP, axis=0)

    def window(t):
        if t not in windows:
            windows[t] = window_of(part, t)
        return windows[t]

    def bias_for(t):
        if t not in biases:
            biases[t] = band_bias(*window(t))
        return biases[t]

    def score_tile(qs, cache_keys, window_keys, bias, s_ref):
        s_c = jnp.dot(qs, cache_keys, preferred_element_type=F32)
        mrun = _store_scores(s_ref, 0, s_c, None)
        s_w = lax.dot_general(qs, window_keys, NT_DIMS, preferred_element_type=F32) + bias
        return _store_scores(s_ref, lc, s_w, mrun)

    def scores(item, s_ref):
        t, jj = item
        _, ws = window(t)
        return score_tile(_stack_group(q_ref, slice(t * tq, (t + 1) * tq), jj), kcd_ref[jj],
                          kl_ref[pl.ds(ws, span), jj * LANES:(jj + 1) * LANES], bias_for(t), s_ref)

    def finish(item, s_ref, mrun):
        t, jj = item
        _, ws = window(t)
        sk = _sink_column(sink_ref, (2 * pair + jj) * SWA_GROUP, tq)
        mb, sink_term = _sink_finish(mrun, sk, rows)
        vwx = jnp.concatenate([vl_ref[pl.ds(ws, span), jj * LANES:(jj + 1) * LANES], ones_col], axis=1)
        acc = (lax.dot_general(_exp_block(s_ref, 0, lc, mb), vcx_ref[jj], NT_DIMS, preferred_element_type=F32)
               + jnp.dot(_exp_block(s_ref, lc, span, mb), vwx, preferred_element_type=F32))
        o = acc[:, :LANES] / (acc[:, LANES:] + sink_term)
        _write_group(o_ref, slice(t * tq, (t + 1) * tq), jj, o, tq)

    @pl.when(step == 0)
    def _():
        q0, ws = window_of(part, 0)
        m_ref[...] = score_tile(_stack_group(q_ref, slice(0, tq), 0), kcd_ref[0],
                                kl_ref[pl.ds(ws, span), 0:LANES], band_bias(q0, ws), s0_ref)

    mrun = m_ref[...]
    for i, item in enumerate(items):
        s_ahead = s_bufs[(i + 1) % 2]
        if i + 1 < len(items):
            ahead = scores(items[i + 1], s_ahead)
        else:
            q0, ws = window_of(next_part, 0)
            ahead = score_tile(_stack_group(qn_ref, slice(0, tq), 0), _dup_rows(kcn_ref[...]),
                               kln_ref[pl.ds(ws, span), :], band_bias(q0, ws), s_ahead)
            m_ref[...] = ahead
        finish(item, s_bufs[i % 2], mrun)
        mrun = ahead


def _swa_latent_attention(q, kd, vd, cache_kt, cache_vt, sink, *, j, seq):
    n, d = q.shape
    nb, _, lc = cache_kt.shape
    tq = 256
    span = tq + 2 * WINDOW
    npair = SWA_KV_HEADS // 2
    wq = d // npair
    parts = 2
    tiles = seq // parts // tq
    last_step = nb * npair * parts - 1

    def next_step(b, p, i):
        rest, ni = divmod(jnp.minimum((b * npair + p) * parts + i + 1, last_step), parts)
        return (*divmod(rest, npair), ni)

    def next_query_tile(b, p, i):
        nb_, np_, ni = next_step(b, p, i)
        return (nb_ * parts + ni) * tiles, 2 * np_

    def next_cache_head(b, p, i):
        nb_, np_, _ = next_step(b, p, i)
        return nb_, 2 * (j * npair + np_), 0

    def next_latent_head(b, p, i):
        nb_, np_, _ = next_step(b, p, i)
        return nb_, 2 * np_

    q_spec = pl.BlockSpec((seq // parts, wq), lambda b, p, i: (b * parts + i, p))
    c_spec = pl.BlockSpec((None, 2 * SWA_HD, lc), lambda b, p, i: (b, j * npair + p, 0))
    l_spec = pl.BlockSpec((seq, 2 * LANES), lambda b, p, i: (b, p))
    qn_spec = pl.BlockSpec((tq, 2 * LANES), next_query_tile)
    cn_spec = pl.BlockSpec((None, SWA_HD, lc), next_cache_head)
    ln_spec = pl.BlockSpec((seq, LANES), next_latent_head)
    s_shape = pltpu.VMEM((SWA_GROUP * tq, lc + span), F32)
    return pl.pallas_call(
        functools.partial(_swa_latent_kernel, tq=tq, span=span),
        out_shape=jax.ShapeDtypeStruct((n, d), BF16),
        grid=(nb, npair, parts),
        in_specs=[pl.BlockSpec(memory_space=pltpu.SMEM), q_spec, c_spec, c_spec, l_spec, l_spec,
                  qn_spec, cn_spec, ln_spec],
        out_specs=q_spec,
        scratch_shapes=[pltpu.VMEM((2, 2 * SWA_HD, lc), BF16), pltpu.VMEM((2, 2 * LANES, lc), BF16),
                        pltpu.VMEM((SWA_GROUP * tq, LANES), F32), s_shape, s_shape],
        compiler_params=_params(3),
        name="swa_attn_latent",
    )(sink, q, cache_kt, cache_vt, kd, vd, q, cache_kt, kd)


def _load_weights_as_bf16(jobs, stages, sems):
    order = []
    rings = {w: [] for w in stages}
    for src, dst in jobs:
        w = src.shape[1]
        slots = stages[w].shape[0]
        for k in range(src.shape[0] // WEIGHT_STAGE_ROWS):
            rows = pl.ds(k * WEIGHT_STAGE_ROWS, WEIGHT_STAGE_ROWS)
            slot = len(rings[w]) % slots
            copy = pltpu.make_async_copy(src.at[rows, :], stages[w].at[slot], sems[w].at[slot])
            order.append((w, len(rings[w])))
            rings[w].append((copy, slot, dst, rows))
    for w, ring in rings.items():
        for copy, _, _, _ in ring[:stages[w].shape[0]]:
            copy.start()
    for w, k in order:
        copy, slot, dst, rows = rings[w][k]
        copy.wait()
        dst[rows, :] = stages[w][slot].astype(BF16)
        ahead = k + stages[w].shape[0]
        if ahead < len(rings[w]):
            rings[w][ahead][0].start()


def _weight_stream(pieces, rings):
    per_ring = {name: [] for name in rings}
    plan = []
    for src, dst, idx, name in pieces:
        stage, sem = rings[name]
        slot = len(per_ring[name]) % stage.shape[0]
        view = stage.at[slot].at[0:src.shape[0], 0:src.shape[1]]
        plan.append((name, len(per_ring[name])))
        per_ring[name].append((pltpu.make_async_copy(src, view, sem.at[slot]), view, dst, idx))
    cursor = [0]

    def prime():
        for name, ring in per_ring.items():
            for copy, _, _, _ in ring[:rings[name][0].shape[0]]:
                copy.start()

    def take(n):
        for name, k in plan[cursor[0]:cursor[0] + n]:
            copy, view, dst, idx = per_ring[name][k]
            copy.wait()
            dst[idx] = view[...].astype(BF16)
            ahead = k + rings[name][0].shape[0]
            if ahead < len(per_ring[name]):
                per_ring[name][ahead][0].start()
        cursor[0] += n

    return prime, take


def _post_attn_ffn_kernel(*refs, layer, j, n_prompt_tiles, tiles_per_request, split_x, split_out,
                          overlap_first_tile):
    refs = list(refs)
    o_first_ref, op_ref, os_ref = refs[:3]
    refs = refs[1:]
    x_refs = refs[2:4] if split_x else refs[2:3]
    wo_hbm, wg_hbm, wu_hbm, wd_hbm, g_ref, mod_ref = refs[2 + len(x_refs):8 + len(x_refs)]
    n_out = 2 if split_out else 1
    out_refs = refs[8 + len(x_refs):8 + len(x_refs) + n_out]
    wo_ref, wg_ref, wu_ref, wd_ref, stage_row, stage_col, sem_row, sem_col, y_ref = refs[8 + len(x_refs) + n_out:]
    d = wo_ref.shape[1]
    dff = wg_ref.shape[1]
    cw = FFN_CHUNK
    i = pl.program_id(0)
    is_prompt = i < n_prompt_tiles
    r = jnp.where(is_prompt, 0, 1 + (i - n_prompt_tiles) // tiles_per_request)

    def mod(slot):
        return mod_ref[pl.ds(r, 1), slot * d:(slot + 1) * d]

    def out_projection(o):
        y_ref[...] = jnp.dot(o, wo_ref[...], preferred_element_type=F32)

    def tile(before_chunk):
        x = jnp.where(is_prompt, x_refs[0][...], x_refs[1][...]) if split_x else x_refs[0][...]
        x = x + _rms(y_ref[...], mod(2) * g_ref[1:2, :])
        h = (_rms(x, g_ref[2:3, :] * (1 + mod(4))) + mod(3)).astype(BF16)
        next_is_prompt = jnp.minimum(i + 1, pl.num_programs(0) - 1) < n_prompt_tiles
        out_projection(jnp.where(next_is_prompt, op_ref[...], os_ref[...]))
        y = jnp.zeros((h.shape[0], d), F32)
        for c in range(dff // cw):
            before_chunk(c)
            cols = slice(c * cw, (c + 1) * cw)
            a = jnp.dot(h, wg_ref[:, cols], preferred_element_type=F32)
            u = jnp.dot(h, wu_ref[:, cols], preferred_element_type=F32)
            t = (a * jax.nn.sigmoid(a)) * u
            y = y + jnp.dot(t.astype(BF16), wd_ref[cols, :], preferred_element_type=F32)
        out = x + _rms(y, mod(5) * g_ref[3:4, :])
        if split_out:
            @pl.when(is_prompt)
            def _():
                out_refs[0][...] = out

            @pl.when(jnp.logical_not(is_prompt))
            def _():
                out_refs[1][...] = out
        else:
            out_refs[0][...] = out

    @pl.when(i == 0)
    def _():
        row_chunk = stage_row.shape[1]
        pieces = [(wo_hbm.at[j].at[pl.ds(k * row_chunk, row_chunk), :], wo_ref,
                   (pl.ds(k * row_chunk, row_chunk), slice(None)), "row") for k in range(d // row_chunk)]
        col_chunk = stage_col.shape[2]
        per_take = [d // row_chunk]
        for c in range(dff // cw):
            n = 0
            if (c * cw) % col_chunk == 0:
                cols = pl.ds(c * cw, min(col_chunk, dff - c * cw))
                pieces += [(wg_hbm.at[layer].at[:, cols], wg_ref, (slice(None), cols), "col"),
                           (wu_hbm.at[layer].at[:, cols], wu_ref, (slice(None), cols), "col")]
                n += 2
            rows = pl.ds(c * cw, cw)
            pieces.append((wd_hbm.at[layer].at[rows, :], wd_ref, (rows, slice(None)), "row"))
            per_take.append(n + 1)
        prime, take = _weight_stream(pieces, {"row": (stage_row, sem_row), "col": (stage_col, sem_col)})
        prime()
        if overlap_first_tile:
            take(per_take[0])
            out_projection(o_first_ref[...])
            tile(lambda c: take(per_take[c + 1]))
        else:
            take(len(pieces))
            out_projection(o_first_ref[...])

    if overlap_first_tile:
        @pl.when(i > 0)
        def _():
            tile(lambda c: None)
    else:
        tile(lambda c: None)


def _post_attn_ffn(o_p, o_s, xs_in, w_o, wg, wu, wd, g, mods, *, layer, j, tiles_per_request, split_out):
    n_p, d = o_p.shape
    n_s = o_s.shape[0]
    tm = ROW_TILE
    tp, ts = n_p // tm, n_s // tm
    prompt_rows = pl.BlockSpec((tm, d), lambda i: (jnp.minimum(i, tp - 1), 0))
    latent_rows = pl.BlockSpec((tm, d), lambda i: (jnp.maximum(i - tp, 0), 0))
    all_rows = pl.BlockSpec((tm, d), lambda i: (i, 0))
    first_rows = pl.BlockSpec((tm, d), lambda i: (0, 0), pipeline_mode=pl.Buffered(1))
    next_prompt_rows = pl.BlockSpec((tm, d), lambda i: (jnp.minimum(i + 1, tp - 1), 0))
    next_latent_rows = pl.BlockSpec((tm, d), lambda i: (jnp.clip(i + 1 - tp, 0, ts - 1), 0))
    split_x = len(xs_in) == 2
    in_specs = [first_rows, next_prompt_rows, next_latent_rows]
    in_specs += [prompt_rows, latent_rows] if split_x else [all_rows]
    hbm = pl.BlockSpec(memory_space=pl.ANY)
    in_specs += [hbm, hbm, hbm, hbm, _layer_resident(g.shape, layer), _layer_resident(mods.shape, layer)]
    if split_out:
        out_shape = (jax.ShapeDtypeStruct((n_p, d), F32), jax.ShapeDtypeStruct((n_s, d), F32))
        out_specs = (prompt_rows, latent_rows)
    else:
        out_shape = jax.ShapeDtypeStruct((n_p + n_s, d), F32)
        out_specs = all_rows
    dff = wg.shape[2]
    scratch = [pltpu.VMEM((d, d), BF16), pltpu.VMEM((d, dff), BF16), pltpu.VMEM((d, dff), BF16),
               pltpu.VMEM((dff, d), BF16),
               pltpu.VMEM((WEIGHT_STAGE_SLOTS, FFN_CHUNK, d), F32),
               pltpu.VMEM((WEIGHT_COLUMN_SLOTS, d, 2 * FFN_CHUNK), F32),
               pltpu.SemaphoreType.DMA((WEIGHT_STAGE_SLOTS,)), pltpu.SemaphoreType.DMA((WEIGHT_COLUMN_SLOTS,)),
               pltpu.VMEM((tm, d), F32)]
    return pl.pallas_call(
        functools.partial(_post_attn_ffn_kernel, layer=layer, j=j, n_prompt_tiles=tp,
                          tiles_per_request=tiles_per_request, split_x=split_x, split_out=split_out,
                          overlap_first_tile=not split_x),
        out_shape=out_shape,
        grid=(tp + ts,),
        in_specs=in_specs,
        out_specs=out_specs,
        scratch_shapes=scratch,
        compiler_params=pltpu.CompilerParams(dimension_semantics=("arbitrary",),
                                             vmem_limit_bytes=FFN_VMEM_LIMIT),
        name="post_attn_ffn",
    )(o_p, o_p, o_s, *xs_in, w_o, wg, wu, wd, g, mods)


def _rope_tables(n_lat):
    t = np.arange(n_lat)
    row = (t // GRID_W).astype(np.float32)
    col = (t % GRID_W).astype(np.float32)
    nf = ROT_DIM // 4
    inv = np.float32(ROPE_BASE) ** (-np.arange(nf, dtype=np.float32) / np.float32(nf))
    ar = row[:, None] * inv[None, :]
    ac = col[:, None] * inv[None, :]
    ang = np.concatenate([ar, ar, ac, ac], axis=-1)
    cos, sin = np.cos(ang), np.sin(ang)
    sign = np.where((np.arange(ROT_DIM) % 32) < 16, -1.0, 1.0).astype(np.float32)
    reps = LANES // ROT_DIM
    return jnp.asarray(np.tile(cos, (1, reps))), jnp.asarray(np.tile(sin * sign, (1, reps)))


def _swa_cache_to_feature_major(cache):
    nb, nl, lc, nh, hd = cache.shape
    return cache.transpose(0, 1, 3, 4, 2).reshape(nb, nl * nh * hd, lc)


def _swa_cache_from_feature_major(xt, seq):
    nb = xt.shape[0]
    return xt.reshape(nb, SWA_KV_HEADS, SWA_HD, seq).transpose(0, 3, 1, 2)


def kernel(x_prompt, x_sample, cache_diff_k, cache_diff_v, cache_swa_k, cache_swa_v, c, c_ctx,
           w_mod, b_mod, norm_g, w_qkv_diff, diff_lambda, diff_subln_g, w_o_diff,
           w_qkv_swa, swa_sink, w_o_swa, w_gate, w_up, w_down):
    bp, lp, d = x_prompt.shape
    bs, ls, _ = x_sample.shape
    lc = cache_diff_k.shape[2]
    depth = w_mod.shape[0]
    tm = ROW_TILE

    cond8 = jnp.concatenate([c_ctx[None, :], c, jnp.zeros((8 - 1 - bs, d), F32)], axis=0)
    mods = _modulation(cond8, w_mod, b_mod)
    tables = _rope_tables(ls)

    cdk = cache_diff_k.reshape(bs, -1, 2 * DIFF_HD)
    cdv = cache_diff_v.reshape(bs, -1, 2 * DIFF_HD)
    cskt = _swa_cache_to_feature_major(cache_swa_k)
    csvt = _swa_cache_to_feature_major(cache_swa_v)

    n_p, n_s = bp * lp, bs * ls
    x_parts = (x_prompt.reshape(n_p, d), x_sample.reshape(n_s, d))
    g = norm_g
    sub_g = diff_subln_g.reshape(-1, 1, 2 * DIFF_HD)
    diff_k_out, diff_v_out, swa_k_out, swa_v_out = [], [], [], []

    for i in range(depth):
        j = i // N_MIXERS
        if i % N_MIXERS == 0:
            lam_init = 0.8 - 0.6 * math.exp(-0.3 * i)
            qp, kp, vp, qs, ks, vs = _qkv_diff(x_parts, g, mods, w_qkv_diff, tables, layer=i, j=j,
                                               n_p=n_p, n_s=n_s, tiles_per_request=ls // QKV_ROW_TILE)
            op = _diff_prompt_attention(qp, kp, vp, diff_lambda, sub_g, j=j, seq=lp, lam_init=lam_init)
            diff_k_out.append(kp.reshape(bp, lp, DIFF_HEADS, 2 * DIFF_HD))
            diff_v_out.append(vp.reshape(bp, lp, DIFF_HEADS, 2 * DIFF_HD))
            os_ = _diff_latent_attention(qs, ks, vs, cdk, cdv, diff_lambda, sub_g,
                                         j=j, seq=ls, lc=lc, lam_init=lam_init)
            w_o = w_o_diff
        else:
            qp, ktp, vtp, qs, kds, vds = _qkv_swa(x_parts, g, mods, w_qkv_swa, tables, layer=i, j=j,
                                                  n_p=n_p, n_s=n_s, tiles_per_request=ls // QKV_ROW_TILE,
                                                  seq=lp)
            op = _swa_prompt_attention(qp, ktp, vtp, swa_sink[j], seq=lp)
            swa_k_out.append(_swa_cache_from_feature_major(ktp, lp))
            swa_v_out.append(_swa_cache_from_feature_major(vtp, lp))
            os_ = _swa_latent_attention(qs, kds, vds, cskt, csvt, swa_sink[j], j=j, seq=ls)
            w_o = w_o_swa
        last = i == depth - 1
        out = _post_attn_ffn(op, os_, x_parts, w_o, w_gate, w_up, w_down, g, mods, layer=i, j=j,
                             tiles_per_request=ls // tm, split_out=last)
        x_parts = out if last else (out,)
    xp, xs = x_parts

    return (xp.reshape(bp, lp, d), xs.reshape(bs, ls, d),
            jnp.stack(diff_k_out, axis=1), jnp.stack(diff_v_out, axis=1),
            jnp.stack(swa_k_out, axis=1), jnp.stack(swa_v_out, axis=1))
```

```python
import functools
import math

import jax
import jax.numpy as jnp
import numpy as np
from jax import lax
from jax.experimental import pallas as pl
from jax.experimental.pallas import tpu as pltpu

F32 = jnp.float32
BF16 = jnp.bfloat16

GRID_W = 64
N_MIXERS = 2
DIFF_HEADS = 8
DIFF_HD = 64
SWA_HEADS = 16
SWA_KV_HEADS = 4
SWA_GROUP = SWA_HEADS // SWA_KV_HEADS
SWA_HD = 64
ROT_DIM = 64
WINDOW = 128
ROPE_BASE = 10000.0
EPS = 1e-6
NEG_INF = -1e30

LANES = 128
ROW_TILE = 512
QKV_ROW_TILE = 512
PROMPT_SCORE_BUFFERS = 8
LATENT_SCORE_BUFFERS = 2
VMEM_LIMIT = 48 * 1024 * 1024
FFN_VMEM_LIMIT = 58 * 1024 * 1024
WEIGHT_STAGE_ROWS = 128
WEIGHT_STAGE_SLOTS = 3
WEIGHT_COLUMN_SLOTS = 2
QKV_STAGE_SLOTS = 4
FFN_CHUNK = 256
NT_DIMS = (((1,), (1,)), ((), ()))
LOG2E = math.log2(math.e)


def _params(n_axes):
    return pltpu.CompilerParams(dimension_semantics=("arbitrary",) * n_axes,
                                vmem_limit_bytes=VMEM_LIMIT)


def _layer_resident(shape, layer):
    return pl.BlockSpec((None,) + tuple(shape[1:]), lambda *_: (layer,) + (0,) * (len(shape) - 1),
                        pipeline_mode=pl.Buffered(1))


def _rms(x, g):
    ms = jnp.mean(x * x, axis=-1, keepdims=True)
    return (x * lax.rsqrt(ms + EPS)) * g


def _half_masks(dtype):
    lane = lax.broadcasted_iota(jnp.int32, (1, LANES), 1)
    lo = lane < (LANES // 2)
    return jnp.where(lo, 1.0, 0.0).astype(dtype), jnp.where(lo, 0.0, 1.0).astype(dtype)


def _lo64():
    return lax.broadcasted_iota(jnp.int32, (1, LANES), 1) < (LANES // 2)


def _ones_column(rows):
    return jnp.ones((rows, LANES), BF16)


def _ones_row(cols):
    return jnp.ones((LANES, cols), BF16)


def _mod_kernel(cond_ref, w_ref, b_ref, out_ref):
    c = cond_ref[...]
    s = c * jax.nn.sigmoid(c)
    out_ref[...] = jnp.dot(s.astype(BF16), w_ref[...].astype(BF16),
                           preferred_element_type=F32) + b_ref[pl.ds(pl.program_id(0), 1), :]


def _modulation(cond8, w_mod, b_mod):
    depth, d, n = w_mod.shape
    tn = 1536
    return pl.pallas_call(
        _mod_kernel,
        out_shape=jax.ShapeDtypeStruct((depth, 8, n), F32),
        grid=(depth, n // tn),
        in_specs=[pl.BlockSpec((8, d), lambda i, j: (0, 0)),
                  pl.BlockSpec((None, d, tn), lambda i, j: (i, 0, j)),
                  pl.BlockSpec((depth, tn), lambda i, j: (0, j))],
        out_specs=pl.BlockSpec((None, 8, tn), lambda i, j: (i, 0, j)),
        compiler_params=_params(2),
        name="modulation",
    )(cond8, w_mod, b_mod)


def _rope_slab(xs, cos, sin_signed, lo16):
    left = pltpu.roll(xs, LANES - 16, 1)
    right = pltpu.roll(xs, 16, 1)
    return xs * cos + jnp.where(lo16, left, right) * sin_signed


def _lo16_mask():
    lane = lax.broadcasted_iota(jnp.int32, (1, LANES), 1)
    return (lane % 32) < 16


def _qkv_both_groups(refs, n_prompt_tiles, tiles_per_request, split_x, j, prompt_tile, latent_tile):
    refs = list(refs)
    x_refs = refs[:2] if split_x else refs[:1]
    g_ref, mod_ref, w_hbm, cos_ref, sin_ref = refs[len(x_refs):len(x_refs) + 5]
    outs = refs[len(x_refs) + 5:len(x_refs) + 11]
    w_vmem, stage, sem = refs[len(x_refs) + 11:]
    d = x_refs[0].shape[1]
    i = pl.program_id(0)

    @pl.when(i == 0)
    def _():
        _load_weights_as_bf16([(w_hbm.at[j], w_vmem)], {w_vmem.shape[1]: stage}, {w_vmem.shape[1]: sem})

    def pre_norm(x, r):
        shift, scale = mod_ref[pl.ds(r, 1), 0:d], mod_ref[pl.ds(r, 1), d:2 * d]
        return (_rms(x, g_ref[0:1, :] * (1 + scale)) + shift).astype(BF16)

    @pl.when(i < n_prompt_tiles)
    def _():
        prompt_tile(pre_norm(x_refs[0][...], 0), w_vmem, outs[:3])

    @pl.when(i >= n_prompt_tiles)
    def _():
        r = 1 + (i - n_prompt_tiles) // tiles_per_request
        latent_tile(pre_norm(x_refs[-1][...], r), w_vmem, outs[3:], (cos_ref[...], sin_ref[...], _lo16_mask()))


def _qkv_call(kernel, name, x_parts, g, mods, w, tables, prompt_outs, latent_outs, *, layer, n_p, n_s):
    d = x_parts[0].shape[1]
    tm = QKV_ROW_TILE
    tp, ts = n_p // tm, n_s // tm
    nt = tables[0].shape[0] // tm

    def prompt_block(shape):
        return pl.BlockSpec(shape, lambda i: (jnp.minimum(i, tp - 1),) + (0,) * (len(shape) - 1))

    def latent_block(shape):
        return pl.BlockSpec(shape, lambda i: (jnp.maximum(i - tp, 0),) + (0,) * (len(shape) - 1))

    if len(x_parts) == 2:
        x_specs = [prompt_block((tm, d)), latent_block((tm, d))]
    else:
        x_specs = [pl.BlockSpec((tm, d), lambda i: (i, 0))]
    table_spec = pl.BlockSpec((tm, LANES), lambda i: (jnp.maximum(i - tp, 0) % nt, 0))
    in_specs = x_specs + [_layer_resident(g.shape, layer), _layer_resident(mods.shape, layer),
                          pl.BlockSpec(memory_space=pl.ANY), table_spec, table_spec]
    out_shape, out_specs = [], []
    for outs, n, block in ((prompt_outs, n_p, prompt_block), (latent_outs, n_s, latent_block)):
        for shape_of, dtype in outs:
            out_shape.append(jax.ShapeDtypeStruct(shape_of(n), dtype))
            out_specs.append(block(shape_of(tm)))
    cols = w.shape[2]
    scratch = [pltpu.VMEM((d, cols), BF16), pltpu.VMEM((QKV_STAGE_SLOTS, WEIGHT_STAGE_ROWS, cols), F32),
               pltpu.SemaphoreType.DMA((QKV_STAGE_SLOTS,))]
    return pl.pallas_call(
        kernel,
        out_shape=tuple(out_shape),
        grid=(tp + ts,),
        in_specs=in_specs,
        out_specs=tuple(out_specs),
        scratch_shapes=scratch,
        compiler_params=_params(1),
        name=name,
    )(*x_parts, g, mods, w, *tables)


def _diff_tile(h, w_ref, outs, rope=None):
    q_ref, k_ref, v_ref = outs
    d = h.shape[1]
    cw = 512
    for c in range(3 * d // cw):
        acc = jnp.dot(h, w_ref[:, c * cw:(c + 1) * cw], preferred_element_type=F32)
        which, off = divmod(c * cw, d)
        dst = outs[which]
        for s in range(cw // LANES):
            xs = acc[:, s * LANES:(s + 1) * LANES]
            if rope is not None and which < 2:
                xs = _rope_slab(xs, *rope)
            if which == 0:
                xs = xs * (DIFF_HD ** -0.5 * LOG2E)
            lo = off + s * LANES
            if rope is not None or which == 0:
                dst[:, lo:lo + LANES] = xs.astype(dst.dtype)
            else:
                dst[pl.ds(lo // LANES, h.shape[0], stride=DIFF_HEADS), :] = xs


def _qkv_diff_kernel(*refs, j, n_prompt_tiles, tiles_per_request, split_x):
    _qkv_both_groups(refs, n_prompt_tiles, tiles_per_request, split_x, j, _diff_tile, _diff_tile)


def _qkv_diff(x_parts, g, mods, w, tables, *, layer, j, n_p, n_s, tiles_per_request):
    d = x_parts[0].shape[1]

    def rows(n):
        return (n, d)

    def cache(n):
        return (n * DIFF_HEADS, d // DIFF_HEADS)

    kernel = functools.partial(_qkv_diff_kernel, j=j, n_prompt_tiles=n_p // QKV_ROW_TILE,
                               tiles_per_request=tiles_per_request, split_x=len(x_parts) == 2)
    return _qkv_call(kernel, "qkv_diff", x_parts, g, mods, w, tables,
                     [(rows, BF16), (cache, F32), (cache, F32)], [(rows, BF16)] * 3,
                     layer=layer, n_p=n_p, n_s=n_s)


def _swa_q_tile(h, w_ref, q_ref, rope):
    d = h.shape[1]
    cw = 512
    for c in range(d // cw):
        acc = jnp.dot(h, w_ref[:, c * cw:(c + 1) * cw], preferred_element_type=F32)
        for s in range(cw // LANES):
            xs = acc[:, s * LANES:(s + 1) * LANES]
            if rope is not None:
                xs = _rope_slab(xs, *rope)
            lo = c * cw + s * LANES
            q_ref[:, lo:lo + LANES] = (xs * (SWA_HD ** -0.5 * LOG2E)).astype(BF16)
    nkv = SWA_KV_HEADS * SWA_HD
    return jnp.dot(h, w_ref[:, d:d + 2 * nkv], preferred_element_type=F32), nkv


def _swa_prompt_tile(h, w_ref, outs, *, seq):
    q_ref, kt_ref, vt_ref = outs
    kv, nkv = _swa_q_tile(h, w_ref, q_ref, None)
    for b in range(h.shape[0] // seq):
        kt_ref[b] = kv[b * seq:(b + 1) * seq, :nkv].T
        vt_ref[b] = kv[b * seq:(b + 1) * seq, nkv:].T


def _swa_latent_tile(h, w_ref, outs, rope):
    q_ref, kd_ref, vd_ref = outs
    kv, nkv = _swa_q_tile(h, w_ref, q_ref, rope)
    lo64 = _lo64()
    for which, dst in enumerate((kd_ref, vd_ref)):
        for s in range(nkv // LANES):
            xs = kv[:, which * nkv + s * LANES: which * nkv + (s + 1) * LANES]
            if which == 0:
                xs = _rope_slab(xs, *rope)
            sw = pltpu.roll(xs, LANES // 2, 1)
            dst[:, (2 * s) * LANES:(2 * s + 1) * LANES] = jnp.where(lo64, xs, sw).astype(BF16)
            dst[:, (2 * s + 1) * LANES:(2 * s + 2) * LANES] = jnp.where(lo64, sw, xs).astype(BF16)


def _qkv_swa_kernel(*refs, j, n_prompt_tiles, tiles_per_request, split_x, seq):
    _qkv_both_groups(refs, n_prompt_tiles, tiles_per_request, split_x, j,
                     functools.partial(_swa_prompt_tile, seq=seq), _swa_latent_tile)


def _qkv_swa(x_parts, g, mods, w, tables, *, layer, j, n_p, n_s, tiles_per_request, seq):
    d = x_parts[0].shape[1]
    nkv = SWA_KV_HEADS * SWA_HD

    def rows(n):
        return (n, d)

    def feature_major(n):
        return (n // seq, nkv, seq)

    def duplicated(n):
        return (n, 2 * nkv)

    kernel = functools.partial(_qkv_swa_kernel, j=j, n_prompt_tiles=n_p // QKV_ROW_TILE,
                               tiles_per_request=tiles_per_request, split_x=len(x_parts) == 2, seq=seq)
    return _qkv_call(kernel, "qkv_swa", x_parts, g, mods, w, tables,
                     [(rows, BF16), (feature_major, F32), (feature_major, F32)],
                     [(rows, BF16), (duplicated, BF16), (duplicated, BF16)],
                     layer=layer, n_p=n_p, n_s=n_s)


def _diff_lambda(lam_ref, lam_init):
    lp = lam_ref[...]
    a = jnp.sum(lp[0:1] * lp[1:2], axis=-1, keepdims=True)
    b = jnp.sum(lp[2:3] * lp[3:4], axis=-1, keepdims=True)
    return jnp.exp(a) - jnp.exp(b) + lam_init


def _diff_combine(acc, tq, lam, g, lam_init):
    o12 = acc[:, :LANES] / acc[:, LANES:]
    o = o12[:tq] - lam * o12[tq:]
    return _rms(o, g) * (1.0 - lam_init)


def _stack_maps(q):
    m_lo, m_hi = _half_masks(BF16)
    return jnp.concatenate([q * m_lo, q * m_hi], axis=0)


def _run_pipelined(items, scores, finish, s_bufs):
    depth = len(s_bufs)
    states = {i: scores(items[i], s_bufs[i]) for i in range(min(depth - 1, len(items)))}
    for i, item in enumerate(items):
        ahead = i + depth - 1
        if ahead < len(items):
            states[ahead] = scores(items[ahead], s_bufs[ahead % depth])
        finish(item, s_bufs[i % depth], states.pop(i))


def _store_scores(s_ref, col0, s, mrun):
    s_ref[:, col0:col0 + s.shape[1]] = s
    for t in range(s.shape[1] // LANES):
        blk = s[:, t * LANES:(t + 1) * LANES]
        mrun = blk if mrun is None else jnp.maximum(mrun, blk)
    return mrun


def _exp_block(s_ref, col0, width, mb):
    return jnp.concatenate(
        [jnp.exp2(s_ref[:, col0 + t * LANES:col0 + (t + 1) * LANES] - mb).astype(BF16)
         for t in range(width // LANES)], axis=1)


def _diff_prompt_kernel(q_ref, k_ref, v_ref, lam_ref, g_ref, o_ref, *s_bufs, lam_init, seq):
    lam = _diff_lambda(lam_ref, lam_init)
    g = g_ref[...]
    ones = _ones_column(seq)
    items = [(r, h) for r in range(q_ref.shape[0] // seq) for h in range(DIFF_HEADS)]

    def head_rows(ref, r, h):
        return ref[pl.ds(r * seq * DIFF_HEADS + h, seq, stride=DIFF_HEADS), :]

    def scores(item, s_ref):
        r, h = item
        rows, sl = slice(r * seq, (r + 1) * seq), slice(h * LANES, (h + 1) * LANES)
        s = lax.dot_general(_stack_maps(q_ref[rows, sl]), head_rows(k_ref, r, h).astype(BF16), NT_DIMS,
                            preferred_element_type=F32)
        return _store_scores(s_ref, 0, s, None)

    def finish(item, s_ref, mrun):
        r, h = item
        rows, sl = slice(r * seq, (r + 1) * seq), slice(h * LANES, (h + 1) * LANES)
        mb = jnp.broadcast_to(jnp.max(mrun, axis=-1, keepdims=True), (2 * seq, LANES))
        vx = jnp.concatenate([head_rows(v_ref, r, h).astype(BF16), ones], axis=1)
        acc = jnp.dot(_exp_block(s_ref, 0, seq, mb), vx, preferred_element_type=F32)
        o_ref[rows, sl] = _diff_combine(acc, seq, lam, g, lam_init).astype(BF16)

    _run_pipelined(items, scores, finish, s_bufs)


def _diff_prompt_attention(q, k, v, lam_params, subln_g, *, j, seq, lam_init):
    n, d = q.shape
    req = 2
    spec = pl.BlockSpec((req * seq, d), lambda b: (b, 0))
    kv_spec = pl.BlockSpec((req * seq * DIFF_HEADS, d // DIFF_HEADS), lambda b: (b, 0))
    return pl.pallas_call(
        functools.partial(_diff_prompt_kernel, lam_init=lam_init, seq=seq),
        out_shape=jax.ShapeDtypeStruct((n, d), BF16),
        grid=(n // (req * seq),),
        in_specs=[spec, kv_spec, kv_spec, _layer_resident(lam_params.shape, j),
                  _layer_resident(subln_g.shape, j)],
        out_specs=spec,
        scratch_shapes=[pltpu.VMEM((2 * seq, seq), F32)] * PROMPT_SCORE_BUFFERS,
        compiler_params=_params(1),
        name="diff_attn_prompt",
    )(q, k, v, lam_params, subln_g)


def _diff_latent_kernel(q_ref, kc_ref, vc_ref, kl_ref, vl_ref, qn_ref, kcn_ref, kln_ref, lam_ref, g_ref, o_ref,
                        kk_ref, vx_ref, m_ref, s0_ref, s1_ref, *, lam_init, lc, tq, key_chunk):
    seq = q_ref.shape[0]
    nk = kk_ref.shape[0]
    head = pl.program_id(1)
    step = pl.program_id(0) * pl.num_programs(1) + head
    last_step = pl.num_programs(0) * pl.num_programs(1) - 1
    next_head = jnp.minimum(step + 1, last_step) % DIFF_HEADS
    masks = _half_masks(BF16)
    nchunk = nk // key_chunk
    s_bufs = (s0_ref, s1_ref)

    def fill_keys(cache_ref, h, latent_ref):
        kk_ref[0:lc, :] = cache_ref[pl.ds(h, lc, stride=DIFF_HEADS), :].astype(BF16)
        kk_ref[lc:, :] = latent_ref[...]

    def scores(q_rows, m, s_ref):
        q = q_rows * masks[m]
        mrun = None
        for c in range(nchunk):
            s = lax.dot_general(q, kk_ref[c * key_chunk:(c + 1) * key_chunk, :], NT_DIMS,
                                preferred_element_type=F32)
            mrun = _store_scores(s_ref, c * key_chunk, s, mrun)
        return mrun

    @pl.when(step == 0)
    def _():
        fill_keys(kc_ref, head, kl_ref)
        m_ref[...] = scores(q_ref[0:tq, :], 0, s0_ref)

    vx_ref[0:lc, 0:LANES] = vc_ref[pl.ds(head, lc, stride=DIFF_HEADS), :].astype(BF16)
    vx_ref[lc:, 0:LANES] = vl_ref[...]
    vx_ref[:, LANES:2 * LANES] = _ones_column(nk)

    items = [(rt, m) for rt in range(seq // tq) for m in range(2)]
    lam = _diff_lambda(lam_ref, lam_init)
    g = g_ref[...]
    first_map = {}

    def finish(item, s_ref, mrun):
        rt, m = item
        mb = jnp.broadcast_to(jnp.max(mrun, axis=-1, keepdims=True), (tq, LANES))
        acc = None
        for c in range(nchunk):
            part = jnp.dot(_exp_block(s_ref, c * key_chunk, key_chunk, mb),
                           vx_ref[c * key_chunk:(c + 1) * key_chunk, :], preferred_element_type=F32)
            acc = part if acc is None else acc + part
        o_m = acc[:, :LANES] / acc[:, LANES:]
        if m == 0:
            first_map[rt] = o_m
        else:
            o = _rms(first_map.pop(rt) - lam * o_m, g) * (1.0 - lam_init)
            o_ref[rt * tq:(rt + 1) * tq, :] = o.astype(BF16)

    mrun = m_ref[...]
    for i, item in enumerate(items):
        s_ahead = s_bufs[(i + 1) % 2]
        if i + 1 < len(items):
            rt, m = items[i + 1]
            ahead = scores(q_ref[rt * tq:(rt + 1) * tq, :], m, s_ahead)
        else:
            fill_keys(kcn_ref, next_head, kln_ref)
            ahead = scores(qn_ref[...], 0, s_ahead)
            m_ref[...] = ahead
        finish(item, s_bufs[i % 2], mrun)
        mrun = ahead


def _diff_latent_attention(q, k, v, cache_k, cache_v, lam_params, subln_g, *, j, seq, lc, lam_init):
    n, d = q.shape
    nb = cache_k.shape[0]
    tq = 512
    key_chunk = 512
    last_step = nb * DIFF_HEADS - 1

    def next_step(b, h):
        return divmod(jnp.minimum(b * DIFF_HEADS + h + 1, last_step), DIFF_HEADS)

    def next_query_tile(b, h):
        nb_, nh = next_step(b, h)
        return nb_ * (seq // tq), nh

    q_spec = pl.BlockSpec((seq, LANES), lambda b, h: (b, h))
    c_spec = pl.BlockSpec((None, lc * DIFF_HEADS, LANES), lambda b, h: (b, j, 0))
    qn_spec = pl.BlockSpec((tq, LANES), next_query_tile)
    kln_spec = pl.BlockSpec((seq, LANES), next_step)
    cn_spec = pl.BlockSpec((None, lc * DIFF_HEADS, LANES), lambda b, h: (next_step(b, h)[0], j, 0))
    return pl.pallas_call(
        functools.partial(_diff_latent_kernel, lam_init=lam_init, lc=lc, tq=tq, key_chunk=key_chunk),
        out_shape=jax.ShapeDtypeStruct((n, d), BF16),
        grid=(nb, DIFF_HEADS),
        in_specs=[q_spec, c_spec, c_spec, q_spec, q_spec, qn_spec, cn_spec, kln_spec,
                  _layer_resident(lam_params.shape, j), _layer_resident(subln_g.shape, j)],
        out_specs=q_spec,
        scratch_shapes=[pltpu.VMEM((lc + seq, LANES), BF16),
                        pltpu.VMEM((lc + seq, 2 * LANES), BF16),
                        pltpu.VMEM((tq, LANES), F32),
                        *[pltpu.VMEM((tq, lc + seq), F32)] * LATENT_SCORE_BUFFERS],
        compiler_params=_params(2),
        name="diff_attn_latent",
    )(q, cache_k, cache_v, k, v, q, cache_k, k, lam_params, subln_g)


def _stack_group(q_ref, rows, kv_local):
    m_lo, m_hi = _half_masks(BF16)
    parts = []
    for gb in range(SWA_GROUP // 2):
        blk = kv_local * (SWA_GROUP // 2) + gb
        qb = q_ref[rows, blk * LANES:(blk + 1) * LANES]
        parts += [qb * m_lo, qb * m_hi]
    return jnp.concatenate(parts, axis=0)


def _sink_column(sink_ref, first_head, tq):
    return jnp.concatenate([jnp.full((tq, LANES), sink_ref[first_head + g] * LOG2E, F32)
                            for g in range(SWA_GROUP)], axis=0)


def _write_group(o_ref, rows, kv_local, o, tq):
    lo64 = _lo64()
    for gb in range(SWA_GROUP // 2):
        blk = kv_local * (SWA_GROUP // 2) + gb
        even = o[(2 * gb) * tq:(2 * gb + 1) * tq]
        odd = o[(2 * gb + 1) * tq:(2 * gb + 2) * tq]
        o_ref[rows, blk * LANES:(blk + 1) * LANES] = jnp.where(lo64, even, odd).astype(BF16)


def _dup_rows(x_t):
    xb = x_t.astype(BF16)
    return jnp.concatenate([xb, xb], axis=0)


def _sink_finish(mrun, sk, rows):
    mb = jnp.maximum(jnp.broadcast_to(jnp.max(mrun, axis=-1, keepdims=True), (rows, LANES)), sk)
    return mb, jnp.exp2(sk - mb)


def _swa_prompt_kernel(sink_ref, q_ref, kt_ref, vt_ref, o_ref, *s_bufs):
    seq = kt_ref.shape[2]
    rows = SWA_GROUP * seq
    ones = _ones_row(seq)
    items = [(r, j) for r in range(kt_ref.shape[0]) for j in range(SWA_KV_HEADS)]

    def scores(item, s_ref):
        r, j = item
        kd = _dup_rows(kt_ref[r, j * SWA_HD:(j + 1) * SWA_HD, :])
        s = jnp.dot(_stack_group(q_ref, slice(r * seq, (r + 1) * seq), j), kd, preferred_element_type=F32)
        return _store_scores(s_ref, 0, s, None)

    def finish(item, s_ref, mrun):
        r, j = item
        sk = _sink_column(sink_ref, j * SWA_GROUP, seq)
        mb, sink_term = _sink_finish(mrun, sk, rows)
        vx = jnp.concatenate([_dup_rows(vt_ref[r, j * SWA_HD:(j + 1) * SWA_HD, :]), ones], axis=0)
        acc = lax.dot_general(_exp_block(s_ref, 0, seq, mb), vx, NT_DIMS, preferred_element_type=F32)
        o = acc[:, :LANES] / (acc[:, LANES:] + sink_term)
        _write_group(o_ref, slice(r * seq, (r + 1) * seq), j, o, seq)

    _run_pipelined(items, scores, finish, s_bufs)


def _swa_prompt_attention(q, kt, vt, sink, *, seq):
    n, d = q.shape
    nkv = kt.shape[1]
    req = 2
    t_spec = pl.BlockSpec((req, nkv, seq), lambda b: (b, 0, 0))
    return pl.pallas_call(
        _swa_prompt_kernel,
        out_shape=jax.ShapeDtypeStruct((n, d), BF16),
        grid=(n // (req * seq),),
        in_specs=[pl.BlockSpec(memory_space=pltpu.SMEM),
                  pl.BlockSpec((req * seq, d), lambda b: (b, 0)), t_spec, t_spec],
        out_specs=pl.BlockSpec((req * seq, d), lambda b: (b, 0)),
        scratch_shapes=[pltpu.VMEM((SWA_GROUP * seq, seq), F32)] * (PROMPT_SCORE_BUFFERS // 2),
        compiler_params=_params(1),
        name="swa_attn_prompt",
    )(sink, q, kt, vt)


def _swa_latent_kernel(sink_ref, q_ref, kc_ref, vc_ref, kl_ref, vl_ref, qn_ref, kcn_ref, kln_ref, o_ref,
                       kcd_ref, vcx_ref, m_ref, s0_ref, s1_ref, *, tq, span):
    pair, part = pl.program_id(1), pl.program_id(2)
    n_parts = pl.num_programs(2)
    tiles = q_ref.shape[0] // tq
    seq = kl_ref.shape[0]
    lc = kc_ref.shape[1]
    rows = SWA_GROUP * tq
    step = (pl.program_id(0) * pl.num_programs(1) + pair) * n_parts + part
    last_step = pl.num_programs(0) * pl.num_programs(1) * n_parts - 1
    next_part = jnp.minimum(step + 1, last_step) % n_parts
    ones_row = _ones_row(lc)
    for jj in range(2):
        kcd_ref[jj] = _dup_rows(kc_ref[jj * SWA_HD:(jj + 1) * SWA_HD, :])
        vcx_ref[jj] = jnp.concatenate([_dup_rows(vc_ref[jj * SWA_HD:(jj + 1) * SWA_HD, :]), ones_row], axis=0)
    ones_col = _ones_column(span)
    items = [(t, jj) for t in range(tiles) for jj in range(2)]
    s_bufs = (s0_ref, s1_ref)
    windows, biases = {}, {}

    def window_of(part, t):
        q0 = (part * tiles + t) * tq
        return q0, pl.multiple_of(jnp.clip(q0 - WINDOW, 0, seq - span), WINDOW)

    def band_bias(q0, ws):
        q_pos = q0 + lax.broadcasted_iota(jnp.int32, (tq, span), 0)
        k_pos = ws + lax.broadcasted_iota(jnp.int32, (tq, span), 1)
        b = jnp.where(jnp.abs(q_pos - k_pos) <= WINDOW, 0.0, NEG_INF).astype(F32)
        return jnp.concatenate([b] * SWA_GROUP, axis=0)

    def window(t):
        if t not in windows:
            windows[t] = window_of(part, t)
        return windows[t]

    def bias_for(t):
        if t not in biases:
            biases[t] = band_bias(*window(t))
        return biases[t]

    def score_tile(qs, cache_keys, window_keys, bias, s_ref):
        s_c = jnp.dot(qs, cache_keys, preferred_element_type=F32)
        mrun = _store_scores(s_ref, 0, s_c, None)
        s_w = lax.dot_general(qs, window_keys, NT_DIMS, preferred_element_type=F32) + bias
        return _store_scores(s_ref, lc, s_w, mrun)

    def scores(item, s_ref):
        t, jj = item
        _, ws = window(t)
        return score_tile(_stack_group(q_ref, slice(t * tq, (t + 1) * tq), jj), kcd_ref[jj],
                          kl_ref[pl.ds(ws, span), jj * LANES:(jj + 1) * LANES], bias_for(t), s_ref)

    def finish(item, s_ref, mrun):
        t, jj = item
        _, ws = window(t)
        sk = _sink_column(sink_ref, (2 * pair + jj) * SWA_GROUP, tq)
        mb, sink_term = _sink_finish(mrun, sk, rows)
        vwx = jnp.concatenate([vl_ref[pl.ds(ws, span), jj * LANES:(jj + 1) * LANES], ones_col], axis=1)
        acc = (lax.dot_general(_exp_block(s_ref, 0, lc, mb), vcx_ref[jj], NT_DIMS, preferred_element_type=F32)
               + jnp.dot(_exp_block(s_ref, lc, span, mb), vwx, preferred_element_type=F32))
        o = acc[:, :LANES] / (acc[:, LANES:] + sink_term)
        _write_group(o_ref, slice(t * tq, (t + 1) * tq), jj, o, tq)

    @pl.when(step == 0)
    def _():
        q0, ws = window_of(part, 0)
        m_ref[...] = score_tile(_stack_group(q_ref, slice(0, tq), 0), kcd_ref[0],
                                kl_ref[pl.ds(ws, span), 0:LANES], band_bias(q0, ws), s0_ref)

    mrun = m_ref[...]
    for i, item in enumerate(items):
        s_ahead = s_bufs[(i + 1) % 2]
        if i + 1 < len(items):
            ahead = scores(items[i + 1], s_ahead)
        else:
            q0, ws = window_of(next_part, 0)
            ahead = score_tile(_stack_group(qn_ref, slice(0, tq), 0), _dup_rows(kcn_ref[...]),
                               kln_ref[pl.ds(ws, span), :], band_bias(q0, ws), s_ahead)
            m_ref[...] = ahead
        finish(item, s_bufs[i % 2], mrun)
        mrun = ahead


def _swa_latent_attention(q, kd, vd, cache_kt, cache_vt, sink, *, j, seq):
    n, d = q.shape
    nb, _, lc = cache_kt.shape
    tq = 256
    span = tq + 2 * WINDOW
    npair = SWA_KV_HEADS // 2
    wq = d // npair
    parts = 2
    tiles = seq // parts // tq
    last_step = nb * npair * parts - 1

    def next_step(b, p, i):
        rest, ni = divmod(jnp.minimum((b * npair + p) * parts + i + 1, last_step), parts)
        return (*divmod(rest, npair), ni)

    def next_query_tile(b, p, i):
        nb_, np_, ni = next_step(b, p, i)
        return (nb_ * parts + ni) * tiles, 2 * np_

    def next_cache_head(b, p, i):
        nb_, np_, _ = next_step(b, p, i)
        return nb_, 2 * (j * npair + np_), 0

    def next_latent_head(b, p, i):
        nb_, np_, _ = next_step(b, p, i)
        return nb_, 2 * np_

    q_spec = pl.BlockSpec((seq // parts, wq), lambda b, p, i: (b * parts + i, p))
    c_spec = pl.BlockSpec((None, 2 * SWA_HD, lc), lambda b, p, i: (b, j * npair + p, 0))
    l_spec = pl.BlockSpec((seq, 2 * LANES), lambda b, p, i: (b, p))
    qn_spec = pl.BlockSpec((tq, 2 * LANES), next_query_tile)
    cn_spec = pl.BlockSpec((None, SWA_HD, lc), next_cache_head)
    ln_spec = pl.BlockSpec((seq, LANES), next_latent_head)
    s_shape = pltpu.VMEM((SWA_GROUP * tq, lc + span), F32)
    return pl.pallas_call(
        functools.partial(_swa_latent_kernel, tq=tq, span=span),
        out_shape=jax.ShapeDtypeStruct((n, d), BF16),
        grid=(nb, npair, parts),
        in_specs=[pl.BlockSpec(memory_space=pltpu.SMEM), q_spec, c_spec, c_spec, l_spec, l_spec,
                  qn_spec, cn_spec, ln_spec],
        out_specs=q_spec,
        scratch_shapes=[pltpu.VMEM((2, 2 * SWA_HD, lc), BF16), pltpu.VMEM((2, 2 * LANES, lc), BF16),
                        pltpu.VMEM((SWA_GROUP * tq, LANES), F32), s_shape, s_shape],
        compiler_params=_params(3),
        name="swa_attn_latent",
    )(sink, q, cache_kt, cache_vt, kd, vd, q, cache_kt, kd)


def _load_weights_as_bf16(jobs, stages, sems):
    order = []
    rings = {w: [] for w in stages}
    for src, dst in jobs:
        w = src.shape[1]
        slots = stages[w].shape[0]
        for k in range(src.shape[0] // WEIGHT_STAGE_ROWS):
            rows = pl.ds(k * WEIGHT_STAGE_ROWS, WEIGHT_STAGE_ROWS)
            slot = len(rings[w]) % slots
            copy = pltpu.make_async_copy(src.at[rows, :], stages[w].at[slot], sems[w].at[slot])
            order.append((w, len(rings[w])))
            rings[w].append((copy, slot, dst, rows))
    for w, ring in rings.items():
        for copy, _, _, _ in ring[:stages[w].shape[0]]:
            copy.start()
    for w, k in order:
        copy, slot, dst, rows = rings[w][k]
        copy.wait()
        dst[rows, :] = stages[w][slot].astype(BF16)
        ahead = k + stages[w].shape[0]
        if ahead < len(rings[w]):
            rings[w][ahead][0].start()


def _weight_stream(pieces, rings):
    per_ring = {name: [] for name in rings}
    plan = []
    for src, dst, idx, name in pieces:
        stage, sem = rings[name]
        slot = len(per_ring[name]) % stage.shape[0]
        view = stage.at[slot].at[0:src.shape[0], 0:src.shape[1]]
        plan.append((name, len(per_ring[name])))
        per_ring[name].append((pltpu.make_async_copy(src, view, sem.at[slot]), view, dst, idx))
    cursor = [0]

    def prime():
        for name, ring in per_ring.items():
            for copy, _, _, _ in ring[:rings[name][0].shape[0]]:
                copy.start()

    def take(n):
        for name, k in plan[cursor[0]:cursor[0] + n]:
            copy, view, dst, idx = per_ring[name][k]
            copy.wait()
            dst[idx] = view[...].astype(BF16)
            ahead = k + rings[name][0].shape[0]
            if ahead < len(per_ring[name]):
                per_ring[name][ahead][0].start()
        cursor[0] += n

    return prime, take


def _post_attn_ffn_kernel(*refs, layer, j, n_prompt_tiles, tiles_per_request, split_x, split_out,
                          overlap_first_tile):
    refs = list(refs)
    o_first_ref, op_ref, os_ref = refs[:3]
    refs = refs[1:]
    x_refs = refs[2:4] if split_x else refs[2:3]
    wo_hbm, wg_hbm, wu_hbm, wd_hbm, g_ref, mod_ref = refs[2 + len(x_refs):8 + len(x_refs)]
    n_out = 2 if split_out else 1
    out_refs = refs[8 + len(x_refs):8 + len(x_refs) + n_out]
    wo_ref, wg_ref, wu_ref, wd_ref, stage_row, stage_col, sem_row, sem_col, y_ref = refs[8 + len(x_refs) + n_out:]
    d = wo_ref.shape[1]
    dff = wg_ref.shape[1]
    cw = FFN_CHUNK
    i = pl.program_id(0)
    is_prompt = i < n_prompt_tiles
    r = jnp.where(is_prompt, 0, 1 + (i - n_prompt_tiles) // tiles_per_request)

    def mod(slot):
        return mod_ref[pl.ds(r, 1), slot * d:(slot + 1) * d]

    def out_projection(o):
        y_ref[...] = jnp.dot(o, wo_ref[...], preferred_element_type=F32)

    def tile(before_chunk):
        x = jnp.where(is_prompt, x_refs[0][...], x_refs[1][...]) if split_x else x_refs[0][...]
        x = x + _rms(y_ref[...], mod(2) * g_ref[1:2, :])
        h = (_rms(x, g_ref[2:3, :] * (1 + mod(4))) + mod(3)).astype(BF16)
        next_is_prompt = jnp.minimum(i + 1, pl.num_programs(0) - 1) < n_prompt_tiles
        out_projection(jnp.where(next_is_prompt, op_ref[...], os_ref[...]))
        y = jnp.zeros((h.shape[0], d), F32)
        n_chunks = dff // cw
        pending = []
        for c in range(n_chunks):
            before_chunk(c)
            cols = slice(c * cw, (c + 1) * cw)
            a = jnp.dot(h, wg_ref[:, cols], preferred_element_type=F32)
            u = jnp.dot(h, wu_ref[:, cols], preferred_element_type=F32)
            pending.append(((a * jax.nn.sigmoid(a)) * u).astype(BF16))
            if len(pending) == 2 or c == n_chunks - 1:
                rows = slice((c + 1 - len(pending)) * cw, (c + 1) * cw)
                y = y + jnp.dot(jnp.concatenate(pending, axis=1), wd_ref[rows, :], preferred_element_type=F32)
                pending = []
        out = x + _rms(y, mod(5) * g_ref[3:4, :])
        if split_out:
            @pl.when(is_prompt)
            def _():
                out_refs[0][...] = out

            @pl.when(jnp.logical_not(is_prompt))
            def _():
                out_refs[1][...] = out
        else:
            out_refs[0][...] = out

    @pl.when(i == 0)
    def _():
        row_chunk = stage_row.shape[1]
        pieces = [(wo_hbm.at[j].at[pl.ds(k * row_chunk, row_chunk), :], wo_ref,
                   (pl.ds(k * row_chunk, row_chunk), slice(None)), "row") for k in range(d // row_chunk)]
        col_chunk = stage_col.shape[2]
        per_take = [d // row_chunk]
        for c in range(dff // cw):
            n = 0
            if (c * cw) % col_chunk == 0:
                cols = pl.ds(c * cw, min(col_chunk, dff - c * cw))
                pieces += [(wg_hbm.at[layer].at[:, cols], wg_ref, (slice(None), cols), "col"),
                           (wu_hbm.at[layer].at[:, cols], wu_ref, (slice(None), cols), "col")]
                n += 2
            rows = pl.ds(c * cw, cw)
            pieces.append((wd_hbm.at[layer].at[rows, :], wd_ref, (rows, slice(None)), "row"))
            per_take.append(n + 1)
        prime, take = _weight_stream(pieces, {"row": (stage_row, sem_row), "col": (stage_col, sem_col)})
        prime()
        if overlap_first_tile:
            take(per_take[0])
            out_projection(o_first_ref[...])
            tile(lambda c: take(per_take[c + 1]))
        else:
            take(len(pieces))
            out_projection(o_first_ref[...])

    if overlap_first_tile:
        @pl.when(i > 0)
        def _():
            tile(lambda c: None)
    else:
        tile(lambda c: None)


def _post_attn_ffn(o_p, o_s, xs_in, w_o, wg, wu, wd, g, mods, *, layer, j, tiles_per_request, split_out):
    n_p, d = o_p.shape
    n_s = o_s.shape[0]
    tm = ROW_TILE
    tp, ts = n_p // tm, n_s // tm
    prompt_rows = pl.BlockSpec((tm, d), lambda i: (jnp.minimum(i, tp - 1), 0))
    latent_rows = pl.BlockSpec((tm, d), lambda i: (jnp.maximum(i - tp, 0), 0))
    all_rows = pl.BlockSpec((tm, d), lambda i: (i, 0))
    first_rows = pl.BlockSpec((tm, d), lambda i: (0, 0), pipeline_mode=pl.Buffered(1))
    next_prompt_rows = pl.BlockSpec((tm, d), lambda i: (jnp.minimum(i + 1, tp - 1), 0))
    next_latent_rows = pl.BlockSpec((tm, d), lambda i: (jnp.clip(i + 1 - tp, 0, ts - 1), 0))
    split_x = len(xs_in) == 2
    in_specs = [first_rows, next_prompt_rows, next_latent_rows]
    in_specs += [prompt_rows, latent_rows] if split_x else [all_rows]
    hbm = pl.BlockSpec(memory_space=pl.ANY)
    in_specs += [hbm, hbm, hbm, hbm, _layer_resident(g.shape, layer), _layer_resident(mods.shape, layer)]
    if split_out:
        out_shape = (jax.ShapeDtypeStruct((n_p, d), F32), jax.ShapeDtypeStruct((n_s, d), F32))
        out_specs = (prompt_rows, latent_rows)
    else:
        out_shape = jax.ShapeDtypeStruct((n_p + n_s, d), F32)
        out_specs = all_rows
    dff = wg.shape[2]
    scratch = [pltpu.VMEM((d, d), BF16), pltpu.VMEM((d, dff), BF16), pltpu.VMEM((d, dff), BF16),
               pltpu.VMEM((dff, d), BF16),
               pltpu.VMEM((WEIGHT_STAGE_SLOTS, FFN_CHUNK, d), F32),
               pltpu.VMEM((WEIGHT_COLUMN_SLOTS, d, 2 * FFN_CHUNK), F32),
               pltpu.SemaphoreType.DMA((WEIGHT_STAGE_SLOTS,)), pltpu.SemaphoreType.DMA((WEIGHT_COLUMN_SLOTS,)),
               pltpu.VMEM((tm, d), F32)]
    return pl.pallas_call(
        functools.partial(_post_attn_ffn_kernel, layer=layer, j=j, n_prompt_tiles=tp,
                          tiles_per_request=tiles_per_request, split_x=split_x, split_out=split_out,
                          overlap_first_tile=not split_x),
        out_shape=out_shape,
        grid=(tp + ts,),
        in_specs=in_specs,
        out_specs=out_specs,
        scratch_shapes=scratch,
        compiler_params=pltpu.CompilerParams(dimension_semantics=("arbitrary",),
                                             vmem_limit_bytes=FFN_VMEM_LIMIT),
        name="post_attn_ffn",
    )(o_p, o_p, o_s, *xs_in, w_o, wg, wu, wd, g, mods)


def _rope_tables(n_lat):
    t = np.arange(n_lat)
    row = (t // GRID_W).astype(np.float32)
    col = (t % GRID_W).astype(np.float32)
    nf = ROT_DIM // 4
    inv = np.float32(ROPE_BASE) ** (-np.arange(nf, dtype=np.float32) / np.float32(nf))
    ar = row[:, None] * inv[None, :]
    ac = col[:, None] * inv[None, :]
    ang = np.concatenate([ar, ar, ac, ac], axis=-1)
    cos, sin = np.cos(ang), np.sin(ang)
    sign = np.where((np.arange(ROT_DIM) % 32) < 16, -1.0, 1.0).astype(np.float32)
    reps = LANES // ROT_DIM
    return jnp.asarray(np.tile(cos, (1, reps))), jnp.asarray(np.tile(sin * sign, (1, reps)))


def _swa_cache_to_feature_major(cache):
    nb, nl, lc, nh, hd = cache.shape
    return cache.transpose(0, 1, 3, 4, 2).reshape(nb, nl * nh * hd, lc)


def _swa_cache_from_feature_major(xt, seq):
    nb = xt.shape[0]
    return xt.reshape(nb, SWA_KV_HEADS, SWA_HD, seq).transpose(0, 3, 1, 2)


def kernel(x_prompt, x_sample, cache_diff_k, cache_diff_v, cache_swa_k, cache_swa_v, c, c_ctx,
           w_mod, b_mod, norm_g, w_qkv_diff, diff_lambda, diff_subln_g, w_o_diff,
           w_qkv_swa, swa_sink, w_o_swa, w_gate, w_up, w_down):
    bp, lp, d = x_prompt.shape
    bs, ls, _ = x_sample.shape
    lc = cache_diff_k.shape[2]
    depth = w_mod.shape[0]
    tm = ROW_TILE

    cond8 = jnp.concatenate([c_ctx[None, :], c, jnp.zeros((8 - 1 - bs, d), F32)], axis=0)
    mods = _modulation(cond8, w_mod, b_mod)
    tables = _rope_tables(ls)

    cdk = cache_diff_k.reshape(bs, -1, 2 * DIFF_HD)
    cdv = cache_diff_v.reshape(bs, -1, 2 * DIFF_HD)
    cskt = _swa_cache_to_feature_major(cache_swa_k)
    csvt = _swa_cache_to_feature_major(cache_swa_v)

    n_p, n_s = bp * lp, bs * ls
    x_parts = (x_prompt.reshape(n_p, d), x_sample.reshape(n_s, d))
    g = norm_g
    sub_g = diff_subln_g.reshape(-1, 1, 2 * DIFF_HD)
    diff_k_out, diff_v_out, swa_k_out, swa_v_out = [], [], [], []

    for i in range(depth):
        j = i // N_MIXERS
        if i % N_MIXERS == 0:
            lam_init = 0.8 - 0.6 * math.exp(-0.3 * i)
            qp, kp, vp, qs, ks, vs = _qkv_diff(x_parts, g, mods, w_qkv_diff, tables, layer=i, j=j,
                                               n_p=n_p, n_s=n_s, tiles_per_request=ls // QKV_ROW_TILE)
            op = _diff_prompt_attention(qp, kp, vp, diff_lambda, sub_g, j=j, seq=lp, lam_init=lam_init)
            diff_k_out.append(kp.reshape(bp, lp, DIFF_HEADS, 2 * DIFF_HD))
            diff_v_out.append(vp.reshape(bp, lp, DIFF_HEADS, 2 * DIFF_HD))
            os_ = _diff_latent_attention(qs, ks, vs, cdk, cdv, diff_lambda, sub_g,
                                         j=j, seq=ls, lc=lc, lam_init=lam_init)
            w_o = w_o_diff
        else:
            qp, ktp, vtp, qs, kds, vds = _qkv_swa(x_parts, g, mods, w_qkv_swa, tables, layer=i, j=j,
                                                  n_p=n_p, n_s=n_s, tiles_per_request=ls // QKV_ROW_TILE,
                                                  seq=lp)
            op = _swa_prompt_attention(qp, ktp, vtp, swa_sink[j], seq=lp)
            swa_k_out.append(_swa_cache_from_feature_major(ktp, lp))
            swa_v_out.append(_swa_cache_from_feature_major(vtp, lp))
            os_ = _swa_latent_attention(qs, kds, vds, cskt, csvt, swa_sink[j], j=j, seq=ls)
            w_o = w_o_swa
        last = i == depth - 1
        out = _post_attn_ffn(op, os_, x_parts, w_o, w_gate, w_up, w_down, g, mods, layer=i, j=j,
                             tiles_per_request=ls // tm, split_out=last)
        x_parts = out if last else (out,)
    xp, xs = x_parts

    return (xp.reshape(bp, lp, d), xs.reshape(bs, ls, d),
            jnp.stack(diff_k_out, axis=1), jnp.stack(diff_v_out, axis=1),
            jnp.stack(swa_k_out, axis=1), jnp.stack(swa_v_out, axis=1))
```

```python
import functools
import math

import jax
import jax.numpy as jnp
import numpy as np
from jax import lax
from jax.experimental import pallas as pl
from jax.experimental.pallas import tpu as pltpu

F32 = jnp.float32
BF16 = jnp.bfloat16

GRID_W = 64
N_MIXERS = 2
DIFF_HEADS = 8
DIFF_HD = 64
SWA_HEADS = 16
SWA_KV_HEADS = 4
SWA_GROUP = SWA_HEADS // SWA_KV_HEADS
SWA_HD = 64
ROT_DIM = 64
WINDOW = 128
ROPE_BASE = 10000.0
EPS = 1e-6
NEG_INF = -1e30

LANES = 128
ROW_TILE = 512
QKV_ROW_TILE = 512
PROMPT_SCORE_BUFFERS = 8
LATENT_SCORE_BUFFERS = 2
VMEM_LIMIT = 48 * 1024 * 1024
FFN_VMEM_LIMIT = 58 * 1024 * 1024
WEIGHT_STAGE_ROWS = 128
WEIGHT_STAGE_SLOTS = 3
WEIGHT_COLUMN_SLOTS = 2
QKV_STAGE_SLOTS = 4
FFN_CHUNK = 256
NT_DIMS = (((1,), (1,)), ((), ()))
LOG2E = math.log2(math.e)


def _params(n_axes):
    return pltpu.CompilerParams(dimension_semantics=("arbitrary",) * n_axes,
                                vmem_limit_bytes=VMEM_LIMIT)


def _layer_resident(shape, layer):
    return pl.BlockSpec((None,) + tuple(shape[1:]), lambda *_: (layer,) + (0,) * (len(shape) - 1),
                        pipeline_mode=pl.Buffered(1))


def _rms(x, g):
    ms = jnp.mean(x * x, axis=-1, keepdims=True)
    return (x * lax.rsqrt(ms + EPS)) * g


def _half_masks(dtype):
    lane = lax.broadcasted_iota(jnp.int32, (1, LANES), 1)
    lo = lane < (LANES // 2)
    return jnp.where(lo, 1.0, 0.0).astype(dtype), jnp.where(lo, 0.0, 1.0).astype(dtype)


def _lo64():
    return lax.broadcasted_iota(jnp.int32, (1, LANES), 1) < (LANES // 2)


def _ones_column(rows):
    return jnp.ones((rows, LANES), BF16)


def _ones_row(cols):
    return jnp.ones((LANES, cols), BF16)


def _mod_kernel(cond_ref, w_ref, b_ref, out_ref):
    c = cond_ref[...]
    s = c * jax.nn.sigmoid(c)
    out_ref[...] = jnp.dot(s.astype(BF16), w_ref[...].astype(BF16),
                           preferred_element_type=F32) + b_ref[pl.ds(pl.program_id(0), 1), :]


def _modulation(cond8, w_mod, b_mod):
    depth, d, n = w_mod.shape
    tn = 1536
    return pl.pallas_call(
        _mod_kernel,
        out_shape=jax.ShapeDtypeStruct((depth, 8, n), F32),
        grid=(depth, n // tn),
        in_specs=[pl.BlockSpec((8, d), lambda i, j: (0, 0)),
                  pl.BlockSpec((None, d, tn), lambda i, j: (i, 0, j)),
                  pl.BlockSpec((depth, tn), lambda i, j: (0, j))],
        out_specs=pl.BlockSpec((None, 8, tn), lambda i, j: (i, 0, j)),
        compiler_params=_params(2),
        name="modulation",
    )(cond8, w_mod, b_mod)


def _rope_slab(xs, cos, sin_signed, lo16):
    left = pltpu.roll(xs, LANES - 16, 1)
    right = pltpu.roll(xs, 16, 1)
    return xs * cos + jnp.where(lo16, left, right) * sin_signed


def _lo16_mask():
    lane = lax.broadcasted_iota(jnp.int32, (1, LANES), 1)
    return (lane % 32) < 16


def _qkv_both_groups(refs, n_prompt_tiles, tiles_per_request, split_x, j, prompt_tile, latent_tile):
    refs = list(refs)
    x_refs = refs[:2] if split_x else refs[:1]
    g_ref, mod_ref, w_hbm, cos_ref, sin_ref = refs[len(x_refs):len(x_refs) + 5]
    outs = refs[len(x_refs) + 5:len(x_refs) + 11]
    w_vmem, stage, sem = refs[len(x_refs) + 11:]
    d = x_refs[0].shape[1]
    i = pl.program_id(0)

    @pl.when(i == 0)
    def _():
        _load_weights_as_bf16([(w_hbm.at[j], w_vmem)], {w_vmem.shape[1]: stage}, {w_vmem.shape[1]: sem})

    def pre_norm(x, r):
        shift, scale = mod_ref[pl.ds(r, 1), 0:d], mod_ref[pl.ds(r, 1), d:2 * d]
        return (_rms(x, g_ref[0:1, :] * (1 + scale)) + shift).astype(BF16)

    @pl.when(i < n_prompt_tiles)
    def _():
        prompt_tile(pre_norm(x_refs[0][...], 0), w_vmem, outs[:3])

    @pl.when(i >= n_prompt_tiles)
    def _():
        r = 1 + (i - n_prompt_tiles) // tiles_per_request
        latent_tile(pre_norm(x_refs[-1][...], r), w_vmem, outs[3:], (cos_ref[...], sin_ref[...], _lo16_mask()))


def _qkv_call(kernel, name, x_parts, g, mods, w, tables, prompt_outs, latent_outs, *, layer, n_p, n_s):
    d = x_parts[0].shape[1]
    tm = QKV_ROW_TILE
    tp, ts = n_p // tm, n_s // tm
    nt = tables[0].shape[0] // tm

    def prompt_block(shape):
        return pl.BlockSpec(shape, lambda i: (jnp.minimum(i, tp - 1),) + (0,) * (len(shape) - 1))

    def latent_block(shape):
        return pl.BlockSpec(shape, lambda i: (jnp.maximum(i - tp, 0),) + (0,) * (len(shape) - 1))

    if len(x_parts) == 2:
        x_specs = [prompt_block((tm, d)), latent_block((tm, d))]
    else:
        x_specs = [pl.BlockSpec((tm, d), lambda i: (i, 0))]
    table_spec = pl.BlockSpec((tm, LANES), lambda i: (jnp.maximum(i - tp, 0) % nt, 0))
    in_specs = x_specs + [_layer_resident(g.shape, layer), _layer_resident(mods.shape, layer),
                          pl.BlockSpec(memory_space=pl.ANY), table_spec, table_spec]
    out_shape, out_specs = [], []
    for outs, n, block in ((prompt_outs, n_p, prompt_block), (latent_outs, n_s, latent_block)):
        for shape_of, dtype in outs:
            out_shape.append(jax.ShapeDtypeStruct(shape_of(n), dtype))
            out_specs.append(block(shape_of(tm)))
    cols = w.shape[2]
    scratch = [pltpu.VMEM((d, cols), BF16), pltpu.VMEM((QKV_STAGE_SLOTS, WEIGHT_STAGE_ROWS, cols), F32),
               pltpu.SemaphoreType.DMA((QKV_STAGE_SLOTS,))]
    return pl.pallas_call(
        kernel,
        out_shape=tuple(out_shape),
        grid=(tp + ts,),
        in_specs=in_specs,
        out_specs=tuple(out_specs),
        scratch_shapes=scratch,
        compiler_params=_params(1),
        name=name,
    )(*x_parts, g, mods, w, *tables)


def _diff_tile(h, w_ref, outs, rope=None):
    q_ref, k_ref, v_ref = outs
    d = h.shape[1]
    cw = 512
    for c in range(3 * d // cw):
        acc = jnp.dot(h, w_ref[:, c * cw:(c + 1) * cw], preferred_element_type=F32)
        which, off = divmod(c * cw, d)
        dst = outs[which]
        for s in range(cw // LANES):
            xs = acc[:, s * LANES:(s + 1) * LANES]
            if rope is not None and which < 2:
                xs = _rope_slab(xs, *rope)
            if which == 0:
                xs = xs * (DIFF_HD ** -0.5 * LOG2E)
            lo = off + s * LANES
            if rope is not None or which == 0:
                dst[:, lo:lo + LANES] = xs.astype(dst.dtype)
            else:
                dst[pl.ds(lo // LANES, h.shape[0], stride=DIFF_HEADS), :] = xs


def _qkv_diff_kernel(*refs, j, n_prompt_tiles, tiles_per_request, split_x):
    _qkv_both_groups(refs, n_prompt_tiles, tiles_per_request, split_x, j, _diff_tile, _diff_tile)


def _qkv_diff(x_parts, g, mods, w, tables, *, layer, j, n_p, n_s, tiles_per_request):
    d = x_parts[0].shape[1]

    def rows(n):
        return (n, d)

    def cache(n):
        return (n * DIFF_HEADS, d // DIFF_HEADS)

    kernel = functools.partial(_qkv_diff_kernel, j=j, n_prompt_tiles=n_p // QKV_ROW_TILE,
                               tiles_per_request=tiles_per_request, split_x=len(x_parts) == 2)
    return _qkv_call(kernel, "qkv_diff", x_parts, g, mods, w, tables,
                     [(rows, BF16), (cache, F32), (cache, F32)], [(rows, BF16)] * 3,
                     layer=layer, n_p=n_p, n_s=n_s)


def _swa_q_tile(h, w_ref, q_ref, rope):
    d = h.shape[1]
    cw = 512
    for c in range(d // cw):
        acc = jnp.dot(h, w_ref[:, c * cw:(c + 1) * cw], preferred_element_type=F32)
        for s in range(cw // LANES):
            xs = acc[:, s * LANES:(s + 1) * LANES]
            if rope is not None:
                xs = _rope_slab(xs, *rope)
            lo = c * cw + s * LANES
            q_ref[:, lo:lo + LANES] = (xs * (SWA_HD ** -0.5 * LOG2E)).astype(BF16)
    nkv = SWA_KV_HEADS * SWA_HD
    return jnp.dot(h, w_ref[:, d:d + 2 * nkv], preferred_element_type=F32), nkv


def _swa_prompt_tile(h, w_ref, outs, *, seq):
    q_ref, kt_ref, vt_ref = outs
    kv, nkv = _swa_q_tile(h, w_ref, q_ref, None)
    for b in range(h.shape[0] // seq):
        kt_ref[b] = kv[b * seq:(b + 1) * seq, :nkv].T
        vt_ref[b] = kv[b * seq:(b + 1) * seq, nkv:].T


def _swa_latent_tile(h, w_ref, outs, rope):
    q_ref, kd_ref, vd_ref = outs
    kv, nkv = _swa_q_tile(h, w_ref, q_ref, rope)
    lo64 = _lo64()
    for which, dst in enumerate((kd_ref, vd_ref)):
        for s in range(nkv // LANES):
            xs = kv[:, which * nkv + s * LANES: which * nkv + (s + 1) * LANES]
            if which == 0:
                xs = _rope_slab(xs, *rope)
            sw = pltpu.roll(xs, LANES // 2, 1)
            dst[:, (2 * s) * LANES:(2 * s + 1) * LANES] = jnp.where(lo64, xs, sw).astype(BF16)
            dst[:, (2 * s + 1) * LANES:(2 * s + 2) * LANES] = jnp.where(lo64, sw, xs).astype(BF16)


def _qkv_swa_kernel(*refs, j, n_prompt_tiles, tiles_per_request, split_x, seq):
    _qkv_both_groups(refs, n_prompt_tiles, tiles_per_request, split_x, j,
                     functools.partial(_swa_prompt_tile, seq=seq), _swa_latent_tile)


def _qkv_swa(x_parts, g, mods, w, tables, *, layer, j, n_p, n_s, tiles_per_request, seq):
    d = x_parts[0].shape[1]
    nkv = SWA_KV_HEADS * SWA_HD

    def rows(n):
        return (n, d)

    def feature_major(n):
        return (n // seq, nkv, seq)

    def duplicated(n):
        return (n, 2 * nkv)

    kernel = functools.partial(_qkv_swa_kernel, j=j, n_prompt_tiles=n_p // QKV_ROW_TILE,
                               tiles_per_request=tiles_per_request, split_x=len(x_parts) == 2, seq=seq)
    return _qkv_call(kernel, "qkv_swa", x_parts, g, mods, w, tables,
                     [(rows, BF16), (feature_major, F32), (feature_major, F32)],
                     [(rows, BF16), (duplicated, BF16), (duplicated, BF16)],
                     layer=layer, n_p=n_p, n_s=n_s)


def _diff_lambda(lam_ref, lam_init):
    lp = lam_ref[...]
    a = jnp.sum(lp[0:1] * lp[1:2], axis=-1, keepdims=True)
    b = jnp.sum(lp[2:3] * lp[3:4], axis=-1, keepdims=True)
    return jnp.exp(a) - jnp.exp(b) + lam_init


def _diff_combine(acc, tq, lam, g, lam_init):
    o12 = acc[:, :LANES] / acc[:, LANES:]
    o = o12[:tq] - lam * o12[tq:]
    return _rms(o, g) * (1.0 - lam_init)


def _stack_maps(q):
    m_lo, m_hi = _half_masks(BF16)
    return jnp.concatenate([q * m_lo, q * m_hi], axis=0)


def _run_pipelined(items, scores, finish, s_bufs):
    depth = len(s_bufs)
    states = {i: scores(items[i], s_bufs[i]) for i in range(min(depth - 1, len(items)))}
    for i, item in enumerate(items):
        ahead = i + depth - 1
        if ahead < len(items):
            states[ahead] = scores(items[ahead], s_bufs[ahead % depth])
        finish(item, s_bufs[i % depth], states.pop(i))


def _store_scores(s_ref, col0, s, mrun):
    s_ref[:, col0:col0 + s.shape[1]] = s
    for t in range(s.shape[1] // LANES):
        blk = s[:, t * LANES:(t + 1) * LANES]
        mrun = blk if mrun is None else jnp.maximum(mrun, blk)
    return mrun


def _exp_block(s_ref, col0, width, mb):
    return jnp.concatenate(
        [jnp.exp2(s_ref[:, col0 + t * LANES:col0 + (t + 1) * LANES] - mb).astype(BF16)
         for t in range(width // LANES)], axis=1)


def _diff_prompt_kernel(q_ref, k_ref, v_ref, lam_ref, g_ref, o_ref, *s_bufs, lam_init, seq):
    lam = _diff_lambda(lam_ref, lam_init)
    g = g_ref[...]
    ones = _ones_column(seq)
    items = [(r, h) for r in range(q_ref.shape[0] // seq) for h in range(DIFF_HEADS)]

    def head_rows(ref, r, h):
        return ref[pl.ds(r * seq * DIFF_HEADS + h, seq, stride=DIFF_HEADS), :]

    def scores(item, s_ref):
        r, h = item
        rows, sl = slice(r * seq, (r + 1) * seq), slice(h * LANES, (h + 1) * LANES)
        s = lax.dot_general(_stack_maps(q_ref[rows, sl]), head_rows(k_ref, r, h).astype(BF16), NT_DIMS,
                            preferred_element_type=F32)
        return _store_scores(s_ref, 0, s, None)

    def finish(item, s_ref, mrun):
        r, h = item
        rows, sl = slice(r * seq, (r + 1) * seq), slice(h * LANES, (h + 1) * LANES)
        mb = jnp.broadcast_to(jnp.max(mrun, axis=-1, keepdims=True), (2 * seq, LANES))
        vx = jnp.concatenate([head_rows(v_ref, r, h).astype(BF16), ones], axis=1)
        acc = jnp.dot(_exp_block(s_ref, 0, seq, mb), vx, preferred_element_type=F32)
        o_ref[rows, sl] = _diff_combine(acc, seq, lam, g, lam_init).astype(BF16)

    _run_pipelined(items, scores, finish, s_bufs)


def _diff_prompt_attention(q, k, v, lam_params, subln_g, *, j, seq, lam_init):
    n, d = q.shape
    req = 2
    spec = pl.BlockSpec((req * seq, d), lambda b: (b, 0))
    kv_spec = pl.BlockSpec((req * seq * DIFF_HEADS, d // DIFF_HEADS), lambda b: (b, 0))
    return pl.pallas_call(
        functools.partial(_diff_prompt_kernel, lam_init=lam_init, seq=seq),
        out_shape=jax.ShapeDtypeStruct((n, d), BF16),
        grid=(n // (req * seq),),
        in_specs=[spec, kv_spec, kv_spec, _layer_resident(lam_params.shape, j),
                  _layer_resident(subln_g.shape, j)],
        out_specs=spec,
        scratch_shapes=[pltpu.VMEM((2 * seq, seq), F32)] * PROMPT_SCORE_BUFFERS,
        compiler_params=_params(1),
        name="diff_attn_prompt",
    )(q, k, v, lam_params, subln_g)


def _diff_latent_kernel(q_ref, kc_ref, vc_ref, kl_ref, vl_ref, qn_ref, kcn_ref, kln_ref, lam_ref, g_ref, o_ref,
                        kk_ref, vx_ref, m_ref, s0_ref, s1_ref, *, lam_init, lc, tq, key_chunk):
    seq = q_ref.shape[0]
    nk = kk_ref.shape[0]
    head = pl.program_id(1)
    step = pl.program_id(0) * pl.num_programs(1) + head
    last_step = pl.num_programs(0) * pl.num_programs(1) - 1
    next_head = jnp.minimum(step + 1, last_step) % DIFF_HEADS
    masks = _half_masks(BF16)
    nchunk = nk // key_chunk
    s_bufs = (s0_ref, s1_ref)

    def fill_keys(cache_ref, h, latent_ref):
        kk_ref[0:lc, :] = cache_ref[pl.ds(h, lc, stride=DIFF_HEADS), :].astype(BF16)
        kk_ref[lc:, :] = latent_ref[...]

    def scores(q_rows, m, s_ref):
        q = q_rows * masks[m]
        mrun = None
        for c in range(nchunk):
            s = lax.dot_general(q, kk_ref[c * key_chunk:(c + 1) * key_chunk, :], NT_DIMS,
                                preferred_element_type=F32)
            mrun = _store_scores(s_ref, c * key_chunk, s, mrun)
        return mrun

    @pl.when(step == 0)
    def _():
        fill_keys(kc_ref, head, kl_ref)
        m_ref[...] = scores(q_ref[0:tq, :], 0, s0_ref)

    vx_ref[0:lc, 0:LANES] = vc_ref[pl.ds(head, lc, stride=DIFF_HEADS), :].astype(BF16)
    vx_ref[lc:, 0:LANES] = vl_ref[...]
    vx_ref[:, LANES:2 * LANES] = _ones_column(nk)

    items = [(rt, m) for rt in range(seq // tq) for m in range(2)]
    lam = _diff_lambda(lam_ref, lam_init)
    g = g_ref[...]
    first_map = {}

    def finish(item, s_ref, mrun):
        rt, m = item
        mb = jnp.broadcast_to(jnp.max(mrun, axis=-1, keepdims=True), (tq, LANES))
        acc = None
        for c in range(nchunk):
            part = jnp.dot(_exp_block(s_ref, c * key_chunk, key_chunk, mb),
                           vx_ref[c * key_chunk:(c + 1) * key_chunk, :], preferred_element_type=F32)
            acc = part if acc is None else acc + part
        o_m = acc[:, :LANES] / acc[:, LANES:]
        if m == 0:
            first_map[rt] = o_m
        else:
            o = _rms(first_map.pop(rt) - lam * o_m, g) * (1.0 - lam_init)
            o_ref[rt * tq:(rt + 1) * tq, :] = o.astype(BF16)

    mrun = m_ref[...]
    for i, item in enumerate(items):
        s_ahead = s_bufs[(i + 1) % 2]
        if i + 1 < len(items):
            rt, m = items[i + 1]
            ahead = scores(q_ref[rt * tq:(rt + 1) * tq, :], m, s_ahead)
        else:
            fill_keys(kcn_ref, next_head, kln_ref)
            ahead = scores(qn_ref[...], 0, s_ahead)
            m_ref[...] = ahead
        finish(item, s_bufs[i % 2], mrun)
        mrun = ahead


def _diff_latent_attention(q, k, v, cache_k, cache_v, lam_params, subln_g, *, j, seq, lc, lam_init):
    n, d = q.shape
    nb = cache_k.shape[0]
    tq = 512
    key_chunk = 512
    last_step = nb * DIFF_HEADS - 1

    def next_step(b, h):
        return divmod(jnp.minimum(b * DIFF_HEADS + h + 1, last_step), DIFF_HEADS)

    def next_query_tile(b, h):
        nb_, nh = next_step(b, h)
        return nb_ * (seq // tq), nh

    q_spec = pl.BlockSpec((seq, LANES), lambda b, h: (b, h))
    c_spec = pl.BlockSpec((None, lc * DIFF_HEADS, LANES), lambda b, h: (b, j, 0))
    qn_spec = pl.BlockSpec((tq, LANES), next_query_tile)
    kln_spec = pl.BlockSpec((seq, LANES), next_step)
    cn_spec = pl.BlockSpec((None, lc * DIFF_HEADS, LANES), lambda b, h: (next_step(b, h)[0], j, 0))
    return pl.pallas_call(
        functools.partial(_diff_latent_kernel, lam_init=lam_init, lc=lc, tq=tq, key_chunk=key_chunk),
        out_shape=jax.ShapeDtypeStruct((n, d), BF16),
        grid=(nb, DIFF_HEADS),
        in_specs=[q_spec, c_spec, c_spec, q_spec, q_spec, qn_spec, cn_spec, kln_spec,
                  _layer_resident(lam_params.shape, j), _layer_resident(subln_g.shape, j)],
        out_specs=q_spec,
        scratch_shapes=[pltpu.VMEM((lc + seq, LANES), BF16),
                        pltpu.VMEM((lc + seq, 2 * LANES), BF16),
                        pltpu.VMEM((tq, LANES), F32),
                        *[pltpu.VMEM((tq, lc + seq), F32)] * LATENT_SCORE_BUFFERS],
        compiler_params=_params(2),
        name="diff_attn_latent",
    )(q, cache_k, cache_v, k, v, q, cache_k, k, lam_params, subln_g)


def _stack_group(q_ref, rows, kv_local):
    m_lo, m_hi = _half_masks(BF16)
    parts = []
    for gb in range(SWA_GROUP // 2):
        blk = kv_local * (SWA_GROUP // 2) + gb
        qb = q_ref[rows, blk * LANES:(blk + 1) * LANES]
        parts += [qb * m_lo, qb * m_hi]
    return jnp.concatenate(parts, axis=0)


def _sink_column(sink_ref, first_head, tq):
    return jnp.concatenate([jnp.full((tq, LANES), sink_ref[first_head + g] * LOG2E, F32)
                            for g in range(SWA_GROUP)], axis=0)


def _write_group(o_ref, rows, kv_local, o, tq):
    lo64 = _lo64()
    for gb in range(SWA_GROUP // 2):
        blk = kv_local * (SWA_GROUP // 2) + gb
        even = o[(2 * gb) * tq:(2 * gb + 1) * tq]
        odd = o[(2 * gb + 1) * tq:(2 * gb + 2) * tq]
        o_ref[rows, blk * LANES:(blk + 1) * LANES] = jnp.where(lo64, even, odd).astype(BF16)


def _dup_rows(x_t):
    xb = x_t.astype(BF16)
    return jnp.concatenate([xb, xb], axis=0)


def _sink_finish(mrun, sk, rows):
    mb = jnp.maximum(jnp.broadcast_to(jnp.max(mrun, axis=-1, keepdims=True), (rows, LANES)), sk)
    return mb, jnp.exp2(sk - mb)


def _swa_prompt_kernel(sink_ref, q_ref, kt_ref, vt_ref, o_ref, *s_bufs):
    seq = kt_ref.shape[2]
    rows = SWA_GROUP * seq
    ones = _ones_row(seq)
    items = [(r, j) for r in range(kt_ref.shape[0]) for j in range(SWA_KV_HEADS)]

    def scores(item, s_ref):
        r, j = item
        kd = _dup_rows(kt_ref[r, j * SWA_HD:(j + 1) * SWA_HD, :])
        s = jnp.dot(_stack_group(q_ref, slice(r * seq, (r + 1) * seq), j), kd, preferred_element_type=F32)
        return _store_scores(s_ref, 0, s, None)

    def finish(item, s_ref, mrun):
        r, j = item
        sk = _sink_column(sink_ref, j * SWA_GROUP, seq)
        mb, sink_term = _sink_finish(mrun, sk, rows)
        vx = jnp.concatenate([_dup_rows(vt_ref[r, j * SWA_HD:(j + 1) * SWA_HD, :]), ones], axis=0)
        acc = lax.dot_general(_exp_block(s_ref, 0, seq, mb), vx, NT_DIMS, preferred_element_type=F32)
        o = acc[:, :LANES] / (acc[:, LANES:] + sink_term)
        _write_group(o_ref, slice(r * seq, (r + 1) * seq), j, o, seq)

    _run_pipelined(items, scores, finish, s_bufs)


def _swa_prompt_attention(q, kt, vt, sink, *, seq):
    n, d = q.shape
    nkv = kt.shape[1]
    req = 2
    t_spec = pl.BlockSpec((req, nkv, seq), lambda b: (b, 0, 0))
    return pl.pallas_call(
        _swa_prompt_kernel,
        out_shape=jax.ShapeDtypeStruct((n, d), BF16),
        grid=(n // (req * seq),),
        in_specs=[pl.BlockSpec(memory_space=pltpu.SMEM),
                  pl.BlockSpec((req * seq, d), lambda b: (b, 0)), t_spec, t_spec],
        out_specs=pl.BlockSpec((req * seq, d), lambda b: (b, 0)),
        scratch_shapes=[pltpu.VMEM((SWA_GROUP * seq, seq), F32)] * (PROMPT_SCORE_BUFFERS // 2),
        compiler_params=_params(1),
        name="swa_attn_prompt",
    )(sink, q, kt, vt)


def _swa_latent_kernel(sink_ref, q_ref, kc_ref, vc_ref, kl_ref, vl_ref, qn_ref, kcn_ref, kln_ref, o_ref,
                       kcd_ref, vcx_ref, m_ref, s0_ref, s1_ref, *, tq, span):
    pair, part = pl.program_id(1), pl.program_id(2)
    n_parts = pl.num_programs(2)
    tiles = q_ref.shape[0] // tq
    seq = kl_ref.shape[0]
    lc = kc_ref.shape[1]
    rows = SWA_GROUP * tq
    step = (pl.program_id(0) * pl.num_programs(1) + pair) * n_parts + part
    last_step = pl.num_programs(0) * pl.num_programs(1) * n_parts - 1
    next_part = jnp.minimum(step + 1, last_step) % n_parts
    ones_row = _ones_row(lc)
    for jj in range(2):
        kcd_ref[jj] = _dup_rows(kc_ref[jj * SWA_HD:(jj + 1) * SWA_HD, :])
        vcx_ref[jj] = jnp.concatenate([_dup_rows(vc_ref[jj * SWA_HD:(jj + 1) * SWA_HD, :]), ones_row], axis=0)
    ones_col = _ones_column(span)
    items = [(t, jj) for t in range(tiles) for jj in range(2)]
    s_bufs = (s0_ref, s1_ref)
    windows, biases = {}, {}

    def window_of(part, t):
        q0 = (part * tiles + t) * tq
        return q0, pl.multiple_of(jnp.clip(q0 - WINDOW, 0, seq - span), WINDOW)

    def band_bias(q0, ws):
        q_pos = q0 + lax.broadcasted_iota(jnp.int32, (tq, span), 0)
        k_pos = ws + lax.broadcasted_iota(jnp.int32, (tq, span), 1)
        b = jnp.where(jnp.abs(q_pos - k_pos) <= WINDOW, 0.0, NEG_INF).astype(F32)
        return jnp.concatenate([b] * SWA_GROUP, axis=0)

    def window(t):
        if t not in windows:
            windows[t] = window_of(part, t)
        return windows[t]

    def bias_for(t):
        if t not in biases:
            biases[t] = band_bias(*window(t))
        return biases[t]

    def score_tile(qs, cache_keys, window_keys, bias, s_ref):
        s_c = jnp.dot(qs, cache_keys, preferred_element_type=F32)
        mrun = _store_scores(s_ref, 0, s_c, None)
        s_w = lax.dot_general(qs, window_keys, NT_DIMS, preferred_element_type=F32) + bias
        return _store_scores(s_ref, lc, s_w, mrun)

    def scores(item, s_ref):
        t, jj = item
        _, ws = window(t)
        return score_tile(_stack_group(q_ref, slice(t * tq, (t + 1) * tq), jj), kcd_ref[jj],
                          kl_ref[pl.ds(ws, span), jj * LANES:(jj + 1) * LANES], bias_for(t), s_ref)

    def finish(item, s_ref, mrun):
        t, jj = item
        _, ws = window(t)
        sk = _sink_column(sink_ref, (2 * pair + jj) * SWA_GROUP, tq)
        mb, sink_term = _sink_finish(mrun, sk, rows)
        vwx = jnp.concatenate([vl_ref[pl.ds(ws, span), jj * LANES:(jj + 1) * LANES], ones_col], axis=1)
        acc = (lax.dot_general(_exp_block(s_ref, 0, lc, mb), vcx_ref[jj], NT_DIMS, preferred_element_type=F32)
               + jnp.dot(_exp_block(s_ref, lc, span, mb), vwx, preferred_element_type=F32))
        o = acc[:, :LANES] / (acc[:, LANES:] + sink_term)
        _write_group(o_ref, slice(t * tq, (t + 1) * tq), jj, o, tq)

    @pl.when(step == 0)
    def _():
        q0, ws = window_of(part, 0)
        m_ref[...] = score_tile(_stack_group(q_ref, slice(0, tq), 0), kcd_ref[0],
                                kl_ref[pl.ds(ws, span), 0:LANES], band_bias(q0, ws), s0_ref)

    mrun = m_ref[...]
    for i, item in enumerate(items):
        s_ahead = s_bufs[(i + 1) % 2]
        if i + 1 < len(items):
            ahead = scores(items[i + 1], s_ahead)
        else:
            q0, ws = window_of(next_part, 0)
            ahead = score_tile(_stack_group(qn_ref, slice(0, tq), 0), _dup_rows(kcn_ref[...]),
                               kln_ref[pl.ds(ws, span), :], band_bias(q0, ws), s_ahead)
            m_ref[...] = ahead
        finish(item, s_bufs[i % 2], mrun)
        mrun = ahead


def _swa_latent_attention(q, kd, vd, cache_kt, cache_vt, sink, *, j, seq):
    n, d = q.shape
    nb, _, lc = cache_kt.shape
    tq = 256
    span = tq + 2 * WINDOW
    npair = SWA_KV_HEADS // 2
    wq = d // npair
    parts = 2
    tiles = seq // parts // tq
    last_step = nb * npair * parts - 1

    def next_step(b, p, i):
        rest, ni = divmod(jnp.minimum((b * npair + p) * parts + i + 1, last_step), parts)
        return (*divmod(rest, npair), ni)

    def next_query_tile(b, p, i):
        nb_, np_, ni = next_step(b, p, i)
        return (nb_ * parts + ni) * tiles, 2 * np_

    def next_cache_head(b, p, i):
        nb_, np_, _ = next_step(b, p, i)
        return nb_, 2 * (j * npair + np_), 0

    def next_latent_head(b, p, i):
        nb_, np_, _ = next_step(b, p, i)
        return nb_, 2 * np_

    q_spec = pl.BlockSpec((seq // parts, wq), lambda b, p, i: (b * parts + i, p))
    c_spec = pl.BlockSpec((None, 2 * SWA_HD, lc), lambda b, p, i: (b, j * npair + p, 0))
    l_spec = pl.BlockSpec((seq, 2 * LANES), lambda b, p, i: (b, p))
    qn_spec = pl.BlockSpec((tq, 2 * LANES), next_query_tile)
    cn_spec = pl.BlockSpec((None, SWA_HD, lc), next_cache_head)
    ln_spec = pl.BlockSpec((seq, LANES), next_latent_head)
    s_shape = pltpu.VMEM((SWA_GROUP * tq, lc + span), F32)
    return pl.pallas_call(
        functools.partial(_swa_latent_kernel, tq=tq, span=span),
        out_shape=jax.ShapeDtypeStruct((n, d), BF16),
        grid=(nb, npair, parts),
        in_specs=[pl.BlockSpec(memory_space=pltpu.SMEM), q_spec, c_spec, c_spec, l_spec, l_spec,
                  qn_spec, cn_spec, ln_spec],
        out_specs=q_spec,
        scratch_shapes=[pltpu.VMEM((2, 2 * SWA_HD, lc), BF16), pltpu.VMEM((2, 2 * LANES, lc), BF16),
                        pltpu.VMEM((SWA_GROUP * tq, LANES), F32), s_shape, s_shape],
        compiler_params=_params(3),
        name="swa_attn_latent",
    )(sink, q, cache_kt, cache_vt, kd, vd, q, cache_kt, kd)


def _load_weights_as_bf16(jobs, stages, sems):
    order = []
    rings = {w: [] for w in stages}
    for src, dst in jobs:
        w = src.shape[1]
        slots = stages[w].shape[0]
        for k in range(src.shape[0] // WEIGHT_STAGE_ROWS):
            rows = pl.ds(k * WEIGHT_STAGE_ROWS, WEIGHT_STAGE_ROWS)
            slot = len(rings[w]) % slots
            copy = pltpu.make_async_copy(src.at[rows, :], stages[w].at[slot], sems[w].at[slot])
            order.append((w, len(rings[w])))
            rings[w].append((copy, slot, dst, rows))
    for w, ring in rings.items():
        for copy, _, _, _ in ring[:stages[w].shape[0]]:
            copy.start()
    for w, k in order:
        copy, slot, dst, rows = rings[w][k]
        copy.wait()
        dst[rows, :] = stages[w][slot].astype(BF16)
        ahead = k + stages[w].shape[0]
        if ahead < len(rings[w]):
            rings[w][ahead][0].start()


def _weight_stream(pieces, rings):
    per_ring = {name: [] for name in rings}
    plan = []
    for src, dst, idx, name in pieces:
        stage, sem = rings[name]
        slot = len(per_ring[name]) % stage.shape[0]
        view = stage.at[slot].at[0:src.shape[0], 0:src.shape[1]]
        plan.append((name, len(per_ring[name])))
        per_ring[name].append((pltpu.make_async_copy(src, view, sem.at[slot]), view, dst, idx))
    cursor = [0]

    def prime():
        for name, ring in per_ring.items():
            for copy, _, _, _ in ring[:rings[name][0].shape[0]]:
                copy.start()

    def take(n):
        for name, k in plan[cursor[0]:cursor[0] + n]:
            copy, view, dst, idx = per_ring[name][k]
            copy.wait()
            dst[idx] = view[...].astype(BF16)
            ahead = k + rings[name][0].shape[0]
            if ahead < len(per_ring[name]):
                per_ring[name][ahead][0].start()
        cursor[0] += n

    return prime, take


def _post_attn_ffn_kernel(*refs, layer, j, n_prompt_tiles, tiles_per_request, split_x, split_out,
                          overlap_first_tile):
    refs = list(refs)
    o_first_ref, op_ref, os_ref = refs[:3]
    refs = refs[1:]
    x_refs = refs[2:4] if split_x else refs[2:3]
    wo_hbm, wg_hbm, wu_hbm, wd_hbm, g_ref, mod_ref = refs[2 + len(x_refs):8 + len(x_refs)]
    n_out = 2 if split_out else 1
    out_refs = refs[8 + len(x_refs):8 + len(x_refs) + n_out]
    wo_ref, wg_ref, wu_ref, wd_ref, stage_row, stage_col, sem_row, sem_col, y_ref = refs[8 + len(x_refs) + n_out:]
    d = wo_ref.shape[1]
    dff = wg_ref.shape[1]
    cw = FFN_CHUNK
    i = pl.program_id(0)
    is_prompt = i < n_prompt_tiles
    r = jnp.where(is_prompt, 0, 1 + (i - n_prompt_tiles) // tiles_per_request)

    def mod(slot):
        return mod_ref[pl.ds(r, 1), slot * d:(slot + 1) * d]

    def out_projection(o):
        y_ref[...] = jnp.dot(o, wo_ref[...], preferred_element_type=F32)

    def tile(before_chunk):
        x = jnp.where(is_prompt, x_refs[0][...], x_refs[1][...]) if split_x else x_refs[0][...]
        x = x + _rms(y_ref[...], mod(2) * g_ref[1:2, :])
        h = (_rms(x, g_ref[2:3, :] * (1 + mod(4))) + mod(3)).astype(BF16)
        next_is_prompt = jnp.minimum(i + 1, pl.num_programs(0) - 1) < n_prompt_tiles
        out_projection(jnp.where(next_is_prompt, op_ref[...], os_ref[...]))
        y = jnp.zeros((h.shape[0], d), F32)
        n_chunks = dff // cw
        pending = []
        for c in range(n_chunks):
            before_chunk(c)
            cols = slice(c * cw, (c + 1) * cw)
            a = jnp.dot(h, wg_ref[:, cols], preferred_element_type=F32)
            u = jnp.dot(h, wu_ref[:, cols], preferred_element_type=F32)
            pending.append(((a * jax.nn.sigmoid(a)) * u).astype(BF16))
            if len(pending) == 2 or c == n_chunks - 1:
                rows = slice((c + 1 - len(pending)) * cw, (c + 1) * cw)
                y = y + jnp.dot(jnp.concatenate(pending, axis=1), wd_ref[rows, :], preferred_element_type=F32)
                pending = []
        out = x + _rms(y, mod(5) * g_ref[3:4, :])
        if split_out:
            out_refs[1][...] = out

            @pl.when(is_prompt)
            def _():
                out_refs[0][...] = out
        else:
            out_refs[0][...] = out

    @pl.when(i == 0)
    def _():
        row_chunk = stage_row.shape[1]
        pieces = [(wo_hbm.at[j].at[pl.ds(k * row_chunk, row_chunk), :], wo_ref,
                   (pl.ds(k * row_chunk, row_chunk), slice(None)), "row") for k in range(d // row_chunk)]
        col_chunk = stage_col.shape[2]
        per_take = [d // row_chunk]
        for c in range(dff // cw):
            n = 0
            if (c * cw) % col_chunk == 0:
                cols = pl.ds(c * cw, min(col_chunk, dff - c * cw))
                pieces += [(wg_hbm.at[layer].at[:, cols], wg_ref, (slice(None), cols), "col"),
                           (wu_hbm.at[layer].at[:, cols], wu_ref, (slice(None), cols), "col")]
                n += 2
            rows = pl.ds(c * cw, cw)
            pieces.append((wd_hbm.at[layer].at[rows, :], wd_ref, (rows, slice(None)), "row"))
            per_take.append(n + 1)
        prime, take = _weight_stream(pieces, {"row": (stage_row, sem_row), "col": (stage_col, sem_col)})
        prime()
        if overlap_first_tile:
            take(per_take[0])
            out_projection(o_first_ref[...])
            tile(lambda c: take(per_take[c + 1]))
        else:
            take(len(pieces))
            out_projection(o_first_ref[...])

    if overlap_first_tile:
        @pl.when(i > 0)
        def _():
            tile(lambda c: None)
    else:
        tile(lambda c: None)


def _post_attn_ffn(o_p, o_s, xs_in, w_o, wg, wu, wd, g, mods, *, layer, j, tiles_per_request, split_out):
    n_p, d = o_p.shape
    n_s = o_s.shape[0]
    tm = ROW_TILE
    tp, ts = n_p // tm, n_s // tm
    prompt_rows = pl.BlockSpec((tm, d), lambda i: (jnp.minimum(i, tp - 1), 0))
    latent_rows = pl.BlockSpec((tm, d), lambda i: (jnp.maximum(i - tp, 0), 0))
    all_rows = pl.BlockSpec((tm, d), lambda i: (i, 0))
    first_rows = pl.BlockSpec((tm, d), lambda i: (0, 0), pipeline_mode=pl.Buffered(1))
    next_prompt_rows = pl.BlockSpec((tm, d), lambda i: (jnp.minimum(i + 1, tp - 1), 0))
    next_latent_rows = pl.BlockSpec((tm, d), lambda i: (jnp.clip(i + 1 - tp, 0, ts - 1), 0))
    split_x = len(xs_in) == 2
    in_specs = [first_rows, next_prompt_rows, next_latent_rows]
    in_specs += [prompt_rows, latent_rows] if split_x else [all_rows]
    hbm = pl.BlockSpec(memory_space=pl.ANY)
    in_specs += [hbm, hbm, hbm, hbm, _layer_resident(g.shape, layer), _layer_resident(mods.shape, layer)]
    if split_out:
        out_shape = (jax.ShapeDtypeStruct((n_p, d), F32), jax.ShapeDtypeStruct((n_s, d), F32))
        out_specs = (prompt_rows, latent_rows)
    else:
        out_shape = jax.ShapeDtypeStruct((n_p + n_s, d), F32)
        out_specs = all_rows
    dff = wg.shape[2]
    scratch = [pltpu.VMEM((d, d), BF16), pltpu.VMEM((d, dff), BF16), pltpu.VMEM((d, dff), BF16),
               pltpu.VMEM((dff, d), BF16),
               pltpu.VMEM((WEIGHT_STAGE_SLOTS, FFN_CHUNK, d), F32),
               pltpu.VMEM((WEIGHT_COLUMN_SLOTS, d, 2 * FFN_CHUNK), F32),
               pltpu.SemaphoreType.DMA((WEIGHT_STAGE_SLOTS,)), pltpu.SemaphoreType.DMA((WEIGHT_COLUMN_SLOTS,)),
               pltpu.VMEM((tm, d), F32)]
    return pl.pallas_call(
        functools.partial(_post_attn_ffn_kernel, layer=layer, j=j, n_prompt_tiles=tp,
                          tiles_per_request=tiles_per_request, split_x=split_x, split_out=split_out,
                          overlap_first_tile=not split_x),
        out_shape=out_shape,
        grid=(tp + ts,),
        in_specs=in_specs,
        out_specs=out_specs,
        scratch_shapes=scratch,
        compiler_params=pltpu.CompilerParams(dimension_semantics=("arbitrary",),
                                             vmem_limit_bytes=FFN_VMEM_LIMIT),
        name="post_attn_ffn",
    )(o_p, o_p, o_s, *xs_in, w_o, wg, wu, wd, g, mods)


def _rope_tables(n_lat):
    t = np.arange(n_lat)
    row = (t // GRID_W).astype(np.float32)
    col = (t % GRID_W).astype(np.float32)
    nf = ROT_DIM // 4
    inv = np.float32(ROPE_BASE) ** (-np.arange(nf, dtype=np.float32) / np.float32(nf))
    ar = row[:, None] * inv[None, :]
    ac = col[:, None] * inv[None, :]
    ang = np.concatenate([ar, ar, ac, ac], axis=-1)
    cos, sin = np.cos(ang), np.sin(ang)
    sign = np.where((np.arange(ROT_DIM) % 32) < 16, -1.0, 1.0).astype(np.float32)
    reps = LANES // ROT_DIM
    return jnp.asarray(np.tile(cos, (1, reps))), jnp.asarray(np.tile(sin * sign, (1, reps)))


def _swa_cache_to_feature_major(cache):
    nb, nl, lc, nh, hd = cache.shape
    return cache.transpose(0, 1, 3, 4, 2).reshape(nb, nl * nh * hd, lc)


def _swa_cache_from_feature_major(xt, seq):
    nb = xt.shape[0]
    return xt.reshape(nb, SWA_KV_HEADS, SWA_HD, seq).transpose(0, 3, 1, 2)


def kernel(x_prompt, x_sample, cache_diff_k, cache_diff_v, cache_swa_k, cache_swa_v, c, c_ctx,
           w_mod, b_mod, norm_g, w_qkv_diff, diff_lambda, diff_subln_g, w_o_diff,
           w_qkv_swa, swa_sink, w_o_swa, w_gate, w_up, w_down):
    bp, lp, d = x_prompt.shape
    bs, ls, _ = x_sample.shape
    lc = cache_diff_k.shape[2]
    depth = w_mod.shape[0]
    tm = ROW_TILE

    cond8 = jnp.concatenate([c_ctx[None, :], c, jnp.zeros((8 - 1 - bs, d), F32)], axis=0)
    mods = _modulation(cond8, w_mod, b_mod)
    tables = _rope_tables(ls)

    cdk = cache_diff_k.reshape(bs, -1, 2 * DIFF_HD)
    cdv = cache_diff_v.reshape(bs, -1, 2 * DIFF_HD)
    cskt = _swa_cache_to_feature_major(cache_swa_k)
    csvt = _swa_cache_to_feature_major(cache_swa_v)

    n_p, n_s = bp * lp, bs * ls
    x_parts = (x_prompt.reshape(n_p, d), x_sample.reshape(n_s, d))
    g = norm_g
    sub_g = diff_subln_g.reshape(-1, 1, 2 * DIFF_HD)
    diff_k_out, diff_v_out, swa_k_out, swa_v_out = [], [], [], []

    for i in range(depth):
        j = i // N_MIXERS
        if i % N_MIXERS == 0:
            lam_init = 0.8 - 0.6 * math.exp(-0.3 * i)
            qp, kp, vp, qs, ks, vs = _qkv_diff(x_parts, g, mods, w_qkv_diff, tables, layer=i, j=j,
                                               n_p=n_p, n_s=n_s, tiles_per_request=ls // QKV_ROW_TILE)
            op = _diff_prompt_attention(qp, kp, vp, diff_lambda, sub_g, j=j, seq=lp, lam_init=lam_init)
            diff_k_out.append(kp.reshape(bp, lp, DIFF_HEADS, 2 * DIFF_HD))
            diff_v_out.append(vp.reshape(bp, lp, DIFF_HEADS, 2 * DIFF_HD))
            os_ = _diff_latent_attention(qs, ks, vs, cdk, cdv, diff_lambda, sub_g,
                                         j=j, seq=ls, lc=lc, lam_init=lam_init)
            w_o = w_o_diff
        else:
            qp, ktp, vtp, qs, kds, vds = _qkv_swa(x_parts, g, mods, w_qkv_swa, tables, layer=i, j=j,
                                                  n_p=n_p, n_s=n_s, tiles_per_request=ls // QKV_ROW_TILE,
                                                  seq=lp)
            op = _swa_prompt_attention(qp, ktp, vtp, swa_sink[j], seq=lp)
            swa_k_out.append(_swa_cache_from_feature_major(ktp, lp))
            swa_v_out.append(_swa_cache_from_feature_major(vtp, lp))
            os_ = _swa_latent_attention(qs, kds, vds, cskt, csvt, swa_sink[j], j=j, seq=ls)
            w_o = w_o_swa
        last = i == depth - 1
        out = _post_attn_ffn(op, os_, x_parts, w_o, w_gate, w_up, w_down, g, mods, layer=i, j=j,
                             tiles_per_request=ls // tm, split_out=last)
        x_parts = out if last else (out,)
    xp, xs = x_parts

    return (xp.reshape(bp, lp, d), xs.reshape(bs, ls, d),
            jnp.stack(diff_k_out, axis=1), jnp.stack(diff_v_out, axis=1),
            jnp.stack(swa_k_out, axis=1), jnp.stack(swa_v_out, axis=1))
```
